```python
import math
import jax, jax.numpy as jnp
from jax import lax
import numpy as np

D_MODEL = 1024
BATCH = 8
SEQ = 8192
DEPTH = 1

SSD_INNER = D_MODEL
SSD_HEADS = 16
SSD_HEAD_DIM = SSD_INNER // SSD_HEADS
SSD_GROUPS = 2
SSD_STATE = 128
SSD_CONV = 4
SSD_CHUNK = 128
SSD_XBC = SSD_INNER + 2 * SSD_GROUPS * SSD_STATE

SB_HEADS = 16
SB_HEAD_DIM = 64
SB_WIDTH = SB_HEADS * SB_HEAD_DIM
SB_BLOCK = 128

MIX_WIDTH = SSD_INNER + SB_WIDTH
IN_PROJ = SSD_INNER + SSD_XBC + SSD_HEADS + 3 * SB_WIDTH

MEM_TOKENS = 256
MEM_HEADS = 4
MEM_HEAD_DIM = D_MODEL // MEM_HEADS

D_FF = 2816
FFN_CONV = 3

EPS = 1e-6

kernel_name = "hymba_ssd_stickbreaking_hybrid_layer"


def rmsnorm(x, w):
    xf = x.astype(jnp.float32)
    y = xf * lax.rsqrt(jnp.mean(xf * xf, axis=-1, keepdims=True) + EPS)
    return y.astype(x.dtype) * w


def causal_dwconv(x, w, b):
    K, C = w.shape
    y = lax.conv_general_dilated(
        x, w[:, None, :].astype(x.dtype), window_strides=(1,), padding=[(K - 1, 0)],
        dimension_numbers=("NWC", "WIO", "NWC"), feature_group_count=C)
    return y + b


def ssd_chunked(xh, dt, A, Bh, Ch):
    b, l, h, p = xh.shape
    n = Bh.shape[-1]
    c = l // SSD_CHUNK
    L = SSD_CHUNK
    dtype = xh.dtype
    a = dt * A
    xc = (xh * dt[..., None].astype(dtype)).reshape(b, c, L, h, p)
    Bc = Bh.reshape(b, c, L, h, n)
    Cc = Ch.reshape(b, c, L, h, n)
    a_cs = jnp.cumsum(a.reshape(b, c, L, h).transpose(0, 3, 1, 2), axis=-1)
    seg = a_cs[..., :, None] - a_cs[..., None, :]
    tril = jnp.tril(jnp.ones((L, L), dtype=bool))
    Lmat = jnp.where(tril, jnp.exp(jnp.where(tril, seg, 0.0)), 0.0).astype(dtype)
    cb = jnp.einsum("bclhn,bcshn->bhcls", Cc, Bc) * Lmat
    y_diag = jnp.einsum("bhcls,bcshp->bclhp", cb, xc)
    decay_states = jnp.exp(a_cs[..., -1:] - a_cs).astype(dtype)
    states = jnp.einsum("bclhn,bhcl,bclhp->bchpn", Bc, decay_states, xc)
    chunk_decay = jnp.exp(a_cs[..., -1]).astype(dtype)

    def step(carry, inp):
        s, d = inp
        return carry * d[..., None, None] + s, carry

    init = jnp.zeros((b, h, p, n), dtype=states.dtype)
    _, prev = lax.scan(step, init, (states.transpose(1, 0, 2, 3, 4), chunk_decay.transpose(2, 0, 1)))
    prev = prev.transpose(1, 0, 2, 3, 4)
    y_off = jnp.einsum("bclhn,bchpn,bhcl->bclhp", Cc, prev, jnp.exp(a_cs).astype(dtype))
    return (y_diag + y_off).reshape(b, l, h, p)


def ssd_mixer(z, xbc, dt_raw, conv_w, conv_b, dt_bias, a_log, d_skip, norm_w):
    b, l, _ = xbc.shape
    xbc = jax.nn.silu(causal_dwconv(xbc, conv_w, conv_b))
    xs = xbc[..., :SSD_INNER]
    Bm = xbc[..., SSD_INNER:SSD_INNER + SSD_GROUPS * SSD_STATE].reshape(b, l, SSD_GROUPS, SSD_STATE)
    Cm = xbc[..., SSD_INNER + SSD_GROUPS * SSD_STATE:].reshape(b, l, SSD_GROUPS, SSD_STATE)
    rep = SSD_HEADS // SSD_GROUPS
    Bh = jnp.repeat(Bm, rep, axis=2)
    Ch = jnp.repeat(Cm, rep, axis=2)
    xh = xs.reshape(b, l, SSD_HEADS, SSD_HEAD_DIM)
    dt = jax.nn.softplus((dt_raw + dt_bias).astype(jnp.float32))
    A = -jnp.exp(a_log.astype(jnp.float32))
    y = ssd_chunked(xh, dt, A, Bh, Ch)
    y = y + xh * d_skip[:, None]
    y = y.reshape(b, l, SSD_INNER) * jax.nn.silu(z)
    yg = y.reshape(b, l, SSD_GROUPS, SSD_INNER // SSD_GROUPS)
    yg = rmsnorm(yg, norm_w.reshape(SSD_GROUPS, SSD_INNER // SSD_GROUPS))
    return yg.reshape(b, l, SSD_INNER)


def stick_breaking_attention(q, k, v, norm_w):
    b, S, _ = q.shape
    nb = S // SB_BLOCK
    qh = q.reshape(b, S, SB_HEADS, SB_HEAD_DIM).transpose(0, 2, 1, 3)
    kh = k.reshape(b, S, SB_HEADS, SB_HEAD_DIM).transpose(0, 2, 1, 3)
    vh = v.reshape(b, S, SB_HEADS, SB_HEAD_DIM).transpose(0, 2, 1, 3)
    qb = qh.reshape(b, SB_HEADS, nb, SB_BLOCK, SB_HEAD_DIM).transpose(2, 0, 1, 3, 4)
    starts = jnp.arange(nb, dtype=jnp.int32) * SB_BLOCK
    scale = 1.0 / math.sqrt(SB_HEAD_DIM)
    s_idx = jnp.arange(S, dtype=jnp.int32)

    def block(args):
        qblk, start = args
        zz = jnp.einsum("bhqd,bhkd->bhqk", qblk, kh).astype(jnp.float32) * scale
        t_idx = start + jnp.arange(SB_BLOCK, dtype=jnp.int32)
        causal = s_idx[None, :] < t_idx[:, None]
        lnb = jnp.where(causal, jax.nn.log_sigmoid(-zz), 0.0)
        excl = lax.cumsum(lnb, axis=3, reverse=True) - lnb
        A = jnp.where(causal, jnp.exp(jax.nn.log_sigmoid(zz) + excl), 0.0)
        return jnp.einsum("bhqk,bhkd->bhqd", A.astype(vh.dtype), vh)

    out = lax.map(block, (qb, starts))
    out = out.transpose(1, 0, 3, 2, 4).reshape(b, S, SB_HEADS, SB_HEAD_DIM)
    out = rmsnorm(out, norm_w.reshape(SB_HEADS, SB_HEAD_DIM))
    return out.reshape(b, S, SB_WIDTH)


def memory_cross_attention(h, m, w_q, w_k, w_v, w_o):
    b, S, _ = h.shape
    M = m.shape[1]
    q = (h @ w_q).reshape(b, S, MEM_HEADS, MEM_HEAD_DIM)
    k = (m @ w_k).reshape(b, M, MEM_HEADS, MEM_HEAD_DIM)
    v = (m @ w_v).reshape(b, M, MEM_HEADS, MEM_HEAD_DIM)
    sc = jnp.einsum("bshd,bmhd->bhsm", q, k).astype(jnp.float32) / math.sqrt(MEM_HEAD_DIM)
    p = jax.nn.softmax(sc, axis=-1).astype(v.dtype)
    o = jnp.einsum("bhsm,bmhd->bshd", p, v).reshape(b, S, D_MODEL)
    return o @ w_o


def conv_glu_ffn(h, w_up, conv_w, conv_b, w_down):
    u = causal_dwconv(h @ w_up, conv_w, conv_b)
    g, val = u[..., :D_FF], u[..., D_FF:]
    return (jax.nn.silu(g) * val) @ w_down


def _fwd_setup_inputs(seed: int = 0) -> dict:
    key = jax.random.key(seed)
    ks = jax.random.split(key, 32)
    f32 = jnp.float32

    def nrm(k, shape, fan_in):
        return jax.random.normal(k, shape, f32) * (fan_in ** -0.5)

    def gain(k, shape):
        return 1.0 + 0.02 * jax.random.normal(k, shape, f32)

    Ld = DEPTH
    dt0 = jnp.exp(jax.random.uniform(ks[8], (Ld, SSD_HEADS), f32, math.log(1e-3), math.log(1e-1)))
    dt_bias = dt0 + jnp.log(-jnp.expm1(-dt0))
    return {
        "x": jax.random.normal(ks[0], (BATCH, SEQ, D_MODEL), f32),
        "mem": jax.random.normal(ks[1], (BATCH, MEM_TOKENS, D_MODEL), f32),
        "norm_mix_w": gain(ks[2], (Ld, D_MODEL)),
        "w_in": nrm(ks[3], (Ld, D_MODEL, IN_PROJ), D_MODEL),
        "conv_ssd_w": nrm(ks[4], (Ld, SSD_CONV, SSD_XBC), SSD_CONV),
        "conv_ssd_b": 0.02 * jax.random.normal(ks[5], (Ld, SSD_XBC), f32),
        "dt_bias": dt_bias,
        "a_log": jnp.log(jax.random.uniform(ks[6], (Ld, SSD_HEADS), f32, 1.0, 16.0)),
        "d_skip": gain(ks[7], (Ld, SSD_HEADS)),
        "ssd_norm_w": gain(ks[9], (Ld, SSD_INNER)),
        "sb_norm_w": gain(ks[10], (Ld, SB_WIDTH)),
        "w_out": nrm(ks[11], (Ld, MIX_WIDTH, D_MODEL), MIX_WIDTH),
        "norm_mem_w": gain(ks[12], (Ld, D_MODEL)),
        "norm_memkv_w": gain(ks[13], (Ld, D_MODEL)),
        "w_mq": nrm(ks[14], (Ld, D_MODEL, D_MODEL), D_MODEL),
        "w_mk": nrm(ks[15], (Ld, D_MODEL, D_MODEL), D_MODEL),
        "w_mv": nrm(ks[16], (Ld, D_MODEL, D_MODEL), D_MODEL),
        "w_mo": nrm(ks[17], (Ld, D_MODEL, D_MODEL), D_MODEL),
        "norm_ffn_w": gain(ks[18], (Ld, D_MODEL)),
        "w_up": nrm(ks[19], (Ld, D_MODEL, 2 * D_FF), D_MODEL),
        "conv_ffn_w": nrm(ks[20], (Ld, FFN_CONV, 2 * D_FF), FFN_CONV),
        "conv_ffn_b": 0.02 * jax.random.normal(ks[21], (Ld, 2 * D_FF), f32),
        "w_down": nrm(ks[22], (Ld, D_FF, D_MODEL), D_FF),
        "norm_final_w": gain(ks[23], (D_MODEL,)),
    }


def _fwd_reference(x, mem, norm_mix_w, w_in, conv_ssd_w, conv_ssd_b, dt_bias, a_log, d_skip,
              ssd_norm_w, sb_norm_w, w_out, norm_mem_w, norm_memkv_w, w_mq, w_mk, w_mv, w_mo,
              norm_ffn_w, w_up, conv_ffn_w, conv_ffn_b, w_down, norm_final_w):
    o1 = SSD_INNER
    o2 = o1 + SSD_XBC
    o3 = o2 + SSD_HEADS
    o4 = o3 + SB_WIDTH
    o5 = o4 + SB_WIDTH
    for l in range(DEPTH):
        h = rmsnorm(x, norm_mix_w[l])
        proj = h @ w_in[l]
        z, xbc, dt_raw = proj[..., :o1], proj[..., o1:o2], proj[..., o2:o3]
        q, k, v = proj[..., o3:o4], proj[..., o4:o5], proj[..., o5:]
        y_ssd = ssd_mixer(z, xbc, dt_raw, conv_ssd_w[l], conv_ssd_b[l], dt_bias[l],
                          a_log[l], d_skip[l], ssd_norm_w[l])
        y_sb = stick_breaking_attention(q, k, v, sb_norm_w[l])
        x = x + jnp.concatenate([y_ssd, y_sb], axis=-1) @ w_out[l]
        h = rmsnorm(x, norm_mem_w[l])
        m = rmsnorm(mem, norm_memkv_w[l])
        x = x + memory_cross_attention(h, m, w_mq[l], w_mk[l], w_mv[l], w_mo[l])
        h = rmsnorm(x, norm_ffn_w[l])
        x = x + conv_glu_ffn(h, w_up[l], conv_ffn_w[l], conv_ffn_b[l], w_down[l])
    return rmsnorm(x, norm_final_w)


import jax as _jax
import jax.numpy as _jnp

TWIN_FORMAT = 'train_step'
FWD_PARAMS = ['x', 'mem', 'norm_mix_w', 'w_in', 'conv_ssd_w', 'conv_ssd_b', 'dt_bias', 'a_log', 'd_skip', 'ssd_norm_w', 'sb_norm_w', 'w_out', 'norm_mem_w', 'norm_memkv_w', 'w_mq', 'w_mk', 'w_mv', 'w_mo', 'norm_ffn_w', 'w_up', 'conv_ffn_w', 'conv_ffn_b', 'w_down', 'norm_final_w']
TWIN_WEIGHTS = ['norm_mix_w', 'w_in', 'conv_ssd_w', 'conv_ssd_b', 'dt_bias', 'a_log', 'd_skip', 'ssd_norm_w', 'sb_norm_w', 'w_out', 'norm_mem_w', 'norm_memkv_w', 'w_mq', 'w_mk', 'w_mv', 'w_mo', 'norm_ffn_w', 'w_up', 'conv_ffn_w', 'conv_ffn_b', 'w_down', 'norm_final_w']
TWIN_DIFF_INPUT = 'x'
TWIN_INPUTS = ['x', 'mem', 'norm_mix_w', 'w_in', 'conv_ssd_w', 'conv_ssd_b', 'dt_bias', 'a_log', 'd_skip', 'ssd_norm_w', 'sb_norm_w', 'w_out', 'norm_mem_w', 'norm_memkv_w', 'w_mq', 'w_mk', 'w_mv', 'w_mo', 'norm_ffn_w', 'w_up', 'conv_ffn_w', 'conv_ffn_b', 'w_down', 'norm_final_w', 'loss_target', 'm_norm_mix_w', 'm_w_in', 'm_conv_ssd_w', 'm_conv_ssd_b', 'm_dt_bias', 'm_a_log', 'm_d_skip', 'm_ssd_norm_w', 'm_sb_norm_w', 'm_w_out', 'm_norm_mem_w', 'm_norm_memkv_w', 'm_w_mq', 'm_w_mk', 'm_w_mv', 'm_w_mo', 'm_norm_ffn_w', 'm_w_up', 'm_conv_ffn_w', 'm_conv_ffn_b', 'm_w_down', 'm_norm_final_w', 'v_norm_mix_w', 'v_w_in', 'v_conv_ssd_w', 'v_conv_ssd_b', 'v_dt_bias', 'v_a_log', 'v_d_skip', 'v_ssd_norm_w', 'v_sb_norm_w', 'v_w_out', 'v_norm_mem_w', 'v_norm_memkv_w', 'v_w_mq', 'v_w_mk', 'v_w_mv', 'v_w_mo', 'v_norm_ffn_w', 'v_w_up', 'v_conv_ffn_w', 'v_conv_ffn_b', 'v_w_down', 'v_norm_final_w']
TWIN_OUTPUTS = ['loss', 'grad_x', 'grad_norm_mix_w', 'grad_w_in', 'grad_conv_ssd_w', 'grad_conv_ssd_b', 'grad_dt_bias', 'grad_a_log', 'grad_d_skip', 'grad_ssd_norm_w', 'grad_sb_norm_w', 'grad_w_out', 'grad_norm_mem_w', 'grad_norm_memkv_w', 'grad_w_mq', 'grad_w_mk', 'grad_w_mv', 'grad_w_mo', 'grad_norm_ffn_w', 'grad_w_up', 'grad_conv_ffn_w', 'grad_conv_ffn_b', 'grad_w_down', 'grad_norm_final_w', 'delta_norm_mix_w', 'delta_w_in', 'delta_conv_ssd_w', 'delta_conv_ssd_b', 'delta_dt_bias', 'delta_a_log', 'delta_d_skip', 'delta_ssd_norm_w', 'delta_sb_norm_w', 'delta_w_out', 'delta_norm_mem_w', 'delta_norm_memkv_w', 'delta_w_mq', 'delta_w_mk', 'delta_w_mv', 'delta_w_mo', 'delta_norm_ffn_w', 'delta_w_up', 'delta_conv_ffn_w', 'delta_conv_ffn_b', 'delta_w_down', 'delta_norm_final_w', 'new_m_norm_mix_w', 'new_m_w_in', 'new_m_conv_ssd_w', 'new_m_conv_ssd_b', 'new_m_dt_bias', 'new_m_a_log', 'new_m_d_skip', 'new_m_ssd_norm_w', 'new_m_sb_norm_w', 'new_m_w_out', 'new_m_norm_mem_w', 'new_m_norm_memkv_w', 'new_m_w_mq', 'new_m_w_mk', 'new_m_w_mv', 'new_m_w_mo', 'new_m_norm_ffn_w', 'new_m_w_up', 'new_m_conv_ffn_w', 'new_m_conv_ffn_b', 'new_m_w_down', 'new_m_norm_final_w', 'new_v_norm_mix_w', 'new_v_w_in', 'new_v_conv_ssd_w', 'new_v_conv_ssd_b', 'new_v_dt_bias', 'new_v_a_log', 'new_v_d_skip', 'new_v_ssd_norm_w', 'new_v_sb_norm_w', 'new_v_w_out', 'new_v_norm_mem_w', 'new_v_norm_memkv_w', 'new_v_w_mq', 'new_v_w_mk', 'new_v_w_mv', 'new_v_w_mo', 'new_v_norm_ffn_w', 'new_v_w_up', 'new_v_conv_ffn_w', 'new_v_conv_ffn_b', 'new_v_w_down', 'new_v_norm_final_w']
TWIN_LEAF_KINDS = {'loss': 'loss', 'grad_x': 'grad_x', 'grad_norm_mix_w': 'grad_w', 'grad_w_in': 'grad_w', 'grad_conv_ssd_w': 'grad_w', 'grad_conv_ssd_b': 'grad_w', 'grad_dt_bias': 'grad_w', 'grad_a_log': 'grad_w', 'grad_d_skip': 'grad_w', 'grad_ssd_norm_w': 'grad_w', 'grad_sb_norm_w': 'grad_w', 'grad_w_out': 'grad_w', 'grad_norm_mem_w': 'grad_w', 'grad_norm_memkv_w': 'grad_w', 'grad_w_mq': 'grad_w', 'grad_w_mk': 'grad_w', 'grad_w_mv': 'grad_w', 'grad_w_mo': 'grad_w', 'grad_norm_ffn_w': 'grad_w', 'grad_w_up': 'grad_w', 'grad_conv_ffn_w': 'grad_w', 'grad_conv_ffn_b': 'grad_w', 'grad_w_down': 'grad_w', 'grad_norm_final_w': 'grad_w', 'delta_norm_mix_w': 'delta_w', 'delta_w_in': 'delta_w', 'delta_conv_ssd_w': 'delta_w', 'delta_conv_ssd_b': 'delta_w', 'delta_dt_bias': 'delta_w', 'delta_a_log': 'delta_w', 'delta_d_skip': 'delta_w', 'delta_ssd_norm_w': 'delta_w', 'delta_sb_norm_w': 'delta_w', 'delta_w_out': 'delta_w', 'delta_norm_mem_w': 'delta_w', 'delta_norm_memkv_w': 'delta_w', 'delta_w_mq': 'delta_w', 'delta_w_mk': 'delta_w', 'delta_w_mv': 'delta_w', 'delta_w_mo': 'delta_w', 'delta_norm_ffn_w': 'delta_w', 'delta_w_up': 'delta_w', 'delta_conv_ffn_w': 'delta_w', 'delta_conv_ffn_b': 'delta_w', 'delta_w_down': 'delta_w', 'delta_norm_final_w': 'delta_w', 'new_m_norm_mix_w': 'new_m', 'new_m_w_in': 'new_m', 'new_m_conv_ssd_w': 'new_m', 'new_m_conv_ssd_b': 'new_m', 'new_m_dt_bias': 'new_m', 'new_m_a_log': 'new_m', 'new_m_d_skip': 'new_m', 'new_m_ssd_norm_w': 'new_m', 'new_m_sb_norm_w': 'new_m', 'new_m_w_out': 'new_m', 'new_m_norm_mem_w': 'new_m', 'new_m_norm_memkv_w': 'new_m', 'new_m_w_mq': 'new_m', 'new_m_w_mk': 'new_m', 'new_m_w_mv': 'new_m', 'new_m_w_mo': 'new_m', 'new_m_norm_ffn_w': 'new_m', 'new_m_w_up': 'new_m', 'new_m_conv_ffn_w': 'new_m', 'new_m_conv_ffn_b': 'new_m', 'new_m_w_down': 'new_m', 'new_m_norm_final_w': 'new_m', 'new_v_norm_mix_w': 'new_v', 'new_v_w_in': 'new_v', 'new_v_conv_ssd_w': 'new_v', 'new_v_conv_ssd_b': 'new_v', 'new_v_dt_bias': 'new_v', 'new_v_a_log': 'new_v', 'new_v_d_skip': 'new_v', 'new_v_ssd_norm_w': 'new_v', 'new_v_sb_norm_w': 'new_v', 'new_v_w_out': 'new_v', 'new_v_norm_mem_w': 'new_v', 'new_v_norm_memkv_w': 'new_v', 'new_v_w_mq': 'new_v', 'new_v_w_mk': 'new_v', 'new_v_w_mv': 'new_v', 'new_v_w_mo': 'new_v', 'new_v_norm_ffn_w': 'new_v', 'new_v_w_up': 'new_v', 'new_v_conv_ffn_w': 'new_v', 'new_v_conv_ffn_b': 'new_v', 'new_v_w_down': 'new_v', 'new_v_norm_final_w': 'new_v'}


def _forward(args):
    return _fwd_reference(*[args[k] for k in FWD_PARAMS])


def _output_shape():
    def fwd():
        inp = _fwd_setup_inputs(0)
        return _fwd_reference(*[inp[k] for k in FWD_PARAMS])
    out = _jax.eval_shape(fwd)
    return out.shape, out.dtype

N_MICROBATCH = 1
ADAM_LR = 0.001
ADAM_B1 = 0.9
ADAM_B2 = 0.999
ADAM_EPS = 1e-08
ADAM_WD = 0.01
ADAM_STEP = 10
PER_EXAMPLE_BATCH_AXIS = {'x': 0, 'mem': 0, 'loss_target': 0}
SHARED_INPUTS = []
_WEIGHT_DTYPES = {'norm_mix_w': _jnp.float32, 'w_in': _jnp.float32, 'conv_ssd_w': _jnp.float32, 'conv_ssd_b': _jnp.float32, 'dt_bias': _jnp.float32, 'a_log': _jnp.float32, 'd_skip': _jnp.float32, 'ssd_norm_w': _jnp.float32, 'sb_norm_w': _jnp.float32, 'w_out': _jnp.float32, 'norm_mem_w': _jnp.float32, 'norm_memkv_w': _jnp.float32, 'w_mq': _jnp.float32, 'w_mk': _jnp.float32, 'w_mv': _jnp.float32, 'w_mo': _jnp.float32, 'norm_ffn_w': _jnp.float32, 'w_up': _jnp.float32, 'conv_ffn_w': _jnp.float32, 'conv_ffn_b': _jnp.float32, 'w_down': _jnp.float32, 'norm_final_w': _jnp.float32}
MOMENT_SCALE = {'norm_mix_w': 2.622171e-01, 'w_in': 1.108796e-01, 'conv_ssd_w': 1.167022e-01, 'conv_ssd_b': 1.533431e-01, 'dt_bias': 6.267829e-01, 'a_log': 3.196169e-01, 'd_skip': 1.168047e+00, 'ssd_norm_w': 1.370443e-01, 'sb_norm_w': 1.337071e-01, 'w_out': 1.927783e-01, 'norm_mem_w': 2.029144e-02, 'norm_memkv_w': 3.390645e-02, 'w_mq': 2.001206e-02, 'w_mk': 2.005877e-02, 'w_mv': 2.059232e-02, 'w_mo': 2.043640e-02, 'norm_ffn_w': 1.422747e-01, 'w_up': 6.043765e-02, 'conv_ffn_w': 6.147552e-02, 'conv_ffn_b': 5.966561e-02, 'w_down': 9.859030e-02, 'norm_final_w': 6.400172e+01}


def _to_microbatches(a, axis):
    t = _jnp.moveaxis(a, axis, 0)
    t = t.reshape((N_MICROBATCH, t.shape[0] // N_MICROBATCH) + t.shape[1:])
    return _jnp.moveaxis(t, 1, axis + 1)


def setup_inputs(seed: int = 0) -> dict:
    inp = _fwd_setup_inputs(seed)
    key = _jax.random.fold_in(_jax.random.key(seed), 7919)
    shape, _ = _output_shape()
    out = dict(inp)
    out["loss_target"] = _jax.random.normal(_jax.random.fold_in(key, 0), shape, _jnp.float32)
    for i, name in enumerate(TWIN_WEIGHTS):
        w = inp[name].astype(_jnp.float32)
        if MOMENT_SCALE is None:
            s = _jnp.sqrt(_jnp.mean(_jnp.square(w)) + 1e-30)
        else:
            s = MOMENT_SCALE[name]
        km, kv = _jax.random.split(_jax.random.fold_in(key, i + 1))
        out[name] = w
        out["m_" + name] = s * _jax.random.normal(km, w.shape, _jnp.float32)
        out["v_" + name] = (s * s) * _jax.random.uniform(kv, w.shape, _jnp.float32, 0.5, 1.5)
    if N_MICROBATCH > 1:
        for name, axis in PER_EXAMPLE_BATCH_AXIS.items():
            out[name] = _to_microbatches(out[name], axis)
    return {'x': out['x'], 'mem': out['mem'], 'norm_mix_w': out['norm_mix_w'], 'w_in': out['w_in'], 'conv_ssd_w': out['conv_ssd_w'], 'conv_ssd_b': out['conv_ssd_b'], 'dt_bias': out['dt_bias'], 'a_log': out['a_log'], 'd_skip': out['d_skip'], 'ssd_norm_w': out['ssd_norm_w'], 'sb_norm_w': out['sb_norm_w'], 'w_out': out['w_out'], 'norm_mem_w': out['norm_mem_w'], 'norm_memkv_w': out['norm_memkv_w'], 'w_mq': out['w_mq'], 'w_mk': out['w_mk'], 'w_mv': out['w_mv'], 'w_mo': out['w_mo'], 'norm_ffn_w': out['norm_ffn_w'], 'w_up': out['w_up'], 'conv_ffn_w': out['conv_ffn_w'], 'conv_ffn_b': out['conv_ffn_b'], 'w_down': out['w_down'], 'norm_final_w': out['norm_final_w'], 'loss_target': out['loss_target'], 'm_norm_mix_w': out['m_norm_mix_w'], 'm_w_in': out['m_w_in'], 'm_conv_ssd_w': out['m_conv_ssd_w'], 'm_conv_ssd_b': out['m_conv_ssd_b'], 'm_dt_bias': out['m_dt_bias'], 'm_a_log': out['m_a_log'], 'm_d_skip': out['m_d_skip'], 'm_ssd_norm_w': out['m_ssd_norm_w'], 'm_sb_norm_w': out['m_sb_norm_w'], 'm_w_out': out['m_w_out'], 'm_norm_mem_w': out['m_norm_mem_w'], 'm_norm_memkv_w': out['m_norm_memkv_w'], 'm_w_mq': out['m_w_mq'], 'm_w_mk': out['m_w_mk'], 'm_w_mv': out['m_w_mv'], 'm_w_mo': out['m_w_mo'], 'm_norm_ffn_w': out['m_norm_ffn_w'], 'm_w_up': out['m_w_up'], 'm_conv_ffn_w': out['m_conv_ffn_w'], 'm_conv_ffn_b': out['m_conv_ffn_b'], 'm_w_down': out['m_w_down'], 'm_norm_final_w': out['m_norm_final_w'], 'v_norm_mix_w': out['v_norm_mix_w'], 'v_w_in': out['v_w_in'], 'v_conv_ssd_w': out['v_conv_ssd_w'], 'v_conv_ssd_b': out['v_conv_ssd_b'], 'v_dt_bias': out['v_dt_bias'], 'v_a_log': out['v_a_log'], 'v_d_skip': out['v_d_skip'], 'v_ssd_norm_w': out['v_ssd_norm_w'], 'v_sb_norm_w': out['v_sb_norm_w'], 'v_w_out': out['v_w_out'], 'v_norm_mem_w': out['v_norm_mem_w'], 'v_norm_memkv_w': out['v_norm_memkv_w'], 'v_w_mq': out['v_w_mq'], 'v_w_mk': out['v_w_mk'], 'v_w_mv': out['v_w_mv'], 'v_w_mo': out['v_w_mo'], 'v_norm_ffn_w': out['v_norm_ffn_w'], 'v_w_up': out['v_w_up'], 'v_conv_ffn_w': out['v_conv_ffn_w'], 'v_conv_ffn_b': out['v_conv_ffn_b'], 'v_w_down': out['v_w_down'], 'v_norm_final_w': out['v_norm_final_w']}


def _loss(weights, diff, rest, loss_target):
    with _jax.named_scope("forward"):
        args = {**rest, TWIN_DIFF_INPUT: diff, **{k: w.astype(_WEIGHT_DTYPES[k]) for k, w in weights.items()}}
        y = _forward(args)
    with _jax.named_scope("loss_head"):
        err = _jnp.square(y.astype(_jnp.float32) - loss_target)
        return 0.5 * _jnp.sum(_jnp.mean(err, axis=-1)) if err.ndim else 0.5 * err


def _adamw(w, g, m, v):
    m = ADAM_B1 * m + (1.0 - ADAM_B1) * g
    v = ADAM_B2 * v + (1.0 - ADAM_B2) * _jnp.square(g)
    m_hat = m / (1.0 - ADAM_B1 ** ADAM_STEP)
    v_hat = v / (1.0 - ADAM_B2 ** ADAM_STEP)
    delta = -ADAM_LR * (m_hat / (_jnp.sqrt(v_hat) + ADAM_EPS) + ADAM_WD * w)
    return delta, m, v


def reference(x, mem, norm_mix_w, w_in, conv_ssd_w, conv_ssd_b, dt_bias, a_log, d_skip, ssd_norm_w, sb_norm_w, w_out, norm_mem_w, norm_memkv_w, w_mq, w_mk, w_mv, w_mo, norm_ffn_w, w_up, conv_ffn_w, conv_ffn_b, w_down, norm_final_w, loss_target, m_norm_mix_w, m_w_in, m_conv_ssd_w, m_conv_ssd_b, m_dt_bias, m_a_log, m_d_skip, m_ssd_norm_w, m_sb_norm_w, m_w_out, m_norm_mem_w, m_norm_memkv_w, m_w_mq, m_w_mk, m_w_mv, m_w_mo, m_norm_ffn_w, m_w_up, m_conv_ffn_w, m_conv_ffn_b, m_w_down, m_norm_final_w, v_norm_mix_w, v_w_in, v_conv_ssd_w, v_conv_ssd_b, v_dt_bias, v_a_log, v_d_skip, v_ssd_norm_w, v_sb_norm_w, v_w_out, v_norm_mem_w, v_norm_memkv_w, v_w_mq, v_w_mk, v_w_mv, v_w_mo, v_norm_ffn_w, v_w_up, v_conv_ffn_w, v_conv_ffn_b, v_w_down, v_norm_final_w):
    given = dict(x=x, mem=mem, norm_mix_w=norm_mix_w, w_in=w_in, conv_ssd_w=conv_ssd_w, conv_ssd_b=conv_ssd_b, dt_bias=dt_bias, a_log=a_log, d_skip=d_skip, ssd_norm_w=ssd_norm_w, sb_norm_w=sb_norm_w, w_out=w_out, norm_mem_w=norm_mem_w, norm_memkv_w=norm_memkv_w, w_mq=w_mq, w_mk=w_mk, w_mv=w_mv, w_mo=w_mo, norm_ffn_w=norm_ffn_w, w_up=w_up, conv_ffn_w=conv_ffn_w, conv_ffn_b=conv_ffn_b, w_down=w_down, norm_final_w=norm_final_w, loss_target=loss_target, m_norm_mix_w=m_norm_mix_w, m_w_in=m_w_in, m_conv_ssd_w=m_conv_ssd_w, m_conv_ssd_b=m_conv_ssd_b, m_dt_bias=m_dt_bias, m_a_log=m_a_log, m_d_skip=m_d_skip, m_ssd_norm_w=m_ssd_norm_w, m_sb_norm_w=m_sb_norm_w, m_w_out=m_w_out, m_norm_mem_w=m_norm_mem_w, m_norm_memkv_w=m_norm_memkv_w, m_w_mq=m_w_mq, m_w_mk=m_w_mk, m_w_mv=m_w_mv, m_w_mo=m_w_mo, m_norm_ffn_w=m_norm_ffn_w, m_w_up=m_w_up, m_conv_ffn_w=m_conv_ffn_w, m_conv_ffn_b=m_conv_ffn_b, m_w_down=m_w_down, m_norm_final_w=m_norm_final_w, v_norm_mix_w=v_norm_mix_w, v_w_in=v_w_in, v_conv_ssd_w=v_conv_ssd_w, v_conv_ssd_b=v_conv_ssd_b, v_dt_bias=v_dt_bias, v_a_log=v_a_log, v_d_skip=v_d_skip, v_ssd_norm_w=v_ssd_norm_w, v_sb_norm_w=v_sb_norm_w, v_w_out=v_w_out, v_norm_mem_w=v_norm_mem_w, v_norm_memkv_w=v_norm_memkv_w, v_w_mq=v_w_mq, v_w_mk=v_w_mk, v_w_mv=v_w_mv, v_w_mo=v_w_mo, v_norm_ffn_w=v_norm_ffn_w, v_w_up=v_w_up, v_conv_ffn_w=v_conv_ffn_w, v_conv_ffn_b=v_conv_ffn_b, v_w_down=v_w_down, v_norm_final_w=v_norm_final_w)
    weights = {n: given[n] for n in TWIN_WEIGHTS}
    shared = {n: given[n] for n in SHARED_INPUTS}
    per_example = {n: given[n] for n in ['x', 'mem']}
    grad_fn = _jax.value_and_grad(_loss, argnums=(0, 1))

    def one_microbatch(ex, loss_target):
        ex = dict(ex)
        diff = ex.pop(TWIN_DIFF_INPUT)
        return grad_fn(weights, diff, {**shared, **ex}, loss_target)

    if N_MICROBATCH == 1:
        loss, (grad_w, grad_x) = one_microbatch(per_example, given["loss_target"])
    else:
        def body(carry, xs):
            loss_sum, grad_sum = carry
            l_k, (gw_k, gx_k) = one_microbatch(xs[0], xs[1])
            with _jax.named_scope("update"):
                return (loss_sum + l_k, _jax.tree.map(_jnp.add, grad_sum, gw_k)), gx_k

        init = (_jnp.zeros((), _jnp.float32), _jax.tree.map(_jnp.zeros_like, weights))
        (loss, grad_w), grad_x = _jax.lax.scan(body, init, (per_example, given["loss_target"]))
    with _jax.named_scope("update"):
        delta_w, new_m, new_v = {}, {}, {}
        for n in TWIN_WEIGHTS:
            delta_w[n], new_m[n], new_v[n] = _adamw(weights[n], grad_w[n], given["m_" + n], given["v_" + n])
    return (loss, grad_x, *[grad_w[n] for n in TWIN_WEIGHTS], *[delta_w[n] for n in TWIN_WEIGHTS],
            *[new_m[n] for n in TWIN_WEIGHTS], *[new_v[n] for n in TWIN_WEIGHTS])
```

```python
import math

import jax
import jax.numpy as jnp
from jax import lax
from jax.experimental import pallas as pl
from jax.experimental.pallas import tpu as pltpu

F32 = jnp.float32
_MXU = jnp.bfloat16
EPS = 1e-6
_VMEM_LIMIT = 48 * 1024 * 1024
_HI = lax.Precision.HIGHEST

SSD_HEADS = 16
SSD_HEAD_DIM = 64
SSD_GROUPS = 2
SSD_STATE = 128
SSD_CHUNK = 128
SSD_INNER = SSD_HEADS * SSD_HEAD_DIM
SSD_XBC = SSD_INNER + 2 * SSD_GROUPS * SSD_STATE
SB_HEADS = 16
SB_HEAD_DIM = 64
SB_WIDTH = SB_HEADS * SB_HEAD_DIM
MEM_HEADS = 4
DT_PAD = 128

ADAM_LR = 0.001
ADAM_B1 = 0.9
ADAM_B2 = 0.999
ADAM_EPS = 1e-08
ADAM_WD = 0.01
ADAM_STEP = 10

WEIGHTS = ['norm_mix_w', 'w_in', 'conv_ssd_w', 'conv_ssd_b', 'dt_bias', 'a_log', 'd_skip', 'ssd_norm_w',
           'sb_norm_w', 'w_out', 'norm_mem_w', 'norm_memkv_w', 'w_mq', 'w_mk', 'w_mv', 'w_mo', 'norm_ffn_w',
           'w_up', 'conv_ffn_w', 'conv_ffn_b', 'w_down', 'norm_final_w']
BIG = ['w_in', 'w_out', 'w_mq', 'w_mk', 'w_mv', 'w_mo', 'w_up', 'w_down']
COL_SPLIT = ('w_in', 'w_up', 'conv_ssd_w', 'conv_ffn_w')
CONVW = ['conv_ssd_w', 'conv_ffn_w']
SMALL = ['norm_mix_w', 'conv_ssd_b', 'dt_bias', 'a_log', 'd_skip', 'ssd_norm_w', 'sb_norm_w', 'norm_mem_w',
         'norm_memkv_w', 'norm_ffn_w', 'conv_ffn_b', 'norm_final_w']


def _cparams(*sem):
    return pltpu.CompilerParams(dimension_semantics=sem if sem else None, vmem_limit_bytes=_VMEM_LIMIT)


def _pick(n, cap, mult=128):
    best = None
    for d in range(mult, min(n, cap) + 1, mult):
        if n % d == 0:
            best = d
    return n if best is None else best


def _dot(a, b, ca, cb):
    return lax.dot_general(a.astype(_MXU), b.astype(_MXU), (((ca,), (cb,)), ((), ())), preferred_element_type=F32)


def _sigmoid(v):
    return 1.0 / (1.0 + jnp.exp(-v))


def _log1p(u):
    w = 1.0 + u
    return jnp.where(w == 1.0, u, jnp.log(w) * (u / (w - 1.0)))


def _mm(a, b, *, ta=False, tb=False, res=None, out_dtype=F32, name):
    if ta:
        K, M = a.shape
    else:
        M, K = a.shape
    if tb:
        N, K2 = b.shape
    else:
        K2, N = b.shape
    assert K == K2, (a.shape, b.shape)
    tm = _pick(M, 1408, 128 if ta else 16)
    tn = _pick(N, 1536)
    tk = _pick(K, 1536)
    nk = K // tk
    a_spec = pl.BlockSpec((tk, tm), lambda i, j, k: (k, i)) if ta else pl.BlockSpec((tm, tk), lambda i, j, k: (i, k))
    b_spec = pl.BlockSpec((tn, tk), lambda i, j, k: (j, k)) if tb else pl.BlockSpec((tk, tn), lambda i, j, k: (k, j))
    o_spec = pl.BlockSpec((tm, tn), lambda i, j, k: (i, j))
    ca, cb = (0 if ta else 1), (1 if tb else 0)

    def body(*refs):
        if res is None:
            a_ref, b_ref, o_ref, acc_ref = refs
            r_ref = None
        else:
            a_ref, b_ref, r_ref, o_ref, acc_ref = refs
        k = pl.program_id(2)

        @pl.when(k == 0)
        def _():
            acc_ref[...] = jnp.zeros_like(acc_ref)

        acc_ref[...] += _dot(a_ref[...], b_ref[...], ca, cb)

        @pl.when(k == nk - 1)
        def _():
            r = acc_ref[...]
            if r_ref is not None:
                r = r + r_ref[...].astype(F32)
            o_ref[...] = r.astype(o_ref.dtype)

    ins = [a, b] + ([] if res is None else [res])
    in_specs = [a_spec, b_spec] + ([] if res is None else [o_spec])
    return pl.pallas_call(
        body, grid=(M // tm, N // tn, nk), in_specs=in_specs, out_specs=o_spec,
        out_shape=jax.ShapeDtypeStruct((M, N), out_dtype), scratch_shapes=[pltpu.VMEM((tm, tn), F32)],
        compiler_params=_cparams("parallel", "parallel", "arbitrary"), name=name)(*ins)


def _rows(S, cap):
    return _pick(S, cap, 8)


def _rms_fwd(x, w, name):
    S, D = x.shape
    tm = _rows(S, 512)

    def body(x_ref, w_ref, o_ref):
        xv = x_ref[...]
        r = lax.rsqrt(jnp.mean(xv * xv, axis=-1, keepdims=True) + EPS)
        o_ref[...] = ((xv * r) * w_ref[...]).astype(o_ref.dtype)

    row = pl.BlockSpec((tm, D), lambda i: (i, 0))
    return pl.pallas_call(body, grid=(S // tm,), in_specs=[row, pl.BlockSpec((1, D), lambda i: (0, 0))], out_specs=row,
                          out_shape=jax.ShapeDtypeStruct((S, D), _MXU), compiler_params=_cparams("parallel"), name=name)(x, w)


def _rms_bwd(dh, x, w, dres, name):
    S, D = x.shape
    tm = _rows(S, 256)

    def body(*refs):
        if dres is None:
            dh_ref, x_ref, w_ref, dx_ref, dxb_ref, dw_ref = refs
            dres_ref = None
        else:
            dh_ref, x_ref, w_ref, dres_ref, dx_ref, dxb_ref, dw_ref = refs
        xv = x_ref[...]
        r = lax.rsqrt(jnp.mean(xv * xv, axis=-1, keepdims=True) + EPS)
        xn = xv * r
        dy = dh_ref[...].astype(F32)

        @pl.when(pl.program_id(0) == 0)
        def _():
            dw_ref[...] = jnp.zeros_like(dw_ref)

        dw_ref[...] += jnp.sum(dy * xn, axis=0, keepdims=True)
        dxn = dy * w_ref[...]
        dx = r * (dxn - xn * jnp.mean(dxn * xn, axis=-1, keepdims=True))
        if dres_ref is not None:
            dx = dx + dres_ref[...]
        dx_ref[...] = dx
        dxb_ref[...] = dx.astype(dxb_ref.dtype)

    row = pl.BlockSpec((tm, D), lambda i: (i, 0))
    vec = pl.BlockSpec((1, D), lambda i: (0, 0))
    ins = [dh, x, w] + ([] if dres is None else [dres])
    in_specs = [row, row, vec] + ([] if dres is None else [row])
    return pl.pallas_call(
        body, grid=(S // tm,), in_specs=in_specs, out_specs=[row, row, vec],
        out_shape=[jax.ShapeDtypeStruct((S, D), F32), jax.ShapeDtypeStruct((S, D), _MXU), jax.ShapeDtypeStruct((1, D), F32)],
        compiler_params=_cparams("arbitrary"), name=name)(*ins)


def _loss_bwd(x, tgt, w, name):
    S, D = x.shape
    tm = _rows(S, 256)

    def body(x_ref, t_ref, w_ref, dx_ref, dxb_ref, dw_ref, loss_ref):
        xv = x_ref[...]
        r = lax.rsqrt(jnp.mean(xv * xv, axis=-1, keepdims=True) + EPS)
        xn = xv * r
        e = xn * w_ref[...] - t_ref[...]

        @pl.when(pl.program_id(0) == 0)
        def _():
            dw_ref[...] = jnp.zeros_like(dw_ref)
            loss_ref[...] = jnp.zeros_like(loss_ref)

        tok = jnp.mean(e * e, axis=-1, keepdims=True)
        loss_ref[...] += jnp.broadcast_to(0.5 * jnp.sum(tok, axis=0, keepdims=True), loss_ref.shape)
        dy = e * (1.0 / D)
        dw_ref[...] += jnp.sum(dy * xn, axis=0, keepdims=True)
        dxn = dy * w_ref[...]
        dx = r * (dxn - xn * jnp.mean(dxn * xn, axis=-1, keepdims=True))
        dx_ref[...] = dx
        dxb_ref[...] = dx.astype(dxb_ref.dtype)

    row = pl.BlockSpec((tm, D), lambda i: (i, 0))
    vec = pl.BlockSpec((1, D), lambda i: (0, 0))
    return pl.pallas_call(
        body, grid=(S // tm,), in_specs=[row, row, vec],
        out_specs=[row, row, vec, pl.BlockSpec((8, 128), lambda i: (0, 0))],
        out_shape=[jax.ShapeDtypeStruct((S, D), F32), jax.ShapeDtypeStruct((S, D), _MXU),
                   jax.ShapeDtypeStruct((1, D), F32), jax.ShapeDtypeStruct((8, 128), F32)],
        compiler_params=_cparams("arbitrary"), name=name)(x, tgt, w)


def _conv_tiles(S, C):
    return _rows(S, 256), _pick(C, 1536)


def _dwconv_fwd(x, w, b, name):
    S, C = x.shape
    K = w.shape[0]
    tm, tc = _conv_tiles(S, C)

    def body(x_ref, p_ref, w_ref, b_ref, o_ref):
        cur = x_ref[...]
        prev = jnp.where(pl.program_id(0) > 0, p_ref[...], 0.0)
        xx = jnp.concatenate([prev, cur], axis=0)
        acc = cur * w_ref[K - 1:K, :] + b_ref[...]
        for d in range(1, K):
            acc = acc + pltpu.roll(xx, d, 0)[8:, :] * w_ref[K - 1 - d:K - d, :]
        o_ref[...] = acc

    return pl.pallas_call(
        body, grid=(S // tm, C // tc),
        in_specs=[pl.BlockSpec((tm, tc), lambda i, j: (i, j)),
                  pl.BlockSpec((8, tc), lambda i, j: (jnp.maximum(i * (tm // 8) - 1, 0), j)),
                  pl.BlockSpec((K, tc), lambda i, j: (0, j)), pl.BlockSpec((1, tc), lambda i, j: (0, j))],
        out_specs=pl.BlockSpec((tm, tc), lambda i, j: (i, j)), out_shape=jax.ShapeDtypeStruct((S, C), F32),
        compiler_params=_cparams("parallel", "parallel"), name=name)(x, x, w, b)


def _conv_bwd_w(x, dy, K, name):
    S, C = x.shape
    tm, tc = _conv_tiles(S, C)

    def body(x_ref, p_ref, dy_ref, dw_ref, db_ref):
        i = pl.program_id(1)

        @pl.when(i == 0)
        def _():
            dw_ref[...] = jnp.zeros_like(dw_ref)
            db_ref[...] = jnp.zeros_like(db_ref)

        cur = x_ref[...]
        prev = jnp.where(i > 0, p_ref[...], 0.0)
        xx = jnp.concatenate([prev, cur], axis=0)
        g = dy_ref[...].astype(F32)
        db_ref[...] += jnp.sum(g, axis=0, keepdims=True)
        dw_ref[K - 1:K, :] += jnp.sum(g * cur, axis=0, keepdims=True)
        for d in range(1, K):
            dw_ref[K - 1 - d:K - d, :] += jnp.sum(g * pltpu.roll(xx, d, 0)[8:, :], axis=0, keepdims=True)

    return pl.pallas_call(
        body, grid=(C // tc, S // tm),
        in_specs=[pl.BlockSpec((tm, tc), lambda j, i: (i, j)),
                  pl.BlockSpec((8, tc), lambda j, i: (jnp.maximum(i * (tm // 8) - 1, 0), j)),
                  pl.BlockSpec((tm, tc), lambda j, i: (i, j))],
        out_specs=[pl.BlockSpec((K, tc), lambda j, i: (0, j)), pl.BlockSpec((1, tc), lambda j, i: (0, j))],
        out_shape=[jax.ShapeDtypeStruct((K, C), F32), jax.ShapeDtypeStruct((1, C), F32)],
        compiler_params=_cparams("parallel", "arbitrary"), name=name)(x, x, dy)


def _dwconv_bwd_x(dy, w, name):
    S, C = dy.shape
    K = w.shape[0]
    tm, tc = _conv_tiles(S, C)
    last = S // tm - 1

    def body(g_ref, n_ref, w_ref, o_ref):
        cur = g_ref[...]
        nxt = jnp.where(pl.program_id(0) < last, n_ref[...], 0.0)
        xx = jnp.concatenate([cur, nxt], axis=0)
        acc = cur * w_ref[K - 1:K, :]
        for d in range(1, K):
            acc = acc + pltpu.roll(xx, tm + 8 - d, 0)[:tm, :] * w_ref[K - 1 - d:K - d, :]
        o_ref[...] = acc.astype(o_ref.dtype)

    return pl.pallas_call(
        body, grid=(S // tm, C // tc),
        in_specs=[pl.BlockSpec((tm, tc), lambda i, j: (i, j)),
                  pl.BlockSpec((8, tc), lambda i, j: (jnp.minimum((i + 1) * (tm // 8), S // 8 - 1), j)),
                  pl.BlockSpec((K, tc), lambda i, j: (0, j))],
        out_specs=pl.BlockSpec((tm, tc), lambda i, j: (i, j)), out_shape=jax.ShapeDtypeStruct((S, C), _MXU),
        compiler_params=_cparams("parallel", "parallel"), name=name)(dy, dy, w)


def _silu_fwd(pre, name):
    S, C = pre.shape
    tm, tc = _conv_tiles(S, C)

    def body(p_ref, o_ref):
        p = p_ref[...]
        o_ref[...] = p * _sigmoid(p)

    blk = pl.BlockSpec((tm, tc), lambda i, j: (i, j))
    return pl.pallas_call(body, grid=(S // tm, C // tc), in_specs=[blk], out_specs=blk,
                          out_shape=jax.ShapeDtypeStruct((S, C), F32), compiler_params=_cparams("parallel", "parallel"),
                          name=name)(pre)


def _silu_bwd(pre, dact, name):
    S, C = pre.shape
    tm, tc = _conv_tiles(S, C)

    def body(p_ref, g_ref, o_ref):
        p = p_ref[...]
        s = _sigmoid(p)
        o_ref[...] = g_ref[...] * (s * (1.0 + p * (1.0 - s)))

    blk = pl.BlockSpec((tm, tc), lambda i, j: (i, j))
    return pl.pallas_call(body, grid=(S // tm, C // tc), in_specs=[blk, blk], out_specs=blk,
                          out_shape=jax.ShapeDtypeStruct((S, C), F32), compiler_params=_cparams("parallel", "parallel"),
                          name=name)(pre, dact)


def _glu_fwd(u, name):
    S, C = u.shape
    Fh = C // 2
    tm = _rows(S, 128)

    def body(u_ref, o_ref):
        g = u_ref[:, :Fh]
        o_ref[...] = (g * _sigmoid(g) * u_ref[:, Fh:]).astype(o_ref.dtype)

    return pl.pallas_call(body, grid=(S // tm,), in_specs=[pl.BlockSpec((tm, C), lambda i: (i, 0))],
                          out_specs=pl.BlockSpec((tm, Fh), lambda i: (i, 0)), out_shape=jax.ShapeDtypeStruct((S, Fh), _MXU),
                          compiler_params=_cparams("parallel"), name=name)(u)


def _glu_bwd(u, dact, name):
    S, C = u.shape
    Fh = C // 2
    tm = _rows(S, 128)

    def body(u_ref, g_ref, o_ref):
        g = u_ref[:, :Fh]
        val = u_ref[:, Fh:]
        da = g_ref[...].astype(F32)
        s = _sigmoid(g)
        o_ref[:, :Fh] = da * val * (s * (1.0 + g * (1.0 - s)))
        o_ref[:, Fh:] = da * (g * s)

    return pl.pallas_call(body, grid=(S // tm,),
                          in_specs=[pl.BlockSpec((tm, C), lambda i: (i, 0)), pl.BlockSpec((tm, Fh), lambda i: (i, 0))],
                          out_specs=pl.BlockSpec((tm, C), lambda i: (i, 0)), out_shape=jax.ShapeDtypeStruct((S, C), F32),
                          compiler_params=_cparams("parallel"), name=name)(u, dact)


def _xattn_fwd(q, k, v, name):
    S, D = q.shape
    M = k.shape[0]
    hd = D // MEM_HEADS
    tm = _rows(S, 512)
    scale = 1.0 / math.sqrt(hd)

    def body(q_ref, k_ref, v_ref, o_ref):
        for h in range(MEM_HEADS):
            sl = slice(h * hd, (h + 1) * hd)
            s = _dot(q_ref[:, sl], k_ref[:, sl], 1, 1) * scale
            p = jnp.exp(s - jnp.max(s, axis=-1, keepdims=True))
            p = p / jnp.sum(p, axis=-1, keepdims=True)
            o_ref[:, sl] = _dot(p, v_ref[:, sl], 1, 0).astype(o_ref.dtype)

    kv = pl.BlockSpec((M, D), lambda i: (0, 0))
    row = pl.BlockSpec((tm, D), lambda i: (i, 0))
    return pl.pallas_call(body, grid=(S // tm,), in_specs=[row, kv, kv], out_specs=row,
                          out_shape=jax.ShapeDtypeStruct((S, D), _MXU), compiler_params=_cparams("parallel"), name=name)(q, k, v)


def _xattn_bwd(q, k, v, do, name):
    S, D = q.shape
    M = k.shape[0]
    hd = D // MEM_HEADS
    tm = _rows(S, 512)
    scale = 1.0 / math.sqrt(hd)

    def body(q_ref, k_ref, v_ref, do_ref, dq_ref, dk_ref, dv_ref):
        @pl.when(pl.program_id(0) == 0)
        def _():
            dk_ref[...] = jnp.zeros_like(dk_ref)
            dv_ref[...] = jnp.zeros_like(dv_ref)

        for h in range(MEM_HEADS):
            sl = slice(h * hd, (h + 1) * hd)
            qh, kh, vh, doh = q_ref[:, sl], k_ref[:, sl], v_ref[:, sl], do_ref[:, sl]
            s = _dot(qh, kh, 1, 1) * scale
            p = jnp.exp(s - jnp.max(s, axis=-1, keepdims=True))
            p = p / jnp.sum(p, axis=-1, keepdims=True)
            dp = _dot(doh, vh, 1, 1)
            dv_ref[:, sl] += _dot(p, doh, 0, 0)
            ds = (p * (dp - jnp.sum(dp * p, axis=-1, keepdims=True))) * scale
            dq_ref[:, sl] = _dot(ds, kh, 1, 0).astype(dq_ref.dtype)
            dk_ref[:, sl] += _dot(ds, qh, 0, 0)

    kv = pl.BlockSpec((M, D), lambda i: (0, 0))
    row = pl.BlockSpec((tm, D), lambda i: (i, 0))
    return pl.pallas_call(
        body, grid=(S // tm,), in_specs=[row, kv, kv, row], out_specs=[row, kv, kv],
        out_shape=[jax.ShapeDtypeStruct((S, D), _MXU), jax.ShapeDtypeStruct((M, D), F32), jax.ShapeDtypeStruct((M, D), F32)],
        compiler_params=_cparams("arbitrary"), name=name)(q, k, v, do)


def _tri(n, strict, upper):
    r = lax.broadcasted_iota(jnp.int32, (n, n), 0)
    c = lax.broadcasted_iota(jnp.int32, (n, n), 1)
    if upper:
        return (c > r) if strict else (c >= r)
    return (r > c) if strict else (r >= c)


def _ssd_prep(dtp, dt_bias, a_log, name):
    S = dtp.shape[0]
    L, H = SSD_CHUNK, SSD_HEADS

    def body(p_ref, b_ref, al_ref, dt_ref, cs_ref):
        v = p_ref[:, :H] + b_ref[...]
        dt = jnp.maximum(v, 0.0) + _log1p(jnp.exp(-jnp.abs(v)))
        dt_ref[...] = dt
        a = dt * (-jnp.exp(al_ref[...]))
        cs_ref[...] = jnp.dot(_tri(L, False, False).astype(F32), a, precision=_HI, preferred_element_type=F32)

    blk = pl.BlockSpec((L, H), lambda c: (c, 0))
    vec = pl.BlockSpec((1, H), lambda c: (0, 0))
    return pl.pallas_call(body, grid=(S // L,), in_specs=[pl.BlockSpec((L, DT_PAD), lambda c: (c, 0)), vec, vec],
                          out_specs=[blk, blk], out_shape=[jax.ShapeDtypeStruct((S, H), F32)] * 2,
                          compiler_params=_cparams("parallel"), name=name)(dtp, dt_bias, a_log)


def _head_col(blk_ref, h):
    sel = lax.broadcasted_iota(jnp.int32, (1, SSD_HEADS), 1) == h
    return jnp.sum(jnp.where(sel, blk_ref[...], 0.0), axis=1, keepdims=True)


def _ssd_fwd(xs, Bm, Cm, dt, cs, csT, name):
    H, S, P = xs.shape
    L, N = SSD_CHUNK, SSD_STATE
    nc = S // L
    rep = H // SSD_GROUPS

    def body(x_ref, b_ref, c_ref, dt_ref, cs_ref, csT_ref, y_ref, prev_ref, st_ref):
        c, h = pl.program_id(0), pl.program_id(1)

        @pl.when(c == 0)
        def _():
            st_ref[h] = jnp.zeros((P, N), F32)

        dtc = _head_col(dt_ref, h)
        csc = _head_col(cs_ref, h)
        csr = csT_ref[pl.ds(h, 1), :]
        cs_last = csr[:, L - 1:L]
        xc = x_ref[...] * dtc
        Bv, Cv = b_ref[...], c_ref[...]
        tril = _tri(L, False, False)
        lam = jnp.where(tril, jnp.exp(jnp.where(tril, csc - csr, 0.0)), 0.0)
        m = _dot(Cv, Bv, 1, 1) * lam
        prev = st_ref[h]
        y_ref[...] = _dot(m, xc, 1, 0) + _dot(Cv, prev, 1, 1) * jnp.exp(csc)
        prev_ref[...] = prev
        st_ref[h] = prev * jnp.exp(cs_last) + _dot(xc * jnp.exp(cs_last - csc), Bv, 0, 0)

    grp = lambda c, h: (h // rep, c, 0)
    return pl.pallas_call(
        body, grid=(nc, H),
        in_specs=[pl.BlockSpec((None, L, P), lambda c, h: (h, c, 0)), pl.BlockSpec((None, L, N), grp),
                  pl.BlockSpec((None, L, N), grp), pl.BlockSpec((L, H), lambda c, h: (c, 0)),
                  pl.BlockSpec((L, H), lambda c, h: (c, 0)), pl.BlockSpec((H, L), lambda c, h: (0, c))],
        out_specs=[pl.BlockSpec((None, L, P), lambda c, h: (h, c, 0)),
                   pl.BlockSpec((None, None, P, N), lambda c, h: (h, c, 0, 0))],
        out_shape=[jax.ShapeDtypeStruct((H, S, P), F32), jax.ShapeDtypeStruct((H, nc, P, N), F32)],
        scratch_shapes=[pltpu.VMEM((H, P, N), F32)],
        compiler_params=_cparams("arbitrary", "arbitrary"), name=name)(xs, Bm, Cm, dt, cs, csT)


def _ssd_bwd(xs, Bm, Cm, dt, cs, csT, prev, dy, a_log, d_skip, name):
    H, S, P = xs.shape
    L, N = SSD_CHUNK, SSD_STATE
    nc = S // L
    rep = H // SSD_GROUPS

    def body(x_ref, b_ref, c_ref, dt_ref, cs_ref, csT_ref, prev_ref, dy_ref, al_ref, dk_ref,
             dx_ref, db_ref, dc_ref, ddt_ref, da_ref, g_ref):
        ci, h = pl.program_id(0), pl.program_id(1)
        sel = lax.broadcasted_iota(jnp.int32, (1, H), 1) == h

        @pl.when(ci == 0)
        def _():
            g_ref[h] = jnp.zeros((P, N), F32)

        @pl.when((ci == 0) & (h == 0))
        def _():
            da_ref[...] = jnp.zeros_like(da_ref)

        @pl.when(h == 0)
        def _():
            ddt_ref[...] = jnp.zeros_like(ddt_ref)

        @pl.when(h % rep == 0)
        def _():
            db_ref[...] = jnp.zeros_like(db_ref)
            dc_ref[...] = jnp.zeros_like(dc_ref)

        A_h = -jnp.exp(jnp.sum(jnp.where(sel, al_ref[...], 0.0), axis=1, keepdims=True))
        dsk = jnp.sum(jnp.where(sel, dk_ref[...], 0.0), axis=1, keepdims=True)
        dtc = _head_col(dt_ref, h)
        csc = _head_col(cs_ref, h)
        csr = csT_ref[pl.ds(h, 1), :]
        cs_last = csr[:, L - 1:L]
        xv = x_ref[...]
        xc = xv * dtc
        Bv, Cv = b_ref[...], c_ref[...]
        dY = dy_ref[...]
        prv = prev_ref[...]
        G = g_ref[h]
        ecs = jnp.exp(csc)
        w = jnp.exp(cs_last - csc)
        cd = jnp.exp(cs_last)
        tril = _tri(L, False, False)
        triu = _tri(L, False, True)
        lam = jnp.where(tril, jnp.exp(jnp.where(tril, csc - csr, 0.0)), 0.0)
        lamT = jnp.where(triu, jnp.exp(jnp.where(triu, csr - csc, 0.0)), 0.0)
        m = _dot(Cv, Bv, 1, 1) * lam
        mT = _dot(Bv, Cv, 1, 1) * lamT
        dM = _dot(dY, xc, 1, 1)
        dMT = _dot(xc, dY, 1, 1)
        dxc = _dot(mT, dY, 1, 0)
        dC = _dot(dM * lam, Bv, 1, 0)
        dB = _dot(dMT * lamT, Cv, 1, 0)
        dcs = jnp.sum(dM * m, axis=1, keepdims=True) - jnp.sum(dMT * mT, axis=1, keepdims=True)
        yoff = _dot(Cv, prv, 1, 1) * ecs
        dcs = dcs + jnp.sum(dY * yoff, axis=1, keepdims=True)
        dYe = dY * ecs
        dprev = _dot(dYe, Cv, 0, 0)
        dC = dC + _dot(dYe, prv, 1, 0)
        BG = _dot(Bv, G, 1, 1)
        dxc = dxc + w * BG
        dB = dB + _dot(xc * w, G, 1, 0)
        dww = jnp.sum(xc * BG, axis=1, keepdims=True) * w
        dcs = dcs - dww
        extra = jnp.sum(dww, axis=0, keepdims=True) + cd * jnp.sum(jnp.sum(G * prv, axis=1, keepdims=True), axis=0, keepdims=True)
        g_ref[h] = G * cd + dprev
        da = jnp.dot(_tri(L, False, True).astype(F32), dcs, precision=_HI, preferred_element_type=F32) + extra
        ddt = da * A_h + jnp.sum(dxc * xv, axis=1, keepdims=True)
        dx_ref[...] = dxc * dtc + dY * dsk
        db_ref[...] += dB
        dc_ref[...] += dC
        ddt_ref[...] += jnp.where(sel, ddt, 0.0)
        da_ref[...] += jnp.where(sel, jnp.sum(da * dtc, axis=0, keepdims=True), 0.0)

    rc = lambda ci: nc - 1 - ci
    grp = lambda ci, h: (h // rep, rc(ci), 0)
    hd = lambda ci, h: (h, rc(ci), 0)
    tok = lambda ci, h: (rc(ci), 0)
    vec = pl.BlockSpec((1, H), lambda ci, h: (0, 0))
    return pl.pallas_call(
        body, grid=(nc, H),
        in_specs=[pl.BlockSpec((None, L, P), hd), pl.BlockSpec((None, L, N), grp), pl.BlockSpec((None, L, N), grp),
                  pl.BlockSpec((L, H), tok), pl.BlockSpec((L, H), tok), pl.BlockSpec((H, L), lambda ci, h: (0, rc(ci))),
                  pl.BlockSpec((None, None, P, N), lambda ci, h: (h, rc(ci), 0, 0)), pl.BlockSpec((None, L, P), hd), vec, vec],
        out_specs=[pl.BlockSpec((None, L, P), hd), pl.BlockSpec((None, L, N), grp), pl.BlockSpec((None, L, N), grp),
                   pl.BlockSpec((L, H), tok), vec],
        out_shape=[jax.ShapeDtypeStruct((H, S, P), F32), jax.ShapeDtypeStruct((SSD_GROUPS, S, N), F32),
                   jax.ShapeDtypeStruct((SSD_GROUPS, S, N), F32), jax.ShapeDtypeStruct((S, H), F32),
                   jax.ShapeDtypeStruct((1, H), F32)],
        scratch_shapes=[pltpu.VMEM((H, P, N), F32)],
        compiler_params=_cparams("arbitrary", "arbitrary"), name=name)(xs, Bm, Cm, dt, cs, csT, prev, dy, a_log, d_skip)


def _dt_bwd(ddt, dA, dtp, dt_bias, a_log, name):
    S, H = ddt.shape
    tm = _rows(S, 512)

    def body(g_ref, da_ref, p_ref, b_ref, al_ref, o_ref, db_ref, dal_ref):
        @pl.when(pl.program_id(0) == 0)
        def _():
            db_ref[...] = jnp.zeros_like(db_ref)
            dal_ref[...] = da_ref[...] * (-jnp.exp(al_ref[...]))

        g = g_ref[...] * _sigmoid(p_ref[:, :H] + b_ref[...])
        db_ref[...] += jnp.sum(g, axis=0, keepdims=True)
        o_ref[...] = jnp.zeros_like(o_ref)
        o_ref[:, :H] = g.astype(o_ref.dtype)

    vec = pl.BlockSpec((1, H), lambda i: (0, 0))
    return pl.pallas_call(
        body, grid=(S // tm,),
        in_specs=[pl.BlockSpec((tm, H), lambda i: (i, 0)), vec, pl.BlockSpec((tm, DT_PAD), lambda i: (i, 0)), vec, vec],
        out_specs=[pl.BlockSpec((tm, DT_PAD), lambda i: (i, 0)), vec, vec],
        out_shape=[jax.ShapeDtypeStruct((S, DT_PAD), _MXU), jax.ShapeDtypeStruct((1, H), F32), jax.ShapeDtypeStruct((1, H), F32)],
        compiler_params=_cparams("arbitrary"), name=name)(ddt, dA, dtp, dt_bias, a_log)


def _ssd_gate_fwd(y, act, z, dskip, w, name):
    S, D = y.shape
    tm = _rows(S, 256)
    Gw = D // SSD_GROUPS

    def body(y_ref, x_ref, z_ref, k_ref, w_ref, o_ref):
        zv = z_ref[...]
        y2 = (y_ref[...] + x_ref[...] * k_ref[...]) * (zv * _sigmoid(zv))
        for g in range(SSD_GROUPS):
            sl = slice(g * Gw, (g + 1) * Gw)
            v = y2[:, sl]
            r = lax.rsqrt(jnp.mean(v * v, axis=-1, keepdims=True) + EPS)
            o_ref[:, sl] = ((v * r) * w_ref[:, sl]).astype(o_ref.dtype)

    row = pl.BlockSpec((tm, D), lambda i: (i, 0))
    vec = pl.BlockSpec((1, D), lambda i: (0, 0))
    return pl.pallas_call(body, grid=(S // tm,), in_specs=[row, row, row, vec, vec], out_specs=row,
                          out_shape=jax.ShapeDtypeStruct((S, D), _MXU), compiler_params=_cparams("parallel"),
                          name=name)(y, act, z, dskip, w)


def _ssd_gate_bwd(dyn, y, act, z, dskip, w, name):
    S, D = y.shape
    tm = _rows(S, 256)
    Gw = D // SSD_GROUPS

    def body(g_ref, y_ref, x_ref, z_ref, k_ref, w_ref, dy_ref, dz_ref, dk_ref, dw_ref):
        @pl.when(pl.program_id(0) == 0)
        def _():
            dk_ref[...] = jnp.zeros_like(dk_ref)
            dw_ref[...] = jnp.zeros_like(dw_ref)

        zv = z_ref[...]
        xv = x_ref[...]
        s = _sigmoid(zv)
        sz = zv * s
        y1 = y_ref[...] + xv * k_ref[...]
        y2 = y1 * sz
        for g in range(SSD_GROUPS):
            sl = slice(g * Gw, (g + 1) * Gw)
            v = y2[:, sl]
            r = lax.rsqrt(jnp.mean(v * v, axis=-1, keepdims=True) + EPS)
            vn = v * r
            gy = g_ref[:, sl].astype(F32)
            dw_ref[:, sl] += jnp.sum(gy * vn, axis=0, keepdims=True)
            dvn = gy * w_ref[:, sl]
            dy2 = r * (dvn - vn * jnp.mean(dvn * vn, axis=-1, keepdims=True))
            dy1 = dy2 * sz[:, sl]
            dy_ref[:, sl] = dy1
            dz_ref[:, sl] = (dy2 * y1[:, sl] * (s[:, sl] * (1.0 + zv[:, sl] * (1.0 - s[:, sl])))).astype(dz_ref.dtype)
            dk_ref[:, sl] += jnp.sum(dy1 * xv[:, sl], axis=0, keepdims=True)

    row = pl.BlockSpec((tm, D), lambda i: (i, 0))
    vec = pl.BlockSpec((1, D), lambda i: (0, 0))
    return pl.pallas_call(
        body, grid=(S // tm,), in_specs=[row, row, row, row, vec, vec], out_specs=[row, row, vec, vec],
        out_shape=[jax.ShapeDtypeStruct((S, D), F32), jax.ShapeDtypeStruct((S, D), _MXU),
                   jax.ShapeDtypeStruct((1, D), F32), jax.ShapeDtypeStruct((1, D), F32)],
        compiler_params=_cparams("arbitrary"), name=name)(dyn, y, act, z, dskip, w)


def _split_dot(v, u):
    hi = v.astype(_MXU)
    lo = (v - hi.astype(F32)).astype(_MXU)
    dn = (((1,), (0,)), ((), ()))
    return (lax.dot_general(hi, u, dn, preferred_element_type=F32) + lax.dot_general(lo, u, dn, preferred_element_type=F32))


def _sb_tiles(S):
    return _pick(S, 256, 128)


def _sb_block(q, kj, scale, diag, T):
    z = _dot(q, kj, 1, 1) * scale
    t = jnp.log(1.0 + jnp.exp(-jnp.abs(z)))
    l = -jnp.maximum(z, 0.0) - t
    mask = _tri(T, True, False) if diag else None
    if diag:
        l = jnp.where(mask, l, 0.0)
    return z, t, l, mask


def _sb_fwd(q, k, v, w, name):
    H, S, Dh = q.shape
    T = _sb_tiles(S)
    scale = 1.0 / math.sqrt(Dh)

    def body(q_ref, k_ref, v_ref, w_ref, o_ref, y_ref):
        i = pl.program_id(1)
        qv = q_ref[...]
        U = _tri(T, True, False).astype(_MXU)

        def step(j, carry, diag):
            acc, R = carry
            off = pl.multiple_of(j * T, T)
            kj = k_ref[pl.ds(off, T), :]
            vj = v_ref[pl.ds(off, T), :]
            z, t, l, mask = _sb_block(qv, kj, scale, diag, T)
            E = _split_dot(l, U) + R
            A = jnp.exp(jnp.minimum(z, 0.0) - t + E)
            if diag:
                A = jnp.where(mask, A, 0.0)
            return acc + _dot(A, vj, 1, 0), R + jnp.sum(l, axis=1, keepdims=True)

        carry = step(i, (jnp.zeros((T, Dh), F32), jnp.zeros((T, 1), F32)), True)
        acc, _ = lax.fori_loop(0, i, lambda jj, c: step(i - 1 - jj, c, False), carry)
        o_ref[...] = acc
        r = lax.rsqrt(jnp.mean(acc * acc, axis=-1, keepdims=True) + EPS)
        y_ref[...] = ((acc * r) * w_ref[...]).astype(y_ref.dtype)

    blk = pl.BlockSpec((None, T, Dh), lambda h, i: (h, i, 0))
    full = pl.BlockSpec((None, S, Dh), lambda h, i: (h, 0, 0))
    return pl.pallas_call(
        body, grid=(H, S // T), in_specs=[blk, full, full, pl.BlockSpec((None, 1, Dh), lambda h, i: (h, 0, 0))],
        out_specs=[blk, blk], out_shape=[jax.ShapeDtypeStruct((H, S, Dh), F32), jax.ShapeDtypeStruct((H, S, Dh), _MXU)],
        compiler_params=_cparams("parallel", "parallel"), name=name)(q, k, v, w)


def _sb_bwd(q, k, v, o, dy, w, name):
    H, S, Dh = q.shape
    T = _sb_tiles(S)
    scale = 1.0 / math.sqrt(Dh)

    def body(q_ref, k_ref, v_ref, o_ref, dy_ref, w_ref, dq_ref, dk_ref, dv_ref, dw_ref):
        i = pl.program_id(1)

        @pl.when(i == 0)
        def _():
            dk_ref[...] = jnp.zeros_like(dk_ref)
            dv_ref[...] = jnp.zeros_like(dv_ref)
            dw_ref[...] = jnp.zeros_like(dw_ref)

        qv = q_ref[...]
        ov = o_ref[...]
        gy = dy_ref[...]
        r = lax.rsqrt(jnp.mean(ov * ov, axis=-1, keepdims=True) + EPS)
        on = ov * r
        dw_ref[...] += jnp.sum(gy * on, axis=0, keepdims=True)
        don = gy * w_ref[...]
        do = r * (don - on * jnp.mean(don * on, axis=-1, keepdims=True))
        dob = do.astype(_MXU)
        Dt = jnp.sum(dob.astype(F32) * ov, axis=1, keepdims=True)
        U = _tri(T, True, False).astype(_MXU)
        Ui = _tri(T, False, False).astype(_MXU)

        def step(j, carry, diag):
            dq, R, Q = carry
            off = pl.multiple_of(j * T, T)
            kj = k_ref[pl.ds(off, T), :]
            vj = v_ref[pl.ds(off, T), :]
            z, t, l, mask = _sb_block(qv, kj, scale, diag, T)
            E = _split_dot(l, U) + R
            A = jnp.exp(jnp.minimum(z, 0.0) - t + E)
            if diag:
                A = jnp.where(mask, A, 0.0)
            Ab = A.astype(_MXU)
            G = _dot(dob, vj, 1, 1) * Ab.astype(F32)
            P = Dt - (_split_dot(G, Ui) + Q)
            e = jnp.exp(-jnp.abs(z))
            inv = 1.0 / (1.0 + e)
            pos = z >= 0.0
            sig = jnp.where(pos, inv, e * inv)
            nsig = jnp.where(pos, e * inv, inv)
            dz = (G * nsig - sig * P) * scale
            if diag:
                dz = jnp.where(mask, dz, 0.0)
            dzb = dz.astype(_MXU)
            dk_ref[pl.ds(off, T), :] += _dot(dzb, qv, 0, 0)
            dv_ref[pl.ds(off, T), :] += _dot(Ab, dob, 0, 0)
            return (dq + _dot(dzb, kj, 1, 0), R + jnp.sum(l, axis=1, keepdims=True), Q + jnp.sum(G, axis=1, keepdims=True))

        zero = jnp.zeros((T, 1), F32)
        carry = step(i, (jnp.zeros((T, Dh), F32), zero, zero), True)
        dq, _, _ = lax.fori_loop(0, i, lambda jj, c: step(i - 1 - jj, c, False), carry)
        dq_ref[...] = dq

    blk = pl.BlockSpec((None, T, Dh), lambda h, i: (h, i, 0))
    full = pl.BlockSpec((None, S, Dh), lambda h, i: (h, 0, 0))
    wsp = pl.BlockSpec((None, 1, Dh), lambda h, i: (h, 0, 0))
    return pl.pallas_call(
        body, grid=(H, S // T), in_specs=[blk, full, full, blk, blk, wsp], out_specs=[blk, full, full, wsp],
        out_shape=[jax.ShapeDtypeStruct((H, S, Dh), F32)] * 3 + [jax.ShapeDtypeStruct((H, 1, Dh), F32)],
        compiler_params=_cparams("parallel", "arbitrary"), name=name)(q, k, v, o, dy, w)


def _adamw(w, g, m, v, name):
    R, C = w.shape
    tm = _rows(R, 256) if R % 8 == 0 else R
    c1 = 1.0 - ADAM_B1 ** ADAM_STEP
    c2 = 1.0 - ADAM_B2 ** ADAM_STEP

    def body(w_ref, g_ref, m_ref, v_ref, d_ref, nm_ref, nv_ref):
        gv = g_ref[...]
        mn = ADAM_B1 * m_ref[...] + (1.0 - ADAM_B1) * gv
        vn = ADAM_B2 * v_ref[...] + (1.0 - ADAM_B2) * (gv * gv)
        d_ref[...] = -ADAM_LR * ((mn / c1) / (jnp.sqrt(vn / c2) + ADAM_EPS) + ADAM_WD * w_ref[...])
        nm_ref[...] = mn
        nv_ref[...] = vn

    blk = pl.BlockSpec((tm, C), lambda i: (i, 0))
    return pl.pallas_call(body, grid=(R // tm,), in_specs=[blk] * 4, out_specs=[blk] * 3,
                          out_shape=[jax.ShapeDtypeStruct((R, C), F32)] * 3, compiler_params=_cparams("parallel"),
                          name=name)(w, g, m, v)


def _sum_lead(a, name):
    n, R, C = a.shape
    tm = _rows(R, 256)

    def body(a_ref, o_ref):
        s = a_ref[0].astype(F32)
        for p in range(1, n):
            s = s + a_ref[p].astype(F32)
        o_ref[...] = s

    return pl.pallas_call(body, grid=(R // tm,), in_specs=[pl.BlockSpec((n, tm, C), lambda i: (0, i, 0))],
                          out_specs=pl.BlockSpec((tm, C), lambda i: (i, 0)), out_shape=jax.ShapeDtypeStruct((R, C), F32),
                          compiler_params=_cparams("parallel"), name=name)(a)


_GROUP_BITS = {'c': ((0, 0, 1),), 'xy': ((0, 1, 0), (1, 0, 0), (1, 1, 0)),
               'xyc': tuple((k >> 2 & 1, k >> 1 & 1, k & 1) for k in range(1, 8))}


def _exchange(src, *, group, same_src, name):
    flips = _GROUP_BITS[group]
    n = len(flips) + 1
    blk_shape = src.shape if same_src else src.shape[1:]

    def body(src_ref, dst_ref, send_sems, recv_sems, loc_sem):
        x, y, c = lax.axis_index("x"), lax.axis_index("y"), lax.axis_index("c")

        def member(px, py, pc):
            return {'c': pc, 'xy': 2 * px + py, 'xyc': 4 * px + 2 * py + pc}[group]

        me = member(x, y, c)
        mine = src_ref if same_src else src_ref.at[me]
        own = pltpu.make_async_copy(mine, dst_ref.at[me], loc_sem)
        own.start()
        sends, recvs = [], []
        for kk, (fx, fy, fc) in enumerate(flips):
            px, py, pc = (1 - x if fx else x), (1 - y if fy else y), (1 - c if fc else c)
            peer = member(px, py, pc)
            out = pltpu.make_async_remote_copy(
                src_ref=src_ref if same_src else src_ref.at[peer], dst_ref=dst_ref.at[me],
                send_sem=send_sems.at[kk], recv_sem=recv_sems.at[kk],
                device_id=(px, py, pc), device_id_type=pl.DeviceIdType.MESH)
            out.start()
            sends.append(out)
            recvs.append(pltpu.make_async_remote_copy(
                src_ref=mine, dst_ref=dst_ref.at[peer], send_sem=send_sems.at[kk], recv_sem=recv_sems.at[kk],
                device_id=(px, py, pc), device_id_type=pl.DeviceIdType.MESH))
        for cp in recvs:
            cp.wait_recv()
        for cp in sends:
            cp.wait_send()
        own.wait()

    return pl.pallas_call(
        body, in_specs=[pl.BlockSpec(memory_space=pl.ANY)], out_specs=pl.BlockSpec(memory_space=pl.ANY),
        out_shape=jax.ShapeDtypeStruct((n,) + tuple(blk_shape), src.dtype),
        scratch_shapes=[pltpu.SemaphoreType.DMA((n - 1,)), pltpu.SemaphoreType.DMA((n - 1,)), pltpu.SemaphoreType.DMA(())],
        compiler_params=pltpu.CompilerParams(has_side_effects=True), name=name)(src)


def _to_shards(name, full):
    R, C = full.shape
    if name in COL_SPLIT:
        return full.reshape(R, 4, C // 4).transpose(1, 0, 2)
    return full.reshape(4, R // 4, C)


def _from_shards(name, sh):
    n, R, C = sh.shape
    if name in COL_SPLIT:
        return sh.transpose(1, 0, 2).reshape(R, n * C)
    return sh.reshape(n * R, C)


def _pack_rows(parts, width, rows):
    n = parts[0].shape[0]
    flat = jnp.concatenate([p.reshape(n, -1) for p in parts], axis=1)
    return jnp.pad(flat, ((0, 0), (0, rows * width - flat.shape[1]))).reshape(n, rows, width)


def _unpack_rows(buf, shapes):
    n = buf.shape[0]
    flat = buf.reshape(n, -1)
    out, o = [], 0
    for s in shapes:
        sz = math.prod(s)
        out.append(flat[:, o:o + sz].reshape((n,) + tuple(s)))
        o += sz
    return out


def _ceil_to(v, m):
    return -(-v // m) * m


def kernel(x, mem, norm_mix_w, w_in, conv_ssd_w, conv_ssd_b, dt_bias, a_log, d_skip, ssd_norm_w, sb_norm_w, w_out, norm_mem_w, norm_memkv_w, w_mq, w_mk, w_mv, w_mo, norm_ffn_w, w_up, conv_ffn_w, conv_ffn_b, w_down, norm_final_w, loss_target, m_norm_mix_w, m_w_in, m_conv_ssd_w, m_conv_ssd_b, m_dt_bias, m_a_log, m_d_skip, m_ssd_norm_w, m_sb_norm_w, m_w_out, m_norm_mem_w, m_norm_memkv_w, m_w_mq, m_w_mk, m_w_mv, m_w_mo, m_norm_ffn_w, m_w_up, m_conv_ffn_w, m_conv_ffn_b, m_w_down, m_norm_final_w, v_norm_mix_w, v_w_in, v_conv_ssd_w, v_conv_ssd_b, v_dt_bias, v_a_log, v_d_skip, v_ssd_norm_w, v_sb_norm_w, v_w_out, v_norm_mem_w, v_norm_memkv_w, v_w_mq, v_w_mk, v_w_mv, v_w_mo, v_norm_ffn_w, v_w_up, v_conv_ffn_w, v_conv_ffn_b, v_w_down, v_norm_final_w):
    W = dict(norm_mix_w=norm_mix_w, w_in=w_in, conv_ssd_w=conv_ssd_w, conv_ssd_b=conv_ssd_b, dt_bias=dt_bias, a_log=a_log,
             d_skip=d_skip, ssd_norm_w=ssd_norm_w, sb_norm_w=sb_norm_w, w_out=w_out, norm_mem_w=norm_mem_w,
             norm_memkv_w=norm_memkv_w, w_mq=w_mq, w_mk=w_mk, w_mv=w_mv, w_mo=w_mo, norm_ffn_w=norm_ffn_w, w_up=w_up,
             conv_ffn_w=conv_ffn_w, conv_ffn_b=conv_ffn_b, w_down=w_down, norm_final_w=norm_final_w)
    Mo = dict(norm_mix_w=m_norm_mix_w, w_in=m_w_in, conv_ssd_w=m_conv_ssd_w, conv_ssd_b=m_conv_ssd_b, dt_bias=m_dt_bias,
              a_log=m_a_log, d_skip=m_d_skip, ssd_norm_w=m_ssd_norm_w, sb_norm_w=m_sb_norm_w, w_out=m_w_out,
              norm_mem_w=m_norm_mem_w, norm_memkv_w=m_norm_memkv_w, w_mq=m_w_mq, w_mk=m_w_mk, w_mv=m_w_mv, w_mo=m_w_mo,
              norm_ffn_w=m_norm_ffn_w, w_up=m_w_up, conv_ffn_w=m_conv_ffn_w, conv_ffn_b=m_conv_ffn_b, w_down=m_w_down,
              norm_final_w=m_norm_final_w)
    Vo = dict(norm_mix_w=v_norm_mix_w, w_in=v_w_in, conv_ssd_w=v_conv_ssd_w, conv_ssd_b=v_conv_ssd_b, dt_bias=v_dt_bias,
              a_log=v_a_log, d_skip=v_d_skip, ssd_norm_w=v_ssd_norm_w, sb_norm_w=v_sb_norm_w, w_out=v_w_out,
              norm_mem_w=v_norm_mem_w, norm_memkv_w=v_norm_memkv_w, w_mq=v_w_mq, w_mk=v_w_mk, w_mv=v_w_mv, w_mo=v_w_mo,
              norm_ffn_w=v_norm_ffn_w, w_up=v_w_up, conv_ffn_w=v_conv_ffn_w, conv_ffn_b=v_conv_ffn_b, w_down=v_w_down,
              norm_final_w=v_norm_final_w)
    shapes = {n: W[n].shape for n in WEIGHTS}
    sh2 = {n: (1, a.shape[-1]) if a.ndim < 3 else a.shape[-2:] for n, a in W.items()}
    w2 = {n: W[n].reshape(sh2[n]) for n in WEIGHTS}
    x2d = x[0]
    S, D = x2d.shape
    H, P, N = SSD_HEADS, SSD_HEAD_DIM, SSD_STATE

    big_rows = _ceil_to(sum(math.prod(sh2[n]) for n in BIG) // 1024, 16)
    wpack = _pack_rows([w2[n].astype(_MXU)[None] for n in BIG], 1024, big_rows)[0]
    wall = _exchange(wpack, group='xy', same_src=True, name="gather_weights")
    cv_rows = _ceil_to(-(-sum(math.prod(sh2[n]) for n in CONVW) // 128), 8)
    cpack = _pack_rows([w2[n][None] for n in CONVW], 128, cv_rows)[0]
    call = _exchange(cpack, group='xy', same_src=True, name="gather_conv_weights")
    full = {n: _from_shards(n, a) for n, a in zip(BIG, _unpack_rows(wall, [sh2[n] for n in BIG]))}
    full.update({n: _from_shards(n, a) for n, a in zip(CONVW, _unpack_rows(call, [sh2[n] for n in CONVW]))})

    o1 = SSD_INNER
    o2 = o1 + SSD_XBC
    o3 = o2 + SSD_HEADS
    Wi = full['w_in']
    W_z, W_xbc, W_qkv = Wi[:, :o1], Wi[:, o1:o2], Wi[:, o3:]
    W_dt = jnp.pad(Wi[:, o2:o3], ((0, 0), (0, DT_PAD - SSD_HEADS)))
    W_in_r = jnp.concatenate([W_z, W_xbc, W_qkv, W_dt], axis=1)
    dskip_rep = jnp.repeat(w2['d_skip'], P, axis=1)
    sbw = w2['sb_norm_w'].reshape(SB_HEADS, 1, SB_HEAD_DIM)

    def heads(a, nh):
        return a.reshape(S, nh, a.shape[1] // nh).transpose(1, 0, 2)

    def unheads(a):
        return a.transpose(1, 0, 2).reshape(S, a.shape[0] * a.shape[2])

    h1 = _rms_fwd(x2d, w2['norm_mix_w'], "norm_mix")
    z = _mm(h1, W_z, name="proj_z")
    xbc = _mm(h1, W_xbc, name="proj_xbc")
    dtp = _mm(h1, W_dt, name="proj_dt")
    qkv = _mm(h1, W_qkv, out_dtype=_MXU, name="proj_qkv")
    pre = _dwconv_fwd(xbc, full['conv_ssd_w'], w2['conv_ssd_b'], "ssd_conv")
    act = _silu_fwd(pre, "ssd_conv_silu")
    dt, cs = _ssd_prep(dtp, w2['dt_bias'], w2['a_log'], "ssd_prep")
    csT = cs.T
    xs_h = heads(act[:, :o1], H)
    Bm = heads(act[:, o1:o1 + SSD_GROUPS * N], SSD_GROUPS)
    Cm = heads(act[:, o1 + SSD_GROUPS * N:], SSD_GROUPS)
    y_h, prev = _ssd_fwd(xs_h, Bm, Cm, dt, cs, csT, "ssd_scan")
    y_scan = unheads(y_h)
    xs = act[:, :o1]
    y_ssd = _ssd_gate_fwd(y_scan, xs, z, dskip_rep, w2['ssd_norm_w'], "ssd_gate")
    qh, kh, vh = [heads(qkv[:, i * SB_WIDTH:(i + 1) * SB_WIDTH], SB_HEADS) for i in range(3)]
    o_sb, y_sb_h = _sb_fwd(qh, kh, vh, sbw, "sb_attn")
    ycat = jnp.concatenate([y_ssd, unheads(y_sb_h)], axis=1)
    x_2 = _mm(ycat, full['w_out'], res=x2d, name="out_proj")
    h2 = _rms_fwd(x_2, w2['norm_mem_w'], "norm_mem")
    qm = _mm(h2, full['w_mq'], out_dtype=_MXU, name="mem_q")
    mn = _rms_fwd(mem[0], w2['norm_memkv_w'], "norm_memkv")
    km = _mm(mn, full['w_mk'], out_dtype=_MXU, name="mem_k")
    vm = _mm(mn, full['w_mv'], out_dtype=_MXU, name="mem_v")
    om = _xattn_fwd(qm, km, vm, "mem_attn")
    x_3 = _mm(om, full['w_mo'], res=x_2, name="mem_o")
    h3 = _rms_fwd(x_3, w2['norm_ffn_w'], "norm_ffn")
    up = _mm(h3, full['w_up'], name="ffn_up")
    u = _dwconv_fwd(up, full['conv_ffn_w'], w2['conv_ffn_b'], "ffn_conv")
    a_ffn = _glu_fwd(u, "ffn_glu")
    x_4 = _mm(a_ffn, full['w_down'], res=x_3, name="ffn_down")
    dx4, dx4b, g_final, loss_blk = _loss_bwd(x_4, loss_target[0], w2['norm_final_w'], "loss_head")

    G = {'norm_final_w': g_final}
    dact = _mm(dx4b, full['w_down'], tb=True, name="d_ffn_act")
    G['w_down'] = _mm(a_ffn, dx4b, ta=True, name="g_w_down")
    du = _glu_bwd(u, dact, "d_ffn_glu")
    G['conv_ffn_w'], G['conv_ffn_b'] = _conv_bwd_w(up, du, full['conv_ffn_w'].shape[0], "g_ffn_conv")
    dup = _dwconv_bwd_x(du, full['conv_ffn_w'], "d_ffn_conv")
    dh3 = _mm(dup, full['w_up'], tb=True, name="d_h3")
    G['w_up'] = _mm(h3, dup, ta=True, name="g_w_up")
    dx3, dx3b, G['norm_ffn_w'] = _rms_bwd(dh3, x_3, w2['norm_ffn_w'], dx4, "d_norm_ffn")
    dom = _mm(dx3b, full['w_mo'], tb=True, out_dtype=_MXU, name="d_mem_o")
    G['w_mo'] = _mm(om, dx3b, ta=True, name="g_w_mo")
    dqm, dkm, dvm = _xattn_bwd(qm, km, vm, dom, "d_mem_attn")
    G['w_mq'] = _mm(h2, dqm, ta=True, name="g_w_mq")
    dh2 = _mm(dqm, full['w_mq'], tb=True, name="d_h2")
    dx2, dx2b, G['norm_mem_w'] = _rms_bwd(dh2, x_2, w2['norm_mem_w'], dx3, "d_norm_mem")
    G['w_mk'] = _mm(mn, dkm, ta=True, name="g_w_mk")
    G['w_mv'] = _mm(mn, dvm, ta=True, name="g_w_mv")
    dmn = _mm(dvm, full['w_mv'], tb=True, res=_mm(dkm, full['w_mk'], tb=True, name="d_mn_k"), name="d_mn_v")
    _, _, G['norm_memkv_w'] = _rms_bwd(dmn, mem[0], w2['norm_memkv_w'], None, "d_norm_memkv")
    dycat = _mm(dx2b, full['w_out'], tb=True, name="d_ycat")
    G['w_out'] = _mm(ycat, dx2b, ta=True, name="g_w_out")
    dy1, dz, g_dskip_lane, G['ssd_norm_w'] = _ssd_gate_bwd(dycat[:, :o1], y_scan, xs, z, dskip_rep, w2['ssd_norm_w'], "d_ssd_gate")
    dxs_h, dB, dC, ddt, dA = _ssd_bwd(xs_h, Bm, Cm, dt, cs, csT, prev, heads(dy1, H), w2['a_log'], w2['d_skip'], "d_ssd_scan")
    G['d_skip'] = jnp.sum(g_dskip_lane.reshape(H, P), axis=1)[None, :]
    dact_xbc = jnp.concatenate([unheads(dxs_h), unheads(dB), unheads(dC)], axis=1)
    dpre = _silu_bwd(pre, dact_xbc, "d_ssd_conv_silu")
    G['conv_ssd_w'], G['conv_ssd_b'] = _conv_bwd_w(xbc, dpre, full['conv_ssd_w'].shape[0], "g_ssd_conv")
    dxbc = _dwconv_bwd_x(dpre, full['conv_ssd_w'], "d_ssd_conv")
    ddtp, G['dt_bias'], G['a_log'] = _dt_bwd(ddt, dA, dtp, w2['dt_bias'], w2['a_log'], "d_dt")
    dq, dk, dv, g_sbw = _sb_bwd(qh, kh, vh, o_sb, heads(dycat[:, o1:], SB_HEADS), sbw, "d_sb_attn")
    G['sb_norm_w'] = g_sbw.reshape(1, SB_WIDTH)
    dproj = jnp.concatenate([dz, dxbc, unheads(dq).astype(_MXU), unheads(dk).astype(_MXU), unheads(dv).astype(_MXU), ddtp], axis=1)
    dh1 = _mm(dproj, W_in_r, tb=True, name="d_h1")
    g_in_r = _mm(h1, dproj, ta=True, name="g_w_in")
    nq = 3 * SB_WIDTH
    G['w_in'] = jnp.concatenate([g_in_r[:, :o2], g_in_r[:, o2 + nq:o2 + nq + SSD_HEADS], g_in_r[:, o2:o2 + nq]], axis=1)
    grad_x, _, G['norm_mix_w'] = _rms_bwd(dh1, x2d, w2['norm_mix_w'], dx2, "d_norm_mix")

    split = BIG + CONVW
    g_rows = _ceil_to(-(-sum(math.prod(sh2[n]) for n in split) // 1024), 512)
    gpack = _pack_rows([_to_shards(n, G[n]) for n in split], 1024, g_rows)
    half = g_rows // 2
    to_pair = gpack.reshape(4, 2, half, 1024).transpose(1, 0, 2, 3).reshape(2, 4 * half, 1024)
    pair = _sum_lead(_exchange(to_pair, group='c', same_src=False, name="reduce_pair"), "reduce_pair_sum")
    chips = _sum_lead(_exchange(pair.reshape(4, half, 1024), group='xy', same_src=False, name="reduce_chips"), "reduce_chips_sum")
    red = _exchange(chips, group='c', same_src=True, name="share_pair").reshape(1, g_rows, 1024)
    gsh = dict(zip(split, [a[0] for a in _unpack_rows(red, [sh2[n] for n in split])]))

    small_rows = _ceil_to(-(-(sum(math.prod(sh2[n]) for n in SMALL) + 1) // 128), 8)
    spack = _pack_rows([G[n].reshape(1, -1) for n in SMALL] + [loss_blk[:1, :1]], 128, small_rows)[0]
    ssum = _sum_lead(_exchange(spack, group='xyc', same_src=True, name="gather_small"), "small_sum")
    parts = _unpack_rows(ssum[None], [sh2[n] for n in SMALL] + [(1, 1)])
    gsh.update(zip(SMALL, [a[0] for a in parts[:-1]]))
    loss = parts[-1].reshape(())

    delta, new_m, new_v = {}, {}, {}
    for n in BIG:
        delta[n], new_m[n], new_v[n] = _adamw(w2[n], gsh[n], Mo[n].reshape(sh2[n]), Vo[n].reshape(sh2[n]), "adamw_" + n)
    for grp, width, tag in ((CONVW, 128, "adamw_conv"), (SMALL, 128, "adamw_small")):
        rows = _ceil_to(-(-sum(math.prod(sh2[n]) for n in grp) // width), 8)
        packed = [_pack_rows([src[n].reshape(1, -1) for n in grp], width, rows)[0]
                  for src in (w2, gsh, {n: Mo[n] for n in grp}, {n: Vo[n] for n in grp})]
        outs = _adamw(*packed, tag)
        for dst, o in zip((delta, new_m, new_v), outs):
            dst.update(zip(grp, [a[0] for a in _unpack_rows(o[None], [sh2[n] for n in grp])]))

    def shaped(d):
        return [d[n].reshape(shapes[n]) for n in WEIGHTS]

    return (loss, grad_x[None], *shaped(gsh), *shaped(delta), *shaped(new_m), *shaped(new_v))
```

```python
import math

import jax
import jax.numpy as jnp
from jax import lax
from jax.experimental import pallas as pl
from jax.experimental.pallas import tpu as pltpu

F32 = jnp.float32
_MXU = jnp.bfloat16
EPS = 1e-6
_VMEM_LIMIT = 48 * 1024 * 1024
_HI = lax.Precision.HIGHEST

SSD_HEADS = 16
SSD_HEAD_DIM = 64
SSD_GROUPS = 2
SSD_STATE = 128
SSD_CHUNK = 128
SSD_INNER = SSD_HEADS * SSD_HEAD_DIM
SSD_XBC = SSD_INNER + 2 * SSD_GROUPS * SSD_STATE
SB_HEADS = 16
SB_HEAD_DIM = 64
SB_WIDTH = SB_HEADS * SB_HEAD_DIM
MEM_HEADS = 4
DT_PAD = 128

ADAM_LR = 0.001
ADAM_B1 = 0.9
ADAM_B2 = 0.999
ADAM_EPS = 1e-08
ADAM_WD = 0.01
ADAM_STEP = 10

WEIGHTS = ['norm_mix_w', 'w_in', 'conv_ssd_w', 'conv_ssd_b', 'dt_bias', 'a_log', 'd_skip', 'ssd_norm_w',
           'sb_norm_w', 'w_out', 'norm_mem_w', 'norm_memkv_w', 'w_mq', 'w_mk', 'w_mv', 'w_mo', 'norm_ffn_w',
           'w_up', 'conv_ffn_w', 'conv_ffn_b', 'w_down', 'norm_final_w']
BIG = ['w_in', 'w_out', 'w_mq', 'w_mk', 'w_mv', 'w_mo', 'w_up', 'w_down']
COL_SPLIT = ('w_in', 'w_up', 'conv_ssd_w', 'conv_ffn_w')
ROW_SPLIT = ['w_out', 'w_mq', 'w_mk', 'w_mv', 'w_mo', 'w_down']
CONVW = ['conv_ssd_w', 'conv_ffn_w']
SMALL = ['norm_mix_w', 'conv_ssd_b', 'dt_bias', 'a_log', 'd_skip', 'ssd_norm_w', 'sb_norm_w', 'norm_mem_w',
         'norm_memkv_w', 'norm_ffn_w', 'conv_ffn_b', 'norm_final_w']


def _cparams(*sem):
    return pltpu.CompilerParams(dimension_semantics=sem if sem else None, vmem_limit_bytes=_VMEM_LIMIT)


def _pick(n, cap, mult=128):
    best = None
    for d in range(mult, min(n, cap) + 1, mult):
        if n % d == 0:
            best = d
    return n if best is None else best


def _dot(a, b, ca, cb):
    return lax.dot_general(a.astype(_MXU), b.astype(_MXU), (((ca,), (cb,)), ((), ())), preferred_element_type=F32)


def _sigmoid(v):
    return 1.0 / (1.0 + jnp.exp(-v))


def _log1p(u):
    w = 1.0 + u
    return jnp.where(w == 1.0, u, jnp.log(w) * (u / (w - 1.0)))


def _mm(a, b, *, ta=False, tb=False, res=None, out_dtype=F32, name):
    if ta:
        K, M = a.shape
    else:
        M, K = a.shape
    if tb:
        N, K2 = b.shape
    else:
        K2, N = b.shape
    assert K == K2, (a.shape, b.shape)
    tm = _pick(M, 1408, 128 if ta else 16)
    tn = _pick(N, 1536)
    tk = _pick(K, 1536)
    nk = K // tk
    a_spec = pl.BlockSpec((tk, tm), lambda i, j, k: (k, i)) if ta else pl.BlockSpec((tm, tk), lambda i, j, k: (i, k))
    b_spec = pl.BlockSpec((tn, tk), lambda i, j, k: (j, k)) if tb else pl.BlockSpec((tk, tn), lambda i, j, k: (k, j))
    o_spec = pl.BlockSpec((tm, tn), lambda i, j, k: (i, j))
    ca, cb = (0 if ta else 1), (1 if tb else 0)

    def body(*refs):
        if res is None:
            a_ref, b_ref, o_ref, acc_ref = refs
            r_ref = None
        else:
            a_ref, b_ref, r_ref, o_ref, acc_ref = refs
        k = pl.program_id(2)

        @pl.when(k == 0)
        def _():
            acc_ref[...] = jnp.zeros_like(acc_ref)

        acc_ref[...] += _dot(a_ref[...], b_ref[...], ca, cb)

        @pl.when(k == nk - 1)
        def _():
            r = acc_ref[...]
            if r_ref is not None:
                r = r + r_ref[...].astype(F32)
            o_ref[...] = r.astype(o_ref.dtype)

    ins = [a, b] + ([] if res is None else [res])
    in_specs = [a_spec, b_spec] + ([] if res is None else [o_spec])
    return pl.pallas_call(
        body, grid=(M // tm, N // tn, nk), in_specs=in_specs, out_specs=o_spec,
        out_shape=jax.ShapeDtypeStruct((M, N), out_dtype), scratch_shapes=[pltpu.VMEM((tm, tn), F32)],
        compiler_params=_cparams("parallel", "parallel", "arbitrary"), name=name)(*ins)


def _rows(S, cap):
    return _pick(S, cap, 8)


def _rms_fwd(x, w, name):
    S, D = x.shape
    tm = _rows(S, 512)

    def body(x_ref, w_ref, o_ref):
        xv = x_ref[...]
        r = lax.rsqrt(jnp.mean(xv * xv, axis=-1, keepdims=True) + EPS)
        o_ref[...] = ((xv * r) * w_ref[...]).astype(o_ref.dtype)

    row = pl.BlockSpec((tm, D), lambda i: (i, 0))
    return pl.pallas_call(body, grid=(S // tm,), in_specs=[row, pl.BlockSpec((1, D), lambda i: (0, 0))], out_specs=row,
                          out_shape=jax.ShapeDtypeStruct((S, D), _MXU), compiler_params=_cparams("parallel"), name=name)(x, w)


def _rms_bwd(dh, x, w, dres, name):
    S, D = x.shape
    tm = _rows(S, 256)

    def body(*refs):
        if dres is None:
            dh_ref, x_ref, w_ref, dx_ref, dxb_ref, dw_ref = refs
            dres_ref = None
        else:
            dh_ref, x_ref, w_ref, dres_ref, dx_ref, dxb_ref, dw_ref = refs
        xv = x_ref[...]
        r = lax.rsqrt(jnp.mean(xv * xv, axis=-1, keepdims=True) + EPS)
        xn = xv * r
        dy = dh_ref[...].astype(F32)

        @pl.when(pl.program_id(0) == 0)
        def _():
            dw_ref[...] = jnp.zeros_like(dw_ref)

        dw_ref[...] += jnp.sum(dy * xn, axis=0, keepdims=True)
        dxn = dy * w_ref[...]
        dx = r * (dxn - xn * jnp.mean(dxn * xn, axis=-1, keepdims=True))
        if dres_ref is not None:
            dx = dx + dres_ref[...]
        dx_ref[...] = dx
        dxb_ref[...] = dx.astype(dxb_ref.dtype)

    row = pl.BlockSpec((tm, D), lambda i: (i, 0))
    vec = pl.BlockSpec((1, D), lambda i: (0, 0))
    ins = [dh, x, w] + ([] if dres is None else [dres])
    in_specs = [row, row, vec] + ([] if dres is None else [row])
    return pl.pallas_call(
        body, grid=(S // tm,), in_specs=in_specs, out_specs=[row, row, vec],
        out_shape=[jax.ShapeDtypeStruct((S, D), F32), jax.ShapeDtypeStruct((S, D), _MXU), jax.ShapeDtypeStruct((1, D), F32)],
        compiler_params=_cparams("arbitrary"), name=name)(*ins)


def _loss_bwd(x, tgt, w, name):
    S, D = x.shape
    tm = _rows(S, 256)

    def body(x_ref, t_ref, w_ref, dx_ref, dxb_ref, dw_ref, loss_ref):
        xv = x_ref[...]
        r = lax.rsqrt(jnp.mean(xv * xv, axis=-1, keepdims=True) + EPS)
        xn = xv * r
        e = xn * w_ref[...] - t_ref[...]

        @pl.when(pl.program_id(0) == 0)
        def _():
            dw_ref[...] = jnp.zeros_like(dw_ref)
            loss_ref[...] = jnp.zeros_like(loss_ref)

        tok = jnp.mean(e * e, axis=-1, keepdims=True)
        loss_ref[...] += jnp.broadcast_to(0.5 * jnp.sum(tok, axis=0, keepdims=True), loss_ref.shape)
        dy = e * (1.0 / D)
        dw_ref[...] += jnp.sum(dy * xn, axis=0, keepdims=True)
        dxn = dy * w_ref[...]
        dx = r * (dxn - xn * jnp.mean(dxn * xn, axis=-1, keepdims=True))
        dx_ref[...] = dx
        dxb_ref[...] = dx.astype(dxb_ref.dtype)

    row = pl.BlockSpec((tm, D), lambda i: (i, 0))
    vec = pl.BlockSpec((1, D), lambda i: (0, 0))
    return pl.pallas_call(
        body, grid=(S // tm,), in_specs=[row, row, vec],
        out_specs=[row, row, vec, pl.BlockSpec((8, 128), lambda i: (0, 0))],
        out_shape=[jax.ShapeDtypeStruct((S, D), F32), jax.ShapeDtypeStruct((S, D), _MXU),
                   jax.ShapeDtypeStruct((1, D), F32), jax.ShapeDtypeStruct((8, 128), F32)],
        compiler_params=_cparams("arbitrary"), name=name)(x, tgt, w)


def _conv_tiles(S, C):
    return _rows(S, 256), _pick(C, 1536)


def _dwconv_fwd(x, w, b, name):
    S, C = x.shape
    K = w.shape[0]
    tm, tc = _conv_tiles(S, C)

    def body(x_ref, p_ref, w_ref, b_ref, o_ref):
        cur = x_ref[...]
        prev = jnp.where(pl.program_id(0) > 0, p_ref[...], 0.0)
        xx = jnp.concatenate([prev, cur], axis=0)
        acc = cur * w_ref[K - 1:K, :] + b_ref[...]
        for d in range(1, K):
            acc = acc + pltpu.roll(xx, d, 0)[8:, :] * w_ref[K - 1 - d:K - d, :]
        o_ref[...] = acc

    return pl.pallas_call(
        body, grid=(S // tm, C // tc),
        in_specs=[pl.BlockSpec((tm, tc), lambda i, j: (i, j)),
                  pl.BlockSpec((8, tc), lambda i, j: (jnp.maximum(i * (tm // 8) - 1, 0), j)),
                  pl.BlockSpec((K, tc), lambda i, j: (0, j)), pl.BlockSpec((1, tc), lambda i, j: (0, j))],
        out_specs=pl.BlockSpec((tm, tc), lambda i, j: (i, j)), out_shape=jax.ShapeDtypeStruct((S, C), F32),
        compiler_params=_cparams("parallel", "parallel"), name=name)(x, x, w, b)


def _conv_bwd_w(x, dy, K, name):
    S, C = x.shape
    tm, tc = _conv_tiles(S, C)

    def body(x_ref, p_ref, dy_ref, dw_ref, db_ref):
        i = pl.program_id(1)

        @pl.when(i == 0)
        def _():
            dw_ref[...] = jnp.zeros_like(dw_ref)
            db_ref[...] = jnp.zeros_like(db_ref)

        cur = x_ref[...]
        prev = jnp.where(i > 0, p_ref[...], 0.0)
        xx = jnp.concatenate([prev, cur], axis=0)
        g = dy_ref[...].astype(F32)
        db_ref[...] += jnp.sum(g, axis=0, keepdims=True)
        dw_ref[K - 1:K, :] += jnp.sum(g * cur, axis=0, keepdims=True)
        for d in range(1, K):
            dw_ref[K - 1 - d:K - d, :] += jnp.sum(g * pltpu.roll(xx, d, 0)[8:, :], axis=0, keepdims=True)

    return pl.pallas_call(
        body, grid=(C // tc, S // tm),
        in_specs=[pl.BlockSpec((tm, tc), lambda j, i: (i, j)),
                  pl.BlockSpec((8, tc), lambda j, i: (jnp.maximum(i * (tm // 8) - 1, 0), j)),
                  pl.BlockSpec((tm, tc), lambda j, i: (i, j))],
        out_specs=[pl.BlockSpec((K, tc), lambda j, i: (0, j)), pl.BlockSpec((1, tc), lambda j, i: (0, j))],
        out_shape=[jax.ShapeDtypeStruct((K, C), F32), jax.ShapeDtypeStruct((1, C), F32)],
        compiler_params=_cparams("parallel", "arbitrary"), name=name)(x, x, dy)


def _dwconv_bwd_x(dy, w, name):
    S, C = dy.shape
    K = w.shape[0]
    tm, tc = _conv_tiles(S, C)
    last = S // tm - 1

    def body(g_ref, n_ref, w_ref, o_ref):
        cur = g_ref[...]
        nxt = jnp.where(pl.program_id(0) < last, n_ref[...], 0.0)
        xx = jnp.concatenate([cur, nxt], axis=0)
        acc = cur * w_ref[K - 1:K, :]
        for d in range(1, K):
            acc = acc + pltpu.roll(xx, tm + 8 - d, 0)[:tm, :] * w_ref[K - 1 - d:K - d, :]
        o_ref[...] = acc.astype(o_ref.dtype)

    return pl.pallas_call(
        body, grid=(S // tm, C // tc),
        in_specs=[pl.BlockSpec((tm, tc), lambda i, j: (i, j)),
                  pl.BlockSpec((8, tc), lambda i, j: (jnp.minimum((i + 1) * (tm // 8), S // 8 - 1), j)),
                  pl.BlockSpec((K, tc), lambda i, j: (0, j))],
        out_specs=pl.BlockSpec((tm, tc), lambda i, j: (i, j)), out_shape=jax.ShapeDtypeStruct((S, C), _MXU),
        compiler_params=_cparams("parallel", "parallel"), name=name)(dy, dy, w)


def _silu_fwd(pre, name):
    S, C = pre.shape
    tm, tc = _conv_tiles(S, C)

    def body(p_ref, o_ref):
        p = p_ref[...]
        o_ref[...] = p * _sigmoid(p)

    blk = pl.BlockSpec((tm, tc), lambda i, j: (i, j))
    return pl.pallas_call(body, grid=(S // tm, C // tc), in_specs=[blk], out_specs=blk,
                          out_shape=jax.ShapeDtypeStruct((S, C), F32), compiler_params=_cparams("parallel", "parallel"),
                          name=name)(pre)


def _silu_bwd(pre, dact, name):
    S, C = pre.shape
    tm, tc = _conv_tiles(S, C)

    def body(p_ref, g_ref, o_ref):
        p = p_ref[...]
        s = _sigmoid(p)
        o_ref[...] = g_ref[...] * (s * (1.0 + p * (1.0 - s)))

    blk = pl.BlockSpec((tm, tc), lambda i, j: (i, j))
    return pl.pallas_call(body, grid=(S // tm, C // tc), in_specs=[blk, blk], out_specs=blk,
                          out_shape=jax.ShapeDtypeStruct((S, C), F32), compiler_params=_cparams("parallel", "parallel"),
                          name=name)(pre, dact)


def _glu_fwd(u, name):
    S, C = u.shape
    Fh = C // 2
    tm = _rows(S, 128)

    def body(u_ref, o_ref):
        g = u_ref[:, :Fh]
        o_ref[...] = (g * _sigmoid(g) * u_ref[:, Fh:]).astype(o_ref.dtype)

    return pl.pallas_call(body, grid=(S // tm,), in_specs=[pl.BlockSpec((tm, C), lambda i: (i, 0))],
                          out_specs=pl.BlockSpec((tm, Fh), lambda i: (i, 0)), out_shape=jax.ShapeDtypeStruct((S, Fh), _MXU),
                          compiler_params=_cparams("parallel"), name=name)(u)


def _glu_bwd(u, dact, name):
    S, C = u.shape
    Fh = C // 2
    tm = _rows(S, 128)

    def body(u_ref, g_ref, o_ref):
        g = u_ref[:, :Fh]
        val = u_ref[:, Fh:]
        da = g_ref[...].astype(F32)
        s = _sigmoid(g)
        o_ref[:, :Fh] = da * val * (s * (1.0 + g * (1.0 - s)))
        o_ref[:, Fh:] = da * (g * s)

    return pl.pallas_call(body, grid=(S // tm,),
                          in_specs=[pl.BlockSpec((tm, C), lambda i: (i, 0)), pl.BlockSpec((tm, Fh), lambda i: (i, 0))],
                          out_specs=pl.BlockSpec((tm, C), lambda i: (i, 0)), out_shape=jax.ShapeDtypeStruct((S, C), F32),
                          compiler_params=_cparams("parallel"), name=name)(u, dact)


def _xattn_fwd(q, k, v, name):
    S, D = q.shape
    M = k.shape[0]
    hd = D // MEM_HEADS
    tm = _rows(S, 512)
    scale = 1.0 / math.sqrt(hd)

    def body(q_ref, k_ref, v_ref, o_ref):
        for h in range(MEM_HEADS):
            sl = slice(h * hd, (h + 1) * hd)
            s = _dot(q_ref[:, sl], k_ref[:, sl], 1, 1) * scale
            p = jnp.exp(s - jnp.max(s, axis=-1, keepdims=True))
            p = p / jnp.sum(p, axis=-1, keepdims=True)
            o_ref[:, sl] = _dot(p, v_ref[:, sl], 1, 0).astype(o_ref.dtype)

    kv = pl.BlockSpec((M, D), lambda i: (0, 0))
    row = pl.BlockSpec((tm, D), lambda i: (i, 0))
    return pl.pallas_call(body, grid=(S // tm,), in_specs=[row, kv, kv], out_specs=row,
                          out_shape=jax.ShapeDtypeStruct((S, D), _MXU), compiler_params=_cparams("parallel"), name=name)(q, k, v)


def _xattn_bwd(q, k, v, do, name):
    S, D = q.shape
    M = k.shape[0]
    hd = D // MEM_HEADS
    tm = _rows(S, 512)
    scale = 1.0 / math.sqrt(hd)

    def body(q_ref, k_ref, v_ref, do_ref, dq_ref, dk_ref, dv_ref):
        @pl.when(pl.program_id(0) == 0)
        def _():
            dk_ref[...] = jnp.zeros_like(dk_ref)
            dv_ref[...] = jnp.zeros_like(dv_ref)

        for h in range(MEM_HEADS):
            sl = slice(h * hd, (h + 1) * hd)
            qh, kh, vh, doh = q_ref[:, sl], k_ref[:, sl], v_ref[:, sl], do_ref[:, sl]
            s = _dot(qh, kh, 1, 1) * scale
            p = jnp.exp(s - jnp.max(s, axis=-1, keepdims=True))
            p = p / jnp.sum(p, axis=-1, keepdims=True)
            dp = _dot(doh, vh, 1, 1)
            dv_ref[:, sl] += _dot(p, doh, 0, 0)
            ds = (p * (dp - jnp.sum(dp * p, axis=-1, keepdims=True))) * scale
            dq_ref[:, sl] = _dot(ds, kh, 1, 0).astype(dq_ref.dtype)
            dk_ref[:, sl] += _dot(ds, qh, 0, 0)

    kv = pl.BlockSpec((M, D), lambda i: (0, 0))
    row = pl.BlockSpec((tm, D), lambda i: (i, 0))
    return pl.pallas_call(
        body, grid=(S // tm,), in_specs=[row, kv, kv, row], out_specs=[row, kv, kv],
        out_shape=[jax.ShapeDtypeStruct((S, D), _MXU), jax.ShapeDtypeStruct((M, D), F32), jax.ShapeDtypeStruct((M, D), F32)],
        compiler_params=_cparams("arbitrary"), name=name)(q, k, v, do)


def _tri(n, strict, upper):
    r = lax.broadcasted_iota(jnp.int32, (n, n), 0)
    c = lax.broadcasted_iota(jnp.int32, (n, n), 1)
    if upper:
        return (c > r) if strict else (c >= r)
    return (r > c) if strict else (r >= c)


def _ssd_prep(dtp, dt_bias, a_log, name):
    S = dtp.shape[0]
    L, H = SSD_CHUNK, SSD_HEADS

    def body(p_ref, b_ref, al_ref, dt_ref, cs_ref):
        v = p_ref[:, :H] + b_ref[...]
        dt = jnp.maximum(v, 0.0) + _log1p(jnp.exp(-jnp.abs(v)))
        dt_ref[...] = dt
        a = dt * (-jnp.exp(al_ref[...]))
        cs_ref[...] = jnp.dot(_tri(L, False, False).astype(F32), a, precision=_HI, preferred_element_type=F32)

    blk = pl.BlockSpec((L, H), lambda c: (c, 0))
    vec = pl.BlockSpec((1, H), lambda c: (0, 0))
    return pl.pallas_call(body, grid=(S // L,), in_specs=[pl.BlockSpec((L, DT_PAD), lambda c: (c, 0)), vec, vec],
                          out_specs=[blk, blk], out_shape=[jax.ShapeDtypeStruct((S, H), F32)] * 2,
                          compiler_params=_cparams("parallel"), name=name)(dtp, dt_bias, a_log)


def _head_col(blk_ref, h):
    sel = lax.broadcasted_iota(jnp.int32, (1, SSD_HEADS), 1) == h
    return jnp.sum(jnp.where(sel, blk_ref[...], 0.0), axis=1, keepdims=True)


def _ssd_fwd(xs, Bm, Cm, dt, cs, csT, name):
    H, S, P = xs.shape
    L, N = SSD_CHUNK, SSD_STATE
    nc = S // L
    rep = H // SSD_GROUPS

    def body(x_ref, b_ref, c_ref, dt_ref, cs_ref, csT_ref, y_ref, prev_ref, st_ref):
        c, h = pl.program_id(0), pl.program_id(1)

        @pl.when(c == 0)
        def _():
            st_ref[h] = jnp.zeros((P, N), F32)

        dtc = _head_col(dt_ref, h)
        csc = _head_col(cs_ref, h)
        csr = csT_ref[pl.ds(h, 1), :]
        cs_last = csr[:, L - 1:L]
        xc = x_ref[...] * dtc
        Bv, Cv = b_ref[...], c_ref[...]
        tril = _tri(L, False, False)
        lam = jnp.where(tril, jnp.exp(jnp.where(tril, csc - csr, 0.0)), 0.0)
        m = _dot(Cv, Bv, 1, 1) * lam
        prev = st_ref[h]
        y_ref[...] = _dot(m, xc, 1, 0) + _dot(Cv, prev, 1, 1) * jnp.exp(csc)
        prev_ref[...] = prev
        st_ref[h] = prev * jnp.exp(cs_last) + _dot(xc * jnp.exp(cs_last - csc), Bv, 0, 0)

    grp = lambda c, h: (h // rep, c, 0)
    return pl.pallas_call(
        body, grid=(nc, H),
        in_specs=[pl.BlockSpec((None, L, P), lambda c, h: (h, c, 0)), pl.BlockSpec((None, L, N), grp),
                  pl.BlockSpec((None, L, N), grp), pl.BlockSpec((L, H), lambda c, h: (c, 0)),
                  pl.BlockSpec((L, H), lambda c, h: (c, 0)), pl.BlockSpec((H, L), lambda c, h: (0, c))],
        out_specs=[pl.BlockSpec((None, L, P), lambda c, h: (h, c, 0)),
                   pl.BlockSpec((None, None, P, N), lambda c, h: (h, c, 0, 0))],
        out_shape=[jax.ShapeDtypeStruct((H, S, P), F32), jax.ShapeDtypeStruct((H, nc, P, N), F32)],
        scratch_shapes=[pltpu.VMEM((H, P, N), F32)],
        compiler_params=_cparams("arbitrary", "arbitrary"), name=name)(xs, Bm, Cm, dt, cs, csT)


def _ssd_bwd(xs, Bm, Cm, dt, cs, csT, prev, dy, a_log, d_skip, name):
    H, S, P = xs.shape
    L, N = SSD_CHUNK, SSD_STATE
    nc = S // L
    rep = H // SSD_GROUPS

    def body(x_ref, b_ref, c_ref, dt_ref, cs_ref, csT_ref, prev_ref, dy_ref, al_ref, dk_ref,
             dx_ref, db_ref, dc_ref, ddt_ref, da_ref, g_ref):
        ci, h = pl.program_id(0), pl.program_id(1)
        sel = lax.broadcasted_iota(jnp.int32, (1, H), 1) == h

        @pl.when(ci == 0)
        def _():
            g_ref[h] = jnp.zeros((P, N), F32)

        @pl.when((ci == 0) & (h == 0))
        def _():
            da_ref[...] = jnp.zeros_like(da_ref)

        @pl.when(h == 0)
        def _():
            ddt_ref[...] = jnp.zeros_like(ddt_ref)

        @pl.when(h % rep == 0)
        def _():
            db_ref[...] = jnp.zeros_like(db_ref)
            dc_ref[...] = jnp.zeros_like(dc_ref)

        A_h = -jnp.exp(jnp.sum(jnp.where(sel, al_ref[...], 0.0), axis=1, keepdims=True))
        dsk = jnp.sum(jnp.where(sel, dk_ref[...], 0.0), axis=1, keepdims=True)
        dtc = _head_col(dt_ref, h)
        csc = _head_col(cs_ref, h)
        csr = csT_ref[pl.ds(h, 1), :]
        cs_last = csr[:, L - 1:L]
        xv = x_ref[...]
        xc = xv * dtc
        Bv, Cv = b_ref[...], c_ref[...]
        dY = dy_ref[...]
        prv = prev_ref[...]
        G = g_ref[h]
        ecs = jnp.exp(csc)
        w = jnp.exp(cs_last - csc)
        cd = jnp.exp(cs_last)
        tril = _tri(L, False, False)
        triu = _tri(L, False, True)
        lam = jnp.where(tril, jnp.exp(jnp.where(tril, csc - csr, 0.0)), 0.0)
        lamT = jnp.where(triu, jnp.exp(jnp.where(triu, csr - csc, 0.0)), 0.0)
        m = _dot(Cv, Bv, 1, 1) * lam
        mT = _dot(Bv, Cv, 1, 1) * lamT
        dM = _dot(dY, xc, 1, 1)
        dMT = _dot(xc, dY, 1, 1)
        dxc = _dot(mT, dY, 1, 0)
        dC = _dot(dM * lam, Bv, 1, 0)
        dB = _dot(dMT * lamT, Cv, 1, 0)
        dcs = jnp.sum(dM * m, axis=1, keepdims=True) - jnp.sum(dMT * mT, axis=1, keepdims=True)
        yoff = _dot(Cv, prv, 1, 1) * ecs
        dcs = dcs + jnp.sum(dY * yoff, axis=1, keepdims=True)
        dYe = dY * ecs
        dprev = _dot(dYe, Cv, 0, 0)
        dC = dC + _dot(dYe, prv, 1, 0)
        BG = _dot(Bv, G, 1, 1)
        dxc = dxc + w * BG
        dB = dB + _dot(xc * w, G, 1, 0)
        dww = jnp.sum(xc * BG, axis=1, keepdims=True) * w
        dcs = dcs - dww
        extra = jnp.sum(dww, axis=0, keepdims=True) + cd * jnp.sum(jnp.sum(G * prv, axis=1, keepdims=True), axis=0, keepdims=True)
        g_ref[h] = G * cd + dprev
        da = jnp.dot(_tri(L, False, True).astype(F32), dcs, precision=_HI, preferred_element_type=F32) + extra
        ddt = da * A_h + jnp.sum(dxc * xv, axis=1, keepdims=True)
        dx_ref[...] = dxc * dtc + dY * dsk
        db_ref[...] += dB
        dc_ref[...] += dC
        ddt_ref[...] += jnp.where(sel, ddt, 0.0)
        da_ref[...] += jnp.where(sel, jnp.sum(da * dtc, axis=0, keepdims=True), 0.0)

    rc = lambda ci: nc - 1 - ci
    grp = lambda ci, h: (h // rep, rc(ci), 0)
    hd = lambda ci, h: (h, rc(ci), 0)
    tok = lambda ci, h: (rc(ci), 0)
    vec = pl.BlockSpec((1, H), lambda ci, h: (0, 0))
    return pl.pallas_call(
        body, grid=(nc, H),
        in_specs=[pl.BlockSpec((None, L, P), hd), pl.BlockSpec((None, L, N), grp), pl.BlockSpec((None, L, N), grp),
                  pl.BlockSpec((L, H), tok), pl.BlockSpec((L, H), tok), pl.BlockSpec((H, L), lambda ci, h: (0, rc(ci))),
                  pl.BlockSpec((None, None, P, N), lambda ci, h: (h, rc(ci), 0, 0)), pl.BlockSpec((None, L, P), hd), vec, vec],
        out_specs=[pl.BlockSpec((None, L, P), hd), pl.BlockSpec((None, L, N), grp), pl.BlockSpec((None, L, N), grp),
                   pl.BlockSpec((L, H), tok), vec],
        out_shape=[jax.ShapeDtypeStruct((H, S, P), F32), jax.ShapeDtypeStruct((SSD_GROUPS, S, N), F32),
                   jax.ShapeDtypeStruct((SSD_GROUPS, S, N), F32), jax.ShapeDtypeStruct((S, H), F32),
                   jax.ShapeDtypeStruct((1, H), F32)],
        scratch_shapes=[pltpu.VMEM((H, P, N), F32)],
        compiler_params=_cparams("arbitrary", "arbitrary"), name=name)(xs, Bm, Cm, dt, cs, csT, prev, dy, a_log, d_skip)


def _dt_bwd(ddt, dA, dtp, dt_bias, a_log, name):
    S, H = ddt.shape
    tm = _rows(S, 512)

    def body(g_ref, da_ref, p_ref, b_ref, al_ref, o_ref, db_ref, dal_ref):
        @pl.when(pl.program_id(0) == 0)
        def _():
            db_ref[...] = jnp.zeros_like(db_ref)
            dal_ref[...] = da_ref[...] * (-jnp.exp(al_ref[...]))

        g = g_ref[...] * _sigmoid(p_ref[:, :H] + b_ref[...])
        db_ref[...] += jnp.sum(g, axis=0, keepdims=True)
        o_ref[...] = jnp.zeros_like(o_ref)
        o_ref[:, :H] = g.astype(o_ref.dtype)

    vec = pl.BlockSpec((1, H), lambda i: (0, 0))
    return pl.pallas_call(
        body, grid=(S // tm,),
        in_specs=[pl.BlockSpec((tm, H), lambda i: (i, 0)), vec, pl.BlockSpec((tm, DT_PAD), lambda i: (i, 0)), vec, vec],
        out_specs=[pl.BlockSpec((tm, DT_PAD), lambda i: (i, 0)), vec, vec],
        out_shape=[jax.ShapeDtypeStruct((S, DT_PAD), _MXU), jax.ShapeDtypeStruct((1, H), F32), jax.ShapeDtypeStruct((1, H), F32)],
        compiler_params=_cparams("arbitrary"), name=name)(ddt, dA, dtp, dt_bias, a_log)


def _ssd_gate_fwd(y, act, z, dskip, w, name):
    S, D = y.shape
    tm = _rows(S, 256)
    Gw = D // SSD_GROUPS

    def body(y_ref, x_ref, z_ref, k_ref, w_ref, o_ref):
        zv = z_ref[...]
        y2 = (y_ref[...] + x_ref[...] * k_ref[...]) * (zv * _sigmoid(zv))
        for g in range(SSD_GROUPS):
            sl = slice(g * Gw, (g + 1) * Gw)
            v = y2[:, sl]
            r = lax.rsqrt(jnp.mean(v * v, axis=-1, keepdims=True) + EPS)
            o_ref[:, sl] = ((v * r) * w_ref[:, sl]).astype(o_ref.dtype)

    row = pl.BlockSpec((tm, D), lambda i: (i, 0))
    vec = pl.BlockSpec((1, D), lambda i: (0, 0))
    return pl.pallas_call(body, grid=(S // tm,), in_specs=[row, row, row, vec, vec], out_specs=row,
                          out_shape=jax.ShapeDtypeStruct((S, D), _MXU), compiler_params=_cparams("parallel"),
                          name=name)(y, act, z, dskip, w)


def _ssd_gate_bwd(dyn, y, act, z, dskip, w, name):
    S, D = y.shape
    tm = _rows(S, 256)
    Gw = D // SSD_GROUPS

    def body(g_ref, y_ref, x_ref, z_ref, k_ref, w_ref, dy_ref, dz_ref, dk_ref, dw_ref):
        @pl.when(pl.program_id(0) == 0)
        def _():
            dk_ref[...] = jnp.zeros_like(dk_ref)
            dw_ref[...] = jnp.zeros_like(dw_ref)

        zv = z_ref[...]
        xv = x_ref[...]
        s = _sigmoid(zv)
        sz = zv * s
        y1 = y_ref[...] + xv * k_ref[...]
        y2 = y1 * sz
        for g in range(SSD_GROUPS):
            sl = slice(g * Gw, (g + 1) * Gw)
            v = y2[:, sl]
            r = lax.rsqrt(jnp.mean(v * v, axis=-1, keepdims=True) + EPS)
            vn = v * r
            gy = g_ref[:, sl].astype(F32)
            dw_ref[:, sl] += jnp.sum(gy * vn, axis=0, keepdims=True)
            dvn = gy * w_ref[:, sl]
            dy2 = r * (dvn - vn * jnp.mean(dvn * vn, axis=-1, keepdims=True))
            dy1 = dy2 * sz[:, sl]
            dy_ref[:, sl] = dy1
            dz_ref[:, sl] = (dy2 * y1[:, sl] * (s[:, sl] * (1.0 + zv[:, sl] * (1.0 - s[:, sl])))).astype(dz_ref.dtype)
            dk_ref[:, sl] += jnp.sum(dy1 * xv[:, sl], axis=0, keepdims=True)

    row = pl.BlockSpec((tm, D), lambda i: (i, 0))
    vec = pl.BlockSpec((1, D), lambda i: (0, 0))
    return pl.pallas_call(
        body, grid=(S // tm,), in_specs=[row, row, row, row, vec, vec], out_specs=[row, row, vec, vec],
        out_shape=[jax.ShapeDtypeStruct((S, D), F32), jax.ShapeDtypeStruct((S, D), _MXU),
                   jax.ShapeDtypeStruct((1, D), F32), jax.ShapeDtypeStruct((1, D), F32)],
        compiler_params=_cparams("arbitrary"), name=name)(dyn, y, act, z, dskip, w)


def _split_dot(v, u):
    hi = v.astype(_MXU)
    lo = (v - hi.astype(F32)).astype(_MXU)
    dn = (((1,), (0,)), ((), ()))
    return (lax.dot_general(hi, u, dn, preferred_element_type=F32) + lax.dot_general(lo, u, dn, preferred_element_type=F32))


def _sb_tiles(S):
    return _pick(S, 256, 128)


SB_LANES = 128
SB_PACK = SB_LANES // SB_HEAD_DIM
SB_ROWS = 128
SB_SCALE = 1.0 / math.sqrt(SB_HEAD_DIM)


def _head_masks():
    lane = lax.broadcasted_iota(jnp.int32, (1, SB_LANES), 1)
    return [(lane // SB_HEAD_DIM) == hh for hh in range(SB_PACK)]


def _by_head(hm, vals):
    out = vals[-1]
    for hh in range(SB_PACK - 2, -1, -1):
        out = jnp.where(hm[hh], vals[hh], out)
    return out


def _sb_rows(a, r):
    return a[r * SB_ROWS:(r + 1) * SB_ROWS]


def _sb_assemble(hm, vals):
    nr = len(vals) // SB_PACK
    return jnp.concatenate([_by_head(hm, vals[r * SB_PACK:(r + 1) * SB_PACK]) for r in range(nr)], axis=0)


def _sb_scores(zs, U, Rs, masks):
    ls = [-jnp.maximum(z, 0.0) - jnp.log(1.0 + jnp.exp(-jnp.abs(z))) for z in zs]
    if masks is not None:
        ls = [jnp.where(m, l, 0.0) for m, l in zip(masks, ls)]
    Es = [_split_dot(l, U) for l in ls]
    As = [jnp.exp(l + z + (E + R)) for l, z, E, R in zip(ls, zs, Es, Rs)]
    if masks is not None:
        As = [jnp.where(m, A, 0.0) for m, A in zip(masks, As)]
    return ls, [A.astype(_MXU) for A in As]


SB_GROUP = 2


def _sbg_chains(T):
    return [(b, r, hh) for b in range(SB_GROUP) for r in range(T // SB_ROWS) for hh in range(SB_PACK)]


def _lanes(a, b):
    return a[:, b * SB_LANES:(b + 1) * SB_LANES]


def _sbg_join(hm, vals):
    per = len(vals) // SB_GROUP
    return jnp.concatenate([_sb_assemble(hm, vals[b * per:(b + 1) * per]) for b in range(SB_GROUP)], axis=1)


def _sbg_head_sum(hm, a):
    return jnp.concatenate([_by_head(hm, [jnp.sum(jnp.where(m, _lanes(a, b), 0.0), axis=1, keepdims=True) for m in hm])
                            for b in range(SB_GROUP)], axis=1)


def _sbg_fwd(q_arr, k_arr, v_arr, cols, w, name):
    S = q_arr.shape[0]
    T = _sb_tiles(S)
    cq, ck, cv = cols
    GW = SB_GROUP * SB_LANES
    nb = SB_WIDTH // GW

    def body(q_ref, k_ref, v_ref, w_ref, o_ref, y_ref):
        i = pl.program_id(1)
        hm = _head_masks()
        qs = q_ref[...] * SB_SCALE
        chains = _sbg_chains(T)
        qcs = [_sb_rows(jnp.where(hm[hh], _lanes(qs, b), jnp.zeros((T, SB_LANES), qs.dtype)), r) for b, r, hh in chains]
        U = _tri(T, True, False).astype(_MXU)

        def scores_of(j):
            kj = k_ref[pl.ds(pl.multiple_of(j * T, T), T), :]
            return [_dot(qc, _lanes(kj, b), 1, 1) for qc, (b, _, _) in zip(qcs, chains)]

        def weighted(Abs, j):
            vj = v_ref[pl.ds(pl.multiple_of(j * T, T), T), :]
            return _sbg_join(hm, [_dot(Ab, _lanes(vj, b), 1, 0) for Ab, (b, _, _) in zip(Abs, chains)])

        def step(jj, carry):
            acc, Rs, Aprev = carry
            j = i - 1 - jj
            zs = scores_of(j)
            acc = acc + weighted(Aprev, j + 1)
            ls, Abs = _sb_scores(zs, U, Rs, None)
            return acc, tuple(R + jnp.sum(l, axis=1, keepdims=True) for R, l in zip(Rs, ls)), tuple(Abs)

        masks = [_sb_rows(_tri(T, True, False), r) for _, r, _ in chains]
        zero = jnp.zeros((SB_ROWS, 1), F32)
        ls, Abs = _sb_scores(scores_of(i), U, (zero,) * len(chains), masks)
        carry = (jnp.zeros((T, GW), F32), tuple(jnp.sum(l, axis=1, keepdims=True) for l in ls), tuple(Abs))
        acc, _, Alast = lax.fori_loop(0, i, step, carry)
        acc = acc + weighted(Alast, 0)
        o_ref[...] = acc
        r = lax.rsqrt(_sbg_head_sum(hm, acc * acc) * (1.0 / SB_HEAD_DIM) + EPS)
        y_ref[...] = ((acc * r) * w_ref[...]).astype(y_ref.dtype)

    blk = pl.BlockSpec((T, GW), lambda h, i: (i, h))
    return pl.pallas_call(
        body, grid=(nb, S // T),
        in_specs=[pl.BlockSpec((T, GW), lambda h, i: (i, cq + h)), pl.BlockSpec((S, GW), lambda h, i: (0, ck + h), pipeline_mode=pl.Buffered(1)),
                  pl.BlockSpec((S, GW), lambda h, i: (0, cv + h), pipeline_mode=pl.Buffered(1)), pl.BlockSpec((1, GW), lambda h, i: (0, h))],
        out_specs=[blk, blk], out_shape=[jax.ShapeDtypeStruct((S, SB_WIDTH), F32), jax.ShapeDtypeStruct((S, SB_WIDTH), _MXU)],
        compiler_params=_cparams("parallel", "parallel"), name=name)(q_arr, k_arr, v_arr, w)


def _sbg_bwd(q_arr, k_arr, v_arr, cols, o, dy_arr, cdy, w, name):
    S = q_arr.shape[0]
    T = _sb_tiles(S)
    cq, ck, cv = cols
    GW = SB_GROUP * SB_LANES
    nb = SB_WIDTH // GW

    def body(q_ref, k_ref, v_ref, o_ref, dy_ref, w_ref, dq_ref, dk_ref, dv_ref, dw_ref):
        i = pl.program_id(1)

        @pl.when(i == 0)
        def _():
            dk_ref[...] = jnp.zeros_like(dk_ref)
            dv_ref[...] = jnp.zeros_like(dv_ref)
            dw_ref[...] = jnp.zeros_like(dw_ref)

        hm = _head_masks()
        chains = _sbg_chains(T)
        qs = q_ref[...] * SB_SCALE
        ov = o_ref[...]
        gy = dy_ref[...]
        r = lax.rsqrt(_sbg_head_sum(hm, ov * ov) * (1.0 / SB_HEAD_DIM) + EPS)
        on = ov * r
        dw_ref[...] += jnp.sum(gy * on, axis=0, keepdims=True)
        don = gy * w_ref[...]
        do = r * (don - on * (_sbg_head_sum(hm, don * on) * (1.0 / SB_HEAD_DIM)))
        dob = do.astype(_MXU)
        dprod = dob.astype(F32) * ov
        zt = jnp.zeros((T, SB_LANES), dob.dtype)
        qcs = [_sb_rows(jnp.where(hm[hh], _lanes(qs, b), zt), r_) for b, r_, hh in chains]
        doc = [_sb_rows(jnp.where(hm[hh], _lanes(dob, b), zt), r_) for b, r_, hh in chains]
        Dt = [_sb_rows(jnp.sum(jnp.where(hm[hh], _lanes(dprod, b), 0.0), axis=1, keepdims=True), r_) for b, r_, hh in chains]
        U = _tri(T, True, False).astype(_MXU)
        Ui = _tri(T, False, False).astype(_MXU)

        def products_of(j):
            off = pl.multiple_of(j * T, T)
            kj = k_ref[pl.ds(off, T), :]
            vj = v_ref[pl.ds(off, T), :]
            return ([_dot(qc, _lanes(kj, b), 1, 1) for qc, (b, _, _) in zip(qcs, chains)],
                    [_dot(d, _lanes(vj, b), 1, 1) for d, (b, _, _) in zip(doc, chains)])

        def core(zs, dAs, Rs, Qs, masks):
            ls, Abs = _sb_scores(zs, U, Rs, masks)
            Gs = [dA * Ab.astype(F32) for dA, Ab in zip(dAs, Abs)]
            sfx = [_split_dot(G, Ui) for G in Gs]
            dzs = []
            for c, (l, G, s, D, Q) in enumerate(zip(ls, Gs, sfx, Dt, Qs)):
                P = D - (s + Q)
                dz = jnp.exp(l) * (G + P) - P
                if masks is not None:
                    dz = jnp.where(masks[c], dz, 0.0)
                dzs.append(dz.astype(_MXU))
            newR = tuple(R + jnp.sum(l, axis=1, keepdims=True) for R, l in zip(Rs, ls))
            newQ = tuple(Q + jnp.sum(G, axis=1, keepdims=True) for Q, G in zip(Qs, Gs))
            return tuple(Abs), tuple(dzs), newR, newQ

        def lane_tiles(vals):
            per = len(vals) // SB_GROUP
            return jnp.concatenate([sum(vals[b * per + 1:(b + 1) * per], vals[b * per]) for b in range(SB_GROUP)], axis=1)

        def emit(Abs, dzs, j):
            off = pl.multiple_of(j * T, T)
            kj = k_ref[pl.ds(off, T), :]
            dk_ref[pl.ds(off, T), :] += lane_tiles([_dot(dzb, qc, 0, 0) for dzb, qc in zip(dzs, qcs)])
            dv_ref[pl.ds(off, T), :] += lane_tiles([_dot(Ab, d, 0, 0) for Ab, d in zip(Abs, doc)])
            return _sbg_join(hm, [_dot(dzb, _lanes(kj, b), 1, 0) for dzb, (b, _, _) in zip(dzs, chains)])

        def step(jj, carry):
            dq, Rs, Qs, Aprev, dzprev = carry
            j = i - 1 - jj
            zs, dAs = products_of(j)
            dq = dq + emit(Aprev, dzprev, j + 1)
            Abs, dzs, Rs, Qs = core(zs, dAs, Rs, Qs, None)
            return dq, Rs, Qs, Abs, dzs

        masks = [_sb_rows(_tri(T, True, False), r_) for _, r_, _ in chains]
        zero = (jnp.zeros((SB_ROWS, 1), F32),) * len(chains)
        zs, dAs = products_of(i)
        Abs, dzs, Rs, Qs = core(zs, dAs, zero, zero, masks)
        dq, _, _, Alast, dzlast = lax.fori_loop(0, i, step, (jnp.zeros((T, GW), F32), Rs, Qs, Abs, dzs))
        dq = dq + emit(Alast, dzlast, 0)
        dq_ref[...] = (dq * SB_SCALE).astype(dq_ref.dtype)

    blk = pl.BlockSpec((T, GW), lambda h, i: (i, h))
    full = pl.BlockSpec((S, GW), lambda h, i: (0, h), pipeline_mode=pl.Buffered(1))
    wsp = pl.BlockSpec((1, GW), lambda h, i: (0, h))
    return pl.pallas_call(
        body, grid=(nb, S // T),
        in_specs=[pl.BlockSpec((T, GW), lambda h, i: (i, cq + h)), pl.BlockSpec((S, GW), lambda h, i: (0, ck + h), pipeline_mode=pl.Buffered(1)),
                  pl.BlockSpec((S, GW), lambda h, i: (0, cv + h), pipeline_mode=pl.Buffered(1)), blk,
                  pl.BlockSpec((T, GW), lambda h, i: (i, cdy + h)), wsp],
        out_specs=[blk, full, full, wsp],
        out_shape=[jax.ShapeDtypeStruct((S, SB_WIDTH), _MXU), jax.ShapeDtypeStruct((S, SB_WIDTH), F32),
                   jax.ShapeDtypeStruct((S, SB_WIDTH), F32), jax.ShapeDtypeStruct((1, SB_WIDTH), F32)],
        compiler_params=_cparams("parallel", "arbitrary"), name=name)(q_arr, k_arr, v_arr, o, dy_arr, w)


def _adamw(w, g, m, v, name):
    R, C = w.shape
    tm = _rows(R, 256) if R % 8 == 0 else R
    c1 = 1.0 - ADAM_B1 ** ADAM_STEP
    c2 = 1.0 - ADAM_B2 ** ADAM_STEP

    def body(w_ref, g_ref, m_ref, v_ref, d_ref, nm_ref, nv_ref):
        gv = g_ref[...]
        mn = ADAM_B1 * m_ref[...] + (1.0 - ADAM_B1) * gv
        vn = ADAM_B2 * v_ref[...] + (1.0 - ADAM_B2) * (gv * gv)
        d_ref[...] = -ADAM_LR * ((mn / c1) / (jnp.sqrt(vn / c2) + ADAM_EPS) + ADAM_WD * w_ref[...])
        nm_ref[...] = mn
        nv_ref[...] = vn

    blk = pl.BlockSpec((tm, C), lambda i: (i, 0))
    return pl.pallas_call(body, grid=(R // tm,), in_specs=[blk] * 4, out_specs=[blk] * 3,
                          out_shape=[jax.ShapeDtypeStruct((R, C), F32)] * 3, compiler_params=_cparams("parallel"),
                          name=name)(w, g, m, v)


def _sum_lead(a, name, first=None):
    n, R, C = a.shape
    tm = _rows(R, 256)

    def body(*refs):
        a_ref, o_ref = refs[-2:]
        s = a_ref[0] if first is None else refs[0][...] + a_ref[0]
        for p in range(1, n):
            s = s + a_ref[p]
        o_ref[...] = s

    row = pl.BlockSpec((tm, C), lambda i: (i, 0))
    ins = ([] if first is None else [first]) + [a]
    return pl.pallas_call(body, grid=(R // tm,), in_specs=[row] * (len(ins) - 1) + [pl.BlockSpec((n, tm, C), lambda i: (0, i, 0))],
                          out_specs=row, out_shape=jax.ShapeDtypeStruct((R, C), F32),
                          compiler_params=_cparams("parallel"), name=name)(*ins)


_GROUP_BITS = {'c': ((0, 0, 1),), 'xy': ((0, 1, 0), (1, 0, 0), (1, 1, 0)),
               'xyc': tuple((k >> 2 & 1, k >> 1 & 1, k & 1) for k in range(1, 8))}


def _exchange(srcs, *, group, same_src, own, chunks, name):
    flips = _GROUP_BITS[group]
    n = len(flips) + 1
    na = len(srcs)
    blk_shapes = [tuple(s.shape) if same_src else tuple(s.shape[1:]) for s in srcs]
    assert all(bs[0] % chunks == 0 for bs in blk_shapes), blk_shapes

    def body(*refs):
        src_refs, dst_refs = refs[:na], refs[na:2 * na]
        send_sems, recv_sems, loc_sems = refs[2 * na:]
        x, y, c = lax.axis_index("x"), lax.axis_index("y"), lax.axis_index("c")

        def member(px, py, pc):
            return {'c': pc, 'xy': 2 * px + py, 'xyc': 4 * px + 2 * py + pc}[group]

        def piece(ref, a, q):
            rows = blk_shapes[a][0] // chunks
            return ref.at[pl.ds(q * rows, rows)]

        me = member(x, y, c)
        started, arrivals = [], []
        for a in range(na):
            mine = src_refs[a] if same_src else src_refs[a].at[me]
            if own:
                for q in range(chunks):
                    cp = pltpu.make_async_copy(piece(mine, a, q), piece(dst_refs[a].at[me], a, q), loc_sems.at[a * chunks + q])
                    cp.start()
                    started.append(cp.wait)
            for kk, (fx, fy, fc) in enumerate(flips):
                px, py, pc = (1 - x if fx else x), (1 - y if fy else y), (1 - c if fc else c)
                peer = member(px, py, pc)
                out_blk = src_refs[a] if same_src else src_refs[a].at[peer]
                there = dst_refs[a].at[me if own else kk]
                here = dst_refs[a].at[peer if own else kk]
                for q in range(chunks):
                    s = (a * (n - 1) + kk) * chunks + q
                    out = pltpu.make_async_remote_copy(
                        src_ref=piece(out_blk, a, q), dst_ref=piece(there, a, q), send_sem=send_sems.at[s],
                        recv_sem=recv_sems.at[s], device_id=(px, py, pc), device_id_type=pl.DeviceIdType.MESH)
                    out.start()
                    started.append(out.wait_send)
                    arrivals.append(pltpu.make_async_remote_copy(
                        src_ref=piece(mine, a, q), dst_ref=piece(here, a, q), send_sem=send_sems.at[s],
                        recv_sem=recv_sems.at[s], device_id=(px, py, pc), device_id_type=pl.DeviceIdType.MESH).wait_recv)
        for wait in arrivals + started:
            wait()

    nsem = na * (n - 1) * chunks
    hbm = pl.BlockSpec(memory_space=pl.ANY)
    return pl.pallas_call(
        body, in_specs=[hbm] * na, out_specs=[hbm] * na,
        out_shape=[jax.ShapeDtypeStruct(((n if own else n - 1),) + bs, s.dtype) for bs, s in zip(blk_shapes, srcs)],
        scratch_shapes=[pltpu.SemaphoreType.DMA((nsem,)), pltpu.SemaphoreType.DMA((nsem,)),
                        pltpu.SemaphoreType.DMA((na * chunks,))],
        compiler_params=pltpu.CompilerParams(has_side_effects=True), name=name)(*srcs)


def _to_shards(name, full):
    R, C = full.shape
    if name in COL_SPLIT:
        return full.reshape(R, 4, C // 4).transpose(1, 0, 2)
    return full.reshape(4, R // 4, C)


def _from_shards(name, sh):
    n, R, C = sh.shape
    if name in COL_SPLIT:
        return sh.transpose(1, 0, 2).reshape(R, n * C)
    return sh.reshape(n * R, C)


def _pack_rows(parts, width, rows):
    n = parts[0].shape[0]
    flat = jnp.concatenate([p.reshape(n, -1) for p in parts], axis=1)
    return jnp.pad(flat, ((0, 0), (0, rows * width - flat.shape[1]))).reshape(n, rows, width)


def _unpack_rows(buf, shapes):
    n = buf.shape[0]
    flat = buf.reshape(n, -1)
    out, o = [], 0
    for s in shapes:
        sz = math.prod(s)
        out.append(flat[:, o:o + sz].reshape((n,) + tuple(s)))
        o += sz
    return out


def _split_rows(a, rows):
    out, o = [], 0
    for r in rows:
        out.append(a[:, o:o + r])
        o += r
    return out


def _ceil_to(v, m):
    return -(-v // m) * m


def kernel(x, mem, norm_mix_w, w_in, conv_ssd_w, conv_ssd_b, dt_bias, a_log, d_skip, ssd_norm_w, sb_norm_w, w_out, norm_mem_w, norm_memkv_w, w_mq, w_mk, w_mv, w_mo, norm_ffn_w, w_up, conv_ffn_w, conv_ffn_b, w_down, norm_final_w, loss_target, m_norm_mix_w, m_w_in, m_conv_ssd_w, m_conv_ssd_b, m_dt_bias, m_a_log, m_d_skip, m_ssd_norm_w, m_sb_norm_w, m_w_out, m_norm_mem_w, m_norm_memkv_w, m_w_mq, m_w_mk, m_w_mv, m_w_mo, m_norm_ffn_w, m_w_up, m_conv_ffn_w, m_conv_ffn_b, m_w_down, m_norm_final_w, v_norm_mix_w, v_w_in, v_conv_ssd_w, v_conv_ssd_b, v_dt_bias, v_a_log, v_d_skip, v_ssd_norm_w, v_sb_norm_w, v_w_out, v_norm_mem_w, v_norm_memkv_w, v_w_mq, v_w_mk, v_w_mv, v_w_mo, v_norm_ffn_w, v_w_up, v_conv_ffn_w, v_conv_ffn_b, v_w_down, v_norm_final_w):
    W = dict(norm_mix_w=norm_mix_w, w_in=w_in, conv_ssd_w=conv_ssd_w, conv_ssd_b=conv_ssd_b, dt_bias=dt_bias, a_log=a_log,
             d_skip=d_skip, ssd_norm_w=ssd_norm_w, sb_norm_w=sb_norm_w, w_out=w_out, norm_mem_w=norm_mem_w,
             norm_memkv_w=norm_memkv_w, w_mq=w_mq, w_mk=w_mk, w_mv=w_mv, w_mo=w_mo, norm_ffn_w=norm_ffn_w, w_up=w_up,
             conv_ffn_w=conv_ffn_w, conv_ffn_b=conv_ffn_b, w_down=w_down, norm_final_w=norm_final_w)
    Mo = dict(norm_mix_w=m_norm_mix_w, w_in=m_w_in, conv_ssd_w=m_conv_ssd_w, conv_ssd_b=m_conv_ssd_b, dt_bias=m_dt_bias,
              a_log=m_a_log, d_skip=m_d_skip, ssd_norm_w=m_ssd_norm_w, sb_norm_w=m_sb_norm_w, w_out=m_w_out,
              norm_mem_w=m_norm_mem_w, norm_memkv_w=m_norm_memkv_w, w_mq=m_w_mq, w_mk=m_w_mk, w_mv=m_w_mv, w_mo=m_w_mo,
              norm_ffn_w=m_norm_ffn_w, w_up=m_w_up, conv_ffn_w=m_conv_ffn_w, conv_ffn_b=m_conv_ffn_b, w_down=m_w_down,
              norm_final_w=m_norm_final_w)
    Vo = dict(norm_mix_w=v_norm_mix_w, w_in=v_w_in, conv_ssd_w=v_conv_ssd_w, conv_ssd_b=v_conv_ssd_b, dt_bias=v_dt_bias,
              a_log=v_a_log, d_skip=v_d_skip, ssd_norm_w=v_ssd_norm_w, sb_norm_w=v_sb_norm_w, w_out=v_w_out,
              norm_mem_w=v_norm_mem_w, norm_memkv_w=v_norm_memkv_w, w_mq=v_w_mq, w_mk=v_w_mk, w_mv=v_w_mv, w_mo=v_w_mo,
              norm_ffn_w=v_norm_ffn_w, w_up=v_w_up, conv_ffn_w=v_conv_ffn_w, conv_ffn_b=v_conv_ffn_b, w_down=v_w_down,
              norm_final_w=v_norm_final_w)
    shapes = {n: W[n].shape for n in WEIGHTS}
    sh2 = {n: (1, a.shape[-1]) if a.ndim < 3 else a.shape[-2:] for n, a in W.items()}
    w2 = {n: W[n].reshape(sh2[n]) for n in WEIGHTS}
    x2d = x[0]
    S, D = x2d.shape
    H, P, N = SSD_HEADS, SSD_HEAD_DIM, SSD_STATE

    cv_rows = _ceil_to(-(-sum(math.prod(sh2[n]) for n in CONVW) // 128), 32)
    cpack = _pack_rows([w2[n][None] for n in CONVW], 128, cv_rows)[0]
    stacked = jnp.concatenate([w2[n].astype(_MXU) for n in ROW_SPLIT], axis=0)
    g_rows, g_in, g_up, call = _exchange([stacked, w2['w_in'].astype(_MXU), w2['w_up'].astype(_MXU), cpack], group='xy',
                                         same_src=True, own=True, chunks=4, name="gather_weights")
    full = {'w_in': _from_shards('w_in', g_in), 'w_up': _from_shards('w_up', g_up)}
    full.update({n: _from_shards(n, a) for n, a in zip(ROW_SPLIT, _split_rows(g_rows, [sh2[n][0] for n in ROW_SPLIT]))})
    full.update({n: _from_shards(n, a) for n, a in zip(CONVW, _unpack_rows(call, [sh2[n] for n in CONVW]))})

    o1 = SSD_INNER
    o2 = o1 + SSD_XBC
    o3 = o2 + SSD_HEADS
    Wi = full['w_in']
    W_z, W_xbc, W_qkv = Wi[:, :o1], Wi[:, o1:o2], Wi[:, o3:]
    W_dt = jnp.pad(Wi[:, o2:o3], ((0, 0), (0, DT_PAD - SSD_HEADS)))
    W_in_r = jnp.concatenate([W_z, W_xbc, W_qkv, W_dt], axis=1)
    dskip_rep = jnp.repeat(w2['d_skip'], P, axis=1)

    def heads(a, nh):
        return a.reshape(S, nh, a.shape[1] // nh).transpose(1, 0, 2)

    def unheads(a):
        return a.transpose(1, 0, 2).reshape(S, a.shape[0] * a.shape[2])

    h1 = _rms_fwd(x2d, w2['norm_mix_w'], "norm_mix")
    z = _mm(h1, W_z, name="proj_z")
    xbc = _mm(h1, W_xbc, name="proj_xbc")
    dtp = _mm(h1, W_dt, name="proj_dt")
    qkv = _mm(h1, W_qkv, out_dtype=_MXU, name="proj_qkv")
    pre = _dwconv_fwd(xbc, full['conv_ssd_w'], w2['conv_ssd_b'], "ssd_conv")
    act = _silu_fwd(pre, "ssd_conv_silu")
    dt, cs = _ssd_prep(dtp, w2['dt_bias'], w2['a_log'], "ssd_prep")
    csT = cs.T
    xs_h = heads(act[:, :o1], H)
    Bm = heads(act[:, o1:o1 + SSD_GROUPS * N], SSD_GROUPS)
    Cm = heads(act[:, o1 + SSD_GROUPS * N:], SSD_GROUPS)
    y_h, prev = _ssd_fwd(xs_h, Bm, Cm, dt, cs, csT, "ssd_scan")
    y_scan = unheads(y_h)
    xs = act[:, :o1]
    y_ssd = _ssd_gate_fwd(y_scan, xs, z, dskip_rep, w2['ssd_norm_w'], "ssd_gate")
    nsb = SB_WIDTH // (SB_GROUP * SB_LANES)
    qkv_cols = (0, nsb, 2 * nsb)
    o_sb, y_sb = _sbg_fwd(qkv, qkv, qkv, qkv_cols, w2['sb_norm_w'], "sb_attn")
    ycat = jnp.concatenate([y_ssd, y_sb], axis=1)
    x_2 = _mm(ycat, full['w_out'], res=x2d, name="out_proj")
    h2 = _rms_fwd(x_2, w2['norm_mem_w'], "norm_mem")
    qm = _mm(h2, full['w_mq'], out_dtype=_MXU, name="mem_q")
    mn = _rms_fwd(mem[0], w2['norm_memkv_w'], "norm_memkv")
    km = _mm(mn, full['w_mk'], out_dtype=_MXU, name="mem_k")
    vm = _mm(mn, full['w_mv'], out_dtype=_MXU, name="mem_v")
    om = _xattn_fwd(qm, km, vm, "mem_attn")
    x_3 = _mm(om, full['w_mo'], res=x_2, name="mem_o")
    h3 = _rms_fwd(x_3, w2['norm_ffn_w'], "norm_ffn")
    up = _mm(h3, full['w_up'], name="ffn_up")
    u = _dwconv_fwd(up, full['conv_ffn_w'], w2['conv_ffn_b'], "ffn_conv")
    a_ffn = _glu_fwd(u, "ffn_glu")
    x_4 = _mm(a_ffn, full['w_down'], res=x_3, name="ffn_down")
    dx4, dx4b, g_final, loss_blk = _loss_bwd(x_4, loss_target[0], w2['norm_final_w'], "loss_head")

    G = {'norm_final_w': g_final}
    dact = _mm(dx4b, full['w_down'], tb=True, name="d_ffn_act")
    G['w_down'] = _mm(a_ffn, dx4b, ta=True, name="g_w_down")
    du = _glu_bwd(u, dact, "d_ffn_glu")
    G['conv_ffn_w'], G['conv_ffn_b'] = _conv_bwd_w(up, du, full['conv_ffn_w'].shape[0], "g_ffn_conv")
    dup = _dwconv_bwd_x(du, full['conv_ffn_w'], "d_ffn_conv")
    dh3 = _mm(dup, full['w_up'], tb=True, name="d_h3")
    G['w_up'] = _mm(h3, dup, ta=True, name="g_w_up")
    dx3, dx3b, G['norm_ffn_w'] = _rms_bwd(dh3, x_3, w2['norm_ffn_w'], dx4, "d_norm_ffn")
    dom = _mm(dx3b, full['w_mo'], tb=True, out_dtype=_MXU, name="d_mem_o")
    G['w_mo'] = _mm(om, dx3b, ta=True, name="g_w_mo")
    dqm, dkm, dvm = _xattn_bwd(qm, km, vm, dom, "d_mem_attn")
    G['w_mq'] = _mm(h2, dqm, ta=True, name="g_w_mq")
    dh2 = _mm(dqm, full['w_mq'], tb=True, name="d_h2")
    dx2, dx2b, G['norm_mem_w'] = _rms_bwd(dh2, x_2, w2['norm_mem_w'], dx3, "d_norm_mem")
    G['w_mk'] = _mm(mn, dkm, ta=True, name="g_w_mk")
    G['w_mv'] = _mm(mn, dvm, ta=True, name="g_w_mv")
    dmn = _mm(dvm, full['w_mv'], tb=True, res=_mm(dkm, full['w_mk'], tb=True, name="d_mn_k"), name="d_mn_v")
    _, _, G['norm_memkv_w'] = _rms_bwd(dmn, mem[0], w2['norm_memkv_w'], None, "d_norm_memkv")
    dycat = _mm(dx2b, full['w_out'], tb=True, name="d_ycat")
    G['w_out'] = _mm(ycat, dx2b, ta=True, name="g_w_out")
    dy1, dz, g_dskip_lane, G['ssd_norm_w'] = _ssd_gate_bwd(dycat[:, :o1], y_scan, xs, z, dskip_rep, w2['ssd_norm_w'], "d_ssd_gate")
    dxs_h, dB, dC, ddt, dA = _ssd_bwd(xs_h, Bm, Cm, dt, cs, csT, prev, heads(dy1, H), w2['a_log'], w2['d_skip'], "d_ssd_scan")
    G['d_skip'] = jnp.sum(g_dskip_lane.reshape(H, P), axis=1)[None, :]
    dact_xbc = jnp.concatenate([unheads(dxs_h), unheads(dB), unheads(dC)], axis=1)
    dpre = _silu_bwd(pre, dact_xbc, "d_ssd_conv_silu")
    G['conv_ssd_w'], G['conv_ssd_b'] = _conv_bwd_w(xbc, dpre, full['conv_ssd_w'].shape[0], "g_ssd_conv")
    dxbc = _dwconv_bwd_x(dpre, full['conv_ssd_w'], "d_ssd_conv")
    ddtp, G['dt_bias'], G['a_log'] = _dt_bwd(ddt, dA, dtp, w2['dt_bias'], w2['a_log'], "d_dt")
    dq, dk, dv, G['sb_norm_w'] = _sbg_bwd(qkv, qkv, qkv, qkv_cols, o_sb, dycat, o1 // (SB_GROUP * SB_LANES), w2['sb_norm_w'], "d_sb_attn")
    dproj = jnp.concatenate([dz, dxbc, dq, dk.astype(_MXU), dv.astype(_MXU), ddtp], axis=1)
    dh1 = _mm(dproj, W_in_r, tb=True, name="d_h1")
    g_in_r = _mm(h1, dproj, ta=True, name="g_w_in")
    nq = 3 * SB_WIDTH
    G['w_in'] = jnp.concatenate([g_in_r[:, :o2], g_in_r[:, o2 + nq:o2 + nq + SSD_HEADS], g_in_r[:, o2:o2 + nq]], axis=1)
    grad_x, _, G['norm_mix_w'] = _rms_bwd(dh1, x2d, w2['norm_mix_w'], dx2, "d_norm_mix")

    cidx = lax.axis_index("c")
    oidx = 2 * lax.axis_index("x") + lax.axis_index("y")
    by_owner = [jnp.concatenate([_to_shards(n, G[n]) for n in ROW_SPLIT], axis=1), _to_shards('w_in', G['w_in']),
                _to_shards('w_up', G['w_up'])]
    to_pair = [a.reshape(4, 2, a.shape[1] // 2, a.shape[2]).transpose(1, 0, 2, 3) for a in by_owner]
    got = _exchange(to_pair, group='c', same_src=False, own=False, chunks=4, name="reduce_pair")
    pair = []
    for t, g in zip(to_pair, got):
        _, _, r, cw = t.shape
        mine = lax.dynamic_index_in_dim(t, cidx, 0, keepdims=False).reshape(4 * r, cw)
        pair.append(_sum_lead(g.reshape(1, 4 * r, cw), "reduce_pair_sum%d" % len(pair), first=mine).reshape(4, r, cw))
    got = _exchange(pair, group='xy', same_src=False, own=False, chunks=1, name="reduce_chips")
    chips = [_sum_lead(g, "reduce_chips_sum%d" % k, first=lax.dynamic_index_in_dim(p, oidx, 0, keepdims=False))
             for k, (p, g) in enumerate(zip(pair, got))]
    red = [a.reshape(1, -1, a.shape[-1]) for a in
           _exchange(chips, group='c', same_src=True, own=True, chunks=4, name="share_pair")]
    gsh = dict(zip(ROW_SPLIT, [a[0] for a in _split_rows(red[0], [sh2[n][0] for n in ROW_SPLIT])]))
    gsh['w_in'], gsh['w_up'] = red[1][0], red[2][0]

    small_parts = [G[n].reshape(1, -1) for n in SMALL + CONVW] + [loss_blk[:1, :1]]
    small_shapes = [sh2[n] for n in SMALL] + [G[n].shape for n in CONVW] + [(1, 1)]
    small_rows = _ceil_to(-(-sum(math.prod(s) for s in small_shapes) // 128), 8)
    spack = _pack_rows(small_parts, 128, small_rows)[0]
    (gathered,) = _exchange([spack], group='xyc', same_src=True, own=True, chunks=1, name="gather_small")
    parts = [a[0] for a in _unpack_rows(_sum_lead(gathered, "small_sum")[None], small_shapes)]
    gsh.update(zip(SMALL, parts))
    for n, a in zip(CONVW, parts[len(SMALL):-1]):
        gsh[n] = lax.dynamic_index_in_dim(_to_shards(n, a), oidx, 0, keepdims=False)
    loss = parts[-1].reshape(())

    delta, new_m, new_v = {}, {}, {}
    for n in BIG:
        delta[n], new_m[n], new_v[n] = _adamw(w2[n], gsh[n], Mo[n].reshape(sh2[n]), Vo[n].reshape(sh2[n]), "adamw_" + n)
    for grp, width, tag in ((CONVW, 128, "adamw_conv"), (SMALL, 128, "adamw_small")):
        rows = _ceil_to(-(-sum(math.prod(sh2[n]) for n in grp) // width), 8)
        packed = [_pack_rows([src[n].reshape(1, -1) for n in grp], width, rows)[0]
                  for src in (w2, gsh, {n: Mo[n] for n in grp}, {n: Vo[n] for n in grp})]
        outs = _adamw(*packed, tag)
        for dst, o in zip((delta, new_m, new_v), outs):
            dst.update(zip(grp, [a[0] for a in _unpack_rows(o[None], [sh2[n] for n in grp])]))

    def shaped(d):
        return [d[n].reshape(shapes[n]) for n in WEIGHTS]

    return (loss, grad_x[None], *shaped(gsh), *shaped(delta), *shaped(new_m), *shaped(new_v))
```

```python
import math

import jax
import jax.numpy as jnp
from jax import lax
from jax.experimental import pallas as pl
from jax.experimental.pallas import tpu as pltpu

F32 = jnp.float32
_MXU = jnp.bfloat16
_WIRE = jnp.bfloat16
EPS = 1e-6
_VMEM_LIMIT = 48 * 1024 * 1024
_HI = lax.Precision.HIGHEST

SSD_HEADS = 16
SSD_HEAD_DIM = 64
SSD_GROUPS = 2
SSD_STATE = 128
SSD_CHUNK = 128
SSD_INNER = SSD_HEADS * SSD_HEAD_DIM
SSD_XBC = SSD_INNER + 2 * SSD_GROUPS * SSD_STATE
SB_HEADS = 16
SB_HEAD_DIM = 64
SB_WIDTH = SB_HEADS * SB_HEAD_DIM
MEM_HEADS = 4
DT_PAD = 128

ADAM_LR = 0.001
ADAM_B1 = 0.9
ADAM_B2 = 0.999
ADAM_EPS = 1e-08
ADAM_WD = 0.01
ADAM_STEP = 10

WEIGHTS = ['norm_mix_w', 'w_in', 'conv_ssd_w', 'conv_ssd_b', 'dt_bias', 'a_log', 'd_skip', 'ssd_norm_w',
           'sb_norm_w', 'w_out', 'norm_mem_w', 'norm_memkv_w', 'w_mq', 'w_mk', 'w_mv', 'w_mo', 'norm_ffn_w',
           'w_up', 'conv_ffn_w', 'conv_ffn_b', 'w_down', 'norm_final_w']
BIG = ['w_in', 'w_out', 'w_mq', 'w_mk', 'w_mv', 'w_mo', 'w_up', 'w_down']
COL_SPLIT = ('w_in', 'w_up', 'conv_ssd_w', 'conv_ffn_w')
ROW_SPLIT = ['w_out', 'w_mq', 'w_mk', 'w_mv', 'w_mo', 'w_down']
CONVW = ['conv_ssd_w', 'conv_ffn_w']
SMALL = ['norm_mix_w', 'conv_ssd_b', 'dt_bias', 'a_log', 'd_skip', 'ssd_norm_w', 'sb_norm_w', 'norm_mem_w',
         'norm_memkv_w', 'norm_ffn_w', 'conv_ffn_b', 'norm_final_w']


def _cparams(*sem):
    return pltpu.CompilerParams(dimension_semantics=sem if sem else None, vmem_limit_bytes=_VMEM_LIMIT)


def _pick(n, cap, mult=128):
    best = None
    for d in range(mult, min(n, cap) + 1, mult):
        if n % d == 0:
            best = d
    return n if best is None else best


def _dot(a, b, ca, cb):
    return lax.dot_general(a.astype(_MXU), b.astype(_MXU), (((ca,), (cb,)), ((), ())), preferred_element_type=F32)


def _sigmoid(v):
    return 1.0 / (1.0 + jnp.exp(-v))


def _log1p(u):
    w = 1.0 + u
    return jnp.where(w == 1.0, u, jnp.log(w) * (u / (w - 1.0)))


def _mm(a, b, *, ta=False, tb=False, res=None, out_dtype=F32, name):
    if ta:
        K, M = a.shape
    else:
        M, K = a.shape
    if tb:
        N, K2 = b.shape
    else:
        K2, N = b.shape
    assert K == K2, (a.shape, b.shape)
    tm = _pick(M, 1408, 128 if ta else 16)
    tn = _pick(N, 1536)
    tk = _pick(K, 1536)
    nk = K // tk
    a_spec = pl.BlockSpec((tk, tm), lambda i, j, k: (k, i)) if ta else pl.BlockSpec((tm, tk), lambda i, j, k: (i, k))
    b_spec = pl.BlockSpec((tn, tk), lambda i, j, k: (j, k)) if tb else pl.BlockSpec((tk, tn), lambda i, j, k: (k, j))
    o_spec = pl.BlockSpec((tm, tn), lambda i, j, k: (i, j))
    ca, cb = (0 if ta else 1), (1 if tb else 0)

    def body(*refs):
        if res is None:
            a_ref, b_ref, o_ref, acc_ref = refs
            r_ref = None
        else:
            a_ref, b_ref, r_ref, o_ref, acc_ref = refs
        k = pl.program_id(2)

        @pl.when(k == 0)
        def _():
            acc_ref[...] = jnp.zeros_like(acc_ref)

        acc_ref[...] += _dot(a_ref[...], b_ref[...], ca, cb)

        @pl.when(k == nk - 1)
        def _():
            r = acc_ref[...]
            if r_ref is not None:
                r = r + r_ref[...].astype(F32)
            o_ref[...] = r.astype(o_ref.dtype)

    ins = [a, b] + ([] if res is None else [res])
    in_specs = [a_spec, b_spec] + ([] if res is None else [o_spec])
    return pl.pallas_call(
        body, grid=(M // tm, N // tn, nk), in_specs=in_specs, out_specs=o_spec,
        out_shape=jax.ShapeDtypeStruct((M, N), out_dtype), scratch_shapes=[pltpu.VMEM((tm, tn), F32)],
        compiler_params=_cparams("parallel", "parallel", "arbitrary"), name=name)(*ins)


def _rows(S, cap):
    return _pick(S, cap, 8)


def _rms_fwd(x, w, name):
    S, D = x.shape
    tm = _rows(S, 512)

    def body(x_ref, w_ref, o_ref):
        xv = x_ref[...]
        r = lax.rsqrt(jnp.mean(xv * xv, axis=-1, keepdims=True) + EPS)
        o_ref[...] = ((xv * r) * w_ref[...]).astype(o_ref.dtype)

    row = pl.BlockSpec((tm, D), lambda i: (i, 0))
    return pl.pallas_call(body, grid=(S // tm,), in_specs=[row, pl.BlockSpec((1, D), lambda i: (0, 0))], out_specs=row,
                          out_shape=jax.ShapeDtypeStruct((S, D), _MXU), compiler_params=_cparams("parallel"), name=name)(x, w)


def _rms_bwd(dh, x, w, dres, name):
    S, D = x.shape
    tm = _rows(S, 256)

    def body(*refs):
        if dres is None:
            dh_ref, x_ref, w_ref, dx_ref, dxb_ref, dw_ref = refs
            dres_ref = None
        else:
            dh_ref, x_ref, w_ref, dres_ref, dx_ref, dxb_ref, dw_ref = refs
        xv = x_ref[...]
        r = lax.rsqrt(jnp.mean(xv * xv, axis=-1, keepdims=True) + EPS)
        xn = xv * r
        dy = dh_ref[...].astype(F32)

        @pl.when(pl.program_id(0) == 0)
        def _():
            dw_ref[...] = jnp.zeros_like(dw_ref)

        dw_ref[...] += jnp.sum(dy * xn, axis=0, keepdims=True)
        dxn = dy * w_ref[...]
        dx = r * (dxn - xn * jnp.mean(dxn * xn, axis=-1, keepdims=True))
        if dres_ref is not None:
            dx = dx + dres_ref[...]
        dx_ref[...] = dx
        dxb_ref[...] = dx.astype(dxb_ref.dtype)

    row = pl.BlockSpec((tm, D), lambda i: (i, 0))
    vec = pl.BlockSpec((1, D), lambda i: (0, 0))
    ins = [dh, x, w] + ([] if dres is None else [dres])
    in_specs = [row, row, vec] + ([] if dres is None else [row])
    return pl.pallas_call(
        body, grid=(S // tm,), in_specs=in_specs, out_specs=[row, row, vec],
        out_shape=[jax.ShapeDtypeStruct((S, D), F32), jax.ShapeDtypeStruct((S, D), _MXU), jax.ShapeDtypeStruct((1, D), F32)],
        compiler_params=_cparams("arbitrary"), name=name)(*ins)


def _loss_bwd(x, tgt, w, name):
    S, D = x.shape
    tm = _rows(S, 256)

    def body(x_ref, t_ref, w_ref, dx_ref, dxb_ref, dw_ref, loss_ref):
        xv = x_ref[...]
        r = lax.rsqrt(jnp.mean(xv * xv, axis=-1, keepdims=True) + EPS)
        xn = xv * r
        e = xn * w_ref[...] - t_ref[...]

        @pl.when(pl.program_id(0) == 0)
        def _():
            dw_ref[...] = jnp.zeros_like(dw_ref)
            loss_ref[...] = jnp.zeros_like(loss_ref)

        tok = jnp.mean(e * e, axis=-1, keepdims=True)
        loss_ref[...] += jnp.broadcast_to(0.5 * jnp.sum(tok, axis=0, keepdims=True), loss_ref.shape)
        dy = e * (1.0 / D)
        dw_ref[...] += jnp.sum(dy * xn, axis=0, keepdims=True)
        dxn = dy * w_ref[...]
        dx = r * (dxn - xn * jnp.mean(dxn * xn, axis=-1, keepdims=True))
        dx_ref[...] = dx
        dxb_ref[...] = dx.astype(dxb_ref.dtype)

    row = pl.BlockSpec((tm, D), lambda i: (i, 0))
    vec = pl.BlockSpec((1, D), lambda i: (0, 0))
    return pl.pallas_call(
        body, grid=(S // tm,), in_specs=[row, row, vec],
        out_specs=[row, row, vec, pl.BlockSpec((8, 128), lambda i: (0, 0))],
        out_shape=[jax.ShapeDtypeStruct((S, D), F32), jax.ShapeDtypeStruct((S, D), _MXU),
                   jax.ShapeDtypeStruct((1, D), F32), jax.ShapeDtypeStruct((8, 128), F32)],
        compiler_params=_cparams("arbitrary"), name=name)(x, tgt, w)


def _conv_tiles(S, C):
    return _rows(S, 256), _pick(C, 1536)


def _dwconv_fwd(x, w, b, name):
    S, C = x.shape
    K = w.shape[0]
    tm, tc = _conv_tiles(S, C)

    def body(x_ref, p_ref, w_ref, b_ref, o_ref):
        cur = x_ref[...]
        prev = jnp.where(pl.program_id(0) > 0, p_ref[...], 0.0)
        xx = jnp.concatenate([prev, cur], axis=0)
        acc = cur * w_ref[K - 1:K, :] + b_ref[...]
        for d in range(1, K):
            acc = acc + pltpu.roll(xx, d, 0)[8:, :] * w_ref[K - 1 - d:K - d, :]
        o_ref[...] = acc

    return pl.pallas_call(
        body, grid=(S // tm, C // tc),
        in_specs=[pl.BlockSpec((tm, tc), lambda i, j: (i, j)),
                  pl.BlockSpec((8, tc), lambda i, j: (jnp.maximum(i * (tm // 8) - 1, 0), j)),
                  pl.BlockSpec((K, tc), lambda i, j: (0, j)), pl.BlockSpec((1, tc), lambda i, j: (0, j))],
        out_specs=pl.BlockSpec((tm, tc), lambda i, j: (i, j)), out_shape=jax.ShapeDtypeStruct((S, C), F32),
        compiler_params=_cparams("parallel", "parallel"), name=name)(x, x, w, b)


def _conv_bwd_w(x, dy, K, name):
    S, C = x.shape
    tm, tc = _conv_tiles(S, C)

    def body(x_ref, p_ref, dy_ref, dw_ref, db_ref):
        i = pl.program_id(1)

        @pl.when(i == 0)
        def _():
            dw_ref[...] = jnp.zeros_like(dw_ref)
            db_ref[...] = jnp.zeros_like(db_ref)

        cur = x_ref[...]
        prev = jnp.where(i > 0, p_ref[...], 0.0)
        xx = jnp.concatenate([prev, cur], axis=0)
        g = dy_ref[...].astype(F32)
        db_ref[...] += jnp.sum(g, axis=0, keepdims=True)
        dw_ref[K - 1:K, :] += jnp.sum(g * cur, axis=0, keepdims=True)
        for d in range(1, K):
            dw_ref[K - 1 - d:K - d, :] += jnp.sum(g * pltpu.roll(xx, d, 0)[8:, :], axis=0, keepdims=True)

    return pl.pallas_call(
        body, grid=(C // tc, S // tm),
        in_specs=[pl.BlockSpec((tm, tc), lambda j, i: (i, j)),
                  pl.BlockSpec((8, tc), lambda j, i: (jnp.maximum(i * (tm // 8) - 1, 0), j)),
                  pl.BlockSpec((tm, tc), lambda j, i: (i, j))],
        out_specs=[pl.BlockSpec((K, tc), lambda j, i: (0, j)), pl.BlockSpec((1, tc), lambda j, i: (0, j))],
        out_shape=[jax.ShapeDtypeStruct((K, C), F32), jax.ShapeDtypeStruct((1, C), F32)],
        compiler_params=_cparams("parallel", "arbitrary"), name=name)(x, x, dy)


def _dwconv_bwd_x(dy, w, name):
    S, C = dy.shape
    K = w.shape[0]
    tm, tc = _conv_tiles(S, C)
    last = S // tm - 1

    def body(g_ref, n_ref, w_ref, o_ref):
        cur = g_ref[...]
        nxt = jnp.where(pl.program_id(0) < last, n_ref[...], 0.0)
        xx = jnp.concatenate([cur, nxt], axis=0)
        acc = cur * w_ref[K - 1:K, :]
        for d in range(1, K):
            acc = acc + pltpu.roll(xx, tm + 8 - d, 0)[:tm, :] * w_ref[K - 1 - d:K - d, :]
        o_ref[...] = acc.astype(o_ref.dtype)

    return pl.pallas_call(
        body, grid=(S // tm, C // tc),
        in_specs=[pl.BlockSpec((tm, tc), lambda i, j: (i, j)),
                  pl.BlockSpec((8, tc), lambda i, j: (jnp.minimum((i + 1) * (tm // 8), S // 8 - 1), j)),
                  pl.BlockSpec((K, tc), lambda i, j: (0, j))],
        out_specs=pl.BlockSpec((tm, tc), lambda i, j: (i, j)), out_shape=jax.ShapeDtypeStruct((S, C), _MXU),
        compiler_params=_cparams("parallel", "parallel"), name=name)(dy, dy, w)


def _silu_fwd(pre, name):
    S, C = pre.shape
    tm, tc = _conv_tiles(S, C)

    def body(p_ref, o_ref):
        p = p_ref[...]
        o_ref[...] = p * _sigmoid(p)

    blk = pl.BlockSpec((tm, tc), lambda i, j: (i, j))
    return pl.pallas_call(body, grid=(S // tm, C // tc), in_specs=[blk], out_specs=blk,
                          out_shape=jax.ShapeDtypeStruct((S, C), F32), compiler_params=_cparams("parallel", "parallel"),
                          name=name)(pre)


def _silu_bwd(pre, dact, name):
    S, C = pre.shape
    tm, tc = _conv_tiles(S, C)

    def body(p_ref, g_ref, o_ref):
        p = p_ref[...]
        s = _sigmoid(p)
        o_ref[...] = g_ref[...] * (s * (1.0 + p * (1.0 - s)))

    blk = pl.BlockSpec((tm, tc), lambda i, j: (i, j))
    return pl.pallas_call(body, grid=(S // tm, C // tc), in_specs=[blk, blk], out_specs=blk,
                          out_shape=jax.ShapeDtypeStruct((S, C), F32), compiler_params=_cparams("parallel", "parallel"),
                          name=name)(pre, dact)


def _glu_fwd(u, name):
    S, C = u.shape
    Fh = C // 2
    tm = _rows(S, 128)

    def body(u_ref, o_ref):
        g = u_ref[:, :Fh]
        o_ref[...] = (g * _sigmoid(g) * u_ref[:, Fh:]).astype(o_ref.dtype)

    return pl.pallas_call(body, grid=(S // tm,), in_specs=[pl.BlockSpec((tm, C), lambda i: (i, 0))],
                          out_specs=pl.BlockSpec((tm, Fh), lambda i: (i, 0)), out_shape=jax.ShapeDtypeStruct((S, Fh), _MXU),
                          compiler_params=_cparams("parallel"), name=name)(u)


def _glu_bwd(u, dact, name):
    S, C = u.shape
    Fh = C // 2
    tm = _rows(S, 128)

    def body(u_ref, g_ref, o_ref):
        g = u_ref[:, :Fh]
        val = u_ref[:, Fh:]
        da = g_ref[...].astype(F32)
        s = _sigmoid(g)
        o_ref[:, :Fh] = da * val * (s * (1.0 + g * (1.0 - s)))
        o_ref[:, Fh:] = da * (g * s)

    return pl.pallas_call(body, grid=(S // tm,),
                          in_specs=[pl.BlockSpec((tm, C), lambda i: (i, 0)), pl.BlockSpec((tm, Fh), lambda i: (i, 0))],
                          out_specs=pl.BlockSpec((tm, C), lambda i: (i, 0)), out_shape=jax.ShapeDtypeStruct((S, C), F32),
                          compiler_params=_cparams("parallel"), name=name)(u, dact)


def _xattn_fwd(q, k, v, name):
    S, D = q.shape
    M = k.shape[0]
    hd = D // MEM_HEADS
    tm = _rows(S, 512)
    scale = 1.0 / math.sqrt(hd)

    def body(q_ref, k_ref, v_ref, o_ref):
        for h in range(MEM_HEADS):
            sl = slice(h * hd, (h + 1) * hd)
            s = _dot(q_ref[:, sl], k_ref[:, sl], 1, 1) * scale
            p = jnp.exp(s - jnp.max(s, axis=-1, keepdims=True))
            p = p / jnp.sum(p, axis=-1, keepdims=True)
            o_ref[:, sl] = _dot(p, v_ref[:, sl], 1, 0).astype(o_ref.dtype)

    kv = pl.BlockSpec((M, D), lambda i: (0, 0))
    row = pl.BlockSpec((tm, D), lambda i: (i, 0))
    return pl.pallas_call(body, grid=(S // tm,), in_specs=[row, kv, kv], out_specs=row,
                          out_shape=jax.ShapeDtypeStruct((S, D), _MXU), compiler_params=_cparams("parallel"), name=name)(q, k, v)


def _xattn_bwd(q, k, v, do, name):
    S, D = q.shape
    M = k.shape[0]
    hd = D // MEM_HEADS
    tm = _rows(S, 512)
    scale = 1.0 / math.sqrt(hd)

    def body(q_ref, k_ref, v_ref, do_ref, dq_ref, dk_ref, dv_ref):
        @pl.when(pl.program_id(0) == 0)
        def _():
            dk_ref[...] = jnp.zeros_like(dk_ref)
            dv_ref[...] = jnp.zeros_like(dv_ref)

        for h in range(MEM_HEADS):
            sl = slice(h * hd, (h + 1) * hd)
            qh, kh, vh, doh = q_ref[:, sl], k_ref[:, sl], v_ref[:, sl], do_ref[:, sl]
            s = _dot(qh, kh, 1, 1) * scale
            p = jnp.exp(s - jnp.max(s, axis=-1, keepdims=True))
            p = p / jnp.sum(p, axis=-1, keepdims=True)
            dp = _dot(doh, vh, 1, 1)
            dv_ref[:, sl] += _dot(p, doh, 0, 0)
            ds = (p * (dp - jnp.sum(dp * p, axis=-1, keepdims=True))) * scale
            dq_ref[:, sl] = _dot(ds, kh, 1, 0).astype(dq_ref.dtype)
            dk_ref[:, sl] += _dot(ds, qh, 0, 0)

    kv = pl.BlockSpec((M, D), lambda i: (0, 0))
    row = pl.BlockSpec((tm, D), lambda i: (i, 0))
    return pl.pallas_call(
        body, grid=(S // tm,), in_specs=[row, kv, kv, row], out_specs=[row, kv, kv],
        out_shape=[jax.ShapeDtypeStruct((S, D), _MXU), jax.ShapeDtypeStruct((M, D), F32), jax.ShapeDtypeStruct((M, D), F32)],
        compiler_params=_cparams("arbitrary"), name=name)(q, k, v, do)


def _tri(n, strict, upper):
    r = lax.broadcasted_iota(jnp.int32, (n, n), 0)
    c = lax.broadcasted_iota(jnp.int32, (n, n), 1)
    if upper:
        return (c > r) if strict else (c >= r)
    return (r > c) if strict else (r >= c)


def _ssd_prep(dtp, dt_bias, a_log, name):
    S = dtp.shape[0]
    L, H = SSD_CHUNK, SSD_HEADS

    def body(p_ref, b_ref, al_ref, dt_ref, cs_ref):
        v = p_ref[:, :H] + b_ref[...]
        dt = jnp.maximum(v, 0.0) + _log1p(jnp.exp(-jnp.abs(v)))
        dt_ref[...] = dt
        a = dt * (-jnp.exp(al_ref[...]))
        cs_ref[...] = jnp.dot(_tri(L, False, False).astype(F32), a, precision=_HI, preferred_element_type=F32)

    blk = pl.BlockSpec((L, H), lambda c: (c, 0))
    vec = pl.BlockSpec((1, H), lambda c: (0, 0))
    return pl.pallas_call(body, grid=(S // L,), in_specs=[pl.BlockSpec((L, DT_PAD), lambda c: (c, 0)), vec, vec],
                          out_specs=[blk, blk], out_shape=[jax.ShapeDtypeStruct((S, H), F32)] * 2,
                          compiler_params=_cparams("parallel"), name=name)(dtp, dt_bias, a_log)


def _head_col(blk_ref, h):
    sel = lax.broadcasted_iota(jnp.int32, (1, SSD_HEADS), 1) == h
    return jnp.sum(jnp.where(sel, blk_ref[...], 0.0), axis=1, keepdims=True)


def _ssdg_fwd(xs, Bm, Cm, dt, cs, csT, name):
    H, S, P = xs.shape
    L, N = SSD_CHUNK, SSD_STATE
    nc = S // L
    rep = H // SSD_GROUPS
    hs = range(rep)

    def body(x_ref, b_ref, c_ref, dt_ref, cs_ref, csT_ref, y_ref, prev_ref, st_ref):
        c, g = pl.program_id(0), pl.program_id(1)

        @pl.when(c == 0)
        def _():
            for hh in hs:
                st_ref[g * rep + hh] = jnp.zeros((P, N), F32)

        Bv, Cv = b_ref[...], c_ref[...]
        tril = _tri(L, False, False)
        dtc = [_head_col(dt_ref, g * rep + hh) for hh in hs]
        csc = [_head_col(cs_ref, g * rep + hh) for hh in hs]
        csr = [csT_ref[hh:hh + 1, :] for hh in hs]
        last = [r[:, L - 1:L] for r in csr]
        xc = [x_ref[hh] * dtc[hh] for hh in hs]
        cb = _dot(Cv, Bv, 1, 1)
        m = [cb * jnp.where(tril, jnp.exp(jnp.where(tril, csc[hh] - csr[hh], 0.0)), 0.0) for hh in hs]
        prev = [st_ref[g * rep + hh] for hh in hs]
        yd = [_dot(m[hh], xc[hh], 1, 0) for hh in hs]
        yo = [_dot(Cv, prev[hh], 1, 1) for hh in hs]
        new = [_dot(xc[hh] * jnp.exp(last[hh] - csc[hh]), Bv, 0, 0) for hh in hs]
        for hh in hs:
            y_ref[hh] = yd[hh] + yo[hh] * jnp.exp(csc[hh])
            prev_ref[hh] = prev[hh]
            st_ref[g * rep + hh] = prev[hh] * jnp.exp(last[hh]) + new[hh]

    tok = pl.BlockSpec((L, H), lambda c, g: (c, 0))
    return pl.pallas_call(
        body, grid=(nc, SSD_GROUPS),
        in_specs=[pl.BlockSpec((rep, L, P), lambda c, g: (g, c, 0)), pl.BlockSpec((None, L, N), lambda c, g: (g, c, 0)),
                  pl.BlockSpec((None, L, N), lambda c, g: (g, c, 0)), tok, tok, pl.BlockSpec((rep, L), lambda c, g: (g, c))],
        out_specs=[pl.BlockSpec((rep, L, P), lambda c, g: (g, c, 0)),
                   pl.BlockSpec((rep, None, P, N), lambda c, g: (g, c, 0, 0))],
        out_shape=[jax.ShapeDtypeStruct((H, S, P), F32), jax.ShapeDtypeStruct((H, nc, P, N), F32)],
        scratch_shapes=[pltpu.VMEM((H, P, N), F32)],
        compiler_params=_cparams("arbitrary", "arbitrary"), name=name)(xs, Bm, Cm, dt, cs, csT)


def _ssdg_bwd(xs, Bm, Cm, dt, cs, csT, prev, dy, a_log, d_skip, name):
    H, S, P = xs.shape
    L, N = SSD_CHUNK, SSD_STATE
    nc = S // L
    rep = H // SSD_GROUPS
    hs = range(rep)

    def rowsum(a):
        return jnp.sum(a, axis=1, keepdims=True)

    def body(x_ref, b_ref, c_ref, dt_ref, cs_ref, csT_ref, prev_ref, dy_ref, al_ref, dk_ref,
             dx_ref, db_ref, dc_ref, ddt_ref, da_ref, g_ref):
        ci, g = pl.program_id(0), pl.program_id(1)

        @pl.when(ci == 0)
        def _():
            for hh in hs:
                g_ref[g * rep + hh] = jnp.zeros((P, N), F32)

        @pl.when((ci == 0) & (g == 0))
        def _():
            da_ref[...] = jnp.zeros_like(da_ref)

        @pl.when(g == 0)
        def _():
            ddt_ref[...] = jnp.zeros_like(ddt_ref)

        lane = lax.broadcasted_iota(jnp.int32, (1, H), 1)
        sel = [lane == g * rep + hh for hh in hs]
        A_h = [-jnp.exp(rowsum(jnp.where(s, al_ref[...], 0.0))) for s in sel]
        dsk = [rowsum(jnp.where(s, dk_ref[...], 0.0)) for s in sel]
        dtc = [_head_col(dt_ref, g * rep + hh) for hh in hs]
        csc = [_head_col(cs_ref, g * rep + hh) for hh in hs]
        csr = [csT_ref[hh:hh + 1, :] for hh in hs]
        last = [r[:, L - 1:L] for r in csr]
        Bv, Cv = b_ref[...], c_ref[...]
        xv = [x_ref[hh] for hh in hs]
        xc = [xv[hh] * dtc[hh] for hh in hs]
        dY = [dy_ref[hh] for hh in hs]
        prv = [prev_ref[hh] for hh in hs]
        G = [g_ref[g * rep + hh] for hh in hs]
        ecs = [jnp.exp(v) for v in csc]
        w = [jnp.exp(last[hh] - csc[hh]) for hh in hs]
        cd = [jnp.exp(v) for v in last]
        tril = _tri(L, False, False)
        triu = _tri(L, False, True)
        lam = [jnp.where(tril, jnp.exp(jnp.where(tril, csc[hh] - csr[hh], 0.0)), 0.0) for hh in hs]
        lamT = [jnp.where(triu, jnp.exp(jnp.where(triu, csr[hh] - csc[hh], 0.0)), 0.0) for hh in hs]
        cb = _dot(Cv, Bv, 1, 1)
        bc = _dot(Bv, Cv, 1, 1)
        dM = [_dot(dY[hh], xc[hh], 1, 1) for hh in hs]
        dMT = [_dot(xc[hh], dY[hh], 1, 1) for hh in hs]
        cp = [_dot(Cv, prv[hh], 1, 1) for hh in hs]
        BG = [_dot(Bv, G[hh], 1, 1) for hh in hs]
        dYe = [dY[hh] * ecs[hh] for hh in hs]
        dprev = [_dot(dYe[hh], Cv, 0, 0) for hh in hs]
        m = [cb * lam[hh] for hh in hs]
        mT = [bc * lamT[hh] for hh in hs]
        dxc = [_dot(mT[hh], dY[hh], 1, 0) + w[hh] * BG[hh] for hh in hs]
        dcb = sum([dM[hh] * lam[hh] for hh in hs][1:], dM[0] * lam[0])
        dcbT = sum([dMT[hh] * lamT[hh] for hh in hs][1:], dMT[0] * lamT[0])
        dC = _dot(dcb, Bv, 1, 0)
        dB = _dot(dcbT, Cv, 1, 0)
        for hh in hs:
            dC = dC + _dot(dYe[hh], prv[hh], 1, 0)
            dB = dB + _dot(xc[hh] * w[hh], G[hh], 1, 0)
        dc_ref[...] = dC
        db_ref[...] = dB
        ddt_acc = jnp.zeros((L, H), F32)
        da_acc = jnp.zeros((1, H), F32)
        rev = _tri(L, False, True).astype(F32)
        for hh in hs:
            dww = rowsum(xc[hh] * BG[hh]) * w[hh]
            dcs = (rowsum(dM[hh] * m[hh]) - rowsum(dMT[hh] * mT[hh]) + rowsum(dY[hh] * (cp[hh] * ecs[hh])) - dww)
            extra = jnp.sum(dww, axis=0, keepdims=True) + cd[hh] * jnp.sum(rowsum(G[hh] * prv[hh]), axis=0, keepdims=True)
            g_ref[g * rep + hh] = G[hh] * cd[hh] + dprev[hh]
            da = jnp.dot(rev, dcs, precision=_HI, preferred_element_type=F32) + extra
            dx_ref[hh] = dxc[hh] * dtc[hh] + dY[hh] * dsk[hh]
            ddt_acc = ddt_acc + jnp.where(sel[hh], da * A_h[hh] + rowsum(dxc[hh] * xv[hh]), 0.0)
            da_acc = da_acc + jnp.where(sel[hh], jnp.sum(da * dtc[hh], axis=0, keepdims=True), 0.0)
        ddt_ref[...] += ddt_acc
        da_ref[...] += da_acc

    rc = lambda ci: nc - 1 - ci
    hd = pl.BlockSpec((rep, L, P), lambda ci, g: (g, rc(ci), 0))
    grp = pl.BlockSpec((None, L, N), lambda ci, g: (g, rc(ci), 0))
    tok = pl.BlockSpec((L, H), lambda ci, g: (rc(ci), 0))
    vec = pl.BlockSpec((1, H), lambda ci, g: (0, 0))
    return pl.pallas_call(
        body, grid=(nc, SSD_GROUPS),
        in_specs=[hd, grp, grp, tok, tok, pl.BlockSpec((rep, L), lambda ci, g: (g, rc(ci))),
                  pl.BlockSpec((rep, None, P, N), lambda ci, g: (g, rc(ci), 0, 0)), hd, vec, vec],
        out_specs=[hd, grp, grp, tok, vec],
        out_shape=[jax.ShapeDtypeStruct((H, S, P), F32), jax.ShapeDtypeStruct((SSD_GROUPS, S, N), F32),
                   jax.ShapeDtypeStruct((SSD_GROUPS, S, N), F32), jax.ShapeDtypeStruct((S, H), F32),
                   jax.ShapeDtypeStruct((1, H), F32)],
        scratch_shapes=[pltpu.VMEM((H, P, N), F32)],
        compiler_params=_cparams("arbitrary", "arbitrary"), name=name)(xs, Bm, Cm, dt, cs, csT, prev, dy, a_log, d_skip)


def _dt_bwd(ddt, dA, dtp, dt_bias, a_log, name):
    S, H = ddt.shape
    tm = _rows(S, 512)

    def body(g_ref, da_ref, p_ref, b_ref, al_ref, o_ref, db_ref, dal_ref):
        @pl.when(pl.program_id(0) == 0)
        def _():
            db_ref[...] = jnp.zeros_like(db_ref)
            dal_ref[...] = da_ref[...] * (-jnp.exp(al_ref[...]))

        g = g_ref[...] * _sigmoid(p_ref[:, :H] + b_ref[...])
        db_ref[...] += jnp.sum(g, axis=0, keepdims=True)
        o_ref[...] = jnp.zeros_like(o_ref)
        o_ref[:, :H] = g.astype(o_ref.dtype)

    vec = pl.BlockSpec((1, H), lambda i: (0, 0))
    return pl.pallas_call(
        body, grid=(S // tm,),
        in_specs=[pl.BlockSpec((tm, H), lambda i: (i, 0)), vec, pl.BlockSpec((tm, DT_PAD), lambda i: (i, 0)), vec, vec],
        out_specs=[pl.BlockSpec((tm, DT_PAD), lambda i: (i, 0)), vec, vec],
        out_shape=[jax.ShapeDtypeStruct((S, DT_PAD), _MXU), jax.ShapeDtypeStruct((1, H), F32), jax.ShapeDtypeStruct((1, H), F32)],
        compiler_params=_cparams("arbitrary"), name=name)(ddt, dA, dtp, dt_bias, a_log)


def _ssd_gate_fwd(y, act, z, dskip, w, name):
    S, D = y.shape
    tm = _rows(S, 256)
    Gw = D // SSD_GROUPS

    def body(y_ref, x_ref, z_ref, k_ref, w_ref, o_ref):
        zv = z_ref[...]
        y2 = (y_ref[...] + x_ref[...] * k_ref[...]) * (zv * _sigmoid(zv))
        for g in range(SSD_GROUPS):
            sl = slice(g * Gw, (g + 1) * Gw)
            v = y2[:, sl]
            r = lax.rsqrt(jnp.mean(v * v, axis=-1, keepdims=True) + EPS)
            o_ref[:, sl] = ((v * r) * w_ref[:, sl]).astype(o_ref.dtype)

    row = pl.BlockSpec((tm, D), lambda i: (i, 0))
    vec = pl.BlockSpec((1, D), lambda i: (0, 0))
    return pl.pallas_call(body, grid=(S // tm,), in_specs=[row, row, row, vec, vec], out_specs=row,
                          out_shape=jax.ShapeDtypeStruct((S, D), _MXU), compiler_params=_cparams("parallel"),
                          name=name)(y, act, z, dskip, w)


def _ssd_gate_bwd(dyn, y, act, z, dskip, w, name):
    S, D = y.shape
    tm = _rows(S, 256)
    Gw = D // SSD_GROUPS

    def body(g_ref, y_ref, x_ref, z_ref, k_ref, w_ref, dy_ref, dz_ref, dk_ref, dw_ref):
        @pl.when(pl.program_id(0) == 0)
        def _():
            dk_ref[...] = jnp.zeros_like(dk_ref)
            dw_ref[...] = jnp.zeros_like(dw_ref)

        zv = z_ref[...]
        xv = x_ref[...]
        s = _sigmoid(zv)
        sz = zv * s
        y1 = y_ref[...] + xv * k_ref[...]
        y2 = y1 * sz
        for g in range(SSD_GROUPS):
            sl = slice(g * Gw, (g + 1) * Gw)
            v = y2[:, sl]
            r = lax.rsqrt(jnp.mean(v * v, axis=-1, keepdims=True) + EPS)
            vn = v * r
            gy = g_ref[:, sl].astype(F32)
            dw_ref[:, sl] += jnp.sum(gy * vn, axis=0, keepdims=True)
            dvn = gy * w_ref[:, sl]
            dy2 = r * (dvn - vn * jnp.mean(dvn * vn, axis=-1, keepdims=True))
            dy1 = dy2 * sz[:, sl]
            dy_ref[:, sl] = dy1
            dz_ref[:, sl] = (dy2 * y1[:, sl] * (s[:, sl] * (1.0 + zv[:, sl] * (1.0 - s[:, sl])))).astype(dz_ref.dtype)
            dk_ref[:, sl] += jnp.sum(dy1 * xv[:, sl], axis=0, keepdims=True)

    row = pl.BlockSpec((tm, D), lambda i: (i, 0))
    vec = pl.BlockSpec((1, D), lambda i: (0, 0))
    return pl.pallas_call(
        body, grid=(S // tm,), in_specs=[row, row, row, row, vec, vec], out_specs=[row, row, vec, vec],
        out_shape=[jax.ShapeDtypeStruct((S, D), F32), jax.ShapeDtypeStruct((S, D), _MXU),
                   jax.ShapeDtypeStruct((1, D), F32), jax.ShapeDtypeStruct((1, D), F32)],
        compiler_params=_cparams("arbitrary"), name=name)(dyn, y, act, z, dskip, w)


def _split_dot(v, u):
    hi = v.astype(_MXU)
    lo = (v - hi.astype(F32)).astype(_MXU)
    dn = (((1,), (0,)), ((), ()))
    return (lax.dot_general(hi, u, dn, preferred_element_type=F32) + lax.dot_general(lo, u, dn, preferred_element_type=F32))


def _sb_tiles(S):
    return _pick(S, 256, 128)


SB_LANES = 128
SB_PACK = SB_LANES // SB_HEAD_DIM
SB_ROWS = 128
SB_SCALE = 1.0 / math.sqrt(SB_HEAD_DIM)


def _head_masks():
    lane = lax.broadcasted_iota(jnp.int32, (1, SB_LANES), 1)
    return [(lane // SB_HEAD_DIM) == hh for hh in range(SB_PACK)]


def _by_head(hm, vals):
    out = vals[-1]
    for hh in range(SB_PACK - 2, -1, -1):
        out = jnp.where(hm[hh], vals[hh], out)
    return out


def _sb_rows(a, r):
    return a[r * SB_ROWS:(r + 1) * SB_ROWS]


def _sb_assemble(hm, vals):
    nr = len(vals) // SB_PACK
    return jnp.concatenate([_by_head(hm, vals[r * SB_PACK:(r + 1) * SB_PACK]) for r in range(nr)], axis=0)


def _sb_scores(zs, U, Rs, masks):
    ls = [-jnp.maximum(z, 0.0) - jnp.log(1.0 + jnp.exp(-jnp.abs(z))) for z in zs]
    if masks is not None:
        ls = [jnp.where(m, l, 0.0) for m, l in zip(masks, ls)]
    Es = [lax.dot_general(l.astype(_MXU), U, (((1,), (0,)), ((), ())), preferred_element_type=F32) for l in ls]
    As = [jnp.exp(l + z + (E + R)) for l, z, E, R in zip(ls, zs, Es, Rs)]
    if masks is not None:
        As = [jnp.where(m, A, 0.0) for m, A in zip(masks, As)]
    return ls, [A.astype(_MXU) for A in As]


SB_GROUP = 2


def _sbg_chains(T):
    return [(b, r, hh) for b in range(SB_GROUP) for r in range(T // SB_ROWS) for hh in range(SB_PACK)]


def _lanes(a, b):
    return a[:, b * SB_LANES:(b + 1) * SB_LANES]


def _sbg_join(hm, vals):
    per = len(vals) // SB_GROUP
    return jnp.concatenate([_sb_assemble(hm, vals[b * per:(b + 1) * per]) for b in range(SB_GROUP)], axis=1)


def _sbg_head_sum(hm, a):
    return jnp.concatenate([_by_head(hm, [jnp.sum(jnp.where(m, _lanes(a, b), 0.0), axis=1, keepdims=True) for m in hm])
                            for b in range(SB_GROUP)], axis=1)


def _sbg_fwd(q_arr, k_arr, v_arr, cols, w, name):
    S = q_arr.shape[0]
    T = _sb_tiles(S)
    cq, ck, cv = cols
    GW = SB_GROUP * SB_LANES
    nb = SB_WIDTH // GW

    def body(q_ref, k_ref, v_ref, w_ref, o_ref, y_ref):
        i = pl.program_id(1)
        hm = _head_masks()
        qs = q_ref[...] * SB_SCALE
        chains = _sbg_chains(T)
        qcs = [_sb_rows(jnp.where(hm[hh], _lanes(qs, b), jnp.zeros((T, SB_LANES), qs.dtype)), r) for b, r, hh in chains]
        U = _tri(T, True, False).astype(_MXU)

        def scores_of(j):
            kj = k_ref[pl.ds(pl.multiple_of(j * T, T), T), :]
            return [_dot(qc, _lanes(kj, b), 1, 1) for qc, (b, _, _) in zip(qcs, chains)]

        def weighted(Abs, j):
            vj = v_ref[pl.ds(pl.multiple_of(j * T, T), T), :]
            return _sbg_join(hm, [_dot(Ab, _lanes(vj, b), 1, 0) for Ab, (b, _, _) in zip(Abs, chains)])

        def step(jj, carry):
            acc, Rs, Aprev = carry
            j = i - 1 - jj
            zs = scores_of(j)
            acc = acc + weighted(Aprev, j + 1)
            ls, Abs = _sb_scores(zs, U, Rs, None)
            return acc, tuple(R + jnp.sum(l, axis=1, keepdims=True) for R, l in zip(Rs, ls)), tuple(Abs)

        masks = [_sb_rows(_tri(T, True, False), r) for _, r, _ in chains]
        zero = jnp.zeros((SB_ROWS, 1), F32)
        ls, Abs = _sb_scores(scores_of(i), U, (zero,) * len(chains), masks)
        carry = (jnp.zeros((T, GW), F32), tuple(jnp.sum(l, axis=1, keepdims=True) for l in ls), tuple(Abs))
        acc, _, Alast = lax.fori_loop(0, i, step, carry)
        acc = acc + weighted(Alast, 0)
        o_ref[...] = acc
        r = lax.rsqrt(_sbg_head_sum(hm, acc * acc) * (1.0 / SB_HEAD_DIM) + EPS)
        y_ref[...] = ((acc * r) * w_ref[...]).astype(y_ref.dtype)

    blk = pl.BlockSpec((T, GW), lambda h, i: (i, h))
    return pl.pallas_call(
        body, grid=(nb, S // T),
        in_specs=[pl.BlockSpec((T, GW), lambda h, i: (i, cq + h)), pl.BlockSpec((S, GW), lambda h, i: (0, ck + h), pipeline_mode=pl.Buffered(1)),
                  pl.BlockSpec((S, GW), lambda h, i: (0, cv + h), pipeline_mode=pl.Buffered(1)), pl.BlockSpec((1, GW), lambda h, i: (0, h))],
        out_specs=[blk, blk], out_shape=[jax.ShapeDtypeStruct((S, SB_WIDTH), F32), jax.ShapeDtypeStruct((S, SB_WIDTH), _MXU)],
        compiler_params=_cparams("parallel", "parallel"), name=name)(q_arr, k_arr, v_arr, w)


def _sbg_bwd(q_arr, k_arr, v_arr, cols, o, dy_arr, cdy, w, name):
    S = q_arr.shape[0]
    T = _sb_tiles(S)
    cq, ck, cv = cols
    GW = SB_GROUP * SB_LANES
    nb = SB_WIDTH // GW

    def body(q_ref, k_ref, v_ref, o_ref, dy_ref, w_ref, dq_ref, dk_ref, dv_ref, dw_ref):
        i = pl.program_id(1)

        @pl.when(i == 0)
        def _():
            dk_ref[...] = jnp.zeros_like(dk_ref)
            dv_ref[...] = jnp.zeros_like(dv_ref)
            dw_ref[...] = jnp.zeros_like(dw_ref)

        hm = _head_masks()
        chains = _sbg_chains(T)
        qs = q_ref[...] * SB_SCALE
        ov = o_ref[...]
        gy = dy_ref[...]
        r = lax.rsqrt(_sbg_head_sum(hm, ov * ov) * (1.0 / SB_HEAD_DIM) + EPS)
        on = ov * r
        dw_ref[...] += jnp.sum(gy * on, axis=0, keepdims=True)
        don = gy * w_ref[...]
        do = r * (don - on * (_sbg_head_sum(hm, don * on) * (1.0 / SB_HEAD_DIM)))
        dob = do.astype(_MXU)
        dprod = dob.astype(F32) * ov
        zt = jnp.zeros((T, SB_LANES), dob.dtype)
        qm = [[jnp.where(hm[hh], _lanes(qs, b), zt) for hh in range(SB_PACK)] for b in range(SB_GROUP)]
        dm = [[jnp.where(hm[hh], _lanes(dob, b), zt) for hh in range(SB_PACK)] for b in range(SB_GROUP)]
        qcs = [_sb_rows(qm[b][hh], r_) for b, r_, hh in chains]
        doc = [_sb_rows(dm[b][hh], r_) for b, r_, hh in chains]
        Dt = [_sb_rows(jnp.sum(jnp.where(hm[hh], _lanes(dprod, b), 0.0), axis=1, keepdims=True), r_) for b, r_, hh in chains]
        U = _tri(T, True, False).astype(_MXU)
        Ui = _tri(T, False, False).astype(_MXU)

        def products_of(j):
            off = pl.multiple_of(j * T, T)
            kj = k_ref[pl.ds(off, T), :]
            vj = v_ref[pl.ds(off, T), :]
            return ([_dot(qc, _lanes(kj, b), 1, 1) for qc, (b, _, _) in zip(qcs, chains)],
                    [_dot(d, _lanes(vj, b), 1, 1) for d, (b, _, _) in zip(doc, chains)])

        def core(zs, dAs, Rs, Qs, masks):
            ls, Abs = _sb_scores(zs, U, Rs, masks)
            Gs = [dA * Ab.astype(F32) for dA, Ab in zip(dAs, Abs)]
            sfx = [_split_dot(G, Ui) for G in Gs]
            dzs = []
            for c, (l, G, s, D, Q) in enumerate(zip(ls, Gs, sfx, Dt, Qs)):
                P = D - (s + Q)
                dz = jnp.exp(l) * (G + P) - P
                if masks is not None:
                    dz = jnp.where(masks[c], dz, 0.0)
                dzs.append(dz.astype(_MXU))
            newR = tuple(R + jnp.sum(l, axis=1, keepdims=True) for R, l in zip(Rs, ls))
            newQ = tuple(Q + jnp.sum(G, axis=1, keepdims=True) for Q, G in zip(Qs, Gs))
            return tuple(Abs), tuple(dzs), newR, newQ

        def over_rows(vals, other):
            nr = T // SB_ROWS
            tiles = []
            for b in range(SB_GROUP):
                acc = None
                for hh in range(SB_PACK):
                    rows = jnp.concatenate([vals[(b * nr + r_) * SB_PACK + hh] for r_ in range(nr)], axis=0)
                    part = _dot(rows, other[b][hh], 0, 0)
                    acc = part if acc is None else acc + part
                tiles.append(acc)
            return jnp.concatenate(tiles, axis=1)

        def emit(Abs, dzs, j):
            off = pl.multiple_of(j * T, T)
            kj = k_ref[pl.ds(off, T), :]
            dk_ref[pl.ds(off, T), :] += over_rows(dzs, qm)
            dv_ref[pl.ds(off, T), :] += over_rows(Abs, dm)
            return _sbg_join(hm, [_dot(dzb, _lanes(kj, b), 1, 0) for dzb, (b, _, _) in zip(dzs, chains)])

        def step(jj, carry):
            dq, Rs, Qs, Aprev, dzprev = carry
            j = i - 1 - jj
            zs, dAs = products_of(j)
            dq = dq + emit(Aprev, dzprev, j + 1)
            Abs, dzs, Rs, Qs = core(zs, dAs, Rs, Qs, None)
            return dq, Rs, Qs, Abs, dzs

        masks = [_sb_rows(_tri(T, True, False), r_) for _, r_, _ in chains]
        zero = (jnp.zeros((SB_ROWS, 1), F32),) * len(chains)
        zs, dAs = products_of(i)
        Abs, dzs, Rs, Qs = core(zs, dAs, zero, zero, masks)
        dq, _, _, Alast, dzlast = lax.fori_loop(0, i, step, (jnp.zeros((T, GW), F32), Rs, Qs, Abs, dzs))
        dq = dq + emit(Alast, dzlast, 0)
        dq_ref[...] = (dq * SB_SCALE).astype(dq_ref.dtype)

    blk = pl.BlockSpec((T, GW), lambda h, i: (i, h))
    full = pl.BlockSpec((S, GW), lambda h, i: (0, h), pipeline_mode=pl.Buffered(1))
    wsp = pl.BlockSpec((1, GW), lambda h, i: (0, h))
    return pl.pallas_call(
        body, grid=(nb, S // T),
        in_specs=[pl.BlockSpec((T, GW), lambda h, i: (i, cq + h)), pl.BlockSpec((S, GW), lambda h, i: (0, ck + h), pipeline_mode=pl.Buffered(1)),
                  pl.BlockSpec((S, GW), lambda h, i: (0, cv + h), pipeline_mode=pl.Buffered(1)), blk,
                  pl.BlockSpec((T, GW), lambda h, i: (i, cdy + h)), wsp],
        out_specs=[blk, full, full, wsp],
        out_shape=[jax.ShapeDtypeStruct((S, SB_WIDTH), _MXU), jax.ShapeDtypeStruct((S, SB_WIDTH), F32),
                   jax.ShapeDtypeStruct((S, SB_WIDTH), F32), jax.ShapeDtypeStruct((1, SB_WIDTH), F32)],
        compiler_params=_cparams("parallel", "arbitrary"), name=name)(q_arr, k_arr, v_arr, o, dy_arr, w)


def _adamw(w, g, m, v, name):
    R, C = w.shape
    tm = _rows(R, 256) if R % 8 == 0 else R
    c1 = 1.0 - ADAM_B1 ** ADAM_STEP
    c2 = 1.0 - ADAM_B2 ** ADAM_STEP

    def body(w_ref, g_ref, m_ref, v_ref, d_ref, nm_ref, nv_ref):
        gv = g_ref[...]
        mn = ADAM_B1 * m_ref[...] + (1.0 - ADAM_B1) * gv
        vn = ADAM_B2 * v_ref[...] + (1.0 - ADAM_B2) * (gv * gv)
        d_ref[...] = -ADAM_LR * ((mn / c1) / (jnp.sqrt(vn / c2) + ADAM_EPS) + ADAM_WD * w_ref[...])
        nm_ref[...] = mn
        nv_ref[...] = vn

    blk = pl.BlockSpec((tm, C), lambda i: (i, 0))
    return pl.pallas_call(body, grid=(R // tm,), in_specs=[blk] * 4, out_specs=[blk] * 3,
                          out_shape=[jax.ShapeDtypeStruct((R, C), F32)] * 3, compiler_params=_cparams("parallel"),
                          name=name)(w, g, m, v)


def _sum_lead(a, name, first=None, wire=False):
    n, R, C = a.shape
    tm = _rows(R, 256)
    nin = 1 if first is None else 2

    def body(*refs):
        a_ref = refs[nin - 1]
        s = a_ref[0].astype(F32) if first is None else refs[0][...] + a_ref[0]
        for p in range(1, n):
            s = s + a_ref[p]
        for o_ref in refs[nin:]:
            o_ref[...] = s.astype(o_ref.dtype)

    row = pl.BlockSpec((tm, C), lambda i: (i, 0))
    ins = ([] if first is None else [first]) + [a]
    outs = [jax.ShapeDtypeStruct((R, C), F32)] + ([jax.ShapeDtypeStruct((R, C), _WIRE)] if wire else [])
    res = pl.pallas_call(body, grid=(R // tm,), in_specs=[row] * (nin - 1) + [pl.BlockSpec((n, tm, C), lambda i: (0, i, 0))],
                         out_specs=[row] * len(outs), out_shape=outs, compiler_params=_cparams("parallel"), name=name)(*ins)
    return res if wire else res[0]


_GROUP_BITS = {'c': ((0, 0, 1),), 'xy': ((0, 1, 0), (1, 0, 0), (1, 1, 0)),
               'xyc': tuple((k >> 2 & 1, k >> 1 & 1, k & 1) for k in range(1, 8))}


def _exchange(srcs, *, group, same_src, own, chunks, name):
    flips = _GROUP_BITS[group]
    n = len(flips) + 1
    na = len(srcs)
    blk_shapes = [tuple(s.shape) if same_src else tuple(s.shape[1:]) for s in srcs]
    assert all(bs[0] % chunks == 0 for bs in blk_shapes), blk_shapes

    def body(*refs):
        src_refs, dst_refs = refs[:na], refs[na:2 * na]
        send_sems, recv_sems, loc_sems = refs[2 * na:]
        x, y, c = lax.axis_index("x"), lax.axis_index("y"), lax.axis_index("c")

        def member(px, py, pc):
            return {'c': pc, 'xy': 2 * px + py, 'xyc': 4 * px + 2 * py + pc}[group]

        def piece(ref, a, q):
            rows = blk_shapes[a][0] // chunks
            return ref.at[pl.ds(q * rows, rows)]

        me = member(x, y, c)
        started, arrivals = [], []
        for a in range(na):
            mine = src_refs[a] if same_src else src_refs[a].at[me]
            if own:
                for q in range(chunks):
                    cp = pltpu.make_async_copy(piece(mine, a, q), piece(dst_refs[a].at[me], a, q), loc_sems.at[a * chunks + q])
                    cp.start()
                    started.append(cp.wait)
            for kk, (fx, fy, fc) in enumerate(flips):
                px, py, pc = (1 - x if fx else x), (1 - y if fy else y), (1 - c if fc else c)
                peer = member(px, py, pc)
                out_blk = src_refs[a] if same_src else src_refs[a].at[peer]
                there = dst_refs[a].at[me if own else kk]
                here = dst_refs[a].at[peer if own else kk]
                for q in range(chunks):
                    s = (a * (n - 1) + kk) * chunks + q
                    out = pltpu.make_async_remote_copy(
                        src_ref=piece(out_blk, a, q), dst_ref=piece(there, a, q), send_sem=send_sems.at[s],
                        recv_sem=recv_sems.at[s], device_id=(px, py, pc), device_id_type=pl.DeviceIdType.MESH)
                    out.start()
                    started.append(out.wait_send)
                    arrivals.append(pltpu.make_async_remote_copy(
                        src_ref=piece(mine, a, q), dst_ref=piece(here, a, q), send_sem=send_sems.at[s],
                        recv_sem=recv_sems.at[s], device_id=(px, py, pc), device_id_type=pl.DeviceIdType.MESH).wait_recv)
        for wait in arrivals + started:
            wait()

    nsem = na * (n - 1) * chunks
    hbm = pl.BlockSpec(memory_space=pl.ANY)
    return pl.pallas_call(
        body, in_specs=[hbm] * na, out_specs=[hbm] * na,
        out_shape=[jax.ShapeDtypeStruct(((n if own else n - 1),) + bs, s.dtype) for bs, s in zip(blk_shapes, srcs)],
        scratch_shapes=[pltpu.SemaphoreType.DMA((nsem,)), pltpu.SemaphoreType.DMA((nsem,)),
                        pltpu.SemaphoreType.DMA((na * chunks,))],
        compiler_params=pltpu.CompilerParams(has_side_effects=True), name=name)(*srcs)


def _to_shards(name, full):
    R, C = full.shape
    if name in COL_SPLIT:
        return full.reshape(R, 4, C // 4).transpose(1, 0, 2)
    return full.reshape(4, R // 4, C)


def _from_shards(name, sh):
    n, R, C = sh.shape
    if name in COL_SPLIT:
        return sh.transpose(1, 0, 2).reshape(R, n * C)
    return sh.reshape(n * R, C)


def _pack_rows(parts, width, rows):
    n = parts[0].shape[0]
    flat = jnp.concatenate([p.reshape(n, -1) for p in parts], axis=1)
    return jnp.pad(flat, ((0, 0), (0, rows * width - flat.shape[1]))).reshape(n, rows, width)


def _unpack_rows(buf, shapes):
    n = buf.shape[0]
    flat = buf.reshape(n, -1)
    out, o = [], 0
    for s in shapes:
        sz = math.prod(s)
        out.append(flat[:, o:o + sz].reshape((n,) + tuple(s)))
        o += sz
    return out


def _split_rows(a, rows):
    out, o = [], 0
    for r in rows:
        out.append(a[:, o:o + r])
        o += r
    return out


def _ceil_to(v, m):
    return -(-v // m) * m


def kernel(x, mem, norm_mix_w, w_in, conv_ssd_w, conv_ssd_b, dt_bias, a_log, d_skip, ssd_norm_w, sb_norm_w, w_out, norm_mem_w, norm_memkv_w, w_mq, w_mk, w_mv, w_mo, norm_ffn_w, w_up, conv_ffn_w, conv_ffn_b, w_down, norm_final_w, loss_target, m_norm_mix_w, m_w_in, m_conv_ssd_w, m_conv_ssd_b, m_dt_bias, m_a_log, m_d_skip, m_ssd_norm_w, m_sb_norm_w, m_w_out, m_norm_mem_w, m_norm_memkv_w, m_w_mq, m_w_mk, m_w_mv, m_w_mo, m_norm_ffn_w, m_w_up, m_conv_ffn_w, m_conv_ffn_b, m_w_down, m_norm_final_w, v_norm_mix_w, v_w_in, v_conv_ssd_w, v_conv_ssd_b, v_dt_bias, v_a_log, v_d_skip, v_ssd_norm_w, v_sb_norm_w, v_w_out, v_norm_mem_w, v_norm_memkv_w, v_w_mq, v_w_mk, v_w_mv, v_w_mo, v_norm_ffn_w, v_w_up, v_conv_ffn_w, v_conv_ffn_b, v_w_down, v_norm_final_w):
    W = dict(norm_mix_w=norm_mix_w, w_in=w_in, conv_ssd_w=conv_ssd_w, conv_ssd_b=conv_ssd_b, dt_bias=dt_bias, a_log=a_log,
             d_skip=d_skip, ssd_norm_w=ssd_norm_w, sb_norm_w=sb_norm_w, w_out=w_out, norm_mem_w=norm_mem_w,
             norm_memkv_w=norm_memkv_w, w_mq=w_mq, w_mk=w_mk, w_mv=w_mv, w_mo=w_mo, norm_ffn_w=norm_ffn_w, w_up=w_up,
             conv_ffn_w=conv_ffn_w, conv_ffn_b=conv_ffn_b, w_down=w_down, norm_final_w=norm_final_w)
    Mo = dict(norm_mix_w=m_norm_mix_w, w_in=m_w_in, conv_ssd_w=m_conv_ssd_w, conv_ssd_b=m_conv_ssd_b, dt_bias=m_dt_bias,
              a_log=m_a_log, d_skip=m_d_skip, ssd_norm_w=m_ssd_norm_w, sb_norm_w=m_sb_norm_w, w_out=m_w_out,
              norm_mem_w=m_norm_mem_w, norm_memkv_w=m_norm_memkv_w, w_mq=m_w_mq, w_mk=m_w_mk, w_mv=m_w_mv, w_mo=m_w_mo,
              norm_ffn_w=m_norm_ffn_w, w_up=m_w_up, conv_ffn_w=m_conv_ffn_w, conv_ffn_b=m_conv_ffn_b, w_down=m_w_down,
              norm_final_w=m_norm_final_w)
    Vo = dict(norm_mix_w=v_norm_mix_w, w_in=v_w_in, conv_ssd_w=v_conv_ssd_w, conv_ssd_b=v_conv_ssd_b, dt_bias=v_dt_bias,
              a_log=v_a_log, d_skip=v_d_skip, ssd_norm_w=v_ssd_norm_w, sb_norm_w=v_sb_norm_w, w_out=v_w_out,
              norm_mem_w=v_norm_mem_w, norm_memkv_w=v_norm_memkv_w, w_mq=v_w_mq, w_mk=v_w_mk, w_mv=v_w_mv, w_mo=v_w_mo,
              norm_ffn_w=v_norm_ffn_w, w_up=v_w_up, conv_ffn_w=v_conv_ffn_w, conv_ffn_b=v_conv_ffn_b, w_down=v_w_down,
              norm_final_w=v_norm_final_w)
    shapes = {n: W[n].shape for n in WEIGHTS}
    sh2 = {n: (1, a.shape[-1]) if a.ndim < 3 else a.shape[-2:] for n, a in W.items()}
    w2 = {n: W[n].reshape(sh2[n]) for n in WEIGHTS}
    x2d = x[0]
    S, D = x2d.shape
    H, P, N = SSD_HEADS, SSD_HEAD_DIM, SSD_STATE

    cv_rows = _ceil_to(-(-sum(math.prod(sh2[n]) for n in CONVW) // 128), 32)
    cpack = _pack_rows([w2[n][None] for n in CONVW], 128, cv_rows)[0]
    stacked = jnp.concatenate([w2[n].astype(_MXU) for n in ROW_SPLIT], axis=0)
    g_rows, g_in, g_up, call = _exchange([stacked, w2['w_in'].astype(_MXU), w2['w_up'].astype(_MXU), cpack], group='xy',
                                         same_src=True, own=True, chunks=4, name="gather_weights")
    full = {'w_in': _from_shards('w_in', g_in), 'w_up': _from_shards('w_up', g_up)}
    full.update({n: _from_shards(n, a) for n, a in zip(ROW_SPLIT, _split_rows(g_rows, [sh2[n][0] for n in ROW_SPLIT]))})
    full.update({n: _from_shards(n, a) for n, a in zip(CONVW, _unpack_rows(call, [sh2[n] for n in CONVW]))})

    o1 = SSD_INNER
    o2 = o1 + SSD_XBC
    o3 = o2 + SSD_HEADS
    Wi = full['w_in']
    W_z, W_xbc, W_qkv = Wi[:, :o1], Wi[:, o1:o2], Wi[:, o3:]
    W_dt = jnp.pad(Wi[:, o2:o3], ((0, 0), (0, DT_PAD - SSD_HEADS)))
    W_in_r = jnp.concatenate([W_z, W_xbc, W_qkv, W_dt], axis=1)
    dskip_rep = jnp.repeat(w2['d_skip'], P, axis=1)

    def heads(a, nh):
        return a.reshape(S, nh, a.shape[1] // nh).transpose(1, 0, 2)

    def unheads(a):
        return a.transpose(1, 0, 2).reshape(S, a.shape[0] * a.shape[2])

    h1 = _rms_fwd(x2d, w2['norm_mix_w'], "norm_mix")
    z = _mm(h1, W_z, name="proj_z")
    xbc = _mm(h1, W_xbc, name="proj_xbc")
    dtp = _mm(h1, W_dt, name="proj_dt")
    qkv = _mm(h1, W_qkv, out_dtype=_MXU, name="proj_qkv")
    pre = _dwconv_fwd(xbc, full['conv_ssd_w'], w2['conv_ssd_b'], "ssd_conv")
    act = _silu_fwd(pre, "ssd_conv_silu")
    dt, cs = _ssd_prep(dtp, w2['dt_bias'], w2['a_log'], "ssd_prep")
    csT = cs.T
    xs_h = heads(act[:, :o1], H)
    Bm = heads(act[:, o1:o1 + SSD_GROUPS * N], SSD_GROUPS)
    Cm = heads(act[:, o1 + SSD_GROUPS * N:], SSD_GROUPS)
    y_h, prev = _ssdg_fwd(xs_h, Bm, Cm, dt, cs, csT, "ssd_scan")
    y_scan = unheads(y_h)
    xs = act[:, :o1]
    y_ssd = _ssd_gate_fwd(y_scan, xs, z, dskip_rep, w2['ssd_norm_w'], "ssd_gate")
    nsb = SB_WIDTH // (SB_GROUP * SB_LANES)
    qkv_cols = (0, nsb, 2 * nsb)
    o_sb, y_sb = _sbg_fwd(qkv, qkv, qkv, qkv_cols, w2['sb_norm_w'], "sb_attn")
    ycat = jnp.concatenate([y_ssd, y_sb], axis=1)
    x_2 = _mm(ycat, full['w_out'], res=x2d, name="out_proj")
    h2 = _rms_fwd(x_2, w2['norm_mem_w'], "norm_mem")
    qm = _mm(h2, full['w_mq'], out_dtype=_MXU, name="mem_q")
    mn = _rms_fwd(mem[0], w2['norm_memkv_w'], "norm_memkv")
    km = _mm(mn, full['w_mk'], out_dtype=_MXU, name="mem_k")
    vm = _mm(mn, full['w_mv'], out_dtype=_MXU, name="mem_v")
    om = _xattn_fwd(qm, km, vm, "mem_attn")
    x_3 = _mm(om, full['w_mo'], res=x_2, name="mem_o")
    h3 = _rms_fwd(x_3, w2['norm_ffn_w'], "norm_ffn")
    up = _mm(h3, full['w_up'], name="ffn_up")
    u = _dwconv_fwd(up, full['conv_ffn_w'], w2['conv_ffn_b'], "ffn_conv")
    a_ffn = _glu_fwd(u, "ffn_glu")
    x_4 = _mm(a_ffn, full['w_down'], res=x_3, name="ffn_down")
    dx4, dx4b, g_final, loss_blk = _loss_bwd(x_4, loss_target[0], w2['norm_final_w'], "loss_head")

    G = {'norm_final_w': g_final}
    dact = _mm(dx4b, full['w_down'], tb=True, name="d_ffn_act")
    G['w_down'] = _mm(a_ffn, dx4b, ta=True, name="g_w_down")
    du = _glu_bwd(u, dact, "d_ffn_glu")
    G['conv_ffn_w'], G['conv_ffn_b'] = _conv_bwd_w(up, du, full['conv_ffn_w'].shape[0], "g_ffn_conv")
    dup = _dwconv_bwd_x(du, full['conv_ffn_w'], "d_ffn_conv")
    dh3 = _mm(dup, full['w_up'], tb=True, name="d_h3")
    G['w_up'] = _mm(h3, dup, ta=True, name="g_w_up")
    dx3, dx3b, G['norm_ffn_w'] = _rms_bwd(dh3, x_3, w2['norm_ffn_w'], dx4, "d_norm_ffn")
    dom = _mm(dx3b, full['w_mo'], tb=True, out_dtype=_MXU, name="d_mem_o")
    G['w_mo'] = _mm(om, dx3b, ta=True, name="g_w_mo")
    dqm, dkm, dvm = _xattn_bwd(qm, km, vm, dom, "d_mem_attn")
    G['w_mq'] = _mm(h2, dqm, ta=True, name="g_w_mq")
    dh2 = _mm(dqm, full['w_mq'], tb=True, name="d_h2")
    dx2, dx2b, G['norm_mem_w'] = _rms_bwd(dh2, x_2, w2['norm_mem_w'], dx3, "d_norm_mem")
    G['w_mk'] = _mm(mn, dkm, ta=True, name="g_w_mk")
    G['w_mv'] = _mm(mn, dvm, ta=True, name="g_w_mv")
    dmn = _mm(dvm, full['w_mv'], tb=True, res=_mm(dkm, full['w_mk'], tb=True, name="d_mn_k"), name="d_mn_v")
    _, _, G['norm_memkv_w'] = _rms_bwd(dmn, mem[0], w2['norm_memkv_w'], None, "d_norm_memkv")
    dycat = _mm(dx2b, full['w_out'], tb=True, name="d_ycat")
    G['w_out'] = _mm(ycat, dx2b, ta=True, name="g_w_out")
    dy1, dz, g_dskip_lane, G['ssd_norm_w'] = _ssd_gate_bwd(dycat[:, :o1], y_scan, xs, z, dskip_rep, w2['ssd_norm_w'], "d_ssd_gate")
    dxs_h, dB, dC, ddt, dA = _ssdg_bwd(xs_h, Bm, Cm, dt, cs, csT, prev, heads(dy1, H), w2['a_log'], w2['d_skip'], "d_ssd_scan")
    G['d_skip'] = jnp.sum(g_dskip_lane.reshape(H, P), axis=1)[None, :]
    dact_xbc = jnp.concatenate([unheads(dxs_h), unheads(dB), unheads(dC)], axis=1)
    dpre = _silu_bwd(pre, dact_xbc, "d_ssd_conv_silu")
    G['conv_ssd_w'], G['conv_ssd_b'] = _conv_bwd_w(xbc, dpre, full['conv_ssd_w'].shape[0], "g_ssd_conv")
    dxbc = _dwconv_bwd_x(dpre, full['conv_ssd_w'], "d_ssd_conv")
    ddtp, G['dt_bias'], G['a_log'] = _dt_bwd(ddt, dA, dtp, w2['dt_bias'], w2['a_log'], "d_dt")
    dq, dk, dv, G['sb_norm_w'] = _sbg_bwd(qkv, qkv, qkv, qkv_cols, o_sb, dycat, o1 // (SB_GROUP * SB_LANES), w2['sb_norm_w'], "d_sb_attn")
    dproj = jnp.concatenate([dz, dxbc, dq, dk.astype(_MXU), dv.astype(_MXU), ddtp], axis=1)
    dh1 = _mm(dproj, W_in_r, tb=True, name="d_h1")
    g_in_r = _mm(h1, dproj, ta=True, name="g_w_in")
    nq = 3 * SB_WIDTH
    G['w_in'] = jnp.concatenate([g_in_r[:, :o2], g_in_r[:, o2 + nq:o2 + nq + SSD_HEADS], g_in_r[:, o2:o2 + nq]], axis=1)
    grad_x, _, G['norm_mix_w'] = _rms_bwd(dh1, x2d, w2['norm_mix_w'], dx2, "d_norm_mix")

    cidx = lax.axis_index("c")
    oidx = 2 * lax.axis_index("x") + lax.axis_index("y")
    by_owner = [jnp.concatenate([_to_shards(n, G[n]) for n in ROW_SPLIT], axis=1), _to_shards('w_in', G['w_in']),
                _to_shards('w_up', G['w_up'])]
    to_pair = [a.reshape(4, 2, a.shape[1] // 2, a.shape[2]).transpose(1, 0, 2, 3) for a in by_owner]
    got = _exchange(to_pair, group='c', same_src=False, own=False, chunks=4, name="reduce_pair")
    pair, pair_wire = [], []
    for t, g in zip(to_pair, got):
        _, _, r, cw = t.shape
        mine = lax.dynamic_index_in_dim(t, cidx, 0, keepdims=False).reshape(4 * r, cw)
        full_sum, wire_sum = _sum_lead(g.reshape(1, 4 * r, cw), "reduce_pair_sum%d" % len(pair), first=mine, wire=True)
        pair.append(full_sum.reshape(4, r, cw))
        pair_wire.append(wire_sum.reshape(4, r, cw))
    got = _exchange(pair_wire, group='xy', same_src=False, own=False, chunks=1, name="reduce_chips")
    chips = [_sum_lead(g, "reduce_chips_sum%d" % k, first=lax.dynamic_index_in_dim(p, oidx, 0, keepdims=False))
             for k, (p, g) in enumerate(zip(pair, got))]
    got = _exchange(chips, group='c', same_src=True, own=False, chunks=4, name="share_pair")
    red = [jnp.where(cidx == 0, jnp.concatenate([m, g[0]], axis=0), jnp.concatenate([g[0], m], axis=0))[None]
           for m, g in zip(chips, got)]
    gsh = dict(zip(ROW_SPLIT, [a[0] for a in _split_rows(red[0], [sh2[n][0] for n in ROW_SPLIT])]))
    gsh['w_in'], gsh['w_up'] = red[1][0], red[2][0]

    small_parts = [G[n].reshape(1, -1) for n in SMALL + CONVW] + [loss_blk[:1, :1]]
    small_shapes = [sh2[n] for n in SMALL] + [G[n].shape for n in CONVW] + [(1, 1)]
    small_rows = _ceil_to(-(-sum(math.prod(s) for s in small_shapes) // 128), 8)
    spack = _pack_rows(small_parts, 128, small_rows)[0]
    (gathered,) = _exchange([spack], group='xyc', same_src=True, own=True, chunks=1, name="gather_small")
    parts = [a[0] for a in _unpack_rows(_sum_lead(gathered, "small_sum")[None], small_shapes)]
    gsh.update(zip(SMALL, parts))
    for n, a in zip(CONVW, parts[len(SMALL):-1]):
        gsh[n] = lax.dynamic_index_in_dim(_to_shards(n, a), oidx, 0, keepdims=False)
    loss = parts[-1].reshape(())

    delta, new_m, new_v = {}, {}, {}
    for n in BIG:
        delta[n], new_m[n], new_v[n] = _adamw(w2[n], gsh[n], Mo[n].reshape(sh2[n]), Vo[n].reshape(sh2[n]), "adamw_" + n)
    for grp, width, tag in ((CONVW, 128, "adamw_conv"), (SMALL, 128, "adamw_small")):
        rows = _ceil_to(-(-sum(math.prod(sh2[n]) for n in grp) // width), 8)
        packed = [_pack_rows([src[n].reshape(1, -1) for n in grp], width, rows)[0]
                  for src in (w2, gsh, {n: Mo[n] for n in grp}, {n: Vo[n] for n in grp})]
        outs = _adamw(*packed, tag)
        for dst, o in zip((delta, new_m, new_v), outs):
            dst.update(zip(grp, [a[0] for a in _unpack_rows(o[None], [sh2[n] for n in grp])]))

    def shaped(d):
        return [d[n].reshape(shapes[n]) for n in WEIGHTS]

    return (loss, grad_x[None], *shaped(gsh), *shaped(delta), *shaped(new_m), *shaped(new_v))
```

```python
import math

import jax
import jax.numpy as jnp
from jax import lax
from jax.experimental import pallas as pl
from jax.experimental.pallas import tpu as pltpu

F32 = jnp.float32
_MXU = jnp.bfloat16
_WIRE = jnp.bfloat16
EPS = 1e-6
_VMEM_LIMIT = 48 * 1024 * 1024
_HI = lax.Precision.HIGHEST

SSD_HEADS = 16
SSD_HEAD_DIM = 64
SSD_GROUPS = 2
SSD_STATE = 128
SSD_CHUNK = 128
SSD_INNER = SSD_HEADS * SSD_HEAD_DIM
SSD_XBC = SSD_INNER + 2 * SSD_GROUPS * SSD_STATE
SB_HEADS = 16
SB_HEAD_DIM = 64
SB_WIDTH = SB_HEADS * SB_HEAD_DIM
MEM_HEADS = 4
DT_PAD = 128

ADAM_LR = 0.001
ADAM_B1 = 0.9
ADAM_B2 = 0.999
ADAM_EPS = 1e-08
ADAM_WD = 0.01
ADAM_STEP = 10

WEIGHTS = ['norm_mix_w', 'w_in', 'conv_ssd_w', 'conv_ssd_b', 'dt_bias', 'a_log', 'd_skip', 'ssd_norm_w',
           'sb_norm_w', 'w_out', 'norm_mem_w', 'norm_memkv_w', 'w_mq', 'w_mk', 'w_mv', 'w_mo', 'norm_ffn_w',
           'w_up', 'conv_ffn_w', 'conv_ffn_b', 'w_down', 'norm_final_w']
BIG = ['w_in', 'w_out', 'w_mq', 'w_mk', 'w_mv', 'w_mo', 'w_up', 'w_down']
COL_SPLIT = ('w_in', 'w_up', 'conv_ssd_w', 'conv_ffn_w')
ROW_SPLIT = ['w_out', 'w_mq', 'w_mk', 'w_mv', 'w_mo', 'w_down']
CONVW = ['conv_ssd_w', 'conv_ffn_w']
SMALL = ['norm_mix_w', 'conv_ssd_b', 'dt_bias', 'a_log', 'd_skip', 'ssd_norm_w', 'sb_norm_w', 'norm_mem_w',
         'norm_memkv_w', 'norm_ffn_w', 'conv_ffn_b', 'norm_final_w']


def _cparams(*sem):
    return pltpu.CompilerParams(dimension_semantics=sem if sem else None, vmem_limit_bytes=_VMEM_LIMIT)


def _pick(n, cap, mult=128):
    best = None
    for d in range(mult, min(n, cap) + 1, mult):
        if n % d == 0:
            best = d
    return n if best is None else best


def _dot(a, b, ca, cb):
    return lax.dot_general(a.astype(_MXU), b.astype(_MXU), (((ca,), (cb,)), ((), ())), preferred_element_type=F32)


def _sigmoid(v):
    return 1.0 / (1.0 + jnp.exp(-v))


def _log1p(u):
    w = 1.0 + u
    return jnp.where(w == 1.0, u, jnp.log(w) * (u / (w - 1.0)))


def _mm(a, b, *, ta=False, tb=False, res=None, out_dtype=F32, name):
    if ta:
        K, M = a.shape
    else:
        M, K = a.shape
    if tb:
        N, K2 = b.shape
    else:
        K2, N = b.shape
    assert K == K2, (a.shape, b.shape)
    tm = _pick(M, 1408, 128 if ta else 16)
    tn = _pick(N, 1536)
    tk = _pick(K, 1536)
    nk = K // tk
    a_spec = pl.BlockSpec((tk, tm), lambda i, j, k: (k, i)) if ta else pl.BlockSpec((tm, tk), lambda i, j, k: (i, k))
    b_spec = pl.BlockSpec((tn, tk), lambda i, j, k: (j, k)) if tb else pl.BlockSpec((tk, tn), lambda i, j, k: (k, j))
    o_spec = pl.BlockSpec((tm, tn), lambda i, j, k: (i, j))
    ca, cb = (0 if ta else 1), (1 if tb else 0)

    def body(*refs):
        if res is None:
            a_ref, b_ref, o_ref, acc_ref = refs
            r_ref = None
        else:
            a_ref, b_ref, r_ref, o_ref, acc_ref = refs
        k = pl.program_id(2)

        @pl.when(k == 0)
        def _():
            acc_ref[...] = jnp.zeros_like(acc_ref)

        acc_ref[...] += _dot(a_ref[...], b_ref[...], ca, cb)

        @pl.when(k == nk - 1)
        def _():
            r = acc_ref[...]
            if r_ref is not None:
                r = r + r_ref[...].astype(F32)
            o_ref[...] = r.astype(o_ref.dtype)

    ins = [a, b] + ([] if res is None else [res])
    in_specs = [a_spec, b_spec] + ([] if res is None else [o_spec])
    return pl.pallas_call(
        body, grid=(M // tm, N // tn, nk), in_specs=in_specs, out_specs=o_spec,
        out_shape=jax.ShapeDtypeStruct((M, N), out_dtype), scratch_shapes=[pltpu.VMEM((tm, tn), F32)],
        compiler_params=_cparams("parallel", "parallel", "arbitrary"), name=name)(*ins)


def _rows(S, cap):
    return _pick(S, cap, 8)


def _rms_fwd(x, w, name):
    S, D = x.shape
    tm = _rows(S, 512)

    def body(x_ref, w_ref, o_ref):
        xv = x_ref[...]
        r = lax.rsqrt(jnp.mean(xv * xv, axis=-1, keepdims=True) + EPS)
        o_ref[...] = ((xv * r) * w_ref[...]).astype(o_ref.dtype)

    row = pl.BlockSpec((tm, D), lambda i: (i, 0))
    return pl.pallas_call(body, grid=(S // tm,), in_specs=[row, pl.BlockSpec((1, D), lambda i: (0, 0))], out_specs=row,
                          out_shape=jax.ShapeDtypeStruct((S, D), _MXU), compiler_params=_cparams("parallel"), name=name)(x, w)


def _rms_bwd(dh, x, w, dres, name):
    S, D = x.shape
    tm = _rows(S, 256)

    def body(*refs):
        if dres is None:
            dh_ref, x_ref, w_ref, dx_ref, dxb_ref, dw_ref = refs
            dres_ref = None
        else:
            dh_ref, x_ref, w_ref, dres_ref, dx_ref, dxb_ref, dw_ref = refs
        xv = x_ref[...]
        r = lax.rsqrt(jnp.mean(xv * xv, axis=-1, keepdims=True) + EPS)
        xn = xv * r
        dy = dh_ref[...].astype(F32)

        @pl.when(pl.program_id(0) == 0)
        def _():
            dw_ref[...] = jnp.zeros_like(dw_ref)

        dw_ref[...] += jnp.sum(dy * xn, axis=0, keepdims=True)
        dxn = dy * w_ref[...]
        dx = r * (dxn - xn * jnp.mean(dxn * xn, axis=-1, keepdims=True))
        if dres_ref is not None:
            dx = dx + dres_ref[...]
        dx_ref[...] = dx
        dxb_ref[...] = dx.astype(dxb_ref.dtype)

    row = pl.BlockSpec((tm, D), lambda i: (i, 0))
    vec = pl.BlockSpec((1, D), lambda i: (0, 0))
    ins = [dh, x, w] + ([] if dres is None else [dres])
    in_specs = [row, row, vec] + ([] if dres is None else [row])
    return pl.pallas_call(
        body, grid=(S // tm,), in_specs=in_specs, out_specs=[row, row, vec],
        out_shape=[jax.ShapeDtypeStruct((S, D), F32), jax.ShapeDtypeStruct((S, D), _MXU), jax.ShapeDtypeStruct((1, D), F32)],
        compiler_params=_cparams("arbitrary"), name=name)(*ins)


def _loss_bwd(x, tgt, w, name):
    S, D = x.shape
    tm = _rows(S, 256)

    def body(x_ref, t_ref, w_ref, dx_ref, dxb_ref, dw_ref, loss_ref):
        xv = x_ref[...]
        r = lax.rsqrt(jnp.mean(xv * xv, axis=-1, keepdims=True) + EPS)
        xn = xv * r
        e = xn * w_ref[...] - t_ref[...]

        @pl.when(pl.program_id(0) == 0)
        def _():
            dw_ref[...] = jnp.zeros_like(dw_ref)
            loss_ref[...] = jnp.zeros_like(loss_ref)

        tok = jnp.mean(e * e, axis=-1, keepdims=True)
        loss_ref[...] += jnp.broadcast_to(0.5 * jnp.sum(tok, axis=0, keepdims=True), loss_ref.shape)
        dy = e * (1.0 / D)
        dw_ref[...] += jnp.sum(dy * xn, axis=0, keepdims=True)
        dxn = dy * w_ref[...]
        dx = r * (dxn - xn * jnp.mean(dxn * xn, axis=-1, keepdims=True))
        dx_ref[...] = dx
        dxb_ref[...] = dx.astype(dxb_ref.dtype)

    row = pl.BlockSpec((tm, D), lambda i: (i, 0))
    vec = pl.BlockSpec((1, D), lambda i: (0, 0))
    return pl.pallas_call(
        body, grid=(S // tm,), in_specs=[row, row, vec],
        out_specs=[row, row, vec, pl.BlockSpec((8, 128), lambda i: (0, 0))],
        out_shape=[jax.ShapeDtypeStruct((S, D), F32), jax.ShapeDtypeStruct((S, D), _MXU),
                   jax.ShapeDtypeStruct((1, D), F32), jax.ShapeDtypeStruct((8, 128), F32)],
        compiler_params=_cparams("arbitrary"), name=name)(x, tgt, w)


def _conv_tiles(S, C):
    return _rows(S, 256), _pick(C, 1536)


def _dwconv_fwd(x, w, b, name):
    S, C = x.shape
    K = w.shape[0]
    tm, tc = _conv_tiles(S, C)

    def body(x_ref, p_ref, w_ref, b_ref, o_ref):
        cur = x_ref[...]
        prev = jnp.where(pl.program_id(0) > 0, p_ref[...], 0.0)
        xx = jnp.concatenate([prev, cur], axis=0)
        acc = cur * w_ref[K - 1:K, :] + b_ref[...]
        for d in range(1, K):
            acc = acc + pltpu.roll(xx, d, 0)[8:, :] * w_ref[K - 1 - d:K - d, :]
        o_ref[...] = acc

    return pl.pallas_call(
        body, grid=(S // tm, C // tc),
        in_specs=[pl.BlockSpec((tm, tc), lambda i, j: (i, j)),
                  pl.BlockSpec((8, tc), lambda i, j: (jnp.maximum(i * (tm // 8) - 1, 0), j)),
                  pl.BlockSpec((K, tc), lambda i, j: (0, j)), pl.BlockSpec((1, tc), lambda i, j: (0, j))],
        out_specs=pl.BlockSpec((tm, tc), lambda i, j: (i, j)), out_shape=jax.ShapeDtypeStruct((S, C), F32),
        compiler_params=_cparams("parallel", "parallel"), name=name)(x, x, w, b)


def _conv_bwd_w(x, dy, K, name):
    S, C = x.shape
    tm, tc = _conv_tiles(S, C)

    def body(x_ref, p_ref, dy_ref, dw_ref, db_ref):
        i = pl.program_id(1)

        @pl.when(i == 0)
        def _():
            dw_ref[...] = jnp.zeros_like(dw_ref)
            db_ref[...] = jnp.zeros_like(db_ref)

        cur = x_ref[...]
        prev = jnp.where(i > 0, p_ref[...], 0.0)
        xx = jnp.concatenate([prev, cur], axis=0)
        g = dy_ref[...].astype(F32)
        db_ref[...] += jnp.sum(g, axis=0, keepdims=True)
        dw_ref[K - 1:K, :] += jnp.sum(g * cur, axis=0, keepdims=True)
        for d in range(1, K):
            dw_ref[K - 1 - d:K - d, :] += jnp.sum(g * pltpu.roll(xx, d, 0)[8:, :], axis=0, keepdims=True)

    return pl.pallas_call(
        body, grid=(C // tc, S // tm),
        in_specs=[pl.BlockSpec((tm, tc), lambda j, i: (i, j)),
                  pl.BlockSpec((8, tc), lambda j, i: (jnp.maximum(i * (tm // 8) - 1, 0), j)),
                  pl.BlockSpec((tm, tc), lambda j, i: (i, j))],
        out_specs=[pl.BlockSpec((K, tc), lambda j, i: (0, j)), pl.BlockSpec((1, tc), lambda j, i: (0, j))],
        out_shape=[jax.ShapeDtypeStruct((K, C), F32), jax.ShapeDtypeStruct((1, C), F32)],
        compiler_params=_cparams("parallel", "arbitrary"), name=name)(x, x, dy)


def _dwconv_bwd_x(dy, w, name):
    S, C = dy.shape
    K = w.shape[0]
    tm, tc = _conv_tiles(S, C)
    last = S // tm - 1

    def body(g_ref, n_ref, w_ref, o_ref):
        cur = g_ref[...]
        nxt = jnp.where(pl.program_id(0) < last, n_ref[...], 0.0)
        xx = jnp.concatenate([cur, nxt], axis=0)
        acc = cur * w_ref[K - 1:K, :]
        for d in range(1, K):
            acc = acc + pltpu.roll(xx, tm + 8 - d, 0)[:tm, :] * w_ref[K - 1 - d:K - d, :]
        o_ref[...] = acc.astype(o_ref.dtype)

    return pl.pallas_call(
        body, grid=(S // tm, C // tc),
        in_specs=[pl.BlockSpec((tm, tc), lambda i, j: (i, j)),
                  pl.BlockSpec((8, tc), lambda i, j: (jnp.minimum((i + 1) * (tm // 8), S // 8 - 1), j)),
                  pl.BlockSpec((K, tc), lambda i, j: (0, j))],
        out_specs=pl.BlockSpec((tm, tc), lambda i, j: (i, j)), out_shape=jax.ShapeDtypeStruct((S, C), _MXU),
        compiler_params=_cparams("parallel", "parallel"), name=name)(dy, dy, w)


def _silu_fwd(pre, name):
    S, C = pre.shape
    tm, tc = _conv_tiles(S, C)

    def body(p_ref, o_ref):
        p = p_ref[...]
        o_ref[...] = p * _sigmoid(p)

    blk = pl.BlockSpec((tm, tc), lambda i, j: (i, j))
    return pl.pallas_call(body, grid=(S // tm, C // tc), in_specs=[blk], out_specs=blk,
                          out_shape=jax.ShapeDtypeStruct((S, C), F32), compiler_params=_cparams("parallel", "parallel"),
                          name=name)(pre)


def _silu_bwd(pre, dact, name):
    S, C = pre.shape
    tm, tc = _conv_tiles(S, C)

    def body(p_ref, g_ref, o_ref):
        p = p_ref[...]
        s = _sigmoid(p)
        o_ref[...] = g_ref[...] * (s * (1.0 + p * (1.0 - s)))

    blk = pl.BlockSpec((tm, tc), lambda i, j: (i, j))
    return pl.pallas_call(body, grid=(S // tm, C // tc), in_specs=[blk, blk], out_specs=blk,
                          out_shape=jax.ShapeDtypeStruct((S, C), F32), compiler_params=_cparams("parallel", "parallel"),
                          name=name)(pre, dact)


def _glu_fwd(u, name):
    S, C = u.shape
    Fh = C // 2
    tm = _rows(S, 128)

    def body(u_ref, o_ref):
        g = u_ref[:, :Fh]
        o_ref[...] = (g * _sigmoid(g) * u_ref[:, Fh:]).astype(o_ref.dtype)

    return pl.pallas_call(body, grid=(S // tm,), in_specs=[pl.BlockSpec((tm, C), lambda i: (i, 0))],
                          out_specs=pl.BlockSpec((tm, Fh), lambda i: (i, 0)), out_shape=jax.ShapeDtypeStruct((S, Fh), _MXU),
                          compiler_params=_cparams("parallel"), name=name)(u)


def _glu_bwd(u, dact, name):
    S, C = u.shape
    Fh = C // 2
    tm = _rows(S, 128)

    def body(u_ref, g_ref, o_ref):
        g = u_ref[:, :Fh]
        val = u_ref[:, Fh:]
        da = g_ref[...].astype(F32)
        s = _sigmoid(g)
        o_ref[:, :Fh] = da * val * (s * (1.0 + g * (1.0 - s)))
        o_ref[:, Fh:] = da * (g * s)

    return pl.pallas_call(body, grid=(S // tm,),
                          in_specs=[pl.BlockSpec((tm, C), lambda i: (i, 0)), pl.BlockSpec((tm, Fh), lambda i: (i, 0))],
                          out_specs=pl.BlockSpec((tm, C), lambda i: (i, 0)), out_shape=jax.ShapeDtypeStruct((S, C), F32),
                          compiler_params=_cparams("parallel"), name=name)(u, dact)


def _xattn_fwd(q, k, v, name):
    S, D = q.shape
    M = k.shape[0]
    hd = D // MEM_HEADS
    tm = _rows(S, 512)
    scale = 1.0 / math.sqrt(hd)

    def body(q_ref, k_ref, v_ref, o_ref):
        for h in range(MEM_HEADS):
            sl = slice(h * hd, (h + 1) * hd)
            s = _dot(q_ref[:, sl], k_ref[:, sl], 1, 1) * scale
            p = jnp.exp(s - jnp.max(s, axis=-1, keepdims=True))
            p = p / jnp.sum(p, axis=-1, keepdims=True)
            o_ref[:, sl] = _dot(p, v_ref[:, sl], 1, 0).astype(o_ref.dtype)

    kv = pl.BlockSpec((M, D), lambda i: (0, 0))
    row = pl.BlockSpec((tm, D), lambda i: (i, 0))
    return pl.pallas_call(body, grid=(S // tm,), in_specs=[row, kv, kv], out_specs=row,
                          out_shape=jax.ShapeDtypeStruct((S, D), _MXU), compiler_params=_cparams("parallel"), name=name)(q, k, v)


def _xattn_bwd(q, k, v, do, name):
    S, D = q.shape
    M = k.shape[0]
    hd = D // MEM_HEADS
    tm = _rows(S, 512)
    scale = 1.0 / math.sqrt(hd)

    def body(q_ref, k_ref, v_ref, do_ref, dq_ref, dk_ref, dv_ref):
        @pl.when(pl.program_id(0) == 0)
        def _():
            dk_ref[...] = jnp.zeros_like(dk_ref)
            dv_ref[...] = jnp.zeros_like(dv_ref)

        for h in range(MEM_HEADS):
            sl = slice(h * hd, (h + 1) * hd)
            qh, kh, vh, doh = q_ref[:, sl], k_ref[:, sl], v_ref[:, sl], do_ref[:, sl]
            s = _dot(qh, kh, 1, 1) * scale
            p = jnp.exp(s - jnp.max(s, axis=-1, keepdims=True))
            p = p / jnp.sum(p, axis=-1, keepdims=True)
            dp = _dot(doh, vh, 1, 1)
            dv_ref[:, sl] += _dot(p, doh, 0, 0)
            ds = (p * (dp - jnp.sum(dp * p, axis=-1, keepdims=True))) * scale
            dq_ref[:, sl] = _dot(ds, kh, 1, 0).astype(dq_ref.dtype)
            dk_ref[:, sl] += _dot(ds, qh, 0, 0)

    kv = pl.BlockSpec((M, D), lambda i: (0, 0))
    row = pl.BlockSpec((tm, D), lambda i: (i, 0))
    return pl.pallas_call(
        body, grid=(S // tm,), in_specs=[row, kv, kv, row], out_specs=[row, kv, kv],
        out_shape=[jax.ShapeDtypeStruct((S, D), _MXU), jax.ShapeDtypeStruct((M, D), F32), jax.ShapeDtypeStruct((M, D), F32)],
        compiler_params=_cparams("arbitrary"), name=name)(q, k, v, do)


def _tri(n, strict, upper):
    r = lax.broadcasted_iota(jnp.int32, (n, n), 0)
    c = lax.broadcasted_iota(jnp.int32, (n, n), 1)
    if upper:
        return (c > r) if strict else (c >= r)
    return (r > c) if strict else (r >= c)


def _ssd_prep(dtp, dt_bias, a_log, name):
    S = dtp.shape[0]
    L, H = SSD_CHUNK, SSD_HEADS

    def body(p_ref, b_ref, al_ref, dt_ref, cs_ref):
        v = p_ref[:, :H] + b_ref[...]
        dt = jnp.maximum(v, 0.0) + _log1p(jnp.exp(-jnp.abs(v)))
        dt_ref[...] = dt
        a = dt * (-jnp.exp(al_ref[...]))
        cs_ref[...] = jnp.dot(_tri(L, False, False).astype(F32), a, precision=_HI, preferred_element_type=F32)

    blk = pl.BlockSpec((L, H), lambda c: (c, 0))
    vec = pl.BlockSpec((1, H), lambda c: (0, 0))
    return pl.pallas_call(body, grid=(S // L,), in_specs=[pl.BlockSpec((L, DT_PAD), lambda c: (c, 0)), vec, vec],
                          out_specs=[blk, blk], out_shape=[jax.ShapeDtypeStruct((S, H), F32)] * 2,
                          compiler_params=_cparams("parallel"), name=name)(dtp, dt_bias, a_log)


def _head_col(blk_ref, h):
    sel = lax.broadcasted_iota(jnp.int32, (1, SSD_HEADS), 1) == h
    return jnp.sum(jnp.where(sel, blk_ref[...], 0.0), axis=1, keepdims=True)


def _ssdg_fwd(xs, Bm, Cm, dt, cs, csT, name):
    H, S, P = xs.shape
    L, N = SSD_CHUNK, SSD_STATE
    nc = S // L
    rep = H // SSD_GROUPS
    hs = range(rep)

    def body(x_ref, b_ref, c_ref, dt_ref, cs_ref, csT_ref, y_ref, prev_ref, st_ref):
        c, g = pl.program_id(0), pl.program_id(1)

        @pl.when(c == 0)
        def _():
            for hh in hs:
                st_ref[g * rep + hh] = jnp.zeros((P, N), F32)

        Bv, Cv = b_ref[...], c_ref[...]
        tril = _tri(L, False, False)
        dtc = [_head_col(dt_ref, g * rep + hh) for hh in hs]
        csc = [_head_col(cs_ref, g * rep + hh) for hh in hs]
        csr = [csT_ref[hh:hh + 1, :] for hh in hs]
        last = [r[:, L - 1:L] for r in csr]
        xc = [x_ref[hh] * dtc[hh] for hh in hs]
        cb = _dot(Cv, Bv, 1, 1)
        m = [cb * jnp.where(tril, jnp.exp(jnp.where(tril, csc[hh] - csr[hh], 0.0)), 0.0) for hh in hs]
        prev = [st_ref[g * rep + hh] for hh in hs]
        yd = [_dot(m[hh], xc[hh], 1, 0) for hh in hs]
        yo = [_dot(Cv, prev[hh], 1, 1) for hh in hs]
        new = [_dot(xc[hh] * jnp.exp(last[hh] - csc[hh]), Bv, 0, 0) for hh in hs]
        for hh in hs:
            y_ref[hh] = yd[hh] + yo[hh] * jnp.exp(csc[hh])
            prev_ref[hh] = prev[hh]
            st_ref[g * rep + hh] = prev[hh] * jnp.exp(last[hh]) + new[hh]

    tok = pl.BlockSpec((L, H), lambda c, g: (c, 0))
    return pl.pallas_call(
        body, grid=(nc, SSD_GROUPS),
        in_specs=[pl.BlockSpec((rep, L, P), lambda c, g: (g, c, 0)), pl.BlockSpec((None, L, N), lambda c, g: (g, c, 0)),
                  pl.BlockSpec((None, L, N), lambda c, g: (g, c, 0)), tok, tok, pl.BlockSpec((rep, L), lambda c, g: (g, c))],
        out_specs=[pl.BlockSpec((rep, L, P), lambda c, g: (g, c, 0)),
                   pl.BlockSpec((rep, None, P, N), lambda c, g: (g, c, 0, 0))],
        out_shape=[jax.ShapeDtypeStruct((H, S, P), F32), jax.ShapeDtypeStruct((H, nc, P, N), F32)],
        scratch_shapes=[pltpu.VMEM((H, P, N), F32)],
        compiler_params=_cparams("arbitrary", "arbitrary"), name=name)(xs, Bm, Cm, dt, cs, csT)


def _ssdg_bwd(xs, Bm, Cm, dt, cs, csT, prev, dy, a_log, d_skip, name):
    H, S, P = xs.shape
    L, N = SSD_CHUNK, SSD_STATE
    nc = S // L
    rep = H // SSD_GROUPS
    hs = range(rep)

    def rowsum(a):
        return jnp.sum(a, axis=1, keepdims=True)

    def body(x_ref, b_ref, c_ref, dt_ref, cs_ref, csT_ref, prev_ref, dy_ref, al_ref, dk_ref,
             dx_ref, db_ref, dc_ref, ddt_ref, da_ref, g_ref):
        ci, g = pl.program_id(0), pl.program_id(1)

        @pl.when(ci == 0)
        def _():
            for hh in hs:
                g_ref[g * rep + hh] = jnp.zeros((P, N), F32)

        @pl.when((ci == 0) & (g == 0))
        def _():
            da_ref[...] = jnp.zeros_like(da_ref)

        @pl.when(g == 0)
        def _():
            ddt_ref[...] = jnp.zeros_like(ddt_ref)

        lane = lax.broadcasted_iota(jnp.int32, (1, H), 1)
        sel = [lane == g * rep + hh for hh in hs]
        A_h = [-jnp.exp(rowsum(jnp.where(s, al_ref[...], 0.0))) for s in sel]
        dsk = [rowsum(jnp.where(s, dk_ref[...], 0.0)) for s in sel]
        dtc = [_head_col(dt_ref, g * rep + hh) for hh in hs]
        csc = [_head_col(cs_ref, g * rep + hh) for hh in hs]
        csr = [csT_ref[hh:hh + 1, :] for hh in hs]
        last = [r[:, L - 1:L] for r in csr]
        Bv, Cv = b_ref[...], c_ref[...]
        xv = [x_ref[hh] for hh in hs]
        xc = [xv[hh] * dtc[hh] for hh in hs]
        dY = [dy_ref[hh] for hh in hs]
        prv = [prev_ref[hh] for hh in hs]
        G = [g_ref[g * rep + hh] for hh in hs]
        ecs = [jnp.exp(v) for v in csc]
        w = [jnp.exp(last[hh] - csc[hh]) for hh in hs]
        cd = [jnp.exp(v) for v in last]
        tril = _tri(L, False, False)
        triu = _tri(L, False, True)
        lam = [jnp.where(tril, jnp.exp(jnp.where(tril, csc[hh] - csr[hh], 0.0)), 0.0) for hh in hs]
        lamT = [jnp.where(triu, jnp.exp(jnp.where(triu, csr[hh] - csc[hh], 0.0)), 0.0) for hh in hs]
        cb = _dot(Cv, Bv, 1, 1)
        bc = _dot(Bv, Cv, 1, 1)
        dM = [_dot(dY[hh], xc[hh], 1, 1) for hh in hs]
        dMT = [_dot(xc[hh], dY[hh], 1, 1) for hh in hs]
        cp = [_dot(Cv, prv[hh], 1, 1) for hh in hs]
        BG = [_dot(Bv, G[hh], 1, 1) for hh in hs]
        dYe = [dY[hh] * ecs[hh] for hh in hs]
        dprev = [_dot(dYe[hh], Cv, 0, 0) for hh in hs]
        m = [cb * lam[hh] for hh in hs]
        mT = [bc * lamT[hh] for hh in hs]
        dxc = [_dot(mT[hh], dY[hh], 1, 0) + w[hh] * BG[hh] for hh in hs]
        dcb = sum([dM[hh] * lam[hh] for hh in hs][1:], dM[0] * lam[0])
        dcbT = sum([dMT[hh] * lamT[hh] for hh in hs][1:], dMT[0] * lamT[0])
        dC = _dot(dcb, Bv, 1, 0)
        dB = _dot(dcbT, Cv, 1, 0)
        for hh in hs:
            dC = dC + _dot(dYe[hh], prv[hh], 1, 0)
            dB = dB + _dot(xc[hh] * w[hh], G[hh], 1, 0)
        dc_ref[...] = dC
        db_ref[...] = dB
        ddt_acc = jnp.zeros((L, H), F32)
        da_acc = jnp.zeros((1, H), F32)
        rev = _tri(L, False, True).astype(F32)
        for hh in hs:
            dww = rowsum(xc[hh] * BG[hh]) * w[hh]
            dcs = (rowsum(dM[hh] * m[hh]) - rowsum(dMT[hh] * mT[hh]) + rowsum(dY[hh] * (cp[hh] * ecs[hh])) - dww)
            extra = jnp.sum(dww, axis=0, keepdims=True) + cd[hh] * jnp.sum(rowsum(G[hh] * prv[hh]), axis=0, keepdims=True)
            g_ref[g * rep + hh] = G[hh] * cd[hh] + dprev[hh]
            da = jnp.dot(rev, dcs, precision=_HI, preferred_element_type=F32) + extra
            dx_ref[hh] = dxc[hh] * dtc[hh] + dY[hh] * dsk[hh]
            ddt_acc = ddt_acc + jnp.where(sel[hh], da * A_h[hh] + rowsum(dxc[hh] * xv[hh]), 0.0)
            da_acc = da_acc + jnp.where(sel[hh], jnp.sum(da * dtc[hh], axis=0, keepdims=True), 0.0)
        ddt_ref[...] += ddt_acc
        da_ref[...] += da_acc

    rc = lambda ci: nc - 1 - ci
    hd = pl.BlockSpec((rep, L, P), lambda ci, g: (g, rc(ci), 0))
    grp = pl.BlockSpec((None, L, N), lambda ci, g: (g, rc(ci), 0))
    tok = pl.BlockSpec((L, H), lambda ci, g: (rc(ci), 0))
    vec = pl.BlockSpec((1, H), lambda ci, g: (0, 0))
    return pl.pallas_call(
        body, grid=(nc, SSD_GROUPS),
        in_specs=[hd, grp, grp, tok, tok, pl.BlockSpec((rep, L), lambda ci, g: (g, rc(ci))),
                  pl.BlockSpec((rep, None, P, N), lambda ci, g: (g, rc(ci), 0, 0)), hd, vec, vec],
        out_specs=[hd, grp, grp, tok, vec],
        out_shape=[jax.ShapeDtypeStruct((H, S, P), F32), jax.ShapeDtypeStruct((SSD_GROUPS, S, N), F32),
                   jax.ShapeDtypeStruct((SSD_GROUPS, S, N), F32), jax.ShapeDtypeStruct((S, H), F32),
                   jax.ShapeDtypeStruct((1, H), F32)],
        scratch_shapes=[pltpu.VMEM((H, P, N), F32)],
        compiler_params=_cparams("arbitrary", "arbitrary"), name=name)(xs, Bm, Cm, dt, cs, csT, prev, dy, a_log, d_skip)


def _dt_bwd(ddt, dA, dtp, dt_bias, a_log, name):
    S, H = ddt.shape
    tm = _rows(S, 512)

    def body(g_ref, da_ref, p_ref, b_ref, al_ref, o_ref, db_ref, dal_ref):
        @pl.when(pl.program_id(0) == 0)
        def _():
            db_ref[...] = jnp.zeros_like(db_ref)
            dal_ref[...] = da_ref[...] * (-jnp.exp(al_ref[...]))

        g = g_ref[...] * _sigmoid(p_ref[:, :H] + b_ref[...])
        db_ref[...] += jnp.sum(g, axis=0, keepdims=True)
        o_ref[...] = jnp.zeros_like(o_ref)
        o_ref[:, :H] = g.astype(o_ref.dtype)

    vec = pl.BlockSpec((1, H), lambda i: (0, 0))
    return pl.pallas_call(
        body, grid=(S // tm,),
        in_specs=[pl.BlockSpec((tm, H), lambda i: (i, 0)), vec, pl.BlockSpec((tm, DT_PAD), lambda i: (i, 0)), vec, vec],
        out_specs=[pl.BlockSpec((tm, DT_PAD), lambda i: (i, 0)), vec, vec],
        out_shape=[jax.ShapeDtypeStruct((S, DT_PAD), _MXU), jax.ShapeDtypeStruct((1, H), F32), jax.ShapeDtypeStruct((1, H), F32)],
        compiler_params=_cparams("arbitrary"), name=name)(ddt, dA, dtp, dt_bias, a_log)


def _ssd_gate_fwd(y, act, z, dskip, w, name):
    S, D = y.shape
    tm = _rows(S, 256)
    Gw = D // SSD_GROUPS

    def body(y_ref, x_ref, z_ref, k_ref, w_ref, o_ref):
        zv = z_ref[...]
        y2 = (y_ref[...] + x_ref[...] * k_ref[...]) * (zv * _sigmoid(zv))
        for g in range(SSD_GROUPS):
            sl = slice(g * Gw, (g + 1) * Gw)
            v = y2[:, sl]
            r = lax.rsqrt(jnp.mean(v * v, axis=-1, keepdims=True) + EPS)
            o_ref[:, sl] = ((v * r) * w_ref[:, sl]).astype(o_ref.dtype)

    row = pl.BlockSpec((tm, D), lambda i: (i, 0))
    vec = pl.BlockSpec((1, D), lambda i: (0, 0))
    return pl.pallas_call(body, grid=(S // tm,), in_specs=[row, row, row, vec, vec], out_specs=row,
                          out_shape=jax.ShapeDtypeStruct((S, D), _MXU), compiler_params=_cparams("parallel"),
                          name=name)(y, act, z, dskip, w)


def _ssd_gate_bwd(dyn, y, act, z, dskip, w, name):
    S, D = y.shape
    tm = _rows(S, 256)
    Gw = D // SSD_GROUPS

    def body(g_ref, y_ref, x_ref, z_ref, k_ref, w_ref, dy_ref, dz_ref, dk_ref, dw_ref):
        @pl.when(pl.program_id(0) == 0)
        def _():
            dk_ref[...] = jnp.zeros_like(dk_ref)
            dw_ref[...] = jnp.zeros_like(dw_ref)

        zv = z_ref[...]
        xv = x_ref[...]
        s = _sigmoid(zv)
        sz = zv * s
        y1 = y_ref[...] + xv * k_ref[...]
        y2 = y1 * sz
        for g in range(SSD_GROUPS):
            sl = slice(g * Gw, (g + 1) * Gw)
            v = y2[:, sl]
            r = lax.rsqrt(jnp.mean(v * v, axis=-1, keepdims=True) + EPS)
            vn = v * r
            gy = g_ref[:, sl].astype(F32)
            dw_ref[:, sl] += jnp.sum(gy * vn, axis=0, keepdims=True)
            dvn = gy * w_ref[:, sl]
            dy2 = r * (dvn - vn * jnp.mean(dvn * vn, axis=-1, keepdims=True))
            dy1 = dy2 * sz[:, sl]
            dy_ref[:, sl] = dy1
            dz_ref[:, sl] = (dy2 * y1[:, sl] * (s[:, sl] * (1.0 + zv[:, sl] * (1.0 - s[:, sl])))).astype(dz_ref.dtype)
            dk_ref[:, sl] += jnp.sum(dy1 * xv[:, sl], axis=0, keepdims=True)

    row = pl.BlockSpec((tm, D), lambda i: (i, 0))
    vec = pl.BlockSpec((1, D), lambda i: (0, 0))
    return pl.pallas_call(
        body, grid=(S // tm,), in_specs=[row, row, row, row, vec, vec], out_specs=[row, row, vec, vec],
        out_shape=[jax.ShapeDtypeStruct((S, D), F32), jax.ShapeDtypeStruct((S, D), _MXU),
                   jax.ShapeDtypeStruct((1, D), F32), jax.ShapeDtypeStruct((1, D), F32)],
        compiler_params=_cparams("arbitrary"), name=name)(dyn, y, act, z, dskip, w)


def _split_dot(v, u):
    hi = v.astype(_MXU)
    lo = (v - hi.astype(F32)).astype(_MXU)
    dn = (((1,), (0,)), ((), ()))
    return (lax.dot_general(hi, u, dn, preferred_element_type=F32) + lax.dot_general(lo, u, dn, preferred_element_type=F32))


def _sb_tiles(S):
    return _pick(S, 256, 128)


SB_LANES = 128
SB_PACK = SB_LANES // SB_HEAD_DIM
SB_ROWS = 128
SB_SCALE = 1.0 / math.sqrt(SB_HEAD_DIM)


def _head_masks():
    lane = lax.broadcasted_iota(jnp.int32, (1, SB_LANES), 1)
    return [(lane // SB_HEAD_DIM) == hh for hh in range(SB_PACK)]


def _by_head(hm, vals):
    out = vals[-1]
    for hh in range(SB_PACK - 2, -1, -1):
        out = jnp.where(hm[hh], vals[hh], out)
    return out


SB_DEAD = -110.0


def _sb_alive(Rs):
    m = Rs[0]
    for R in Rs[1:]:
        m = jnp.maximum(m, R)
    return jnp.max(m) > SB_DEAD


def _sb_rows(a, r):
    return a[r * SB_ROWS:(r + 1) * SB_ROWS]


def _sb_assemble(hm, vals):
    nr = len(vals) // SB_PACK
    return jnp.concatenate([_by_head(hm, vals[r * SB_PACK:(r + 1) * SB_PACK]) for r in range(nr)], axis=0)


def _sb_scores(zs, U, Rs, masks):
    ls = [-jnp.maximum(z, 0.0) - jnp.log(1.0 + jnp.exp(-jnp.abs(z))) for z in zs]
    if masks is not None:
        ls = [jnp.where(m, l, 0.0) for m, l in zip(masks, ls)]
    Es = [lax.dot_general(l.astype(_MXU), U, (((1,), (0,)), ((), ())), preferred_element_type=F32) for l in ls]
    As = [jnp.exp(l + z + (E + R)) for l, z, E, R in zip(ls, zs, Es, Rs)]
    if masks is not None:
        As = [jnp.where(m, A, 0.0) for m, A in zip(masks, As)]
    return ls, [A.astype(_MXU) for A in As]


SB_GROUP = 2


def _sbg_chains(T):
    return [(b, r, hh) for b in range(SB_GROUP) for r in range(T // SB_ROWS) for hh in range(SB_PACK)]


def _lanes(a, b):
    return a[:, b * SB_LANES:(b + 1) * SB_LANES]


def _sbg_join(hm, vals):
    per = len(vals) // SB_GROUP
    return jnp.concatenate([_sb_assemble(hm, vals[b * per:(b + 1) * per]) for b in range(SB_GROUP)], axis=1)


def _sbg_head_sum(hm, a):
    return jnp.concatenate([_by_head(hm, [jnp.sum(jnp.where(m, _lanes(a, b), 0.0), axis=1, keepdims=True) for m in hm])
                            for b in range(SB_GROUP)], axis=1)


def _sbg_fwd(q_arr, k_arr, v_arr, cols, w, name):
    S = q_arr.shape[0]
    T = _sb_tiles(S)
    cq, ck, cv = cols
    GW = SB_GROUP * SB_LANES
    nb = SB_WIDTH // GW

    def body(q_ref, k_ref, v_ref, w_ref, o_ref, y_ref):
        i = pl.program_id(1)
        hm = _head_masks()
        qs = q_ref[...] * SB_SCALE
        chains = _sbg_chains(T)
        qcs = [_sb_rows(jnp.where(hm[hh], _lanes(qs, b), jnp.zeros((T, SB_LANES), qs.dtype)), r) for b, r, hh in chains]
        U = _tri(T, True, False).astype(_MXU)

        def scores_of(j):
            kj = k_ref[pl.ds(pl.multiple_of(j * T, T), T), :]
            return [_dot(qc, _lanes(kj, b), 1, 1) for qc, (b, _, _) in zip(qcs, chains)]

        def weighted(Abs, j):
            vj = v_ref[pl.ds(pl.multiple_of(j * T, T), T), :]
            return _sbg_join(hm, [_dot(Ab, _lanes(vj, b), 1, 0) for Ab, (b, _, _) in zip(Abs, chains)])

        def step(carry):
            jj, acc, Rs, Aprev = carry
            j = i - 1 - jj
            zs = scores_of(j)
            acc = acc + weighted(Aprev, j + 1)
            ls, Abs = _sb_scores(zs, U, Rs, None)
            return jj + 1, acc, tuple(R + jnp.sum(l, axis=1, keepdims=True) for R, l in zip(Rs, ls)), tuple(Abs)

        masks = [_sb_rows(_tri(T, True, False), r) for _, r, _ in chains]
        zero = jnp.zeros((SB_ROWS, 1), F32)
        ls, Abs = _sb_scores(scores_of(i), U, (zero,) * len(chains), masks)
        carry = (jnp.int32(0), jnp.zeros((T, GW), F32), tuple(jnp.sum(l, axis=1, keepdims=True) for l in ls), tuple(Abs))
        jj, acc, _, Alast = lax.while_loop(lambda c: (c[0] < i) & _sb_alive(c[2]), step, carry)
        acc = acc + weighted(Alast, i - jj)
        o_ref[...] = acc
        r = lax.rsqrt(_sbg_head_sum(hm, acc * acc) * (1.0 / SB_HEAD_DIM) + EPS)
        y_ref[...] = ((acc * r) * w_ref[...]).astype(y_ref.dtype)

    blk = pl.BlockSpec((T, GW), lambda h, i: (i, h))
    return pl.pallas_call(
        body, grid=(nb, S // T),
        in_specs=[pl.BlockSpec((T, GW), lambda h, i: (i, cq + h)), pl.BlockSpec((S, GW), lambda h, i: (0, ck + h), pipeline_mode=pl.Buffered(1)),
                  pl.BlockSpec((S, GW), lambda h, i: (0, cv + h), pipeline_mode=pl.Buffered(1)), pl.BlockSpec((1, GW), lambda h, i: (0, h))],
        out_specs=[blk, blk], out_shape=[jax.ShapeDtypeStruct((S, SB_WIDTH), F32), jax.ShapeDtypeStruct((S, SB_WIDTH), _MXU)],
        compiler_params=_cparams("parallel", "parallel"), name=name)(q_arr, k_arr, v_arr, w)


def _sbg_bwd(q_arr, k_arr, v_arr, cols, o, dy_arr, cdy, w, name):
    S = q_arr.shape[0]
    T = _sb_tiles(S)
    cq, ck, cv = cols
    GW = SB_GROUP * SB_LANES
    nb = SB_WIDTH // GW

    def body(q_ref, k_ref, v_ref, o_ref, dy_ref, w_ref, dq_ref, dk_ref, dv_ref, dw_ref):
        i = pl.program_id(1)

        @pl.when(i == 0)
        def _():
            dk_ref[...] = jnp.zeros_like(dk_ref)
            dv_ref[...] = jnp.zeros_like(dv_ref)
            dw_ref[...] = jnp.zeros_like(dw_ref)

        hm = _head_masks()
        chains = _sbg_chains(T)
        qs = q_ref[...] * SB_SCALE
        ov = o_ref[...]
        gy = dy_ref[...]
        r = lax.rsqrt(_sbg_head_sum(hm, ov * ov) * (1.0 / SB_HEAD_DIM) + EPS)
        on = ov * r
        dw_ref[...] += jnp.sum(gy * on, axis=0, keepdims=True)
        don = gy * w_ref[...]
        do = r * (don - on * (_sbg_head_sum(hm, don * on) * (1.0 / SB_HEAD_DIM)))
        dob = do.astype(_MXU)
        dprod = dob.astype(F32) * ov
        zt = jnp.zeros((T, SB_LANES), dob.dtype)
        qm = [[jnp.where(hm[hh], _lanes(qs, b), zt) for hh in range(SB_PACK)] for b in range(SB_GROUP)]
        dm = [[jnp.where(hm[hh], _lanes(dob, b), zt) for hh in range(SB_PACK)] for b in range(SB_GROUP)]
        qcs = [_sb_rows(qm[b][hh], r_) for b, r_, hh in chains]
        doc = [_sb_rows(dm[b][hh], r_) for b, r_, hh in chains]
        Dt = [_sb_rows(jnp.sum(jnp.where(hm[hh], _lanes(dprod, b), 0.0), axis=1, keepdims=True), r_) for b, r_, hh in chains]
        U = _tri(T, True, False).astype(_MXU)
        Ui = _tri(T, False, False).astype(_MXU)

        def products_of(j):
            off = pl.multiple_of(j * T, T)
            kj = k_ref[pl.ds(off, T), :]
            vj = v_ref[pl.ds(off, T), :]
            return ([_dot(qc, _lanes(kj, b), 1, 1) for qc, (b, _, _) in zip(qcs, chains)],
                    [_dot(d, _lanes(vj, b), 1, 1) for d, (b, _, _) in zip(doc, chains)])

        def core(zs, dAs, Rs, Qs, masks):
            ls, Abs = _sb_scores(zs, U, Rs, masks)
            Gs = [dA * Ab.astype(F32) for dA, Ab in zip(dAs, Abs)]
            sfx = [_split_dot(G, Ui) for G in Gs]
            dzs = []
            for c, (l, G, s, D, Q) in enumerate(zip(ls, Gs, sfx, Dt, Qs)):
                P = D - (s + Q)
                dz = jnp.exp(l) * (G + P) - P
                if masks is not None:
                    dz = jnp.where(masks[c], dz, 0.0)
                dzs.append(dz.astype(_MXU))
            newR = tuple(R + jnp.sum(l, axis=1, keepdims=True) for R, l in zip(Rs, ls))
            newQ = tuple(Q + jnp.sum(G, axis=1, keepdims=True) for Q, G in zip(Qs, Gs))
            return tuple(Abs), tuple(dzs), newR, newQ

        def over_rows(vals, other):
            nr = T // SB_ROWS
            tiles = []
            for b in range(SB_GROUP):
                acc = None
                for hh in range(SB_PACK):
                    rows = jnp.concatenate([vals[(b * nr + r_) * SB_PACK + hh] for r_ in range(nr)], axis=0)
                    part = _dot(rows, other[b][hh], 0, 0)
                    acc = part if acc is None else acc + part
                tiles.append(acc)
            return jnp.concatenate(tiles, axis=1)

        def emit(Abs, dzs, j):
            off = pl.multiple_of(j * T, T)
            kj = k_ref[pl.ds(off, T), :]
            dk_ref[pl.ds(off, T), :] += over_rows(dzs, qm)
            dv_ref[pl.ds(off, T), :] += over_rows(Abs, dm)
            return _sbg_join(hm, [_dot(dzb, _lanes(kj, b), 1, 0) for dzb, (b, _, _) in zip(dzs, chains)])

        def step(carry):
            jj, dq, Rs, Qs, Aprev, dzprev = carry
            j = i - 1 - jj
            zs, dAs = products_of(j)
            dq = dq + emit(Aprev, dzprev, j + 1)
            Abs, dzs, Rs, Qs = core(zs, dAs, Rs, Qs, None)
            return jj + 1, dq, Rs, Qs, Abs, dzs

        masks = [_sb_rows(_tri(T, True, False), r_) for _, r_, _ in chains]
        zero = (jnp.zeros((SB_ROWS, 1), F32),) * len(chains)
        zs, dAs = products_of(i)
        Abs, dzs, Rs, Qs = core(zs, dAs, zero, zero, masks)
        jj, dq, _, _, Alast, dzlast = lax.while_loop(lambda c: (c[0] < i) & _sb_alive(c[2]), step,
                                                     (jnp.int32(0), jnp.zeros((T, GW), F32), Rs, Qs, Abs, dzs))
        dq = dq + emit(Alast, dzlast, i - jj)
        dq_ref[...] = (dq * SB_SCALE).astype(dq_ref.dtype)

    blk = pl.BlockSpec((T, GW), lambda h, i: (i, h))
    full = pl.BlockSpec((S, GW), lambda h, i: (0, h), pipeline_mode=pl.Buffered(1))
    wsp = pl.BlockSpec((1, GW), lambda h, i: (0, h))
    return pl.pallas_call(
        body, grid=(nb, S // T),
        in_specs=[pl.BlockSpec((T, GW), lambda h, i: (i, cq + h)), pl.BlockSpec((S, GW), lambda h, i: (0, ck + h), pipeline_mode=pl.Buffered(1)),
                  pl.BlockSpec((S, GW), lambda h, i: (0, cv + h), pipeline_mode=pl.Buffered(1)), blk,
                  pl.BlockSpec((T, GW), lambda h, i: (i, cdy + h)), wsp],
        out_specs=[blk, full, full, wsp],
        out_shape=[jax.ShapeDtypeStruct((S, SB_WIDTH), _MXU), jax.ShapeDtypeStruct((S, SB_WIDTH), F32),
                   jax.ShapeDtypeStruct((S, SB_WIDTH), F32), jax.ShapeDtypeStruct((1, SB_WIDTH), F32)],
        compiler_params=_cparams("parallel", "arbitrary"), name=name)(q_arr, k_arr, v_arr, o, dy_arr, w)


def _adamw(w, g, m, v, name):
    R, C = w.shape
    tm = _rows(R, 256) if R % 8 == 0 else R
    c1 = 1.0 - ADAM_B1 ** ADAM_STEP
    c2 = 1.0 - ADAM_B2 ** ADAM_STEP

    def body(w_ref, g_ref, m_ref, v_ref, d_ref, nm_ref, nv_ref):
        gv = g_ref[...]
        mn = ADAM_B1 * m_ref[...] + (1.0 - ADAM_B1) * gv
        vn = ADAM_B2 * v_ref[...] + (1.0 - ADAM_B2) * (gv * gv)
        d_ref[...] = -ADAM_LR * ((mn / c1) / (jnp.sqrt(vn / c2) + ADAM_EPS) + ADAM_WD * w_ref[...])
        nm_ref[...] = mn
        nv_ref[...] = vn

    blk = pl.BlockSpec((tm, C), lambda i: (i, 0))
    return pl.pallas_call(body, grid=(R // tm,), in_specs=[blk] * 4, out_specs=[blk] * 3,
                          out_shape=[jax.ShapeDtypeStruct((R, C), F32)] * 3, compiler_params=_cparams("parallel"),
                          name=name)(w, g, m, v)


def _sum_lead(a, name, first=None, wire=False):
    n, R, C = a.shape
    tm = _rows(R, 256)
    nin = 1 if first is None else 2

    def body(*refs):
        a_ref = refs[nin - 1]
        s = a_ref[0].astype(F32) if first is None else refs[0][...] + a_ref[0]
        for p in range(1, n):
            s = s + a_ref[p]
        for o_ref in refs[nin:]:
            o_ref[...] = s.astype(o_ref.dtype)

    row = pl.BlockSpec((tm, C), lambda i: (i, 0))
    ins = ([] if first is None else [first]) + [a]
    outs = [jax.ShapeDtypeStruct((R, C), F32)] + ([jax.ShapeDtypeStruct((R, C), _WIRE)] if wire else [])
    res = pl.pallas_call(body, grid=(R // tm,), in_specs=[row] * (nin - 1) + [pl.BlockSpec((n, tm, C), lambda i: (0, i, 0))],
                         out_specs=[row] * len(outs), out_shape=outs, compiler_params=_cparams("parallel"), name=name)(*ins)
    return res if wire else res[0]


_GROUP_BITS = {'c': ((0, 0, 1),), 'xy': ((0, 1, 0), (1, 0, 0), (1, 1, 0)),
               'xyc': tuple((k >> 2 & 1, k >> 1 & 1, k & 1) for k in range(1, 8))}


def _exchange(srcs, *, group, same_src, own, chunks, name):
    flips = _GROUP_BITS[group]
    n = len(flips) + 1
    na = len(srcs)
    blk_shapes = [tuple(s.shape) if same_src else tuple(s.shape[1:]) for s in srcs]
    assert all(bs[0] % chunks == 0 for bs in blk_shapes), blk_shapes

    def body(*refs):
        src_refs, dst_refs = refs[:na], refs[na:2 * na]
        send_sems, recv_sems, loc_sems = refs[2 * na:]
        x, y, c = lax.axis_index("x"), lax.axis_index("y"), lax.axis_index("c")

        def member(px, py, pc):
            return {'c': pc, 'xy': 2 * px + py, 'xyc': 4 * px + 2 * py + pc}[group]

        def piece(ref, a, q):
            rows = blk_shapes[a][0] // chunks
            return ref.at[pl.ds(q * rows, rows)]

        me = member(x, y, c)
        started, arrivals = [], []
        for a in range(na):
            mine = src_refs[a] if same_src else src_refs[a].at[me]
            if own:
                for q in range(chunks):
                    cp = pltpu.make_async_copy(piece(mine, a, q), piece(dst_refs[a].at[me], a, q), loc_sems.at[a * chunks + q])
                    cp.start()
                    started.append(cp.wait)
            for kk, (fx, fy, fc) in enumerate(flips):
                px, py, pc = (1 - x if fx else x), (1 - y if fy else y), (1 - c if fc else c)
                peer = member(px, py, pc)
                out_blk = src_refs[a] if same_src else src_refs[a].at[peer]
                there = dst_refs[a].at[me if own else kk]
                here = dst_refs[a].at[peer if own else kk]
                for q in range(chunks):
                    s = (a * (n - 1) + kk) * chunks + q
                    out = pltpu.make_async_remote_copy(
                        src_ref=piece(out_blk, a, q), dst_ref=piece(there, a, q), send_sem=send_sems.at[s],
                        recv_sem=recv_sems.at[s], device_id=(px, py, pc), device_id_type=pl.DeviceIdType.MESH)
                    out.start()
                    started.append(out.wait_send)
                    arrivals.append(pltpu.make_async_remote_copy(
                        src_ref=piece(mine, a, q), dst_ref=piece(here, a, q), send_sem=send_sems.at[s],
                        recv_sem=recv_sems.at[s], device_id=(px, py, pc), device_id_type=pl.DeviceIdType.MESH).wait_recv)
        for wait in arrivals + started:
            wait()

    nsem = na * (n - 1) * chunks
    hbm = pl.BlockSpec(memory_space=pl.ANY)
    return pl.pallas_call(
        body, in_specs=[hbm] * na, out_specs=[hbm] * na,
        out_shape=[jax.ShapeDtypeStruct(((n if own else n - 1),) + bs, s.dtype) for bs, s in zip(blk_shapes, srcs)],
        scratch_shapes=[pltpu.SemaphoreType.DMA((nsem,)), pltpu.SemaphoreType.DMA((nsem,)),
                        pltpu.SemaphoreType.DMA((na * chunks,))],
        compiler_params=pltpu.CompilerParams(has_side_effects=True), name=name)(*srcs)


def _to_shards(name, full):
    R, C = full.shape
    if name in COL_SPLIT:
        return full.reshape(R, 4, C // 4).transpose(1, 0, 2)
    return full.reshape(4, R // 4, C)


def _from_shards(name, sh):
    n, R, C = sh.shape
    if name in COL_SPLIT:
        return sh.transpose(1, 0, 2).reshape(R, n * C)
    return sh.reshape(n * R, C)


def _pack_rows(parts, width, rows):
    n = parts[0].shape[0]
    flat = jnp.concatenate([p.reshape(n, -1) for p in parts], axis=1)
    return jnp.pad(flat, ((0, 0), (0, rows * width - flat.shape[1]))).reshape(n, rows, width)


def _unpack_rows(buf, shapes):
    n = buf.shape[0]
    flat = buf.reshape(n, -1)
    out, o = [], 0
    for s in shapes:
        sz = math.prod(s)
        out.append(flat[:, o:o + sz].reshape((n,) + tuple(s)))
        o += sz
    return out


def _split_rows(a, rows):
    out, o = [], 0
    for r in rows:
        out.append(a[:, o:o + r])
        o += r
    return out


def _ceil_to(v, m):
    return -(-v // m) * m


def kernel(x, mem, norm_mix_w, w_in, conv_ssd_w, conv_ssd_b, dt_bias, a_log, d_skip, ssd_norm_w, sb_norm_w, w_out, norm_mem_w, norm_memkv_w, w_mq, w_mk, w_mv, w_mo, norm_ffn_w, w_up, conv_ffn_w, conv_ffn_b, w_down, norm_final_w, loss_target, m_norm_mix_w, m_w_in, m_conv_ssd_w, m_conv_ssd_b, m_dt_bias, m_a_log, m_d_skip, m_ssd_norm_w, m_sb_norm_w, m_w_out, m_norm_mem_w, m_norm_memkv_w, m_w_mq, m_w_mk, m_w_mv, m_w_mo, m_norm_ffn_w, m_w_up, m_conv_ffn_w, m_conv_ffn_b, m_w_down, m_norm_final_w, v_norm_mix_w, v_w_in, v_conv_ssd_w, v_conv_ssd_b, v_dt_bias, v_a_log, v_d_skip, v_ssd_norm_w, v_sb_norm_w, v_w_out, v_norm_mem_w, v_norm_memkv_w, v_w_mq, v_w_mk, v_w_mv, v_w_mo, v_norm_ffn_w, v_w_up, v_conv_ffn_w, v_conv_ffn_b, v_w_down, v_norm_final_w):
    W = dict(norm_mix_w=norm_mix_w, w_in=w_in, conv_ssd_w=conv_ssd_w, conv_ssd_b=conv_ssd_b, dt_bias=dt_bias, a_log=a_log,
             d_skip=d_skip, ssd_norm_w=ssd_norm_w, sb_norm_w=sb_norm_w, w_out=w_out, norm_mem_w=norm_mem_w,
             norm_memkv_w=norm_memkv_w, w_mq=w_mq, w_mk=w_mk, w_mv=w_mv, w_mo=w_mo, norm_ffn_w=norm_ffn_w, w_up=w_up,
             conv_ffn_w=conv_ffn_w, conv_ffn_b=conv_ffn_b, w_down=w_down, norm_final_w=norm_final_w)
    Mo = dict(norm_mix_w=m_norm_mix_w, w_in=m_w_in, conv_ssd_w=m_conv_ssd_w, conv_ssd_b=m_conv_ssd_b, dt_bias=m_dt_bias,
              a_log=m_a_log, d_skip=m_d_skip, ssd_norm_w=m_ssd_norm_w, sb_norm_w=m_sb_norm_w, w_out=m_w_out,
              norm_mem_w=m_norm_mem_w, norm_memkv_w=m_norm_memkv_w, w_mq=m_w_mq, w_mk=m_w_mk, w_mv=m_w_mv, w_mo=m_w_mo,
              norm_ffn_w=m_norm_ffn_w, w_up=m_w_up, conv_ffn_w=m_conv_ffn_w, conv_ffn_b=m_conv_ffn_b, w_down=m_w_down,
              norm_final_w=m_norm_final_w)
    Vo = dict(norm_mix_w=v_norm_mix_w, w_in=v_w_in, conv_ssd_w=v_conv_ssd_w, conv_ssd_b=v_conv_ssd_b, dt_bias=v_dt_bias,
              a_log=v_a_log, d_skip=v_d_skip, ssd_norm_w=v_ssd_norm_w, sb_norm_w=v_sb_norm_w, w_out=v_w_out,
              norm_mem_w=v_norm_mem_w, norm_memkv_w=v_norm_memkv_w, w_mq=v_w_mq, w_mk=v_w_mk, w_mv=v_w_mv, w_mo=v_w_mo,
              norm_ffn_w=v_norm_ffn_w, w_up=v_w_up, conv_ffn_w=v_conv_ffn_w, conv_ffn_b=v_conv_ffn_b, w_down=v_w_down,
              norm_final_w=v_norm_final_w)
    shapes = {n: W[n].shape for n in WEIGHTS}
    sh2 = {n: (1, a.shape[-1]) if a.ndim < 3 else a.shape[-2:] for n, a in W.items()}
    w2 = {n: W[n].reshape(sh2[n]) for n in WEIGHTS}
    x2d = x[0]
    S, D = x2d.shape
    H, P, N = SSD_HEADS, SSD_HEAD_DIM, SSD_STATE

    cv_rows = _ceil_to(-(-sum(math.prod(sh2[n]) for n in CONVW) // 128), 32)
    cpack = _pack_rows([w2[n][None] for n in CONVW], 128, cv_rows)[0]
    stacked = jnp.concatenate([w2[n].astype(_MXU) for n in ROW_SPLIT], axis=0)
    g_rows, g_in, g_up, call = _exchange([stacked, w2['w_in'].astype(_MXU), w2['w_up'].astype(_MXU), cpack], group='xy',
                                         same_src=True, own=True, chunks=4, name="gather_weights")
    full = {'w_in': _from_shards('w_in', g_in), 'w_up': _from_shards('w_up', g_up)}
    full.update({n: _from_shards(n, a) for n, a in zip(ROW_SPLIT, _split_rows(g_rows, [sh2[n][0] for n in ROW_SPLIT]))})
    full.update({n: _from_shards(n, a) for n, a in zip(CONVW, _unpack_rows(call, [sh2[n] for n in CONVW]))})

    o1 = SSD_INNER
    o2 = o1 + SSD_XBC
    o3 = o2 + SSD_HEADS
    Wi = full['w_in']
    W_z, W_xbc, W_qkv = Wi[:, :o1], Wi[:, o1:o2], Wi[:, o3:]
    W_dt = jnp.pad(Wi[:, o2:o3], ((0, 0), (0, DT_PAD - SSD_HEADS)))
    W_in_r = jnp.concatenate([W_z, W_xbc, W_qkv, W_dt], axis=1)
    dskip_rep = jnp.repeat(w2['d_skip'], P, axis=1)

    def heads(a, nh):
        return a.reshape(S, nh, a.shape[1] // nh).transpose(1, 0, 2)

    def unheads(a):
        return a.transpose(1, 0, 2).reshape(S, a.shape[0] * a.shape[2])

    h1 = _rms_fwd(x2d, w2['norm_mix_w'], "norm_mix")
    z = _mm(h1, W_z, name="proj_z")
    xbc = _mm(h1, W_xbc, name="proj_xbc")
    dtp = _mm(h1, W_dt, name="proj_dt")
    qkv = _mm(h1, W_qkv, out_dtype=_MXU, name="proj_qkv")
    pre = _dwconv_fwd(xbc, full['conv_ssd_w'], w2['conv_ssd_b'], "ssd_conv")
    act = _silu_fwd(pre, "ssd_conv_silu")
    dt, cs = _ssd_prep(dtp, w2['dt_bias'], w2['a_log'], "ssd_prep")
    csT = cs.T
    xs_h = heads(act[:, :o1], H)
    Bm = heads(act[:, o1:o1 + SSD_GROUPS * N], SSD_GROUPS)
    Cm = heads(act[:, o1 + SSD_GROUPS * N:], SSD_GROUPS)
    y_h, prev = _ssdg_fwd(xs_h, Bm, Cm, dt, cs, csT, "ssd_scan")
    y_scan = unheads(y_h)
    xs = act[:, :o1]
    y_ssd = _ssd_gate_fwd(y_scan, xs, z, dskip_rep, w2['ssd_norm_w'], "ssd_gate")
    nsb = SB_WIDTH // (SB_GROUP * SB_LANES)
    qkv_cols = (0, nsb, 2 * nsb)
    o_sb, y_sb = _sbg_fwd(qkv, qkv, qkv, qkv_cols, w2['sb_norm_w'], "sb_attn")
    ycat = jnp.concatenate([y_ssd, y_sb], axis=1)
    x_2 = _mm(ycat, full['w_out'], res=x2d, name="out_proj")
    h2 = _rms_fwd(x_2, w2['norm_mem_w'], "norm_mem")
    qm = _mm(h2, full['w_mq'], out_dtype=_MXU, name="mem_q")
    mn = _rms_fwd(mem[0], w2['norm_memkv_w'], "norm_memkv")
    km = _mm(mn, full['w_mk'], out_dtype=_MXU, name="mem_k")
    vm = _mm(mn, full['w_mv'], out_dtype=_MXU, name="mem_v")
    om = _xattn_fwd(qm, km, vm, "mem_attn")
    x_3 = _mm(om, full['w_mo'], res=x_2, name="mem_o")
    h3 = _rms_fwd(x_3, w2['norm_ffn_w'], "norm_ffn")
    up = _mm(h3, full['w_up'], name="ffn_up")
    u = _dwconv_fwd(up, full['conv_ffn_w'], w2['conv_ffn_b'], "ffn_conv")
    a_ffn = _glu_fwd(u, "ffn_glu")
    x_4 = _mm(a_ffn, full['w_down'], res=x_3, name="ffn_down")
    dx4, dx4b, g_final, loss_blk = _loss_bwd(x_4, loss_target[0], w2['norm_final_w'], "loss_head")

    G = {'norm_final_w': g_final}
    dact = _mm(dx4b, full['w_down'], tb=True, name="d_ffn_act")
    G['w_down'] = _mm(a_ffn, dx4b, ta=True, name="g_w_down")
    du = _glu_bwd(u, dact, "d_ffn_glu")
    G['conv_ffn_w'], G['conv_ffn_b'] = _conv_bwd_w(up, du, full['conv_ffn_w'].shape[0], "g_ffn_conv")
    dup = _dwconv_bwd_x(du, full['conv_ffn_w'], "d_ffn_conv")
    dh3 = _mm(dup, full['w_up'], tb=True, name="d_h3")
    G['w_up'] = _mm(h3, dup, ta=True, name="g_w_up")
    dx3, dx3b, G['norm_ffn_w'] = _rms_bwd(dh3, x_3, w2['norm_ffn_w'], dx4, "d_norm_ffn")
    dom = _mm(dx3b, full['w_mo'], tb=True, out_dtype=_MXU, name="d_mem_o")
    G['w_mo'] = _mm(om, dx3b, ta=True, name="g_w_mo")
    dqm, dkm, dvm = _xattn_bwd(qm, km, vm, dom, "d_mem_attn")
    G['w_mq'] = _mm(h2, dqm, ta=True, name="g_w_mq")
    dh2 = _mm(dqm, full['w_mq'], tb=True, name="d_h2")
    dx2, dx2b, G['norm_mem_w'] = _rms_bwd(dh2, x_2, w2['norm_mem_w'], dx3, "d_norm_mem")
    G['w_mk'] = _mm(mn, dkm, ta=True, name="g_w_mk")
    G['w_mv'] = _mm(mn, dvm, ta=True, name="g_w_mv")
    dmn = _mm(dvm, full['w_mv'], tb=True, res=_mm(dkm, full['w_mk'], tb=True, name="d_mn_k"), name="d_mn_v")
    _, _, G['norm_memkv_w'] = _rms_bwd(dmn, mem[0], w2['norm_memkv_w'], None, "d_norm_memkv")
    dycat = _mm(dx2b, full['w_out'], tb=True, name="d_ycat")
    G['w_out'] = _mm(ycat, dx2b, ta=True, name="g_w_out")
    dy1, dz, g_dskip_lane, G['ssd_norm_w'] = _ssd_gate_bwd(dycat[:, :o1], y_scan, xs, z, dskip_rep, w2['ssd_norm_w'], "d_ssd_gate")
    dxs_h, dB, dC, ddt, dA = _ssdg_bwd(xs_h, Bm, Cm, dt, cs, csT, prev, heads(dy1, H), w2['a_log'], w2['d_skip'], "d_ssd_scan")
    G['d_skip'] = jnp.sum(g_dskip_lane.reshape(H, P), axis=1)[None, :]
    dact_xbc = jnp.concatenate([unheads(dxs_h), unheads(dB), unheads(dC)], axis=1)
    dpre = _silu_bwd(pre, dact_xbc, "d_ssd_conv_silu")
    G['conv_ssd_w'], G['conv_ssd_b'] = _conv_bwd_w(xbc, dpre, full['conv_ssd_w'].shape[0], "g_ssd_conv")
    dxbc = _dwconv_bwd_x(dpre, full['conv_ssd_w'], "d_ssd_conv")
    ddtp, G['dt_bias'], G['a_log'] = _dt_bwd(ddt, dA, dtp, w2['dt_bias'], w2['a_log'], "d_dt")
    dq, dk, dv, G['sb_norm_w'] = _sbg_bwd(qkv, qkv, qkv, qkv_cols, o_sb, dycat, o1 // (SB_GROUP * SB_LANES), w2['sb_norm_w'], "d_sb_attn")
    dproj = jnp.concatenate([dz, dxbc, dq, dk.astype(_MXU), dv.astype(_MXU), ddtp], axis=1)
    dh1 = _mm(dproj, W_in_r, tb=True, name="d_h1")
    g_in_r = _mm(h1, dproj, ta=True, name="g_w_in")
    nq = 3 * SB_WIDTH
    G['w_in'] = jnp.concatenate([g_in_r[:, :o2], g_in_r[:, o2 + nq:o2 + nq + SSD_HEADS], g_in_r[:, o2:o2 + nq]], axis=1)
    grad_x, _, G['norm_mix_w'] = _rms_bwd(dh1, x2d, w2['norm_mix_w'], dx2, "d_norm_mix")

    cidx = lax.axis_index("c")
    oidx = 2 * lax.axis_index("x") + lax.axis_index("y")
    by_owner = [jnp.concatenate([_to_shards(n, G[n]) for n in ROW_SPLIT], axis=1), _to_shards('w_in', G['w_in']),
                _to_shards('w_up', G['w_up'])]
    to_pair = [a.reshape(4, 2, a.shape[1] // 2, a.shape[2]).transpose(1, 0, 2, 3) for a in by_owner]
    got = _exchange(to_pair, group='c', same_src=False, own=False, chunks=4, name="reduce_pair")
    pair, pair_wire = [], []
    for t, g in zip(to_pair, got):
        _, _, r, cw = t.shape
        mine = lax.dynamic_index_in_dim(t, cidx, 0, keepdims=False).reshape(4 * r, cw)
        full_sum, wire_sum = _sum_lead(g.reshape(1, 4 * r, cw), "reduce_pair_sum%d" % len(pair), first=mine, wire=True)
        pair.append(full_sum.reshape(4, r, cw))
        pair_wire.append(wire_sum.reshape(4, r, cw))
    got = _exchange(pair_wire, group='xy', same_src=False, own=False, chunks=1, name="reduce_chips")
    chips = [_sum_lead(g, "reduce_chips_sum%d" % k, first=lax.dynamic_index_in_dim(p, oidx, 0, keepdims=False))
             for k, (p, g) in enumerate(zip(pair, got))]
    got = _exchange(chips, group='c', same_src=True, own=False, chunks=4, name="share_pair")
    red = [jnp.where(cidx == 0, jnp.concatenate([m, g[0]], axis=0), jnp.concatenate([g[0], m], axis=0))[None]
           for m, g in zip(chips, got)]
    gsh = dict(zip(ROW_SPLIT, [a[0] for a in _split_rows(red[0], [sh2[n][0] for n in ROW_SPLIT])]))
    gsh['w_in'], gsh['w_up'] = red[1][0], red[2][0]

    small_parts = [G[n].reshape(1, -1) for n in SMALL + CONVW] + [loss_blk[:1, :1]]
    small_shapes = [sh2[n] for n in SMALL] + [G[n].shape for n in CONVW] + [(1, 1)]
    small_rows = _ceil_to(-(-sum(math.prod(s) for s in small_shapes) // 128), 8)
    spack = _pack_rows(small_parts, 128, small_rows)[0]
    (gathered,) = _exchange([spack], group='xyc', same_src=True, own=True, chunks=1, name="gather_small")
    parts = [a[0] for a in _unpack_rows(_sum_lead(gathered, "small_sum")[None], small_shapes)]
    gsh.update(zip(SMALL, parts))
    for n, a in zip(CONVW, parts[len(SMALL):-1]):
        gsh[n] = lax.dynamic_index_in_dim(_to_shards(n, a), oidx, 0, keepdims=False)
    loss = parts[-1].reshape(())

    delta, new_m, new_v = {}, {}, {}
    for n in BIG:
        delta[n], new_m[n], new_v[n] = _adamw(w2[n], gsh[n], Mo[n].reshape(sh2[n]), Vo[n].reshape(sh2[n]), "adamw_" + n)
    for grp, width, tag in ((CONVW, 128, "adamw_conv"), (SMALL, 128, "adamw_small")):
        rows = _ceil_to(-(-sum(math.prod(sh2[n]) for n in grp) // width), 8)
        packed = [_pack_rows([src[n].reshape(1, -1) for n in grp], width, rows)[0]
                  for src in (w2, gsh, {n: Mo[n] for n in grp}, {n: Vo[n] for n in grp})]
        outs = _adamw(*packed, tag)
        for dst, o in zip((delta, new_m, new_v), outs):
            dst.update(zip(grp, [a[0] for a in _unpack_rows(o[None], [sh2[n] for n in grp])]))

    def shaped(d):
        return [d[n].reshape(shapes[n]) for n in WEIGHTS]

    return (loss, grad_x[None], *shaped(gsh), *shaped(delta), *shaped(new_m), *shaped(new_v))
```

```python
import math

import jax
import jax.numpy as jnp
from jax import lax
from jax.experimental import pallas as pl
from jax.experimental.pallas import tpu as pltpu

F32 = jnp.float32
_MXU = jnp.bfloat16
_WIRE = jnp.bfloat16
EPS = 1e-6
_VMEM_LIMIT = 48 * 1024 * 1024
_HI = lax.Precision.HIGHEST

SSD_HEADS = 16
SSD_HEAD_DIM = 64
SSD_GROUPS = 2
SSD_STATE = 128
SSD_CHUNK = 128
SSD_INNER = SSD_HEADS * SSD_HEAD_DIM
SSD_XBC = SSD_INNER + 2 * SSD_GROUPS * SSD_STATE
SB_HEADS = 16
SB_HEAD_DIM = 64
SB_WIDTH = SB_HEADS * SB_HEAD_DIM
MEM_HEADS = 4
DT_PAD = 128

ADAM_LR = 0.001
ADAM_B1 = 0.9
ADAM_B2 = 0.999
ADAM_EPS = 1e-08
ADAM_WD = 0.01
ADAM_STEP = 10

WEIGHTS = ['norm_mix_w', 'w_in', 'conv_ssd_w', 'conv_ssd_b', 'dt_bias', 'a_log', 'd_skip', 'ssd_norm_w',
           'sb_norm_w', 'w_out', 'norm_mem_w', 'norm_memkv_w', 'w_mq', 'w_mk', 'w_mv', 'w_mo', 'norm_ffn_w',
           'w_up', 'conv_ffn_w', 'conv_ffn_b', 'w_down', 'norm_final_w']
BIG = ['w_in', 'w_out', 'w_mq', 'w_mk', 'w_mv', 'w_mo', 'w_up', 'w_down']
COL_SPLIT = ('w_in', 'w_up', 'conv_ssd_w', 'conv_ffn_w')
ROW_SPLIT = ['w_out', 'w_mq', 'w_mk', 'w_mv', 'w_mo', 'w_down']
CONVW = ['conv_ssd_w', 'conv_ffn_w']
SMALL = ['norm_mix_w', 'conv_ssd_b', 'dt_bias', 'a_log', 'd_skip', 'ssd_norm_w', 'sb_norm_w', 'norm_mem_w',
         'norm_memkv_w', 'norm_ffn_w', 'conv_ffn_b', 'norm_final_w']


def _cparams(*sem):
    return pltpu.CompilerParams(dimension_semantics=sem if sem else None, vmem_limit_bytes=_VMEM_LIMIT)


def _pick(n, cap, mult=128):
    best = None
    for d in range(mult, min(n, cap) + 1, mult):
        if n % d == 0:
            best = d
    return n if best is None else best


def _dot(a, b, ca, cb):
    return lax.dot_general(a.astype(_MXU), b.astype(_MXU), (((ca,), (cb,)), ((), ())), preferred_element_type=F32)


def _sigmoid(v):
    return 1.0 / (1.0 + jnp.exp(-v))


def _log1p(u):
    w = 1.0 + u
    return jnp.where(w == 1.0, u, jnp.log(w) * (u / (w - 1.0)))


def _mm(a, b, *, ta=False, tb=False, res=None, out_dtype=F32, name):
    if ta:
        K, M = a.shape
    else:
        M, K = a.shape
    if tb:
        N, K2 = b.shape
    else:
        K2, N = b.shape
    assert K == K2, (a.shape, b.shape)
    tm = _pick(M, 1408, 128 if ta else 16)
    tn = _pick(N, 1536)
    tk = _pick(K, 1536)
    nk = K // tk
    a_spec = pl.BlockSpec((tk, tm), lambda i, j, k: (k, i)) if ta else pl.BlockSpec((tm, tk), lambda i, j, k: (i, k))
    b_spec = pl.BlockSpec((tn, tk), lambda i, j, k: (j, k)) if tb else pl.BlockSpec((tk, tn), lambda i, j, k: (k, j))
    o_spec = pl.BlockSpec((tm, tn), lambda i, j, k: (i, j))
    ca, cb = (0 if ta else 1), (1 if tb else 0)

    def body(*refs):
        if res is None:
            a_ref, b_ref, o_ref, acc_ref = refs
            r_ref = None
        else:
            a_ref, b_ref, r_ref, o_ref, acc_ref = refs
        k = pl.program_id(2)

        @pl.when(k == 0)
        def _():
            acc_ref[...] = jnp.zeros_like(acc_ref)

        acc_ref[...] += _dot(a_ref[...], b_ref[...], ca, cb)

        @pl.when(k == nk - 1)
        def _():
            r = acc_ref[...]
            if r_ref is not None:
                r = r + r_ref[...].astype(F32)
            o_ref[...] = r.astype(o_ref.dtype)

    ins = [a, b] + ([] if res is None else [res])
    in_specs = [a_spec, b_spec] + ([] if res is None else [o_spec])
    return pl.pallas_call(
        body, grid=(M // tm, N // tn, nk), in_specs=in_specs, out_specs=o_spec,
        out_shape=jax.ShapeDtypeStruct((M, N), out_dtype), scratch_shapes=[pltpu.VMEM((tm, tn), F32)],
        compiler_params=_cparams("parallel", "parallel", "arbitrary"), name=name)(*ins)


def _rows(S, cap):
    return _pick(S, cap, 8)


def _rms_fwd(x, w, name):
    S, D = x.shape
    tm = _rows(S, 512)

    def body(x_ref, w_ref, o_ref):
        xv = x_ref[...]
        r = lax.rsqrt(jnp.mean(xv * xv, axis=-1, keepdims=True) + EPS)
        o_ref[...] = ((xv * r) * w_ref[...]).astype(o_ref.dtype)

    row = pl.BlockSpec((tm, D), lambda i: (i, 0))
    return pl.pallas_call(body, grid=(S // tm,), in_specs=[row, pl.BlockSpec((1, D), lambda i: (0, 0))], out_specs=row,
                          out_shape=jax.ShapeDtypeStruct((S, D), _MXU), compiler_params=_cparams("parallel"), name=name)(x, w)


def _rms_bwd(dh, x, w, dres, name):
    S, D = x.shape
    tm = _rows(S, 256)

    def body(*refs):
        if dres is None:
            dh_ref, x_ref, w_ref, dx_ref, dxb_ref, dw_ref = refs
            dres_ref = None
        else:
            dh_ref, x_ref, w_ref, dres_ref, dx_ref, dxb_ref, dw_ref = refs
        xv = x_ref[...]
        r = lax.rsqrt(jnp.mean(xv * xv, axis=-1, keepdims=True) + EPS)
        xn = xv * r
        dy = dh_ref[...].astype(F32)

        @pl.when(pl.program_id(0) == 0)
        def _():
            dw_ref[...] = jnp.zeros_like(dw_ref)

        dw_ref[...] += jnp.sum(dy * xn, axis=0, keepdims=True)
        dxn = dy * w_ref[...]
        dx = r * (dxn - xn * jnp.mean(dxn * xn, axis=-1, keepdims=True))
        if dres_ref is not None:
            dx = dx + dres_ref[...]
        dx_ref[...] = dx
        dxb_ref[...] = dx.astype(dxb_ref.dtype)

    row = pl.BlockSpec((tm, D), lambda i: (i, 0))
    vec = pl.BlockSpec((1, D), lambda i: (0, 0))
    ins = [dh, x, w] + ([] if dres is None else [dres])
    in_specs = [row, row, vec] + ([] if dres is None else [row])
    return pl.pallas_call(
        body, grid=(S // tm,), in_specs=in_specs, out_specs=[row, row, vec],
        out_shape=[jax.ShapeDtypeStruct((S, D), F32), jax.ShapeDtypeStruct((S, D), _MXU), jax.ShapeDtypeStruct((1, D), F32)],
        compiler_params=_cparams("arbitrary"), name=name)(*ins)


def _loss_bwd(x, tgt, w, name):
    S, D = x.shape
    tm = _rows(S, 256)

    def body(x_ref, t_ref, w_ref, dx_ref, dxb_ref, dw_ref, loss_ref):
        xv = x_ref[...]
        r = lax.rsqrt(jnp.mean(xv * xv, axis=-1, keepdims=True) + EPS)
        xn = xv * r
        e = xn * w_ref[...] - t_ref[...]

        @pl.when(pl.program_id(0) == 0)
        def _():
            dw_ref[...] = jnp.zeros_like(dw_ref)
            loss_ref[...] = jnp.zeros_like(loss_ref)

        tok = jnp.mean(e * e, axis=-1, keepdims=True)
        loss_ref[...] += jnp.broadcast_to(0.5 * jnp.sum(tok, axis=0, keepdims=True), loss_ref.shape)
        dy = e * (1.0 / D)
        dw_ref[...] += jnp.sum(dy * xn, axis=0, keepdims=True)
        dxn = dy * w_ref[...]
        dx = r * (dxn - xn * jnp.mean(dxn * xn, axis=-1, keepdims=True))
        dx_ref[...] = dx
        dxb_ref[...] = dx.astype(dxb_ref.dtype)

    row = pl.BlockSpec((tm, D), lambda i: (i, 0))
    vec = pl.BlockSpec((1, D), lambda i: (0, 0))
    return pl.pallas_call(
        body, grid=(S // tm,), in_specs=[row, row, vec],
        out_specs=[row, row, vec, pl.BlockSpec((8, 128), lambda i: (0, 0))],
        out_shape=[jax.ShapeDtypeStruct((S, D), F32), jax.ShapeDtypeStruct((S, D), _MXU),
                   jax.ShapeDtypeStruct((1, D), F32), jax.ShapeDtypeStruct((8, 128), F32)],
        compiler_params=_cparams("arbitrary"), name=name)(x, tgt, w)


def _conv_tiles(S, C):
    return _rows(S, 256), _pick(C, 1536)


def _dwconv_fwd(x, w, b, name):
    S, C = x.shape
    K = w.shape[0]
    tm, tc = _conv_tiles(S, C)

    def body(x_ref, p_ref, w_ref, b_ref, o_ref):
        cur = x_ref[...]
        prev = jnp.where(pl.program_id(0) > 0, p_ref[...], 0.0)
        xx = jnp.concatenate([prev, cur], axis=0)
        acc = cur * w_ref[K - 1:K, :] + b_ref[...]
        for d in range(1, K):
            acc = acc + pltpu.roll(xx, d, 0)[8:, :] * w_ref[K - 1 - d:K - d, :]
        o_ref[...] = acc

    return pl.pallas_call(
        body, grid=(S // tm, C // tc),
        in_specs=[pl.BlockSpec((tm, tc), lambda i, j: (i, j)),
                  pl.BlockSpec((8, tc), lambda i, j: (jnp.maximum(i * (tm // 8) - 1, 0), j)),
                  pl.BlockSpec((K, tc), lambda i, j: (0, j)), pl.BlockSpec((1, tc), lambda i, j: (0, j))],
        out_specs=pl.BlockSpec((tm, tc), lambda i, j: (i, j)), out_shape=jax.ShapeDtypeStruct((S, C), F32),
        compiler_params=_cparams("parallel", "parallel"), name=name)(x, x, w, b)


def _conv_bwd_w(x, dy, K, name):
    S, C = x.shape
    tm, tc = _conv_tiles(S, C)

    def body(x_ref, p_ref, dy_ref, dw_ref, db_ref):
        i = pl.program_id(1)

        @pl.when(i == 0)
        def _():
            dw_ref[...] = jnp.zeros_like(dw_ref)
            db_ref[...] = jnp.zeros_like(db_ref)

        cur = x_ref[...]
        prev = jnp.where(i > 0, p_ref[...], 0.0)
        xx = jnp.concatenate([prev, cur], axis=0)
        g = dy_ref[...].astype(F32)
        db_ref[...] += jnp.sum(g, axis=0, keepdims=True)
        dw_ref[K - 1:K, :] += jnp.sum(g * cur, axis=0, keepdims=True)
        for d in range(1, K):
            dw_ref[K - 1 - d:K - d, :] += jnp.sum(g * pltpu.roll(xx, d, 0)[8:, :], axis=0, keepdims=True)

    return pl.pallas_call(
        body, grid=(C // tc, S // tm),
        in_specs=[pl.BlockSpec((tm, tc), lambda j, i: (i, j)),
                  pl.BlockSpec((8, tc), lambda j, i: (jnp.maximum(i * (tm // 8) - 1, 0), j)),
                  pl.BlockSpec((tm, tc), lambda j, i: (i, j))],
        out_specs=[pl.BlockSpec((K, tc), lambda j, i: (0, j)), pl.BlockSpec((1, tc), lambda j, i: (0, j))],
        out_shape=[jax.ShapeDtypeStruct((K, C), F32), jax.ShapeDtypeStruct((1, C), F32)],
        compiler_params=_cparams("parallel", "arbitrary"), name=name)(x, x, dy)


def _dwconv_bwd_x(dy, w, name):
    S, C = dy.shape
    K = w.shape[0]
    tm, tc = _conv_tiles(S, C)
    last = S // tm - 1

    def body(g_ref, n_ref, w_ref, o_ref):
        cur = g_ref[...]
        nxt = jnp.where(pl.program_id(0) < last, n_ref[...], 0.0)
        xx = jnp.concatenate([cur, nxt], axis=0)
        acc = cur * w_ref[K - 1:K, :]
        for d in range(1, K):
            acc = acc + pltpu.roll(xx, tm + 8 - d, 0)[:tm, :] * w_ref[K - 1 - d:K - d, :]
        o_ref[...] = acc.astype(o_ref.dtype)

    return pl.pallas_call(
        body, grid=(S // tm, C // tc),
        in_specs=[pl.BlockSpec((tm, tc), lambda i, j: (i, j)),
                  pl.BlockSpec((8, tc), lambda i, j: (jnp.minimum((i + 1) * (tm // 8), S // 8 - 1), j)),
                  pl.BlockSpec((K, tc), lambda i, j: (0, j))],
        out_specs=pl.BlockSpec((tm, tc), lambda i, j: (i, j)), out_shape=jax.ShapeDtypeStruct((S, C), _MXU),
        compiler_params=_cparams("parallel", "parallel"), name=name)(dy, dy, w)


def _silu_fwd(pre, name):
    S, C = pre.shape
    tm, tc = _conv_tiles(S, C)

    def body(p_ref, o_ref):
        p = p_ref[...]
        o_ref[...] = p * _sigmoid(p)

    blk = pl.BlockSpec((tm, tc), lambda i, j: (i, j))
    return pl.pallas_call(body, grid=(S // tm, C // tc), in_specs=[blk], out_specs=blk,
                          out_shape=jax.ShapeDtypeStruct((S, C), F32), compiler_params=_cparams("parallel", "parallel"),
                          name=name)(pre)


def _silu_bwd(pre, dact, name):
    S, C = pre.shape
    tm, tc = _conv_tiles(S, C)

    def body(p_ref, g_ref, o_ref):
        p = p_ref[...]
        s = _sigmoid(p)
        o_ref[...] = g_ref[...] * (s * (1.0 + p * (1.0 - s)))

    blk = pl.BlockSpec((tm, tc), lambda i, j: (i, j))
    return pl.pallas_call(body, grid=(S // tm, C // tc), in_specs=[blk, blk], out_specs=blk,
                          out_shape=jax.ShapeDtypeStruct((S, C), F32), compiler_params=_cparams("parallel", "parallel"),
                          name=name)(pre, dact)


def _shifted_rows(cur, prev, K):
    xx = jnp.concatenate([prev, cur], axis=0)
    return [cur] + [pltpu.roll(xx, d, 0)[8:, :] for d in range(1, K)]


def _conv_glu_fwd(x, w, b, name):
    S, C = x.shape
    K = w.shape[0]
    Fh = C // 2
    tm = _rows(S, 128)

    def body(x_ref, p_ref, w_ref, b_ref, o_ref):
        sh = _shifted_rows(x_ref[...], jnp.where(pl.program_id(0) > 0, p_ref[...], 0.0), K)
        u = b_ref[...] + sum(sh[d] * w_ref[K - 1 - d:K - d, :] for d in range(K))
        g = u[:, :Fh]
        o_ref[...] = (g * _sigmoid(g) * u[:, Fh:]).astype(o_ref.dtype)

    return pl.pallas_call(
        body, grid=(S // tm,),
        in_specs=[pl.BlockSpec((tm, C), lambda i: (i, 0)), pl.BlockSpec((8, C), lambda i: (jnp.maximum(i * (tm // 8) - 1, 0), 0)),
                  pl.BlockSpec((K, C), lambda i: (0, 0)), pl.BlockSpec((1, C), lambda i: (0, 0))],
        out_specs=pl.BlockSpec((tm, Fh), lambda i: (i, 0)), out_shape=jax.ShapeDtypeStruct((S, Fh), _MXU),
        compiler_params=_cparams("parallel"), name=name)(x, x, w, b)


def _conv_glu_bwd(x, dact, w, b, name):
    S, C = x.shape
    K = w.shape[0]
    Fh = C // 2
    tm = _rows(S, 128)

    def body(x_ref, p_ref, g_ref, w_ref, b_ref, du_ref, dw_ref, db_ref):
        i = pl.program_id(0)

        @pl.when(i == 0)
        def _():
            dw_ref[...] = jnp.zeros_like(dw_ref)
            db_ref[...] = jnp.zeros_like(db_ref)

        sh = _shifted_rows(x_ref[...], jnp.where(i > 0, p_ref[...], 0.0), K)
        u = b_ref[...] + sum(sh[d] * w_ref[K - 1 - d:K - d, :] for d in range(K))
        g = u[:, :Fh]
        da = g_ref[...].astype(F32)
        s = _sigmoid(g)
        halves = ((slice(0, Fh), da * u[:, Fh:] * (s * (1.0 + g * (1.0 - s)))), (slice(Fh, C), da * (g * s)))
        for cols, du in halves:
            du_ref[:, cols] = du
            db_ref[:, cols] += jnp.sum(du, axis=0, keepdims=True)
            for d in range(K):
                dw_ref[K - 1 - d:K - d, cols] += jnp.sum(du * sh[d][:, cols], axis=0, keepdims=True)

    return pl.pallas_call(
        body, grid=(S // tm,),
        in_specs=[pl.BlockSpec((tm, C), lambda i: (i, 0)), pl.BlockSpec((8, C), lambda i: (jnp.maximum(i * (tm // 8) - 1, 0), 0)),
                  pl.BlockSpec((tm, Fh), lambda i: (i, 0)), pl.BlockSpec((K, C), lambda i: (0, 0)), pl.BlockSpec((1, C), lambda i: (0, 0))],
        out_specs=[pl.BlockSpec((tm, C), lambda i: (i, 0)), pl.BlockSpec((K, C), lambda i: (0, 0)), pl.BlockSpec((1, C), lambda i: (0, 0))],
        out_shape=[jax.ShapeDtypeStruct((S, C), F32), jax.ShapeDtypeStruct((K, C), F32), jax.ShapeDtypeStruct((1, C), F32)],
        compiler_params=_cparams("arbitrary"), name=name)(x, x, dact, w, b)


def _xattn_fwd(q, k, v, name):
    S, D = q.shape
    M = k.shape[0]
    hd = D // MEM_HEADS
    tm = _rows(S, 512)
    scale = 1.0 / math.sqrt(hd)

    def body(q_ref, k_ref, v_ref, o_ref):
        for h in range(MEM_HEADS):
            sl = slice(h * hd, (h + 1) * hd)
            s = _dot(q_ref[:, sl], k_ref[:, sl], 1, 1) * scale
            p = jnp.exp(s - jnp.max(s, axis=-1, keepdims=True))
            p = p / jnp.sum(p, axis=-1, keepdims=True)
            o_ref[:, sl] = _dot(p, v_ref[:, sl], 1, 0).astype(o_ref.dtype)

    kv = pl.BlockSpec((M, D), lambda i: (0, 0))
    row = pl.BlockSpec((tm, D), lambda i: (i, 0))
    return pl.pallas_call(body, grid=(S // tm,), in_specs=[row, kv, kv], out_specs=row,
                          out_shape=jax.ShapeDtypeStruct((S, D), _MXU), compiler_params=_cparams("parallel"), name=name)(q, k, v)


def _xattn_bwd(q, k, v, do, name):
    S, D = q.shape
    M = k.shape[0]
    hd = D // MEM_HEADS
    tm = _rows(S, 512)
    scale = 1.0 / math.sqrt(hd)

    def body(q_ref, k_ref, v_ref, do_ref, dq_ref, dk_ref, dv_ref):
        @pl.when(pl.program_id(0) == 0)
        def _():
            dk_ref[...] = jnp.zeros_like(dk_ref)
            dv_ref[...] = jnp.zeros_like(dv_ref)

        for h in range(MEM_HEADS):
            sl = slice(h * hd, (h + 1) * hd)
            qh, kh, vh, doh = q_ref[:, sl], k_ref[:, sl], v_ref[:, sl], do_ref[:, sl]
            s = _dot(qh, kh, 1, 1) * scale
            p = jnp.exp(s - jnp.max(s, axis=-1, keepdims=True))
            p = p / jnp.sum(p, axis=-1, keepdims=True)
            dp = _dot(doh, vh, 1, 1)
            dv_ref[:, sl] += _dot(p, doh, 0, 0)
            ds = (p * (dp - jnp.sum(dp * p, axis=-1, keepdims=True))) * scale
            dq_ref[:, sl] = _dot(ds, kh, 1, 0).astype(dq_ref.dtype)
            dk_ref[:, sl] += _dot(ds, qh, 0, 0)

    kv = pl.BlockSpec((M, D), lambda i: (0, 0))
    row = pl.BlockSpec((tm, D), lambda i: (i, 0))
    return pl.pallas_call(
        body, grid=(S // tm,), in_specs=[row, kv, kv, row], out_specs=[row, kv, kv],
        out_shape=[jax.ShapeDtypeStruct((S, D), _MXU), jax.ShapeDtypeStruct((M, D), F32), jax.ShapeDtypeStruct((M, D), F32)],
        compiler_params=_cparams("arbitrary"), name=name)(q, k, v, do)


def _tri(n, strict, upper):
    r = lax.broadcasted_iota(jnp.int32, (n, n), 0)
    c = lax.broadcasted_iota(jnp.int32, (n, n), 1)
    if upper:
        return (c > r) if strict else (c >= r)
    return (r > c) if strict else (r >= c)


def _ssd_prep(dtp, dt_bias, a_log, name):
    S = dtp.shape[0]
    L, H = SSD_CHUNK, SSD_HEADS

    def body(p_ref, b_ref, al_ref, dt_ref, cs_ref):
        v = p_ref[:, :H] + b_ref[...]
        dt = jnp.maximum(v, 0.0) + _log1p(jnp.exp(-jnp.abs(v)))
        dt_ref[...] = dt
        a = dt * (-jnp.exp(al_ref[...]))
        cs_ref[...] = jnp.dot(_tri(L, False, False).astype(F32), a, precision=_HI, preferred_element_type=F32)

    blk = pl.BlockSpec((L, H), lambda c: (c, 0))
    vec = pl.BlockSpec((1, H), lambda c: (0, 0))
    return pl.pallas_call(body, grid=(S // L,), in_specs=[pl.BlockSpec((L, DT_PAD), lambda c: (c, 0)), vec, vec],
                          out_specs=[blk, blk], out_shape=[jax.ShapeDtypeStruct((S, H), F32)] * 2,
                          compiler_params=_cparams("parallel"), name=name)(dtp, dt_bias, a_log)


def _head_col(blk_ref, h):
    sel = lax.broadcasted_iota(jnp.int32, (1, SSD_HEADS), 1) == h
    return jnp.sum(jnp.where(sel, blk_ref[...], 0.0), axis=1, keepdims=True)


def _ssdg_fwd(xs, Bm, Cm, dt, cs, csT, name):
    H, S, P = xs.shape
    L, N = SSD_CHUNK, SSD_STATE
    nc = S // L
    rep = H // SSD_GROUPS
    hs = range(rep)

    def body(x_ref, b_ref, c_ref, dt_ref, cs_ref, csT_ref, y_ref, prev_ref, st_ref):
        c, g = pl.program_id(0), pl.program_id(1)

        @pl.when(c == 0)
        def _():
            for hh in hs:
                st_ref[g * rep + hh] = jnp.zeros((P, N), F32)

        Bv, Cv = b_ref[...], c_ref[...]
        tril = _tri(L, False, False)
        dtc = [_head_col(dt_ref, g * rep + hh) for hh in hs]
        csc = [_head_col(cs_ref, g * rep + hh) for hh in hs]
        csr = [csT_ref[hh:hh + 1, :] for hh in hs]
        last = [r[:, L - 1:L] for r in csr]
        xc = [x_ref[hh] * dtc[hh] for hh in hs]
        cb = _dot(Cv, Bv, 1, 1)
        m = [cb * jnp.where(tril, jnp.exp(jnp.where(tril, csc[hh] - csr[hh], 0.0)), 0.0) for hh in hs]
        prev = [st_ref[g * rep + hh] for hh in hs]
        yd = [_dot(m[hh], xc[hh], 1, 0) for hh in hs]
        yo = [_dot(Cv, prev[hh], 1, 1) for hh in hs]
        new = [_dot(xc[hh] * jnp.exp(last[hh] - csc[hh]), Bv, 0, 0) for hh in hs]
        for hh in hs:
            y_ref[hh] = yd[hh] + yo[hh] * jnp.exp(csc[hh])
            prev_ref[hh] = prev[hh]
            st_ref[g * rep + hh] = prev[hh] * jnp.exp(last[hh]) + new[hh]

    tok = pl.BlockSpec((L, H), lambda c, g: (c, 0))
    return pl.pallas_call(
        body, grid=(nc, SSD_GROUPS),
        in_specs=[pl.BlockSpec((rep, L, P), lambda c, g: (g, c, 0)), pl.BlockSpec((None, L, N), lambda c, g: (g, c, 0)),
                  pl.BlockSpec((None, L, N), lambda c, g: (g, c, 0)), tok, tok, pl.BlockSpec((rep, L), lambda c, g: (g, c))],
        out_specs=[pl.BlockSpec((rep, L, P), lambda c, g: (g, c, 0)),
                   pl.BlockSpec((rep, None, P, N), lambda c, g: (g, c, 0, 0))],
        out_shape=[jax.ShapeDtypeStruct((H, S, P), F32), jax.ShapeDtypeStruct((H, nc, P, N), F32)],
        scratch_shapes=[pltpu.VMEM((H, P, N), F32)],
        compiler_params=_cparams("arbitrary", "arbitrary"), name=name)(xs, Bm, Cm, dt, cs, csT)


def _ssdg_bwd(xs, Bm, Cm, dt, cs, csT, prev, dy, a_log, d_skip, name):
    H, S, P = xs.shape
    L, N = SSD_CHUNK, SSD_STATE
    nc = S // L
    rep = H // SSD_GROUPS
    hs = range(rep)

    def rowsum(a):
        return jnp.sum(a, axis=1, keepdims=True)

    def body(x_ref, b_ref, c_ref, dt_ref, cs_ref, csT_ref, prev_ref, dy_ref, al_ref, dk_ref,
             dx_ref, db_ref, dc_ref, ddt_ref, da_ref, g_ref):
        ci, g = pl.program_id(0), pl.program_id(1)

        @pl.when(ci == 0)
        def _():
            for hh in hs:
                g_ref[g * rep + hh] = jnp.zeros((P, N), F32)

        @pl.when((ci == 0) & (g == 0))
        def _():
            da_ref[...] = jnp.zeros_like(da_ref)

        @pl.when(g == 0)
        def _():
            ddt_ref[...] = jnp.zeros_like(ddt_ref)

        lane = lax.broadcasted_iota(jnp.int32, (1, H), 1)
        sel = [lane == g * rep + hh for hh in hs]
        A_h = [-jnp.exp(rowsum(jnp.where(s, al_ref[...], 0.0))) for s in sel]
        dsk = [rowsum(jnp.where(s, dk_ref[...], 0.0)) for s in sel]
        dtc = [_head_col(dt_ref, g * rep + hh) for hh in hs]
        csc = [_head_col(cs_ref, g * rep + hh) for hh in hs]
        csr = [csT_ref[hh:hh + 1, :] for hh in hs]
        last = [r[:, L - 1:L] for r in csr]
        Bv, Cv = b_ref[...], c_ref[...]
        xv = [x_ref[hh] for hh in hs]
        xc = [xv[hh] * dtc[hh] for hh in hs]
        dY = [dy_ref[hh] for hh in hs]
        prv = [prev_ref[hh] for hh in hs]
        G = [g_ref[g * rep + hh] for hh in hs]
        ecs = [jnp.exp(v) for v in csc]
        w = [jnp.exp(last[hh] - csc[hh]) for hh in hs]
        cd = [jnp.exp(v) for v in last]
        tril = _tri(L, False, False)
        triu = _tri(L, False, True)
        lam = [jnp.where(tril, jnp.exp(jnp.where(tril, csc[hh] - csr[hh], 0.0)), 0.0) for hh in hs]
        lamT = [jnp.where(triu, jnp.exp(jnp.where(triu, csr[hh] - csc[hh], 0.0)), 0.0) for hh in hs]
        cb = _dot(Cv, Bv, 1, 1)
        bc = _dot(Bv, Cv, 1, 1)
        dM = [_dot(dY[hh], xc[hh], 1, 1) for hh in hs]
        dMT = [_dot(xc[hh], dY[hh], 1, 1) for hh in hs]
        cp = [_dot(Cv, prv[hh], 1, 1) for hh in hs]
        BG = [_dot(Bv, G[hh], 1, 1) for hh in hs]
        dYe = [dY[hh] * ecs[hh] for hh in hs]
        dprev = [_dot(dYe[hh], Cv, 0, 0) for hh in hs]
        m = [cb * lam[hh] for hh in hs]
        mT = [bc * lamT[hh] for hh in hs]
        dxc = [_dot(mT[hh], dY[hh], 1, 0) + w[hh] * BG[hh] for hh in hs]
        dcb = sum([dM[hh] * lam[hh] for hh in hs][1:], dM[0] * lam[0])
        dcbT = sum([dMT[hh] * lamT[hh] for hh in hs][1:], dMT[0] * lamT[0])
        dC = _dot(dcb, Bv, 1, 0)
        dB = _dot(dcbT, Cv, 1, 0)
        for hh in hs:
            dC = dC + _dot(dYe[hh], prv[hh], 1, 0)
            dB = dB + _dot(xc[hh] * w[hh], G[hh], 1, 0)
        dc_ref[...] = dC
        db_ref[...] = dB
        ddt_acc = jnp.zeros((L, H), F32)
        da_acc = jnp.zeros((1, H), F32)
        rev = _tri(L, False, True).astype(F32)
        for hh in hs:
            dww = rowsum(xc[hh] * BG[hh]) * w[hh]
            dcs = (rowsum(dM[hh] * m[hh]) - rowsum(dMT[hh] * mT[hh]) + rowsum(dY[hh] * (cp[hh] * ecs[hh])) - dww)
            extra = jnp.sum(dww, axis=0, keepdims=True) + cd[hh] * jnp.sum(rowsum(G[hh] * prv[hh]), axis=0, keepdims=True)
            g_ref[g * rep + hh] = G[hh] * cd[hh] + dprev[hh]
            da = jnp.dot(rev, dcs, precision=_HI, preferred_element_type=F32) + extra
            dx_ref[hh] = dxc[hh] * dtc[hh] + dY[hh] * dsk[hh]
            ddt_acc = ddt_acc + jnp.where(sel[hh], da * A_h[hh] + rowsum(dxc[hh] * xv[hh]), 0.0)
            da_acc = da_acc + jnp.where(sel[hh], jnp.sum(da * dtc[hh], axis=0, keepdims=True), 0.0)
        ddt_ref[...] += ddt_acc
        da_ref[...] += da_acc

    rc = lambda ci: nc - 1 - ci
    hd = pl.BlockSpec((rep, L, P), lambda ci, g: (g, rc(ci), 0))
    grp = pl.BlockSpec((None, L, N), lambda ci, g: (g, rc(ci), 0))
    tok = pl.BlockSpec((L, H), lambda ci, g: (rc(ci), 0))
    vec = pl.BlockSpec((1, H), lambda ci, g: (0, 0))
    return pl.pallas_call(
        body, grid=(nc, SSD_GROUPS),
        in_specs=[hd, grp, grp, tok, tok, pl.BlockSpec((rep, L), lambda ci, g: (g, rc(ci))),
                  pl.BlockSpec((rep, None, P, N), lambda ci, g: (g, rc(ci), 0, 0)), hd, vec, vec],
        out_specs=[hd, grp, grp, tok, vec],
        out_shape=[jax.ShapeDtypeStruct((H, S, P), F32), jax.ShapeDtypeStruct((SSD_GROUPS, S, N), F32),
                   jax.ShapeDtypeStruct((SSD_GROUPS, S, N), F32), jax.ShapeDtypeStruct((S, H), F32),
                   jax.ShapeDtypeStruct((1, H), F32)],
        scratch_shapes=[pltpu.VMEM((H, P, N), F32)],
        compiler_params=_cparams("arbitrary", "arbitrary"), name=name)(xs, Bm, Cm, dt, cs, csT, prev, dy, a_log, d_skip)


def _dt_bwd(ddt, dA, dtp, dt_bias, a_log, name):
    S, H = ddt.shape
    tm = _rows(S, 512)

    def body(g_ref, da_ref, p_ref, b_ref, al_ref, o_ref, db_ref, dal_ref):
        @pl.when(pl.program_id(0) == 0)
        def _():
            db_ref[...] = jnp.zeros_like(db_ref)
            dal_ref[...] = da_ref[...] * (-jnp.exp(al_ref[...]))

        g = g_ref[...] * _sigmoid(p_ref[:, :H] + b_ref[...])
        db_ref[...] += jnp.sum(g, axis=0, keepdims=True)
        o_ref[...] = jnp.zeros_like(o_ref)
        o_ref[:, :H] = g.astype(o_ref.dtype)

    vec = pl.BlockSpec((1, H), lambda i: (0, 0))
    return pl.pallas_call(
        body, grid=(S // tm,),
        in_specs=[pl.BlockSpec((tm, H), lambda i: (i, 0)), vec, pl.BlockSpec((tm, DT_PAD), lambda i: (i, 0)), vec, vec],
        out_specs=[pl.BlockSpec((tm, DT_PAD), lambda i: (i, 0)), vec, vec],
        out_shape=[jax.ShapeDtypeStruct((S, DT_PAD), _MXU), jax.ShapeDtypeStruct((1, H), F32), jax.ShapeDtypeStruct((1, H), F32)],
        compiler_params=_cparams("arbitrary"), name=name)(ddt, dA, dtp, dt_bias, a_log)


def _ssd_gate_fwd(y, act, z, dskip, w, name):
    S, D = y.shape
    tm = _rows(S, 256)
    Gw = D // SSD_GROUPS

    def body(y_ref, x_ref, z_ref, k_ref, w_ref, o_ref):
        zv = z_ref[...]
        y2 = (y_ref[...] + x_ref[...] * k_ref[...]) * (zv * _sigmoid(zv))
        for g in range(SSD_GROUPS):
            sl = slice(g * Gw, (g + 1) * Gw)
            v = y2[:, sl]
            r = lax.rsqrt(jnp.mean(v * v, axis=-1, keepdims=True) + EPS)
            o_ref[:, sl] = ((v * r) * w_ref[:, sl]).astype(o_ref.dtype)

    row = pl.BlockSpec((tm, D), lambda i: (i, 0))
    vec = pl.BlockSpec((1, D), lambda i: (0, 0))
    return pl.pallas_call(body, grid=(S // tm,), in_specs=[row, row, row, vec, vec], out_specs=row,
                          out_shape=jax.ShapeDtypeStruct((S, D), _MXU), compiler_params=_cparams("parallel"),
                          name=name)(y, act, z, dskip, w)


def _ssd_gate_bwd(dyn, y, act, z, dskip, w, name):
    S, D = y.shape
    tm = _rows(S, 256)
    Gw = D // SSD_GROUPS

    def body(g_ref, y_ref, x_ref, z_ref, k_ref, w_ref, dy_ref, dz_ref, dk_ref, dw_ref):
        @pl.when(pl.program_id(0) == 0)
        def _():
            dk_ref[...] = jnp.zeros_like(dk_ref)
            dw_ref[...] = jnp.zeros_like(dw_ref)

        zv = z_ref[...]
        xv = x_ref[...]
        s = _sigmoid(zv)
        sz = zv * s
        y1 = y_ref[...] + xv * k_ref[...]
        y2 = y1 * sz
        for g in range(SSD_GROUPS):
            sl = slice(g * Gw, (g + 1) * Gw)
            v = y2[:, sl]
            r = lax.rsqrt(jnp.mean(v * v, axis=-1, keepdims=True) + EPS)
            vn = v * r
            gy = g_ref[:, sl].astype(F32)
            dw_ref[:, sl] += jnp.sum(gy * vn, axis=0, keepdims=True)
            dvn = gy * w_ref[:, sl]
            dy2 = r * (dvn - vn * jnp.mean(dvn * vn, axis=-1, keepdims=True))
            dy1 = dy2 * sz[:, sl]
            dy_ref[:, sl] = dy1
            dz_ref[:, sl] = (dy2 * y1[:, sl] * (s[:, sl] * (1.0 + zv[:, sl] * (1.0 - s[:, sl])))).astype(dz_ref.dtype)
            dk_ref[:, sl] += jnp.sum(dy1 * xv[:, sl], axis=0, keepdims=True)

    row = pl.BlockSpec((tm, D), lambda i: (i, 0))
    vec = pl.BlockSpec((1, D), lambda i: (0, 0))
    return pl.pallas_call(
        body, grid=(S // tm,), in_specs=[row, row, row, row, vec, vec], out_specs=[row, row, vec, vec],
        out_shape=[jax.ShapeDtypeStruct((S, D), F32), jax.ShapeDtypeStruct((S, D), _MXU),
                   jax.ShapeDtypeStruct((1, D), F32), jax.ShapeDtypeStruct((1, D), F32)],
        compiler_params=_cparams("arbitrary"), name=name)(dyn, y, act, z, dskip, w)


def _split_dot(v, u):
    hi = v.astype(_MXU)
    lo = (v - hi.astype(F32)).astype(_MXU)
    dn = (((1,), (0,)), ((), ()))
    return (lax.dot_general(hi, u, dn, preferred_element_type=F32) + lax.dot_general(lo, u, dn, preferred_element_type=F32))


def _sb_tiles(S):
    return _pick(S, 256, 128)


SB_LANES = 128
SB_PACK = SB_LANES // SB_HEAD_DIM
SB_ROWS = 128
SB_SCALE = 1.0 / math.sqrt(SB_HEAD_DIM)


def _head_masks():
    lane = lax.broadcasted_iota(jnp.int32, (1, SB_LANES), 1)
    return [(lane // SB_HEAD_DIM) == hh for hh in range(SB_PACK)]


def _by_head(hm, vals):
    out = vals[-1]
    for hh in range(SB_PACK - 2, -1, -1):
        out = jnp.where(hm[hh], vals[hh], out)
    return out


SB_DEAD = -110.0


def _sb_alive(Rs):
    m = Rs[0]
    for R in Rs[1:]:
        m = jnp.maximum(m, R)
    return jnp.max(m) > SB_DEAD


def _sb_rows(a, r):
    return a[r * SB_ROWS:(r + 1) * SB_ROWS]


def _sb_assemble(hm, vals):
    nr = len(vals) // SB_PACK
    return jnp.concatenate([_by_head(hm, vals[r * SB_PACK:(r + 1) * SB_PACK]) for r in range(nr)], axis=0)


def _sb_scores(zs, U, Rs, masks):
    ls = [-jnp.maximum(z, 0.0) - jnp.log(1.0 + jnp.exp(-jnp.abs(z))) for z in zs]
    if masks is not None:
        ls = [jnp.where(m, l, 0.0) for m, l in zip(masks, ls)]
    Es = [lax.dot_general(l.astype(_MXU), U, (((1,), (0,)), ((), ())), preferred_element_type=F32) for l in ls]
    As = [jnp.exp(l + z + (E + R)) for l, z, E, R in zip(ls, zs, Es, Rs)]
    if masks is not None:
        As = [jnp.where(m, A, 0.0) for m, A in zip(masks, As)]
    return ls, [A.astype(_MXU) for A in As]


SB_GROUP = 2


def _sbg_chains(T):
    return [(b, r, hh) for b in range(SB_GROUP) for r in range(T // SB_ROWS) for hh in range(SB_PACK)]


def _lanes(a, b):
    return a[:, b * SB_LANES:(b + 1) * SB_LANES]


def _sbg_join(hm, vals):
    per = len(vals) // SB_GROUP
    return jnp.concatenate([_sb_assemble(hm, vals[b * per:(b + 1) * per]) for b in range(SB_GROUP)], axis=1)


def _sbg_head_sum(hm, a):
    return jnp.concatenate([_by_head(hm, [jnp.sum(jnp.where(m, _lanes(a, b), 0.0), axis=1, keepdims=True) for m in hm])
                            for b in range(SB_GROUP)], axis=1)


def _sbg_fwd(q_arr, k_arr, v_arr, cols, w, name):
    S = q_arr.shape[0]
    T = _sb_tiles(S)
    cq, ck, cv = cols
    GW = SB_GROUP * SB_LANES
    nb = SB_WIDTH // GW

    def body(q_ref, k_ref, v_ref, w_ref, o_ref, y_ref):
        i = pl.program_id(1)
        hm = _head_masks()
        qs = q_ref[...] * SB_SCALE
        chains = _sbg_chains(T)
        qcs = [_sb_rows(jnp.where(hm[hh], _lanes(qs, b), jnp.zeros((T, SB_LANES), qs.dtype)), r) for b, r, hh in chains]
        U = _tri(T, True, False).astype(_MXU)

        def scores_of(j):
            kj = k_ref[pl.ds(pl.multiple_of(j * T, T), T), :]
            return [_dot(qc, _lanes(kj, b), 1, 1) for qc, (b, _, _) in zip(qcs, chains)]

        def weighted(Abs, j):
            vj = v_ref[pl.ds(pl.multiple_of(j * T, T), T), :]
            return _sbg_join(hm, [_dot(Ab, _lanes(vj, b), 1, 0) for Ab, (b, _, _) in zip(Abs, chains)])

        def step(carry):
            jj, acc, Rs, Aprev = carry
            j = i - 1 - jj
            zs = scores_of(j)
            acc = acc + weighted(Aprev, j + 1)
            ls, Abs = _sb_scores(zs, U, Rs, None)
            return jj + 1, acc, tuple(R + jnp.sum(l, axis=1, keepdims=True) for R, l in zip(Rs, ls)), tuple(Abs)

        masks = [_sb_rows(_tri(T, True, False), r) for _, r, _ in chains]
        zero = jnp.zeros((SB_ROWS, 1), F32)
        ls, Abs = _sb_scores(scores_of(i), U, (zero,) * len(chains), masks)
        carry = (jnp.int32(0), jnp.zeros((T, GW), F32), tuple(jnp.sum(l, axis=1, keepdims=True) for l in ls), tuple(Abs))
        jj, acc, _, Alast = lax.while_loop(lambda c: (c[0] < i) & _sb_alive(c[2]), step, carry)
        acc = acc + weighted(Alast, i - jj)
        o_ref[...] = acc
        r = lax.rsqrt(_sbg_head_sum(hm, acc * acc) * (1.0 / SB_HEAD_DIM) + EPS)
        y_ref[...] = ((acc * r) * w_ref[...]).astype(y_ref.dtype)

    blk = pl.BlockSpec((T, GW), lambda h, i: (i, h))
    return pl.pallas_call(
        body, grid=(nb, S // T),
        in_specs=[pl.BlockSpec((T, GW), lambda h, i: (i, cq + h)), pl.BlockSpec((S, GW), lambda h, i: (0, ck + h), pipeline_mode=pl.Buffered(1)),
                  pl.BlockSpec((S, GW), lambda h, i: (0, cv + h), pipeline_mode=pl.Buffered(1)), pl.BlockSpec((1, GW), lambda h, i: (0, h))],
        out_specs=[blk, blk], out_shape=[jax.ShapeDtypeStruct((S, SB_WIDTH), F32), jax.ShapeDtypeStruct((S, SB_WIDTH), _MXU)],
        compiler_params=_cparams("parallel", "parallel"), name=name)(q_arr, k_arr, v_arr, w)


def _sbg_bwd(q_arr, k_arr, v_arr, cols, o, dy_arr, cdy, w, name):
    S = q_arr.shape[0]
    T = _sb_tiles(S)
    cq, ck, cv = cols
    GW = SB_GROUP * SB_LANES
    nb = SB_WIDTH // GW

    def body(q_ref, k_ref, v_ref, o_ref, dy_ref, w_ref, dq_ref, dk_ref, dv_ref, dw_ref):
        i = pl.program_id(1)

        @pl.when(i == 0)
        def _():
            dk_ref[...] = jnp.zeros_like(dk_ref)
            dv_ref[...] = jnp.zeros_like(dv_ref)
            dw_ref[...] = jnp.zeros_like(dw_ref)

        hm = _head_masks()
        chains = _sbg_chains(T)
        qs = q_ref[...] * SB_SCALE
        ov = o_ref[...]
        gy = dy_ref[...]
        r = lax.rsqrt(_sbg_head_sum(hm, ov * ov) * (1.0 / SB_HEAD_DIM) + EPS)
        on = ov * r
        dw_ref[...] += jnp.sum(gy * on, axis=0, keepdims=True)
        don = gy * w_ref[...]
        do = r * (don - on * (_sbg_head_sum(hm, don * on) * (1.0 / SB_HEAD_DIM)))
        dob = do.astype(_MXU)
        dprod = dob.astype(F32) * ov
        zt = jnp.zeros((T, SB_LANES), dob.dtype)
        qm = [[jnp.where(hm[hh], _lanes(qs, b), zt) for hh in range(SB_PACK)] for b in range(SB_GROUP)]
        dm = [[jnp.where(hm[hh], _lanes(dob, b), zt) for hh in range(SB_PACK)] for b in range(SB_GROUP)]
        qcs = [_sb_rows(qm[b][hh], r_) for b, r_, hh in chains]
        doc = [_sb_rows(dm[b][hh], r_) for b, r_, hh in chains]
        Dt = [_sb_rows(jnp.sum(jnp.where(hm[hh], _lanes(dprod, b), 0.0), axis=1, keepdims=True), r_) for b, r_, hh in chains]
        U = _tri(T, True, False).astype(_MXU)
        Ui = _tri(T, False, False).astype(_MXU)

        def products_of(j):
            off = pl.multiple_of(j * T, T)
            kj = k_ref[pl.ds(off, T), :]
            vj = v_ref[pl.ds(off, T), :]
            return ([_dot(qc, _lanes(kj, b), 1, 1) for qc, (b, _, _) in zip(qcs, chains)],
                    [_dot(d, _lanes(vj, b), 1, 1) for d, (b, _, _) in zip(doc, chains)])

        def core(zs, dAs, Rs, Qs, masks):
            ls, Abs = _sb_scores(zs, U, Rs, masks)
            Gs = [dA * Ab.astype(F32) for dA, Ab in zip(dAs, Abs)]
            sfx = [_split_dot(G, Ui) for G in Gs]
            dzs = []
            for c, (l, G, s, D, Q) in enumerate(zip(ls, Gs, sfx, Dt, Qs)):
                P = D - (s + Q)
                dz = jnp.exp(l) * (G + P) - P
                if masks is not None:
                    dz = jnp.where(masks[c], dz, 0.0)
                dzs.append(dz.astype(_MXU))
            newR = tuple(R + jnp.sum(l, axis=1, keepdims=True) for R, l in zip(Rs, ls))
            newQ = tuple(Q + jnp.sum(G, axis=1, keepdims=True) for Q, G in zip(Qs, Gs))
            return tuple(Abs), tuple(dzs), newR, newQ

        def over_rows(vals, other):
            nr = T // SB_ROWS
            tiles = []
            for b in range(SB_GROUP):
                acc = None
                for hh in range(SB_PACK):
                    rows = jnp.concatenate([vals[(b * nr + r_) * SB_PACK + hh] for r_ in range(nr)], axis=0)
                    part = _dot(rows, other[b][hh], 0, 0)
                    acc = part if acc is None else acc + part
                tiles.append(acc)
            return jnp.concatenate(tiles, axis=1)

        def emit(Abs, dzs, j):
            off = pl.multiple_of(j * T, T)
            kj = k_ref[pl.ds(off, T), :]
            dk_ref[pl.ds(off, T), :] += over_rows(dzs, qm)
            dv_ref[pl.ds(off, T), :] += over_rows(Abs, dm)
            return _sbg_join(hm, [_dot(dzb, _lanes(kj, b), 1, 0) for dzb, (b, _, _) in zip(dzs, chains)])

        def step(carry):
            jj, dq, Rs, Qs, Aprev, dzprev = carry
            j = i - 1 - jj
            zs, dAs = products_of(j)
            dq = dq + emit(Aprev, dzprev, j + 1)
            Abs, dzs, Rs, Qs = core(zs, dAs, Rs, Qs, None)
            return jj + 1, dq, Rs, Qs, Abs, dzs

        masks = [_sb_rows(_tri(T, True, False), r_) for _, r_, _ in chains]
        zero = (jnp.zeros((SB_ROWS, 1), F32),) * len(chains)
        zs, dAs = products_of(i)
        Abs, dzs, Rs, Qs = core(zs, dAs, zero, zero, masks)
        jj, dq, _, _, Alast, dzlast = lax.while_loop(lambda c: (c[0] < i) & _sb_alive(c[2]), step,
                                                     (jnp.int32(0), jnp.zeros((T, GW), F32), Rs, Qs, Abs, dzs))
        dq = dq + emit(Alast, dzlast, i - jj)
        dq_ref[...] = (dq * SB_SCALE).astype(dq_ref.dtype)

    blk = pl.BlockSpec((T, GW), lambda h, i: (i, h))
    full = pl.BlockSpec((S, GW), lambda h, i: (0, h), pipeline_mode=pl.Buffered(1))
    wsp = pl.BlockSpec((1, GW), lambda h, i: (0, h))
    return pl.pallas_call(
        body, grid=(nb, S // T),
        in_specs=[pl.BlockSpec((T, GW), lambda h, i: (i, cq + h)), pl.BlockSpec((S, GW), lambda h, i: (0, ck + h), pipeline_mode=pl.Buffered(1)),
                  pl.BlockSpec((S, GW), lambda h, i: (0, cv + h), pipeline_mode=pl.Buffered(1)), blk,
                  pl.BlockSpec((T, GW), lambda h, i: (i, cdy + h)), wsp],
        out_specs=[blk, full, full, wsp],
        out_shape=[jax.ShapeDtypeStruct((S, SB_WIDTH), _MXU), jax.ShapeDtypeStruct((S, SB_WIDTH), F32),
                   jax.ShapeDtypeStruct((S, SB_WIDTH), F32), jax.ShapeDtypeStruct((1, SB_WIDTH), F32)],
        compiler_params=_cparams("parallel", "arbitrary"), name=name)(q_arr, k_arr, v_arr, o, dy_arr, w)


def _adamw(w, g, m, v, name):
    R, C = w.shape
    tm = _rows(R, 256) if R % 8 == 0 else R
    c1 = 1.0 - ADAM_B1 ** ADAM_STEP
    c2 = 1.0 - ADAM_B2 ** ADAM_STEP

    def body(w_ref, g_ref, m_ref, v_ref, d_ref, nm_ref, nv_ref):
        gv = g_ref[...]
        mn = ADAM_B1 * m_ref[...] + (1.0 - ADAM_B1) * gv
        vn = ADAM_B2 * v_ref[...] + (1.0 - ADAM_B2) * (gv * gv)
        d_ref[...] = -ADAM_LR * ((mn / c1) / (jnp.sqrt(vn / c2) + ADAM_EPS) + ADAM_WD * w_ref[...])
        nm_ref[...] = mn
        nv_ref[...] = vn

    blk = pl.BlockSpec((tm, C), lambda i: (i, 0))
    return pl.pallas_call(body, grid=(R // tm,), in_specs=[blk] * 4, out_specs=[blk] * 3,
                          out_shape=[jax.ShapeDtypeStruct((R, C), F32)] * 3, compiler_params=_cparams("parallel"),
                          name=name)(w, g, m, v)


def _sum_lead(a, name, first=None, wire=False):
    n, R, C = a.shape
    tm = _rows(R, 256)
    nin = 1 if first is None else 2

    def body(*refs):
        a_ref = refs[nin - 1]
        s = a_ref[0].astype(F32) if first is None else refs[0][...] + a_ref[0]
        for p in range(1, n):
            s = s + a_ref[p]
        for o_ref in refs[nin:]:
            o_ref[...] = s.astype(o_ref.dtype)

    row = pl.BlockSpec((tm, C), lambda i: (i, 0))
    ins = ([] if first is None else [first]) + [a]
    outs = [jax.ShapeDtypeStruct((R, C), F32)] + ([jax.ShapeDtypeStruct((R, C), _WIRE)] if wire else [])
    res = pl.pallas_call(body, grid=(R // tm,), in_specs=[row] * (nin - 1) + [pl.BlockSpec((n, tm, C), lambda i: (0, i, 0))],
                         out_specs=[row] * len(outs), out_shape=outs, compiler_params=_cparams("parallel"), name=name)(*ins)
    return res if wire else res[0]


_GROUP_BITS = {'c': ((0, 0, 1),), 'xy': ((0, 1, 0), (1, 0, 0), (1, 1, 0)),
               'xyc': tuple((k >> 2 & 1, k >> 1 & 1, k & 1) for k in range(1, 8))}


def _exchange(srcs, *, group, same_src, own, chunks, name):
    flips = _GROUP_BITS[group]
    n = len(flips) + 1
    na = len(srcs)
    blk_shapes = [tuple(s.shape) if same_src else tuple(s.shape[1:]) for s in srcs]
    assert all(bs[0] % chunks == 0 for bs in blk_shapes), blk_shapes

    def body(*refs):
        src_refs, dst_refs = refs[:na], refs[na:2 * na]
        send_sems, recv_sems, loc_sems = refs[2 * na:]
        x, y, c = lax.axis_index("x"), lax.axis_index("y"), lax.axis_index("c")

        def member(px, py, pc):
            return {'c': pc, 'xy': 2 * px + py, 'xyc': 4 * px + 2 * py + pc}[group]

        def piece(ref, a, q):
            rows = blk_shapes[a][0] // chunks
            return ref.at[pl.ds(q * rows, rows)]

        me = member(x, y, c)
        started, arrivals = [], []
        for a in range(na):
            mine = src_refs[a] if same_src else src_refs[a].at[me]
            if own:
                for q in range(chunks):
                    cp = pltpu.make_async_copy(piece(mine, a, q), piece(dst_refs[a].at[me], a, q), loc_sems.at[a * chunks + q])
                    cp.start()
                    started.append(cp.wait)
            for kk, (fx, fy, fc) in enumerate(flips):
                px, py, pc = (1 - x if fx else x), (1 - y if fy else y), (1 - c if fc else c)
                peer = member(px, py, pc)
                out_blk = src_refs[a] if same_src else src_refs[a].at[peer]
                there = dst_refs[a].at[me if own else kk]
                here = dst_refs[a].at[peer if own else kk]
                for q in range(chunks):
                    s = (a * (n - 1) + kk) * chunks + q
                    out = pltpu.make_async_remote_copy(
                        src_ref=piece(out_blk, a, q), dst_ref=piece(there, a, q), send_sem=send_sems.at[s],
                        recv_sem=recv_sems.at[s], device_id=(px, py, pc), device_id_type=pl.DeviceIdType.MESH)
                    out.start()
                    started.append(out.wait_send)
                    arrivals.append(pltpu.make_async_remote_copy(
                        src_ref=piece(mine, a, q), dst_ref=piece(here, a, q), send_sem=send_sems.at[s],
                        recv_sem=recv_sems.at[s], device_id=(px, py, pc), device_id_type=pl.DeviceIdType.MESH).wait_recv)
        for wait in arrivals + started:
            wait()

    nsem = na * (n - 1) * chunks
    hbm = pl.BlockSpec(memory_space=pl.ANY)
    return pl.pallas_call(
        body, in_specs=[hbm] * na, out_specs=[hbm] * na,
        out_shape=[jax.ShapeDtypeStruct(((n if own else n - 1),) + bs, s.dtype) for bs, s in zip(blk_shapes, srcs)],
        scratch_shapes=[pltpu.SemaphoreType.DMA((nsem,)), pltpu.SemaphoreType.DMA((nsem,)),
                        pltpu.SemaphoreType.DMA((na * chunks,))],
        compiler_params=pltpu.CompilerParams(has_side_effects=True), name=name)(*srcs)


def _gather_chips(srcs, name):
    flips = _GROUP_BITS['xy']
    nf = len(flips)
    na = len(srcs)
    halves = [s.shape[0] // 2 for s in srcs]

    def body(*refs):
        src_refs, dst_refs = refs[:na], refs[na:2 * na]
        send_sems, recv_sems, fsend_sems, frecv_sems = refs[2 * na:]
        x, y, c = lax.axis_index("x"), lax.axis_index("y"), lax.axis_index("c")

        def half(ref, a, which):
            return ref.at[pl.ds(pl.multiple_of(which * halves[a], 8), halves[a])]

        waits, landed = [], []
        for a in range(na):
            for kk, (fx, fy, _) in enumerate(flips):
                peer = ((1 - x if fx else x), (1 - y if fy else y), c)
                s = a * nf + kk
                out = pltpu.make_async_remote_copy(
                    src_ref=half(src_refs[a], a, c), dst_ref=half(dst_refs[a].at[kk], a, c), send_sem=send_sems.at[s],
                    recv_sem=recv_sems.at[s], device_id=peer, device_id_type=pl.DeviceIdType.MESH)
                out.start()
                waits.append(out.wait_send)
                landed.append((a, kk, out))
        for a, kk, out in landed:
            out.wait_recv()
            s = a * nf + kk
            fwd = pltpu.make_async_remote_copy(
                src_ref=half(dst_refs[a].at[kk], a, c), dst_ref=half(dst_refs[a].at[kk], a, c), send_sem=fsend_sems.at[s],
                recv_sem=frecv_sems.at[s], device_id=(x, y, 1 - c), device_id_type=pl.DeviceIdType.MESH)
            fwd.start()
            waits.append(fwd.wait_send)
        for a in range(na):
            for kk in range(nf):
                s = a * nf + kk
                pltpu.make_async_remote_copy(
                    src_ref=half(dst_refs[a].at[kk], a, 1 - c), dst_ref=half(dst_refs[a].at[kk], a, 1 - c),
                    send_sem=fsend_sems.at[s], recv_sem=frecv_sems.at[s], device_id=(x, y, 1 - c),
                    device_id_type=pl.DeviceIdType.MESH).wait_recv()
        for wait in waits:
            wait()

    hbm = pl.BlockSpec(memory_space=pl.ANY)
    sems = pltpu.SemaphoreType.DMA((na * nf,))
    return pl.pallas_call(
        body, in_specs=[hbm] * na, out_specs=[hbm] * na,
        out_shape=[jax.ShapeDtypeStruct((nf,) + tuple(s.shape), s.dtype) for s in srcs],
        scratch_shapes=[sems, sems, sems, sems],
        compiler_params=pltpu.CompilerParams(has_side_effects=True), name=name)(*srcs)


def _to_shards(name, full):
    R, C = full.shape
    if name in COL_SPLIT:
        return full.reshape(R, 4, C // 4).transpose(1, 0, 2)
    return full.reshape(4, R // 4, C)


def _from_shards(name, sh):
    n, R, C = sh.shape
    if name in COL_SPLIT:
        return sh.transpose(1, 0, 2).reshape(R, n * C)
    return sh.reshape(n * R, C)


def _pack_rows(parts, width, rows):
    n = parts[0].shape[0]
    flat = jnp.concatenate([p.reshape(n, -1) for p in parts], axis=1)
    return jnp.pad(flat, ((0, 0), (0, rows * width - flat.shape[1]))).reshape(n, rows, width)


def _unpack_rows(buf, shapes):
    n = buf.shape[0]
    flat = buf.reshape(n, -1)
    out, o = [], 0
    for s in shapes:
        sz = math.prod(s)
        out.append(flat[:, o:o + sz].reshape((n,) + tuple(s)))
        o += sz
    return out


def _split_rows(a, rows):
    out, o = [], 0
    for r in rows:
        out.append(a[:, o:o + r])
        o += r
    return out


def _ceil_to(v, m):
    return -(-v // m) * m


def kernel(x, mem, norm_mix_w, w_in, conv_ssd_w, conv_ssd_b, dt_bias, a_log, d_skip, ssd_norm_w, sb_norm_w, w_out, norm_mem_w, norm_memkv_w, w_mq, w_mk, w_mv, w_mo, norm_ffn_w, w_up, conv_ffn_w, conv_ffn_b, w_down, norm_final_w, loss_target, m_norm_mix_w, m_w_in, m_conv_ssd_w, m_conv_ssd_b, m_dt_bias, m_a_log, m_d_skip, m_ssd_norm_w, m_sb_norm_w, m_w_out, m_norm_mem_w, m_norm_memkv_w, m_w_mq, m_w_mk, m_w_mv, m_w_mo, m_norm_ffn_w, m_w_up, m_conv_ffn_w, m_conv_ffn_b, m_w_down, m_norm_final_w, v_norm_mix_w, v_w_in, v_conv_ssd_w, v_conv_ssd_b, v_dt_bias, v_a_log, v_d_skip, v_ssd_norm_w, v_sb_norm_w, v_w_out, v_norm_mem_w, v_norm_memkv_w, v_w_mq, v_w_mk, v_w_mv, v_w_mo, v_norm_ffn_w, v_w_up, v_conv_ffn_w, v_conv_ffn_b, v_w_down, v_norm_final_w):
    W = dict(norm_mix_w=norm_mix_w, w_in=w_in, conv_ssd_w=conv_ssd_w, conv_ssd_b=conv_ssd_b, dt_bias=dt_bias, a_log=a_log,
             d_skip=d_skip, ssd_norm_w=ssd_norm_w, sb_norm_w=sb_norm_w, w_out=w_out, norm_mem_w=norm_mem_w,
             norm_memkv_w=norm_memkv_w, w_mq=w_mq, w_mk=w_mk, w_mv=w_mv, w_mo=w_mo, norm_ffn_w=norm_ffn_w, w_up=w_up,
             conv_ffn_w=conv_ffn_w, conv_ffn_b=conv_ffn_b, w_down=w_down, norm_final_w=norm_final_w)
    Mo = dict(norm_mix_w=m_norm_mix_w, w_in=m_w_in, conv_ssd_w=m_conv_ssd_w, conv_ssd_b=m_conv_ssd_b, dt_bias=m_dt_bias,
              a_log=m_a_log, d_skip=m_d_skip, ssd_norm_w=m_ssd_norm_w, sb_norm_w=m_sb_norm_w, w_out=m_w_out,
              norm_mem_w=m_norm_mem_w, norm_memkv_w=m_norm_memkv_w, w_mq=m_w_mq, w_mk=m_w_mk, w_mv=m_w_mv, w_mo=m_w_mo,
              norm_ffn_w=m_norm_ffn_w, w_up=m_w_up, conv_ffn_w=m_conv_ffn_w, conv_ffn_b=m_conv_ffn_b, w_down=m_w_down,
              norm_final_w=m_norm_final_w)
    Vo = dict(norm_mix_w=v_norm_mix_w, w_in=v_w_in, conv_ssd_w=v_conv_ssd_w, conv_ssd_b=v_conv_ssd_b, dt_bias=v_dt_bias,
              a_log=v_a_log, d_skip=v_d_skip, ssd_norm_w=v_ssd_norm_w, sb_norm_w=v_sb_norm_w, w_out=v_w_out,
              norm_mem_w=v_norm_mem_w, norm_memkv_w=v_norm_memkv_w, w_mq=v_w_mq, w_mk=v_w_mk, w_mv=v_w_mv, w_mo=v_w_mo,
              norm_ffn_w=v_norm_ffn_w, w_up=v_w_up, conv_ffn_w=v_conv_ffn_w, conv_ffn_b=v_conv_ffn_b, w_down=v_w_down,
              norm_final_w=v_norm_final_w)
    shapes = {n: W[n].shape for n in WEIGHTS}
    sh2 = {n: (1, a.shape[-1]) if a.ndim < 3 else a.shape[-2:] for n, a in W.items()}
    w2 = {n: W[n].reshape(sh2[n]) for n in WEIGHTS}
    x2d = x[0]
    S, D = x2d.shape
    H, P, N = SSD_HEADS, SSD_HEAD_DIM, SSD_STATE

    cv_rows = _ceil_to(-(-sum(math.prod(sh2[n]) for n in CONVW) // 128), 32)
    cpack = _pack_rows([w2[n][None] for n in CONVW], 128, cv_rows)[0]
    stacked = jnp.concatenate([w2[n].astype(_MXU) for n in ROW_SPLIT], axis=0)
    cidx = lax.axis_index("c")
    oidx = 2 * lax.axis_index("x") + lax.axis_index("y")
    mine = [stacked, w2['w_in'].astype(_MXU), w2['w_up'].astype(_MXU), cpack]
    others = _gather_chips(mine, "gather_weights")

    def by_owner(own, got):
        by_flip = jnp.concatenate([own[None], got], axis=0)
        return jnp.stack([lax.dynamic_index_in_dim(by_flip, oidx ^ j, 0, keepdims=False) for j in range(4)])

    g_rows, g_in, g_up, call = [by_owner(m, g) for m, g in zip(mine, others)]
    full = {'w_in': _from_shards('w_in', g_in), 'w_up': _from_shards('w_up', g_up)}
    full.update({n: _from_shards(n, a) for n, a in zip(ROW_SPLIT, _split_rows(g_rows, [sh2[n][0] for n in ROW_SPLIT]))})
    full.update({n: _from_shards(n, a) for n, a in zip(CONVW, _unpack_rows(call, [sh2[n] for n in CONVW]))})

    o1 = SSD_INNER
    o2 = o1 + SSD_XBC
    o3 = o2 + SSD_HEADS
    Wi = full['w_in']
    W_z, W_xbc, W_qkv = Wi[:, :o1], Wi[:, o1:o2], Wi[:, o3:]
    W_dt = jnp.pad(Wi[:, o2:o3], ((0, 0), (0, DT_PAD - SSD_HEADS)))
    W_in_r = jnp.concatenate([W_z, W_xbc, W_qkv, W_dt], axis=1)
    dskip_rep = jnp.repeat(w2['d_skip'], P, axis=1)

    h1 = _rms_fwd(x2d, w2['norm_mix_w'], "norm_mix")
    z = _mm(h1, W_z, name="proj_z")
    xbc = _mm(h1, W_xbc, name="proj_xbc")
    dtp = _mm(h1, W_dt, name="proj_dt")
    qkv = _mm(h1, W_qkv, out_dtype=_MXU, name="proj_qkv")
    pre = _dwconv_fwd(xbc, full['conv_ssd_w'], w2['conv_ssd_b'], "ssd_conv")
    act = _silu_fwd(pre, "ssd_conv_silu")
    dt, cs = _ssd_prep(dtp, w2['dt_bias'], w2['a_log'], "ssd_prep")
    csT = cs.T
    def heads(a, nh):
        return a.reshape(S, nh, a.shape[1] // nh).transpose(1, 0, 2)

    def unheads(a):
        return a.transpose(1, 0, 2).reshape(S, a.shape[0] * a.shape[2])

    xs_h = heads(act[:, :o1], H)
    Bm = heads(act[:, o1:o1 + SSD_GROUPS * N], SSD_GROUPS)
    Cm = heads(act[:, o1 + SSD_GROUPS * N:], SSD_GROUPS)
    y_h, prev = _ssdg_fwd(xs_h, Bm, Cm, dt, cs, csT, "ssd_scan")
    y_scan = unheads(y_h)
    y_ssd = _ssd_gate_fwd(y_scan, act, z, dskip_rep, w2['ssd_norm_w'], "ssd_gate")
    nsb = SB_WIDTH // (SB_GROUP * SB_LANES)
    qkv_cols = (0, nsb, 2 * nsb)
    o_sb, y_sb = _sbg_fwd(qkv, qkv, qkv, qkv_cols, w2['sb_norm_w'], "sb_attn")
    ycat = jnp.concatenate([y_ssd, y_sb], axis=1)
    x_2 = _mm(ycat, full['w_out'], res=x2d, name="out_proj")
    h2 = _rms_fwd(x_2, w2['norm_mem_w'], "norm_mem")
    qm = _mm(h2, full['w_mq'], out_dtype=_MXU, name="mem_q")
    mn = _rms_fwd(mem[0], w2['norm_memkv_w'], "norm_memkv")
    km = _mm(mn, full['w_mk'], out_dtype=_MXU, name="mem_k")
    vm = _mm(mn, full['w_mv'], out_dtype=_MXU, name="mem_v")
    om = _xattn_fwd(qm, km, vm, "mem_attn")
    x_3 = _mm(om, full['w_mo'], res=x_2, name="mem_o")
    h3 = _rms_fwd(x_3, w2['norm_ffn_w'], "norm_ffn")
    up = _mm(h3, full['w_up'], name="ffn_up")
    a_ffn = _conv_glu_fwd(up, full['conv_ffn_w'], w2['conv_ffn_b'], "ffn_conv_glu")
    x_4 = _mm(a_ffn, full['w_down'], res=x_3, name="ffn_down")
    dx4, dx4b, g_final, loss_blk = _loss_bwd(x_4, loss_target[0], w2['norm_final_w'], "loss_head")

    G = {'norm_final_w': g_final}
    dact = _mm(dx4b, full['w_down'], tb=True, name="d_ffn_act")
    G['w_down'] = _mm(a_ffn, dx4b, ta=True, name="g_w_down")
    du, G['conv_ffn_w'], G['conv_ffn_b'] = _conv_glu_bwd(up, dact, full['conv_ffn_w'], w2['conv_ffn_b'], "d_ffn_conv_glu")
    dup = _dwconv_bwd_x(du, full['conv_ffn_w'], "d_ffn_conv")
    dh3 = _mm(dup, full['w_up'], tb=True, name="d_h3")
    G['w_up'] = _mm(h3, dup, ta=True, name="g_w_up")
    dx3, dx3b, G['norm_ffn_w'] = _rms_bwd(dh3, x_3, w2['norm_ffn_w'], dx4, "d_norm_ffn")
    dom = _mm(dx3b, full['w_mo'], tb=True, out_dtype=_MXU, name="d_mem_o")
    G['w_mo'] = _mm(om, dx3b, ta=True, name="g_w_mo")
    dqm, dkm, dvm = _xattn_bwd(qm, km, vm, dom, "d_mem_attn")
    G['w_mq'] = _mm(h2, dqm, ta=True, name="g_w_mq")
    dh2 = _mm(dqm, full['w_mq'], tb=True, name="d_h2")
    dx2, dx2b, G['norm_mem_w'] = _rms_bwd(dh2, x_2, w2['norm_mem_w'], dx3, "d_norm_mem")
    G['w_mk'] = _mm(mn, dkm, ta=True, name="g_w_mk")
    G['w_mv'] = _mm(mn, dvm, ta=True, name="g_w_mv")
    dmn = _mm(dvm, full['w_mv'], tb=True, res=_mm(dkm, full['w_mk'], tb=True, name="d_mn_k"), name="d_mn_v")
    _, _, G['norm_memkv_w'] = _rms_bwd(dmn, mem[0], w2['norm_memkv_w'], None, "d_norm_memkv")
    dycat = _mm(dx2b, full['w_out'], tb=True, name="d_ycat")
    G['w_out'] = _mm(ycat, dx2b, ta=True, name="g_w_out")
    dy1, dz, g_dskip_lane, G['ssd_norm_w'] = _ssd_gate_bwd(dycat, y_scan, act, z, dskip_rep, w2['ssd_norm_w'], "d_ssd_gate")
    dxs_h, dB, dC, ddt, dA = _ssdg_bwd(xs_h, Bm, Cm, dt, cs, csT, prev, heads(dy1, H), w2['a_log'], w2['d_skip'], "d_ssd_scan")
    G['d_skip'] = jnp.sum(g_dskip_lane.reshape(H, P), axis=1)[None, :]
    dact_xbc = jnp.concatenate([unheads(dxs_h), unheads(dB), unheads(dC)], axis=1)
    dpre = _silu_bwd(pre, dact_xbc, "d_ssd_conv_silu")
    G['conv_ssd_w'], G['conv_ssd_b'] = _conv_bwd_w(xbc, dpre, full['conv_ssd_w'].shape[0], "g_ssd_conv")
    dxbc = _dwconv_bwd_x(dpre, full['conv_ssd_w'], "d_ssd_conv")
    ddtp, G['dt_bias'], G['a_log'] = _dt_bwd(ddt, dA, dtp, w2['dt_bias'], w2['a_log'], "d_dt")
    dq, dk, dv, G['sb_norm_w'] = _sbg_bwd(qkv, qkv, qkv, qkv_cols, o_sb, dycat, o1 // (SB_GROUP * SB_LANES), w2['sb_norm_w'], "d_sb_attn")
    dproj = jnp.concatenate([dz, dxbc, dq, dk.astype(_MXU), dv.astype(_MXU), ddtp], axis=1)
    dh1 = _mm(dproj, W_in_r, tb=True, name="d_h1")
    g_in_r = _mm(h1, dproj, ta=True, name="g_w_in")
    nq = 3 * SB_WIDTH
    G['w_in'] = jnp.concatenate([g_in_r[:, :o2], g_in_r[:, o2 + nq:o2 + nq + SSD_HEADS], g_in_r[:, o2:o2 + nq]], axis=1)
    grad_x, _, G['norm_mix_w'] = _rms_bwd(dh1, x2d, w2['norm_mix_w'], dx2, "d_norm_mix")

    by_owner = [jnp.concatenate([_to_shards(n, G[n]) for n in ROW_SPLIT], axis=1), _to_shards('w_in', G['w_in']),
                _to_shards('w_up', G['w_up'])]
    to_pair = [a.reshape(4, 2, a.shape[1] // 2, a.shape[2]).transpose(1, 0, 2, 3) for a in by_owner]
    got = _exchange(to_pair, group='c', same_src=False, own=False, chunks=4, name="reduce_pair")
    pair, pair_wire = [], []
    for t, g in zip(to_pair, got):
        _, _, r, cw = t.shape
        mine = lax.dynamic_index_in_dim(t, cidx, 0, keepdims=False).reshape(4 * r, cw)
        full_sum, wire_sum = _sum_lead(g.reshape(1, 4 * r, cw), "reduce_pair_sum%d" % len(pair), first=mine, wire=True)
        pair.append(full_sum.reshape(4, r, cw))
        pair_wire.append(wire_sum.reshape(4, r, cw))
    got = _exchange(pair_wire, group='xy', same_src=False, own=False, chunks=1, name="reduce_chips")
    chips = [_sum_lead(g, "reduce_chips_sum%d" % k, first=lax.dynamic_index_in_dim(p, oidx, 0, keepdims=False))
             for k, (p, g) in enumerate(zip(pair, got))]
    got = _exchange(chips, group='c', same_src=True, own=False, chunks=4, name="share_pair")
    red = [jnp.where(cidx == 0, jnp.concatenate([m, g[0]], axis=0), jnp.concatenate([g[0], m], axis=0))[None]
           for m, g in zip(chips, got)]
    gsh = dict(zip(ROW_SPLIT, [a[0] for a in _split_rows(red[0], [sh2[n][0] for n in ROW_SPLIT])]))
    gsh['w_in'], gsh['w_up'] = red[1][0], red[2][0]

    small_parts = [G[n].reshape(1, -1) for n in SMALL + CONVW] + [loss_blk[:1, :1]]
    small_shapes = [sh2[n] for n in SMALL] + [G[n].shape for n in CONVW] + [(1, 1)]
    small_rows = _ceil_to(-(-sum(math.prod(s) for s in small_shapes) // 128), 8)
    spack = _pack_rows(small_parts, 128, small_rows)[0]
    (gathered,) = _exchange([spack], group='xyc', same_src=True, own=True, chunks=1, name="gather_small")
    parts = [a[0] for a in _unpack_rows(_sum_lead(gathered, "small_sum")[None], small_shapes)]
    gsh.update(zip(SMALL, parts))
    for n, a in zip(CONVW, parts[len(SMALL):-1]):
        gsh[n] = lax.dynamic_index_in_dim(_to_shards(n, a), oidx, 0, keepdims=False)
    loss = parts[-1].reshape(())

    delta, new_m, new_v = {}, {}, {}
    for n in BIG:
        delta[n], new_m[n], new_v[n] = _adamw(w2[n], gsh[n], Mo[n].reshape(sh2[n]), Vo[n].reshape(sh2[n]), "adamw_" + n)
    for grp, width, tag in ((CONVW, 128, "adamw_conv"), (SMALL, 128, "adamw_small")):
        rows = _ceil_to(-(-sum(math.prod(sh2[n]) for n in grp) // width), 8)
        packed = [_pack_rows([src[n].reshape(1, -1) for n in grp], width, rows)[0]
                  for src in (w2, gsh, {n: Mo[n] for n in grp}, {n: Vo[n] for n in grp})]
        outs = _adamw(*packed, tag)
        for dst, o in zip((delta, new_m, new_v), outs):
            dst.update(zip(grp, [a[0] for a in _unpack_rows(o[None], [sh2[n] for n in grp])]))

    def shaped(d):
        return [d[n].reshape(shapes[n]) for n in WEIGHTS]

    return (loss, grad_x[None], *shaped(gsh), *shaped(delta), *shaped(new_m), *shaped(new_v))
```

```python
import math

import jax
import jax.numpy as jnp
from jax import lax
from jax.experimental import pallas as pl
from jax.experimental.pallas import tpu as pltpu

F32 = jnp.float32
_MXU = jnp.bfloat16
_WIRE = jnp.bfloat16
EPS = 1e-6
_VMEM_LIMIT = 48 * 1024 * 1024
_HI = lax.Precision.HIGHEST

SSD_HEADS = 16
SSD_HEAD_DIM = 64
SSD_GROUPS = 2
SSD_STATE = 128
SSD_CHUNK = 128
SSD_INNER = SSD_HEADS * SSD_HEAD_DIM
SSD_XBC = SSD_INNER + 2 * SSD_GROUPS * SSD_STATE
SB_HEADS = 16
SB_HEAD_DIM = 64
SB_WIDTH = SB_HEADS * SB_HEAD_DIM
MEM_HEADS = 4
DT_PAD = 128

ADAM_LR = 0.001
ADAM_B1 = 0.9
ADAM_B2 = 0.999
ADAM_EPS = 1e-08
ADAM_WD = 0.01
ADAM_STEP = 10

WEIGHTS = ['norm_mix_w', 'w_in', 'conv_ssd_w', 'conv_ssd_b', 'dt_bias', 'a_log', 'd_skip', 'ssd_norm_w',
           'sb_norm_w', 'w_out', 'norm_mem_w', 'norm_memkv_w', 'w_mq', 'w_mk', 'w_mv', 'w_mo', 'norm_ffn_w',
           'w_up', 'conv_ffn_w', 'conv_ffn_b', 'w_down', 'norm_final_w']
BIG = ['w_in', 'w_out', 'w_mq', 'w_mk', 'w_mv', 'w_mo', 'w_up', 'w_down']
COL_SPLIT = ('w_in', 'w_up', 'conv_ssd_w', 'conv_ffn_w')
ROW_SPLIT = ['w_out', 'w_mq', 'w_mk', 'w_mv', 'w_mo', 'w_down']
CONVW = ['conv_ssd_w', 'conv_ffn_w']
SMALL = ['norm_mix_w', 'conv_ssd_b', 'dt_bias', 'a_log', 'd_skip', 'ssd_norm_w', 'sb_norm_w', 'norm_mem_w',
         'norm_memkv_w', 'norm_ffn_w', 'conv_ffn_b', 'norm_final_w']


def _cparams(*sem):
    return pltpu.CompilerParams(dimension_semantics=sem if sem else None, vmem_limit_bytes=_VMEM_LIMIT)


def _pick(n, cap, mult=128):
    best = None
    for d in range(mult, min(n, cap) + 1, mult):
        if n % d == 0:
            best = d
    return n if best is None else best


def _dot(a, b, ca, cb):
    return lax.dot_general(a.astype(_MXU), b.astype(_MXU), (((ca,), (cb,)), ((), ())), preferred_element_type=F32)


def _sigmoid(v):
    return 1.0 / (1.0 + jnp.exp(-v))


def _log1p(u):
    w = 1.0 + u
    return jnp.where(w == 1.0, u, jnp.log(w) * (u / (w - 1.0)))


def _mm(a, b, *, ta=False, tb=False, res=None, out_dtype=F32, name):
    if ta:
        K, M = a.shape
    else:
        M, K = a.shape
    if tb:
        N, K2 = b.shape
    else:
        K2, N = b.shape
    assert K == K2, (a.shape, b.shape)
    tm = _pick(M, 1408, 128 if ta else 16)
    tn = _pick(N, 1536)
    tk = _pick(K, 1536)
    nk = K // tk
    a_spec = pl.BlockSpec((tk, tm), lambda i, j, k: (k, i)) if ta else pl.BlockSpec((tm, tk), lambda i, j, k: (i, k))
    b_spec = pl.BlockSpec((tn, tk), lambda i, j, k: (j, k)) if tb else pl.BlockSpec((tk, tn), lambda i, j, k: (k, j))
    o_spec = pl.BlockSpec((tm, tn), lambda i, j, k: (i, j))
    ca, cb = (0 if ta else 1), (1 if tb else 0)

    def body(*refs):
        if res is None:
            a_ref, b_ref, o_ref, acc_ref = refs
            r_ref = None
        else:
            a_ref, b_ref, r_ref, o_ref, acc_ref = refs
        k = pl.program_id(2)

        @pl.when(k == 0)
        def _():
            acc_ref[...] = jnp.zeros_like(acc_ref)

        acc_ref[...] += _dot(a_ref[...], b_ref[...], ca, cb)

        @pl.when(k == nk - 1)
        def _():
            r = acc_ref[...]
            if r_ref is not None:
                r = r + r_ref[...].astype(F32)
            o_ref[...] = r.astype(o_ref.dtype)

    ins = [a, b] + ([] if res is None else [res])
    in_specs = [a_spec, b_spec] + ([] if res is None else [o_spec])
    return pl.pallas_call(
        body, grid=(M // tm, N // tn, nk), in_specs=in_specs, out_specs=o_spec,
        out_shape=jax.ShapeDtypeStruct((M, N), out_dtype), scratch_shapes=[pltpu.VMEM((tm, tn), F32)],
        compiler_params=_cparams("parallel", "parallel", "arbitrary"), name=name)(*ins)


def _rows(S, cap):
    return _pick(S, cap, 8)


def _rms_fwd(x, w, name):
    S, D = x.shape
    tm = _rows(S, 512)

    def body(x_ref, w_ref, o_ref):
        xv = x_ref[...]
        r = lax.rsqrt(jnp.mean(xv * xv, axis=-1, keepdims=True) + EPS)
        o_ref[...] = ((xv * r) * w_ref[...]).astype(o_ref.dtype)

    row = pl.BlockSpec((tm, D), lambda i: (i, 0))
    return pl.pallas_call(body, grid=(S // tm,), in_specs=[row, pl.BlockSpec((1, D), lambda i: (0, 0))], out_specs=row,
                          out_shape=jax.ShapeDtypeStruct((S, D), _MXU), compiler_params=_cparams("parallel"), name=name)(x, w)


def _rms_bwd(dh, x, w, dres, name):
    S, D = x.shape
    tm = _rows(S, 256)

    def body(*refs):
        if dres is None:
            dh_ref, x_ref, w_ref, dx_ref, dxb_ref, dw_ref = refs
            dres_ref = None
        else:
            dh_ref, x_ref, w_ref, dres_ref, dx_ref, dxb_ref, dw_ref = refs
        xv = x_ref[...]
        r = lax.rsqrt(jnp.mean(xv * xv, axis=-1, keepdims=True) + EPS)
        xn = xv * r
        dy = dh_ref[...].astype(F32)

        @pl.when(pl.program_id(0) == 0)
        def _():
            dw_ref[...] = jnp.zeros_like(dw_ref)

        dw_ref[...] += jnp.sum(dy * xn, axis=0, keepdims=True)
        dxn = dy * w_ref[...]
        dx = r * (dxn - xn * jnp.mean(dxn * xn, axis=-1, keepdims=True))
        if dres_ref is not None:
            dx = dx + dres_ref[...]
        dx_ref[...] = dx
        dxb_ref[...] = dx.astype(dxb_ref.dtype)

    row = pl.BlockSpec((tm, D), lambda i: (i, 0))
    vec = pl.BlockSpec((1, D), lambda i: (0, 0))
    ins = [dh, x, w] + ([] if dres is None else [dres])
    in_specs = [row, row, vec] + ([] if dres is None else [row])
    return pl.pallas_call(
        body, grid=(S // tm,), in_specs=in_specs, out_specs=[row, row, vec],
        out_shape=[jax.ShapeDtypeStruct((S, D), F32), jax.ShapeDtypeStruct((S, D), _MXU), jax.ShapeDtypeStruct((1, D), F32)],
        compiler_params=_cparams("arbitrary"), name=name)(*ins)


def _loss_bwd(x, tgt, w, name):
    S, D = x.shape
    tm = _rows(S, 256)

    def body(x_ref, t_ref, w_ref, dx_ref, dxb_ref, dw_ref, loss_ref):
        xv = x_ref[...]
        r = lax.rsqrt(jnp.mean(xv * xv, axis=-1, keepdims=True) + EPS)
        xn = xv * r
        e = xn * w_ref[...] - t_ref[...]

        @pl.when(pl.program_id(0) == 0)
        def _():
            dw_ref[...] = jnp.zeros_like(dw_ref)
            loss_ref[...] = jnp.zeros_like(loss_ref)

        tok = jnp.mean(e * e, axis=-1, keepdims=True)
        loss_ref[...] += jnp.broadcast_to(0.5 * jnp.sum(tok, axis=0, keepdims=True), loss_ref.shape)
        dy = e * (1.0 / D)
        dw_ref[...] += jnp.sum(dy * xn, axis=0, keepdims=True)
        dxn = dy * w_ref[...]
        dx = r * (dxn - xn * jnp.mean(dxn * xn, axis=-1, keepdims=True))
        dx_ref[...] = dx
        dxb_ref[...] = dx.astype(dxb_ref.dtype)

    row = pl.BlockSpec((tm, D), lambda i: (i, 0))
    vec = pl.BlockSpec((1, D), lambda i: (0, 0))
    return pl.pallas_call(
        body, grid=(S // tm,), in_specs=[row, row, vec],
        out_specs=[row, row, vec, pl.BlockSpec((8, 128), lambda i: (0, 0))],
        out_shape=[jax.ShapeDtypeStruct((S, D), F32), jax.ShapeDtypeStruct((S, D), _MXU),
                   jax.ShapeDtypeStruct((1, D), F32), jax.ShapeDtypeStruct((8, 128), F32)],
        compiler_params=_cparams("arbitrary"), name=name)(x, tgt, w)


def _conv_tiles(S, C):
    return _rows(S, 256), _pick(C, 1536)


def _dwconv_fwd(x, w, b, name):
    S, C = x.shape
    K = w.shape[0]
    tm, tc = _conv_tiles(S, C)

    def body(x_ref, p_ref, w_ref, b_ref, o_ref):
        cur = x_ref[...]
        prev = jnp.where(pl.program_id(0) > 0, p_ref[...], 0.0)
        xx = jnp.concatenate([prev, cur], axis=0)
        acc = cur * w_ref[K - 1:K, :] + b_ref[...]
        for d in range(1, K):
            acc = acc + pltpu.roll(xx, d, 0)[8:, :] * w_ref[K - 1 - d:K - d, :]
        o_ref[...] = acc

    return pl.pallas_call(
        body, grid=(S // tm, C // tc),
        in_specs=[pl.BlockSpec((tm, tc), lambda i, j: (i, j)),
                  pl.BlockSpec((8, tc), lambda i, j: (jnp.maximum(i * (tm // 8) - 1, 0), j)),
                  pl.BlockSpec((K, tc), lambda i, j: (0, j)), pl.BlockSpec((1, tc), lambda i, j: (0, j))],
        out_specs=pl.BlockSpec((tm, tc), lambda i, j: (i, j)), out_shape=jax.ShapeDtypeStruct((S, C), F32),
        compiler_params=_cparams("parallel", "parallel"), name=name)(x, x, w, b)


def _conv_bwd_w(x, dy, K, name):
    S, C = x.shape
    tm, tc = _conv_tiles(S, C)

    def body(x_ref, p_ref, dy_ref, dw_ref, db_ref):
        i = pl.program_id(1)

        @pl.when(i == 0)
        def _():
            dw_ref[...] = jnp.zeros_like(dw_ref)
            db_ref[...] = jnp.zeros_like(db_ref)

        cur = x_ref[...]
        prev = jnp.where(i > 0, p_ref[...], 0.0)
        xx = jnp.concatenate([prev, cur], axis=0)
        g = dy_ref[...].astype(F32)
        db_ref[...] += jnp.sum(g, axis=0, keepdims=True)
        dw_ref[K - 1:K, :] += jnp.sum(g * cur, axis=0, keepdims=True)
        for d in range(1, K):
            dw_ref[K - 1 - d:K - d, :] += jnp.sum(g * pltpu.roll(xx, d, 0)[8:, :], axis=0, keepdims=True)

    return pl.pallas_call(
        body, grid=(C // tc, S // tm),
        in_specs=[pl.BlockSpec((tm, tc), lambda j, i: (i, j)),
                  pl.BlockSpec((8, tc), lambda j, i: (jnp.maximum(i * (tm // 8) - 1, 0), j)),
                  pl.BlockSpec((tm, tc), lambda j, i: (i, j))],
        out_specs=[pl.BlockSpec((K, tc), lambda j, i: (0, j)), pl.BlockSpec((1, tc), lambda j, i: (0, j))],
        out_shape=[jax.ShapeDtypeStruct((K, C), F32), jax.ShapeDtypeStruct((1, C), F32)],
        compiler_params=_cparams("parallel", "arbitrary"), name=name)(x, x, dy)


def _dwconv_bwd_x(dy, w, name):
    S, C = dy.shape
    K = w.shape[0]
    tm, tc = _conv_tiles(S, C)
    last = S // tm - 1
    hr = 8 * (4 // dy.dtype.itemsize)

    def body(g_ref, n_ref, w_ref, o_ref):
        cur = g_ref[...].astype(F32)
        nxt = jnp.where(pl.program_id(0) < last, n_ref[...].astype(F32), 0.0)
        xx = jnp.concatenate([cur, nxt], axis=0)
        acc = cur * w_ref[K - 1:K, :]
        for d in range(1, K):
            acc = acc + pltpu.roll(xx, tm + hr - d, 0)[:tm, :] * w_ref[K - 1 - d:K - d, :]
        o_ref[...] = acc.astype(o_ref.dtype)

    return pl.pallas_call(
        body, grid=(S // tm, C // tc),
        in_specs=[pl.BlockSpec((tm, tc), lambda i, j: (i, j)),
                  pl.BlockSpec((hr, tc), lambda i, j: (jnp.minimum((i + 1) * (tm // hr), S // hr - 1), j)),
                  pl.BlockSpec((K, tc), lambda i, j: (0, j))],
        out_specs=pl.BlockSpec((tm, tc), lambda i, j: (i, j)), out_shape=jax.ShapeDtypeStruct((S, C), _MXU),
        compiler_params=_cparams("parallel", "parallel"), name=name)(dy, dy, w)


def _silu_fwd(pre, name):
    S, C = pre.shape
    tm, tc = _conv_tiles(S, C)

    def body(p_ref, o_ref):
        p = p_ref[...]
        o_ref[...] = p * _sigmoid(p)

    blk = pl.BlockSpec((tm, tc), lambda i, j: (i, j))
    return pl.pallas_call(body, grid=(S // tm, C // tc), in_specs=[blk], out_specs=blk,
                          out_shape=jax.ShapeDtypeStruct((S, C), F32), compiler_params=_cparams("parallel", "parallel"),
                          name=name)(pre)


def _silu_bwd(pre, dact, name):
    S, C = pre.shape
    tm, tc = _conv_tiles(S, C)

    def body(p_ref, g_ref, o_ref):
        p = p_ref[...]
        s = _sigmoid(p)
        o_ref[...] = g_ref[...] * (s * (1.0 + p * (1.0 - s)))

    blk = pl.BlockSpec((tm, tc), lambda i, j: (i, j))
    return pl.pallas_call(body, grid=(S // tm, C // tc), in_specs=[blk, blk], out_specs=blk,
                          out_shape=jax.ShapeDtypeStruct((S, C), F32), compiler_params=_cparams("parallel", "parallel"),
                          name=name)(pre, dact)


def _shifted_rows(cur, prev, K):
    xx = jnp.concatenate([prev, cur], axis=0)
    return [cur] + [pltpu.roll(xx, d, 0)[8:, :] for d in range(1, K)]


def _conv_glu_fwd(x, w, b, name):
    S, C = x.shape
    K = w.shape[0]
    Fh = C // 2
    tm = _rows(S, 128)

    def body(x_ref, p_ref, w_ref, b_ref, o_ref):
        sh = _shifted_rows(x_ref[...], jnp.where(pl.program_id(0) > 0, p_ref[...], 0.0), K)
        u = b_ref[...] + sum(sh[d] * w_ref[K - 1 - d:K - d, :] for d in range(K))
        g = u[:, :Fh]
        o_ref[...] = (g * _sigmoid(g) * u[:, Fh:]).astype(o_ref.dtype)

    return pl.pallas_call(
        body, grid=(S // tm,),
        in_specs=[pl.BlockSpec((tm, C), lambda i: (i, 0)), pl.BlockSpec((8, C), lambda i: (jnp.maximum(i * (tm // 8) - 1, 0), 0)),
                  pl.BlockSpec((K, C), lambda i: (0, 0)), pl.BlockSpec((1, C), lambda i: (0, 0))],
        out_specs=pl.BlockSpec((tm, Fh), lambda i: (i, 0)), out_shape=jax.ShapeDtypeStruct((S, Fh), _MXU),
        compiler_params=_cparams("parallel"), name=name)(x, x, w, b)


def _conv_glu_bwd(x, dact, w, b, name):
    S, C = x.shape
    K = w.shape[0]
    Fh = C // 2
    tm = _rows(S, 128)

    def body(x_ref, p_ref, g_ref, w_ref, b_ref, du_ref, dw_ref, db_ref):
        i = pl.program_id(0)

        @pl.when(i == 0)
        def _():
            dw_ref[...] = jnp.zeros_like(dw_ref)
            db_ref[...] = jnp.zeros_like(db_ref)

        sh = _shifted_rows(x_ref[...], jnp.where(i > 0, p_ref[...], 0.0), K)
        u = b_ref[...] + sum(sh[d] * w_ref[K - 1 - d:K - d, :] for d in range(K))
        g = u[:, :Fh]
        da = g_ref[...].astype(F32)
        s = _sigmoid(g)
        halves = ((slice(0, Fh), da * u[:, Fh:] * (s * (1.0 + g * (1.0 - s)))), (slice(Fh, C), da * (g * s)))
        for cols, du in halves:
            du_ref[:, cols] = du.astype(du_ref.dtype)
            db_ref[:, cols] += jnp.sum(du, axis=0, keepdims=True)
            for d in range(K):
                dw_ref[K - 1 - d:K - d, cols] += jnp.sum(du * sh[d][:, cols], axis=0, keepdims=True)

    return pl.pallas_call(
        body, grid=(S // tm,),
        in_specs=[pl.BlockSpec((tm, C), lambda i: (i, 0)), pl.BlockSpec((8, C), lambda i: (jnp.maximum(i * (tm // 8) - 1, 0), 0)),
                  pl.BlockSpec((tm, Fh), lambda i: (i, 0)), pl.BlockSpec((K, C), lambda i: (0, 0)), pl.BlockSpec((1, C), lambda i: (0, 0))],
        out_specs=[pl.BlockSpec((tm, C), lambda i: (i, 0)), pl.BlockSpec((K, C), lambda i: (0, 0)), pl.BlockSpec((1, C), lambda i: (0, 0))],
        out_shape=[jax.ShapeDtypeStruct((S, C), _MXU), jax.ShapeDtypeStruct((K, C), F32), jax.ShapeDtypeStruct((1, C), F32)],
        compiler_params=_cparams("arbitrary"), name=name)(x, x, dact, w, b)


def _xattn_fwd(q, k, v, name):
    S, D = q.shape
    M = k.shape[0]
    hd = D // MEM_HEADS
    tm = _rows(S, 512)
    scale = 1.0 / math.sqrt(hd)

    def body(q_ref, k_ref, v_ref, o_ref):
        for h in range(MEM_HEADS):
            sl = slice(h * hd, (h + 1) * hd)
            s = _dot(q_ref[:, sl], k_ref[:, sl], 1, 1) * scale
            p = jnp.exp(s - jnp.max(s, axis=-1, keepdims=True))
            p = p / jnp.sum(p, axis=-1, keepdims=True)
            o_ref[:, sl] = _dot(p, v_ref[:, sl], 1, 0).astype(o_ref.dtype)

    kv = pl.BlockSpec((M, D), lambda i: (0, 0))
    row = pl.BlockSpec((tm, D), lambda i: (i, 0))
    return pl.pallas_call(body, grid=(S // tm,), in_specs=[row, kv, kv], out_specs=row,
                          out_shape=jax.ShapeDtypeStruct((S, D), _MXU), compiler_params=_cparams("parallel"), name=name)(q, k, v)


def _xattn_bwd(q, k, v, do, name):
    S, D = q.shape
    M = k.shape[0]
    hd = D // MEM_HEADS
    tm = _rows(S, 512)
    scale = 1.0 / math.sqrt(hd)

    def body(q_ref, k_ref, v_ref, do_ref, dq_ref, dk_ref, dv_ref):
        @pl.when(pl.program_id(0) == 0)
        def _():
            dk_ref[...] = jnp.zeros_like(dk_ref)
            dv_ref[...] = jnp.zeros_like(dv_ref)

        for h in range(MEM_HEADS):
            sl = slice(h * hd, (h + 1) * hd)
            qh, kh, vh, doh = q_ref[:, sl], k_ref[:, sl], v_ref[:, sl], do_ref[:, sl]
            s = _dot(qh, kh, 1, 1) * scale
            p = jnp.exp(s - jnp.max(s, axis=-1, keepdims=True))
            p = p / jnp.sum(p, axis=-1, keepdims=True)
            dp = _dot(doh, vh, 1, 1)
            dv_ref[:, sl] += _dot(p, doh, 0, 0)
            ds = (p * (dp - jnp.sum(dp * p, axis=-1, keepdims=True))) * scale
            dq_ref[:, sl] = _dot(ds, kh, 1, 0).astype(dq_ref.dtype)
            dk_ref[:, sl] += _dot(ds, qh, 0, 0)

    kv = pl.BlockSpec((M, D), lambda i: (0, 0))
    row = pl.BlockSpec((tm, D), lambda i: (i, 0))
    return pl.pallas_call(
        body, grid=(S // tm,), in_specs=[row, kv, kv, row], out_specs=[row, kv, kv],
        out_shape=[jax.ShapeDtypeStruct((S, D), _MXU), jax.ShapeDtypeStruct((M, D), F32), jax.ShapeDtypeStruct((M, D), F32)],
        compiler_params=_cparams("arbitrary"), name=name)(q, k, v, do)


def _tri(n, strict, upper):
    r = lax.broadcasted_iota(jnp.int32, (n, n), 0)
    c = lax.broadcasted_iota(jnp.int32, (n, n), 1)
    if upper:
        return (c > r) if strict else (c >= r)
    return (r > c) if strict else (r >= c)


def _ssd_prep(dtp, dt_bias, a_log, name):
    S = dtp.shape[0]
    L, H = SSD_CHUNK, SSD_HEADS

    def body(p_ref, b_ref, al_ref, dt_ref, cs_ref):
        v = p_ref[:, :H] + b_ref[...]
        dt = jnp.maximum(v, 0.0) + _log1p(jnp.exp(-jnp.abs(v)))
        dt_ref[...] = dt
        a = dt * (-jnp.exp(al_ref[...]))
        cs_ref[...] = jnp.dot(_tri(L, False, False).astype(F32), a, precision=_HI, preferred_element_type=F32)

    blk = pl.BlockSpec((L, H), lambda c: (c, 0))
    vec = pl.BlockSpec((1, H), lambda c: (0, 0))
    return pl.pallas_call(body, grid=(S // L,), in_specs=[pl.BlockSpec((L, DT_PAD), lambda c: (c, 0)), vec, vec],
                          out_specs=[blk, blk], out_shape=[jax.ShapeDtypeStruct((S, H), F32)] * 2,
                          compiler_params=_cparams("parallel"), name=name)(dtp, dt_bias, a_log)


def _head_col(blk_ref, h):
    sel = lax.broadcasted_iota(jnp.int32, (1, SSD_HEADS), 1) == h
    return jnp.sum(jnp.where(sel, blk_ref[...], 0.0), axis=1, keepdims=True)


def _ssdg_fwd(xs, Bm, Cm, dt, cs, csT, name):
    H, S, P = xs.shape
    L, N = SSD_CHUNK, SSD_STATE
    nc = S // L
    rep = H // SSD_GROUPS
    hs = range(rep)

    def body(x_ref, b_ref, c_ref, dt_ref, cs_ref, csT_ref, y_ref, prev_ref, st_ref):
        c, g = pl.program_id(0), pl.program_id(1)

        @pl.when(c == 0)
        def _():
            for hh in hs:
                st_ref[g * rep + hh] = jnp.zeros((P, N), F32)

        Bv, Cv = b_ref[...], c_ref[...]
        tril = _tri(L, False, False)
        dtc = [_head_col(dt_ref, g * rep + hh) for hh in hs]
        csc = [_head_col(cs_ref, g * rep + hh) for hh in hs]
        csr = [csT_ref[hh:hh + 1, :] for hh in hs]
        last = [r[:, L - 1:L] for r in csr]
        xc = [x_ref[hh] * dtc[hh] for hh in hs]
        cb = _dot(Cv, Bv, 1, 1)
        m = [cb * jnp.where(tril, jnp.exp(jnp.where(tril, csc[hh] - csr[hh], 0.0)), 0.0) for hh in hs]
        prev = [st_ref[g * rep + hh] for hh in hs]
        yd = [_dot(m[hh], xc[hh], 1, 0) for hh in hs]
        yo = [_dot(Cv, prev[hh], 1, 1) for hh in hs]
        new = [_dot(xc[hh] * jnp.exp(last[hh] - csc[hh]), Bv, 0, 0) for hh in hs]
        for hh in hs:
            y_ref[hh] = yd[hh] + yo[hh] * jnp.exp(csc[hh])
            prev_ref[hh] = prev[hh]
            st_ref[g * rep + hh] = prev[hh] * jnp.exp(last[hh]) + new[hh]

    tok = pl.BlockSpec((L, H), lambda c, g: (c, 0))
    return pl.pallas_call(
        body, grid=(nc, SSD_GROUPS),
        in_specs=[pl.BlockSpec((rep, L, P), lambda c, g: (g, c, 0)), pl.BlockSpec((None, L, N), lambda c, g: (g, c, 0)),
                  pl.BlockSpec((None, L, N), lambda c, g: (g, c, 0)), tok, tok, pl.BlockSpec((rep, L), lambda c, g: (g, c))],
        out_specs=[pl.BlockSpec((rep, L, P), lambda c, g: (g, c, 0)),
                   pl.BlockSpec((rep, None, P, N), lambda c, g: (g, c, 0, 0))],
        out_shape=[jax.ShapeDtypeStruct((H, S, P), F32), jax.ShapeDtypeStruct((H, nc, P, N), F32)],
        scratch_shapes=[pltpu.VMEM((H, P, N), F32)],
        compiler_params=_cparams("arbitrary", "arbitrary"), name=name)(xs, Bm, Cm, dt, cs, csT)


def _ssdg_bwd(xs, Bm, Cm, dt, cs, csT, prev, dy, a_log, d_skip, name):
    H, S, P = xs.shape
    L, N = SSD_CHUNK, SSD_STATE
    nc = S // L
    rep = H // SSD_GROUPS
    hs = range(rep)

    def rowsum(a):
        return jnp.sum(a, axis=1, keepdims=True)

    def body(x_ref, b_ref, c_ref, dt_ref, cs_ref, csT_ref, prev_ref, dy_ref, al_ref, dk_ref,
             dx_ref, db_ref, dc_ref, ddt_ref, da_ref, g_ref):
        ci, g = pl.program_id(0), pl.program_id(1)

        @pl.when(ci == 0)
        def _():
            for hh in hs:
                g_ref[g * rep + hh] = jnp.zeros((P, N), F32)

        @pl.when((ci == 0) & (g == 0))
        def _():
            da_ref[...] = jnp.zeros_like(da_ref)

        @pl.when(g == 0)
        def _():
            ddt_ref[...] = jnp.zeros_like(ddt_ref)

        lane = lax.broadcasted_iota(jnp.int32, (1, H), 1)
        sel = [lane == g * rep + hh for hh in hs]
        A_h = [-jnp.exp(rowsum(jnp.where(s, al_ref[...], 0.0))) for s in sel]
        dsk = [rowsum(jnp.where(s, dk_ref[...], 0.0)) for s in sel]
        dtc = [_head_col(dt_ref, g * rep + hh) for hh in hs]
        csc = [_head_col(cs_ref, g * rep + hh) for hh in hs]
        csr = [csT_ref[hh:hh + 1, :] for hh in hs]
        last = [r[:, L - 1:L] for r in csr]
        Bv, Cv = b_ref[...], c_ref[...]
        xv = [x_ref[hh] for hh in hs]
        xc = [xv[hh] * dtc[hh] for hh in hs]
        dY = [dy_ref[hh] for hh in hs]
        prv = [prev_ref[hh] for hh in hs]
        G = [g_ref[g * rep + hh] for hh in hs]
        ecs = [jnp.exp(v) for v in csc]
        w = [jnp.exp(last[hh] - csc[hh]) for hh in hs]
        cd = [jnp.exp(v) for v in last]
        tril = _tri(L, False, False)
        triu = _tri(L, False, True)
        lam = [jnp.where(tril, jnp.exp(jnp.where(tril, csc[hh] - csr[hh], 0.0)), 0.0) for hh in hs]
        lamT = [jnp.where(triu, jnp.exp(jnp.where(triu, csr[hh] - csc[hh], 0.0)), 0.0) for hh in hs]
        cb = _dot(Cv, Bv, 1, 1)
        bc = _dot(Bv, Cv, 1, 1)
        dM = [_dot(dY[hh], xc[hh], 1, 1) for hh in hs]
        dMT = [_dot(xc[hh], dY[hh], 1, 1) for hh in hs]
        cp = [_dot(Cv, prv[hh], 1, 1) for hh in hs]
        BG = [_dot(Bv, G[hh], 1, 1) for hh in hs]
        dYe = [dY[hh] * ecs[hh] for hh in hs]
        dprev = [_dot(dYe[hh], Cv, 0, 0) for hh in hs]
        m = [cb * lam[hh] for hh in hs]
        mT = [bc * lamT[hh] for hh in hs]
        dxc = [_dot(mT[hh], dY[hh], 1, 0) + w[hh] * BG[hh] for hh in hs]
        dcb = sum([dM[hh] * lam[hh] for hh in hs][1:], dM[0] * lam[0])
        dcbT = sum([dMT[hh] * lamT[hh] for hh in hs][1:], dMT[0] * lamT[0])
        dC = _dot(dcb, Bv, 1, 0)
        dB = _dot(dcbT, Cv, 1, 0)
        for hh in hs:
            dC = dC + _dot(dYe[hh], prv[hh], 1, 0)
            dB = dB + _dot(xc[hh] * w[hh], G[hh], 1, 0)
        dc_ref[...] = dC
        db_ref[...] = dB
        ddt_acc = jnp.zeros((L, H), F32)
        da_acc = jnp.zeros((1, H), F32)
        rev = _tri(L, False, True).astype(F32)
        for hh in hs:
            dww = rowsum(xc[hh] * BG[hh]) * w[hh]
            dcs = (rowsum(dM[hh] * m[hh]) - rowsum(dMT[hh] * mT[hh]) + rowsum(dY[hh] * (cp[hh] * ecs[hh])) - dww)
            extra = jnp.sum(dww, axis=0, keepdims=True) + cd[hh] * jnp.sum(rowsum(G[hh] * prv[hh]), axis=0, keepdims=True)
            g_ref[g * rep + hh] = G[hh] * cd[hh] + dprev[hh]
            da = jnp.dot(rev, dcs, precision=_HI, preferred_element_type=F32) + extra
            dx_ref[hh] = dxc[hh] * dtc[hh] + dY[hh] * dsk[hh]
            ddt_acc = ddt_acc + jnp.where(sel[hh], da * A_h[hh] + rowsum(dxc[hh] * xv[hh]), 0.0)
            da_acc = da_acc + jnp.where(sel[hh], jnp.sum(da * dtc[hh], axis=0, keepdims=True), 0.0)
        ddt_ref[...] += ddt_acc
        da_ref[...] += da_acc

    rc = lambda ci: nc - 1 - ci
    hd = pl.BlockSpec((rep, L, P), lambda ci, g: (g, rc(ci), 0))
    grp = pl.BlockSpec((None, L, N), lambda ci, g: (g, rc(ci), 0))
    tok = pl.BlockSpec((L, H), lambda ci, g: (rc(ci), 0))
    vec = pl.BlockSpec((1, H), lambda ci, g: (0, 0))
    return pl.pallas_call(
        body, grid=(nc, SSD_GROUPS),
        in_specs=[hd, grp, grp, tok, tok, pl.BlockSpec((rep, L), lambda ci, g: (g, rc(ci))),
                  pl.BlockSpec((rep, None, P, N), lambda ci, g: (g, rc(ci), 0, 0)), hd, vec, vec],
        out_specs=[hd, grp, grp, tok, vec],
        out_shape=[jax.ShapeDtypeStruct((H, S, P), F32), jax.ShapeDtypeStruct((SSD_GROUPS, S, N), F32),
                   jax.ShapeDtypeStruct((SSD_GROUPS, S, N), F32), jax.ShapeDtypeStruct((S, H), F32),
                   jax.ShapeDtypeStruct((1, H), F32)],
        scratch_shapes=[pltpu.VMEM((H, P, N), F32)],
        compiler_params=_cparams("arbitrary", "arbitrary"), name=name)(xs, Bm, Cm, dt, cs, csT, prev, dy, a_log, d_skip)


def _dt_bwd(ddt, dA, dtp, dt_bias, a_log, name):
    S, H = ddt.shape
    tm = _rows(S, 512)

    def body(g_ref, da_ref, p_ref, b_ref, al_ref, o_ref, db_ref, dal_ref):
        @pl.when(pl.program_id(0) == 0)
        def _():
            db_ref[...] = jnp.zeros_like(db_ref)
            dal_ref[...] = da_ref[...] * (-jnp.exp(al_ref[...]))

        g = g_ref[...] * _sigmoid(p_ref[:, :H] + b_ref[...])
        db_ref[...] += jnp.sum(g, axis=0, keepdims=True)
        o_ref[...] = jnp.zeros_like(o_ref)
        o_ref[:, :H] = g.astype(o_ref.dtype)

    vec = pl.BlockSpec((1, H), lambda i: (0, 0))
    return pl.pallas_call(
        body, grid=(S // tm,),
        in_specs=[pl.BlockSpec((tm, H), lambda i: (i, 0)), vec, pl.BlockSpec((tm, DT_PAD), lambda i: (i, 0)), vec, vec],
        out_specs=[pl.BlockSpec((tm, DT_PAD), lambda i: (i, 0)), vec, vec],
        out_shape=[jax.ShapeDtypeStruct((S, DT_PAD), _MXU), jax.ShapeDtypeStruct((1, H), F32), jax.ShapeDtypeStruct((1, H), F32)],
        compiler_params=_cparams("arbitrary"), name=name)(ddt, dA, dtp, dt_bias, a_log)


def _ssd_gate_fwd(y, act, z, dskip, w, name):
    S, D = y.shape
    tm = _rows(S, 256)
    Gw = D // SSD_GROUPS

    def body(y_ref, x_ref, z_ref, k_ref, w_ref, o_ref):
        zv = z_ref[...]
        y2 = (y_ref[...] + x_ref[...] * k_ref[...]) * (zv * _sigmoid(zv))
        for g in range(SSD_GROUPS):
            sl = slice(g * Gw, (g + 1) * Gw)
            v = y2[:, sl]
            r = lax.rsqrt(jnp.mean(v * v, axis=-1, keepdims=True) + EPS)
            o_ref[:, sl] = ((v * r) * w_ref[:, sl]).astype(o_ref.dtype)

    row = pl.BlockSpec((tm, D), lambda i: (i, 0))
    vec = pl.BlockSpec((1, D), lambda i: (0, 0))
    return pl.pallas_call(body, grid=(S // tm,), in_specs=[row, row, row, vec, vec], out_specs=row,
                          out_shape=jax.ShapeDtypeStruct((S, D), _MXU), compiler_params=_cparams("parallel"),
                          name=name)(y, act, z, dskip, w)


def _ssd_gate_bwd(dyn, y, act, z, dskip, w, name):
    S, D = y.shape
    tm = _rows(S, 256)
    Gw = D // SSD_GROUPS

    def body(g_ref, y_ref, x_ref, z_ref, k_ref, w_ref, dy_ref, dz_ref, dk_ref, dw_ref):
        @pl.when(pl.program_id(0) == 0)
        def _():
            dk_ref[...] = jnp.zeros_like(dk_ref)
            dw_ref[...] = jnp.zeros_like(dw_ref)

        zv = z_ref[...]
        xv = x_ref[...]
        s = _sigmoid(zv)
        sz = zv * s
        y1 = y_ref[...] + xv * k_ref[...]
        y2 = y1 * sz
        for g in range(SSD_GROUPS):
            sl = slice(g * Gw, (g + 1) * Gw)
            v = y2[:, sl]
            r = lax.rsqrt(jnp.mean(v * v, axis=-1, keepdims=True) + EPS)
            vn = v * r
            gy = g_ref[:, sl].astype(F32)
            dw_ref[:, sl] += jnp.sum(gy * vn, axis=0, keepdims=True)
            dvn = gy * w_ref[:, sl]
            dy2 = r * (dvn - vn * jnp.mean(dvn * vn, axis=-1, keepdims=True))
            dy1 = dy2 * sz[:, sl]
            dy_ref[:, sl] = dy1
            dz_ref[:, sl] = (dy2 * y1[:, sl] * (s[:, sl] * (1.0 + zv[:, sl] * (1.0 - s[:, sl])))).astype(dz_ref.dtype)
            dk_ref[:, sl] += jnp.sum(dy1 * xv[:, sl], axis=0, keepdims=True)

    row = pl.BlockSpec((tm, D), lambda i: (i, 0))
    vec = pl.BlockSpec((1, D), lambda i: (0, 0))
    return pl.pallas_call(
        body, grid=(S // tm,), in_specs=[row, row, row, row, vec, vec], out_specs=[row, row, vec, vec],
        out_shape=[jax.ShapeDtypeStruct((S, D), F32), jax.ShapeDtypeStruct((S, D), _MXU),
                   jax.ShapeDtypeStruct((1, D), F32), jax.ShapeDtypeStruct((1, D), F32)],
        compiler_params=_cparams("arbitrary"), name=name)(dyn, y, act, z, dskip, w)


def _split_dot(v, u):
    hi = v.astype(_MXU)
    lo = (v - hi.astype(F32)).astype(_MXU)
    dn = (((1,), (0,)), ((), ()))
    return (lax.dot_general(hi, u, dn, preferred_element_type=F32) + lax.dot_general(lo, u, dn, preferred_element_type=F32))


def _sb_tiles(S):
    return _pick(S, 256, 128)


SB_LANES = 128
SB_PACK = SB_LANES // SB_HEAD_DIM
SB_ROWS = 128
SB_SCALE = 1.0 / math.sqrt(SB_HEAD_DIM)


def _head_masks():
    lane = lax.broadcasted_iota(jnp.int32, (1, SB_LANES), 1)
    return [(lane // SB_HEAD_DIM) == hh for hh in range(SB_PACK)]


def _by_head(hm, vals):
    out = vals[-1]
    for hh in range(SB_PACK - 2, -1, -1):
        out = jnp.where(hm[hh], vals[hh], out)
    return out


SB_DEAD = -110.0


def _sb_alive(Rs):
    m = Rs[0]
    for R in Rs[1:]:
        m = jnp.maximum(m, R)
    return jnp.max(m) > SB_DEAD


def _sb_rows(a, r):
    return a[r * SB_ROWS:(r + 1) * SB_ROWS]


def _sb_assemble(hm, vals):
    nr = len(vals) // SB_PACK
    return jnp.concatenate([_by_head(hm, vals[r * SB_PACK:(r + 1) * SB_PACK]) for r in range(nr)], axis=0)


def _sb_scores(zs, U, Rs, masks):
    ls = [-jnp.maximum(z, 0.0) - jnp.log(1.0 + jnp.exp(-jnp.abs(z))) for z in zs]
    if masks is not None:
        ls = [jnp.where(m, l, 0.0) for m, l in zip(masks, ls)]
    Es = [lax.dot_general(l.astype(_MXU), U, (((1,), (0,)), ((), ())), preferred_element_type=F32) for l in ls]
    As = [jnp.exp(l + z + (E + R)) for l, z, E, R in zip(ls, zs, Es, Rs)]
    if masks is not None:
        As = [jnp.where(m, A, 0.0) for m, A in zip(masks, As)]
    return ls, [A.astype(_MXU) for A in As]


SB_GROUP = 2


def _sbg_chains(T):
    return [(b, r, hh) for b in range(SB_GROUP) for r in range(T // SB_ROWS) for hh in range(SB_PACK)]


def _lanes(a, b):
    return a[:, b * SB_LANES:(b + 1) * SB_LANES]


def _sbg_join(hm, vals):
    per = len(vals) // SB_GROUP
    return jnp.concatenate([_sb_assemble(hm, vals[b * per:(b + 1) * per]) for b in range(SB_GROUP)], axis=1)


def _sbg_head_sum(hm, a):
    return jnp.concatenate([_by_head(hm, [jnp.sum(jnp.where(m, _lanes(a, b), 0.0), axis=1, keepdims=True) for m in hm])
                            for b in range(SB_GROUP)], axis=1)


def _sbg_fwd(q_arr, k_arr, v_arr, cols, w, name):
    S = q_arr.shape[0]
    T = _sb_tiles(S)
    cq, ck, cv = cols
    GW = SB_GROUP * SB_LANES
    nb = SB_WIDTH // GW

    def body(q_ref, k_ref, v_ref, w_ref, o_ref, y_ref):
        i = pl.program_id(1)
        hm = _head_masks()
        qs = q_ref[...] * SB_SCALE
        chains = _sbg_chains(T)
        qcs = [_sb_rows(jnp.where(hm[hh], _lanes(qs, b), jnp.zeros((T, SB_LANES), qs.dtype)), r) for b, r, hh in chains]
        U = _tri(T, True, False).astype(_MXU)

        def scores_of(j):
            kj = k_ref[pl.ds(pl.multiple_of(j * T, T), T), :]
            return [_dot(qc, _lanes(kj, b), 1, 1) for qc, (b, _, _) in zip(qcs, chains)]

        def weighted(Abs, j):
            vj = v_ref[pl.ds(pl.multiple_of(j * T, T), T), :]
            return _sbg_join(hm, [_dot(Ab, _lanes(vj, b), 1, 0) for Ab, (b, _, _) in zip(Abs, chains)])

        def step(carry):
            jj, acc, Rs, Aprev = carry
            j = i - 1 - jj
            zs = scores_of(j)
            acc = acc + weighted(Aprev, j + 1)
            ls, Abs = _sb_scores(zs, U, Rs, None)
            return jj + 1, acc, tuple(R + jnp.sum(l, axis=1, keepdims=True) for R, l in zip(Rs, ls)), tuple(Abs)

        masks = [_sb_rows(_tri(T, True, False), r) for _, r, _ in chains]
        zero = jnp.zeros((SB_ROWS, 1), F32)
        ls, Abs = _sb_scores(scores_of(i), U, (zero,) * len(chains), masks)
        carry = (jnp.int32(0), jnp.zeros((T, GW), F32), tuple(jnp.sum(l, axis=1, keepdims=True) for l in ls), tuple(Abs))
        jj, acc, _, Alast = lax.while_loop(lambda c: (c[0] < i) & _sb_alive(c[2]), step, carry)
        acc = acc + weighted(Alast, i - jj)
        o_ref[...] = acc
        r = lax.rsqrt(_sbg_head_sum(hm, acc * acc) * (1.0 / SB_HEAD_DIM) + EPS)
        y_ref[...] = ((acc * r) * w_ref[...]).astype(y_ref.dtype)

    blk = pl.BlockSpec((T, GW), lambda h, i: (i, h))
    return pl.pallas_call(
        body, grid=(nb, S // T),
        in_specs=[pl.BlockSpec((T, GW), lambda h, i: (i, cq + h)), pl.BlockSpec((S, GW), lambda h, i: (0, ck + h), pipeline_mode=pl.Buffered(1)),
                  pl.BlockSpec((S, GW), lambda h, i: (0, cv + h), pipeline_mode=pl.Buffered(1)), pl.BlockSpec((1, GW), lambda h, i: (0, h))],
        out_specs=[blk, blk], out_shape=[jax.ShapeDtypeStruct((S, SB_WIDTH), F32), jax.ShapeDtypeStruct((S, SB_WIDTH), _MXU)],
        compiler_params=_cparams("parallel", "parallel"), name=name)(q_arr, k_arr, v_arr, w)


def _sbg_bwd(q_arr, k_arr, v_arr, cols, o, dy_arr, cdy, w, name):
    S = q_arr.shape[0]
    T = _sb_tiles(S)
    cq, ck, cv = cols
    GW = SB_GROUP * SB_LANES
    nb = SB_WIDTH // GW

    def body(q_ref, k_ref, v_ref, o_ref, dy_ref, w_ref, dq_ref, dk_ref, dv_ref, dw_ref):
        i = pl.program_id(1)

        @pl.when(i == 0)
        def _():
            dk_ref[...] = jnp.zeros_like(dk_ref)
            dv_ref[...] = jnp.zeros_like(dv_ref)
            dw_ref[...] = jnp.zeros_like(dw_ref)

        hm = _head_masks()
        chains = _sbg_chains(T)
        qs = q_ref[...] * SB_SCALE
        ov = o_ref[...]
        gy = dy_ref[...]
        r = lax.rsqrt(_sbg_head_sum(hm, ov * ov) * (1.0 / SB_HEAD_DIM) + EPS)
        on = ov * r
        dw_ref[...] += jnp.sum(gy * on, axis=0, keepdims=True)
        don = gy * w_ref[...]
        do = r * (don - on * (_sbg_head_sum(hm, don * on) * (1.0 / SB_HEAD_DIM)))
        dob = do.astype(_MXU)
        dprod = dob.astype(F32) * ov
        zt = jnp.zeros((T, SB_LANES), dob.dtype)
        qm = [[jnp.where(hm[hh], _lanes(qs, b), zt) for hh in range(SB_PACK)] for b in range(SB_GROUP)]
        dm = [[jnp.where(hm[hh], _lanes(dob, b), zt) for hh in range(SB_PACK)] for b in range(SB_GROUP)]
        qcs = [_sb_rows(qm[b][hh], r_) for b, r_, hh in chains]
        doc = [_sb_rows(dm[b][hh], r_) for b, r_, hh in chains]
        Dt = [_sb_rows(jnp.sum(jnp.where(hm[hh], _lanes(dprod, b), 0.0), axis=1, keepdims=True), r_) for b, r_, hh in chains]
        U = _tri(T, True, False).astype(_MXU)
        Ui = _tri(T, False, False).astype(_MXU)

        def products_of(j):
            off = pl.multiple_of(j * T, T)
            kj = k_ref[pl.ds(off, T), :]
            vj = v_ref[pl.ds(off, T), :]
            return ([_dot(qc, _lanes(kj, b), 1, 1) for qc, (b, _, _) in zip(qcs, chains)],
                    [_dot(d, _lanes(vj, b), 1, 1) for d, (b, _, _) in zip(doc, chains)])

        def core(zs, dAs, Rs, Qs, masks):
            ls, Abs = _sb_scores(zs, U, Rs, masks)
            Gs = [dA * Ab.astype(F32) for dA, Ab in zip(dAs, Abs)]
            sfx = [_split_dot(G, Ui) for G in Gs]
            dzs = []
            for c, (l, G, s, D, Q) in enumerate(zip(ls, Gs, sfx, Dt, Qs)):
                P = D - (s + Q)
                dz = jnp.exp(l) * (G + P) - P
                if masks is not None:
                    dz = jnp.where(masks[c], dz, 0.0)
                dzs.append(dz.astype(_MXU))
            newR = tuple(R + jnp.sum(l, axis=1, keepdims=True) for R, l in zip(Rs, ls))
            newQ = tuple(Q + jnp.sum(G, axis=1, keepdims=True) for Q, G in zip(Qs, Gs))
            return tuple(Abs), tuple(dzs), newR, newQ

        def over_rows(vals, other):
            nr = T // SB_ROWS
            tiles = []
            for b in range(SB_GROUP):
                acc = None
                for hh in range(SB_PACK):
                    rows = jnp.concatenate([vals[(b * nr + r_) * SB_PACK + hh] for r_ in range(nr)], axis=0)
                    part = _dot(rows, other[b][hh], 0, 0)
                    acc = part if acc is None else acc + part
                tiles.append(acc)
            return jnp.concatenate(tiles, axis=1)

        def emit(Abs, dzs, j):
            off = pl.multiple_of(j * T, T)
            kj = k_ref[pl.ds(off, T), :]
            dk_ref[pl.ds(off, T), :] += over_rows(dzs, qm)
            dv_ref[pl.ds(off, T), :] += over_rows(Abs, dm)
            return _sbg_join(hm, [_dot(dzb, _lanes(kj, b), 1, 0) for dzb, (b, _, _) in zip(dzs, chains)])

        def step(carry):
            jj, dq, Rs, Qs, Aprev, dzprev = carry
            j = i - 1 - jj
            zs, dAs = products_of(j)
            dq = dq + emit(Aprev, dzprev, j + 1)
            Abs, dzs, Rs, Qs = core(zs, dAs, Rs, Qs, None)
            return jj + 1, dq, Rs, Qs, Abs, dzs

        masks = [_sb_rows(_tri(T, True, False), r_) for _, r_, _ in chains]
        zero = (jnp.zeros((SB_ROWS, 1), F32),) * len(chains)
        zs, dAs = products_of(i)
        Abs, dzs, Rs, Qs = core(zs, dAs, zero, zero, masks)
        jj, dq, _, _, Alast, dzlast = lax.while_loop(lambda c: (c[0] < i) & _sb_alive(c[2]), step,
                                                     (jnp.int32(0), jnp.zeros((T, GW), F32), Rs, Qs, Abs, dzs))
        dq = dq + emit(Alast, dzlast, i - jj)
        dq_ref[...] = (dq * SB_SCALE).astype(dq_ref.dtype)

    blk = pl.BlockSpec((T, GW), lambda h, i: (i, h))
    full = pl.BlockSpec((S, GW), lambda h, i: (0, h), pipeline_mode=pl.Buffered(1))
    wsp = pl.BlockSpec((1, GW), lambda h, i: (0, h))
    return pl.pallas_call(
        body, grid=(nb, S // T),
        in_specs=[pl.BlockSpec((T, GW), lambda h, i: (i, cq + h)), pl.BlockSpec((S, GW), lambda h, i: (0, ck + h), pipeline_mode=pl.Buffered(1)),
                  pl.BlockSpec((S, GW), lambda h, i: (0, cv + h), pipeline_mode=pl.Buffered(1)), blk,
                  pl.BlockSpec((T, GW), lambda h, i: (i, cdy + h)), wsp],
        out_specs=[blk, full, full, wsp],
        out_shape=[jax.ShapeDtypeStruct((S, SB_WIDTH), _MXU), jax.ShapeDtypeStruct((S, SB_WIDTH), F32),
                   jax.ShapeDtypeStruct((S, SB_WIDTH), F32), jax.ShapeDtypeStruct((1, SB_WIDTH), F32)],
        compiler_params=_cparams("parallel", "arbitrary"), name=name)(q_arr, k_arr, v_arr, o, dy_arr, w)


def _adamw(w, g, m, v, name):
    R, C = w.shape
    tm = _rows(R, 256) if R % 8 == 0 else R
    c1 = 1.0 - ADAM_B1 ** ADAM_STEP
    c2 = 1.0 - ADAM_B2 ** ADAM_STEP

    def body(w_ref, g_ref, m_ref, v_ref, d_ref, nm_ref, nv_ref):
        gv = g_ref[...]
        mn = ADAM_B1 * m_ref[...] + (1.0 - ADAM_B1) * gv
        vn = ADAM_B2 * v_ref[...] + (1.0 - ADAM_B2) * (gv * gv)
        d_ref[...] = -ADAM_LR * ((mn / c1) / (jnp.sqrt(vn / c2) + ADAM_EPS) + ADAM_WD * w_ref[...])
        nm_ref[...] = mn
        nv_ref[...] = vn

    blk = pl.BlockSpec((tm, C), lambda i: (i, 0))
    return pl.pallas_call(body, grid=(R // tm,), in_specs=[blk] * 4, out_specs=[blk] * 3,
                          out_shape=[jax.ShapeDtypeStruct((R, C), F32)] * 3, compiler_params=_cparams("parallel"),
                          name=name)(w, g, m, v)


def _sum_lead(a, name, first=None, pick=None, wire=False):
    n, R, C = a.shape
    tm = _rows(R, 256)
    nin = 1 if first is None else 2

    def body(*refs):
        refs = refs[nin - 1:]
        a_ref = refs[nin - 1]
        s = a_ref[0].astype(F32) if first is None else refs[0][...] + a_ref[0]
        for p in range(1, n):
            s = s + a_ref[p]
        for o_ref in refs[nin:]:
            o_ref[...] = s.astype(o_ref.dtype)

    outs = [jax.ShapeDtypeStruct((R, C), F32)] + ([jax.ShapeDtypeStruct((R, C), _WIRE)] if wire else [])
    if first is None:
        row = pl.BlockSpec((tm, C), lambda i: (i, 0))
        res = pl.pallas_call(body, grid=(R // tm,), in_specs=[pl.BlockSpec((n, tm, C), lambda i: (0, i, 0))],
                             out_specs=[row] * len(outs), out_shape=outs, compiler_params=_cparams("parallel"), name=name)(a)
    else:
        row = pl.BlockSpec((tm, C), lambda i, p: (i, 0))
        grid_spec = pltpu.PrefetchScalarGridSpec(
            num_scalar_prefetch=1, grid=(R // tm,),
            in_specs=[pl.BlockSpec((None, tm, C), lambda i, p: (p[0], i, 0)), pl.BlockSpec((n, tm, C), lambda i, p: (0, i, 0))],
            out_specs=[row] * len(outs))
        res = pl.pallas_call(body, grid_spec=grid_spec, out_shape=outs, compiler_params=_cparams("parallel"), name=name)(pick, first, a)
    return res if wire else res[0]


_GROUP_BITS = {'c': ((0, 0, 1),), 'xy': ((0, 1, 0), (1, 0, 0), (1, 1, 0)),
               'xyc': tuple((k >> 2 & 1, k >> 1 & 1, k & 1) for k in range(1, 8))}


def _exchange(srcs, *, group, same_src, own, chunks, name):
    flips = _GROUP_BITS[group]
    n = len(flips) + 1
    na = len(srcs)
    blk_shapes = [tuple(s.shape) if same_src else tuple(s.shape[1:]) for s in srcs]
    assert all(bs[0] % chunks == 0 for bs in blk_shapes), blk_shapes

    def body(*refs):
        src_refs, dst_refs = refs[:na], refs[na:2 * na]
        send_sems, recv_sems, loc_sems = refs[2 * na:]
        x, y, c = lax.axis_index("x"), lax.axis_index("y"), lax.axis_index("c")

        def member(px, py, pc):
            return {'c': pc, 'xy': 2 * px + py, 'xyc': 4 * px + 2 * py + pc}[group]

        def piece(ref, a, q):
            rows = blk_shapes[a][0] // chunks
            return ref.at[pl.ds(q * rows, rows)]

        me = member(x, y, c)
        started, arrivals = [], []
        for a in range(na):
            mine = src_refs[a] if same_src else src_refs[a].at[me]
            if own:
                for q in range(chunks):
                    cp = pltpu.make_async_copy(piece(mine, a, q), piece(dst_refs[a].at[me], a, q), loc_sems.at[a * chunks + q])
                    cp.start()
                    started.append(cp.wait)
            for kk, (fx, fy, fc) in enumerate(flips):
                px, py, pc = (1 - x if fx else x), (1 - y if fy else y), (1 - c if fc else c)
                peer = member(px, py, pc)
                out_blk = src_refs[a] if same_src else src_refs[a].at[peer]
                there = dst_refs[a].at[me if own else kk]
                here = dst_refs[a].at[peer if own else kk]
                for q in range(chunks):
                    s = (a * (n - 1) + kk) * chunks + q
                    out = pltpu.make_async_remote_copy(
                        src_ref=piece(out_blk, a, q), dst_ref=piece(there, a, q), send_sem=send_sems.at[s],
                        recv_sem=recv_sems.at[s], device_id=(px, py, pc), device_id_type=pl.DeviceIdType.MESH)
                    out.start()
                    started.append(out.wait_send)
                    arrivals.append(pltpu.make_async_remote_copy(
                        src_ref=piece(mine, a, q), dst_ref=piece(here, a, q), send_sem=send_sems.at[s],
                        recv_sem=recv_sems.at[s], device_id=(px, py, pc), device_id_type=pl.DeviceIdType.MESH).wait_recv)
        for wait in arrivals + started:
            wait()

    nsem = na * (n - 1) * chunks
    hbm = pl.BlockSpec(memory_space=pl.ANY)
    return pl.pallas_call(
        body, in_specs=[hbm] * na, out_specs=[hbm] * na,
        out_shape=[jax.ShapeDtypeStruct(((n if own else n - 1),) + bs, s.dtype) for bs, s in zip(blk_shapes, srcs)],
        scratch_shapes=[pltpu.SemaphoreType.DMA((nsem,)), pltpu.SemaphoreType.DMA((nsem,)),
                        pltpu.SemaphoreType.DMA((na * chunks,))],
        compiler_params=pltpu.CompilerParams(has_side_effects=True), name=name)(*srcs)


def _gather_chips(srcs, name):
    flips = _GROUP_BITS['xy']
    nf = len(flips)
    na = len(srcs)
    halves = [s.shape[0] // 2 for s in srcs]

    def body(*refs):
        src_refs, dst_refs = refs[:na], refs[na:2 * na]
        send_sems, recv_sems, fsend_sems, frecv_sems = refs[2 * na:]
        x, y, c = lax.axis_index("x"), lax.axis_index("y"), lax.axis_index("c")

        def half(ref, a, which):
            return ref.at[pl.ds(pl.multiple_of(which * halves[a], 8), halves[a])]

        waits, landed = [], []
        for a in range(na):
            for kk, (fx, fy, _) in enumerate(flips):
                peer = ((1 - x if fx else x), (1 - y if fy else y), c)
                s = a * nf + kk
                out = pltpu.make_async_remote_copy(
                    src_ref=half(src_refs[a], a, c), dst_ref=half(dst_refs[a].at[2 * x + y], a, c), send_sem=send_sems.at[s],
                    recv_sem=recv_sems.at[s], device_id=peer, device_id_type=pl.DeviceIdType.MESH)
                out.start()
                waits.append(out.wait_send)
                landed.append((a, s, 2 * peer[0] + peer[1]))
        for a, s, chip in landed:
            pltpu.make_async_remote_copy(
                src_ref=half(src_refs[a], a, c), dst_ref=half(dst_refs[a].at[chip], a, c), send_sem=send_sems.at[s],
                recv_sem=recv_sems.at[s], device_id=(x, y, c), device_id_type=pl.DeviceIdType.MESH).wait_recv()
            fwd = pltpu.make_async_remote_copy(
                src_ref=half(dst_refs[a].at[chip], a, c), dst_ref=half(dst_refs[a].at[chip], a, c), send_sem=fsend_sems.at[s],
                recv_sem=frecv_sems.at[s], device_id=(x, y, 1 - c), device_id_type=pl.DeviceIdType.MESH)
            fwd.start()
            waits.append(fwd.wait_send)
        for a, s, chip in landed:
            pltpu.make_async_remote_copy(
                src_ref=half(dst_refs[a].at[chip], a, 1 - c), dst_ref=half(dst_refs[a].at[chip], a, 1 - c),
                send_sem=fsend_sems.at[s], recv_sem=frecv_sems.at[s], device_id=(x, y, 1 - c),
                device_id_type=pl.DeviceIdType.MESH).wait_recv()
        for wait in waits:
            wait()

    hbm = pl.BlockSpec(memory_space=pl.ANY)
    sems = pltpu.SemaphoreType.DMA((na * nf,))
    return pl.pallas_call(
        body, in_specs=[hbm] * na, out_specs=[hbm] * na,
        out_shape=[jax.ShapeDtypeStruct((nf + 1,) + tuple(s.shape), s.dtype) for s in srcs],
        scratch_shapes=[sems, sems, sems, sems],
        compiler_params=pltpu.CompilerParams(has_side_effects=True), name=name)(*srcs)


def _to_shards(name, full):
    R, C = full.shape
    if name in COL_SPLIT:
        return full.reshape(R, 4, C // 4).transpose(1, 0, 2)
    return full.reshape(4, R // 4, C)


def _from_shards(name, sh):
    n, R, C = sh.shape
    if name in COL_SPLIT:
        return sh.transpose(1, 0, 2).reshape(R, n * C)
    return sh.reshape(n * R, C)


def _pack_rows(parts, width, rows):
    n = parts[0].shape[0]
    flat = jnp.concatenate([p.reshape(n, -1) for p in parts], axis=1)
    return jnp.pad(flat, ((0, 0), (0, rows * width - flat.shape[1]))).reshape(n, rows, width)


def _unpack_rows(buf, shapes):
    n = buf.shape[0]
    flat = buf.reshape(n, -1)
    out, o = [], 0
    for s in shapes:
        sz = math.prod(s)
        out.append(flat[:, o:o + sz].reshape((n,) + tuple(s)))
        o += sz
    return out


def _split_rows(a, rows):
    out, o = [], 0
    for r in rows:
        out.append(a[:, o:o + r])
        o += r
    return out


def _ceil_to(v, m):
    return -(-v // m) * m


def kernel(x, mem, norm_mix_w, w_in, conv_ssd_w, conv_ssd_b, dt_bias, a_log, d_skip, ssd_norm_w, sb_norm_w, w_out, norm_mem_w, norm_memkv_w, w_mq, w_mk, w_mv, w_mo, norm_ffn_w, w_up, conv_ffn_w, conv_ffn_b, w_down, norm_final_w, loss_target, m_norm_mix_w, m_w_in, m_conv_ssd_w, m_conv_ssd_b, m_dt_bias, m_a_log, m_d_skip, m_ssd_norm_w, m_sb_norm_w, m_w_out, m_norm_mem_w, m_norm_memkv_w, m_w_mq, m_w_mk, m_w_mv, m_w_mo, m_norm_ffn_w, m_w_up, m_conv_ffn_w, m_conv_ffn_b, m_w_down, m_norm_final_w, v_norm_mix_w, v_w_in, v_conv_ssd_w, v_conv_ssd_b, v_dt_bias, v_a_log, v_d_skip, v_ssd_norm_w, v_sb_norm_w, v_w_out, v_norm_mem_w, v_norm_memkv_w, v_w_mq, v_w_mk, v_w_mv, v_w_mo, v_norm_ffn_w, v_w_up, v_conv_ffn_w, v_conv_ffn_b, v_w_down, v_norm_final_w):
    W = dict(norm_mix_w=norm_mix_w, w_in=w_in, conv_ssd_w=conv_ssd_w, conv_ssd_b=conv_ssd_b, dt_bias=dt_bias, a_log=a_log,
             d_skip=d_skip, ssd_norm_w=ssd_norm_w, sb_norm_w=sb_norm_w, w_out=w_out, norm_mem_w=norm_mem_w,
             norm_memkv_w=norm_memkv_w, w_mq=w_mq, w_mk=w_mk, w_mv=w_mv, w_mo=w_mo, norm_ffn_w=norm_ffn_w, w_up=w_up,
             conv_ffn_w=conv_ffn_w, conv_ffn_b=conv_ffn_b, w_down=w_down, norm_final_w=norm_final_w)
    Mo = dict(norm_mix_w=m_norm_mix_w, w_in=m_w_in, conv_ssd_w=m_conv_ssd_w, conv_ssd_b=m_conv_ssd_b, dt_bias=m_dt_bias,
              a_log=m_a_log, d_skip=m_d_skip, ssd_norm_w=m_ssd_norm_w, sb_norm_w=m_sb_norm_w, w_out=m_w_out,
              norm_mem_w=m_norm_mem_w, norm_memkv_w=m_norm_memkv_w, w_mq=m_w_mq, w_mk=m_w_mk, w_mv=m_w_mv, w_mo=m_w_mo,
              norm_ffn_w=m_norm_ffn_w, w_up=m_w_up, conv_ffn_w=m_conv_ffn_w, conv_ffn_b=m_conv_ffn_b, w_down=m_w_down,
              norm_final_w=m_norm_final_w)
    Vo = dict(norm_mix_w=v_norm_mix_w, w_in=v_w_in, conv_ssd_w=v_conv_ssd_w, conv_ssd_b=v_conv_ssd_b, dt_bias=v_dt_bias,
              a_log=v_a_log, d_skip=v_d_skip, ssd_norm_w=v_ssd_norm_w, sb_norm_w=v_sb_norm_w, w_out=v_w_out,
              norm_mem_w=v_norm_mem_w, norm_memkv_w=v_norm_memkv_w, w_mq=v_w_mq, w_mk=v_w_mk, w_mv=v_w_mv, w_mo=v_w_mo,
              norm_ffn_w=v_norm_ffn_w, w_up=v_w_up, conv_ffn_w=v_conv_ffn_w, conv_ffn_b=v_conv_ffn_b, w_down=v_w_down,
              norm_final_w=v_norm_final_w)
    shapes = {n: W[n].shape for n in WEIGHTS}
    sh2 = {n: (1, a.shape[-1]) if a.ndim < 3 else a.shape[-2:] for n, a in W.items()}
    w2 = {n: W[n].reshape(sh2[n]) for n in WEIGHTS}
    x2d = x[0]
    S, D = x2d.shape
    H, P, N = SSD_HEADS, SSD_HEAD_DIM, SSD_STATE

    cv_rows = _ceil_to(-(-sum(math.prod(sh2[n]) for n in CONVW) // 128), 32)
    cpack = _pack_rows([w2[n][None] for n in CONVW], 128, cv_rows)[0]
    stacked = jnp.concatenate([w2[n].astype(_MXU) for n in ROW_SPLIT], axis=0)
    cidx = lax.axis_index("c")
    oidx = 2 * lax.axis_index("x") + lax.axis_index("y")
    mine = [stacked, w2['w_in'].astype(_MXU), w2['w_up'].astype(_MXU), cpack]
    others = _gather_chips(mine, "gather_weights")

    g_rows, g_in, g_up, call = [lax.dynamic_update_index_in_dim(g, m, oidx, 0) for m, g in zip(mine, others)]
    full = {'w_in': _from_shards('w_in', g_in), 'w_up': _from_shards('w_up', g_up)}
    full.update({n: _from_shards(n, a) for n, a in zip(ROW_SPLIT, _split_rows(g_rows, [sh2[n][0] for n in ROW_SPLIT]))})
    full.update({n: _from_shards(n, a) for n, a in zip(CONVW, _unpack_rows(call, [sh2[n] for n in CONVW]))})

    o1 = SSD_INNER
    o2 = o1 + SSD_XBC
    o3 = o2 + SSD_HEADS
    Wi = full['w_in']
    W_z, W_xbc, W_qkv = Wi[:, :o1], Wi[:, o1:o2], Wi[:, o3:]
    W_dt = jnp.pad(Wi[:, o2:o3], ((0, 0), (0, DT_PAD - SSD_HEADS)))
    W_in_r = jnp.concatenate([W_z, W_xbc, W_qkv, W_dt], axis=1)
    dskip_rep = jnp.repeat(w2['d_skip'], P, axis=1)

    h1 = _rms_fwd(x2d, w2['norm_mix_w'], "norm_mix")
    z = _mm(h1, W_z, name="proj_z")
    xbc = _mm(h1, W_xbc, name="proj_xbc")
    dtp = _mm(h1, W_dt, name="proj_dt")
    qkv = _mm(h1, W_qkv, out_dtype=_MXU, name="proj_qkv")
    pre = _dwconv_fwd(xbc, full['conv_ssd_w'], w2['conv_ssd_b'], "ssd_conv")
    act = _silu_fwd(pre, "ssd_conv_silu")
    dt, cs = _ssd_prep(dtp, w2['dt_bias'], w2['a_log'], "ssd_prep")
    csT = cs.T
    def heads(a, nh):
        return a.reshape(S, nh, a.shape[1] // nh).transpose(1, 0, 2)

    def unheads(a):
        return a.transpose(1, 0, 2).reshape(S, a.shape[0] * a.shape[2])

    xs_h = heads(act[:, :o1], H)
    Bm = heads(act[:, o1:o1 + SSD_GROUPS * N], SSD_GROUPS)
    Cm = heads(act[:, o1 + SSD_GROUPS * N:], SSD_GROUPS)
    y_h, prev = _ssdg_fwd(xs_h, Bm, Cm, dt, cs, csT, "ssd_scan")
    y_scan = unheads(y_h)
    y_ssd = _ssd_gate_fwd(y_scan, act, z, dskip_rep, w2['ssd_norm_w'], "ssd_gate")
    nsb = SB_WIDTH // (SB_GROUP * SB_LANES)
    qkv_cols = (0, nsb, 2 * nsb)
    o_sb, y_sb = _sbg_fwd(qkv, qkv, qkv, qkv_cols, w2['sb_norm_w'], "sb_attn")
    ycat = jnp.concatenate([y_ssd, y_sb], axis=1)
    x_2 = _mm(ycat, full['w_out'], res=x2d, name="out_proj")
    h2 = _rms_fwd(x_2, w2['norm_mem_w'], "norm_mem")
    qm = _mm(h2, full['w_mq'], out_dtype=_MXU, name="mem_q")
    mn = _rms_fwd(mem[0], w2['norm_memkv_w'], "norm_memkv")
    km = _mm(mn, full['w_mk'], out_dtype=_MXU, name="mem_k")
    vm = _mm(mn, full['w_mv'], out_dtype=_MXU, name="mem_v")
    om = _xattn_fwd(qm, km, vm, "mem_attn")
    x_3 = _mm(om, full['w_mo'], res=x_2, name="mem_o")
    h3 = _rms_fwd(x_3, w2['norm_ffn_w'], "norm_ffn")
    up = _mm(h3, full['w_up'], name="ffn_up")
    a_ffn = _conv_glu_fwd(up, full['conv_ffn_w'], w2['conv_ffn_b'], "ffn_conv_glu")
    x_4 = _mm(a_ffn, full['w_down'], res=x_3, name="ffn_down")
    dx4, dx4b, g_final, loss_blk = _loss_bwd(x_4, loss_target[0], w2['norm_final_w'], "loss_head")

    G = {'norm_final_w': g_final}
    dact = _mm(dx4b, full['w_down'], tb=True, name="d_ffn_act")
    G['w_down'] = _mm(a_ffn, dx4b, ta=True, name="g_w_down")
    du, G['conv_ffn_w'], G['conv_ffn_b'] = _conv_glu_bwd(up, dact, full['conv_ffn_w'], w2['conv_ffn_b'], "d_ffn_conv_glu")
    dup = _dwconv_bwd_x(du, full['conv_ffn_w'], "d_ffn_conv")
    dh3 = _mm(dup, full['w_up'], tb=True, name="d_h3")
    G['w_up'] = _mm(h3, dup, ta=True, name="g_w_up")
    dx3, dx3b, G['norm_ffn_w'] = _rms_bwd(dh3, x_3, w2['norm_ffn_w'], dx4, "d_norm_ffn")
    dom = _mm(dx3b, full['w_mo'], tb=True, out_dtype=_MXU, name="d_mem_o")
    G['w_mo'] = _mm(om, dx3b, ta=True, name="g_w_mo")
    dqm, dkm, dvm = _xattn_bwd(qm, km, vm, dom, "d_mem_attn")
    G['w_mq'] = _mm(h2, dqm, ta=True, name="g_w_mq")
    dh2 = _mm(dqm, full['w_mq'], tb=True, name="d_h2")
    dx2, dx2b, G['norm_mem_w'] = _rms_bwd(dh2, x_2, w2['norm_mem_w'], dx3, "d_norm_mem")
    G['w_mk'] = _mm(mn, dkm, ta=True, name="g_w_mk")
    G['w_mv'] = _mm(mn, dvm, ta=True, name="g_w_mv")
    dmn = _mm(dvm, full['w_mv'], tb=True, res=_mm(dkm, full['w_mk'], tb=True, name="d_mn_k"), name="d_mn_v")
    _, _, G['norm_memkv_w'] = _rms_bwd(dmn, mem[0], w2['norm_memkv_w'], None, "d_norm_memkv")
    dycat = _mm(dx2b, full['w_out'], tb=True, name="d_ycat")
    G['w_out'] = _mm(ycat, dx2b, ta=True, name="g_w_out")
    dy1, dz, g_dskip_lane, G['ssd_norm_w'] = _ssd_gate_bwd(dycat, y_scan, act, z, dskip_rep, w2['ssd_norm_w'], "d_ssd_gate")
    dxs_h, dB, dC, ddt, dA = _ssdg_bwd(xs_h, Bm, Cm, dt, cs, csT, prev, heads(dy1, H), w2['a_log'], w2['d_skip'], "d_ssd_scan")
    G['d_skip'] = jnp.sum(g_dskip_lane.reshape(H, P), axis=1)[None, :]
    dact_xbc = jnp.concatenate([unheads(dxs_h), unheads(dB), unheads(dC)], axis=1)
    dpre = _silu_bwd(pre, dact_xbc, "d_ssd_conv_silu")
    G['conv_ssd_w'], G['conv_ssd_b'] = _conv_bwd_w(xbc, dpre, full['conv_ssd_w'].shape[0], "g_ssd_conv")
    dxbc = _dwconv_bwd_x(dpre, full['conv_ssd_w'], "d_ssd_conv")
    ddtp, G['dt_bias'], G['a_log'] = _dt_bwd(ddt, dA, dtp, w2['dt_bias'], w2['a_log'], "d_dt")
    dq, dk, dv, G['sb_norm_w'] = _sbg_bwd(qkv, qkv, qkv, qkv_cols, o_sb, dycat, o1 // (SB_GROUP * SB_LANES), w2['sb_norm_w'], "d_sb_attn")
    dproj = jnp.concatenate([dz, dxbc, dq, dk.astype(_MXU), dv.astype(_MXU), ddtp], axis=1)
    dh1 = _mm(dproj, W_in_r, tb=True, name="d_h1")
    g_in_r = _mm(h1, dproj, ta=True, name="g_w_in")
    nq = 3 * SB_WIDTH

    def in_cols(lo, hi):
        spans = []
        for a, b, shift in ((0, o2, 0), (o2, o3, nq), (o3, o3 + nq, o2 - o3)):
            s, e = max(lo, a), min(hi, b)
            if s < e:
                spans.append((s + shift, e + shift))
        return spans

    hr_in, cs_in = g_in_r.shape[0] // 2, (o3 + nq) // 4
    g_in_pair = jnp.stack([jnp.stack([jnp.concatenate([g_in_r[h * hr_in:(h + 1) * hr_in, s:e] for s, e in in_cols(j * cs_in, (j + 1) * cs_in)],
                                                      axis=1) for j in range(4)]) for h in range(2)])
    grad_x, _, G['norm_mix_w'] = _rms_bwd(dh1, x2d, w2['norm_mix_w'], dx2, "d_norm_mix")

    by_owner = [jnp.concatenate([_to_shards(n, G[n]) for n in ROW_SPLIT], axis=1), _to_shards('w_up', G['w_up'])]
    to_pair = [a.reshape(4, 2, a.shape[1] // 2, a.shape[2]).transpose(1, 0, 2, 3) for a in by_owner]
    to_pair.insert(1, g_in_pair)
    got = _exchange(to_pair, group='c', same_src=False, own=False, chunks=4, name="reduce_pair")
    pair, pair_wire = [], []
    for t, g in zip(to_pair, got):
        _, _, r, cw = t.shape
        full_sum, wire_sum = _sum_lead(g.reshape(1, 4 * r, cw), "reduce_pair_sum%d" % len(pair), first=t.reshape(2, 4 * r, cw),
                                       pick=cidx.reshape(1), wire=True)
        pair.append(full_sum.reshape(4, r, cw))
        pair_wire.append(wire_sum.reshape(4, r, cw))
    got = _exchange(pair_wire, group='xy', same_src=False, own=False, chunks=1, name="reduce_chips")
    chips = [_sum_lead(g, "reduce_chips_sum%d" % k, first=p, pick=oidx.reshape(1)) for k, (p, g) in enumerate(zip(pair, got))]
    got = _exchange(chips, group='c', same_src=True, own=False, chunks=4, name="share_pair")
    red = [jnp.where(cidx == 0, jnp.concatenate([m, g[0]], axis=0), jnp.concatenate([g[0], m], axis=0))[None]
           for m, g in zip(chips, got)]
    gsh = dict(zip(ROW_SPLIT, [a[0] for a in _split_rows(red[0], [sh2[n][0] for n in ROW_SPLIT])]))
    gsh['w_in'], gsh['w_up'] = red[1][0], red[2][0]

    small_parts = [G[n].reshape(1, -1) for n in SMALL + CONVW] + [loss_blk[:1, :1]]
    small_shapes = [sh2[n] for n in SMALL] + [G[n].shape for n in CONVW] + [(1, 1)]
    small_rows = _ceil_to(-(-sum(math.prod(s) for s in small_shapes) // 128), 8)
    spack = _pack_rows(small_parts, 128, small_rows)[0]
    (gathered,) = _exchange([spack], group='xyc', same_src=True, own=True, chunks=1, name="gather_small")
    parts = [a[0] for a in _unpack_rows(_sum_lead(gathered, "small_sum")[None], small_shapes)]
    gsh.update(zip(SMALL, parts))
    for n, a in zip(CONVW, parts[len(SMALL):-1]):
        gsh[n] = lax.dynamic_index_in_dim(_to_shards(n, a), oidx, 0, keepdims=False)
    loss = parts[-1].reshape(())

    delta, new_m, new_v = {}, {}, {}
    for n in BIG:
        delta[n], new_m[n], new_v[n] = _adamw(w2[n], gsh[n], Mo[n].reshape(sh2[n]), Vo[n].reshape(sh2[n]), "adamw_" + n)
    for grp, width, tag in ((CONVW, 128, "adamw_conv"), (SMALL, 128, "adamw_small")):
        rows = _ceil_to(-(-sum(math.prod(sh2[n]) for n in grp) // width), 8)
        packed = [_pack_rows([src[n].reshape(1, -1) for n in grp], width, rows)[0]
                  for src in (w2, gsh, {n: Mo[n] for n in grp}, {n: Vo[n] for n in grp})]
        outs = _adamw(*packed, tag)
        for dst, o in zip((delta, new_m, new_v), outs):
            dst.update(zip(grp, [a[0] for a in _unpack_rows(o[None], [sh2[n] for n in grp])]))

    def shaped(d):
        return [d[n].reshape(shapes[n]) for n in WEIGHTS]

    return (loss, grad_x[None], *shaped(gsh), *shaped(delta), *shaped(new_m), *shaped(new_v))
```

```python
import math

import jax
import jax.numpy as jnp
from jax import lax
from jax.experimental import pallas as pl
from jax.experimental.pallas import tpu as pltpu

F32 = jnp.float32
_MXU = jnp.bfloat16
_WIRE = jnp.bfloat16
EPS = 1e-6
_VMEM_LIMIT = 48 * 1024 * 1024
_HI = lax.Precision.HIGHEST

SSD_HEADS = 16
SSD_HEAD_DIM = 64
SSD_GROUPS = 2
SSD_STATE = 128
SSD_CHUNK = 128
SSD_INNER = SSD_HEADS * SSD_HEAD_DIM
SSD_XBC = SSD_INNER + 2 * SSD_GROUPS * SSD_STATE
SB_HEADS = 16
SB_HEAD_DIM = 64
SB_WIDTH = SB_HEADS * SB_HEAD_DIM
MEM_HEADS = 4
DT_PAD = 128

ADAM_LR = 0.001
ADAM_B1 = 0.9
ADAM_B2 = 0.999
ADAM_EPS = 1e-08
ADAM_WD = 0.01
ADAM_STEP = 10

WEIGHTS = ['norm_mix_w', 'w_in', 'conv_ssd_w', 'conv_ssd_b', 'dt_bias', 'a_log', 'd_skip', 'ssd_norm_w',
           'sb_norm_w', 'w_out', 'norm_mem_w', 'norm_memkv_w', 'w_mq', 'w_mk', 'w_mv', 'w_mo', 'norm_ffn_w',
           'w_up', 'conv_ffn_w', 'conv_ffn_b', 'w_down', 'norm_final_w']
BIG = ['w_in', 'w_out', 'w_mq', 'w_mk', 'w_mv', 'w_mo', 'w_up', 'w_down']
COL_SPLIT = ('w_in', 'w_up', 'conv_ssd_w', 'conv_ffn_w')
ROW_SPLIT = ['w_out', 'w_mq', 'w_mk', 'w_mv', 'w_mo', 'w_down']
CONVW = ['conv_ssd_w', 'conv_ffn_w']
SMALL = ['norm_mix_w', 'conv_ssd_b', 'dt_bias', 'a_log', 'd_skip', 'ssd_norm_w', 'sb_norm_w', 'norm_mem_w',
         'norm_memkv_w', 'norm_ffn_w', 'conv_ffn_b', 'norm_final_w']


def _cparams(*sem):
    return pltpu.CompilerParams(dimension_semantics=sem if sem else None, vmem_limit_bytes=_VMEM_LIMIT)


def _pick(n, cap, mult=128):
    best = None
    for d in range(mult, min(n, cap) + 1, mult):
        if n % d == 0:
            best = d
    return n if best is None else best


def _dot(a, b, ca, cb):
    return lax.dot_general(a.astype(_MXU), b.astype(_MXU), (((ca,), (cb,)), ((), ())), preferred_element_type=F32)


def _sigmoid(v):
    return 1.0 / (1.0 + jnp.exp(-v))


def _log1p(u):
    w = 1.0 + u
    return jnp.where(w == 1.0, u, jnp.log(w) * (u / (w - 1.0)))


def _mm(a, b, *, ta=False, tb=False, res=None, out_dtype=F32, name):
    if ta:
        K, M = a.shape
    else:
        M, K = a.shape
    if tb:
        N, K2 = b.shape
    else:
        K2, N = b.shape
    assert K == K2, (a.shape, b.shape)
    tm = _pick(M, 1408, 128 if ta else 16)
    tn = _pick(N, 1536)
    tk = _pick(K, 1536)
    nk = K // tk
    a_spec = pl.BlockSpec((tk, tm), lambda i, j, k: (k, i)) if ta else pl.BlockSpec((tm, tk), lambda i, j, k: (i, k))
    b_spec = pl.BlockSpec((tn, tk), lambda i, j, k: (j, k)) if tb else pl.BlockSpec((tk, tn), lambda i, j, k: (k, j))
    o_spec = pl.BlockSpec((tm, tn), lambda i, j, k: (i, j))
    ca, cb = (0 if ta else 1), (1 if tb else 0)

    def body(*refs):
        if res is None:
            a_ref, b_ref, o_ref, acc_ref = refs
            r_ref = None
        else:
            a_ref, b_ref, r_ref, o_ref, acc_ref = refs
        k = pl.program_id(2)

        @pl.when(k == 0)
        def _():
            acc_ref[...] = jnp.zeros_like(acc_ref)

        acc_ref[...] += _dot(a_ref[...], b_ref[...], ca, cb)

        @pl.when(k == nk - 1)
        def _():
            r = acc_ref[...]
            if r_ref is not None:
                r = r + r_ref[...].astype(F32)
            o_ref[...] = r.astype(o_ref.dtype)

    ins = [a, b] + ([] if res is None else [res])
    in_specs = [a_spec, b_spec] + ([] if res is None else [o_spec])
    return pl.pallas_call(
        body, grid=(M // tm, N // tn, nk), in_specs=in_specs, out_specs=o_spec,
        out_shape=jax.ShapeDtypeStruct((M, N), out_dtype), scratch_shapes=[pltpu.VMEM((tm, tn), F32)],
        compiler_params=_cparams("parallel", "parallel", "arbitrary"), name=name)(*ins)


def _rows(S, cap):
    return _pick(S, cap, 8)


def _rms_fwd(x, w, name):
    S, D = x.shape
    tm = _rows(S, 512)

    def body(x_ref, w_ref, o_ref):
        xv = x_ref[...]
        r = lax.rsqrt(jnp.mean(xv * xv, axis=-1, keepdims=True) + EPS)
        o_ref[...] = ((xv * r) * w_ref[...]).astype(o_ref.dtype)

    row = pl.BlockSpec((tm, D), lambda i: (i, 0))
    return pl.pallas_call(body, grid=(S // tm,), in_specs=[row, pl.BlockSpec((1, D), lambda i: (0, 0))], out_specs=row,
                          out_shape=jax.ShapeDtypeStruct((S, D), _MXU), compiler_params=_cparams("parallel"), name=name)(x, w)


def _rms_bwd(dh, x, w, dres, name):
    S, D = x.shape
    tm = _rows(S, 256)

    def body(*refs):
        if dres is None:
            dh_ref, x_ref, w_ref, dx_ref, dxb_ref, dw_ref = refs
            dres_ref = None
        else:
            dh_ref, x_ref, w_ref, dres_ref, dx_ref, dxb_ref, dw_ref = refs
        xv = x_ref[...]
        r = lax.rsqrt(jnp.mean(xv * xv, axis=-1, keepdims=True) + EPS)
        xn = xv * r
        dy = dh_ref[...].astype(F32)

        @pl.when(pl.program_id(0) == 0)
        def _():
            dw_ref[...] = jnp.zeros_like(dw_ref)

        dw_ref[...] += jnp.sum(dy * xn, axis=0, keepdims=True)
        dxn = dy * w_ref[...]
        dx = r * (dxn - xn * jnp.mean(dxn * xn, axis=-1, keepdims=True))
        if dres_ref is not None:
            dx = dx + dres_ref[...]
        dx_ref[...] = dx
        dxb_ref[...] = dx.astype(dxb_ref.dtype)

    row = pl.BlockSpec((tm, D), lambda i: (i, 0))
    vec = pl.BlockSpec((1, D), lambda i: (0, 0))
    ins = [dh, x, w] + ([] if dres is None else [dres])
    in_specs = [row, row, vec] + ([] if dres is None else [row])
    return pl.pallas_call(
        body, grid=(S // tm,), in_specs=in_specs, out_specs=[row, row, vec],
        out_shape=[jax.ShapeDtypeStruct((S, D), F32), jax.ShapeDtypeStruct((S, D), _MXU), jax.ShapeDtypeStruct((1, D), F32)],
        compiler_params=_cparams("arbitrary"), name=name)(*ins)


def _loss_bwd(x, tgt, w, name):
    S, D = x.shape
    tm = _rows(S, 256)

    def body(x_ref, t_ref, w_ref, dx_ref, dxb_ref, dw_ref, loss_ref):
        xv = x_ref[...]
        r = lax.rsqrt(jnp.mean(xv * xv, axis=-1, keepdims=True) + EPS)
        xn = xv * r
        e = xn * w_ref[...] - t_ref[...]

        @pl.when(pl.program_id(0) == 0)
        def _():
            dw_ref[...] = jnp.zeros_like(dw_ref)
            loss_ref[...] = jnp.zeros_like(loss_ref)

        tok = jnp.mean(e * e, axis=-1, keepdims=True)
        loss_ref[...] += jnp.broadcast_to(0.5 * jnp.sum(tok, axis=0, keepdims=True), loss_ref.shape)
        dy = e * (1.0 / D)
        dw_ref[...] += jnp.sum(dy * xn, axis=0, keepdims=True)
        dxn = dy * w_ref[...]
        dx = r * (dxn - xn * jnp.mean(dxn * xn, axis=-1, keepdims=True))
        dx_ref[...] = dx
        dxb_ref[...] = dx.astype(dxb_ref.dtype)

    row = pl.BlockSpec((tm, D), lambda i: (i, 0))
    vec = pl.BlockSpec((1, D), lambda i: (0, 0))
    return pl.pallas_call(
        body, grid=(S // tm,), in_specs=[row, row, vec],
        out_specs=[row, row, vec, pl.BlockSpec((8, 128), lambda i: (0, 0))],
        out_shape=[jax.ShapeDtypeStruct((S, D), F32), jax.ShapeDtypeStruct((S, D), _MXU),
                   jax.ShapeDtypeStruct((1, D), F32), jax.ShapeDtypeStruct((8, 128), F32)],
        compiler_params=_cparams("arbitrary"), name=name)(x, tgt, w)


def _conv_tiles(S, C):
    return _rows(S, 256), _pick(C, 1536)


def _dwconv_fwd(x, w, b, name):
    S, C = x.shape
    K = w.shape[0]
    tm, tc = _conv_tiles(S, C)

    def body(x_ref, p_ref, w_ref, b_ref, o_ref):
        cur = x_ref[...]
        prev = jnp.where(pl.program_id(0) > 0, p_ref[...], 0.0)
        xx = jnp.concatenate([prev, cur], axis=0)
        acc = cur * w_ref[K - 1:K, :] + b_ref[...]
        for d in range(1, K):
            acc = acc + pltpu.roll(xx, d, 0)[8:, :] * w_ref[K - 1 - d:K - d, :]
        o_ref[...] = acc

    return pl.pallas_call(
        body, grid=(S // tm, C // tc),
        in_specs=[pl.BlockSpec((tm, tc), lambda i, j: (i, j)),
                  pl.BlockSpec((8, tc), lambda i, j: (jnp.maximum(i * (tm // 8) - 1, 0), j)),
                  pl.BlockSpec((K, tc), lambda i, j: (0, j)), pl.BlockSpec((1, tc), lambda i, j: (0, j))],
        out_specs=pl.BlockSpec((tm, tc), lambda i, j: (i, j)), out_shape=jax.ShapeDtypeStruct((S, C), F32),
        compiler_params=_cparams("parallel", "parallel"), name=name)(x, x, w, b)


def _conv_bwd_w(x, dy, K, name):
    S, C = x.shape
    tm, tc = _conv_tiles(S, C)

    def body(x_ref, p_ref, dy_ref, dw_ref, db_ref):
        i = pl.program_id(1)

        @pl.when(i == 0)
        def _():
            dw_ref[...] = jnp.zeros_like(dw_ref)
            db_ref[...] = jnp.zeros_like(db_ref)

        cur = x_ref[...]
        prev = jnp.where(i > 0, p_ref[...], 0.0)
        xx = jnp.concatenate([prev, cur], axis=0)
        g = dy_ref[...].astype(F32)
        db_ref[...] += jnp.sum(g, axis=0, keepdims=True)
        dw_ref[K - 1:K, :] += jnp.sum(g * cur, axis=0, keepdims=True)
        for d in range(1, K):
            dw_ref[K - 1 - d:K - d, :] += jnp.sum(g * pltpu.roll(xx, d, 0)[8:, :], axis=0, keepdims=True)

    return pl.pallas_call(
        body, grid=(C // tc, S // tm),
        in_specs=[pl.BlockSpec((tm, tc), lambda j, i: (i, j)),
                  pl.BlockSpec((8, tc), lambda j, i: (jnp.maximum(i * (tm // 8) - 1, 0), j)),
                  pl.BlockSpec((tm, tc), lambda j, i: (i, j))],
        out_specs=[pl.BlockSpec((K, tc), lambda j, i: (0, j)), pl.BlockSpec((1, tc), lambda j, i: (0, j))],
        out_shape=[jax.ShapeDtypeStruct((K, C), F32), jax.ShapeDtypeStruct((1, C), F32)],
        compiler_params=_cparams("parallel", "arbitrary"), name=name)(x, x, dy)


def _dwconv_bwd_x(dy, w, name):
    S, C = dy.shape
    K = w.shape[0]
    tm, tc = _conv_tiles(S, C)
    last = S // tm - 1
    hr = 8 * (4 // dy.dtype.itemsize)

    def body(g_ref, n_ref, w_ref, o_ref):
        cur = g_ref[...].astype(F32)
        nxt = jnp.where(pl.program_id(0) < last, n_ref[...].astype(F32), 0.0)
        xx = jnp.concatenate([cur, nxt], axis=0)
        acc = cur * w_ref[K - 1:K, :]
        for d in range(1, K):
            acc = acc + pltpu.roll(xx, tm + hr - d, 0)[:tm, :] * w_ref[K - 1 - d:K - d, :]
        o_ref[...] = acc.astype(o_ref.dtype)

    return pl.pallas_call(
        body, grid=(S // tm, C // tc),
        in_specs=[pl.BlockSpec((tm, tc), lambda i, j: (i, j)),
                  pl.BlockSpec((hr, tc), lambda i, j: (jnp.minimum((i + 1) * (tm // hr), S // hr - 1), j)),
                  pl.BlockSpec((K, tc), lambda i, j: (0, j))],
        out_specs=pl.BlockSpec((tm, tc), lambda i, j: (i, j)), out_shape=jax.ShapeDtypeStruct((S, C), _MXU),
        compiler_params=_cparams("parallel", "parallel"), name=name)(dy, dy, w)


def _silu_fwd(pre, name):
    S, C = pre.shape
    tm, tc = _conv_tiles(S, C)

    def body(p_ref, o_ref):
        p = p_ref[...]
        o_ref[...] = p * _sigmoid(p)

    blk = pl.BlockSpec((tm, tc), lambda i, j: (i, j))
    return pl.pallas_call(body, grid=(S // tm, C // tc), in_specs=[blk], out_specs=blk,
                          out_shape=jax.ShapeDtypeStruct((S, C), F32), compiler_params=_cparams("parallel", "parallel"),
                          name=name)(pre)


def _silu_bwd(pre, dact, name):
    S, C = pre.shape
    tm, tc = _conv_tiles(S, C)

    def body(p_ref, g_ref, o_ref):
        p = p_ref[...]
        s = _sigmoid(p)
        o_ref[...] = g_ref[...] * (s * (1.0 + p * (1.0 - s)))

    blk = pl.BlockSpec((tm, tc), lambda i, j: (i, j))
    return pl.pallas_call(body, grid=(S // tm, C // tc), in_specs=[blk, blk], out_specs=blk,
                          out_shape=jax.ShapeDtypeStruct((S, C), F32), compiler_params=_cparams("parallel", "parallel"),
                          name=name)(pre, dact)


def _shifted_rows(cur, prev, K):
    xx = jnp.concatenate([prev, cur], axis=0)
    return [cur] + [pltpu.roll(xx, d, 0)[8:, :] for d in range(1, K)]


def _conv_glu_fwd(x, w, b, name):
    S, C = x.shape
    K = w.shape[0]
    Fh = C // 2
    tm = _rows(S, 128)

    def body(x_ref, p_ref, w_ref, b_ref, o_ref):
        sh = _shifted_rows(x_ref[...], jnp.where(pl.program_id(0) > 0, p_ref[...], 0.0), K)
        u = b_ref[...] + sum(sh[d] * w_ref[K - 1 - d:K - d, :] for d in range(K))
        g = u[:, :Fh]
        o_ref[...] = (g * _sigmoid(g) * u[:, Fh:]).astype(o_ref.dtype)

    return pl.pallas_call(
        body, grid=(S // tm,),
        in_specs=[pl.BlockSpec((tm, C), lambda i: (i, 0)), pl.BlockSpec((8, C), lambda i: (jnp.maximum(i * (tm // 8) - 1, 0), 0)),
                  pl.BlockSpec((K, C), lambda i: (0, 0)), pl.BlockSpec((1, C), lambda i: (0, 0))],
        out_specs=pl.BlockSpec((tm, Fh), lambda i: (i, 0)), out_shape=jax.ShapeDtypeStruct((S, Fh), _MXU),
        compiler_params=_cparams("parallel"), name=name)(x, x, w, b)


def _conv_glu_bwd(x, dact, w, b, name):
    S, C = x.shape
    K = w.shape[0]
    Fh = C // 2
    tm = _rows(S, 128)

    def body(x_ref, p_ref, g_ref, w_ref, b_ref, du_ref, dw_ref, db_ref):
        i = pl.program_id(0)

        @pl.when(i == 0)
        def _():
            dw_ref[...] = jnp.zeros_like(dw_ref)
            db_ref[...] = jnp.zeros_like(db_ref)

        sh = _shifted_rows(x_ref[...], jnp.where(i > 0, p_ref[...], 0.0), K)
        u = b_ref[...] + sum(sh[d] * w_ref[K - 1 - d:K - d, :] for d in range(K))
        g = u[:, :Fh]
        da = g_ref[...].astype(F32)
        s = _sigmoid(g)
        halves = ((slice(0, Fh), da * u[:, Fh:] * (s * (1.0 + g * (1.0 - s)))), (slice(Fh, C), da * (g * s)))
        for cols, du in halves:
            du_ref[:, cols] = du.astype(du_ref.dtype)
            db_ref[:, cols] += jnp.sum(du, axis=0, keepdims=True)
            for d in range(K):
                dw_ref[K - 1 - d:K - d, cols] += jnp.sum(du * sh[d][:, cols], axis=0, keepdims=True)

    return pl.pallas_call(
        body, grid=(S // tm,),
        in_specs=[pl.BlockSpec((tm, C), lambda i: (i, 0)), pl.BlockSpec((8, C), lambda i: (jnp.maximum(i * (tm // 8) - 1, 0), 0)),
                  pl.BlockSpec((tm, Fh), lambda i: (i, 0)), pl.BlockSpec((K, C), lambda i: (0, 0)), pl.BlockSpec((1, C), lambda i: (0, 0))],
        out_specs=[pl.BlockSpec((tm, C), lambda i: (i, 0)), pl.BlockSpec((K, C), lambda i: (0, 0)), pl.BlockSpec((1, C), lambda i: (0, 0))],
        out_shape=[jax.ShapeDtypeStruct((S, C), _MXU), jax.ShapeDtypeStruct((K, C), F32), jax.ShapeDtypeStruct((1, C), F32)],
        compiler_params=_cparams("arbitrary"), name=name)(x, x, dact, w, b)


def _xattn_fwd(q, k, v, name):
    S, D = q.shape
    M = k.shape[0]
    hd = D // MEM_HEADS
    tm = _rows(S, 512)
    scale = 1.0 / math.sqrt(hd)

    def body(q_ref, k_ref, v_ref, o_ref):
        for h in range(MEM_HEADS):
            sl = slice(h * hd, (h + 1) * hd)
            s = _dot(q_ref[:, sl], k_ref[:, sl], 1, 1) * scale
            p = jnp.exp(s - jnp.max(s, axis=-1, keepdims=True))
            p = p / jnp.sum(p, axis=-1, keepdims=True)
            o_ref[:, sl] = _dot(p, v_ref[:, sl], 1, 0).astype(o_ref.dtype)

    kv = pl.BlockSpec((M, D), lambda i: (0, 0))
    row = pl.BlockSpec((tm, D), lambda i: (i, 0))
    return pl.pallas_call(body, grid=(S // tm,), in_specs=[row, kv, kv], out_specs=row,
                          out_shape=jax.ShapeDtypeStruct((S, D), _MXU), compiler_params=_cparams("parallel"), name=name)(q, k, v)


def _xattn_bwd(q, k, v, do, name):
    S, D = q.shape
    M = k.shape[0]
    hd = D // MEM_HEADS
    tm = _rows(S, 512)
    scale = 1.0 / math.sqrt(hd)

    def body(q_ref, k_ref, v_ref, do_ref, dq_ref, dk_ref, dv_ref):
        @pl.when(pl.program_id(0) == 0)
        def _():
            dk_ref[...] = jnp.zeros_like(dk_ref)
            dv_ref[...] = jnp.zeros_like(dv_ref)

        for h in range(MEM_HEADS):
            sl = slice(h * hd, (h + 1) * hd)
            qh, kh, vh, doh = q_ref[:, sl], k_ref[:, sl], v_ref[:, sl], do_ref[:, sl]
            s = _dot(qh, kh, 1, 1) * scale
            p = jnp.exp(s - jnp.max(s, axis=-1, keepdims=True))
            p = p / jnp.sum(p, axis=-1, keepdims=True)
            dp = _dot(doh, vh, 1, 1)
            dv_ref[:, sl] += _dot(p, doh, 0, 0)
            ds = (p * (dp - jnp.sum(dp * p, axis=-1, keepdims=True))) * scale
            dq_ref[:, sl] = _dot(ds, kh, 1, 0).astype(dq_ref.dtype)
            dk_ref[:, sl] += _dot(ds, qh, 0, 0)

    kv = pl.BlockSpec((M, D), lambda i: (0, 0))
    row = pl.BlockSpec((tm, D), lambda i: (i, 0))
    return pl.pallas_call(
        body, grid=(S // tm,), in_specs=[row, kv, kv, row], out_specs=[row, kv, kv],
        out_shape=[jax.ShapeDtypeStruct((S, D), _MXU), jax.ShapeDtypeStruct((M, D), F32), jax.ShapeDtypeStruct((M, D), F32)],
        compiler_params=_cparams("arbitrary"), name=name)(q, k, v, do)


def _tri(n, strict, upper):
    r = lax.broadcasted_iota(jnp.int32, (n, n), 0)
    c = lax.broadcasted_iota(jnp.int32, (n, n), 1)
    if upper:
        return (c > r) if strict else (c >= r)
    return (r > c) if strict else (r >= c)


def _ssd_prep(dtp, dt_bias, a_log, name):
    S = dtp.shape[0]
    L, H = SSD_CHUNK, SSD_HEADS

    def body(p_ref, b_ref, al_ref, dt_ref, cs_ref):
        v = p_ref[:, :H] + b_ref[...]
        dt = jnp.maximum(v, 0.0) + _log1p(jnp.exp(-jnp.abs(v)))
        dt_ref[...] = dt
        a = dt * (-jnp.exp(al_ref[...]))
        cs_ref[...] = jnp.dot(_tri(L, False, False).astype(F32), a, precision=_HI, preferred_element_type=F32)

    blk = pl.BlockSpec((L, H), lambda c: (c, 0))
    vec = pl.BlockSpec((1, H), lambda c: (0, 0))
    return pl.pallas_call(body, grid=(S // L,), in_specs=[pl.BlockSpec((L, DT_PAD), lambda c: (c, 0)), vec, vec],
                          out_specs=[blk, blk], out_shape=[jax.ShapeDtypeStruct((S, H), F32)] * 2,
                          compiler_params=_cparams("parallel"), name=name)(dtp, dt_bias, a_log)


def _head_col(blk_ref, h):
    sel = lax.broadcasted_iota(jnp.int32, (1, SSD_HEADS), 1) == h
    return jnp.sum(jnp.where(sel, blk_ref[...], 0.0), axis=1, keepdims=True)


def _ssdg_fwd(xs, Bm, Cm, dt, cs, csT, name):
    H, S, P = xs.shape
    L, N = SSD_CHUNK, SSD_STATE
    nc = S // L
    rep = H // SSD_GROUPS
    hs = range(rep)

    def body(x_ref, b_ref, c_ref, dt_ref, cs_ref, csT_ref, y_ref, prev_ref, st_ref):
        c, g = pl.program_id(0), pl.program_id(1)

        @pl.when(c == 0)
        def _():
            for hh in hs:
                st_ref[g * rep + hh] = jnp.zeros((P, N), F32)

        Bv, Cv = b_ref[...], c_ref[...]
        tril = _tri(L, False, False)
        dtc = [_head_col(dt_ref, g * rep + hh) for hh in hs]
        csc = [_head_col(cs_ref, g * rep + hh) for hh in hs]
        csr = [csT_ref[hh:hh + 1, :] for hh in hs]
        last = [r[:, L - 1:L] for r in csr]
        xc = [x_ref[hh] * dtc[hh] for hh in hs]
        cb = _dot(Cv, Bv, 1, 1)
        m = [cb * jnp.where(tril, jnp.exp(jnp.where(tril, csc[hh] - csr[hh], 0.0)), 0.0) for hh in hs]
        prev = [st_ref[g * rep + hh] for hh in hs]
        yd = [_dot(m[hh], xc[hh], 1, 0) for hh in hs]
        yo = [_dot(Cv, prev[hh], 1, 1) for hh in hs]
        new = [_dot(xc[hh] * jnp.exp(last[hh] - csc[hh]), Bv, 0, 0) for hh in hs]
        for hh in hs:
            y_ref[hh] = yd[hh] + yo[hh] * jnp.exp(csc[hh])
            prev_ref[hh] = prev[hh]
            st_ref[g * rep + hh] = prev[hh] * jnp.exp(last[hh]) + new[hh]

    tok = pl.BlockSpec((L, H), lambda c, g: (c, 0))
    return pl.pallas_call(
        body, grid=(nc, SSD_GROUPS),
        in_specs=[pl.BlockSpec((rep, L, P), lambda c, g: (g, c, 0)), pl.BlockSpec((None, L, N), lambda c, g: (g, c, 0)),
                  pl.BlockSpec((None, L, N), lambda c, g: (g, c, 0)), tok, tok, pl.BlockSpec((rep, L), lambda c, g: (g, c))],
        out_specs=[pl.BlockSpec((rep, L, P), lambda c, g: (g, c, 0)),
                   pl.BlockSpec((rep, None, P, N), lambda c, g: (g, c, 0, 0))],
        out_shape=[jax.ShapeDtypeStruct((H, S, P), F32), jax.ShapeDtypeStruct((H, nc, P, N), F32)],
        scratch_shapes=[pltpu.VMEM((H, P, N), F32)],
        compiler_params=_cparams("arbitrary", "arbitrary"), name=name)(xs, Bm, Cm, dt, cs, csT)


def _ssdg_bwd(xs, Bm, Cm, dt, cs, csT, prev, dy, a_log, d_skip, name):
    H, S, P = xs.shape
    L, N = SSD_CHUNK, SSD_STATE
    nc = S // L
    rep = H // SSD_GROUPS
    hs = range(rep)

    def rowsum(a):
        return jnp.sum(a, axis=1, keepdims=True)

    def body(x_ref, b_ref, c_ref, dt_ref, cs_ref, csT_ref, prev_ref, dy_ref, al_ref, dk_ref,
             dx_ref, db_ref, dc_ref, ddt_ref, da_ref, g_ref):
        ci, g = pl.program_id(0), pl.program_id(1)

        @pl.when(ci == 0)
        def _():
            for hh in hs:
                g_ref[g * rep + hh] = jnp.zeros((P, N), F32)

        @pl.when((ci == 0) & (g == 0))
        def _():
            da_ref[...] = jnp.zeros_like(da_ref)

        @pl.when(g == 0)
        def _():
            ddt_ref[...] = jnp.zeros_like(ddt_ref)

        lane = lax.broadcasted_iota(jnp.int32, (1, H), 1)
        sel = [lane == g * rep + hh for hh in hs]
        A_h = [-jnp.exp(rowsum(jnp.where(s, al_ref[...], 0.0))) for s in sel]
        dsk = [rowsum(jnp.where(s, dk_ref[...], 0.0)) for s in sel]
        dtc = [_head_col(dt_ref, g * rep + hh) for hh in hs]
        csc = [_head_col(cs_ref, g * rep + hh) for hh in hs]
        csr = [csT_ref[hh:hh + 1, :] for hh in hs]
        last = [r[:, L - 1:L] for r in csr]
        Bv, Cv = b_ref[...], c_ref[...]
        xv = [x_ref[hh] for hh in hs]
        xc = [xv[hh] * dtc[hh] for hh in hs]
        dY = [dy_ref[hh] for hh in hs]
        prv = [prev_ref[hh] for hh in hs]
        G = [g_ref[g * rep + hh] for hh in hs]
        ecs = [jnp.exp(v) for v in csc]
        w = [jnp.exp(last[hh] - csc[hh]) for hh in hs]
        cd = [jnp.exp(v) for v in last]
        tril = _tri(L, False, False)
        triu = _tri(L, False, True)
        lam = [jnp.where(tril, jnp.exp(jnp.where(tril, csc[hh] - csr[hh], 0.0)), 0.0) for hh in hs]
        lamT = [jnp.where(triu, jnp.exp(jnp.where(triu, csr[hh] - csc[hh], 0.0)), 0.0) for hh in hs]
        cb = _dot(Cv, Bv, 1, 1)
        bc = _dot(Bv, Cv, 1, 1)
        dM = [_dot(dY[hh], xc[hh], 1, 1) for hh in hs]
        dMT = [_dot(xc[hh], dY[hh], 1, 1) for hh in hs]
        cp = [_dot(Cv, prv[hh], 1, 1) for hh in hs]
        BG = [_dot(Bv, G[hh], 1, 1) for hh in hs]
        dYe = [dY[hh] * ecs[hh] for hh in hs]
        dprev = [_dot(dYe[hh], Cv, 0, 0) for hh in hs]
        m = [cb * lam[hh] for hh in hs]
        mT = [bc * lamT[hh] for hh in hs]
        dxc = [_dot(mT[hh], dY[hh], 1, 0) + w[hh] * BG[hh] for hh in hs]
        dcb = sum([dM[hh] * lam[hh] for hh in hs][1:], dM[0] * lam[0])
        dcbT = sum([dMT[hh] * lamT[hh] for hh in hs][1:], dMT[0] * lamT[0])
        dC = _dot(dcb, Bv, 1, 0)
        dB = _dot(dcbT, Cv, 1, 0)
        for hh in hs:
            dC = dC + _dot(dYe[hh], prv[hh], 1, 0)
            dB = dB + _dot(xc[hh] * w[hh], G[hh], 1, 0)
        dc_ref[...] = dC
        db_ref[...] = dB
        ddt_acc = jnp.zeros((L, H), F32)
        da_acc = jnp.zeros((1, H), F32)
        rev = _tri(L, False, True).astype(F32)
        for hh in hs:
            dww = rowsum(xc[hh] * BG[hh]) * w[hh]
            dcs = (rowsum(dM[hh] * m[hh]) - rowsum(dMT[hh] * mT[hh]) + rowsum(dY[hh] * (cp[hh] * ecs[hh])) - dww)
            extra = jnp.sum(dww, axis=0, keepdims=True) + cd[hh] * jnp.sum(rowsum(G[hh] * prv[hh]), axis=0, keepdims=True)
            g_ref[g * rep + hh] = G[hh] * cd[hh] + dprev[hh]
            da = jnp.dot(rev, dcs, precision=_HI, preferred_element_type=F32) + extra
            dx_ref[hh] = dxc[hh] * dtc[hh] + dY[hh] * dsk[hh]
            ddt_acc = ddt_acc + jnp.where(sel[hh], da * A_h[hh] + rowsum(dxc[hh] * xv[hh]), 0.0)
            da_acc = da_acc + jnp.where(sel[hh], jnp.sum(da * dtc[hh], axis=0, keepdims=True), 0.0)
        ddt_ref[...] += ddt_acc
        da_ref[...] += da_acc

    rc = lambda ci: nc - 1 - ci
    hd = pl.BlockSpec((rep, L, P), lambda ci, g: (g, rc(ci), 0))
    grp = pl.BlockSpec((None, L, N), lambda ci, g: (g, rc(ci), 0))
    tok = pl.BlockSpec((L, H), lambda ci, g: (rc(ci), 0))
    vec = pl.BlockSpec((1, H), lambda ci, g: (0, 0))
    return pl.pallas_call(
        body, grid=(nc, SSD_GROUPS),
        in_specs=[hd, grp, grp, tok, tok, pl.BlockSpec((rep, L), lambda ci, g: (g, rc(ci))),
                  pl.BlockSpec((rep, None, P, N), lambda ci, g: (g, rc(ci), 0, 0)), hd, vec, vec],
        out_specs=[hd, grp, grp, tok, vec],
        out_shape=[jax.ShapeDtypeStruct((H, S, P), F32), jax.ShapeDtypeStruct((SSD_GROUPS, S, N), F32),
                   jax.ShapeDtypeStruct((SSD_GROUPS, S, N), F32), jax.ShapeDtypeStruct((S, H), F32),
                   jax.ShapeDtypeStruct((1, H), F32)],
        scratch_shapes=[pltpu.VMEM((H, P, N), F32)],
        compiler_params=_cparams("arbitrary", "arbitrary"), name=name)(xs, Bm, Cm, dt, cs, csT, prev, dy, a_log, d_skip)


def _dt_bwd(ddt, dA, dtp, dt_bias, a_log, name):
    S, H = ddt.shape
    tm = _rows(S, 512)

    def body(g_ref, da_ref, p_ref, b_ref, al_ref, o_ref, db_ref, dal_ref):
        @pl.when(pl.program_id(0) == 0)
        def _():
            db_ref[...] = jnp.zeros_like(db_ref)
            dal_ref[...] = da_ref[...] * (-jnp.exp(al_ref[...]))

        g = g_ref[...] * _sigmoid(p_ref[:, :H] + b_ref[...])
        db_ref[...] += jnp.sum(g, axis=0, keepdims=True)
        o_ref[...] = jnp.zeros_like(o_ref)
        o_ref[:, :H] = g.astype(o_ref.dtype)

    vec = pl.BlockSpec((1, H), lambda i: (0, 0))
    return pl.pallas_call(
        body, grid=(S // tm,),
        in_specs=[pl.BlockSpec((tm, H), lambda i: (i, 0)), vec, pl.BlockSpec((tm, DT_PAD), lambda i: (i, 0)), vec, vec],
        out_specs=[pl.BlockSpec((tm, DT_PAD), lambda i: (i, 0)), vec, vec],
        out_shape=[jax.ShapeDtypeStruct((S, DT_PAD), _MXU), jax.ShapeDtypeStruct((1, H), F32), jax.ShapeDtypeStruct((1, H), F32)],
        compiler_params=_cparams("arbitrary"), name=name)(ddt, dA, dtp, dt_bias, a_log)


def _ssd_gate_fwd(y, act, z, dskip, w, name):
    S, D = y.shape
    tm = _rows(S, 256)
    Gw = D // SSD_GROUPS

    def body(y_ref, x_ref, z_ref, k_ref, w_ref, o_ref):
        zv = z_ref[...]
        y2 = (y_ref[...] + x_ref[...] * k_ref[...]) * (zv * _sigmoid(zv))
        for g in range(SSD_GROUPS):
            sl = slice(g * Gw, (g + 1) * Gw)
            v = y2[:, sl]
            r = lax.rsqrt(jnp.mean(v * v, axis=-1, keepdims=True) + EPS)
            o_ref[:, sl] = ((v * r) * w_ref[:, sl]).astype(o_ref.dtype)

    row = pl.BlockSpec((tm, D), lambda i: (i, 0))
    vec = pl.BlockSpec((1, D), lambda i: (0, 0))
    return pl.pallas_call(body, grid=(S // tm,), in_specs=[row, row, row, vec, vec], out_specs=row,
                          out_shape=jax.ShapeDtypeStruct((S, D), _MXU), compiler_params=_cparams("parallel"),
                          name=name)(y, act, z, dskip, w)


def _ssd_gate_bwd(dyn, y, act, z, dskip, w, name):
    S, D = y.shape
    tm = _rows(S, 256)
    Gw = D // SSD_GROUPS

    def body(g_ref, y_ref, x_ref, z_ref, k_ref, w_ref, dy_ref, dz_ref, dk_ref, dw_ref):
        @pl.when(pl.program_id(0) == 0)
        def _():
            dk_ref[...] = jnp.zeros_like(dk_ref)
            dw_ref[...] = jnp.zeros_like(dw_ref)

        zv = z_ref[...]
        xv = x_ref[...]
        s = _sigmoid(zv)
        sz = zv * s
        y1 = y_ref[...] + xv * k_ref[...]
        y2 = y1 * sz
        for g in range(SSD_GROUPS):
            sl = slice(g * Gw, (g + 1) * Gw)
            v = y2[:, sl]
            r = lax.rsqrt(jnp.mean(v * v, axis=-1, keepdims=True) + EPS)
            vn = v * r
            gy = g_ref[:, sl].astype(F32)
            dw_ref[:, sl] += jnp.sum(gy * vn, axis=0, keepdims=True)
            dvn = gy * w_ref[:, sl]
            dy2 = r * (dvn - vn * jnp.mean(dvn * vn, axis=-1, keepdims=True))
            dy1 = dy2 * sz[:, sl]
            dy_ref[:, sl] = dy1
            dz_ref[:, sl] = (dy2 * y1[:, sl] * (s[:, sl] * (1.0 + zv[:, sl] * (1.0 - s[:, sl])))).astype(dz_ref.dtype)
            dk_ref[:, sl] += jnp.sum(dy1 * xv[:, sl], axis=0, keepdims=True)

    row = pl.BlockSpec((tm, D), lambda i: (i, 0))
    vec = pl.BlockSpec((1, D), lambda i: (0, 0))
    return pl.pallas_call(
        body, grid=(S // tm,), in_specs=[row, row, row, row, vec, vec], out_specs=[row, row, vec, vec],
        out_shape=[jax.ShapeDtypeStruct((S, D), F32), jax.ShapeDtypeStruct((S, D), _MXU),
                   jax.ShapeDtypeStruct((1, D), F32), jax.ShapeDtypeStruct((1, D), F32)],
        compiler_params=_cparams("arbitrary"), name=name)(dyn, y, act, z, dskip, w)


def _split_dot(v, u):
    hi = v.astype(_MXU)
    lo = (v - hi.astype(F32)).astype(_MXU)
    dn = (((1,), (0,)), ((), ()))
    return (lax.dot_general(hi, u, dn, preferred_element_type=F32) + lax.dot_general(lo, u, dn, preferred_element_type=F32))


def _sb_tiles(S):
    return _pick(S, 256, 128)


SB_LANES = 128
SB_PACK = SB_LANES // SB_HEAD_DIM
SB_ROWS = 128
SB_SCALE = 1.0 / math.sqrt(SB_HEAD_DIM)


def _head_masks():
    lane = lax.broadcasted_iota(jnp.int32, (1, SB_LANES), 1)
    return [(lane // SB_HEAD_DIM) == hh for hh in range(SB_PACK)]


def _by_head(hm, vals):
    out = vals[-1]
    for hh in range(SB_PACK - 2, -1, -1):
        out = jnp.where(hm[hh], vals[hh], out)
    return out


SB_DEAD = -110.0


def _sb_alive(Rs):
    m = Rs[0]
    for R in Rs[1:]:
        m = jnp.maximum(m, R)
    return jnp.max(m) > SB_DEAD


def _sb_rows(a, r):
    return a[r * SB_ROWS:(r + 1) * SB_ROWS]


def _sb_assemble(hm, vals):
    nr = len(vals) // SB_PACK
    return jnp.concatenate([_by_head(hm, vals[r * SB_PACK:(r + 1) * SB_PACK]) for r in range(nr)], axis=0)


def _sb_scores(zs, U, Rs, masks):
    ls = [-jnp.maximum(z, 0.0) - jnp.log(1.0 + jnp.exp(-jnp.abs(z))) for z in zs]
    if masks is not None:
        ls = [jnp.where(m, l, 0.0) for m, l in zip(masks, ls)]
    Es = [lax.dot_general(l.astype(_MXU), U, (((1,), (0,)), ((), ())), preferred_element_type=F32) for l in ls]
    As = [jnp.exp(l + z + (E + R)) for l, z, E, R in zip(ls, zs, Es, Rs)]
    if masks is not None:
        As = [jnp.where(m, A, 0.0) for m, A in zip(masks, As)]
    return ls, [A.astype(_MXU) for A in As]


SB_GROUP = 2


def _sbg_chains(T):
    return [(b, r, hh) for b in range(SB_GROUP) for r in range(T // SB_ROWS) for hh in range(SB_PACK)]


def _lanes(a, b):
    return a[:, b * SB_LANES:(b + 1) * SB_LANES]


def _sbg_join(hm, vals):
    per = len(vals) // SB_GROUP
    return jnp.concatenate([_sb_assemble(hm, vals[b * per:(b + 1) * per]) for b in range(SB_GROUP)], axis=1)


def _sbg_head_sum(hm, a):
    return jnp.concatenate([_by_head(hm, [jnp.sum(jnp.where(m, _lanes(a, b), 0.0), axis=1, keepdims=True) for m in hm])
                            for b in range(SB_GROUP)], axis=1)


def _sbg_fwd(q_arr, k_arr, v_arr, cols, w, name, gather=()):
    S = q_arr.shape[0]
    T = _sb_tiles(S)
    cq, ck, cv = cols
    GW = SB_GROUP * SB_LANES
    nb = SB_WIDTH // GW
    ng = len(gather)
    send, finish = _gather_phases([g.shape[0] // 2 for g in gather])

    def body(q_ref, k_ref, v_ref, w_ref, *rest):
        o_ref, y_ref = rest[ng:ng + 2]
        comm = (rest[:ng], rest[ng + 2:2 * ng + 2], rest[2 * ng + 2:])
        i = pl.program_id(1)
        if ng:
            @pl.when((pl.program_id(0) == 0) & (i == 0))
            def _():
                send(*comm)
        hm = _head_masks()
        qs = q_ref[...] * SB_SCALE
        chains = _sbg_chains(T)
        qcs = [_sb_rows(jnp.where(hm[hh], _lanes(qs, b), jnp.zeros((T, SB_LANES), qs.dtype)), r) for b, r, hh in chains]
        U = _tri(T, True, False).astype(_MXU)

        def scores_of(j):
            kj = k_ref[pl.ds(pl.multiple_of(j * T, T), T), :]
            return [_dot(qc, _lanes(kj, b), 1, 1) for qc, (b, _, _) in zip(qcs, chains)]

        def weighted(Abs, j):
            vj = v_ref[pl.ds(pl.multiple_of(j * T, T), T), :]
            return _sbg_join(hm, [_dot(Ab, _lanes(vj, b), 1, 0) for Ab, (b, _, _) in zip(Abs, chains)])

        def step(carry):
            jj, acc, Rs, Aprev = carry
            j = i - 1 - jj
            zs = scores_of(j)
            acc = acc + weighted(Aprev, j + 1)
            ls, Abs = _sb_scores(zs, U, Rs, None)
            return jj + 1, acc, tuple(R + jnp.sum(l, axis=1, keepdims=True) for R, l in zip(Rs, ls)), tuple(Abs)

        masks = [_sb_rows(_tri(T, True, False), r) for _, r, _ in chains]
        zero = jnp.zeros((SB_ROWS, 1), F32)
        ls, Abs = _sb_scores(scores_of(i), U, (zero,) * len(chains), masks)
        carry = (jnp.int32(0), jnp.zeros((T, GW), F32), tuple(jnp.sum(l, axis=1, keepdims=True) for l in ls), tuple(Abs))
        jj, acc, _, Alast = lax.while_loop(lambda c: (c[0] < i) & _sb_alive(c[2]), step, carry)
        acc = acc + weighted(Alast, i - jj)
        o_ref[...] = acc
        r = lax.rsqrt(_sbg_head_sum(hm, acc * acc) * (1.0 / SB_HEAD_DIM) + EPS)
        y_ref[...] = ((acc * r) * w_ref[...]).astype(y_ref.dtype)
        if ng:
            @pl.when((pl.program_id(0) == nb - 1) & (i == S // T - 1))
            def _():
                finish(*comm)

    blk = pl.BlockSpec((T, GW), lambda h, i: (i, h))
    hbm = pl.BlockSpec(memory_space=pl.ANY)
    return pl.pallas_call(
        body, grid=(nb, S // T),
        in_specs=[pl.BlockSpec((T, GW), lambda h, i: (i, cq + h)), pl.BlockSpec((S, GW), lambda h, i: (0, ck + h), pipeline_mode=pl.Buffered(1)),
                  pl.BlockSpec((S, GW), lambda h, i: (0, cv + h), pipeline_mode=pl.Buffered(1)), pl.BlockSpec((1, GW), lambda h, i: (0, h))]
                 + [hbm] * ng,
        out_specs=[blk, blk] + [hbm] * ng,
        out_shape=[jax.ShapeDtypeStruct((S, SB_WIDTH), F32), jax.ShapeDtypeStruct((S, SB_WIDTH), _MXU)]
                  + [jax.ShapeDtypeStruct((4,) + tuple(g.shape), g.dtype) for g in gather],
        scratch_shapes=_gather_sems(ng) if ng else [],
        compiler_params=_cparams("arbitrary", "arbitrary") if ng else _cparams("parallel", "parallel"), name=name)(q_arr, k_arr, v_arr, w, *gather)


def _sbg_bwd(q_arr, k_arr, v_arr, cols, o, dy_arr, cdy, w, name):
    S = q_arr.shape[0]
    T = _sb_tiles(S)
    cq, ck, cv = cols
    GW = SB_GROUP * SB_LANES
    nb = SB_WIDTH // GW

    def body(q_ref, k_ref, v_ref, o_ref, dy_ref, w_ref, dq_ref, dk_ref, dv_ref, dw_ref):
        i = pl.program_id(1)

        @pl.when(i == 0)
        def _():
            dk_ref[...] = jnp.zeros_like(dk_ref)
            dv_ref[...] = jnp.zeros_like(dv_ref)
            dw_ref[...] = jnp.zeros_like(dw_ref)

        hm = _head_masks()
        chains = _sbg_chains(T)
        qs = q_ref[...] * SB_SCALE
        ov = o_ref[...]
        gy = dy_ref[...]
        r = lax.rsqrt(_sbg_head_sum(hm, ov * ov) * (1.0 / SB_HEAD_DIM) + EPS)
        on = ov * r
        dw_ref[...] += jnp.sum(gy * on, axis=0, keepdims=True)
        don = gy * w_ref[...]
        do = r * (don - on * (_sbg_head_sum(hm, don * on) * (1.0 / SB_HEAD_DIM)))
        dob = do.astype(_MXU)
        dprod = dob.astype(F32) * ov
        zt = jnp.zeros((T, SB_LANES), dob.dtype)
        qm = [[jnp.where(hm[hh], _lanes(qs, b), zt) for hh in range(SB_PACK)] for b in range(SB_GROUP)]
        dm = [[jnp.where(hm[hh], _lanes(dob, b), zt) for hh in range(SB_PACK)] for b in range(SB_GROUP)]
        qcs = [_sb_rows(qm[b][hh], r_) for b, r_, hh in chains]
        doc = [_sb_rows(dm[b][hh], r_) for b, r_, hh in chains]
        Dt = [_sb_rows(jnp.sum(jnp.where(hm[hh], _lanes(dprod, b), 0.0), axis=1, keepdims=True), r_) for b, r_, hh in chains]
        U = _tri(T, True, False).astype(_MXU)
        Ui = _tri(T, False, False).astype(_MXU)

        def products_of(j):
            off = pl.multiple_of(j * T, T)
            kj = k_ref[pl.ds(off, T), :]
            vj = v_ref[pl.ds(off, T), :]
            return ([_dot(qc, _lanes(kj, b), 1, 1) for qc, (b, _, _) in zip(qcs, chains)],
                    [_dot(d, _lanes(vj, b), 1, 1) for d, (b, _, _) in zip(doc, chains)])

        def core(zs, dAs, Rs, Qs, masks):
            ls, Abs = _sb_scores(zs, U, Rs, masks)
            Gs = [dA * Ab.astype(F32) for dA, Ab in zip(dAs, Abs)]
            sfx = [_split_dot(G, Ui) for G in Gs]
            dzs = []
            for c, (l, G, s, D, Q) in enumerate(zip(ls, Gs, sfx, Dt, Qs)):
                P = D - (s + Q)
                dz = jnp.exp(l) * (G + P) - P
                if masks is not None:
                    dz = jnp.where(masks[c], dz, 0.0)
                dzs.append(dz.astype(_MXU))
            newR = tuple(R + jnp.sum(l, axis=1, keepdims=True) for R, l in zip(Rs, ls))
            newQ = tuple(Q + jnp.sum(G, axis=1, keepdims=True) for Q, G in zip(Qs, Gs))
            return tuple(Abs), tuple(dzs), newR, newQ

        def over_rows(vals, other):
            nr = T // SB_ROWS
            tiles = []
            for b in range(SB_GROUP):
                acc = None
                for hh in range(SB_PACK):
                    rows = jnp.concatenate([vals[(b * nr + r_) * SB_PACK + hh] for r_ in range(nr)], axis=0)
                    part = _dot(rows, other[b][hh], 0, 0)
                    acc = part if acc is None else acc + part
                tiles.append(acc)
            return jnp.concatenate(tiles, axis=1)

        def emit(Abs, dzs, j):
            off = pl.multiple_of(j * T, T)
            kj = k_ref[pl.ds(off, T), :]
            dk_ref[pl.ds(off, T), :] += over_rows(dzs, qm)
            dv_ref[pl.ds(off, T), :] += over_rows(Abs, dm)
            return _sbg_join(hm, [_dot(dzb, _lanes(kj, b), 1, 0) for dzb, (b, _, _) in zip(dzs, chains)])

        def step(carry):
            jj, dq, Rs, Qs, Aprev, dzprev = carry
            j = i - 1 - jj
            zs, dAs = products_of(j)
            dq = dq + emit(Aprev, dzprev, j + 1)
            Abs, dzs, Rs, Qs = core(zs, dAs, Rs, Qs, None)
            return jj + 1, dq, Rs, Qs, Abs, dzs

        masks = [_sb_rows(_tri(T, True, False), r_) for _, r_, _ in chains]
        zero = (jnp.zeros((SB_ROWS, 1), F32),) * len(chains)
        zs, dAs = products_of(i)
        Abs, dzs, Rs, Qs = core(zs, dAs, zero, zero, masks)
        jj, dq, _, _, Alast, dzlast = lax.while_loop(lambda c: (c[0] < i) & _sb_alive(c[2]), step,
                                                     (jnp.int32(0), jnp.zeros((T, GW), F32), Rs, Qs, Abs, dzs))
        dq = dq + emit(Alast, dzlast, i - jj)
        dq_ref[...] = (dq * SB_SCALE).astype(dq_ref.dtype)

    blk = pl.BlockSpec((T, GW), lambda h, i: (i, h))
    full = pl.BlockSpec((S, GW), lambda h, i: (0, h), pipeline_mode=pl.Buffered(1))
    wsp = pl.BlockSpec((1, GW), lambda h, i: (0, h))
    return pl.pallas_call(
        body, grid=(nb, S // T),
        in_specs=[pl.BlockSpec((T, GW), lambda h, i: (i, cq + h)), pl.BlockSpec((S, GW), lambda h, i: (0, ck + h), pipeline_mode=pl.Buffered(1)),
                  pl.BlockSpec((S, GW), lambda h, i: (0, cv + h), pipeline_mode=pl.Buffered(1)), blk,
                  pl.BlockSpec((T, GW), lambda h, i: (i, cdy + h)), wsp],
        out_specs=[blk, full, full, wsp],
        out_shape=[jax.ShapeDtypeStruct((S, SB_WIDTH), _MXU), jax.ShapeDtypeStruct((S, SB_WIDTH), F32),
                   jax.ShapeDtypeStruct((S, SB_WIDTH), F32), jax.ShapeDtypeStruct((1, SB_WIDTH), F32)],
        compiler_params=_cparams("parallel", "arbitrary"), name=name)(q_arr, k_arr, v_arr, o, dy_arr, w)


def _adamw(w, g, m, v, name):
    R, C = w.shape
    tm = _rows(R, 256) if R % 8 == 0 else R
    c1 = 1.0 - ADAM_B1 ** ADAM_STEP
    c2 = 1.0 - ADAM_B2 ** ADAM_STEP

    def body(w_ref, g_ref, m_ref, v_ref, d_ref, nm_ref, nv_ref):
        gv = g_ref[...]
        mn = ADAM_B1 * m_ref[...] + (1.0 - ADAM_B1) * gv
        vn = ADAM_B2 * v_ref[...] + (1.0 - ADAM_B2) * (gv * gv)
        d_ref[...] = -ADAM_LR * ((mn / c1) / (jnp.sqrt(vn / c2) + ADAM_EPS) + ADAM_WD * w_ref[...])
        nm_ref[...] = mn
        nv_ref[...] = vn

    blk = pl.BlockSpec((tm, C), lambda i: (i, 0))
    return pl.pallas_call(body, grid=(R // tm,), in_specs=[blk] * 4, out_specs=[blk] * 3,
                          out_shape=[jax.ShapeDtypeStruct((R, C), F32)] * 3, compiler_params=_cparams("parallel"),
                          name=name)(w, g, m, v)


def _sum_lead(a, name, first=None, pick=None, wire=False):
    n, R, C = a.shape
    tm = _rows(R, 256)
    nin = 1 if first is None else 2

    def body(*refs):
        refs = refs[nin - 1:]
        a_ref = refs[nin - 1]
        s = a_ref[0].astype(F32) if first is None else refs[0][...] + a_ref[0]
        for p in range(1, n):
            s = s + a_ref[p]
        for o_ref in refs[nin:]:
            o_ref[...] = s.astype(o_ref.dtype)

    outs = [jax.ShapeDtypeStruct((R, C), F32)] + ([jax.ShapeDtypeStruct((R, C), _WIRE)] if wire else [])
    if first is None:
        row = pl.BlockSpec((tm, C), lambda i: (i, 0))
        res = pl.pallas_call(body, grid=(R // tm,), in_specs=[pl.BlockSpec((n, tm, C), lambda i: (0, i, 0))],
                             out_specs=[row] * len(outs), out_shape=outs, compiler_params=_cparams("parallel"), name=name)(a)
    else:
        row = pl.BlockSpec((tm, C), lambda i, p: (i, 0))
        grid_spec = pltpu.PrefetchScalarGridSpec(
            num_scalar_prefetch=1, grid=(R // tm,),
            in_specs=[pl.BlockSpec((None, tm, C), lambda i, p: (p[0], i, 0)), pl.BlockSpec((n, tm, C), lambda i, p: (0, i, 0))],
            out_specs=[row] * len(outs))
        res = pl.pallas_call(body, grid_spec=grid_spec, out_shape=outs, compiler_params=_cparams("parallel"), name=name)(pick, first, a)
    return res if wire else res[0]


_GROUP_BITS = {'c': ((0, 0, 1),), 'xy': ((0, 1, 0), (1, 0, 0), (1, 1, 0)),
               'xyc': tuple((k >> 2 & 1, k >> 1 & 1, k & 1) for k in range(1, 8))}


def _exchange(srcs, *, group, same_src, own, chunks, name):
    flips = _GROUP_BITS[group]
    n = len(flips) + 1
    na = len(srcs)
    blk_shapes = [tuple(s.shape) if same_src else tuple(s.shape[1:]) for s in srcs]
    assert all(bs[0] % chunks == 0 for bs in blk_shapes), blk_shapes

    def body(*refs):
        src_refs, dst_refs = refs[:na], refs[na:2 * na]
        send_sems, recv_sems, loc_sems = refs[2 * na:]
        x, y, c = lax.axis_index("x"), lax.axis_index("y"), lax.axis_index("c")

        def member(px, py, pc):
            return {'c': pc, 'xy': 2 * px + py, 'xyc': 4 * px + 2 * py + pc}[group]

        def piece(ref, a, q):
            rows = blk_shapes[a][0] // chunks
            return ref.at[pl.ds(q * rows, rows)]

        me = member(x, y, c)
        started, arrivals = [], []
        for a in range(na):
            mine = src_refs[a] if same_src else src_refs[a].at[me]
            if own:
                for q in range(chunks):
                    cp = pltpu.make_async_copy(piece(mine, a, q), piece(dst_refs[a].at[me], a, q), loc_sems.at[a * chunks + q])
                    cp.start()
                    started.append(cp.wait)
            for kk, (fx, fy, fc) in enumerate(flips):
                px, py, pc = (1 - x if fx else x), (1 - y if fy else y), (1 - c if fc else c)
                peer = member(px, py, pc)
                out_blk = src_refs[a] if same_src else src_refs[a].at[peer]
                there = dst_refs[a].at[me if own else kk]
                here = dst_refs[a].at[peer if own else kk]
                for q in range(chunks):
                    s = (a * (n - 1) + kk) * chunks + q
                    out = pltpu.make_async_remote_copy(
                        src_ref=piece(out_blk, a, q), dst_ref=piece(there, a, q), send_sem=send_sems.at[s],
                        recv_sem=recv_sems.at[s], device_id=(px, py, pc), device_id_type=pl.DeviceIdType.MESH)
                    out.start()
                    started.append(out.wait_send)
                    arrivals.append(pltpu.make_async_remote_copy(
                        src_ref=piece(mine, a, q), dst_ref=piece(here, a, q), send_sem=send_sems.at[s],
                        recv_sem=recv_sems.at[s], device_id=(px, py, pc), device_id_type=pl.DeviceIdType.MESH).wait_recv)
        for wait in arrivals + started:
            wait()

    nsem = na * (n - 1) * chunks
    hbm = pl.BlockSpec(memory_space=pl.ANY)
    return pl.pallas_call(
        body, in_specs=[hbm] * na, out_specs=[hbm] * na,
        out_shape=[jax.ShapeDtypeStruct(((n if own else n - 1),) + bs, s.dtype) for bs, s in zip(blk_shapes, srcs)],
        scratch_shapes=[pltpu.SemaphoreType.DMA((nsem,)), pltpu.SemaphoreType.DMA((nsem,)),
                        pltpu.SemaphoreType.DMA((na * chunks,))],
        compiler_params=pltpu.CompilerParams(has_side_effects=True), name=name)(*srcs)


def _gather_phases(halves):
    flips = _GROUP_BITS['xy']
    nf = len(flips)
    na = len(halves)

    def copies(src_refs, dst_refs, sems):
        send_sems, recv_sems, fsend_sems, frecv_sems = sems
        x, y, c = lax.axis_index("x"), lax.axis_index("y"), lax.axis_index("c")

        def half(ref, a, which):
            return ref.at[pl.ds(pl.multiple_of(which * halves[a], 8), halves[a])]

        def copy(src, dst, pair, s, to):
            return pltpu.make_async_remote_copy(src_ref=src, dst_ref=dst, send_sem=pair[0].at[s], recv_sem=pair[1].at[s],
                                                device_id=to, device_id_type=pl.DeviceIdType.MESH)

        out = []
        for a in range(na):
            for kk, (fx, fy, _) in enumerate(flips):
                peer = ((1 - x if fx else x), (1 - y if fy else y), c)
                there = dst_refs[a].at[2 * peer[0] + peer[1]]
                s = a * nf + kk
                ici, d2d = (send_sems, recv_sems), (fsend_sems, frecv_sems)
                out.append((copy(half(src_refs[a], a, c), half(dst_refs[a].at[2 * x + y], a, c), ici, s, peer),
                            copy(half(src_refs[a], a, c), half(there, a, c), ici, s, (x, y, c)),
                            copy(half(there, a, c), half(there, a, c), d2d, s, (x, y, 1 - c)),
                            copy(half(there, a, 1 - c), half(there, a, 1 - c), d2d, s, (x, y, 1 - c))))
        return out

    def send(src_refs, dst_refs, sems):
        for first, _, _, _ in copies(src_refs, dst_refs, sems):
            first.start()

    def finish(src_refs, dst_refs, sems):
        cs = copies(src_refs, dst_refs, sems)
        for _, landed, onward, _ in cs:
            landed.wait_recv()
            onward.start()
        for _, _, _, passed in cs:
            passed.wait_recv()
        for first, _, onward, _ in cs:
            first.wait_send()
            onward.wait_send()

    return send, finish


def _gather_sems(na):
    return [pltpu.SemaphoreType.DMA((na * len(_GROUP_BITS['xy']),))] * 4


def _gather_chips(srcs, name):
    na = len(srcs)
    send, finish = _gather_phases([s.shape[0] // 2 for s in srcs])

    def body(*refs):
        send(refs[:na], refs[na:2 * na], refs[2 * na:])
        finish(refs[:na], refs[na:2 * na], refs[2 * na:])

    hbm = pl.BlockSpec(memory_space=pl.ANY)
    return pl.pallas_call(
        body, in_specs=[hbm] * na, out_specs=[hbm] * na,
        out_shape=[jax.ShapeDtypeStruct((4,) + tuple(s.shape), s.dtype) for s in srcs], scratch_shapes=_gather_sems(na),
        compiler_params=pltpu.CompilerParams(has_side_effects=True), name=name)(*srcs)


def _to_shards(name, full):
    R, C = full.shape
    if name in COL_SPLIT:
        return full.reshape(R, 4, C // 4).transpose(1, 0, 2)
    return full.reshape(4, R // 4, C)


def _from_shards(name, sh):
    n, R, C = sh.shape
    if name in COL_SPLIT:
        return sh.transpose(1, 0, 2).reshape(R, n * C)
    return sh.reshape(n * R, C)


def _pack_rows(parts, width, rows):
    n = parts[0].shape[0]
    flat = jnp.concatenate([p.reshape(n, -1) for p in parts], axis=1)
    return jnp.pad(flat, ((0, 0), (0, rows * width - flat.shape[1]))).reshape(n, rows, width)


def _unpack_rows(buf, shapes):
    n = buf.shape[0]
    flat = buf.reshape(n, -1)
    out, o = [], 0
    for s in shapes:
        sz = math.prod(s)
        out.append(flat[:, o:o + sz].reshape((n,) + tuple(s)))
        o += sz
    return out


def _split_rows(a, rows):
    out, o = [], 0
    for r in rows:
        out.append(a[:, o:o + r])
        o += r
    return out


def _ceil_to(v, m):
    return -(-v // m) * m


def kernel(x, mem, norm_mix_w, w_in, conv_ssd_w, conv_ssd_b, dt_bias, a_log, d_skip, ssd_norm_w, sb_norm_w, w_out, norm_mem_w, norm_memkv_w, w_mq, w_mk, w_mv, w_mo, norm_ffn_w, w_up, conv_ffn_w, conv_ffn_b, w_down, norm_final_w, loss_target, m_norm_mix_w, m_w_in, m_conv_ssd_w, m_conv_ssd_b, m_dt_bias, m_a_log, m_d_skip, m_ssd_norm_w, m_sb_norm_w, m_w_out, m_norm_mem_w, m_norm_memkv_w, m_w_mq, m_w_mk, m_w_mv, m_w_mo, m_norm_ffn_w, m_w_up, m_conv_ffn_w, m_conv_ffn_b, m_w_down, m_norm_final_w, v_norm_mix_w, v_w_in, v_conv_ssd_w, v_conv_ssd_b, v_dt_bias, v_a_log, v_d_skip, v_ssd_norm_w, v_sb_norm_w, v_w_out, v_norm_mem_w, v_norm_memkv_w, v_w_mq, v_w_mk, v_w_mv, v_w_mo, v_norm_ffn_w, v_w_up, v_conv_ffn_w, v_conv_ffn_b, v_w_down, v_norm_final_w):
    W = dict(norm_mix_w=norm_mix_w, w_in=w_in, conv_ssd_w=conv_ssd_w, conv_ssd_b=conv_ssd_b, dt_bias=dt_bias, a_log=a_log,
             d_skip=d_skip, ssd_norm_w=ssd_norm_w, sb_norm_w=sb_norm_w, w_out=w_out, norm_mem_w=norm_mem_w,
             norm_memkv_w=norm_memkv_w, w_mq=w_mq, w_mk=w_mk, w_mv=w_mv, w_mo=w_mo, norm_ffn_w=norm_ffn_w, w_up=w_up,
             conv_ffn_w=conv_ffn_w, conv_ffn_b=conv_ffn_b, w_down=w_down, norm_final_w=norm_final_w)
    Mo = dict(norm_mix_w=m_norm_mix_w, w_in=m_w_in, conv_ssd_w=m_conv_ssd_w, conv_ssd_b=m_conv_ssd_b, dt_bias=m_dt_bias,
              a_log=m_a_log, d_skip=m_d_skip, ssd_norm_w=m_ssd_norm_w, sb_norm_w=m_sb_norm_w, w_out=m_w_out,
              norm_mem_w=m_norm_mem_w, norm_memkv_w=m_norm_memkv_w, w_mq=m_w_mq, w_mk=m_w_mk, w_mv=m_w_mv, w_mo=m_w_mo,
              norm_ffn_w=m_norm_ffn_w, w_up=m_w_up, conv_ffn_w=m_conv_ffn_w, conv_ffn_b=m_conv_ffn_b, w_down=m_w_down,
              norm_final_w=m_norm_final_w)
    Vo = dict(norm_mix_w=v_norm_mix_w, w_in=v_w_in, conv_ssd_w=v_conv_ssd_w, conv_ssd_b=v_conv_ssd_b, dt_bias=v_dt_bias,
              a_log=v_a_log, d_skip=v_d_skip, ssd_norm_w=v_ssd_norm_w, sb_norm_w=v_sb_norm_w, w_out=v_w_out,
              norm_mem_w=v_norm_mem_w, norm_memkv_w=v_norm_memkv_w, w_mq=v_w_mq, w_mk=v_w_mk, w_mv=v_w_mv, w_mo=v_w_mo,
              norm_ffn_w=v_norm_ffn_w, w_up=v_w_up, conv_ffn_w=v_conv_ffn_w, conv_ffn_b=v_conv_ffn_b, w_down=v_w_down,
              norm_final_w=v_norm_final_w)
    shapes = {n: W[n].shape for n in WEIGHTS}
    sh2 = {n: (1, a.shape[-1]) if a.ndim < 3 else a.shape[-2:] for n, a in W.items()}
    w2 = {n: W[n].reshape(sh2[n]) for n in WEIGHTS}
    x2d = x[0]
    S, D = x2d.shape
    H, P, N = SSD_HEADS, SSD_HEAD_DIM, SSD_STATE

    cv_rows = _ceil_to(-(-sum(math.prod(sh2[n]) for n in CONVW) // 128), 32)
    cpack = _pack_rows([w2[n][None] for n in CONVW], 128, cv_rows)[0]
    stacked = jnp.concatenate([w2[n].astype(_MXU) for n in ROW_SPLIT], axis=0)
    cidx = lax.axis_index("c")
    oidx = 2 * lax.axis_index("x") + lax.axis_index("y")
    now, later = [w2['w_in'].astype(_MXU), cpack], [stacked, w2['w_up'].astype(_MXU)]
    g_in, call = [lax.dynamic_update_index_in_dim(g, m, oidx, 0) for m, g in zip(now, _gather_chips(now, "gather_weights"))]
    full = {'w_in': _from_shards('w_in', g_in)}
    full.update({n: _from_shards(n, a) for n, a in zip(CONVW, _unpack_rows(call, [sh2[n] for n in CONVW]))})

    o1 = SSD_INNER
    o2 = o1 + SSD_XBC
    o3 = o2 + SSD_HEADS
    Wi = full['w_in']
    W_z, W_xbc, W_qkv = Wi[:, :o1], Wi[:, o1:o2], Wi[:, o3:]
    W_dt = jnp.pad(Wi[:, o2:o3], ((0, 0), (0, DT_PAD - SSD_HEADS)))
    W_in_r = jnp.concatenate([W_z, W_xbc, W_qkv, W_dt], axis=1)
    dskip_rep = jnp.repeat(w2['d_skip'], P, axis=1)

    h1 = _rms_fwd(x2d, w2['norm_mix_w'], "norm_mix")
    z = _mm(h1, W_z, name="proj_z")
    xbc = _mm(h1, W_xbc, name="proj_xbc")
    dtp = _mm(h1, W_dt, name="proj_dt")
    qkv = _mm(h1, W_qkv, out_dtype=_MXU, name="proj_qkv")
    pre = _dwconv_fwd(xbc, full['conv_ssd_w'], w2['conv_ssd_b'], "ssd_conv")
    act = _silu_fwd(pre, "ssd_conv_silu")
    dt, cs = _ssd_prep(dtp, w2['dt_bias'], w2['a_log'], "ssd_prep")
    csT = cs.T
    def heads(a, nh):
        return a.reshape(S, nh, a.shape[1] // nh).transpose(1, 0, 2)

    def unheads(a):
        return a.transpose(1, 0, 2).reshape(S, a.shape[0] * a.shape[2])

    xs_h = heads(act[:, :o1], H)
    Bm = heads(act[:, o1:o1 + SSD_GROUPS * N], SSD_GROUPS)
    Cm = heads(act[:, o1 + SSD_GROUPS * N:], SSD_GROUPS)
    y_h, prev = _ssdg_fwd(xs_h, Bm, Cm, dt, cs, csT, "ssd_scan")
    y_scan = unheads(y_h)
    y_ssd = _ssd_gate_fwd(y_scan, act, z, dskip_rep, w2['ssd_norm_w'], "ssd_gate")
    nsb = SB_WIDTH // (SB_GROUP * SB_LANES)
    qkv_cols = (0, nsb, 2 * nsb)
    o_sb, y_sb, *others = _sbg_fwd(qkv, qkv, qkv, qkv_cols, w2['sb_norm_w'], "sb_attn", gather=later)
    g_rows, g_up = [lax.dynamic_update_index_in_dim(g, m, oidx, 0) for m, g in zip(later, others)]
    full['w_up'] = _from_shards('w_up', g_up)
    full.update({n: _from_shards(n, a) for n, a in zip(ROW_SPLIT, _split_rows(g_rows, [sh2[n][0] for n in ROW_SPLIT]))})
    ycat = jnp.concatenate([y_ssd, y_sb], axis=1)
    x_2 = _mm(ycat, full['w_out'], res=x2d, name="out_proj")
    h2 = _rms_fwd(x_2, w2['norm_mem_w'], "norm_mem")
    qm = _mm(h2, full['w_mq'], out_dtype=_MXU, name="mem_q")
    mn = _rms_fwd(mem[0], w2['norm_memkv_w'], "norm_memkv")
    km = _mm(mn, full['w_mk'], out_dtype=_MXU, name="mem_k")
    vm = _mm(mn, full['w_mv'], out_dtype=_MXU, name="mem_v")
    om = _xattn_fwd(qm, km, vm, "mem_attn")
    x_3 = _mm(om, full['w_mo'], res=x_2, name="mem_o")
    h3 = _rms_fwd(x_3, w2['norm_ffn_w'], "norm_ffn")
    up = _mm(h3, full['w_up'], name="ffn_up")
    a_ffn = _conv_glu_fwd(up, full['conv_ffn_w'], w2['conv_ffn_b'], "ffn_conv_glu")
    x_4 = _mm(a_ffn, full['w_down'], res=x_3, name="ffn_down")
    dx4, dx4b, g_final, loss_blk = _loss_bwd(x_4, loss_target[0], w2['norm_final_w'], "loss_head")

    G = {'norm_final_w': g_final}
    dact = _mm(dx4b, full['w_down'], tb=True, name="d_ffn_act")
    G['w_down'] = _mm(a_ffn, dx4b, ta=True, name="g_w_down")
    du, G['conv_ffn_w'], G['conv_ffn_b'] = _conv_glu_bwd(up, dact, full['conv_ffn_w'], w2['conv_ffn_b'], "d_ffn_conv_glu")
    dup = _dwconv_bwd_x(du, full['conv_ffn_w'], "d_ffn_conv")
    dh3 = _mm(dup, full['w_up'], tb=True, name="d_h3")
    G['w_up'] = _mm(h3, dup, ta=True, name="g_w_up")
    dx3, dx3b, G['norm_ffn_w'] = _rms_bwd(dh3, x_3, w2['norm_ffn_w'], dx4, "d_norm_ffn")
    dom = _mm(dx3b, full['w_mo'], tb=True, out_dtype=_MXU, name="d_mem_o")
    G['w_mo'] = _mm(om, dx3b, ta=True, name="g_w_mo")
    dqm, dkm, dvm = _xattn_bwd(qm, km, vm, dom, "d_mem_attn")
    G['w_mq'] = _mm(h2, dqm, ta=True, name="g_w_mq")
    dh2 = _mm(dqm, full['w_mq'], tb=True, name="d_h2")
    dx2, dx2b, G['norm_mem_w'] = _rms_bwd(dh2, x_2, w2['norm_mem_w'], dx3, "d_norm_mem")
    G['w_mk'] = _mm(mn, dkm, ta=True, name="g_w_mk")
    G['w_mv'] = _mm(mn, dvm, ta=True, name="g_w_mv")
    dmn = _mm(dvm, full['w_mv'], tb=True, res=_mm(dkm, full['w_mk'], tb=True, name="d_mn_k"), name="d_mn_v")
    _, _, G['norm_memkv_w'] = _rms_bwd(dmn, mem[0], w2['norm_memkv_w'], None, "d_norm_memkv")
    dycat = _mm(dx2b, full['w_out'], tb=True, name="d_ycat")
    G['w_out'] = _mm(ycat, dx2b, ta=True, name="g_w_out")
    dy1, dz, g_dskip_lane, G['ssd_norm_w'] = _ssd_gate_bwd(dycat, y_scan, act, z, dskip_rep, w2['ssd_norm_w'], "d_ssd_gate")
    dxs_h, dB, dC, ddt, dA = _ssdg_bwd(xs_h, Bm, Cm, dt, cs, csT, prev, heads(dy1, H), w2['a_log'], w2['d_skip'], "d_ssd_scan")
    G['d_skip'] = jnp.sum(g_dskip_lane.reshape(H, P), axis=1)[None, :]
    dact_xbc = jnp.concatenate([unheads(dxs_h), unheads(dB), unheads(dC)], axis=1)
    dpre = _silu_bwd(pre, dact_xbc, "d_ssd_conv_silu")
    G['conv_ssd_w'], G['conv_ssd_b'] = _conv_bwd_w(xbc, dpre, full['conv_ssd_w'].shape[0], "g_ssd_conv")
    dxbc = _dwconv_bwd_x(dpre, full['conv_ssd_w'], "d_ssd_conv")
    ddtp, G['dt_bias'], G['a_log'] = _dt_bwd(ddt, dA, dtp, w2['dt_bias'], w2['a_log'], "d_dt")
    dq, dk, dv, G['sb_norm_w'] = _sbg_bwd(qkv, qkv, qkv, qkv_cols, o_sb, dycat, o1 // (SB_GROUP * SB_LANES), w2['sb_norm_w'], "d_sb_attn")
    dproj = jnp.concatenate([dz, dxbc, dq, dk.astype(_MXU), dv.astype(_MXU), ddtp], axis=1)
    dh1 = _mm(dproj, W_in_r, tb=True, name="d_h1")
    g_in_r = _mm(h1, dproj, ta=True, name="g_w_in")
    nq = 3 * SB_WIDTH

    def in_cols(lo, hi):
        spans = []
        for a, b, shift in ((0, o2, 0), (o2, o3, nq), (o3, o3 + nq, o2 - o3)):
            s, e = max(lo, a), min(hi, b)
            if s < e:
                spans.append((s + shift, e + shift))
        return spans

    hr_in, cs_in = g_in_r.shape[0] // 2, (o3 + nq) // 4
    g_in_pair = jnp.stack([jnp.stack([jnp.concatenate([g_in_r[h * hr_in:(h + 1) * hr_in, s:e] for s, e in in_cols(j * cs_in, (j + 1) * cs_in)],
                                                      axis=1) for j in range(4)]) for h in range(2)])
    grad_x, _, G['norm_mix_w'] = _rms_bwd(dh1, x2d, w2['norm_mix_w'], dx2, "d_norm_mix")

    by_owner = [jnp.concatenate([_to_shards(n, G[n]) for n in ROW_SPLIT], axis=1), _to_shards('w_up', G['w_up'])]
    to_pair = [a.reshape(4, 2, a.shape[1] // 2, a.shape[2]).transpose(1, 0, 2, 3) for a in by_owner]
    to_pair.insert(1, g_in_pair)
    got = _exchange(to_pair, group='c', same_src=False, own=False, chunks=4, name="reduce_pair")
    pair, pair_wire = [], []
    for t, g in zip(to_pair, got):
        _, _, r, cw = t.shape
        full_sum, wire_sum = _sum_lead(g.reshape(1, 4 * r, cw), "reduce_pair_sum%d" % len(pair), first=t.reshape(2, 4 * r, cw),
                                       pick=cidx.reshape(1), wire=True)
        pair.append(full_sum.reshape(4, r, cw))
        pair_wire.append(wire_sum.reshape(4, r, cw))
    got = _exchange(pair_wire, group='xy', same_src=False, own=False, chunks=1, name="reduce_chips")
    chips = [_sum_lead(g, "reduce_chips_sum%d" % k, first=p, pick=oidx.reshape(1)) for k, (p, g) in enumerate(zip(pair, got))]
    got = _exchange(chips, group='c', same_src=True, own=False, chunks=4, name="share_pair")
    red = [jnp.where(cidx == 0, jnp.concatenate([m, g[0]], axis=0), jnp.concatenate([g[0], m], axis=0))[None]
           for m, g in zip(chips, got)]
    gsh = dict(zip(ROW_SPLIT, [a[0] for a in _split_rows(red[0], [sh2[n][0] for n in ROW_SPLIT])]))
    gsh['w_in'], gsh['w_up'] = red[1][0], red[2][0]

    small_parts = [G[n].reshape(1, -1) for n in SMALL + CONVW] + [loss_blk[:1, :1]]
    small_shapes = [sh2[n] for n in SMALL] + [G[n].shape for n in CONVW] + [(1, 1)]
    small_rows = _ceil_to(-(-sum(math.prod(s) for s in small_shapes) // 128), 8)
    spack = _pack_rows(small_parts, 128, small_rows)[0]
    (gathered,) = _exchange([spack], group='xyc', same_src=True, own=True, chunks=1, name="gather_small")
    parts = [a[0] for a in _unpack_rows(_sum_lead(gathered, "small_sum")[None], small_shapes)]
    gsh.update(zip(SMALL, parts))
    for n, a in zip(CONVW, parts[len(SMALL):-1]):
        gsh[n] = lax.dynamic_index_in_dim(_to_shards(n, a), oidx, 0, keepdims=False)
    loss = parts[-1].reshape(())

    delta, new_m, new_v = {}, {}, {}
    for n in BIG:
        delta[n], new_m[n], new_v[n] = _adamw(w2[n], gsh[n], Mo[n].reshape(sh2[n]), Vo[n].reshape(sh2[n]), "adamw_" + n)
    for grp, width, tag in ((CONVW, 128, "adamw_conv"), (SMALL, 128, "adamw_small")):
        rows = _ceil_to(-(-sum(math.prod(sh2[n]) for n in grp) // width), 8)
        packed = [_pack_rows([src[n].reshape(1, -1) for n in grp], width, rows)[0]
                  for src in (w2, gsh, {n: Mo[n] for n in grp}, {n: Vo[n] for n in grp})]
        outs = _adamw(*packed, tag)
        for dst, o in zip((delta, new_m, new_v), outs):
            dst.update(zip(grp, [a[0] for a in _unpack_rows(o[None], [sh2[n] for n in grp])]))

    def shaped(d):
        return [d[n].reshape(shapes[n]) for n in WEIGHTS]

    return (loss, grad_x[None], *shaped(gsh), *shaped(delta), *shaped(new_m), *shaped(new_v))
```

```python
import math

import jax
import jax.numpy as jnp
from jax import lax
from jax.experimental import pallas as pl
from jax.experimental.pallas import tpu as pltpu

F32 = jnp.float32
_MXU = jnp.bfloat16
_WIRE = jnp.bfloat16
EPS = 1e-6
_VMEM_LIMIT = 48 * 1024 * 1024
_HI = lax.Precision.HIGHEST

SSD_HEADS = 16
SSD_HEAD_DIM = 64
SSD_GROUPS = 2
SSD_STATE = 128
SSD_CHUNK = 128
SSD_INNER = SSD_HEADS * SSD_HEAD_DIM
SSD_XBC = SSD_INNER + 2 * SSD_GROUPS * SSD_STATE
SB_HEADS = 16
SB_HEAD_DIM = 64
SB_WIDTH = SB_HEADS * SB_HEAD_DIM
MEM_HEADS = 4
DT_PAD = 128

ADAM_LR = 0.001
ADAM_B1 = 0.9
ADAM_B2 = 0.999
ADAM_EPS = 1e-08
ADAM_WD = 0.01
ADAM_STEP = 10

WEIGHTS = ['norm_mix_w', 'w_in', 'conv_ssd_w', 'conv_ssd_b', 'dt_bias', 'a_log', 'd_skip', 'ssd_norm_w',
           'sb_norm_w', 'w_out', 'norm_mem_w', 'norm_memkv_w', 'w_mq', 'w_mk', 'w_mv', 'w_mo', 'norm_ffn_w',
           'w_up', 'conv_ffn_w', 'conv_ffn_b', 'w_down', 'norm_final_w']
BIG = ['w_in', 'w_out', 'w_mq', 'w_mk', 'w_mv', 'w_mo', 'w_up', 'w_down']
COL_SPLIT = ('w_in', 'w_up', 'conv_ssd_w', 'conv_ffn_w')
ROW_SPLIT = ['w_out', 'w_mq', 'w_mk', 'w_mv', 'w_mo', 'w_down']
CONVW = ['conv_ssd_w', 'conv_ffn_w']
SMALL = ['norm_mix_w', 'conv_ssd_b', 'dt_bias', 'a_log', 'd_skip', 'ssd_norm_w', 'sb_norm_w', 'norm_mem_w',
         'norm_memkv_w', 'norm_ffn_w', 'conv_ffn_b', 'norm_final_w']


def _cparams(*sem):
    return pltpu.CompilerParams(dimension_semantics=sem if sem else None, vmem_limit_bytes=_VMEM_LIMIT)


def _pick(n, cap, mult=128):
    best = None
    for d in range(mult, min(n, cap) + 1, mult):
        if n % d == 0:
            best = d
    return n if best is None else best


def _dot(a, b, ca, cb):
    return lax.dot_general(a.astype(_MXU), b.astype(_MXU), (((ca,), (cb,)), ((), ())), preferred_element_type=F32)


def _sigmoid(v):
    return 1.0 / (1.0 + jnp.exp(-v))


def _log1p(u):
    w = 1.0 + u
    return jnp.where(w == 1.0, u, jnp.log(w) * (u / (w - 1.0)))


def _mm(a, b, *, ta=False, tb=False, res=None, out_dtype=F32, name):
    if ta:
        K, M = a.shape
    else:
        M, K = a.shape
    if tb:
        N, K2 = b.shape
    else:
        K2, N = b.shape
    assert K == K2, (a.shape, b.shape)
    tm = _pick(M, 1408, 128 if ta else 16)
    tn = _pick(N, 1536)
    tk = _pick(K, 1536)
    nk = K // tk
    a_spec = pl.BlockSpec((tk, tm), lambda i, j, k: (k, i)) if ta else pl.BlockSpec((tm, tk), lambda i, j, k: (i, k))
    b_spec = pl.BlockSpec((tn, tk), lambda i, j, k: (j, k)) if tb else pl.BlockSpec((tk, tn), lambda i, j, k: (k, j))
    o_spec = pl.BlockSpec((tm, tn), lambda i, j, k: (i, j))
    ca, cb = (0 if ta else 1), (1 if tb else 0)

    def body(*refs):
        if res is None:
            a_ref, b_ref, o_ref, acc_ref = refs
            r_ref = None
        else:
            a_ref, b_ref, r_ref, o_ref, acc_ref = refs
        k = pl.program_id(2)

        @pl.when(k == 0)
        def _():
            acc_ref[...] = jnp.zeros_like(acc_ref)

        acc_ref[...] += _dot(a_ref[...], b_ref[...], ca, cb)

        @pl.when(k == nk - 1)
        def _():
            r = acc_ref[...]
            if r_ref is not None:
                r = r + r_ref[...].astype(F32)
            o_ref[...] = r.astype(o_ref.dtype)

    ins = [a, b] + ([] if res is None else [res])
    in_specs = [a_spec, b_spec] + ([] if res is None else [o_spec])
    return pl.pallas_call(
        body, grid=(M // tm, N // tn, nk), in_specs=in_specs, out_specs=o_spec,
        out_shape=jax.ShapeDtypeStruct((M, N), out_dtype), scratch_shapes=[pltpu.VMEM((tm, tn), F32)],
        compiler_params=_cparams("parallel", "parallel", "arbitrary"), name=name)(*ins)


def _rows(S, cap):
    return _pick(S, cap, 8)


def _rms_fwd(x, w, name):
    S, D = x.shape
    tm = _rows(S, 512)

    def body(x_ref, w_ref, o_ref):
        xv = x_ref[...]
        r = lax.rsqrt(jnp.mean(xv * xv, axis=-1, keepdims=True) + EPS)
        o_ref[...] = ((xv * r) * w_ref[...]).astype(o_ref.dtype)

    row = pl.BlockSpec((tm, D), lambda i: (i, 0))
    return pl.pallas_call(body, grid=(S // tm,), in_specs=[row, pl.BlockSpec((1, D), lambda i: (0, 0))], out_specs=row,
                          out_shape=jax.ShapeDtypeStruct((S, D), _MXU), compiler_params=_cparams("parallel"), name=name)(x, w)


def _rms_bwd(dh, x, w, dres, name):
    S, D = x.shape
    tm = _rows(S, 256)

    def body(*refs):
        if dres is None:
            dh_ref, x_ref, w_ref, dx_ref, dxb_ref, dw_ref = refs
            dres_ref = None
        else:
            dh_ref, x_ref, w_ref, dres_ref, dx_ref, dxb_ref, dw_ref = refs
        xv = x_ref[...]
        r = lax.rsqrt(jnp.mean(xv * xv, axis=-1, keepdims=True) + EPS)
        xn = xv * r
        dy = dh_ref[...].astype(F32)

        @pl.when(pl.program_id(0) == 0)
        def _():
            dw_ref[...] = jnp.zeros_like(dw_ref)

        dw_ref[...] += jnp.sum(dy * xn, axis=0, keepdims=True)
        dxn = dy * w_ref[...]
        dx = r * (dxn - xn * jnp.mean(dxn * xn, axis=-1, keepdims=True))
        if dres_ref is not None:
            dx = dx + dres_ref[...]
        dx_ref[...] = dx
        dxb_ref[...] = dx.astype(dxb_ref.dtype)

    row = pl.BlockSpec((tm, D), lambda i: (i, 0))
    vec = pl.BlockSpec((1, D), lambda i: (0, 0))
    ins = [dh, x, w] + ([] if dres is None else [dres])
    in_specs = [row, row, vec] + ([] if dres is None else [row])
    return pl.pallas_call(
        body, grid=(S // tm,), in_specs=in_specs, out_specs=[row, row, vec],
        out_shape=[jax.ShapeDtypeStruct((S, D), F32), jax.ShapeDtypeStruct((S, D), _MXU), jax.ShapeDtypeStruct((1, D), F32)],
        compiler_params=_cparams("arbitrary"), name=name)(*ins)


def _loss_bwd(x, tgt, w, name):
    S, D = x.shape
    tm = _rows(S, 256)

    def body(x_ref, t_ref, w_ref, dx_ref, dxb_ref, dw_ref, loss_ref):
        xv = x_ref[...]
        r = lax.rsqrt(jnp.mean(xv * xv, axis=-1, keepdims=True) + EPS)
        xn = xv * r
        e = xn * w_ref[...] - t_ref[...]

        @pl.when(pl.program_id(0) == 0)
        def _():
            dw_ref[...] = jnp.zeros_like(dw_ref)
            loss_ref[...] = jnp.zeros_like(loss_ref)

        tok = jnp.mean(e * e, axis=-1, keepdims=True)
        loss_ref[...] += jnp.broadcast_to(0.5 * jnp.sum(tok, axis=0, keepdims=True), loss_ref.shape)
        dy = e * (1.0 / D)
        dw_ref[...] += jnp.sum(dy * xn, axis=0, keepdims=True)
        dxn = dy * w_ref[...]
        dx = r * (dxn - xn * jnp.mean(dxn * xn, axis=-1, keepdims=True))
        dx_ref[...] = dx
        dxb_ref[...] = dx.astype(dxb_ref.dtype)

    row = pl.BlockSpec((tm, D), lambda i: (i, 0))
    vec = pl.BlockSpec((1, D), lambda i: (0, 0))
    return pl.pallas_call(
        body, grid=(S // tm,), in_specs=[row, row, vec],
        out_specs=[row, row, vec, pl.BlockSpec((8, 128), lambda i: (0, 0))],
        out_shape=[jax.ShapeDtypeStruct((S, D), F32), jax.ShapeDtypeStruct((S, D), _MXU),
                   jax.ShapeDtypeStruct((1, D), F32), jax.ShapeDtypeStruct((8, 128), F32)],
        compiler_params=_cparams("arbitrary"), name=name)(x, tgt, w)


def _conv_tiles(S, C):
    return _rows(S, 256), _pick(C, 1536)


def _dwconv_fwd(x, w, b, name):
    S, C = x.shape
    K = w.shape[0]
    tm, tc = _conv_tiles(S, C)

    def body(x_ref, p_ref, w_ref, b_ref, o_ref):
        cur = x_ref[...]
        prev = jnp.where(pl.program_id(0) > 0, p_ref[...], 0.0)
        xx = jnp.concatenate([prev, cur], axis=0)
        acc = cur * w_ref[K - 1:K, :] + b_ref[...]
        for d in range(1, K):
            acc = acc + pltpu.roll(xx, d, 0)[8:, :] * w_ref[K - 1 - d:K - d, :]
        o_ref[...] = acc

    return pl.pallas_call(
        body, grid=(S // tm, C // tc),
        in_specs=[pl.BlockSpec((tm, tc), lambda i, j: (i, j)),
                  pl.BlockSpec((8, tc), lambda i, j: (jnp.maximum(i * (tm // 8) - 1, 0), j)),
                  pl.BlockSpec((K, tc), lambda i, j: (0, j)), pl.BlockSpec((1, tc), lambda i, j: (0, j))],
        out_specs=pl.BlockSpec((tm, tc), lambda i, j: (i, j)), out_shape=jax.ShapeDtypeStruct((S, C), F32),
        compiler_params=_cparams("parallel", "parallel"), name=name)(x, x, w, b)


def _conv_bwd_w(x, dy, K, name):
    S, C = x.shape
    tm, tc = _conv_tiles(S, C)

    def body(x_ref, p_ref, dy_ref, dw_ref, db_ref):
        i = pl.program_id(1)

        @pl.when(i == 0)
        def _():
            dw_ref[...] = jnp.zeros_like(dw_ref)
            db_ref[...] = jnp.zeros_like(db_ref)

        cur = x_ref[...]
        prev = jnp.where(i > 0, p_ref[...], 0.0)
        xx = jnp.concatenate([prev, cur], axis=0)
        g = dy_ref[...].astype(F32)
        db_ref[...] += jnp.sum(g, axis=0, keepdims=True)
        dw_ref[K - 1:K, :] += jnp.sum(g * cur, axis=0, keepdims=True)
        for d in range(1, K):
            dw_ref[K - 1 - d:K - d, :] += jnp.sum(g * pltpu.roll(xx, d, 0)[8:, :], axis=0, keepdims=True)

    return pl.pallas_call(
        body, grid=(C // tc, S // tm),
        in_specs=[pl.BlockSpec((tm, tc), lambda j, i: (i, j)),
                  pl.BlockSpec((8, tc), lambda j, i: (jnp.maximum(i * (tm // 8) - 1, 0), j)),
                  pl.BlockSpec((tm, tc), lambda j, i: (i, j))],
        out_specs=[pl.BlockSpec((K, tc), lambda j, i: (0, j)), pl.BlockSpec((1, tc), lambda j, i: (0, j))],
        out_shape=[jax.ShapeDtypeStruct((K, C), F32), jax.ShapeDtypeStruct((1, C), F32)],
        compiler_params=_cparams("parallel", "arbitrary"), name=name)(x, x, dy)


def _dwconv_bwd_x(dy, w, name):
    S, C = dy.shape
    K = w.shape[0]
    tm, tc = _conv_tiles(S, C)
    last = S // tm - 1
    hr = 8 * (4 // dy.dtype.itemsize)

    def body(g_ref, n_ref, w_ref, o_ref):
        cur = g_ref[...].astype(F32)
        nxt = jnp.where(pl.program_id(0) < last, n_ref[...].astype(F32), 0.0)
        xx = jnp.concatenate([cur, nxt], axis=0)
        acc = cur * w_ref[K - 1:K, :]
        for d in range(1, K):
            acc = acc + pltpu.roll(xx, tm + hr - d, 0)[:tm, :] * w_ref[K - 1 - d:K - d, :]
        o_ref[...] = acc.astype(o_ref.dtype)

    return pl.pallas_call(
        body, grid=(S // tm, C // tc),
        in_specs=[pl.BlockSpec((tm, tc), lambda i, j: (i, j)),
                  pl.BlockSpec((hr, tc), lambda i, j: (jnp.minimum((i + 1) * (tm // hr), S // hr - 1), j)),
                  pl.BlockSpec((K, tc), lambda i, j: (0, j))],
        out_specs=pl.BlockSpec((tm, tc), lambda i, j: (i, j)), out_shape=jax.ShapeDtypeStruct((S, C), _MXU),
        compiler_params=_cparams("parallel", "parallel"), name=name)(dy, dy, w)


def _silu_fwd(pre, name):
    S, C = pre.shape
    tm, tc = _conv_tiles(S, C)

    def body(p_ref, o_ref):
        p = p_ref[...]
        o_ref[...] = p * _sigmoid(p)

    blk = pl.BlockSpec((tm, tc), lambda i, j: (i, j))
    return pl.pallas_call(body, grid=(S // tm, C // tc), in_specs=[blk], out_specs=blk,
                          out_shape=jax.ShapeDtypeStruct((S, C), F32), compiler_params=_cparams("parallel", "parallel"),
                          name=name)(pre)


def _silu_bwd(pre, dact, name):
    S, C = pre.shape
    tm, tc = _conv_tiles(S, C)

    def body(p_ref, g_ref, o_ref):
        p = p_ref[...]
        s = _sigmoid(p)
        o_ref[...] = g_ref[...] * (s * (1.0 + p * (1.0 - s)))

    blk = pl.BlockSpec((tm, tc), lambda i, j: (i, j))
    return pl.pallas_call(body, grid=(S // tm, C // tc), in_specs=[blk, blk], out_specs=blk,
                          out_shape=jax.ShapeDtypeStruct((S, C), F32), compiler_params=_cparams("parallel", "parallel"),
                          name=name)(pre, dact)


def _shifted_rows(cur, prev, K):
    xx = jnp.concatenate([prev, cur], axis=0)
    return [cur] + [pltpu.roll(xx, d, 0)[8:, :] for d in range(1, K)]


def _conv_glu_fwd(x, w, b, name):
    S, C = x.shape
    K = w.shape[0]
    Fh = C // 2
    tm = _rows(S, 128)

    def body(x_ref, p_ref, w_ref, b_ref, o_ref):
        sh = _shifted_rows(x_ref[...], jnp.where(pl.program_id(0) > 0, p_ref[...], 0.0), K)
        u = b_ref[...] + sum(sh[d] * w_ref[K - 1 - d:K - d, :] for d in range(K))
        g = u[:, :Fh]
        o_ref[...] = (g * _sigmoid(g) * u[:, Fh:]).astype(o_ref.dtype)

    return pl.pallas_call(
        body, grid=(S // tm,),
        in_specs=[pl.BlockSpec((tm, C), lambda i: (i, 0)), pl.BlockSpec((8, C), lambda i: (jnp.maximum(i * (tm // 8) - 1, 0), 0)),
                  pl.BlockSpec((K, C), lambda i: (0, 0)), pl.BlockSpec((1, C), lambda i: (0, 0))],
        out_specs=pl.BlockSpec((tm, Fh), lambda i: (i, 0)), out_shape=jax.ShapeDtypeStruct((S, Fh), _MXU),
        compiler_params=_cparams("parallel"), name=name)(x, x, w, b)


def _conv_glu_bwd(x, dact, w, b, name):
    S, C = x.shape
    K = w.shape[0]
    Fh = C // 2
    tm = _rows(S, 128)

    def body(x_ref, p_ref, g_ref, w_ref, b_ref, du_ref, dw_ref, db_ref):
        i = pl.program_id(0)

        @pl.when(i == 0)
        def _():
            dw_ref[...] = jnp.zeros_like(dw_ref)
            db_ref[...] = jnp.zeros_like(db_ref)

        sh = _shifted_rows(x_ref[...], jnp.where(i > 0, p_ref[...], 0.0), K)
        u = b_ref[...] + sum(sh[d] * w_ref[K - 1 - d:K - d, :] for d in range(K))
        g = u[:, :Fh]
        da = g_ref[...].astype(F32)
        s = _sigmoid(g)
        halves = ((slice(0, Fh), da * u[:, Fh:] * (s * (1.0 + g * (1.0 - s)))), (slice(Fh, C), da * (g * s)))
        for cols, du in halves:
            du_ref[:, cols] = du.astype(du_ref.dtype)
            db_ref[:, cols] += jnp.sum(du, axis=0, keepdims=True)
            for d in range(K):
                dw_ref[K - 1 - d:K - d, cols] += jnp.sum(du * sh[d][:, cols], axis=0, keepdims=True)

    return pl.pallas_call(
        body, grid=(S // tm,),
        in_specs=[pl.BlockSpec((tm, C), lambda i: (i, 0)), pl.BlockSpec((8, C), lambda i: (jnp.maximum(i * (tm // 8) - 1, 0), 0)),
                  pl.BlockSpec((tm, Fh), lambda i: (i, 0)), pl.BlockSpec((K, C), lambda i: (0, 0)), pl.BlockSpec((1, C), lambda i: (0, 0))],
        out_specs=[pl.BlockSpec((tm, C), lambda i: (i, 0)), pl.BlockSpec((K, C), lambda i: (0, 0)), pl.BlockSpec((1, C), lambda i: (0, 0))],
        out_shape=[jax.ShapeDtypeStruct((S, C), _MXU), jax.ShapeDtypeStruct((K, C), F32), jax.ShapeDtypeStruct((1, C), F32)],
        compiler_params=_cparams("arbitrary"), name=name)(x, x, dact, w, b)


def _xattn_fwd(q, k, v, name):
    S, D = q.shape
    M = k.shape[0]
    hd = D // MEM_HEADS
    tm = _rows(S, 512)
    scale = 1.0 / math.sqrt(hd)

    def body(q_ref, k_ref, v_ref, o_ref):
        for h in range(MEM_HEADS):
            sl = slice(h * hd, (h + 1) * hd)
            s = _dot(q_ref[:, sl], k_ref[:, sl], 1, 1) * scale
            p = jnp.exp(s - jnp.max(s, axis=-1, keepdims=True))
            p = p / jnp.sum(p, axis=-1, keepdims=True)
            o_ref[:, sl] = _dot(p, v_ref[:, sl], 1, 0).astype(o_ref.dtype)

    kv = pl.BlockSpec((M, D), lambda i: (0, 0))
    row = pl.BlockSpec((tm, D), lambda i: (i, 0))
    return pl.pallas_call(body, grid=(S // tm,), in_specs=[row, kv, kv], out_specs=row,
                          out_shape=jax.ShapeDtypeStruct((S, D), _MXU), compiler_params=_cparams("parallel"), name=name)(q, k, v)


def _xattn_bwd(q, k, v, do, name):
    S, D = q.shape
    M = k.shape[0]
    hd = D // MEM_HEADS
    tm = _rows(S, 512)
    scale = 1.0 / math.sqrt(hd)

    def body(q_ref, k_ref, v_ref, do_ref, dq_ref, dk_ref, dv_ref):
        @pl.when(pl.program_id(0) == 0)
        def _():
            dk_ref[...] = jnp.zeros_like(dk_ref)
            dv_ref[...] = jnp.zeros_like(dv_ref)

        for h in range(MEM_HEADS):
            sl = slice(h * hd, (h + 1) * hd)
            qh, kh, vh, doh = q_ref[:, sl], k_ref[:, sl], v_ref[:, sl], do_ref[:, sl]
            s = _dot(qh, kh, 1, 1) * scale
            p = jnp.exp(s - jnp.max(s, axis=-1, keepdims=True))
            p = p / jnp.sum(p, axis=-1, keepdims=True)
            dp = _dot(doh, vh, 1, 1)
            dv_ref[:, sl] += _dot(p, doh, 0, 0)
            ds = (p * (dp - jnp.sum(dp * p, axis=-1, keepdims=True))) * scale
            dq_ref[:, sl] = _dot(ds, kh, 1, 0).astype(dq_ref.dtype)
            dk_ref[:, sl] += _dot(ds, qh, 0, 0)

    kv = pl.BlockSpec((M, D), lambda i: (0, 0))
    row = pl.BlockSpec((tm, D), lambda i: (i, 0))
    return pl.pallas_call(
        body, grid=(S // tm,), in_specs=[row, kv, kv, row], out_specs=[row, kv, kv],
        out_shape=[jax.ShapeDtypeStruct((S, D), _MXU), jax.ShapeDtypeStruct((M, D), F32), jax.ShapeDtypeStruct((M, D), F32)],
        compiler_params=_cparams("arbitrary"), name=name)(q, k, v, do)


def _tri(n, strict, upper):
    r = lax.broadcasted_iota(jnp.int32, (n, n), 0)
    c = lax.broadcasted_iota(jnp.int32, (n, n), 1)
    if upper:
        return (c > r) if strict else (c >= r)
    return (r > c) if strict else (r >= c)


def _ssd_prep(dtp, dt_bias, a_log, name):
    S = dtp.shape[0]
    L, H = SSD_CHUNK, SSD_HEADS

    def body(p_ref, b_ref, al_ref, dt_ref, cs_ref):
        v = p_ref[:, :H] + b_ref[...]
        dt = jnp.maximum(v, 0.0) + _log1p(jnp.exp(-jnp.abs(v)))
        dt_ref[...] = dt
        a = dt * (-jnp.exp(al_ref[...]))
        cs_ref[...] = jnp.dot(_tri(L, False, False).astype(F32), a, precision=_HI, preferred_element_type=F32)

    blk = pl.BlockSpec((L, H), lambda c: (c, 0))
    vec = pl.BlockSpec((1, H), lambda c: (0, 0))
    return pl.pallas_call(body, grid=(S // L,), in_specs=[pl.BlockSpec((L, DT_PAD), lambda c: (c, 0)), vec, vec],
                          out_specs=[blk, blk], out_shape=[jax.ShapeDtypeStruct((S, H), F32)] * 2,
                          compiler_params=_cparams("parallel"), name=name)(dtp, dt_bias, a_log)


def _head_col(blk_ref, h):
    sel = lax.broadcasted_iota(jnp.int32, (1, SSD_HEADS), 1) == h
    return jnp.sum(jnp.where(sel, blk_ref[...], 0.0), axis=1, keepdims=True)


def _ssdg_fwd(xs, Bm, Cm, dt, cs, csT, name):
    H, S, P = xs.shape
    L, N = SSD_CHUNK, SSD_STATE
    nc = S // L
    rep = H // SSD_GROUPS
    hs = range(rep)

    def body(x_ref, b_ref, c_ref, dt_ref, cs_ref, csT_ref, y_ref, prev_ref, st_ref):
        c, g = pl.program_id(0), pl.program_id(1)

        @pl.when(c == 0)
        def _():
            for hh in hs:
                st_ref[g * rep + hh] = jnp.zeros((P, N), F32)

        Bv, Cv = b_ref[...], c_ref[...]
        tril = _tri(L, False, False)
        dtc = [_head_col(dt_ref, g * rep + hh) for hh in hs]
        csc = [_head_col(cs_ref, g * rep + hh) for hh in hs]
        csr = [csT_ref[hh:hh + 1, :] for hh in hs]
        last = [r[:, L - 1:L] for r in csr]
        xc = [x_ref[hh] * dtc[hh] for hh in hs]
        cb = _dot(Cv, Bv, 1, 1)
        m = [cb * jnp.where(tril, jnp.exp(jnp.where(tril, csc[hh] - csr[hh], 0.0)), 0.0) for hh in hs]
        prev = [st_ref[g * rep + hh] for hh in hs]
        yd = [_dot(m[hh], xc[hh], 1, 0) for hh in hs]
        yo = [_dot(Cv, prev[hh], 1, 1) for hh in hs]
        new = [_dot(xc[hh] * jnp.exp(last[hh] - csc[hh]), Bv, 0, 0) for hh in hs]
        for hh in hs:
            y_ref[hh] = yd[hh] + yo[hh] * jnp.exp(csc[hh])
            prev_ref[hh] = prev[hh]
            st_ref[g * rep + hh] = prev[hh] * jnp.exp(last[hh]) + new[hh]

    tok = pl.BlockSpec((L, H), lambda c, g: (c, 0))
    return pl.pallas_call(
        body, grid=(nc, SSD_GROUPS),
        in_specs=[pl.BlockSpec((rep, L, P), lambda c, g: (g, c, 0)), pl.BlockSpec((None, L, N), lambda c, g: (g, c, 0)),
                  pl.BlockSpec((None, L, N), lambda c, g: (g, c, 0)), tok, tok, pl.BlockSpec((rep, L), lambda c, g: (g, c))],
        out_specs=[pl.BlockSpec((rep, L, P), lambda c, g: (g, c, 0)),
                   pl.BlockSpec((rep, None, P, N), lambda c, g: (g, c, 0, 0))],
        out_shape=[jax.ShapeDtypeStruct((H, S, P), F32), jax.ShapeDtypeStruct((H, nc, P, N), F32)],
        scratch_shapes=[pltpu.VMEM((H, P, N), F32)],
        compiler_params=_cparams("arbitrary", "arbitrary"), name=name)(xs, Bm, Cm, dt, cs, csT)


def _ssdg_bwd(xs, Bm, Cm, dt, cs, csT, prev, dy, a_log, d_skip, name):
    H, S, P = xs.shape
    L, N = SSD_CHUNK, SSD_STATE
    nc = S // L
    rep = H // SSD_GROUPS
    hs = range(rep)

    def rowsum(a):
        return jnp.sum(a, axis=1, keepdims=True)

    def body(x_ref, b_ref, c_ref, dt_ref, cs_ref, csT_ref, prev_ref, dy_ref, al_ref, dk_ref,
             dx_ref, db_ref, dc_ref, ddt_ref, da_ref, g_ref):
        ci, g = pl.program_id(0), pl.program_id(1)

        @pl.when(ci == 0)
        def _():
            for hh in hs:
                g_ref[g * rep + hh] = jnp.zeros((P, N), F32)

        @pl.when((ci == 0) & (g == 0))
        def _():
            da_ref[...] = jnp.zeros_like(da_ref)

        @pl.when(g == 0)
        def _():
            ddt_ref[...] = jnp.zeros_like(ddt_ref)

        lane = lax.broadcasted_iota(jnp.int32, (1, H), 1)
        sel = [lane == g * rep + hh for hh in hs]
        A_h = [-jnp.exp(rowsum(jnp.where(s, al_ref[...], 0.0))) for s in sel]
        dsk = [rowsum(jnp.where(s, dk_ref[...], 0.0)) for s in sel]
        dtc = [_head_col(dt_ref, g * rep + hh) for hh in hs]
        csc = [_head_col(cs_ref, g * rep + hh) for hh in hs]
        csr = [csT_ref[hh:hh + 1, :] for hh in hs]
        last = [r[:, L - 1:L] for r in csr]
        Bv, Cv = b_ref[...], c_ref[...]
        xv = [x_ref[hh] for hh in hs]
        xc = [xv[hh] * dtc[hh] for hh in hs]
        dY = [dy_ref[hh] for hh in hs]
        prv = [prev_ref[hh] for hh in hs]
        G = [g_ref[g * rep + hh] for hh in hs]
        ecs = [jnp.exp(v) for v in csc]
        w = [jnp.exp(last[hh] - csc[hh]) for hh in hs]
        cd = [jnp.exp(v) for v in last]
        tril = _tri(L, False, False)
        triu = _tri(L, False, True)
        lam = [jnp.where(tril, jnp.exp(jnp.where(tril, csc[hh] - csr[hh], 0.0)), 0.0) for hh in hs]
        lamT = [jnp.where(triu, jnp.exp(jnp.where(triu, csr[hh] - csc[hh], 0.0)), 0.0) for hh in hs]
        cb = _dot(Cv, Bv, 1, 1)
        bc = _dot(Bv, Cv, 1, 1)
        dM = [_dot(dY[hh], xc[hh], 1, 1) for hh in hs]
        dMT = [_dot(xc[hh], dY[hh], 1, 1) for hh in hs]
        cp = [_dot(Cv, prv[hh], 1, 1) for hh in hs]
        BG = [_dot(Bv, G[hh], 1, 1) for hh in hs]
        dYe = [dY[hh] * ecs[hh] for hh in hs]
        dprev = [_dot(dYe[hh], Cv, 0, 0) for hh in hs]
        m = [cb * lam[hh] for hh in hs]
        mT = [bc * lamT[hh] for hh in hs]
        dxc = [_dot(mT[hh], dY[hh], 1, 0) + w[hh] * BG[hh] for hh in hs]
        dcb = sum([dM[hh] * lam[hh] for hh in hs][1:], dM[0] * lam[0])
        dcbT = sum([dMT[hh] * lamT[hh] for hh in hs][1:], dMT[0] * lamT[0])
        dC = _dot(dcb, Bv, 1, 0)
        dB = _dot(dcbT, Cv, 1, 0)
        for hh in hs:
            dC = dC + _dot(dYe[hh], prv[hh], 1, 0)
            dB = dB + _dot(xc[hh] * w[hh], G[hh], 1, 0)
        dc_ref[...] = dC
        db_ref[...] = dB
        ddt_acc = jnp.zeros((L, H), F32)
        da_acc = jnp.zeros((1, H), F32)
        rev = _tri(L, False, True).astype(F32)
        for hh in hs:
            dww = rowsum(xc[hh] * BG[hh]) * w[hh]
            dcs = (rowsum(dM[hh] * m[hh]) - rowsum(dMT[hh] * mT[hh]) + rowsum(dY[hh] * (cp[hh] * ecs[hh])) - dww)
            extra = jnp.sum(dww, axis=0, keepdims=True) + cd[hh] * jnp.sum(rowsum(G[hh] * prv[hh]), axis=0, keepdims=True)
            g_ref[g * rep + hh] = G[hh] * cd[hh] + dprev[hh]
            da = jnp.dot(rev, dcs, precision=_HI, preferred_element_type=F32) + extra
            dx_ref[hh] = dxc[hh] * dtc[hh] + dY[hh] * dsk[hh]
            ddt_acc = ddt_acc + jnp.where(sel[hh], da * A_h[hh] + rowsum(dxc[hh] * xv[hh]), 0.0)
            da_acc = da_acc + jnp.where(sel[hh], jnp.sum(da * dtc[hh], axis=0, keepdims=True), 0.0)
        ddt_ref[...] += ddt_acc
        da_ref[...] += da_acc

    rc = lambda ci: nc - 1 - ci
    hd = pl.BlockSpec((rep, L, P), lambda ci, g: (g, rc(ci), 0))
    grp = pl.BlockSpec((None, L, N), lambda ci, g: (g, rc(ci), 0))
    tok = pl.BlockSpec((L, H), lambda ci, g: (rc(ci), 0))
    vec = pl.BlockSpec((1, H), lambda ci, g: (0, 0))
    return pl.pallas_call(
        body, grid=(nc, SSD_GROUPS),
        in_specs=[hd, grp, grp, tok, tok, pl.BlockSpec((rep, L), lambda ci, g: (g, rc(ci))),
                  pl.BlockSpec((rep, None, P, N), lambda ci, g: (g, rc(ci), 0, 0)), hd, vec, vec],
        out_specs=[hd, grp, grp, tok, vec],
        out_shape=[jax.ShapeDtypeStruct((H, S, P), F32), jax.ShapeDtypeStruct((SSD_GROUPS, S, N), F32),
                   jax.ShapeDtypeStruct((SSD_GROUPS, S, N), F32), jax.ShapeDtypeStruct((S, H), F32),
                   jax.ShapeDtypeStruct((1, H), F32)],
        scratch_shapes=[pltpu.VMEM((H, P, N), F32)],
        compiler_params=_cparams("arbitrary", "arbitrary"), name=name)(xs, Bm, Cm, dt, cs, csT, prev, dy, a_log, d_skip)


def _dt_bwd(ddt, dA, dtp, dt_bias, a_log, name):
    S, H = ddt.shape
    tm = _rows(S, 512)

    def body(g_ref, da_ref, p_ref, b_ref, al_ref, o_ref, db_ref, dal_ref):
        @pl.when(pl.program_id(0) == 0)
        def _():
            db_ref[...] = jnp.zeros_like(db_ref)
            dal_ref[...] = da_ref[...] * (-jnp.exp(al_ref[...]))

        g = g_ref[...] * _sigmoid(p_ref[:, :H] + b_ref[...])
        db_ref[...] += jnp.sum(g, axis=0, keepdims=True)
        o_ref[...] = jnp.zeros_like(o_ref)
        o_ref[:, :H] = g.astype(o_ref.dtype)

    vec = pl.BlockSpec((1, H), lambda i: (0, 0))
    return pl.pallas_call(
        body, grid=(S // tm,),
        in_specs=[pl.BlockSpec((tm, H), lambda i: (i, 0)), vec, pl.BlockSpec((tm, DT_PAD), lambda i: (i, 0)), vec, vec],
        out_specs=[pl.BlockSpec((tm, DT_PAD), lambda i: (i, 0)), vec, vec],
        out_shape=[jax.ShapeDtypeStruct((S, DT_PAD), _MXU), jax.ShapeDtypeStruct((1, H), F32), jax.ShapeDtypeStruct((1, H), F32)],
        compiler_params=_cparams("arbitrary"), name=name)(ddt, dA, dtp, dt_bias, a_log)


def _ssd_gate_fwd(y, act, z, dskip, w, name):
    S, D = y.shape
    tm = _rows(S, 256)
    Gw = D // SSD_GROUPS

    def body(y_ref, x_ref, z_ref, k_ref, w_ref, o_ref):
        zv = z_ref[...]
        y2 = (y_ref[...] + x_ref[...] * k_ref[...]) * (zv * _sigmoid(zv))
        for g in range(SSD_GROUPS):
            sl = slice(g * Gw, (g + 1) * Gw)
            v = y2[:, sl]
            r = lax.rsqrt(jnp.mean(v * v, axis=-1, keepdims=True) + EPS)
            o_ref[:, sl] = ((v * r) * w_ref[:, sl]).astype(o_ref.dtype)

    row = pl.BlockSpec((tm, D), lambda i: (i, 0))
    vec = pl.BlockSpec((1, D), lambda i: (0, 0))
    return pl.pallas_call(body, grid=(S // tm,), in_specs=[row, row, row, vec, vec], out_specs=row,
                          out_shape=jax.ShapeDtypeStruct((S, D), _MXU), compiler_params=_cparams("parallel"),
                          name=name)(y, act, z, dskip, w)


def _ssd_gate_bwd(dyn, y, act, z, dskip, w, name):
    S, D = y.shape
    tm = _rows(S, 256)
    Gw = D // SSD_GROUPS

    def body(g_ref, y_ref, x_ref, z_ref, k_ref, w_ref, dy_ref, dz_ref, dk_ref, dw_ref):
        @pl.when(pl.program_id(0) == 0)
        def _():
            dk_ref[...] = jnp.zeros_like(dk_ref)
            dw_ref[...] = jnp.zeros_like(dw_ref)

        zv = z_ref[...]
        xv = x_ref[...]
        s = _sigmoid(zv)
        sz = zv * s
        y1 = y_ref[...] + xv * k_ref[...]
        y2 = y1 * sz
        for g in range(SSD_GROUPS):
            sl = slice(g * Gw, (g + 1) * Gw)
            v = y2[:, sl]
            r = lax.rsqrt(jnp.mean(v * v, axis=-1, keepdims=True) + EPS)
            vn = v * r
            gy = g_ref[:, sl].astype(F32)
            dw_ref[:, sl] += jnp.sum(gy * vn, axis=0, keepdims=True)
            dvn = gy * w_ref[:, sl]
            dy2 = r * (dvn - vn * jnp.mean(dvn * vn, axis=-1, keepdims=True))
            dy1 = dy2 * sz[:, sl]
            dy_ref[:, sl] = dy1
            dz_ref[:, sl] = (dy2 * y1[:, sl] * (s[:, sl] * (1.0 + zv[:, sl] * (1.0 - s[:, sl])))).astype(dz_ref.dtype)
            dk_ref[:, sl] += jnp.sum(dy1 * xv[:, sl], axis=0, keepdims=True)

    row = pl.BlockSpec((tm, D), lambda i: (i, 0))
    vec = pl.BlockSpec((1, D), lambda i: (0, 0))
    return pl.pallas_call(
        body, grid=(S // tm,), in_specs=[row, row, row, row, vec, vec], out_specs=[row, row, vec, vec],
        out_shape=[jax.ShapeDtypeStruct((S, D), F32), jax.ShapeDtypeStruct((S, D), _MXU),
                   jax.ShapeDtypeStruct((1, D), F32), jax.ShapeDtypeStruct((1, D), F32)],
        compiler_params=_cparams("arbitrary"), name=name)(dyn, y, act, z, dskip, w)


def _split_dot(v, u):
    hi = v.astype(_MXU)
    lo = (v - hi.astype(F32)).astype(_MXU)
    dn = (((1,), (0,)), ((), ()))
    return (lax.dot_general(hi, u, dn, preferred_element_type=F32) + lax.dot_general(lo, u, dn, preferred_element_type=F32))


def _sb_tiles(S):
    return _pick(S, 256, 128)


SB_LANES = 128
SB_PACK = SB_LANES // SB_HEAD_DIM
SB_ROWS = 128
SB_SCALE = 1.0 / math.sqrt(SB_HEAD_DIM)


def _head_masks():
    lane = lax.broadcasted_iota(jnp.int32, (1, SB_LANES), 1)
    return [(lane // SB_HEAD_DIM) == hh for hh in range(SB_PACK)]


def _by_head(hm, vals):
    out = vals[-1]
    for hh in range(SB_PACK - 2, -1, -1):
        out = jnp.where(hm[hh], vals[hh], out)
    return out


SB_DEAD = -110.0


def _sb_alive(Rs):
    m = Rs[0]
    for R in Rs[1:]:
        m = jnp.maximum(m, R)
    return jnp.max(m) > SB_DEAD


def _sb_rows(a, r):
    return a[r * SB_ROWS:(r + 1) * SB_ROWS]


def _sb_assemble(hm, vals):
    nr = len(vals) // SB_PACK
    return jnp.concatenate([_by_head(hm, vals[r * SB_PACK:(r + 1) * SB_PACK]) for r in range(nr)], axis=0)


def _sb_scores(zs, U, Rs, masks):
    ls = [-jnp.maximum(z, 0.0) - jnp.log(1.0 + jnp.exp(-jnp.abs(z))) for z in zs]
    if masks is not None:
        ls = [jnp.where(m, l, 0.0) for m, l in zip(masks, ls)]
    Es = [lax.dot_general(l.astype(_MXU), U, (((1,), (0,)), ((), ())), preferred_element_type=F32) for l in ls]
    As = [jnp.exp(l + z + (E + R)) for l, z, E, R in zip(ls, zs, Es, Rs)]
    if masks is not None:
        As = [jnp.where(m, A, 0.0) for m, A in zip(masks, As)]
    return ls, [A.astype(_MXU) for A in As]


SB_GROUP = 2


def _sbg_chains(T):
    return [(b, r, hh) for b in range(SB_GROUP) for r in range(T // SB_ROWS) for hh in range(SB_PACK)]


def _lanes(a, b):
    return a[:, b * SB_LANES:(b + 1) * SB_LANES]


def _sbg_join(hm, vals):
    per = len(vals) // SB_GROUP
    return jnp.concatenate([_sb_assemble(hm, vals[b * per:(b + 1) * per]) for b in range(SB_GROUP)], axis=1)


def _sbg_head_sum(hm, a):
    return jnp.concatenate([_by_head(hm, [jnp.sum(jnp.where(m, _lanes(a, b), 0.0), axis=1, keepdims=True) for m in hm])
                            for b in range(SB_GROUP)], axis=1)


def _sbg_fwd(q_arr, k_arr, v_arr, cols, w, name, gather=()):
    S = q_arr.shape[0]
    T = _sb_tiles(S)
    cq, ck, cv = cols
    GW = SB_GROUP * SB_LANES
    nb = SB_WIDTH // GW
    ng = len(gather)
    send, finish = _gather_phases([g.shape[0] // 2 for g in gather])

    def body(q_ref, k_ref, v_ref, w_ref, *rest):
        o_ref, y_ref = rest[ng:ng + 2]
        comm = (rest[:ng], rest[ng + 2:2 * ng + 2], rest[2 * ng + 2:])
        i = pl.program_id(1)
        if ng:
            @pl.when((pl.program_id(0) == 0) & (i == 0))
            def _():
                send(*comm)
        hm = _head_masks()
        qs = q_ref[...] * SB_SCALE
        chains = _sbg_chains(T)
        qcs = [_sb_rows(jnp.where(hm[hh], _lanes(qs, b), jnp.zeros((T, SB_LANES), qs.dtype)), r) for b, r, hh in chains]
        U = _tri(T, True, False).astype(_MXU)

        def scores_of(j):
            kj = k_ref[pl.ds(pl.multiple_of(j * T, T), T), :]
            return [_dot(qc, _lanes(kj, b), 1, 1) for qc, (b, _, _) in zip(qcs, chains)]

        def weighted(Abs, j):
            vj = v_ref[pl.ds(pl.multiple_of(j * T, T), T), :]
            return _sbg_join(hm, [_dot(Ab, _lanes(vj, b), 1, 0) for Ab, (b, _, _) in zip(Abs, chains)])

        def step(carry):
            jj, acc, Rs, Aprev = carry
            j = i - 1 - jj
            zs = scores_of(j)
            acc = acc + weighted(Aprev, j + 1)
            ls, Abs = _sb_scores(zs, U, Rs, None)
            return jj + 1, acc, tuple(R + jnp.sum(l, axis=1, keepdims=True) for R, l in zip(Rs, ls)), tuple(Abs)

        masks = [_sb_rows(_tri(T, True, False), r) for _, r, _ in chains]
        zero = jnp.zeros((SB_ROWS, 1), F32)
        ls, Abs = _sb_scores(scores_of(i), U, (zero,) * len(chains), masks)
        carry = (jnp.int32(0), jnp.zeros((T, GW), F32), tuple(jnp.sum(l, axis=1, keepdims=True) for l in ls), tuple(Abs))
        jj, acc, _, Alast = lax.while_loop(lambda c: (c[0] < i) & _sb_alive(c[2]), step, carry)
        acc = acc + weighted(Alast, i - jj)
        o_ref[...] = acc
        r = lax.rsqrt(_sbg_head_sum(hm, acc * acc) * (1.0 / SB_HEAD_DIM) + EPS)
        y_ref[...] = ((acc * r) * w_ref[...]).astype(y_ref.dtype)
        if ng:
            @pl.when((pl.program_id(0) == nb - 1) & (i == S // T - 1))
            def _():
                finish(*comm)

    blk = pl.BlockSpec((T, GW), lambda h, i: (i, h))
    hbm = pl.BlockSpec(memory_space=pl.ANY)
    return pl.pallas_call(
        body, grid=(nb, S // T),
        in_specs=[pl.BlockSpec((T, GW), lambda h, i: (i, cq + h)), pl.BlockSpec((S, GW), lambda h, i: (0, ck + h), pipeline_mode=pl.Buffered(1)),
                  pl.BlockSpec((S, GW), lambda h, i: (0, cv + h), pipeline_mode=pl.Buffered(1)), pl.BlockSpec((1, GW), lambda h, i: (0, h))]
                 + [hbm] * ng,
        out_specs=[blk, blk] + [hbm] * ng,
        out_shape=[jax.ShapeDtypeStruct((S, SB_WIDTH), F32), jax.ShapeDtypeStruct((S, SB_WIDTH), _MXU)]
                  + [jax.ShapeDtypeStruct((4,) + tuple(g.shape), g.dtype) for g in gather],
        scratch_shapes=_gather_sems(ng) if ng else [],
        compiler_params=_cparams("arbitrary", "arbitrary") if ng else _cparams("parallel", "parallel"), name=name)(q_arr, k_arr, v_arr, w, *gather)


def _sbg_bwd(q_arr, k_arr, v_arr, cols, o, dy_arr, cdy, w, name, swap=()):
    S = q_arr.shape[0]
    T = _sb_tiles(S)
    cq, ck, cv = cols
    GW = SB_GROUP * SB_LANES
    nb = SB_WIDTH // GW
    ns = len(swap)
    send, finish = _swap_phases(ns)

    def body(q_ref, k_ref, v_ref, o_ref, dy_ref, w_ref, *rest):
        dq_ref, dk_ref, dv_ref, dw_ref = rest[ns:ns + 4]
        comm = (rest[:ns], rest[ns + 4:2 * ns + 4], rest[2 * ns + 4:])
        i = pl.program_id(1)
        if ns:
            @pl.when((pl.program_id(0) == 0) & (i == 0))
            def _():
                send(*comm)

        @pl.when(i == 0)
        def _():
            dk_ref[...] = jnp.zeros_like(dk_ref)
            dv_ref[...] = jnp.zeros_like(dv_ref)
            dw_ref[...] = jnp.zeros_like(dw_ref)

        hm = _head_masks()
        chains = _sbg_chains(T)
        qs = q_ref[...] * SB_SCALE
        ov = o_ref[...]
        gy = dy_ref[...]
        r = lax.rsqrt(_sbg_head_sum(hm, ov * ov) * (1.0 / SB_HEAD_DIM) + EPS)
        on = ov * r
        dw_ref[...] += jnp.sum(gy * on, axis=0, keepdims=True)
        don = gy * w_ref[...]
        do = r * (don - on * (_sbg_head_sum(hm, don * on) * (1.0 / SB_HEAD_DIM)))
        dob = do.astype(_MXU)
        dprod = dob.astype(F32) * ov
        zt = jnp.zeros((T, SB_LANES), dob.dtype)
        qm = [[jnp.where(hm[hh], _lanes(qs, b), zt) for hh in range(SB_PACK)] for b in range(SB_GROUP)]
        dm = [[jnp.where(hm[hh], _lanes(dob, b), zt) for hh in range(SB_PACK)] for b in range(SB_GROUP)]
        qcs = [_sb_rows(qm[b][hh], r_) for b, r_, hh in chains]
        doc = [_sb_rows(dm[b][hh], r_) for b, r_, hh in chains]
        Dt = [_sb_rows(jnp.sum(jnp.where(hm[hh], _lanes(dprod, b), 0.0), axis=1, keepdims=True), r_) for b, r_, hh in chains]
        U = _tri(T, True, False).astype(_MXU)
        Ui = _tri(T, False, False).astype(_MXU)

        def products_of(j):
            off = pl.multiple_of(j * T, T)
            kj = k_ref[pl.ds(off, T), :]
            vj = v_ref[pl.ds(off, T), :]
            return ([_dot(qc, _lanes(kj, b), 1, 1) for qc, (b, _, _) in zip(qcs, chains)],
                    [_dot(d, _lanes(vj, b), 1, 1) for d, (b, _, _) in zip(doc, chains)])

        def core(zs, dAs, Rs, Qs, masks):
            ls, Abs = _sb_scores(zs, U, Rs, masks)
            Gs = [dA * Ab.astype(F32) for dA, Ab in zip(dAs, Abs)]
            sfx = [_split_dot(G, Ui) for G in Gs]
            dzs = []
            for c, (l, G, s, D, Q) in enumerate(zip(ls, Gs, sfx, Dt, Qs)):
                P = D - (s + Q)
                dz = jnp.exp(l) * (G + P) - P
                if masks is not None:
                    dz = jnp.where(masks[c], dz, 0.0)
                dzs.append(dz.astype(_MXU))
            newR = tuple(R + jnp.sum(l, axis=1, keepdims=True) for R, l in zip(Rs, ls))
            newQ = tuple(Q + jnp.sum(G, axis=1, keepdims=True) for Q, G in zip(Qs, Gs))
            return tuple(Abs), tuple(dzs), newR, newQ

        def over_rows(vals, other):
            nr = T // SB_ROWS
            tiles = []
            for b in range(SB_GROUP):
                acc = None
                for hh in range(SB_PACK):
                    rows = jnp.concatenate([vals[(b * nr + r_) * SB_PACK + hh] for r_ in range(nr)], axis=0)
                    part = _dot(rows, other[b][hh], 0, 0)
                    acc = part if acc is None else acc + part
                tiles.append(acc)
            return jnp.concatenate(tiles, axis=1)

        def emit(Abs, dzs, j):
            off = pl.multiple_of(j * T, T)
            kj = k_ref[pl.ds(off, T), :]
            dk_ref[pl.ds(off, T), :] += over_rows(dzs, qm)
            dv_ref[pl.ds(off, T), :] += over_rows(Abs, dm)
            return _sbg_join(hm, [_dot(dzb, _lanes(kj, b), 1, 0) for dzb, (b, _, _) in zip(dzs, chains)])

        def step(carry):
            jj, dq, Rs, Qs, Aprev, dzprev = carry
            j = i - 1 - jj
            zs, dAs = products_of(j)
            dq = dq + emit(Aprev, dzprev, j + 1)
            Abs, dzs, Rs, Qs = core(zs, dAs, Rs, Qs, None)
            return jj + 1, dq, Rs, Qs, Abs, dzs

        masks = [_sb_rows(_tri(T, True, False), r_) for _, r_, _ in chains]
        zero = (jnp.zeros((SB_ROWS, 1), F32),) * len(chains)
        zs, dAs = products_of(i)
        Abs, dzs, Rs, Qs = core(zs, dAs, zero, zero, masks)
        jj, dq, _, _, Alast, dzlast = lax.while_loop(lambda c: (c[0] < i) & _sb_alive(c[2]), step,
                                                     (jnp.int32(0), jnp.zeros((T, GW), F32), Rs, Qs, Abs, dzs))
        dq = dq + emit(Alast, dzlast, i - jj)
        dq_ref[...] = (dq * SB_SCALE).astype(dq_ref.dtype)
        if ns:
            @pl.when((pl.program_id(0) == nb - 1) & (i == S // T - 1))
            def _():
                finish(*comm)

    blk = pl.BlockSpec((T, GW), lambda h, i: (i, h))
    full = pl.BlockSpec((S, GW), lambda h, i: (0, h), pipeline_mode=pl.Buffered(1))
    wsp = pl.BlockSpec((1, GW), lambda h, i: (0, h))
    hbm = pl.BlockSpec(memory_space=pl.ANY)
    return pl.pallas_call(
        body, grid=(nb, S // T),
        in_specs=[pl.BlockSpec((T, GW), lambda h, i: (i, cq + h)), pl.BlockSpec((S, GW), lambda h, i: (0, ck + h), pipeline_mode=pl.Buffered(1)),
                  pl.BlockSpec((S, GW), lambda h, i: (0, cv + h), pipeline_mode=pl.Buffered(1)), blk,
                  pl.BlockSpec((T, GW), lambda h, i: (i, cdy + h)), wsp] + [hbm] * ns,
        out_specs=[blk, full, full, wsp] + [hbm] * ns,
        out_shape=[jax.ShapeDtypeStruct((S, SB_WIDTH), _MXU), jax.ShapeDtypeStruct((S, SB_WIDTH), F32),
                   jax.ShapeDtypeStruct((S, SB_WIDTH), F32), jax.ShapeDtypeStruct((1, SB_WIDTH), F32)]
                  + [jax.ShapeDtypeStruct((3,) + tuple(a.shape[1:]), a.dtype) for a in swap],
        scratch_shapes=[pltpu.SemaphoreType.DMA((3 * ns,))] * 2 if ns else [],
        compiler_params=_cparams("arbitrary", "arbitrary") if ns else _cparams("parallel", "arbitrary"),
        name=name)(q_arr, k_arr, v_arr, o, dy_arr, w, *swap)


def _adamw(w, g, m, v, name):
    R, C = w.shape
    tm = _rows(R, 256) if R % 8 == 0 else R
    c1 = 1.0 - ADAM_B1 ** ADAM_STEP
    c2 = 1.0 - ADAM_B2 ** ADAM_STEP

    def body(w_ref, g_ref, m_ref, v_ref, d_ref, nm_ref, nv_ref):
        gv = g_ref[...]
        mn = ADAM_B1 * m_ref[...] + (1.0 - ADAM_B1) * gv
        vn = ADAM_B2 * v_ref[...] + (1.0 - ADAM_B2) * (gv * gv)
        d_ref[...] = -ADAM_LR * ((mn / c1) / (jnp.sqrt(vn / c2) + ADAM_EPS) + ADAM_WD * w_ref[...])
        nm_ref[...] = mn
        nv_ref[...] = vn

    blk = pl.BlockSpec((tm, C), lambda i: (i, 0))
    return pl.pallas_call(body, grid=(R // tm,), in_specs=[blk] * 4, out_specs=[blk] * 3,
                          out_shape=[jax.ShapeDtypeStruct((R, C), F32)] * 3, compiler_params=_cparams("parallel"),
                          name=name)(w, g, m, v)


def _sum_lead(a, name, first=None, pick=None, wire=False):
    n, R, C = a.shape
    tm = _rows(R, 256)
    nin = 1 if first is None else 2

    def body(*refs):
        refs = refs[nin - 1:]
        a_ref = refs[nin - 1]
        s = a_ref[0].astype(F32) if first is None else refs[0][...] + a_ref[0]
        for p in range(1, n):
            s = s + a_ref[p]
        for o_ref in refs[nin:]:
            o_ref[...] = s.astype(o_ref.dtype)

    outs = [jax.ShapeDtypeStruct((R, C), F32)] + ([jax.ShapeDtypeStruct((R, C), _WIRE)] if wire else [])
    if first is None:
        row = pl.BlockSpec((tm, C), lambda i: (i, 0))
        res = pl.pallas_call(body, grid=(R // tm,), in_specs=[pl.BlockSpec((n, tm, C), lambda i: (0, i, 0))],
                             out_specs=[row] * len(outs), out_shape=outs, compiler_params=_cparams("parallel"), name=name)(a)
    else:
        row = pl.BlockSpec((tm, C), lambda i, p: (i, 0))
        grid_spec = pltpu.PrefetchScalarGridSpec(
            num_scalar_prefetch=1, grid=(R // tm,),
            in_specs=[pl.BlockSpec((None, tm, C), lambda i, p: (p[0], i, 0)), pl.BlockSpec((n, tm, C), lambda i, p: (0, i, 0))],
            out_specs=[row] * len(outs))
        res = pl.pallas_call(body, grid_spec=grid_spec, out_shape=outs, compiler_params=_cparams("parallel"), name=name)(pick, first, a)
    return res if wire else res[0]


_GROUP_BITS = {'c': ((0, 0, 1),), 'xy': ((0, 1, 0), (1, 0, 0), (1, 1, 0)),
               'xyc': tuple((k >> 2 & 1, k >> 1 & 1, k & 1) for k in range(1, 8))}


def _exchange(srcs, *, group, same_src, own, chunks, name):
    flips = _GROUP_BITS[group]
    n = len(flips) + 1
    na = len(srcs)
    blk_shapes = [tuple(s.shape) if same_src else tuple(s.shape[1:]) for s in srcs]
    assert all(bs[0] % chunks == 0 for bs in blk_shapes), blk_shapes

    def body(*refs):
        src_refs, dst_refs = refs[:na], refs[na:2 * na]
        send_sems, recv_sems, loc_sems = refs[2 * na:]
        x, y, c = lax.axis_index("x"), lax.axis_index("y"), lax.axis_index("c")

        def member(px, py, pc):
            return {'c': pc, 'xy': 2 * px + py, 'xyc': 4 * px + 2 * py + pc}[group]

        def piece(ref, a, q):
            rows = blk_shapes[a][0] // chunks
            return ref.at[pl.ds(q * rows, rows)]

        me = member(x, y, c)
        started, arrivals = [], []
        for a in range(na):
            mine = src_refs[a] if same_src else src_refs[a].at[me]
            if own:
                for q in range(chunks):
                    cp = pltpu.make_async_copy(piece(mine, a, q), piece(dst_refs[a].at[me], a, q), loc_sems.at[a * chunks + q])
                    cp.start()
                    started.append(cp.wait)
            for kk, (fx, fy, fc) in enumerate(flips):
                px, py, pc = (1 - x if fx else x), (1 - y if fy else y), (1 - c if fc else c)
                peer = member(px, py, pc)
                out_blk = src_refs[a] if same_src else src_refs[a].at[peer]
                there = dst_refs[a].at[me if own else kk]
                here = dst_refs[a].at[peer if own else kk]
                for q in range(chunks):
                    s = (a * (n - 1) + kk) * chunks + q
                    out = pltpu.make_async_remote_copy(
                        src_ref=piece(out_blk, a, q), dst_ref=piece(there, a, q), send_sem=send_sems.at[s],
                        recv_sem=recv_sems.at[s], device_id=(px, py, pc), device_id_type=pl.DeviceIdType.MESH)
                    out.start()
                    started.append(out.wait_send)
                    arrivals.append(pltpu.make_async_remote_copy(
                        src_ref=piece(mine, a, q), dst_ref=piece(here, a, q), send_sem=send_sems.at[s],
                        recv_sem=recv_sems.at[s], device_id=(px, py, pc), device_id_type=pl.DeviceIdType.MESH).wait_recv)
        for wait in arrivals + started:
            wait()

    nsem = na * (n - 1) * chunks
    hbm = pl.BlockSpec(memory_space=pl.ANY)
    return pl.pallas_call(
        body, in_specs=[hbm] * na, out_specs=[hbm] * na,
        out_shape=[jax.ShapeDtypeStruct(((n if own else n - 1),) + bs, s.dtype) for bs, s in zip(blk_shapes, srcs)],
        scratch_shapes=[pltpu.SemaphoreType.DMA((nsem,)), pltpu.SemaphoreType.DMA((nsem,)),
                        pltpu.SemaphoreType.DMA((na * chunks,))],
        compiler_params=pltpu.CompilerParams(has_side_effects=True), name=name)(*srcs)


def _gather_phases(halves):
    flips = _GROUP_BITS['xy']
    nf = len(flips)
    na = len(halves)

    def copies(src_refs, dst_refs, sems):
        send_sems, recv_sems, fsend_sems, frecv_sems = sems
        x, y, c = lax.axis_index("x"), lax.axis_index("y"), lax.axis_index("c")

        def half(ref, a, which):
            return ref.at[pl.ds(pl.multiple_of(which * halves[a], 8), halves[a])]

        def copy(src, dst, pair, s, to):
            return pltpu.make_async_remote_copy(src_ref=src, dst_ref=dst, send_sem=pair[0].at[s], recv_sem=pair[1].at[s],
                                                device_id=to, device_id_type=pl.DeviceIdType.MESH)

        out = []
        for a in range(na):
            for kk, (fx, fy, _) in enumerate(flips):
                peer = ((1 - x if fx else x), (1 - y if fy else y), c)
                there = dst_refs[a].at[2 * peer[0] + peer[1]]
                s = a * nf + kk
                ici, d2d = (send_sems, recv_sems), (fsend_sems, frecv_sems)
                out.append((copy(half(src_refs[a], a, c), half(dst_refs[a].at[2 * x + y], a, c), ici, s, peer),
                            copy(half(src_refs[a], a, c), half(there, a, c), ici, s, (x, y, c)),
                            copy(half(there, a, c), half(there, a, c), d2d, s, (x, y, 1 - c)),
                            copy(half(there, a, 1 - c), half(there, a, 1 - c), d2d, s, (x, y, 1 - c))))
        return out

    def send(src_refs, dst_refs, sems):
        for first, _, _, _ in copies(src_refs, dst_refs, sems):
            first.start()

    def finish(src_refs, dst_refs, sems):
        cs = copies(src_refs, dst_refs, sems)
        for _, landed, onward, _ in cs:
            landed.wait_recv()
            onward.start()
        for _, _, _, passed in cs:
            passed.wait_recv()
        for first, _, onward, _ in cs:
            first.wait_send()
            onward.wait_send()

    return send, finish


def _swap_phases(na):
    flips = _GROUP_BITS['xy']

    def copies(src_refs, dst_refs, sems):
        x, y, c = lax.axis_index("x"), lax.axis_index("y"), lax.axis_index("c")
        out = []
        for a in range(na):
            for kk, (fx, fy, _) in enumerate(flips):
                px, py = (1 - x if fx else x), (1 - y if fy else y)
                s = a * len(flips) + kk
                out.append(pltpu.make_async_remote_copy(
                    src_ref=src_refs[a].at[2 * px + py], dst_ref=dst_refs[a].at[kk], send_sem=sems[0].at[s], recv_sem=sems[1].at[s],
                    device_id=(px, py, c), device_id_type=pl.DeviceIdType.MESH))
        return out

    def send(src_refs, dst_refs, sems):
        for cp in copies(src_refs, dst_refs, sems):
            cp.start()

    def finish(src_refs, dst_refs, sems):
        cs = copies(src_refs, dst_refs, sems)
        for cp in cs:
            cp.wait_recv()
        for cp in cs:
            cp.wait_send()

    return send, finish


def _gather_sems(na):
    return [pltpu.SemaphoreType.DMA((na * len(_GROUP_BITS['xy']),))] * 4


def _gather_chips(srcs, name):
    na = len(srcs)
    send, finish = _gather_phases([s.shape[0] // 2 for s in srcs])

    def body(*refs):
        send(refs[:na], refs[na:2 * na], refs[2 * na:])
        finish(refs[:na], refs[na:2 * na], refs[2 * na:])

    hbm = pl.BlockSpec(memory_space=pl.ANY)
    return pl.pallas_call(
        body, in_specs=[hbm] * na, out_specs=[hbm] * na,
        out_shape=[jax.ShapeDtypeStruct((4,) + tuple(s.shape), s.dtype) for s in srcs], scratch_shapes=_gather_sems(na),
        compiler_params=pltpu.CompilerParams(has_side_effects=True), name=name)(*srcs)


def _to_shards(name, full):
    R, C = full.shape
    if name in COL_SPLIT:
        return full.reshape(R, 4, C // 4).transpose(1, 0, 2)
    return full.reshape(4, R // 4, C)


def _from_shards(name, sh):
    n, R, C = sh.shape
    if name in COL_SPLIT:
        return sh.transpose(1, 0, 2).reshape(R, n * C)
    return sh.reshape(n * R, C)


def _pack_rows(parts, width, rows):
    n = parts[0].shape[0]
    flat = jnp.concatenate([p.reshape(n, -1) for p in parts], axis=1)
    return jnp.pad(flat, ((0, 0), (0, rows * width - flat.shape[1]))).reshape(n, rows, width)


def _unpack_rows(buf, shapes):
    n = buf.shape[0]
    flat = buf.reshape(n, -1)
    out, o = [], 0
    for s in shapes:
        sz = math.prod(s)
        out.append(flat[:, o:o + sz].reshape((n,) + tuple(s)))
        o += sz
    return out


def _split_rows(a, rows):
    out, o = [], 0
    for r in rows:
        out.append(a[:, o:o + r])
        o += r
    return out


def _ceil_to(v, m):
    return -(-v // m) * m


def kernel(x, mem, norm_mix_w, w_in, conv_ssd_w, conv_ssd_b, dt_bias, a_log, d_skip, ssd_norm_w, sb_norm_w, w_out, norm_mem_w, norm_memkv_w, w_mq, w_mk, w_mv, w_mo, norm_ffn_w, w_up, conv_ffn_w, conv_ffn_b, w_down, norm_final_w, loss_target, m_norm_mix_w, m_w_in, m_conv_ssd_w, m_conv_ssd_b, m_dt_bias, m_a_log, m_d_skip, m_ssd_norm_w, m_sb_norm_w, m_w_out, m_norm_mem_w, m_norm_memkv_w, m_w_mq, m_w_mk, m_w_mv, m_w_mo, m_norm_ffn_w, m_w_up, m_conv_ffn_w, m_conv_ffn_b, m_w_down, m_norm_final_w, v_norm_mix_w, v_w_in, v_conv_ssd_w, v_conv_ssd_b, v_dt_bias, v_a_log, v_d_skip, v_ssd_norm_w, v_sb_norm_w, v_w_out, v_norm_mem_w, v_norm_memkv_w, v_w_mq, v_w_mk, v_w_mv, v_w_mo, v_norm_ffn_w, v_w_up, v_conv_ffn_w, v_conv_ffn_b, v_w_down, v_norm_final_w):
    W = dict(norm_mix_w=norm_mix_w, w_in=w_in, conv_ssd_w=conv_ssd_w, conv_ssd_b=conv_ssd_b, dt_bias=dt_bias, a_log=a_log,
             d_skip=d_skip, ssd_norm_w=ssd_norm_w, sb_norm_w=sb_norm_w, w_out=w_out, norm_mem_w=norm_mem_w,
             norm_memkv_w=norm_memkv_w, w_mq=w_mq, w_mk=w_mk, w_mv=w_mv, w_mo=w_mo, norm_ffn_w=norm_ffn_w, w_up=w_up,
             conv_ffn_w=conv_ffn_w, conv_ffn_b=conv_ffn_b, w_down=w_down, norm_final_w=norm_final_w)
    Mo = dict(norm_mix_w=m_norm_mix_w, w_in=m_w_in, conv_ssd_w=m_conv_ssd_w, conv_ssd_b=m_conv_ssd_b, dt_bias=m_dt_bias,
              a_log=m_a_log, d_skip=m_d_skip, ssd_norm_w=m_ssd_norm_w, sb_norm_w=m_sb_norm_w, w_out=m_w_out,
              norm_mem_w=m_norm_mem_w, norm_memkv_w=m_norm_memkv_w, w_mq=m_w_mq, w_mk=m_w_mk, w_mv=m_w_mv, w_mo=m_w_mo,
              norm_ffn_w=m_norm_ffn_w, w_up=m_w_up, conv_ffn_w=m_conv_ffn_w, conv_ffn_b=m_conv_ffn_b, w_down=m_w_down,
              norm_final_w=m_norm_final_w)
    Vo = dict(norm_mix_w=v_norm_mix_w, w_in=v_w_in, conv_ssd_w=v_conv_ssd_w, conv_ssd_b=v_conv_ssd_b, dt_bias=v_dt_bias,
              a_log=v_a_log, d_skip=v_d_skip, ssd_norm_w=v_ssd_norm_w, sb_norm_w=v_sb_norm_w, w_out=v_w_out,
              norm_mem_w=v_norm_mem_w, norm_memkv_w=v_norm_memkv_w, w_mq=v_w_mq, w_mk=v_w_mk, w_mv=v_w_mv, w_mo=v_w_mo,
              norm_ffn_w=v_norm_ffn_w, w_up=v_w_up, conv_ffn_w=v_conv_ffn_w, conv_ffn_b=v_conv_ffn_b, w_down=v_w_down,
              norm_final_w=v_norm_final_w)
    shapes = {n: W[n].shape for n in WEIGHTS}
    sh2 = {n: (1, a.shape[-1]) if a.ndim < 3 else a.shape[-2:] for n, a in W.items()}
    w2 = {n: W[n].reshape(sh2[n]) for n in WEIGHTS}
    x2d = x[0]
    S, D = x2d.shape
    H, P, N = SSD_HEADS, SSD_HEAD_DIM, SSD_STATE

    cv_rows = _ceil_to(-(-sum(math.prod(sh2[n]) for n in CONVW) // 128), 32)
    cpack = _pack_rows([w2[n][None] for n in CONVW], 128, cv_rows)[0]
    stacked = jnp.concatenate([w2[n].astype(_MXU) for n in ROW_SPLIT], axis=0)
    cidx = lax.axis_index("c")
    oidx = 2 * lax.axis_index("x") + lax.axis_index("y")
    now, later = [w2['w_in'].astype(_MXU), cpack], [stacked, w2['w_up'].astype(_MXU)]
    g_in, call = [lax.dynamic_update_index_in_dim(g, m, oidx, 0) for m, g in zip(now, _gather_chips(now, "gather_weights"))]
    full = {'w_in': _from_shards('w_in', g_in)}
    full.update({n: _from_shards(n, a) for n, a in zip(CONVW, _unpack_rows(call, [sh2[n] for n in CONVW]))})

    o1 = SSD_INNER
    o2 = o1 + SSD_XBC
    o3 = o2 + SSD_HEADS
    Wi = full['w_in']
    W_z, W_xbc, W_qkv = Wi[:, :o1], Wi[:, o1:o2], Wi[:, o3:]
    W_dt = jnp.pad(Wi[:, o2:o3], ((0, 0), (0, DT_PAD - SSD_HEADS)))
    W_in_r = jnp.concatenate([W_z, W_xbc, W_qkv, W_dt], axis=1)
    dskip_rep = jnp.repeat(w2['d_skip'], P, axis=1)

    h1 = _rms_fwd(x2d, w2['norm_mix_w'], "norm_mix")
    z = _mm(h1, W_z, name="proj_z")
    xbc = _mm(h1, W_xbc, name="proj_xbc")
    dtp = _mm(h1, W_dt, name="proj_dt")
    qkv = _mm(h1, W_qkv, out_dtype=_MXU, name="proj_qkv")
    pre = _dwconv_fwd(xbc, full['conv_ssd_w'], w2['conv_ssd_b'], "ssd_conv")
    act = _silu_fwd(pre, "ssd_conv_silu")
    dt, cs = _ssd_prep(dtp, w2['dt_bias'], w2['a_log'], "ssd_prep")
    csT = cs.T
    def heads(a, nh):
        return a.reshape(S, nh, a.shape[1] // nh).transpose(1, 0, 2)

    def unheads(a):
        return a.transpose(1, 0, 2).reshape(S, a.shape[0] * a.shape[2])

    xs_h = heads(act[:, :o1], H)
    Bm = heads(act[:, o1:o1 + SSD_GROUPS * N], SSD_GROUPS)
    Cm = heads(act[:, o1 + SSD_GROUPS * N:], SSD_GROUPS)
    y_h, prev = _ssdg_fwd(xs_h, Bm, Cm, dt, cs, csT, "ssd_scan")
    y_scan = unheads(y_h)
    y_ssd = _ssd_gate_fwd(y_scan, act, z, dskip_rep, w2['ssd_norm_w'], "ssd_gate")
    nsb = SB_WIDTH // (SB_GROUP * SB_LANES)
    qkv_cols = (0, nsb, 2 * nsb)
    o_sb, y_sb, *others = _sbg_fwd(qkv, qkv, qkv, qkv_cols, w2['sb_norm_w'], "sb_attn", gather=later)
    g_rows, g_up = [lax.dynamic_update_index_in_dim(g, m, oidx, 0) for m, g in zip(later, others)]
    full['w_up'] = _from_shards('w_up', g_up)
    full.update({n: _from_shards(n, a) for n, a in zip(ROW_SPLIT, _split_rows(g_rows, [sh2[n][0] for n in ROW_SPLIT]))})
    ycat = jnp.concatenate([y_ssd, y_sb], axis=1)
    x_2 = _mm(ycat, full['w_out'], res=x2d, name="out_proj")
    h2 = _rms_fwd(x_2, w2['norm_mem_w'], "norm_mem")
    qm = _mm(h2, full['w_mq'], out_dtype=_MXU, name="mem_q")
    mn = _rms_fwd(mem[0], w2['norm_memkv_w'], "norm_memkv")
    km = _mm(mn, full['w_mk'], out_dtype=_MXU, name="mem_k")
    vm = _mm(mn, full['w_mv'], out_dtype=_MXU, name="mem_v")
    om = _xattn_fwd(qm, km, vm, "mem_attn")
    x_3 = _mm(om, full['w_mo'], res=x_2, name="mem_o")
    h3 = _rms_fwd(x_3, w2['norm_ffn_w'], "norm_ffn")
    up = _mm(h3, full['w_up'], name="ffn_up")
    a_ffn = _conv_glu_fwd(up, full['conv_ffn_w'], w2['conv_ffn_b'], "ffn_conv_glu")
    x_4 = _mm(a_ffn, full['w_down'], res=x_3, name="ffn_down")
    dx4, dx4b, g_final, loss_blk = _loss_bwd(x_4, loss_target[0], w2['norm_final_w'], "loss_head")

    G = {'norm_final_w': g_final}
    dact = _mm(dx4b, full['w_down'], tb=True, name="d_ffn_act")
    G['w_down'] = _mm(a_ffn, dx4b, ta=True, name="g_w_down")
    du, G['conv_ffn_w'], G['conv_ffn_b'] = _conv_glu_bwd(up, dact, full['conv_ffn_w'], w2['conv_ffn_b'], "d_ffn_conv_glu")
    dup = _dwconv_bwd_x(du, full['conv_ffn_w'], "d_ffn_conv")
    dh3 = _mm(dup, full['w_up'], tb=True, name="d_h3")
    G['w_up'] = _mm(h3, dup, ta=True, name="g_w_up")
    dx3, dx3b, G['norm_ffn_w'] = _rms_bwd(dh3, x_3, w2['norm_ffn_w'], dx4, "d_norm_ffn")
    dom = _mm(dx3b, full['w_mo'], tb=True, out_dtype=_MXU, name="d_mem_o")
    G['w_mo'] = _mm(om, dx3b, ta=True, name="g_w_mo")
    dqm, dkm, dvm = _xattn_bwd(qm, km, vm, dom, "d_mem_attn")
    G['w_mq'] = _mm(h2, dqm, ta=True, name="g_w_mq")
    dh2 = _mm(dqm, full['w_mq'], tb=True, name="d_h2")
    dx2, dx2b, G['norm_mem_w'] = _rms_bwd(dh2, x_2, w2['norm_mem_w'], dx3, "d_norm_mem")
    G['w_mk'] = _mm(mn, dkm, ta=True, name="g_w_mk")
    G['w_mv'] = _mm(mn, dvm, ta=True, name="g_w_mv")
    dmn = _mm(dvm, full['w_mv'], tb=True, res=_mm(dkm, full['w_mk'], tb=True, name="d_mn_k"), name="d_mn_v")
    _, _, G['norm_memkv_w'] = _rms_bwd(dmn, mem[0], w2['norm_memkv_w'], None, "d_norm_memkv")
    dycat = _mm(dx2b, full['w_out'], tb=True, name="d_ycat")
    G['w_out'] = _mm(ycat, dx2b, ta=True, name="g_w_out")
    dy1, dz, g_dskip_lane, G['ssd_norm_w'] = _ssd_gate_bwd(dycat, y_scan, act, z, dskip_rep, w2['ssd_norm_w'], "d_ssd_gate")
    dxs_h, dB, dC, ddt, dA = _ssdg_bwd(xs_h, Bm, Cm, dt, cs, csT, prev, heads(dy1, H), w2['a_log'], w2['d_skip'], "d_ssd_scan")
    G['d_skip'] = jnp.sum(g_dskip_lane.reshape(H, P), axis=1)[None, :]
    dact_xbc = jnp.concatenate([unheads(dxs_h), unheads(dB), unheads(dC)], axis=1)
    dpre = _silu_bwd(pre, dact_xbc, "d_ssd_conv_silu")
    G['conv_ssd_w'], G['conv_ssd_b'] = _conv_bwd_w(xbc, dpre, full['conv_ssd_w'].shape[0], "g_ssd_conv")
    dxbc = _dwconv_bwd_x(dpre, full['conv_ssd_w'], "d_ssd_conv")
    ddtp, G['dt_bias'], G['a_log'] = _dt_bwd(ddt, dA, dtp, w2['dt_bias'], w2['a_log'], "d_dt")
    def pair_sums(to_pair, tag):
        got = _exchange(to_pair, group='c', same_src=False, own=False, chunks=4, name="reduce_pair" + tag)
        sums, wires = [], []
        for k, (t, g) in enumerate(zip(to_pair, got)):
            _, _, r, cw = t.shape
            full_sum, wire_sum = _sum_lead(g.reshape(1, 4 * r, cw), "reduce_pair_sum%s%d" % (tag, k), first=t.reshape(2, 4 * r, cw),
                                           pick=cidx.reshape(1), wire=True)
            sums.append(full_sum.reshape(4, r, cw))
            wires.append(wire_sum.reshape(4, r, cw))
        return sums, wires

    def chip_sums(sums, got, tag):
        return [_sum_lead(g, "reduce_chips_sum%s%d" % (tag, k), first=p, pick=oidx.reshape(1)) for k, (p, g) in enumerate(zip(sums, got))]

    by_owner = [jnp.concatenate([_to_shards(n, G[n]) for n in ROW_SPLIT], axis=1), _to_shards('w_up', G['w_up'])]
    pair_a, wire_a = pair_sums([a.reshape(4, 2, a.shape[1] // 2, a.shape[2]).transpose(1, 0, 2, 3) for a in by_owner], "_a")

    dq, dk, dv, G['sb_norm_w'], *got_a = _sbg_bwd(qkv, qkv, qkv, qkv_cols, o_sb, dycat, o1 // (SB_GROUP * SB_LANES), w2['sb_norm_w'],
                                                  "d_sb_attn", swap=wire_a)
    chips_rows, chips_up = chip_sums(pair_a, got_a, "_a")
    dproj = jnp.concatenate([dz, dxbc, dq, dk.astype(_MXU), dv.astype(_MXU), ddtp], axis=1)
    dh1 = _mm(dproj, W_in_r, tb=True, name="d_h1")
    g_in_r = _mm(h1, dproj, ta=True, name="g_w_in")
    nq = 3 * SB_WIDTH

    def in_cols(lo, hi):
        spans = []
        for a, b, shift in ((0, o2, 0), (o2, o3, nq), (o3, o3 + nq, o2 - o3)):
            s, e = max(lo, a), min(hi, b)
            if s < e:
                spans.append((s + shift, e + shift))
        return spans

    hr_in, cs_in = g_in_r.shape[0] // 2, (o3 + nq) // 4
    g_in_pair = jnp.stack([jnp.stack([jnp.concatenate([g_in_r[h * hr_in:(h + 1) * hr_in, s:e] for s, e in in_cols(j * cs_in, (j + 1) * cs_in)],
                                                      axis=1) for j in range(4)]) for h in range(2)])
    grad_x, _, G['norm_mix_w'] = _rms_bwd(dh1, x2d, w2['norm_mix_w'], dx2, "d_norm_mix")

    pair_b, wire_b = pair_sums([g_in_pair], "_b")
    (chips_in,) = chip_sums(pair_b, _exchange(wire_b, group='xy', same_src=False, own=False, chunks=1, name="reduce_chips"), "_b")
    chips = [chips_rows, chips_in, chips_up]
    got = _exchange(chips, group='c', same_src=True, own=False, chunks=4, name="share_pair")
    red = [jnp.where(cidx == 0, jnp.concatenate([m, g[0]], axis=0), jnp.concatenate([g[0], m], axis=0))[None]
           for m, g in zip(chips, got)]
    gsh = dict(zip(ROW_SPLIT, [a[0] for a in _split_rows(red[0], [sh2[n][0] for n in ROW_SPLIT])]))
    gsh['w_in'], gsh['w_up'] = red[1][0], red[2][0]

    small_parts = [G[n].reshape(1, -1) for n in SMALL + CONVW] + [loss_blk[:1, :1]]
    small_shapes = [sh2[n] for n in SMALL] + [G[n].shape for n in CONVW] + [(1, 1)]
    small_rows = _ceil_to(-(-sum(math.prod(s) for s in small_shapes) // 128), 8)
    spack = _pack_rows(small_parts, 128, small_rows)[0]
    (gathered,) = _exchange([spack], group='xyc', same_src=True, own=True, chunks=1, name="gather_small")
    parts = [a[0] for a in _unpack_rows(_sum_lead(gathered, "small_sum")[None], small_shapes)]
    gsh.update(zip(SMALL, parts))
    for n, a in zip(CONVW, parts[len(SMALL):-1]):
        gsh[n] = lax.dynamic_index_in_dim(_to_shards(n, a), oidx, 0, keepdims=False)
    loss = parts[-1].reshape(())

    delta, new_m, new_v = {}, {}, {}
    for n in BIG:
        delta[n], new_m[n], new_v[n] = _adamw(w2[n], gsh[n], Mo[n].reshape(sh2[n]), Vo[n].reshape(sh2[n]), "adamw_" + n)
    for grp, width, tag in ((CONVW, 128, "adamw_conv"), (SMALL, 128, "adamw_small")):
        rows = _ceil_to(-(-sum(math.prod(sh2[n]) for n in grp) // width), 8)
        packed = [_pack_rows([src[n].reshape(1, -1) for n in grp], width, rows)[0]
                  for src in (w2, gsh, {n: Mo[n] for n in grp}, {n: Vo[n] for n in grp})]
        outs = _adamw(*packed, tag)
        for dst, o in zip((delta, new_m, new_v), outs):
            dst.update(zip(grp, [a[0] for a in _unpack_rows(o[None], [sh2[n] for n in grp])]))

    def shaped(d):
        return [d[n].reshape(shapes[n]) for n in WEIGHTS]

    return (loss, grad_x[None], *shaped(gsh), *shaped(delta), *shaped(new_m), *shaped(new_v))
```

```python
import math

import jax
import jax.numpy as jnp
from jax import lax
from jax.experimental import pallas as pl
from jax.experimental.pallas import tpu as pltpu

F32 = jnp.float32
_MXU = jnp.bfloat16
_WIRE = jnp.bfloat16
EPS = 1e-6
_VMEM_LIMIT = 48 * 1024 * 1024
_HI = lax.Precision.HIGHEST

SSD_HEADS = 16
SSD_HEAD_DIM = 64
SSD_GROUPS = 2
SSD_STATE = 128
SSD_CHUNK = 128
SSD_INNER = SSD_HEADS * SSD_HEAD_DIM
SSD_XBC = SSD_INNER + 2 * SSD_GROUPS * SSD_STATE
SB_HEADS = 16
SB_HEAD_DIM = 64
SB_WIDTH = SB_HEADS * SB_HEAD_DIM
MEM_HEADS = 4
DT_PAD = 128

ADAM_LR = 0.001
ADAM_B1 = 0.9
ADAM_B2 = 0.999
ADAM_EPS = 1e-08
ADAM_WD = 0.01
ADAM_STEP = 10

WEIGHTS = ['norm_mix_w', 'w_in', 'conv_ssd_w', 'conv_ssd_b', 'dt_bias', 'a_log', 'd_skip', 'ssd_norm_w',
           'sb_norm_w', 'w_out', 'norm_mem_w', 'norm_memkv_w', 'w_mq', 'w_mk', 'w_mv', 'w_mo', 'norm_ffn_w',
           'w_up', 'conv_ffn_w', 'conv_ffn_b', 'w_down', 'norm_final_w']
BIG = ['w_in', 'w_out', 'w_mq', 'w_mk', 'w_mv', 'w_mo', 'w_up', 'w_down']
COL_SPLIT = ('w_in', 'w_up', 'conv_ssd_w', 'conv_ffn_w')
ROW_SPLIT = ['w_out', 'w_mq', 'w_mk', 'w_mv', 'w_mo', 'w_down']
CONVW = ['conv_ssd_w', 'conv_ffn_w']
SMALL = ['norm_mix_w', 'conv_ssd_b', 'dt_bias', 'a_log', 'd_skip', 'ssd_norm_w', 'sb_norm_w', 'norm_mem_w',
         'norm_memkv_w', 'norm_ffn_w', 'conv_ffn_b', 'norm_final_w']


def _cparams(*sem):
    return pltpu.CompilerParams(dimension_semantics=sem if sem else None, vmem_limit_bytes=_VMEM_LIMIT)


def _pick(n, cap, mult=128):
    best = None
    for d in range(mult, min(n, cap) + 1, mult):
        if n % d == 0:
            best = d
    return n if best is None else best


def _dot(a, b, ca, cb):
    return lax.dot_general(a.astype(_MXU), b.astype(_MXU), (((ca,), (cb,)), ((), ())), preferred_element_type=F32)


def _sigmoid(v):
    return 1.0 / (1.0 + jnp.exp(-v))


def _log1p(u):
    w = 1.0 + u
    return jnp.where(w == 1.0, u, jnp.log(w) * (u / (w - 1.0)))


def _mm(a, b, *, ta=False, tb=False, res=None, out_dtype=F32, name):
    if ta:
        K, M = a.shape
    else:
        M, K = a.shape
    if tb:
        N, K2 = b.shape
    else:
        K2, N = b.shape
    assert K == K2, (a.shape, b.shape)
    tm = _pick(M, 1408, 128 if ta else 16)
    tn = _pick(N, 1536)
    tk = _pick(K, 1536)
    nk = K // tk
    a_spec = pl.BlockSpec((tk, tm), lambda i, j, k: (k, i)) if ta else pl.BlockSpec((tm, tk), lambda i, j, k: (i, k))
    b_spec = pl.BlockSpec((tn, tk), lambda i, j, k: (j, k)) if tb else pl.BlockSpec((tk, tn), lambda i, j, k: (k, j))
    o_spec = pl.BlockSpec((tm, tn), lambda i, j, k: (i, j))
    ca, cb = (0 if ta else 1), (1 if tb else 0)

    def body(*refs):
        if res is None:
            a_ref, b_ref, o_ref, acc_ref = refs
            r_ref = None
        else:
            a_ref, b_ref, r_ref, o_ref, acc_ref = refs
        k = pl.program_id(2)

        @pl.when(k == 0)
        def _():
            acc_ref[...] = jnp.zeros_like(acc_ref)

        acc_ref[...] += _dot(a_ref[...], b_ref[...], ca, cb)

        @pl.when(k == nk - 1)
        def _():
            r = acc_ref[...]
            if r_ref is not None:
                r = r + r_ref[...].astype(F32)
            o_ref[...] = r.astype(o_ref.dtype)

    ins = [a, b] + ([] if res is None else [res])
    in_specs = [a_spec, b_spec] + ([] if res is None else [o_spec])
    return pl.pallas_call(
        body, grid=(M // tm, N // tn, nk), in_specs=in_specs, out_specs=o_spec,
        out_shape=jax.ShapeDtypeStruct((M, N), out_dtype), scratch_shapes=[pltpu.VMEM((tm, tn), F32)],
        compiler_params=_cparams("parallel", "parallel", "arbitrary"), name=name)(*ins)


def _rows(S, cap):
    return _pick(S, cap, 8)


def _rms_fwd(x, w, name):
    S, D = x.shape
    tm = _rows(S, 512)

    def body(x_ref, w_ref, o_ref):
        xv = x_ref[...]
        r = lax.rsqrt(jnp.mean(xv * xv, axis=-1, keepdims=True) + EPS)
        o_ref[...] = ((xv * r) * w_ref[...]).astype(o_ref.dtype)

    row = pl.BlockSpec((tm, D), lambda i: (i, 0))
    return pl.pallas_call(body, grid=(S // tm,), in_specs=[row, pl.BlockSpec((1, D), lambda i: (0, 0))], out_specs=row,
                          out_shape=jax.ShapeDtypeStruct((S, D), _MXU), compiler_params=_cparams("parallel"), name=name)(x, w)


def _rms_bwd(dh, x, w, dres, name):
    S, D = x.shape
    tm = _rows(S, 256)

    def body(*refs):
        if dres is None:
            dh_ref, x_ref, w_ref, dx_ref, dxb_ref, dw_ref = refs
            dres_ref = None
        else:
            dh_ref, x_ref, w_ref, dres_ref, dx_ref, dxb_ref, dw_ref = refs
        xv = x_ref[...]
        r = lax.rsqrt(jnp.mean(xv * xv, axis=-1, keepdims=True) + EPS)
        xn = xv * r
        dy = dh_ref[...].astype(F32)

        @pl.when(pl.program_id(0) == 0)
        def _():
            dw_ref[...] = jnp.zeros_like(dw_ref)

        dw_ref[...] += jnp.sum(dy * xn, axis=0, keepdims=True)
        dxn = dy * w_ref[...]
        dx = r * (dxn - xn * jnp.mean(dxn * xn, axis=-1, keepdims=True))
        if dres_ref is not None:
            dx = dx + dres_ref[...]
        dx_ref[...] = dx
        dxb_ref[...] = dx.astype(dxb_ref.dtype)

    row = pl.BlockSpec((tm, D), lambda i: (i, 0))
    vec = pl.BlockSpec((1, D), lambda i: (0, 0))
    ins = [dh, x, w] + ([] if dres is None else [dres])
    in_specs = [row, row, vec] + ([] if dres is None else [row])
    return pl.pallas_call(
        body, grid=(S // tm,), in_specs=in_specs, out_specs=[row, row, vec],
        out_shape=[jax.ShapeDtypeStruct((S, D), F32), jax.ShapeDtypeStruct((S, D), _MXU), jax.ShapeDtypeStruct((1, D), F32)],
        compiler_params=_cparams("arbitrary"), name=name)(*ins)


def _loss_bwd(x, tgt, w, name):
    S, D = x.shape
    tm = _rows(S, 256)

    def body(x_ref, t_ref, w_ref, dx_ref, dxb_ref, dw_ref, loss_ref):
        xv = x_ref[...]
        r = lax.rsqrt(jnp.mean(xv * xv, axis=-1, keepdims=True) + EPS)
        xn = xv * r
        e = xn * w_ref[...] - t_ref[...]

        @pl.when(pl.program_id(0) == 0)
        def _():
            dw_ref[...] = jnp.zeros_like(dw_ref)
            loss_ref[...] = jnp.zeros_like(loss_ref)

        tok = jnp.mean(e * e, axis=-1, keepdims=True)
        loss_ref[...] += jnp.broadcast_to(0.5 * jnp.sum(tok, axis=0, keepdims=True), loss_ref.shape)
        dy = e * (1.0 / D)
        dw_ref[...] += jnp.sum(dy * xn, axis=0, keepdims=True)
        dxn = dy * w_ref[...]
        dx = r * (dxn - xn * jnp.mean(dxn * xn, axis=-1, keepdims=True))
        dx_ref[...] = dx
        dxb_ref[...] = dx.astype(dxb_ref.dtype)

    row = pl.BlockSpec((tm, D), lambda i: (i, 0))
    vec = pl.BlockSpec((1, D), lambda i: (0, 0))
    return pl.pallas_call(
        body, grid=(S // tm,), in_specs=[row, row, vec],
        out_specs=[row, row, vec, pl.BlockSpec((8, 128), lambda i: (0, 0))],
        out_shape=[jax.ShapeDtypeStruct((S, D), F32), jax.ShapeDtypeStruct((S, D), _MXU),
                   jax.ShapeDtypeStruct((1, D), F32), jax.ShapeDtypeStruct((8, 128), F32)],
        compiler_params=_cparams("arbitrary"), name=name)(x, tgt, w)


def _conv_tiles(S, C):
    return _rows(S, 256), _pick(C, 1536)


def _conv_silu_fwd(x, w, b, name):
    S, C = x.shape
    K = w.shape[0]
    tm, tc = _conv_tiles(S, C)

    def body(x_ref, p_ref, w_ref, b_ref, o_ref):
        sh = _shifted_rows(x_ref[...], jnp.where(pl.program_id(0) > 0, p_ref[...], 0.0), K)
        u = b_ref[...] + sum(sh[d] * w_ref[K - 1 - d:K - d, :] for d in range(K))
        o_ref[...] = u * _sigmoid(u)

    return pl.pallas_call(
        body, grid=(S // tm, C // tc),
        in_specs=[pl.BlockSpec((tm, tc), lambda i, j: (i, j)),
                  pl.BlockSpec((8, tc), lambda i, j: (jnp.maximum(i * (tm // 8) - 1, 0), j)),
                  pl.BlockSpec((K, tc), lambda i, j: (0, j)), pl.BlockSpec((1, tc), lambda i, j: (0, j))],
        out_specs=pl.BlockSpec((tm, tc), lambda i, j: (i, j)), out_shape=jax.ShapeDtypeStruct((S, C), F32),
        compiler_params=_cparams("parallel", "parallel"), name=name)(x, x, w, b)


def _conv_silu_bwd(x, dact, w, b, name):
    S, C = x.shape
    K = w.shape[0]
    tm, tc = _conv_tiles(S, C)

    def body(x_ref, p_ref, g_ref, w_ref, b_ref, du_ref, dw_ref, db_ref):
        i = pl.program_id(1)

        @pl.when(i == 0)
        def _():
            dw_ref[...] = jnp.zeros_like(dw_ref)
            db_ref[...] = jnp.zeros_like(db_ref)

        sh = _shifted_rows(x_ref[...], jnp.where(i > 0, p_ref[...], 0.0), K)
        u = b_ref[...] + sum(sh[d] * w_ref[K - 1 - d:K - d, :] for d in range(K))
        s = _sigmoid(u)
        du = g_ref[...].astype(F32) * (s * (1.0 + u * (1.0 - s)))
        du_ref[...] = du.astype(du_ref.dtype)
        db_ref[...] += jnp.sum(du, axis=0, keepdims=True)
        for d in range(K):
            dw_ref[K - 1 - d:K - d, :] += jnp.sum(du * sh[d], axis=0, keepdims=True)

    blk = pl.BlockSpec((tm, tc), lambda j, i: (i, j))
    kc, bc = pl.BlockSpec((K, tc), lambda j, i: (0, j)), pl.BlockSpec((1, tc), lambda j, i: (0, j))
    return pl.pallas_call(
        body, grid=(C // tc, S // tm),
        in_specs=[blk, pl.BlockSpec((8, tc), lambda j, i: (jnp.maximum(i * (tm // 8) - 1, 0), j)), blk, kc, bc],
        out_specs=[blk, kc, bc],
        out_shape=[jax.ShapeDtypeStruct((S, C), _MXU), jax.ShapeDtypeStruct((K, C), F32), jax.ShapeDtypeStruct((1, C), F32)],
        compiler_params=_cparams("parallel", "arbitrary"), name=name)(x, x, dact, w, b)


def _dwconv_bwd_x(dy, w, name):
    S, C = dy.shape
    K = w.shape[0]
    tm, tc = _conv_tiles(S, C)
    last = S // tm - 1
    hr = 8 * (4 // dy.dtype.itemsize)

    def body(g_ref, n_ref, w_ref, o_ref):
        cur = g_ref[...].astype(F32)
        nxt = jnp.where(pl.program_id(0) < last, n_ref[...].astype(F32), 0.0)
        xx = jnp.concatenate([cur, nxt], axis=0)
        acc = cur * w_ref[K - 1:K, :]
        for d in range(1, K):
            acc = acc + pltpu.roll(xx, tm + hr - d, 0)[:tm, :] * w_ref[K - 1 - d:K - d, :]
        o_ref[...] = acc.astype(o_ref.dtype)

    return pl.pallas_call(
        body, grid=(S // tm, C // tc),
        in_specs=[pl.BlockSpec((tm, tc), lambda i, j: (i, j)),
                  pl.BlockSpec((hr, tc), lambda i, j: (jnp.minimum((i + 1) * (tm // hr), S // hr - 1), j)),
                  pl.BlockSpec((K, tc), lambda i, j: (0, j))],
        out_specs=pl.BlockSpec((tm, tc), lambda i, j: (i, j)), out_shape=jax.ShapeDtypeStruct((S, C), _MXU),
        compiler_params=_cparams("parallel", "parallel"), name=name)(dy, dy, w)


def _shifted_rows(cur, prev, K):
    xx = jnp.concatenate([prev, cur], axis=0)
    return [cur] + [pltpu.roll(xx, d, 0)[8:, :] for d in range(1, K)]


def _conv_glu_fwd(x, w, b, name):
    S, C = x.shape
    K = w.shape[0]
    Fh = C // 2
    tm = _rows(S, 128)

    def body(x_ref, p_ref, w_ref, b_ref, o_ref):
        sh = _shifted_rows(x_ref[...], jnp.where(pl.program_id(0) > 0, p_ref[...], 0.0), K)
        u = b_ref[...] + sum(sh[d] * w_ref[K - 1 - d:K - d, :] for d in range(K))
        g = u[:, :Fh]
        o_ref[...] = (g * _sigmoid(g) * u[:, Fh:]).astype(o_ref.dtype)

    return pl.pallas_call(
        body, grid=(S // tm,),
        in_specs=[pl.BlockSpec((tm, C), lambda i: (i, 0)), pl.BlockSpec((8, C), lambda i: (jnp.maximum(i * (tm // 8) - 1, 0), 0)),
                  pl.BlockSpec((K, C), lambda i: (0, 0)), pl.BlockSpec((1, C), lambda i: (0, 0))],
        out_specs=pl.BlockSpec((tm, Fh), lambda i: (i, 0)), out_shape=jax.ShapeDtypeStruct((S, Fh), _MXU),
        compiler_params=_cparams("parallel"), name=name)(x, x, w, b)


def _conv_glu_bwd(x, dact, w, b, name):
    S, C = x.shape
    K = w.shape[0]
    Fh = C // 2
    tm = _rows(S, 128)

    def body(x_ref, p_ref, g_ref, w_ref, b_ref, du_ref, dw_ref, db_ref):
        i = pl.program_id(0)

        @pl.when(i == 0)
        def _():
            dw_ref[...] = jnp.zeros_like(dw_ref)
            db_ref[...] = jnp.zeros_like(db_ref)

        sh = _shifted_rows(x_ref[...], jnp.where(i > 0, p_ref[...], 0.0), K)
        u = b_ref[...] + sum(sh[d] * w_ref[K - 1 - d:K - d, :] for d in range(K))
        g = u[:, :Fh]
        da = g_ref[...].astype(F32)
        s = _sigmoid(g)
        halves = ((slice(0, Fh), da * u[:, Fh:] * (s * (1.0 + g * (1.0 - s)))), (slice(Fh, C), da * (g * s)))
        for cols, du in halves:
            du_ref[:, cols] = du.astype(du_ref.dtype)
            db_ref[:, cols] += jnp.sum(du, axis=0, keepdims=True)
            for d in range(K):
                dw_ref[K - 1 - d:K - d, cols] += jnp.sum(du * sh[d][:, cols], axis=0, keepdims=True)

    return pl.pallas_call(
        body, grid=(S // tm,),
        in_specs=[pl.BlockSpec((tm, C), lambda i: (i, 0)), pl.BlockSpec((8, C), lambda i: (jnp.maximum(i * (tm // 8) - 1, 0), 0)),
                  pl.BlockSpec((tm, Fh), lambda i: (i, 0)), pl.BlockSpec((K, C), lambda i: (0, 0)), pl.BlockSpec((1, C), lambda i: (0, 0))],
        out_specs=[pl.BlockSpec((tm, C), lambda i: (i, 0)), pl.BlockSpec((K, C), lambda i: (0, 0)), pl.BlockSpec((1, C), lambda i: (0, 0))],
        out_shape=[jax.ShapeDtypeStruct((S, C), _MXU), jax.ShapeDtypeStruct((K, C), F32), jax.ShapeDtypeStruct((1, C), F32)],
        compiler_params=_cparams("arbitrary"), name=name)(x, x, dact, w, b)


def _xattn_fwd(q, k, v, name):
    S, D = q.shape
    M = k.shape[0]
    hd = D // MEM_HEADS
    tm = _rows(S, 512)
    scale = 1.0 / math.sqrt(hd)

    def body(q_ref, k_ref, v_ref, o_ref):
        for h in range(MEM_HEADS):
            sl = slice(h * hd, (h + 1) * hd)
            s = _dot(q_ref[:, sl], k_ref[:, sl], 1, 1) * scale
            p = jnp.exp(s - jnp.max(s, axis=-1, keepdims=True))
            p = p / jnp.sum(p, axis=-1, keepdims=True)
            o_ref[:, sl] = _dot(p, v_ref[:, sl], 1, 0).astype(o_ref.dtype)

    kv = pl.BlockSpec((M, D), lambda i: (0, 0))
    row = pl.BlockSpec((tm, D), lambda i: (i, 0))
    return pl.pallas_call(body, grid=(S // tm,), in_specs=[row, kv, kv], out_specs=row,
                          out_shape=jax.ShapeDtypeStruct((S, D), _MXU), compiler_params=_cparams("parallel"), name=name)(q, k, v)


def _xattn_bwd(q, k, v, do, name):
    S, D = q.shape
    M = k.shape[0]
    hd = D // MEM_HEADS
    tm = _rows(S, 512)
    scale = 1.0 / math.sqrt(hd)

    def body(q_ref, k_ref, v_ref, do_ref, dq_ref, dk_ref, dv_ref):
        @pl.when(pl.program_id(0) == 0)
        def _():
            dk_ref[...] = jnp.zeros_like(dk_ref)
            dv_ref[...] = jnp.zeros_like(dv_ref)

        for h in range(MEM_HEADS):
            sl = slice(h * hd, (h + 1) * hd)
            qh, kh, vh, doh = q_ref[:, sl], k_ref[:, sl], v_ref[:, sl], do_ref[:, sl]
            s = _dot(qh, kh, 1, 1) * scale
            p = jnp.exp(s - jnp.max(s, axis=-1, keepdims=True))
            p = p / jnp.sum(p, axis=-1, keepdims=True)
            dp = _dot(doh, vh, 1, 1)
            dv_ref[:, sl] += _dot(p, doh, 0, 0)
            ds = (p * (dp - jnp.sum(dp * p, axis=-1, keepdims=True))) * scale
            dq_ref[:, sl] = _dot(ds, kh, 1, 0).astype(dq_ref.dtype)
            dk_ref[:, sl] += _dot(ds, qh, 0, 0)

    kv = pl.BlockSpec((M, D), lambda i: (0, 0))
    row = pl.BlockSpec((tm, D), lambda i: (i, 0))
    return pl.pallas_call(
        body, grid=(S // tm,), in_specs=[row, kv, kv, row], out_specs=[row, kv, kv],
        out_shape=[jax.ShapeDtypeStruct((S, D), _MXU), jax.ShapeDtypeStruct((M, D), F32), jax.ShapeDtypeStruct((M, D), F32)],
        compiler_params=_cparams("arbitrary"), name=name)(q, k, v, do)


def _tri(n, strict, upper):
    r = lax.broadcasted_iota(jnp.int32, (n, n), 0)
    c = lax.broadcasted_iota(jnp.int32, (n, n), 1)
    if upper:
        return (c > r) if strict else (c >= r)
    return (r > c) if strict else (r >= c)


def _ssd_prep(dtp, dt_bias, a_log, name):
    S = dtp.shape[0]
    L, H = SSD_CHUNK, SSD_HEADS

    def body(p_ref, b_ref, al_ref, dt_ref, cs_ref):
        v = p_ref[:, :H] + b_ref[...]
        dt = jnp.maximum(v, 0.0) + _log1p(jnp.exp(-jnp.abs(v)))
        dt_ref[...] = dt
        a = dt * (-jnp.exp(al_ref[...]))
        cs_ref[...] = jnp.dot(_tri(L, False, False).astype(F32), a, precision=_HI, preferred_element_type=F32)

    blk = pl.BlockSpec((L, H), lambda c: (c, 0))
    vec = pl.BlockSpec((1, H), lambda c: (0, 0))
    return pl.pallas_call(body, grid=(S // L,), in_specs=[pl.BlockSpec((L, DT_PAD), lambda c: (c, 0)), vec, vec],
                          out_specs=[blk, blk], out_shape=[jax.ShapeDtypeStruct((S, H), F32)] * 2,
                          compiler_params=_cparams("parallel"), name=name)(dtp, dt_bias, a_log)


def _head_col(blk_ref, h):
    sel = lax.broadcasted_iota(jnp.int32, (1, SSD_HEADS), 1) == h
    return jnp.sum(jnp.where(sel, blk_ref[...], 0.0), axis=1, keepdims=True)


def _ssdg_fwd(xs, Bm, Cm, dt, cs, csT, name):
    H, S, P = xs.shape
    L, N = SSD_CHUNK, SSD_STATE
    nc = S // L
    rep = H // SSD_GROUPS
    hs = range(rep)

    def body(x_ref, b_ref, c_ref, dt_ref, cs_ref, csT_ref, y_ref, prev_ref, st_ref):
        c, g = pl.program_id(0), pl.program_id(1)

        @pl.when(c == 0)
        def _():
            for hh in hs:
                st_ref[g * rep + hh] = jnp.zeros((P, N), F32)

        Bv, Cv = b_ref[...], c_ref[...]
        tril = _tri(L, False, False)
        dtc = [_head_col(dt_ref, g * rep + hh) for hh in hs]
        csc = [_head_col(cs_ref, g * rep + hh) for hh in hs]
        csr = [csT_ref[hh:hh + 1, :] for hh in hs]
        last = [r[:, L - 1:L] for r in csr]
        xc = [x_ref[hh] * dtc[hh] for hh in hs]
        cb = _dot(Cv, Bv, 1, 1)
        m = [cb * jnp.where(tril, jnp.exp(jnp.where(tril, csc[hh] - csr[hh], 0.0)), 0.0) for hh in hs]
        prev = [st_ref[g * rep + hh] for hh in hs]
        yd = [_dot(m[hh], xc[hh], 1, 0) for hh in hs]
        yo = [_dot(Cv, prev[hh], 1, 1) for hh in hs]
        new = [_dot(xc[hh] * jnp.exp(last[hh] - csc[hh]), Bv, 0, 0) for hh in hs]
        for hh in hs:
            y_ref[hh] = yd[hh] + yo[hh] * jnp.exp(csc[hh])
            prev_ref[hh] = prev[hh]
            st_ref[g * rep + hh] = prev[hh] * jnp.exp(last[hh]) + new[hh]

    tok = pl.BlockSpec((L, H), lambda c, g: (c, 0))
    return pl.pallas_call(
        body, grid=(nc, SSD_GROUPS),
        in_specs=[pl.BlockSpec((rep, L, P), lambda c, g: (g, c, 0)), pl.BlockSpec((None, L, N), lambda c, g: (g, c, 0)),
                  pl.BlockSpec((None, L, N), lambda c, g: (g, c, 0)), tok, tok, pl.BlockSpec((rep, L), lambda c, g: (g, c))],
        out_specs=[pl.BlockSpec((rep, L, P), lambda c, g: (g, c, 0)),
                   pl.BlockSpec((rep, None, P, N), lambda c, g: (g, c, 0, 0))],
        out_shape=[jax.ShapeDtypeStruct((H, S, P), F32), jax.ShapeDtypeStruct((H, nc, P, N), F32)],
        scratch_shapes=[pltpu.VMEM((H, P, N), F32)],
        compiler_params=_cparams("arbitrary", "arbitrary"), name=name)(xs, Bm, Cm, dt, cs, csT)


def _ssdg_bwd(xs, Bm, Cm, dt, cs, csT, prev, dy, a_log, d_skip, name):
    H, S, P = xs.shape
    L, N = SSD_CHUNK, SSD_STATE
    nc = S // L
    rep = H // SSD_GROUPS
    hs = range(rep)

    def rowsum(a):
        return jnp.sum(a, axis=1, keepdims=True)

    def body(x_ref, b_ref, c_ref, dt_ref, cs_ref, csT_ref, prev_ref, dy_ref, al_ref, dk_ref,
             dx_ref, db_ref, dc_ref, ddt_ref, da_ref, g_ref):
        ci, g = pl.program_id(0), pl.program_id(1)

        @pl.when(ci == 0)
        def _():
            for hh in hs:
                g_ref[g * rep + hh] = jnp.zeros((P, N), F32)

        @pl.when((ci == 0) & (g == 0))
        def _():
            da_ref[...] = jnp.zeros_like(da_ref)

        @pl.when(g == 0)
        def _():
            ddt_ref[...] = jnp.zeros_like(ddt_ref)

        lane = lax.broadcasted_iota(jnp.int32, (1, H), 1)
        sel = [lane == g * rep + hh for hh in hs]
        A_h = [-jnp.exp(rowsum(jnp.where(s, al_ref[...], 0.0))) for s in sel]
        dsk = [rowsum(jnp.where(s, dk_ref[...], 0.0)) for s in sel]
        dtc = [_head_col(dt_ref, g * rep + hh) for hh in hs]
        csc = [_head_col(cs_ref, g * rep + hh) for hh in hs]
        csr = [csT_ref[hh:hh + 1, :] for hh in hs]
        last = [r[:, L - 1:L] for r in csr]
        Bv, Cv = b_ref[...], c_ref[...]
        xv = [x_ref[hh] for hh in hs]
        xc = [xv[hh] * dtc[hh] for hh in hs]
        dY = [dy_ref[hh] for hh in hs]
        prv = [prev_ref[hh] for hh in hs]
        G = [g_ref[g * rep + hh] for hh in hs]
        ecs = [jnp.exp(v) for v in csc]
        w = [jnp.exp(last[hh] - csc[hh]) for hh in hs]
        cd = [jnp.exp(v) for v in last]
        tril = _tri(L, False, False)
        triu = _tri(L, False, True)
        lam = [jnp.where(tril, jnp.exp(jnp.where(tril, csc[hh] - csr[hh], 0.0)), 0.0) for hh in hs]
        lamT = [jnp.where(triu, jnp.exp(jnp.where(triu, csr[hh] - csc[hh], 0.0)), 0.0) for hh in hs]
        cb = _dot(Cv, Bv, 1, 1)
        bc = _dot(Bv, Cv, 1, 1)
        dM = [_dot(dY[hh], xc[hh], 1, 1) for hh in hs]
        dMT = [_dot(xc[hh], dY[hh], 1, 1) for hh in hs]
        cp = [_dot(Cv, prv[hh], 1, 1) for hh in hs]
        BG = [_dot(Bv, G[hh], 1, 1) for hh in hs]
        dYe = [dY[hh] * ecs[hh] for hh in hs]
        dprev = [_dot(dYe[hh], Cv, 0, 0) for hh in hs]
        m = [cb * lam[hh] for hh in hs]
        mT = [bc * lamT[hh] for hh in hs]
        dxc = [_dot(mT[hh], dY[hh], 1, 0) + w[hh] * BG[hh] for hh in hs]
        dcb = sum([dM[hh] * lam[hh] for hh in hs][1:], dM[0] * lam[0])
        dcbT = sum([dMT[hh] * lamT[hh] for hh in hs][1:], dMT[0] * lamT[0])
        dC = _dot(dcb, Bv, 1, 0)
        dB = _dot(dcbT, Cv, 1, 0)
        for hh in hs:
            dC = dC + _dot(dYe[hh], prv[hh], 1, 0)
            dB = dB + _dot(xc[hh] * w[hh], G[hh], 1, 0)
        dc_ref[...] = dC
        db_ref[...] = dB
        ddt_acc = jnp.zeros((L, H), F32)
        da_acc = jnp.zeros((1, H), F32)
        rev = _tri(L, False, True).astype(F32)
        for hh in hs:
            dww = rowsum(xc[hh] * BG[hh]) * w[hh]
            dcs = (rowsum(dM[hh] * m[hh]) - rowsum(dMT[hh] * mT[hh]) + rowsum(dY[hh] * (cp[hh] * ecs[hh])) - dww)
            extra = jnp.sum(dww, axis=0, keepdims=True) + cd[hh] * jnp.sum(rowsum(G[hh] * prv[hh]), axis=0, keepdims=True)
            g_ref[g * rep + hh] = G[hh] * cd[hh] + dprev[hh]
            da = jnp.dot(rev, dcs, precision=_HI, preferred_element_type=F32) + extra
            dx_ref[hh] = dxc[hh] * dtc[hh] + dY[hh] * dsk[hh]
            ddt_acc = ddt_acc + jnp.where(sel[hh], da * A_h[hh] + rowsum(dxc[hh] * xv[hh]), 0.0)
            da_acc = da_acc + jnp.where(sel[hh], jnp.sum(da * dtc[hh], axis=0, keepdims=True), 0.0)
        ddt_ref[...] += ddt_acc
        da_ref[...] += da_acc

    rc = lambda ci: nc - 1 - ci
    hd = pl.BlockSpec((rep, L, P), lambda ci, g: (g, rc(ci), 0))
    grp = pl.BlockSpec((None, L, N), lambda ci, g: (g, rc(ci), 0))
    tok = pl.BlockSpec((L, H), lambda ci, g: (rc(ci), 0))
    vec = pl.BlockSpec((1, H), lambda ci, g: (0, 0))
    return pl.pallas_call(
        body, grid=(nc, SSD_GROUPS),
        in_specs=[hd, grp, grp, tok, tok, pl.BlockSpec((rep, L), lambda ci, g: (g, rc(ci))),
                  pl.BlockSpec((rep, None, P, N), lambda ci, g: (g, rc(ci), 0, 0)), hd, vec, vec],
        out_specs=[hd, grp, grp, tok, vec],
        out_shape=[jax.ShapeDtypeStruct((H, S, P), F32), jax.ShapeDtypeStruct((SSD_GROUPS, S, N), F32),
                   jax.ShapeDtypeStruct((SSD_GROUPS, S, N), F32), jax.ShapeDtypeStruct((S, H), F32),
                   jax.ShapeDtypeStruct((1, H), F32)],
        scratch_shapes=[pltpu.VMEM((H, P, N), F32)],
        compiler_params=_cparams("arbitrary", "arbitrary"), name=name)(xs, Bm, Cm, dt, cs, csT, prev, dy, a_log, d_skip)


def _dt_bwd(ddt, dA, dtp, dt_bias, a_log, name):
    S, H = ddt.shape
    tm = _rows(S, 512)

    def body(g_ref, da_ref, p_ref, b_ref, al_ref, o_ref, db_ref, dal_ref):
        @pl.when(pl.program_id(0) == 0)
        def _():
            db_ref[...] = jnp.zeros_like(db_ref)
            dal_ref[...] = da_ref[...] * (-jnp.exp(al_ref[...]))

        g = g_ref[...] * _sigmoid(p_ref[:, :H] + b_ref[...])
        db_ref[...] += jnp.sum(g, axis=0, keepdims=True)
        o_ref[...] = jnp.zeros_like(o_ref)
        o_ref[:, :H] = g.astype(o_ref.dtype)

    vec = pl.BlockSpec((1, H), lambda i: (0, 0))
    return pl.pallas_call(
        body, grid=(S // tm,),
        in_specs=[pl.BlockSpec((tm, H), lambda i: (i, 0)), vec, pl.BlockSpec((tm, DT_PAD), lambda i: (i, 0)), vec, vec],
        out_specs=[pl.BlockSpec((tm, DT_PAD), lambda i: (i, 0)), vec, vec],
        out_shape=[jax.ShapeDtypeStruct((S, DT_PAD), _MXU), jax.ShapeDtypeStruct((1, H), F32), jax.ShapeDtypeStruct((1, H), F32)],
        compiler_params=_cparams("arbitrary"), name=name)(ddt, dA, dtp, dt_bias, a_log)


def _ssd_gate_fwd(y, act, z, dskip, w, name):
    S, D = y.shape
    tm = _rows(S, 256)
    Gw = D // SSD_GROUPS

    def body(y_ref, x_ref, z_ref, k_ref, w_ref, o_ref):
        zv = z_ref[...]
        y2 = (y_ref[...] + x_ref[...] * k_ref[...]) * (zv * _sigmoid(zv))
        for g in range(SSD_GROUPS):
            sl = slice(g * Gw, (g + 1) * Gw)
            v = y2[:, sl]
            r = lax.rsqrt(jnp.mean(v * v, axis=-1, keepdims=True) + EPS)
            o_ref[:, sl] = ((v * r) * w_ref[:, sl]).astype(o_ref.dtype)

    row = pl.BlockSpec((tm, D), lambda i: (i, 0))
    vec = pl.BlockSpec((1, D), lambda i: (0, 0))
    return pl.pallas_call(body, grid=(S // tm,), in_specs=[row, row, row, vec, vec], out_specs=row,
                          out_shape=jax.ShapeDtypeStruct((S, D), _MXU), compiler_params=_cparams("parallel"),
                          name=name)(y, act, z, dskip, w)


def _ssd_gate_bwd(dyn, y, act, z, dskip, w, name):
    S, D = y.shape
    tm = _rows(S, 256)
    Gw = D // SSD_GROUPS

    def body(g_ref, y_ref, x_ref, z_ref, k_ref, w_ref, dy_ref, dz_ref, dk_ref, dw_ref):
        @pl.when(pl.program_id(0) == 0)
        def _():
            dk_ref[...] = jnp.zeros_like(dk_ref)
            dw_ref[...] = jnp.zeros_like(dw_ref)

        zv = z_ref[...]
        xv = x_ref[...]
        s = _sigmoid(zv)
        sz = zv * s
        y1 = y_ref[...] + xv * k_ref[...]
        y2 = y1 * sz
        for g in range(SSD_GROUPS):
            sl = slice(g * Gw, (g + 1) * Gw)
            v = y2[:, sl]
            r = lax.rsqrt(jnp.mean(v * v, axis=-1, keepdims=True) + EPS)
            vn = v * r
            gy = g_ref[:, sl].astype(F32)
            dw_ref[:, sl] += jnp.sum(gy * vn, axis=0, keepdims=True)
            dvn = gy * w_ref[:, sl]
            dy2 = r * (dvn - vn * jnp.mean(dvn * vn, axis=-1, keepdims=True))
            dy1 = dy2 * sz[:, sl]
            dy_ref[:, sl] = dy1
            dz_ref[:, sl] = (dy2 * y1[:, sl] * (s[:, sl] * (1.0 + zv[:, sl] * (1.0 - s[:, sl])))).astype(dz_ref.dtype)
            dk_ref[:, sl] += jnp.sum(dy1 * xv[:, sl], axis=0, keepdims=True)

    row = pl.BlockSpec((tm, D), lambda i: (i, 0))
    vec = pl.BlockSpec((1, D), lambda i: (0, 0))
    return pl.pallas_call(
        body, grid=(S // tm,), in_specs=[row, row, row, row, vec, vec], out_specs=[row, row, vec, vec],
        out_shape=[jax.ShapeDtypeStruct((S, D), F32), jax.ShapeDtypeStruct((S, D), _MXU),
                   jax.ShapeDtypeStruct((1, D), F32), jax.ShapeDtypeStruct((1, D), F32)],
        compiler_params=_cparams("arbitrary"), name=name)(dyn, y, act, z, dskip, w)


def _split_dot(v, u):
    hi = v.astype(_MXU)
    lo = (v - hi.astype(F32)).astype(_MXU)
    dn = (((1,), (0,)), ((), ()))
    return (lax.dot_general(hi, u, dn, preferred_element_type=F32) + lax.dot_general(lo, u, dn, preferred_element_type=F32))


def _sb_tiles(S):
    return _pick(S, 256, 128)


SB_LANES = 128
SB_PACK = SB_LANES // SB_HEAD_DIM
SB_ROWS = 128
SB_SCALE = 1.0 / math.sqrt(SB_HEAD_DIM)


def _head_masks():
    lane = lax.broadcasted_iota(jnp.int32, (1, SB_LANES), 1)
    return [(lane // SB_HEAD_DIM) == hh for hh in range(SB_PACK)]


def _by_head(hm, vals):
    out = vals[-1]
    for hh in range(SB_PACK - 2, -1, -1):
        out = jnp.where(hm[hh], vals[hh], out)
    return out


SB_DEAD = -110.0


def _sb_alive(Rs):
    m = Rs[0]
    for R in Rs[1:]:
        m = jnp.maximum(m, R)
    return jnp.max(m) > SB_DEAD


def _sb_rows(a, r):
    return a[r * SB_ROWS:(r + 1) * SB_ROWS]


def _sb_assemble(hm, vals):
    nr = len(vals) // SB_PACK
    return jnp.concatenate([_by_head(hm, vals[r * SB_PACK:(r + 1) * SB_PACK]) for r in range(nr)], axis=0)


def _sb_scores(zs, U, Rs, masks):
    ls = [-jnp.maximum(z, 0.0) - jnp.log(1.0 + jnp.exp(-jnp.abs(z))) for z in zs]
    if masks is not None:
        ls = [jnp.where(m, l, 0.0) for m, l in zip(masks, ls)]
    Es = [lax.dot_general(l.astype(_MXU), U, (((1,), (0,)), ((), ())), preferred_element_type=F32) for l in ls]
    As = [jnp.exp(l + z + (E + R)) for l, z, E, R in zip(ls, zs, Es, Rs)]
    if masks is not None:
        As = [jnp.where(m, A, 0.0) for m, A in zip(masks, As)]
    return ls, [A.astype(_MXU) for A in As]


SB_GROUP = 2


def _sbg_chains(T):
    return [(b, r, hh) for b in range(SB_GROUP) for r in range(T // SB_ROWS) for hh in range(SB_PACK)]


def _lanes(a, b):
    return a[:, b * SB_LANES:(b + 1) * SB_LANES]


def _sbg_join(hm, vals):
    per = len(vals) // SB_GROUP
    return jnp.concatenate([_sb_assemble(hm, vals[b * per:(b + 1) * per]) for b in range(SB_GROUP)], axis=1)


def _sbg_head_sum(hm, a):
    return jnp.concatenate([_by_head(hm, [jnp.sum(jnp.where(m, _lanes(a, b), 0.0), axis=1, keepdims=True) for m in hm])
                            for b in range(SB_GROUP)], axis=1)


def _sbg_fwd(q_arr, k_arr, v_arr, cols, w, name, gather=()):
    S = q_arr.shape[0]
    T = _sb_tiles(S)
    cq, ck, cv = cols
    GW = SB_GROUP * SB_LANES
    nb = SB_WIDTH // GW
    ng = len(gather)
    send, finish = _gather_phases([g.shape[0] // 2 for g in gather])

    def body(q_ref, k_ref, v_ref, w_ref, *rest):
        o_ref, y_ref = rest[ng:ng + 2]
        comm = (rest[:ng], rest[ng + 2:2 * ng + 2], rest[2 * ng + 2:])
        i = pl.program_id(1)
        if ng:
            @pl.when((pl.program_id(0) == 0) & (i == 0))
            def _():
                send(*comm)
        hm = _head_masks()
        qs = q_ref[...] * SB_SCALE
        chains = _sbg_chains(T)
        qcs = [_sb_rows(jnp.where(hm[hh], _lanes(qs, b), jnp.zeros((T, SB_LANES), qs.dtype)), r) for b, r, hh in chains]
        U = _tri(T, True, False).astype(_MXU)

        def scores_of(j):
            kj = k_ref[pl.ds(pl.multiple_of(j * T, T), T), :]
            return [_dot(qc, _lanes(kj, b), 1, 1) for qc, (b, _, _) in zip(qcs, chains)]

        def weighted(Abs, j):
            vj = v_ref[pl.ds(pl.multiple_of(j * T, T), T), :]
            return _sbg_join(hm, [_dot(Ab, _lanes(vj, b), 1, 0) for Ab, (b, _, _) in zip(Abs, chains)])

        def step(carry):
            jj, acc, Rs, Aprev = carry
            j = i - 1 - jj
            zs = scores_of(j)
            acc = acc + weighted(Aprev, j + 1)
            ls, Abs = _sb_scores(zs, U, Rs, None)
            return jj + 1, acc, tuple(R + jnp.sum(l, axis=1, keepdims=True) for R, l in zip(Rs, ls)), tuple(Abs)

        masks = [_sb_rows(_tri(T, True, False), r) for _, r, _ in chains]
        zero = jnp.zeros((SB_ROWS, 1), F32)
        ls, Abs = _sb_scores(scores_of(i), U, (zero,) * len(chains), masks)
        carry = (jnp.int32(0), jnp.zeros((T, GW), F32), tuple(jnp.sum(l, axis=1, keepdims=True) for l in ls), tuple(Abs))
        jj, acc, _, Alast = lax.while_loop(lambda c: (c[0] < i) & _sb_alive(c[2]), step, carry)
        acc = acc + weighted(Alast, i - jj)
        o_ref[...] = acc
        r = lax.rsqrt(_sbg_head_sum(hm, acc * acc) * (1.0 / SB_HEAD_DIM) + EPS)
        y_ref[...] = ((acc * r) * w_ref[...]).astype(y_ref.dtype)
        if ng:
            @pl.when((pl.program_id(0) == nb - 1) & (i == S // T - 1))
            def _():
                finish(*comm)

    blk = pl.BlockSpec((T, GW), lambda h, i: (i, h))
    hbm = pl.BlockSpec(memory_space=pl.ANY)
    return pl.pallas_call(
        body, grid=(nb, S // T),
        in_specs=[pl.BlockSpec((T, GW), lambda h, i: (i, cq + h)), pl.BlockSpec((S, GW), lambda h, i: (0, ck + h), pipeline_mode=pl.Buffered(1)),
                  pl.BlockSpec((S, GW), lambda h, i: (0, cv + h), pipeline_mode=pl.Buffered(1)), pl.BlockSpec((1, GW), lambda h, i: (0, h))]
                 + [hbm] * ng,
        out_specs=[blk, blk] + [hbm] * ng,
        out_shape=[jax.ShapeDtypeStruct((S, SB_WIDTH), F32), jax.ShapeDtypeStruct((S, SB_WIDTH), _MXU)]
                  + [jax.ShapeDtypeStruct((4,) + tuple(g.shape), g.dtype) for g in gather],
        scratch_shapes=_gather_sems(ng) if ng else [],
        compiler_params=_cparams("arbitrary", "arbitrary") if ng else _cparams("parallel", "parallel"), name=name)(q_arr, k_arr, v_arr, w, *gather)


def _sbg_bwd(q_arr, k_arr, v_arr, cols, o, dy_arr, cdy, w, name, swap=()):
    S = q_arr.shape[0]
    T = _sb_tiles(S)
    cq, ck, cv = cols
    GW = SB_GROUP * SB_LANES
    nb = SB_WIDTH // GW
    ns = len(swap)
    send, finish = _swap_phases(ns)

    def body(q_ref, k_ref, v_ref, o_ref, dy_ref, w_ref, *rest):
        dq_ref, dk_ref, dv_ref, dw_ref = rest[ns:ns + 4]
        comm = (rest[:ns], rest[ns + 4:2 * ns + 4], rest[2 * ns + 4:])
        i = pl.program_id(1)
        if ns:
            @pl.when((pl.program_id(0) == 0) & (i == 0))
            def _():
                send(*comm)

        @pl.when(i == 0)
        def _():
            dk_ref[...] = jnp.zeros_like(dk_ref)
            dv_ref[...] = jnp.zeros_like(dv_ref)
            dw_ref[...] = jnp.zeros_like(dw_ref)

        hm = _head_masks()
        chains = _sbg_chains(T)
        qs = q_ref[...] * SB_SCALE
        ov = o_ref[...]
        gy = dy_ref[...]
        r = lax.rsqrt(_sbg_head_sum(hm, ov * ov) * (1.0 / SB_HEAD_DIM) + EPS)
        on = ov * r
        dw_ref[...] += jnp.sum(gy * on, axis=0, keepdims=True)
        don = gy * w_ref[...]
        do = r * (don - on * (_sbg_head_sum(hm, don * on) * (1.0 / SB_HEAD_DIM)))
        dob = do.astype(_MXU)
        dprod = dob.astype(F32) * ov
        zt = jnp.zeros((T, SB_LANES), dob.dtype)
        qm = [[jnp.where(hm[hh], _lanes(qs, b), zt) for hh in range(SB_PACK)] for b in range(SB_GROUP)]
        dm = [[jnp.where(hm[hh], _lanes(dob, b), zt) for hh in range(SB_PACK)] for b in range(SB_GROUP)]
        qcs = [_sb_rows(qm[b][hh], r_) for b, r_, hh in chains]
        doc = [_sb_rows(dm[b][hh], r_) for b, r_, hh in chains]
        Dt = [_sb_rows(jnp.sum(jnp.where(hm[hh], _lanes(dprod, b), 0.0), axis=1, keepdims=True), r_) for b, r_, hh in chains]
        U = _tri(T, True, False).astype(_MXU)
        Ui = _tri(T, False, False).astype(_MXU)

        def products_of(j):
            off = pl.multiple_of(j * T, T)
            kj = k_ref[pl.ds(off, T), :]
            vj = v_ref[pl.ds(off, T), :]
            return ([_dot(qc, _lanes(kj, b), 1, 1) for qc, (b, _, _) in zip(qcs, chains)],
                    [_dot(d, _lanes(vj, b), 1, 1) for d, (b, _, _) in zip(doc, chains)])

        def core(zs, dAs, Rs, Qs, masks):
            ls, Abs = _sb_scores(zs, U, Rs, masks)
            Gs = [dA * Ab.astype(F32) for dA, Ab in zip(dAs, Abs)]
            sfx = [_split_dot(G, Ui) for G in Gs]
            dzs = []
            for c, (l, G, s, D, Q) in enumerate(zip(ls, Gs, sfx, Dt, Qs)):
                P = D - (s + Q)
                dz = jnp.exp(l) * (G + P) - P
                if masks is not None:
                    dz = jnp.where(masks[c], dz, 0.0)
                dzs.append(dz.astype(_MXU))
            newR = tuple(R + jnp.sum(l, axis=1, keepdims=True) for R, l in zip(Rs, ls))
            newQ = tuple(Q + jnp.sum(G, axis=1, keepdims=True) for Q, G in zip(Qs, Gs))
            return tuple(Abs), tuple(dzs), newR, newQ

        def over_rows(vals, other):
            nr = T // SB_ROWS
            tiles = []
            for b in range(SB_GROUP):
                acc = None
                for hh in range(SB_PACK):
                    rows = jnp.concatenate([vals[(b * nr + r_) * SB_PACK + hh] for r_ in range(nr)], axis=0)
                    part = _dot(rows, other[b][hh], 0, 0)
                    acc = part if acc is None else acc + part
                tiles.append(acc)
            return jnp.concatenate(tiles, axis=1)

        def emit(Abs, dzs, j):
            off = pl.multiple_of(j * T, T)
            kj = k_ref[pl.ds(off, T), :]
            dk_ref[pl.ds(off, T), :] += over_rows(dzs, qm)
            dv_ref[pl.ds(off, T), :] += over_rows(Abs, dm)
            return _sbg_join(hm, [_dot(dzb, _lanes(kj, b), 1, 0) for dzb, (b, _, _) in zip(dzs, chains)])

        def step(carry):
            jj, dq, Rs, Qs, Aprev, dzprev = carry
            j = i - 1 - jj
            zs, dAs = products_of(j)
            dq = dq + emit(Aprev, dzprev, j + 1)
            Abs, dzs, Rs, Qs = core(zs, dAs, Rs, Qs, None)
            return jj + 1, dq, Rs, Qs, Abs, dzs

        masks = [_sb_rows(_tri(T, True, False), r_) for _, r_, _ in chains]
        zero = (jnp.zeros((SB_ROWS, 1), F32),) * len(chains)
        zs, dAs = products_of(i)
        Abs, dzs, Rs, Qs = core(zs, dAs, zero, zero, masks)
        jj, dq, _, _, Alast, dzlast = lax.while_loop(lambda c: (c[0] < i) & _sb_alive(c[2]), step,
                                                     (jnp.int32(0), jnp.zeros((T, GW), F32), Rs, Qs, Abs, dzs))
        dq = dq + emit(Alast, dzlast, i - jj)
        dq_ref[...] = (dq * SB_SCALE).astype(dq_ref.dtype)
        if ns:
            @pl.when((pl.program_id(0) == nb - 1) & (i == S // T - 1))
            def _():
                finish(*comm)

    blk = pl.BlockSpec((T, GW), lambda h, i: (i, h))
    full = pl.BlockSpec((S, GW), lambda h, i: (0, h), pipeline_mode=pl.Buffered(1))
    wsp = pl.BlockSpec((1, GW), lambda h, i: (0, h))
    hbm = pl.BlockSpec(memory_space=pl.ANY)
    return pl.pallas_call(
        body, grid=(nb, S // T),
        in_specs=[pl.BlockSpec((T, GW), lambda h, i: (i, cq + h)), pl.BlockSpec((S, GW), lambda h, i: (0, ck + h), pipeline_mode=pl.Buffered(1)),
                  pl.BlockSpec((S, GW), lambda h, i: (0, cv + h), pipeline_mode=pl.Buffered(1)), blk,
                  pl.BlockSpec((T, GW), lambda h, i: (i, cdy + h)), wsp] + [hbm] * ns,
        out_specs=[blk, full, full, wsp] + [hbm] * ns,
        out_shape=[jax.ShapeDtypeStruct((S, SB_WIDTH), _MXU), jax.ShapeDtypeStruct((S, SB_WIDTH), F32),
                   jax.ShapeDtypeStruct((S, SB_WIDTH), F32), jax.ShapeDtypeStruct((1, SB_WIDTH), F32)]
                  + [jax.ShapeDtypeStruct((3,) + tuple(a.shape[1:]), a.dtype) for a in swap],
        scratch_shapes=[pltpu.SemaphoreType.DMA((3 * ns,))] * 2 if ns else [],
        compiler_params=_cparams("arbitrary", "arbitrary") if ns else _cparams("parallel", "arbitrary"),
        name=name)(q_arr, k_arr, v_arr, o, dy_arr, w, *swap)


def _adamw(w, g, m, v, name):
    R, C = w.shape
    tm = _rows(R, 256) if R % 8 == 0 else R
    c1 = 1.0 - ADAM_B1 ** ADAM_STEP
    c2 = 1.0 - ADAM_B2 ** ADAM_STEP

    def body(w_ref, g_ref, m_ref, v_ref, d_ref, nm_ref, nv_ref):
        gv = g_ref[...]
        mn = ADAM_B1 * m_ref[...] + (1.0 - ADAM_B1) * gv
        vn = ADAM_B2 * v_ref[...] + (1.0 - ADAM_B2) * (gv * gv)
        d_ref[...] = -ADAM_LR * ((mn / c1) / (jnp.sqrt(vn / c2) + ADAM_EPS) + ADAM_WD * w_ref[...])
        nm_ref[...] = mn
        nv_ref[...] = vn

    blk = pl.BlockSpec((tm, C), lambda i: (i, 0))
    return pl.pallas_call(body, grid=(R // tm,), in_specs=[blk] * 4, out_specs=[blk] * 3,
                          out_shape=[jax.ShapeDtypeStruct((R, C), F32)] * 3, compiler_params=_cparams("parallel"),
                          name=name)(w, g, m, v)


def _sum_lead(a, name, first=None, pick=None, wire=False):
    n, R, C = a.shape
    tm = _rows(R, 256)
    nin = 1 if first is None else 2

    def body(*refs):
        refs = refs[nin - 1:]
        a_ref = refs[nin - 1]
        s = a_ref[0].astype(F32) if first is None else refs[0][...] + a_ref[0]
        for p in range(1, n):
            s = s + a_ref[p]
        for o_ref in refs[nin:]:
            o_ref[...] = s.astype(o_ref.dtype)

    outs = [jax.ShapeDtypeStruct((R, C), F32)] + ([jax.ShapeDtypeStruct((R, C), _WIRE)] if wire else [])
    if first is None:
        row = pl.BlockSpec((tm, C), lambda i: (i, 0))
        res = pl.pallas_call(body, grid=(R // tm,), in_specs=[pl.BlockSpec((n, tm, C), lambda i: (0, i, 0))],
                             out_specs=[row] * len(outs), out_shape=outs, compiler_params=_cparams("parallel"), name=name)(a)
    else:
        row = pl.BlockSpec((tm, C), lambda i, p: (i, 0))
        grid_spec = pltpu.PrefetchScalarGridSpec(
            num_scalar_prefetch=1, grid=(R // tm,),
            in_specs=[pl.BlockSpec((None, tm, C), lambda i, p: (p[0], i, 0)), pl.BlockSpec((n, tm, C), lambda i, p: (0, i, 0))],
            out_specs=[row] * len(outs))
        res = pl.pallas_call(body, grid_spec=grid_spec, out_shape=outs, compiler_params=_cparams("parallel"), name=name)(pick, first, a)
    return res if wire else res[0]


_GROUP_BITS = {'c': ((0, 0, 1),), 'xy': ((0, 1, 0), (1, 0, 0), (1, 1, 0)),
               'xyc': tuple((k >> 2 & 1, k >> 1 & 1, k & 1) for k in range(1, 8))}


def _exchange(srcs, *, group, same_src, own, chunks, name):
    flips = _GROUP_BITS[group]
    n = len(flips) + 1
    na = len(srcs)
    blk_shapes = [tuple(s.shape) if same_src else tuple(s.shape[1:]) for s in srcs]
    assert all(bs[0] % chunks == 0 for bs in blk_shapes), blk_shapes

    def body(*refs):
        src_refs, dst_refs = refs[:na], refs[na:2 * na]
        send_sems, recv_sems, loc_sems = refs[2 * na:]
        x, y, c = lax.axis_index("x"), lax.axis_index("y"), lax.axis_index("c")

        def member(px, py, pc):
            return {'c': pc, 'xy': 2 * px + py, 'xyc': 4 * px + 2 * py + pc}[group]

        def piece(ref, a, q):
            rows = blk_shapes[a][0] // chunks
            return ref.at[pl.ds(q * rows, rows)]

        me = member(x, y, c)
        started, arrivals = [], []
        for a in range(na):
            mine = src_refs[a] if same_src else src_refs[a].at[me]
            if own:
                for q in range(chunks):
                    cp = pltpu.make_async_copy(piece(mine, a, q), piece(dst_refs[a].at[me], a, q), loc_sems.at[a * chunks + q])
                    cp.start()
                    started.append(cp.wait)
            for kk, (fx, fy, fc) in enumerate(flips):
                px, py, pc = (1 - x if fx else x), (1 - y if fy else y), (1 - c if fc else c)
                peer = member(px, py, pc)
                out_blk = src_refs[a] if same_src else src_refs[a].at[peer]
                there = dst_refs[a].at[me if own else kk]
                here = dst_refs[a].at[peer if own else kk]
                for q in range(chunks):
                    s = (a * (n - 1) + kk) * chunks + q
                    out = pltpu.make_async_remote_copy(
                        src_ref=piece(out_blk, a, q), dst_ref=piece(there, a, q), send_sem=send_sems.at[s],
                        recv_sem=recv_sems.at[s], device_id=(px, py, pc), device_id_type=pl.DeviceIdType.MESH)
                    out.start()
                    started.append(out.wait_send)
                    arrivals.append(pltpu.make_async_remote_copy(
                        src_ref=piece(mine, a, q), dst_ref=piece(here, a, q), send_sem=send_sems.at[s],
                        recv_sem=recv_sems.at[s], device_id=(px, py, pc), device_id_type=pl.DeviceIdType.MESH).wait_recv)
        for wait in arrivals + started:
            wait()

    nsem = na * (n - 1) * chunks
    hbm = pl.BlockSpec(memory_space=pl.ANY)
    return pl.pallas_call(
        body, in_specs=[hbm] * na, out_specs=[hbm] * na,
        out_shape=[jax.ShapeDtypeStruct(((n if own else n - 1),) + bs, s.dtype) for bs, s in zip(blk_shapes, srcs)],
        scratch_shapes=[pltpu.SemaphoreType.DMA((nsem,)), pltpu.SemaphoreType.DMA((nsem,)),
                        pltpu.SemaphoreType.DMA((na * chunks,))],
        compiler_params=pltpu.CompilerParams(has_side_effects=True), name=name)(*srcs)


def _gather_phases(halves):
    flips = _GROUP_BITS['xy']
    nf = len(flips)
    na = len(halves)

    def copies(src_refs, dst_refs, sems):
        send_sems, recv_sems, fsend_sems, frecv_sems = sems
        x, y, c = lax.axis_index("x"), lax.axis_index("y"), lax.axis_index("c")

        def half(ref, a, which):
            return ref.at[pl.ds(pl.multiple_of(which * halves[a], 8), halves[a])]

        def copy(src, dst, pair, s, to):
            return pltpu.make_async_remote_copy(src_ref=src, dst_ref=dst, send_sem=pair[0].at[s], recv_sem=pair[1].at[s],
                                                device_id=to, device_id_type=pl.DeviceIdType.MESH)

        out = []
        for a in range(na):
            for kk, (fx, fy, _) in enumerate(flips):
                peer = ((1 - x if fx else x), (1 - y if fy else y), c)
                there = dst_refs[a].at[2 * peer[0] + peer[1]]
                s = a * nf + kk
                ici, d2d = (send_sems, recv_sems), (fsend_sems, frecv_sems)
                out.append((copy(half(src_refs[a], a, c), half(dst_refs[a].at[2 * x + y], a, c), ici, s, peer),
                            copy(half(src_refs[a], a, c), half(there, a, c), ici, s, (x, y, c)),
                            copy(half(there, a, c), half(there, a, c), d2d, s, (x, y, 1 - c)),
                            copy(half(there, a, 1 - c), half(there, a, 1 - c), d2d, s, (x, y, 1 - c))))
        return out

    def send(src_refs, dst_refs, sems):
        for first, _, _, _ in copies(src_refs, dst_refs, sems):
            first.start()

    def finish(src_refs, dst_refs, sems):
        cs = copies(src_refs, dst_refs, sems)
        for _, landed, onward, _ in cs:
            landed.wait_recv()
            onward.start()
        for _, _, _, passed in cs:
            passed.wait_recv()
        for first, _, onward, _ in cs:
            first.wait_send()
            onward.wait_send()

    return send, finish


def _swap_phases(na):
    flips = _GROUP_BITS['xy']

    def copies(src_refs, dst_refs, sems):
        x, y, c = lax.axis_index("x"), lax.axis_index("y"), lax.axis_index("c")
        out = []
        for a in range(na):
            for kk, (fx, fy, _) in enumerate(flips):
                px, py = (1 - x if fx else x), (1 - y if fy else y)
                s = a * len(flips) + kk
                out.append(pltpu.make_async_remote_copy(
                    src_ref=src_refs[a].at[2 * px + py], dst_ref=dst_refs[a].at[kk], send_sem=sems[0].at[s], recv_sem=sems[1].at[s],
                    device_id=(px, py, c), device_id_type=pl.DeviceIdType.MESH))
        return out

    def send(src_refs, dst_refs, sems):
        for cp in copies(src_refs, dst_refs, sems):
            cp.start()

    def finish(src_refs, dst_refs, sems):
        cs = copies(src_refs, dst_refs, sems)
        for cp in cs:
            cp.wait_recv()
        for cp in cs:
            cp.wait_send()

    return send, finish


def _gather_sems(na):
    return [pltpu.SemaphoreType.DMA((na * len(_GROUP_BITS['xy']),))] * 4


def _gather_chips(srcs, name):
    na = len(srcs)
    send, finish = _gather_phases([s.shape[0] // 2 for s in srcs])

    def body(*refs):
        send(refs[:na], refs[na:2 * na], refs[2 * na:])
        finish(refs[:na], refs[na:2 * na], refs[2 * na:])

    hbm = pl.BlockSpec(memory_space=pl.ANY)
    return pl.pallas_call(
        body, in_specs=[hbm] * na, out_specs=[hbm] * na,
        out_shape=[jax.ShapeDtypeStruct((4,) + tuple(s.shape), s.dtype) for s in srcs], scratch_shapes=_gather_sems(na),
        compiler_params=pltpu.CompilerParams(has_side_effects=True), name=name)(*srcs)


def _to_shards(name, full):
    R, C = full.shape
    if name in COL_SPLIT:
        return full.reshape(R, 4, C // 4).transpose(1, 0, 2)
    return full.reshape(4, R // 4, C)


def _from_shards(name, sh):
    n, R, C = sh.shape
    if name in COL_SPLIT:
        return sh.transpose(1, 0, 2).reshape(R, n * C)
    return sh.reshape(n * R, C)


def _pack_rows(parts, width, rows):
    n = parts[0].shape[0]
    flat = jnp.concatenate([p.reshape(n, -1) for p in parts], axis=1)
    return jnp.pad(flat, ((0, 0), (0, rows * width - flat.shape[1]))).reshape(n, rows, width)


def _unpack_rows(buf, shapes):
    n = buf.shape[0]
    flat = buf.reshape(n, -1)
    out, o = [], 0
    for s in shapes:
        sz = math.prod(s)
        out.append(flat[:, o:o + sz].reshape((n,) + tuple(s)))
        o += sz
    return out


def _split_rows(a, rows):
    out, o = [], 0
    for r in rows:
        out.append(a[:, o:o + r])
        o += r
    return out


def _ceil_to(v, m):
    return -(-v // m) * m


def kernel(x, mem, norm_mix_w, w_in, conv_ssd_w, conv_ssd_b, dt_bias, a_log, d_skip, ssd_norm_w, sb_norm_w, w_out, norm_mem_w, norm_memkv_w, w_mq, w_mk, w_mv, w_mo, norm_ffn_w, w_up, conv_ffn_w, conv_ffn_b, w_down, norm_final_w, loss_target, m_norm_mix_w, m_w_in, m_conv_ssd_w, m_conv_ssd_b, m_dt_bias, m_a_log, m_d_skip, m_ssd_norm_w, m_sb_norm_w, m_w_out, m_norm_mem_w, m_norm_memkv_w, m_w_mq, m_w_mk, m_w_mv, m_w_mo, m_norm_ffn_w, m_w_up, m_conv_ffn_w, m_conv_ffn_b, m_w_down, m_norm_final_w, v_norm_mix_w, v_w_in, v_conv_ssd_w, v_conv_ssd_b, v_dt_bias, v_a_log, v_d_skip, v_ssd_norm_w, v_sb_norm_w, v_w_out, v_norm_mem_w, v_norm_memkv_w, v_w_mq, v_w_mk, v_w_mv, v_w_mo, v_norm_ffn_w, v_w_up, v_conv_ffn_w, v_conv_ffn_b, v_w_down, v_norm_final_w):
    W = dict(norm_mix_w=norm_mix_w, w_in=w_in, conv_ssd_w=conv_ssd_w, conv_ssd_b=conv_ssd_b, dt_bias=dt_bias, a_log=a_log,
             d_skip=d_skip, ssd_norm_w=ssd_norm_w, sb_norm_w=sb_norm_w, w_out=w_out, norm_mem_w=norm_mem_w,
             norm_memkv_w=norm_memkv_w, w_mq=w_mq, w_mk=w_mk, w_mv=w_mv, w_mo=w_mo, norm_ffn_w=norm_ffn_w, w_up=w_up,
             conv_ffn_w=conv_ffn_w, conv_ffn_b=conv_ffn_b, w_down=w_down, norm_final_w=norm_final_w)
    Mo = dict(norm_mix_w=m_norm_mix_w, w_in=m_w_in, conv_ssd_w=m_conv_ssd_w, conv_ssd_b=m_conv_ssd_b, dt_bias=m_dt_bias,
              a_log=m_a_log, d_skip=m_d_skip, ssd_norm_w=m_ssd_norm_w, sb_norm_w=m_sb_norm_w, w_out=m_w_out,
              norm_mem_w=m_norm_mem_w, norm_memkv_w=m_norm_memkv_w, w_mq=m_w_mq, w_mk=m_w_mk, w_mv=m_w_mv, w_mo=m_w_mo,
              norm_ffn_w=m_norm_ffn_w, w_up=m_w_up, conv_ffn_w=m_conv_ffn_w, conv_ffn_b=m_conv_ffn_b, w_down=m_w_down,
              norm_final_w=m_norm_final_w)
    Vo = dict(norm_mix_w=v_norm_mix_w, w_in=v_w_in, conv_ssd_w=v_conv_ssd_w, conv_ssd_b=v_conv_ssd_b, dt_bias=v_dt_bias,
              a_log=v_a_log, d_skip=v_d_skip, ssd_norm_w=v_ssd_norm_w, sb_norm_w=v_sb_norm_w, w_out=v_w_out,
              norm_mem_w=v_norm_mem_w, norm_memkv_w=v_norm_memkv_w, w_mq=v_w_mq, w_mk=v_w_mk, w_mv=v_w_mv, w_mo=v_w_mo,
              norm_ffn_w=v_norm_ffn_w, w_up=v_w_up, conv_ffn_w=v_conv_ffn_w, conv_ffn_b=v_conv_ffn_b, w_down=v_w_down,
              norm_final_w=v_norm_final_w)
    shapes = {n: W[n].shape for n in WEIGHTS}
    sh2 = {n: (1, a.shape[-1]) if a.ndim < 3 else a.shape[-2:] for n, a in W.items()}
    w2 = {n: W[n].reshape(sh2[n]) for n in WEIGHTS}
    x2d = x[0]
    S, D = x2d.shape
    H, P, N = SSD_HEADS, SSD_HEAD_DIM, SSD_STATE

    cv_rows = _ceil_to(-(-sum(math.prod(sh2[n]) for n in CONVW) // 128), 32)
    cpack = _pack_rows([w2[n][None] for n in CONVW], 128, cv_rows)[0]
    stacked = jnp.concatenate([w2[n].astype(_MXU) for n in ROW_SPLIT], axis=0)
    cidx = lax.axis_index("c")
    oidx = 2 * lax.axis_index("x") + lax.axis_index("y")
    now, later = [w2['w_in'].astype(_MXU), cpack], [stacked, w2['w_up'].astype(_MXU)]
    g_in, call = [lax.dynamic_update_index_in_dim(g, m, oidx, 0) for m, g in zip(now, _gather_chips(now, "gather_weights"))]
    full = {'w_in': _from_shards('w_in', g_in)}
    full.update({n: _from_shards(n, a) for n, a in zip(CONVW, _unpack_rows(call, [sh2[n] for n in CONVW]))})

    o1 = SSD_INNER
    o2 = o1 + SSD_XBC
    o3 = o2 + SSD_HEADS
    Wi = full['w_in']
    W_z, W_xbc, W_qkv = Wi[:, :o1], Wi[:, o1:o2], Wi[:, o3:]
    W_dt = jnp.pad(Wi[:, o2:o3], ((0, 0), (0, DT_PAD - SSD_HEADS)))
    W_in_r = jnp.concatenate([W_z, W_xbc, W_qkv, W_dt], axis=1)
    dskip_rep = jnp.repeat(w2['d_skip'], P, axis=1)

    h1 = _rms_fwd(x2d, w2['norm_mix_w'], "norm_mix")
    z = _mm(h1, W_z, name="proj_z")
    xbc = _mm(h1, W_xbc, name="proj_xbc")
    dtp = _mm(h1, W_dt, name="proj_dt")
    qkv = _mm(h1, W_qkv, out_dtype=_MXU, name="proj_qkv")
    act = _conv_silu_fwd(xbc, full['conv_ssd_w'], w2['conv_ssd_b'], "ssd_conv_silu")
    dt, cs = _ssd_prep(dtp, w2['dt_bias'], w2['a_log'], "ssd_prep")
    csT = cs.T
    def heads(a, nh):
        return a.reshape(S, nh, a.shape[1] // nh).transpose(1, 0, 2)

    def unheads(a):
        return a.transpose(1, 0, 2).reshape(S, a.shape[0] * a.shape[2])

    xs_h = heads(act[:, :o1], H)
    Bm = heads(act[:, o1:o1 + SSD_GROUPS * N], SSD_GROUPS)
    Cm = heads(act[:, o1 + SSD_GROUPS * N:], SSD_GROUPS)
    y_h, prev = _ssdg_fwd(xs_h, Bm, Cm, dt, cs, csT, "ssd_scan")
    y_scan = unheads(y_h)
    y_ssd = _ssd_gate_fwd(y_scan, act, z, dskip_rep, w2['ssd_norm_w'], "ssd_gate")
    nsb = SB_WIDTH // (SB_GROUP * SB_LANES)
    qkv_cols = (0, nsb, 2 * nsb)
    o_sb, y_sb, *others = _sbg_fwd(qkv, qkv, qkv, qkv_cols, w2['sb_norm_w'], "sb_attn", gather=later)
    g_rows, g_up = [lax.dynamic_update_index_in_dim(g, m, oidx, 0) for m, g in zip(later, others)]
    full['w_up'] = _from_shards('w_up', g_up)
    full.update({n: _from_shards(n, a) for n, a in zip(ROW_SPLIT, _split_rows(g_rows, [sh2[n][0] for n in ROW_SPLIT]))})
    ycat = jnp.concatenate([y_ssd, y_sb], axis=1)
    x_2 = _mm(ycat, full['w_out'], res=x2d, name="out_proj")
    h2 = _rms_fwd(x_2, w2['norm_mem_w'], "norm_mem")
    qm = _mm(h2, full['w_mq'], out_dtype=_MXU, name="mem_q")
    mn = _rms_fwd(mem[0], w2['norm_memkv_w'], "norm_memkv")
    km = _mm(mn, full['w_mk'], out_dtype=_MXU, name="mem_k")
    vm = _mm(mn, full['w_mv'], out_dtype=_MXU, name="mem_v")
    om = _xattn_fwd(qm, km, vm, "mem_attn")
    x_3 = _mm(om, full['w_mo'], res=x_2, name="mem_o")
    h3 = _rms_fwd(x_3, w2['norm_ffn_w'], "norm_ffn")
    up = _mm(h3, full['w_up'], name="ffn_up")
    a_ffn = _conv_glu_fwd(up, full['conv_ffn_w'], w2['conv_ffn_b'], "ffn_conv_glu")
    x_4 = _mm(a_ffn, full['w_down'], res=x_3, name="ffn_down")
    dx4, dx4b, g_final, loss_blk = _loss_bwd(x_4, loss_target[0], w2['norm_final_w'], "loss_head")

    G = {'norm_final_w': g_final}
    dact = _mm(dx4b, full['w_down'], tb=True, name="d_ffn_act")
    G['w_down'] = _mm(a_ffn, dx4b, ta=True, name="g_w_down")
    du, G['conv_ffn_w'], G['conv_ffn_b'] = _conv_glu_bwd(up, dact, full['conv_ffn_w'], w2['conv_ffn_b'], "d_ffn_conv_glu")
    dup = _dwconv_bwd_x(du, full['conv_ffn_w'], "d_ffn_conv")
    dh3 = _mm(dup, full['w_up'], tb=True, name="d_h3")
    G['w_up'] = _mm(h3, dup, ta=True, name="g_w_up")
    dx3, dx3b, G['norm_ffn_w'] = _rms_bwd(dh3, x_3, w2['norm_ffn_w'], dx4, "d_norm_ffn")
    dom = _mm(dx3b, full['w_mo'], tb=True, out_dtype=_MXU, name="d_mem_o")
    G['w_mo'] = _mm(om, dx3b, ta=True, name="g_w_mo")
    dqm, dkm, dvm = _xattn_bwd(qm, km, vm, dom, "d_mem_attn")
    G['w_mq'] = _mm(h2, dqm, ta=True, name="g_w_mq")
    dh2 = _mm(dqm, full['w_mq'], tb=True, name="d_h2")
    dx2, dx2b, G['norm_mem_w'] = _rms_bwd(dh2, x_2, w2['norm_mem_w'], dx3, "d_norm_mem")
    G['w_mk'] = _mm(mn, dkm, ta=True, name="g_w_mk")
    G['w_mv'] = _mm(mn, dvm, ta=True, name="g_w_mv")
    dmn = _mm(dvm, full['w_mv'], tb=True, res=_mm(dkm, full['w_mk'], tb=True, name="d_mn_k"), name="d_mn_v")
    _, _, G['norm_memkv_w'] = _rms_bwd(dmn, mem[0], w2['norm_memkv_w'], None, "d_norm_memkv")
    dycat = _mm(dx2b, full['w_out'], tb=True, name="d_ycat")
    G['w_out'] = _mm(ycat, dx2b, ta=True, name="g_w_out")
    dy1, dz, g_dskip_lane, G['ssd_norm_w'] = _ssd_gate_bwd(dycat, y_scan, act, z, dskip_rep, w2['ssd_norm_w'], "d_ssd_gate")
    dxs_h, dB, dC, ddt, dA = _ssdg_bwd(xs_h, Bm, Cm, dt, cs, csT, prev, heads(dy1, H), w2['a_log'], w2['d_skip'], "d_ssd_scan")
    G['d_skip'] = jnp.sum(g_dskip_lane.reshape(H, P), axis=1)[None, :]
    dact_xbc = jnp.concatenate([unheads(dxs_h), unheads(dB), unheads(dC)], axis=1)
    dpre, G['conv_ssd_w'], G['conv_ssd_b'] = _conv_silu_bwd(xbc, dact_xbc, full['conv_ssd_w'], w2['conv_ssd_b'], "d_ssd_conv_silu")
    dxbc = _dwconv_bwd_x(dpre, full['conv_ssd_w'], "d_ssd_conv")
    ddtp, G['dt_bias'], G['a_log'] = _dt_bwd(ddt, dA, dtp, w2['dt_bias'], w2['a_log'], "d_dt")
    def pair_sums(to_pair, tag):
        got = _exchange(to_pair, group='c', same_src=False, own=False, chunks=4, name="reduce_pair" + tag)
        sums, wires = [], []
        for k, (t, g) in enumerate(zip(to_pair, got)):
            _, _, r, cw = t.shape
            full_sum, wire_sum = _sum_lead(g.reshape(1, 4 * r, cw), "reduce_pair_sum%s%d" % (tag, k), first=t.reshape(2, 4 * r, cw),
                                           pick=cidx.reshape(1), wire=True)
            sums.append(full_sum.reshape(4, r, cw))
            wires.append(wire_sum.reshape(4, r, cw))
        return sums, wires

    def chip_sums(sums, got, tag):
        return [_sum_lead(g, "reduce_chips_sum%s%d" % (tag, k), first=p, pick=oidx.reshape(1)) for k, (p, g) in enumerate(zip(sums, got))]

    by_owner = [jnp.concatenate([_to_shards(n, G[n]) for n in ROW_SPLIT], axis=1), _to_shards('w_up', G['w_up'])]
    pair_a, wire_a = pair_sums([a.reshape(4, 2, a.shape[1] // 2, a.shape[2]).transpose(1, 0, 2, 3) for a in by_owner], "_a")

    dq, dk, dv, G['sb_norm_w'], *got_a = _sbg_bwd(qkv, qkv, qkv, qkv_cols, o_sb, dycat, o1 // (SB_GROUP * SB_LANES), w2['sb_norm_w'],
                                                  "d_sb_attn", swap=wire_a)
    chips_rows, chips_up = chip_sums(pair_a, got_a, "_a")
    dproj = jnp.concatenate([dz, dxbc, dq, dk.astype(_MXU), dv.astype(_MXU), ddtp], axis=1)
    dh1 = _mm(dproj, W_in_r, tb=True, name="d_h1")
    g_in_r = _mm(h1, dproj, ta=True, name="g_w_in")
    nq = 3 * SB_WIDTH

    def in_cols(lo, hi):
        spans = []
        for a, b, shift in ((0, o2, 0), (o2, o3, nq), (o3, o3 + nq, o2 - o3)):
            s, e = max(lo, a), min(hi, b)
            if s < e:
                spans.append((s + shift, e + shift))
        return spans

    hr_in, cs_in = g_in_r.shape[0] // 2, (o3 + nq) // 4
    g_in_pair = jnp.stack([jnp.stack([jnp.concatenate([g_in_r[h * hr_in:(h + 1) * hr_in, s:e] for s, e in in_cols(j * cs_in, (j + 1) * cs_in)],
                                                      axis=1) for j in range(4)]) for h in range(2)])
    grad_x, _, G['norm_mix_w'] = _rms_bwd(dh1, x2d, w2['norm_mix_w'], dx2, "d_norm_mix")

    pair_b, wire_b = pair_sums([g_in_pair], "_b")
    (chips_in,) = chip_sums(pair_b, _exchange(wire_b, group='xy', same_src=False, own=False, chunks=1, name="reduce_chips"), "_b")
    chips = [chips_rows, chips_in, chips_up]
    got = _exchange(chips, group='c', same_src=True, own=False, chunks=4, name="share_pair")
    red = [jnp.where(cidx == 0, jnp.concatenate([m, g[0]], axis=0), jnp.concatenate([g[0], m], axis=0))[None]
           for m, g in zip(chips, got)]
    gsh = dict(zip(ROW_SPLIT, [a[0] for a in _split_rows(red[0], [sh2[n][0] for n in ROW_SPLIT])]))
    gsh['w_in'], gsh['w_up'] = red[1][0], red[2][0]

    small_parts = [G[n].reshape(1, -1) for n in SMALL + CONVW] + [loss_blk[:1, :1]]
    small_shapes = [sh2[n] for n in SMALL] + [G[n].shape for n in CONVW] + [(1, 1)]
    small_rows = _ceil_to(-(-sum(math.prod(s) for s in small_shapes) // 128), 8)
    spack = _pack_rows(small_parts, 128, small_rows)[0]
    (gathered,) = _exchange([spack], group='xyc', same_src=True, own=True, chunks=1, name="gather_small")
    parts = [a[0] for a in _unpack_rows(_sum_lead(gathered, "small_sum")[None], small_shapes)]
    gsh.update(zip(SMALL, parts))
    for n, a in zip(CONVW, parts[len(SMALL):-1]):
        gsh[n] = lax.dynamic_index_in_dim(_to_shards(n, a), oidx, 0, keepdims=False)
    loss = parts[-1].reshape(())

    delta, new_m, new_v = {}, {}, {}
    for n in BIG:
        delta[n], new_m[n], new_v[n] = _adamw(w2[n], gsh[n], Mo[n].reshape(sh2[n]), Vo[n].reshape(sh2[n]), "adamw_" + n)
    for grp, width, tag in ((CONVW, 128, "adamw_conv"), (SMALL, 128, "adamw_small")):
        rows = _ceil_to(-(-sum(math.prod(sh2[n]) for n in grp) // width), 8)
        packed = [_pack_rows([src[n].reshape(1, -1) for n in grp], width, rows)[0]
                  for src in (w2, gsh, {n: Mo[n] for n in grp}, {n: Vo[n] for n in grp})]
        outs = _adamw(*packed, tag)
        for dst, o in zip((delta, new_m, new_v), outs):
            dst.update(zip(grp, [a[0] for a in _unpack_rows(o[None], [sh2[n] for n in grp])]))

    def shaped(d):
        return [d[n].reshape(shapes[n]) for n in WEIGHTS]

    return (loss, grad_x[None], *shaped(gsh), *shaped(delta), *shaped(new_m), *shaped(new_v))
```

```python
import math

import jax
import jax.numpy as jnp
from jax import lax
from jax.experimental import pallas as pl
from jax.experimental.pallas import tpu as pltpu

F32 = jnp.float32
_MXU = jnp.bfloat16
_WIRE = jnp.bfloat16
EPS = 1e-6
_VMEM_LIMIT = 48 * 1024 * 1024
_HI = lax.Precision.HIGHEST

SSD_HEADS = 16
SSD_HEAD_DIM = 64
SSD_GROUPS = 2
SSD_STATE = 128
SSD_CHUNK = 128
SSD_INNER = SSD_HEADS * SSD_HEAD_DIM
SSD_XBC = SSD_INNER + 2 * SSD_GROUPS * SSD_STATE
SB_HEADS = 16
SB_HEAD_DIM = 64
SB_WIDTH = SB_HEADS * SB_HEAD_DIM
MEM_HEADS = 4
DT_PAD = 128

ADAM_LR = 0.001
ADAM_B1 = 0.9
ADAM_B2 = 0.999
ADAM_EPS = 1e-08
ADAM_WD = 0.01
ADAM_STEP = 10

WEIGHTS = ['norm_mix_w', 'w_in', 'conv_ssd_w', 'conv_ssd_b', 'dt_bias', 'a_log', 'd_skip', 'ssd_norm_w',
           'sb_norm_w', 'w_out', 'norm_mem_w', 'norm_memkv_w', 'w_mq', 'w_mk', 'w_mv', 'w_mo', 'norm_ffn_w',
           'w_up', 'conv_ffn_w', 'conv_ffn_b', 'w_down', 'norm_final_w']
BIG = ['w_in', 'w_out', 'w_mq', 'w_mk', 'w_mv', 'w_mo', 'w_up', 'w_down']
COL_SPLIT = ('w_in', 'w_up', 'conv_ssd_w', 'conv_ffn_w')
ROW_SPLIT = ['w_out', 'w_mq', 'w_mk', 'w_mv', 'w_mo', 'w_down']
CONVW = ['conv_ssd_w', 'conv_ffn_w']
SMALL = ['norm_mix_w', 'conv_ssd_b', 'dt_bias', 'a_log', 'd_skip', 'ssd_norm_w', 'sb_norm_w', 'norm_mem_w',
         'norm_memkv_w', 'norm_ffn_w', 'conv_ffn_b', 'norm_final_w']


def _cparams(*sem):
    return pltpu.CompilerParams(dimension_semantics=sem if sem else None, vmem_limit_bytes=_VMEM_LIMIT)


def _pick(n, cap, mult=128):
    best = None
    for d in range(mult, min(n, cap) + 1, mult):
        if n % d == 0:
            best = d
    return n if best is None else best


def _dot(a, b, ca, cb):
    return lax.dot_general(a.astype(_MXU), b.astype(_MXU), (((ca,), (cb,)), ((), ())), preferred_element_type=F32)


def _sigmoid(v):
    return 1.0 / (1.0 + jnp.exp(-v))


def _log1p(u):
    w = 1.0 + u
    return jnp.where(w == 1.0, u, jnp.log(w) * (u / (w - 1.0)))


def _mm(a, b, *, ta=False, tb=False, res=None, out_dtype=F32, name):
    if ta:
        K, M = a.shape
    else:
        M, K = a.shape
    if tb:
        N, K2 = b.shape
    else:
        K2, N = b.shape
    assert K == K2, (a.shape, b.shape)
    tm = _pick(M, 1408, 128 if ta else 16)
    tn = _pick(N, 1536)
    tk = _pick(K, 1536)
    nk = K // tk
    a_spec = pl.BlockSpec((tk, tm), lambda i, j, k: (k, i)) if ta else pl.BlockSpec((tm, tk), lambda i, j, k: (i, k))
    b_spec = pl.BlockSpec((tn, tk), lambda i, j, k: (j, k)) if tb else pl.BlockSpec((tk, tn), lambda i, j, k: (k, j))
    o_spec = pl.BlockSpec((tm, tn), lambda i, j, k: (i, j))
    ca, cb = (0 if ta else 1), (1 if tb else 0)

    def body(*refs):
        if res is None:
            a_ref, b_ref, o_ref, acc_ref = refs
            r_ref = None
        else:
            a_ref, b_ref, r_ref, o_ref, acc_ref = refs
        k = pl.program_id(2)

        @pl.when(k == 0)
        def _():
            acc_ref[...] = jnp.zeros_like(acc_ref)

        acc_ref[...] += _dot(a_ref[...], b_ref[...], ca, cb)

        @pl.when(k == nk - 1)
        def _():
            r = acc_ref[...]
            if r_ref is not None:
                r = r + r_ref[...].astype(F32)
            o_ref[...] = r.astype(o_ref.dtype)

    ins = [a, b] + ([] if res is None else [res])
    in_specs = [a_spec, b_spec] + ([] if res is None else [o_spec])
    return pl.pallas_call(
        body, grid=(M // tm, N // tn, nk), in_specs=in_specs, out_specs=o_spec,
        out_shape=jax.ShapeDtypeStruct((M, N), out_dtype), scratch_shapes=[pltpu.VMEM((tm, tn), F32)],
        compiler_params=_cparams("parallel", "parallel", "arbitrary"), name=name)(*ins)


def _rows(S, cap):
    return _pick(S, cap, 8)


def _rms_fwd(x, w, name, gather=()):
    S, D = x.shape
    tm = _rows(S, 512)
    ng = len(gather)
    send, finish = _gather_phases([g.shape[0] // 2 for g in gather])

    def body(x_ref, w_ref, *rest):
        o_ref = rest[ng]
        comm = (rest[:ng], rest[ng + 1:2 * ng + 1], rest[2 * ng + 1:])
        if ng:
            @pl.when(pl.program_id(0) == 0)
            def _():
                send(*comm)

        xv = x_ref[...]
        r = lax.rsqrt(jnp.mean(xv * xv, axis=-1, keepdims=True) + EPS)
        o_ref[...] = ((xv * r) * w_ref[...]).astype(o_ref.dtype)
        if ng:
            @pl.when(pl.program_id(0) == S // tm - 1)
            def _():
                finish(*comm)

    row = pl.BlockSpec((tm, D), lambda i: (i, 0))
    hbm = pl.BlockSpec(memory_space=pl.ANY)
    res = pl.pallas_call(
        body, grid=(S // tm,), in_specs=[row, pl.BlockSpec((1, D), lambda i: (0, 0))] + [hbm] * ng, out_specs=[row] + [hbm] * ng,
        out_shape=[jax.ShapeDtypeStruct((S, D), _MXU)] + [jax.ShapeDtypeStruct((4,) + tuple(g.shape), g.dtype) for g in gather],
        scratch_shapes=_gather_sems(ng) if ng else [],
        compiler_params=_cparams("arbitrary") if ng else _cparams("parallel"), name=name)(x, w, *gather)
    return res if ng else res[0]


def _rms_bwd(dh, x, w, dres, name):
    S, D = x.shape
    tm = _rows(S, 256)

    def body(*refs):
        if dres is None:
            dh_ref, x_ref, w_ref, dx_ref, dxb_ref, dw_ref = refs
            dres_ref = None
        else:
            dh_ref, x_ref, w_ref, dres_ref, dx_ref, dxb_ref, dw_ref = refs
        xv = x_ref[...]
        r = lax.rsqrt(jnp.mean(xv * xv, axis=-1, keepdims=True) + EPS)
        xn = xv * r
        dy = dh_ref[...].astype(F32)

        @pl.when(pl.program_id(0) == 0)
        def _():
            dw_ref[...] = jnp.zeros_like(dw_ref)

        dw_ref[...] += jnp.sum(dy * xn, axis=0, keepdims=True)
        dxn = dy * w_ref[...]
        dx = r * (dxn - xn * jnp.mean(dxn * xn, axis=-1, keepdims=True))
        if dres_ref is not None:
            dx = dx + dres_ref[...]
        dx_ref[...] = dx
        dxb_ref[...] = dx.astype(dxb_ref.dtype)

    row = pl.BlockSpec((tm, D), lambda i: (i, 0))
    vec = pl.BlockSpec((1, D), lambda i: (0, 0))
    ins = [dh, x, w] + ([] if dres is None else [dres])
    in_specs = [row, row, vec] + ([] if dres is None else [row])
    return pl.pallas_call(
        body, grid=(S // tm,), in_specs=in_specs, out_specs=[row, row, vec],
        out_shape=[jax.ShapeDtypeStruct((S, D), F32), jax.ShapeDtypeStruct((S, D), _MXU), jax.ShapeDtypeStruct((1, D), F32)],
        compiler_params=_cparams("arbitrary"), name=name)(*ins)


def _loss_bwd(x, tgt, w, name):
    S, D = x.shape
    tm = _rows(S, 256)

    def body(x_ref, t_ref, w_ref, dx_ref, dxb_ref, dw_ref, loss_ref):
        xv = x_ref[...]
        r = lax.rsqrt(jnp.mean(xv * xv, axis=-1, keepdims=True) + EPS)
        xn = xv * r
        e = xn * w_ref[...] - t_ref[...]

        @pl.when(pl.program_id(0) == 0)
        def _():
            dw_ref[...] = jnp.zeros_like(dw_ref)
            loss_ref[...] = jnp.zeros_like(loss_ref)

        tok = jnp.mean(e * e, axis=-1, keepdims=True)
        loss_ref[...] += jnp.broadcast_to(0.5 * jnp.sum(tok, axis=0, keepdims=True), loss_ref.shape)
        dy = e * (1.0 / D)
        dw_ref[...] += jnp.sum(dy * xn, axis=0, keepdims=True)
        dxn = dy * w_ref[...]
        dx = r * (dxn - xn * jnp.mean(dxn * xn, axis=-1, keepdims=True))
        dx_ref[...] = dx
        dxb_ref[...] = dx.astype(dxb_ref.dtype)

    row = pl.BlockSpec((tm, D), lambda i: (i, 0))
    vec = pl.BlockSpec((1, D), lambda i: (0, 0))
    return pl.pallas_call(
        body, grid=(S // tm,), in_specs=[row, row, vec],
        out_specs=[row, row, vec, pl.BlockSpec((8, 128), lambda i: (0, 0))],
        out_shape=[jax.ShapeDtypeStruct((S, D), F32), jax.ShapeDtypeStruct((S, D), _MXU),
                   jax.ShapeDtypeStruct((1, D), F32), jax.ShapeDtypeStruct((8, 128), F32)],
        compiler_params=_cparams("arbitrary"), name=name)(x, tgt, w)


def _conv_tiles(S, C):
    return _rows(S, 256), _pick(C, 1536)


def _conv_silu_fwd(x, w, b, name):
    S, C = x.shape
    K = w.shape[0]
    tm, tc = _conv_tiles(S, C)

    def body(x_ref, p_ref, w_ref, b_ref, o_ref):
        sh = _shifted_rows(x_ref[...], jnp.where(pl.program_id(0) > 0, p_ref[...], 0.0), K)
        u = b_ref[...] + sum(sh[d] * w_ref[K - 1 - d:K - d, :] for d in range(K))
        o_ref[...] = u * _sigmoid(u)

    return pl.pallas_call(
        body, grid=(S // tm, C // tc),
        in_specs=[pl.BlockSpec((tm, tc), lambda i, j: (i, j)),
                  pl.BlockSpec((8, tc), lambda i, j: (jnp.maximum(i * (tm // 8) - 1, 0), j)),
                  pl.BlockSpec((K, tc), lambda i, j: (0, j)), pl.BlockSpec((1, tc), lambda i, j: (0, j))],
        out_specs=pl.BlockSpec((tm, tc), lambda i, j: (i, j)), out_shape=jax.ShapeDtypeStruct((S, C), F32),
        compiler_params=_cparams("parallel", "parallel"), name=name)(x, x, w, b)


def _conv_silu_bwd(x, dact, w, b, name):
    S, C = x.shape
    K = w.shape[0]
    tm, tc = _conv_tiles(S, C)

    def body(x_ref, p_ref, g_ref, w_ref, b_ref, du_ref, dw_ref, db_ref):
        i = pl.program_id(1)

        @pl.when(i == 0)
        def _():
            dw_ref[...] = jnp.zeros_like(dw_ref)
            db_ref[...] = jnp.zeros_like(db_ref)

        sh = _shifted_rows(x_ref[...], jnp.where(i > 0, p_ref[...], 0.0), K)
        u = b_ref[...] + sum(sh[d] * w_ref[K - 1 - d:K - d, :] for d in range(K))
        s = _sigmoid(u)
        du = g_ref[...].astype(F32) * (s * (1.0 + u * (1.0 - s)))
        du_ref[...] = du.astype(du_ref.dtype)
        db_ref[...] += jnp.sum(du, axis=0, keepdims=True)
        for d in range(K):
            dw_ref[K - 1 - d:K - d, :] += jnp.sum(du * sh[d], axis=0, keepdims=True)

    blk = pl.BlockSpec((tm, tc), lambda j, i: (i, j))
    kc, bc = pl.BlockSpec((K, tc), lambda j, i: (0, j)), pl.BlockSpec((1, tc), lambda j, i: (0, j))
    return pl.pallas_call(
        body, grid=(C // tc, S // tm),
        in_specs=[blk, pl.BlockSpec((8, tc), lambda j, i: (jnp.maximum(i * (tm // 8) - 1, 0), j)), blk, kc, bc],
        out_specs=[blk, kc, bc],
        out_shape=[jax.ShapeDtypeStruct((S, C), _MXU), jax.ShapeDtypeStruct((K, C), F32), jax.ShapeDtypeStruct((1, C), F32)],
        compiler_params=_cparams("parallel", "arbitrary"), name=name)(x, x, dact, w, b)


def _dwconv_bwd_x(dy, w, name):
    S, C = dy.shape
    K = w.shape[0]
    tm, tc = _conv_tiles(S, C)
    last = S // tm - 1
    hr = 8 * (4 // dy.dtype.itemsize)

    def body(g_ref, n_ref, w_ref, o_ref):
        cur = g_ref[...].astype(F32)
        nxt = jnp.where(pl.program_id(0) < last, n_ref[...].astype(F32), 0.0)
        xx = jnp.concatenate([cur, nxt], axis=0)
        acc = cur * w_ref[K - 1:K, :]
        for d in range(1, K):
            acc = acc + pltpu.roll(xx, tm + hr - d, 0)[:tm, :] * w_ref[K - 1 - d:K - d, :]
        o_ref[...] = acc.astype(o_ref.dtype)

    return pl.pallas_call(
        body, grid=(S // tm, C // tc),
        in_specs=[pl.BlockSpec((tm, tc), lambda i, j: (i, j)),
                  pl.BlockSpec((hr, tc), lambda i, j: (jnp.minimum((i + 1) * (tm // hr), S // hr - 1), j)),
                  pl.BlockSpec((K, tc), lambda i, j: (0, j))],
        out_specs=pl.BlockSpec((tm, tc), lambda i, j: (i, j)), out_shape=jax.ShapeDtypeStruct((S, C), _MXU),
        compiler_params=_cparams("parallel", "parallel"), name=name)(dy, dy, w)


def _shifted_rows(cur, prev, K):
    xx = jnp.concatenate([prev, cur], axis=0)
    return [cur] + [pltpu.roll(xx, d, 0)[8:, :] for d in range(1, K)]


def _conv_glu_fwd(x, w, b, name):
    S, C = x.shape
    K = w.shape[0]
    Fh = C // 2
    tm = _rows(S, 128)

    def body(x_ref, p_ref, w_ref, b_ref, o_ref):
        sh = _shifted_rows(x_ref[...], jnp.where(pl.program_id(0) > 0, p_ref[...], 0.0), K)
        u = b_ref[...] + sum(sh[d] * w_ref[K - 1 - d:K - d, :] for d in range(K))
        g = u[:, :Fh]
        o_ref[...] = (g * _sigmoid(g) * u[:, Fh:]).astype(o_ref.dtype)

    return pl.pallas_call(
        body, grid=(S // tm,),
        in_specs=[pl.BlockSpec((tm, C), lambda i: (i, 0)), pl.BlockSpec((8, C), lambda i: (jnp.maximum(i * (tm // 8) - 1, 0), 0)),
                  pl.BlockSpec((K, C), lambda i: (0, 0)), pl.BlockSpec((1, C), lambda i: (0, 0))],
        out_specs=pl.BlockSpec((tm, Fh), lambda i: (i, 0)), out_shape=jax.ShapeDtypeStruct((S, Fh), _MXU),
        compiler_params=_cparams("parallel"), name=name)(x, x, w, b)


def _conv_glu_bwd(x, dact, w, b, name):
    S, C = x.shape
    K = w.shape[0]
    Fh = C // 2
    tm = _rows(S, 128)

    def body(x_ref, p_ref, g_ref, w_ref, b_ref, du_ref, dw_ref, db_ref):
        i = pl.program_id(0)

        @pl.when(i == 0)
        def _():
            dw_ref[...] = jnp.zeros_like(dw_ref)
            db_ref[...] = jnp.zeros_like(db_ref)

        sh = _shifted_rows(x_ref[...], jnp.where(i > 0, p_ref[...], 0.0), K)
        u = b_ref[...] + sum(sh[d] * w_ref[K - 1 - d:K - d, :] for d in range(K))
        g = u[:, :Fh]
        da = g_ref[...].astype(F32)
        s = _sigmoid(g)
        halves = ((slice(0, Fh), da * u[:, Fh:] * (s * (1.0 + g * (1.0 - s)))), (slice(Fh, C), da * (g * s)))
        for cols, du in halves:
            du_ref[:, cols] = du.astype(du_ref.dtype)
            db_ref[:, cols] += jnp.sum(du, axis=0, keepdims=True)
            for d in range(K):
                dw_ref[K - 1 - d:K - d, cols] += jnp.sum(du * sh[d][:, cols], axis=0, keepdims=True)

    return pl.pallas_call(
        body, grid=(S // tm,),
        in_specs=[pl.BlockSpec((tm, C), lambda i: (i, 0)), pl.BlockSpec((8, C), lambda i: (jnp.maximum(i * (tm // 8) - 1, 0), 0)),
                  pl.BlockSpec((tm, Fh), lambda i: (i, 0)), pl.BlockSpec((K, C), lambda i: (0, 0)), pl.BlockSpec((1, C), lambda i: (0, 0))],
        out_specs=[pl.BlockSpec((tm, C), lambda i: (i, 0)), pl.BlockSpec((K, C), lambda i: (0, 0)), pl.BlockSpec((1, C), lambda i: (0, 0))],
        out_shape=[jax.ShapeDtypeStruct((S, C), _MXU), jax.ShapeDtypeStruct((K, C), F32), jax.ShapeDtypeStruct((1, C), F32)],
        compiler_params=_cparams("arbitrary"), name=name)(x, x, dact, w, b)


def _xattn_fwd(q, k, v, name):
    S, D = q.shape
    M = k.shape[0]
    hd = D // MEM_HEADS
    tm = _rows(S, 512)
    scale = 1.0 / math.sqrt(hd)

    def body(q_ref, k_ref, v_ref, o_ref):
        for h in range(MEM_HEADS):
            sl = slice(h * hd, (h + 1) * hd)
            s = _dot(q_ref[:, sl], k_ref[:, sl], 1, 1) * scale
            p = jnp.exp(s - jnp.max(s, axis=-1, keepdims=True))
            p = p / jnp.sum(p, axis=-1, keepdims=True)
            o_ref[:, sl] = _dot(p, v_ref[:, sl], 1, 0).astype(o_ref.dtype)

    kv = pl.BlockSpec((M, D), lambda i: (0, 0))
    row = pl.BlockSpec((tm, D), lambda i: (i, 0))
    return pl.pallas_call(body, grid=(S // tm,), in_specs=[row, kv, kv], out_specs=row,
                          out_shape=jax.ShapeDtypeStruct((S, D), _MXU), compiler_params=_cparams("parallel"), name=name)(q, k, v)


def _xattn_bwd(q, k, v, do, name):
    S, D = q.shape
    M = k.shape[0]
    hd = D // MEM_HEADS
    tm = _rows(S, 512)
    scale = 1.0 / math.sqrt(hd)

    def body(q_ref, k_ref, v_ref, do_ref, dq_ref, dk_ref, dv_ref):
        @pl.when(pl.program_id(0) == 0)
        def _():
            dk_ref[...] = jnp.zeros_like(dk_ref)
            dv_ref[...] = jnp.zeros_like(dv_ref)

        for h in range(MEM_HEADS):
            sl = slice(h * hd, (h + 1) * hd)
            qh, kh, vh, doh = q_ref[:, sl], k_ref[:, sl], v_ref[:, sl], do_ref[:, sl]
            s = _dot(qh, kh, 1, 1) * scale
            p = jnp.exp(s - jnp.max(s, axis=-1, keepdims=True))
            p = p / jnp.sum(p, axis=-1, keepdims=True)
            dp = _dot(doh, vh, 1, 1)
            dv_ref[:, sl] += _dot(p, doh, 0, 0)
            ds = (p * (dp - jnp.sum(dp * p, axis=-1, keepdims=True))) * scale
            dq_ref[:, sl] = _dot(ds, kh, 1, 0).astype(dq_ref.dtype)
            dk_ref[:, sl] += _dot(ds, qh, 0, 0)

    kv = pl.BlockSpec((M, D), lambda i: (0, 0))
    row = pl.BlockSpec((tm, D), lambda i: (i, 0))
    return pl.pallas_call(
        body, grid=(S // tm,), in_specs=[row, kv, kv, row], out_specs=[row, kv, kv],
        out_shape=[jax.ShapeDtypeStruct((S, D), _MXU), jax.ShapeDtypeStruct((M, D), F32), jax.ShapeDtypeStruct((M, D), F32)],
        compiler_params=_cparams("arbitrary"), name=name)(q, k, v, do)


def _tri(n, strict, upper):
    r = lax.broadcasted_iota(jnp.int32, (n, n), 0)
    c = lax.broadcasted_iota(jnp.int32, (n, n), 1)
    if upper:
        return (c > r) if strict else (c >= r)
    return (r > c) if strict else (r >= c)


def _ssd_prep(dtp, dt_bias, a_log, name):
    S = dtp.shape[0]
    L, H = SSD_CHUNK, SSD_HEADS

    def body(p_ref, b_ref, al_ref, dt_ref, cs_ref):
        v = p_ref[:, :H] + b_ref[...]
        dt = jnp.maximum(v, 0.0) + _log1p(jnp.exp(-jnp.abs(v)))
        dt_ref[...] = dt
        a = dt * (-jnp.exp(al_ref[...]))
        cs_ref[...] = jnp.dot(_tri(L, False, False).astype(F32), a, precision=_HI, preferred_element_type=F32)

    blk = pl.BlockSpec((L, H), lambda c: (c, 0))
    vec = pl.BlockSpec((1, H), lambda c: (0, 0))
    return pl.pallas_call(body, grid=(S // L,), in_specs=[pl.BlockSpec((L, DT_PAD), lambda c: (c, 0)), vec, vec],
                          out_specs=[blk, blk], out_shape=[jax.ShapeDtypeStruct((S, H), F32)] * 2,
                          compiler_params=_cparams("parallel"), name=name)(dtp, dt_bias, a_log)


def _head_col(blk_ref, h):
    sel = lax.broadcasted_iota(jnp.int32, (1, SSD_HEADS), 1) == h
    return jnp.sum(jnp.where(sel, blk_ref[...], 0.0), axis=1, keepdims=True)


def _ssdg_fwd(xs, Bm, Cm, dt, cs, csT, name):
    H, S, P = xs.shape
    L, N = SSD_CHUNK, SSD_STATE
    nc = S // L
    rep = H // SSD_GROUPS
    hs = range(rep)

    def body(x_ref, b_ref, c_ref, dt_ref, cs_ref, csT_ref, y_ref, prev_ref, st_ref):
        c, g = pl.program_id(0), pl.program_id(1)

        @pl.when(c == 0)
        def _():
            for hh in hs:
                st_ref[g * rep + hh] = jnp.zeros((P, N), F32)

        Bv, Cv = b_ref[...], c_ref[...]
        tril = _tri(L, False, False)
        dtc = [_head_col(dt_ref, g * rep + hh) for hh in hs]
        csc = [_head_col(cs_ref, g * rep + hh) for hh in hs]
        csr = [csT_ref[hh:hh + 1, :] for hh in hs]
        last = [r[:, L - 1:L] for r in csr]
        xc = [x_ref[hh] * dtc[hh] for hh in hs]
        cb = _dot(Cv, Bv, 1, 1)
        m = [cb * jnp.where(tril, jnp.exp(jnp.where(tril, csc[hh] - csr[hh], 0.0)), 0.0) for hh in hs]
        prev = [st_ref[g * rep + hh] for hh in hs]
        yd = [_dot(m[hh], xc[hh], 1, 0) for hh in hs]
        yo = [_dot(Cv, prev[hh], 1, 1) for hh in hs]
        new = [_dot(xc[hh] * jnp.exp(last[hh] - csc[hh]), Bv, 0, 0) for hh in hs]
        for hh in hs:
            y_ref[hh] = yd[hh] + yo[hh] * jnp.exp(csc[hh])
            prev_ref[hh] = prev[hh]
            st_ref[g * rep + hh] = prev[hh] * jnp.exp(last[hh]) + new[hh]

    tok = pl.BlockSpec((L, H), lambda c, g: (c, 0))
    return pl.pallas_call(
        body, grid=(nc, SSD_GROUPS),
        in_specs=[pl.BlockSpec((rep, L, P), lambda c, g: (g, c, 0)), pl.BlockSpec((None, L, N), lambda c, g: (g, c, 0)),
                  pl.BlockSpec((None, L, N), lambda c, g: (g, c, 0)), tok, tok, pl.BlockSpec((rep, L), lambda c, g: (g, c))],
        out_specs=[pl.BlockSpec((rep, L, P), lambda c, g: (g, c, 0)),
                   pl.BlockSpec((rep, None, P, N), lambda c, g: (g, c, 0, 0))],
        out_shape=[jax.ShapeDtypeStruct((H, S, P), F32), jax.ShapeDtypeStruct((H, nc, P, N), F32)],
        scratch_shapes=[pltpu.VMEM((H, P, N), F32)],
        compiler_params=_cparams("arbitrary", "arbitrary"), name=name)(xs, Bm, Cm, dt, cs, csT)


def _ssdg_bwd(xs, Bm, Cm, dt, cs, csT, prev, dy, a_log, d_skip, name):
    H, S, P = xs.shape
    L, N = SSD_CHUNK, SSD_STATE
    nc = S // L
    rep = H // SSD_GROUPS
    hs = range(rep)

    def rowsum(a):
        return jnp.sum(a, axis=1, keepdims=True)

    def body(x_ref, b_ref, c_ref, dt_ref, cs_ref, csT_ref, prev_ref, dy_ref, al_ref, dk_ref,
             dx_ref, db_ref, dc_ref, ddt_ref, da_ref, g_ref):
        ci, g = pl.program_id(0), pl.program_id(1)

        @pl.when(ci == 0)
        def _():
            for hh in hs:
                g_ref[g * rep + hh] = jnp.zeros((P, N), F32)

        @pl.when((ci == 0) & (g == 0))
        def _():
            da_ref[...] = jnp.zeros_like(da_ref)

        @pl.when(g == 0)
        def _():
            ddt_ref[...] = jnp.zeros_like(ddt_ref)

        lane = lax.broadcasted_iota(jnp.int32, (1, H), 1)
        sel = [lane == g * rep + hh for hh in hs]
        A_h = [-jnp.exp(rowsum(jnp.where(s, al_ref[...], 0.0))) for s in sel]
        dsk = [rowsum(jnp.where(s, dk_ref[...], 0.0)) for s in sel]
        dtc = [_head_col(dt_ref, g * rep + hh) for hh in hs]
        csc = [_head_col(cs_ref, g * rep + hh) for hh in hs]
        csr = [csT_ref[hh:hh + 1, :] for hh in hs]
        last = [r[:, L - 1:L] for r in csr]
        Bv, Cv = b_ref[...], c_ref[...]
        xv = [x_ref[hh] for hh in hs]
        xc = [xv[hh] * dtc[hh] for hh in hs]
        dY = [dy_ref[hh] for hh in hs]
        prv = [prev_ref[hh] for hh in hs]
        G = [g_ref[g * rep + hh] for hh in hs]
        ecs = [jnp.exp(v) for v in csc]
        w = [jnp.exp(last[hh] - csc[hh]) for hh in hs]
        cd = [jnp.exp(v) for v in last]
        tril = _tri(L, False, False)
        triu = _tri(L, False, True)
        lam = [jnp.where(tril, jnp.exp(jnp.where(tril, csc[hh] - csr[hh], 0.0)), 0.0) for hh in hs]
        lamT = [jnp.where(triu, jnp.exp(jnp.where(triu, csr[hh] - csc[hh], 0.0)), 0.0) for hh in hs]
        cb = _dot(Cv, Bv, 1, 1)
        bc = _dot(Bv, Cv, 1, 1)
        dM = [_dot(dY[hh], xc[hh], 1, 1) for hh in hs]
        dMT = [_dot(xc[hh], dY[hh], 1, 1) for hh in hs]
        cp = [_dot(Cv, prv[hh], 1, 1) for hh in hs]
        BG = [_dot(Bv, G[hh], 1, 1) for hh in hs]
        dYe = [dY[hh] * ecs[hh] for hh in hs]
        dprev = [_dot(dYe[hh], Cv, 0, 0) for hh in hs]
        m = [cb * lam[hh] for hh in hs]
        mT = [bc * lamT[hh] for hh in hs]
        dxc = [_dot(mT[hh], dY[hh], 1, 0) + w[hh] * BG[hh] for hh in hs]
        dcb = sum([dM[hh] * lam[hh] for hh in hs][1:], dM[0] * lam[0])
        dcbT = sum([dMT[hh] * lamT[hh] for hh in hs][1:], dMT[0] * lamT[0])
        dC = _dot(dcb, Bv, 1, 0)
        dB = _dot(dcbT, Cv, 1, 0)
        for hh in hs:
            dC = dC + _dot(dYe[hh], prv[hh], 1, 0)
            dB = dB + _dot(xc[hh] * w[hh], G[hh], 1, 0)
        dc_ref[...] = dC
        db_ref[...] = dB
        ddt_acc = jnp.zeros((L, H), F32)
        da_acc = jnp.zeros((1, H), F32)
        rev = _tri(L, False, True).astype(F32)
        for hh in hs:
            dww = rowsum(xc[hh] * BG[hh]) * w[hh]
            dcs = (rowsum(dM[hh] * m[hh]) - rowsum(dMT[hh] * mT[hh]) + rowsum(dY[hh] * (cp[hh] * ecs[hh])) - dww)
            extra = jnp.sum(dww, axis=0, keepdims=True) + cd[hh] * jnp.sum(rowsum(G[hh] * prv[hh]), axis=0, keepdims=True)
            g_ref[g * rep + hh] = G[hh] * cd[hh] + dprev[hh]
            da = jnp.dot(rev, dcs, precision=_HI, preferred_element_type=F32) + extra
            dx_ref[hh] = dxc[hh] * dtc[hh] + dY[hh] * dsk[hh]
            ddt_acc = ddt_acc + jnp.where(sel[hh], da * A_h[hh] + rowsum(dxc[hh] * xv[hh]), 0.0)
            da_acc = da_acc + jnp.where(sel[hh], jnp.sum(da * dtc[hh], axis=0, keepdims=True), 0.0)
        ddt_ref[...] += ddt_acc
        da_ref[...] += da_acc

    rc = lambda ci: nc - 1 - ci
    hd = pl.BlockSpec((rep, L, P), lambda ci, g: (g, rc(ci), 0))
    grp = pl.BlockSpec((None, L, N), lambda ci, g: (g, rc(ci), 0))
    tok = pl.BlockSpec((L, H), lambda ci, g: (rc(ci), 0))
    vec = pl.BlockSpec((1, H), lambda ci, g: (0, 0))
    return pl.pallas_call(
        body, grid=(nc, SSD_GROUPS),
        in_specs=[hd, grp, grp, tok, tok, pl.BlockSpec((rep, L), lambda ci, g: (g, rc(ci))),
                  pl.BlockSpec((rep, None, P, N), lambda ci, g: (g, rc(ci), 0, 0)), hd, vec, vec],
        out_specs=[hd, grp, grp, tok, vec],
        out_shape=[jax.ShapeDtypeStruct((H, S, P), F32), jax.ShapeDtypeStruct((SSD_GROUPS, S, N), F32),
                   jax.ShapeDtypeStruct((SSD_GROUPS, S, N), F32), jax.ShapeDtypeStruct((S, H), F32),
                   jax.ShapeDtypeStruct((1, H), F32)],
        scratch_shapes=[pltpu.VMEM((H, P, N), F32)],
        compiler_params=_cparams("arbitrary", "arbitrary"), name=name)(xs, Bm, Cm, dt, cs, csT, prev, dy, a_log, d_skip)


def _dt_bwd(ddt, dA, dtp, dt_bias, a_log, name):
    S, H = ddt.shape
    tm = _rows(S, 512)

    def body(g_ref, da_ref, p_ref, b_ref, al_ref, o_ref, db_ref, dal_ref):
        @pl.when(pl.program_id(0) == 0)
        def _():
            db_ref[...] = jnp.zeros_like(db_ref)
            dal_ref[...] = da_ref[...] * (-jnp.exp(al_ref[...]))

        g = g_ref[...] * _sigmoid(p_ref[:, :H] + b_ref[...])
        db_ref[...] += jnp.sum(g, axis=0, keepdims=True)
        o_ref[...] = jnp.zeros_like(o_ref)
        o_ref[:, :H] = g.astype(o_ref.dtype)

    vec = pl.BlockSpec((1, H), lambda i: (0, 0))
    return pl.pallas_call(
        body, grid=(S // tm,),
        in_specs=[pl.BlockSpec((tm, H), lambda i: (i, 0)), vec, pl.BlockSpec((tm, DT_PAD), lambda i: (i, 0)), vec, vec],
        out_specs=[pl.BlockSpec((tm, DT_PAD), lambda i: (i, 0)), vec, vec],
        out_shape=[jax.ShapeDtypeStruct((S, DT_PAD), _MXU), jax.ShapeDtypeStruct((1, H), F32), jax.ShapeDtypeStruct((1, H), F32)],
        compiler_params=_cparams("arbitrary"), name=name)(ddt, dA, dtp, dt_bias, a_log)


def _ssd_gate_fwd(y, act, z, dskip, w, name):
    S, D = y.shape
    tm = _rows(S, 256)
    Gw = D // SSD_GROUPS

    def body(y_ref, x_ref, z_ref, k_ref, w_ref, o_ref):
        zv = z_ref[...]
        y2 = (y_ref[...] + x_ref[...] * k_ref[...]) * (zv * _sigmoid(zv))
        for g in range(SSD_GROUPS):
            sl = slice(g * Gw, (g + 1) * Gw)
            v = y2[:, sl]
            r = lax.rsqrt(jnp.mean(v * v, axis=-1, keepdims=True) + EPS)
            o_ref[:, sl] = ((v * r) * w_ref[:, sl]).astype(o_ref.dtype)

    row = pl.BlockSpec((tm, D), lambda i: (i, 0))
    vec = pl.BlockSpec((1, D), lambda i: (0, 0))
    return pl.pallas_call(body, grid=(S // tm,), in_specs=[row, row, row, vec, vec], out_specs=row,
                          out_shape=jax.ShapeDtypeStruct((S, D), _MXU), compiler_params=_cparams("parallel"),
                          name=name)(y, act, z, dskip, w)


def _ssd_gate_bwd(dyn, y, act, z, dskip, w, name):
    S, D = y.shape
    tm = _rows(S, 256)
    Gw = D // SSD_GROUPS

    def body(g_ref, y_ref, x_ref, z_ref, k_ref, w_ref, dy_ref, dz_ref, dk_ref, dw_ref):
        @pl.when(pl.program_id(0) == 0)
        def _():
            dk_ref[...] = jnp.zeros_like(dk_ref)
            dw_ref[...] = jnp.zeros_like(dw_ref)

        zv = z_ref[...]
        xv = x_ref[...]
        s = _sigmoid(zv)
        sz = zv * s
        y1 = y_ref[...] + xv * k_ref[...]
        y2 = y1 * sz
        for g in range(SSD_GROUPS):
            sl = slice(g * Gw, (g + 1) * Gw)
            v = y2[:, sl]
            r = lax.rsqrt(jnp.mean(v * v, axis=-1, keepdims=True) + EPS)
            vn = v * r
            gy = g_ref[:, sl].astype(F32)
            dw_ref[:, sl] += jnp.sum(gy * vn, axis=0, keepdims=True)
            dvn = gy * w_ref[:, sl]
            dy2 = r * (dvn - vn * jnp.mean(dvn * vn, axis=-1, keepdims=True))
            dy1 = dy2 * sz[:, sl]
            dy_ref[:, sl] = dy1
            dz_ref[:, sl] = (dy2 * y1[:, sl] * (s[:, sl] * (1.0 + zv[:, sl] * (1.0 - s[:, sl])))).astype(dz_ref.dtype)
            dk_ref[:, sl] += jnp.sum(dy1 * xv[:, sl], axis=0, keepdims=True)

    row = pl.BlockSpec((tm, D), lambda i: (i, 0))
    vec = pl.BlockSpec((1, D), lambda i: (0, 0))
    return pl.pallas_call(
        body, grid=(S // tm,), in_specs=[row, row, row, row, vec, vec], out_specs=[row, row, vec, vec],
        out_shape=[jax.ShapeDtypeStruct((S, D), F32), jax.ShapeDtypeStruct((S, D), _MXU),
                   jax.ShapeDtypeStruct((1, D), F32), jax.ShapeDtypeStruct((1, D), F32)],
        compiler_params=_cparams("arbitrary"), name=name)(dyn, y, act, z, dskip, w)


def _split_dot(v, u):
    hi = v.astype(_MXU)
    lo = (v - hi.astype(F32)).astype(_MXU)
    dn = (((1,), (0,)), ((), ()))
    return (lax.dot_general(hi, u, dn, preferred_element_type=F32) + lax.dot_general(lo, u, dn, preferred_element_type=F32))


def _sb_tiles(S):
    return _pick(S, 256, 128)


SB_LANES = 128
SB_PACK = SB_LANES // SB_HEAD_DIM
SB_ROWS = 128
SB_SCALE = 1.0 / math.sqrt(SB_HEAD_DIM)


def _head_masks():
    lane = lax.broadcasted_iota(jnp.int32, (1, SB_LANES), 1)
    return [(lane // SB_HEAD_DIM) == hh for hh in range(SB_PACK)]


def _by_head(hm, vals):
    out = vals[-1]
    for hh in range(SB_PACK - 2, -1, -1):
        out = jnp.where(hm[hh], vals[hh], out)
    return out


SB_DEAD = -110.0


def _sb_alive(Rs):
    m = Rs[0]
    for R in Rs[1:]:
        m = jnp.maximum(m, R)
    return jnp.max(m) > SB_DEAD


def _sb_rows(a, r):
    return a[r * SB_ROWS:(r + 1) * SB_ROWS]


def _sb_assemble(hm, vals):
    nr = len(vals) // SB_PACK
    return jnp.concatenate([_by_head(hm, vals[r * SB_PACK:(r + 1) * SB_PACK]) for r in range(nr)], axis=0)


def _sb_scores(zs, U, Rs, masks):
    ls = [-jnp.maximum(z, 0.0) - jnp.log(1.0 + jnp.exp(-jnp.abs(z))) for z in zs]
    if masks is not None:
        ls = [jnp.where(m, l, 0.0) for m, l in zip(masks, ls)]
    Es = [lax.dot_general(l.astype(_MXU), U, (((1,), (0,)), ((), ())), preferred_element_type=F32) for l in ls]
    As = [jnp.exp(l + z + (E + R)) for l, z, E, R in zip(ls, zs, Es, Rs)]
    if masks is not None:
        As = [jnp.where(m, A, 0.0) for m, A in zip(masks, As)]
    return ls, [A.astype(_MXU) for A in As]


SB_GROUP = 2


def _sbg_chains(T):
    return [(b, r, hh) for b in range(SB_GROUP) for r in range(T // SB_ROWS) for hh in range(SB_PACK)]


def _lanes(a, b):
    return a[:, b * SB_LANES:(b + 1) * SB_LANES]


def _sbg_join(hm, vals):
    per = len(vals) // SB_GROUP
    return jnp.concatenate([_sb_assemble(hm, vals[b * per:(b + 1) * per]) for b in range(SB_GROUP)], axis=1)


def _sbg_head_sum(hm, a):
    return jnp.concatenate([_by_head(hm, [jnp.sum(jnp.where(m, _lanes(a, b), 0.0), axis=1, keepdims=True) for m in hm])
                            for b in range(SB_GROUP)], axis=1)


def _sbg_fwd(q_arr, k_arr, v_arr, cols, w, name, gather=()):
    S = q_arr.shape[0]
    T = _sb_tiles(S)
    cq, ck, cv = cols
    GW = SB_GROUP * SB_LANES
    nb = SB_WIDTH // GW
    ng = len(gather)
    send, finish = _gather_phases([g.shape[0] // 2 for g in gather])

    def body(q_ref, k_ref, v_ref, w_ref, *rest):
        o_ref, y_ref = rest[ng:ng + 2]
        comm = (rest[:ng], rest[ng + 2:2 * ng + 2], rest[2 * ng + 2:])
        i = pl.program_id(1)
        if ng:
            @pl.when((pl.program_id(0) == 0) & (i == 0))
            def _():
                send(*comm)
        hm = _head_masks()
        qs = q_ref[...] * SB_SCALE
        chains = _sbg_chains(T)
        qcs = [_sb_rows(jnp.where(hm[hh], _lanes(qs, b), jnp.zeros((T, SB_LANES), qs.dtype)), r) for b, r, hh in chains]
        U = _tri(T, True, False).astype(_MXU)

        def scores_of(j):
            kj = k_ref[pl.ds(pl.multiple_of(j * T, T), T), :]
            return [_dot(qc, _lanes(kj, b), 1, 1) for qc, (b, _, _) in zip(qcs, chains)]

        def weighted(Abs, j):
            vj = v_ref[pl.ds(pl.multiple_of(j * T, T), T), :]
            return _sbg_join(hm, [_dot(Ab, _lanes(vj, b), 1, 0) for Ab, (b, _, _) in zip(Abs, chains)])

        def step(carry):
            jj, acc, Rs, Aprev = carry
            j = i - 1 - jj
            zs = scores_of(j)
            acc = acc + weighted(Aprev, j + 1)
            ls, Abs = _sb_scores(zs, U, Rs, None)
            return jj + 1, acc, tuple(R + jnp.sum(l, axis=1, keepdims=True) for R, l in zip(Rs, ls)), tuple(Abs)

        masks = [_sb_rows(_tri(T, True, False), r) for _, r, _ in chains]
        zero = jnp.zeros((SB_ROWS, 1), F32)
        ls, Abs = _sb_scores(scores_of(i), U, (zero,) * len(chains), masks)
        carry = (jnp.int32(0), jnp.zeros((T, GW), F32), tuple(jnp.sum(l, axis=1, keepdims=True) for l in ls), tuple(Abs))
        jj, acc, _, Alast = lax.while_loop(lambda c: (c[0] < i) & _sb_alive(c[2]), step, carry)
        acc = acc + weighted(Alast, i - jj)
        o_ref[...] = acc
        r = lax.rsqrt(_sbg_head_sum(hm, acc * acc) * (1.0 / SB_HEAD_DIM) + EPS)
        y_ref[...] = ((acc * r) * w_ref[...]).astype(y_ref.dtype)
        if ng:
            @pl.when((pl.program_id(0) == nb - 1) & (i == S // T - 1))
            def _():
                finish(*comm)

    blk = pl.BlockSpec((T, GW), lambda h, i: (i, h))
    hbm = pl.BlockSpec(memory_space=pl.ANY)
    return pl.pallas_call(
        body, grid=(nb, S // T),
        in_specs=[pl.BlockSpec((T, GW), lambda h, i: (i, cq + h)), pl.BlockSpec((S, GW), lambda h, i: (0, ck + h), pipeline_mode=pl.Buffered(1)),
                  pl.BlockSpec((S, GW), lambda h, i: (0, cv + h), pipeline_mode=pl.Buffered(1)), pl.BlockSpec((1, GW), lambda h, i: (0, h))]
                 + [hbm] * ng,
        out_specs=[blk, blk] + [hbm] * ng,
        out_shape=[jax.ShapeDtypeStruct((S, SB_WIDTH), F32), jax.ShapeDtypeStruct((S, SB_WIDTH), _MXU)]
                  + [jax.ShapeDtypeStruct((4,) + tuple(g.shape), g.dtype) for g in gather],
        scratch_shapes=_gather_sems(ng) if ng else [],
        compiler_params=_cparams("arbitrary", "arbitrary") if ng else _cparams("parallel", "parallel"), name=name)(q_arr, k_arr, v_arr, w, *gather)


def _sbg_bwd(q_arr, k_arr, v_arr, cols, o, dy_arr, cdy, w, name, swap=()):
    S = q_arr.shape[0]
    T = _sb_tiles(S)
    cq, ck, cv = cols
    GW = SB_GROUP * SB_LANES
    nb = SB_WIDTH // GW
    ns = len(swap)
    send, finish = _swap_phases(ns)

    def body(q_ref, k_ref, v_ref, o_ref, dy_ref, w_ref, *rest):
        dq_ref, dk_ref, dv_ref, dw_ref = rest[ns:ns + 4]
        comm = (rest[:ns], rest[ns + 4:2 * ns + 4], rest[2 * ns + 4:])
        i = pl.program_id(1)
        if ns:
            @pl.when((pl.program_id(0) == 0) & (i == 0))
            def _():
                send(*comm)

        @pl.when(i == 0)
        def _():
            dk_ref[...] = jnp.zeros_like(dk_ref)
            dv_ref[...] = jnp.zeros_like(dv_ref)
            dw_ref[...] = jnp.zeros_like(dw_ref)

        hm = _head_masks()
        chains = _sbg_chains(T)
        qs = q_ref[...] * SB_SCALE
        ov = o_ref[...]
        gy = dy_ref[...]
        r = lax.rsqrt(_sbg_head_sum(hm, ov * ov) * (1.0 / SB_HEAD_DIM) + EPS)
        on = ov * r
        dw_ref[...] += jnp.sum(gy * on, axis=0, keepdims=True)
        don = gy * w_ref[...]
        do = r * (don - on * (_sbg_head_sum(hm, don * on) * (1.0 / SB_HEAD_DIM)))
        dob = do.astype(_MXU)
        dprod = dob.astype(F32) * ov
        zt = jnp.zeros((T, SB_LANES), dob.dtype)
        qm = [[jnp.where(hm[hh], _lanes(qs, b), zt) for hh in range(SB_PACK)] for b in range(SB_GROUP)]
        dm = [[jnp.where(hm[hh], _lanes(dob, b), zt) for hh in range(SB_PACK)] for b in range(SB_GROUP)]
        qcs = [_sb_rows(qm[b][hh], r_) for b, r_, hh in chains]
        doc = [_sb_rows(dm[b][hh], r_) for b, r_, hh in chains]
        Dt = [_sb_rows(jnp.sum(jnp.where(hm[hh], _lanes(dprod, b), 0.0), axis=1, keepdims=True), r_) for b, r_, hh in chains]
        U = _tri(T, True, False).astype(_MXU)
        Ui = _tri(T, False, False).astype(_MXU)

        def products_of(j):
            off = pl.multiple_of(j * T, T)
            kj = k_ref[pl.ds(off, T), :]
            vj = v_ref[pl.ds(off, T), :]
            return ([_dot(qc, _lanes(kj, b), 1, 1) for qc, (b, _, _) in zip(qcs, chains)],
                    [_dot(d, _lanes(vj, b), 1, 1) for d, (b, _, _) in zip(doc, chains)])

        def core(zs, dAs, Rs, Qs, masks):
            ls, Abs = _sb_scores(zs, U, Rs, masks)
            Gs = [dA * Ab.astype(F32) for dA, Ab in zip(dAs, Abs)]
            sfx = [_split_dot(G, Ui) for G in Gs]
            dzs = []
            for c, (l, G, s, D, Q) in enumerate(zip(ls, Gs, sfx, Dt, Qs)):
                P = D - (s + Q)
                dz = jnp.exp(l) * (G + P) - P
                if masks is not None:
                    dz = jnp.where(masks[c], dz, 0.0)
                dzs.append(dz.astype(_MXU))
            newR = tuple(R + jnp.sum(l, axis=1, keepdims=True) for R, l in zip(Rs, ls))
            newQ = tuple(Q + jnp.sum(G, axis=1, keepdims=True) for Q, G in zip(Qs, Gs))
            return tuple(Abs), tuple(dzs), newR, newQ

        def over_rows(vals, other):
            nr = T // SB_ROWS
            tiles = []
            for b in range(SB_GROUP):
                acc = None
                for hh in range(SB_PACK):
                    rows = jnp.concatenate([vals[(b * nr + r_) * SB_PACK + hh] for r_ in range(nr)], axis=0)
                    part = _dot(rows, other[b][hh], 0, 0)
                    acc = part if acc is None else acc + part
                tiles.append(acc)
            return jnp.concatenate(tiles, axis=1)

        def emit(Abs, dzs, j):
            off = pl.multiple_of(j * T, T)
            kj = k_ref[pl.ds(off, T), :]
            dk_ref[pl.ds(off, T), :] += over_rows(dzs, qm)
            dv_ref[pl.ds(off, T), :] += over_rows(Abs, dm)
            return _sbg_join(hm, [_dot(dzb, _lanes(kj, b), 1, 0) for dzb, (b, _, _) in zip(dzs, chains)])

        def step(carry):
            jj, dq, Rs, Qs, Aprev, dzprev = carry
            j = i - 1 - jj
            zs, dAs = products_of(j)
            dq = dq + emit(Aprev, dzprev, j + 1)
            Abs, dzs, Rs, Qs = core(zs, dAs, Rs, Qs, None)
            return jj + 1, dq, Rs, Qs, Abs, dzs

        masks = [_sb_rows(_tri(T, True, False), r_) for _, r_, _ in chains]
        zero = (jnp.zeros((SB_ROWS, 1), F32),) * len(chains)
        zs, dAs = products_of(i)
        Abs, dzs, Rs, Qs = core(zs, dAs, zero, zero, masks)
        jj, dq, _, _, Alast, dzlast = lax.while_loop(lambda c: (c[0] < i) & _sb_alive(c[2]), step,
                                                     (jnp.int32(0), jnp.zeros((T, GW), F32), Rs, Qs, Abs, dzs))
        dq = dq + emit(Alast, dzlast, i - jj)
        dq_ref[...] = (dq * SB_SCALE).astype(dq_ref.dtype)
        if ns:
            @pl.when((pl.program_id(0) == nb - 1) & (i == S // T - 1))
            def _():
                finish(*comm)

    blk = pl.BlockSpec((T, GW), lambda h, i: (i, h))
    full = pl.BlockSpec((S, GW), lambda h, i: (0, h), pipeline_mode=pl.Buffered(1))
    wsp = pl.BlockSpec((1, GW), lambda h, i: (0, h))
    hbm = pl.BlockSpec(memory_space=pl.ANY)
    return pl.pallas_call(
        body, grid=(nb, S // T),
        in_specs=[pl.BlockSpec((T, GW), lambda h, i: (i, cq + h)), pl.BlockSpec((S, GW), lambda h, i: (0, ck + h), pipeline_mode=pl.Buffered(1)),
                  pl.BlockSpec((S, GW), lambda h, i: (0, cv + h), pipeline_mode=pl.Buffered(1)), blk,
                  pl.BlockSpec((T, GW), lambda h, i: (i, cdy + h)), wsp] + [hbm] * ns,
        out_specs=[blk, full, full, wsp] + [hbm] * ns,
        out_shape=[jax.ShapeDtypeStruct((S, SB_WIDTH), _MXU), jax.ShapeDtypeStruct((S, SB_WIDTH), F32),
                   jax.ShapeDtypeStruct((S, SB_WIDTH), F32), jax.ShapeDtypeStruct((1, SB_WIDTH), F32)]
                  + [jax.ShapeDtypeStruct((3,) + tuple(a.shape[1:]), a.dtype) for a in swap],
        scratch_shapes=[pltpu.SemaphoreType.DMA((3 * ns,))] * 2 if ns else [],
        compiler_params=_cparams("arbitrary", "arbitrary") if ns else _cparams("parallel", "arbitrary"),
        name=name)(q_arr, k_arr, v_arr, o, dy_arr, w, *swap)


def _adamw(w, g, m, v, name):
    R, C = w.shape
    tm = _rows(R, 256) if R % 8 == 0 else R
    c1 = 1.0 - ADAM_B1 ** ADAM_STEP
    c2 = 1.0 - ADAM_B2 ** ADAM_STEP

    def body(w_ref, g_ref, m_ref, v_ref, d_ref, nm_ref, nv_ref):
        gv = g_ref[...]
        mn = ADAM_B1 * m_ref[...] + (1.0 - ADAM_B1) * gv
        vn = ADAM_B2 * v_ref[...] + (1.0 - ADAM_B2) * (gv * gv)
        d_ref[...] = -ADAM_LR * ((mn / c1) / (jnp.sqrt(vn / c2) + ADAM_EPS) + ADAM_WD * w_ref[...])
        nm_ref[...] = mn
        nv_ref[...] = vn

    blk = pl.BlockSpec((tm, C), lambda i: (i, 0))
    return pl.pallas_call(body, grid=(R // tm,), in_specs=[blk] * 4, out_specs=[blk] * 3,
                          out_shape=[jax.ShapeDtypeStruct((R, C), F32)] * 3, compiler_params=_cparams("parallel"),
                          name=name)(w, g, m, v)


def _sum_lead(a, name, first=None, pick=None, wire=False):
    n, R, C = a.shape
    tm = _rows(R, 256)
    nin = 1 if first is None else 2

    def body(*refs):
        refs = refs[nin - 1:]
        a_ref = refs[nin - 1]
        s = a_ref[0].astype(F32) if first is None else refs[0][...] + a_ref[0]
        for p in range(1, n):
            s = s + a_ref[p]
        for o_ref in refs[nin:]:
            o_ref[...] = s.astype(o_ref.dtype)

    outs = [jax.ShapeDtypeStruct((R, C), F32)] + ([jax.ShapeDtypeStruct((R, C), _WIRE)] if wire else [])
    if first is None:
        row = pl.BlockSpec((tm, C), lambda i: (i, 0))
        res = pl.pallas_call(body, grid=(R // tm,), in_specs=[pl.BlockSpec((n, tm, C), lambda i: (0, i, 0))],
                             out_specs=[row] * len(outs), out_shape=outs, compiler_params=_cparams("parallel"), name=name)(a)
    else:
        row = pl.BlockSpec((tm, C), lambda i, p: (i, 0))
        grid_spec = pltpu.PrefetchScalarGridSpec(
            num_scalar_prefetch=1, grid=(R // tm,),
            in_specs=[pl.BlockSpec((None, tm, C), lambda i, p: (p[0], i, 0)), pl.BlockSpec((n, tm, C), lambda i, p: (0, i, 0))],
            out_specs=[row] * len(outs))
        res = pl.pallas_call(body, grid_spec=grid_spec, out_shape=outs, compiler_params=_cparams("parallel"), name=name)(pick, first, a)
    return res if wire else res[0]


_GROUP_BITS = {'c': ((0, 0, 1),), 'xy': ((0, 1, 0), (1, 0, 0), (1, 1, 0)),
               'xyc': tuple((k >> 2 & 1, k >> 1 & 1, k & 1) for k in range(1, 8))}


def _exchange(srcs, *, group, same_src, own, chunks, name):
    flips = _GROUP_BITS[group]
    n = len(flips) + 1
    na = len(srcs)
    blk_shapes = [tuple(s.shape) if same_src else tuple(s.shape[1:]) for s in srcs]
    assert all(bs[0] % chunks == 0 for bs in blk_shapes), blk_shapes

    def body(*refs):
        src_refs, dst_refs = refs[:na], refs[na:2 * na]
        send_sems, recv_sems, loc_sems = refs[2 * na:]
        x, y, c = lax.axis_index("x"), lax.axis_index("y"), lax.axis_index("c")

        def member(px, py, pc):
            return {'c': pc, 'xy': 2 * px + py, 'xyc': 4 * px + 2 * py + pc}[group]

        def piece(ref, a, q):
            rows = blk_shapes[a][0] // chunks
            return ref.at[pl.ds(q * rows, rows)]

        me = member(x, y, c)
        started, arrivals = [], []
        for a in range(na):
            mine = src_refs[a] if same_src else src_refs[a].at[me]
            if own:
                for q in range(chunks):
                    cp = pltpu.make_async_copy(piece(mine, a, q), piece(dst_refs[a].at[me], a, q), loc_sems.at[a * chunks + q])
                    cp.start()
                    started.append(cp.wait)
            for kk, (fx, fy, fc) in enumerate(flips):
                px, py, pc = (1 - x if fx else x), (1 - y if fy else y), (1 - c if fc else c)
                peer = member(px, py, pc)
                out_blk = src_refs[a] if same_src else src_refs[a].at[peer]
                there = dst_refs[a].at[me if own else kk]
                here = dst_refs[a].at[peer if own else kk]
                for q in range(chunks):
                    s = (a * (n - 1) + kk) * chunks + q
                    out = pltpu.make_async_remote_copy(
                        src_ref=piece(out_blk, a, q), dst_ref=piece(there, a, q), send_sem=send_sems.at[s],
                        recv_sem=recv_sems.at[s], device_id=(px, py, pc), device_id_type=pl.DeviceIdType.MESH)
                    out.start()
                    started.append(out.wait_send)
                    arrivals.append(pltpu.make_async_remote_copy(
                        src_ref=piece(mine, a, q), dst_ref=piece(here, a, q), send_sem=send_sems.at[s],
                        recv_sem=recv_sems.at[s], device_id=(px, py, pc), device_id_type=pl.DeviceIdType.MESH).wait_recv)
        for wait in arrivals + started:
            wait()

    nsem = na * (n - 1) * chunks
    hbm = pl.BlockSpec(memory_space=pl.ANY)
    return pl.pallas_call(
        body, in_specs=[hbm] * na, out_specs=[hbm] * na,
        out_shape=[jax.ShapeDtypeStruct(((n if own else n - 1),) + bs, s.dtype) for bs, s in zip(blk_shapes, srcs)],
        scratch_shapes=[pltpu.SemaphoreType.DMA((nsem,)), pltpu.SemaphoreType.DMA((nsem,)),
                        pltpu.SemaphoreType.DMA((na * chunks,))],
        compiler_params=pltpu.CompilerParams(has_side_effects=True), name=name)(*srcs)


def _gather_phases(halves):
    flips = _GROUP_BITS['xy']
    nf = len(flips)
    na = len(halves)

    def copies(src_refs, dst_refs, sems):
        send_sems, recv_sems, fsend_sems, frecv_sems = sems
        x, y, c = lax.axis_index("x"), lax.axis_index("y"), lax.axis_index("c")

        def half(ref, a, which):
            return ref.at[pl.ds(pl.multiple_of(which * halves[a], 8), halves[a])]

        def copy(src, dst, pair, s, to):
            return pltpu.make_async_remote_copy(src_ref=src, dst_ref=dst, send_sem=pair[0].at[s], recv_sem=pair[1].at[s],
                                                device_id=to, device_id_type=pl.DeviceIdType.MESH)

        out = []
        for a in range(na):
            for kk, (fx, fy, _) in enumerate(flips):
                peer = ((1 - x if fx else x), (1 - y if fy else y), c)
                there = dst_refs[a].at[2 * peer[0] + peer[1]]
                s = a * nf + kk
                ici, d2d = (send_sems, recv_sems), (fsend_sems, frecv_sems)
                out.append((copy(half(src_refs[a], a, c), half(dst_refs[a].at[2 * x + y], a, c), ici, s, peer),
                            copy(half(src_refs[a], a, c), half(there, a, c), ici, s, (x, y, c)),
                            copy(half(there, a, c), half(there, a, c), d2d, s, (x, y, 1 - c)),
                            copy(half(there, a, 1 - c), half(there, a, 1 - c), d2d, s, (x, y, 1 - c))))
        return out

    def send(src_refs, dst_refs, sems):
        for first, _, _, _ in copies(src_refs, dst_refs, sems):
            first.start()

    def finish(src_refs, dst_refs, sems):
        cs = copies(src_refs, dst_refs, sems)
        for _, landed, onward, _ in cs:
            landed.wait_recv()
            onward.start()
        for _, _, _, passed in cs:
            passed.wait_recv()
        for first, _, onward, _ in cs:
            first.wait_send()
            onward.wait_send()

    return send, finish


def _swap_phases(na):
    flips = _GROUP_BITS['xy']

    def copies(src_refs, dst_refs, sems):
        x, y, c = lax.axis_index("x"), lax.axis_index("y"), lax.axis_index("c")
        out = []
        for a in range(na):
            for kk, (fx, fy, _) in enumerate(flips):
                px, py = (1 - x if fx else x), (1 - y if fy else y)
                s = a * len(flips) + kk
                out.append(pltpu.make_async_remote_copy(
                    src_ref=src_refs[a].at[2 * px + py], dst_ref=dst_refs[a].at[kk], send_sem=sems[0].at[s], recv_sem=sems[1].at[s],
                    device_id=(px, py, c), device_id_type=pl.DeviceIdType.MESH))
        return out

    def send(src_refs, dst_refs, sems):
        for cp in copies(src_refs, dst_refs, sems):
            cp.start()

    def finish(src_refs, dst_refs, sems):
        cs = copies(src_refs, dst_refs, sems)
        for cp in cs:
            cp.wait_recv()
        for cp in cs:
            cp.wait_send()

    return send, finish


def _gather_sems(na):
    return [pltpu.SemaphoreType.DMA((na * len(_GROUP_BITS['xy']),))] * 4


def _to_shards(name, full):
    R, C = full.shape
    if name in COL_SPLIT:
        return full.reshape(R, 4, C // 4).transpose(1, 0, 2)
    return full.reshape(4, R // 4, C)


def _from_shards(name, sh):
    n, R, C = sh.shape
    if name in COL_SPLIT:
        return sh.transpose(1, 0, 2).reshape(R, n * C)
    return sh.reshape(n * R, C)


def _pack_rows(parts, width, rows):
    n = parts[0].shape[0]
    flat = jnp.concatenate([p.reshape(n, -1) for p in parts], axis=1)
    return jnp.pad(flat, ((0, 0), (0, rows * width - flat.shape[1]))).reshape(n, rows, width)


def _unpack_rows(buf, shapes):
    n = buf.shape[0]
    flat = buf.reshape(n, -1)
    out, o = [], 0
    for s in shapes:
        sz = math.prod(s)
        out.append(flat[:, o:o + sz].reshape((n,) + tuple(s)))
        o += sz
    return out


def _split_rows(a, rows):
    out, o = [], 0
    for r in rows:
        out.append(a[:, o:o + r])
        o += r
    return out


def _ceil_to(v, m):
    return -(-v // m) * m


def kernel(x, mem, norm_mix_w, w_in, conv_ssd_w, conv_ssd_b, dt_bias, a_log, d_skip, ssd_norm_w, sb_norm_w, w_out, norm_mem_w, norm_memkv_w, w_mq, w_mk, w_mv, w_mo, norm_ffn_w, w_up, conv_ffn_w, conv_ffn_b, w_down, norm_final_w, loss_target, m_norm_mix_w, m_w_in, m_conv_ssd_w, m_conv_ssd_b, m_dt_bias, m_a_log, m_d_skip, m_ssd_norm_w, m_sb_norm_w, m_w_out, m_norm_mem_w, m_norm_memkv_w, m_w_mq, m_w_mk, m_w_mv, m_w_mo, m_norm_ffn_w, m_w_up, m_conv_ffn_w, m_conv_ffn_b, m_w_down, m_norm_final_w, v_norm_mix_w, v_w_in, v_conv_ssd_w, v_conv_ssd_b, v_dt_bias, v_a_log, v_d_skip, v_ssd_norm_w, v_sb_norm_w, v_w_out, v_norm_mem_w, v_norm_memkv_w, v_w_mq, v_w_mk, v_w_mv, v_w_mo, v_norm_ffn_w, v_w_up, v_conv_ffn_w, v_conv_ffn_b, v_w_down, v_norm_final_w):
    W = dict(norm_mix_w=norm_mix_w, w_in=w_in, conv_ssd_w=conv_ssd_w, conv_ssd_b=conv_ssd_b, dt_bias=dt_bias, a_log=a_log,
             d_skip=d_skip, ssd_norm_w=ssd_norm_w, sb_norm_w=sb_norm_w, w_out=w_out, norm_mem_w=norm_mem_w,
             norm_memkv_w=norm_memkv_w, w_mq=w_mq, w_mk=w_mk, w_mv=w_mv, w_mo=w_mo, norm_ffn_w=norm_ffn_w, w_up=w_up,
             conv_ffn_w=conv_ffn_w, conv_ffn_b=conv_ffn_b, w_down=w_down, norm_final_w=norm_final_w)
    Mo = dict(norm_mix_w=m_norm_mix_w, w_in=m_w_in, conv_ssd_w=m_conv_ssd_w, conv_ssd_b=m_conv_ssd_b, dt_bias=m_dt_bias,
              a_log=m_a_log, d_skip=m_d_skip, ssd_norm_w=m_ssd_norm_w, sb_norm_w=m_sb_norm_w, w_out=m_w_out,
              norm_mem_w=m_norm_mem_w, norm_memkv_w=m_norm_memkv_w, w_mq=m_w_mq, w_mk=m_w_mk, w_mv=m_w_mv, w_mo=m_w_mo,
              norm_ffn_w=m_norm_ffn_w, w_up=m_w_up, conv_ffn_w=m_conv_ffn_w, conv_ffn_b=m_conv_ffn_b, w_down=m_w_down,
              norm_final_w=m_norm_final_w)
    Vo = dict(norm_mix_w=v_norm_mix_w, w_in=v_w_in, conv_ssd_w=v_conv_ssd_w, conv_ssd_b=v_conv_ssd_b, dt_bias=v_dt_bias,
              a_log=v_a_log, d_skip=v_d_skip, ssd_norm_w=v_ssd_norm_w, sb_norm_w=v_sb_norm_w, w_out=v_w_out,
              norm_mem_w=v_norm_mem_w, norm_memkv_w=v_norm_memkv_w, w_mq=v_w_mq, w_mk=v_w_mk, w_mv=v_w_mv, w_mo=v_w_mo,
              norm_ffn_w=v_norm_ffn_w, w_up=v_w_up, conv_ffn_w=v_conv_ffn_w, conv_ffn_b=v_conv_ffn_b, w_down=v_w_down,
              norm_final_w=v_norm_final_w)
    shapes = {n: W[n].shape for n in WEIGHTS}
    sh2 = {n: (1, a.shape[-1]) if a.ndim < 3 else a.shape[-2:] for n, a in W.items()}
    w2 = {n: W[n].reshape(sh2[n]) for n in WEIGHTS}
    x2d = x[0]
    S, D = x2d.shape
    H, P, N = SSD_HEADS, SSD_HEAD_DIM, SSD_STATE

    cv_rows = _ceil_to(-(-sum(math.prod(sh2[n]) for n in CONVW) // 128), 32)
    cpack = _pack_rows([w2[n][None] for n in CONVW], 128, cv_rows)[0]
    stacked = jnp.concatenate([w2[n].astype(_MXU) for n in ROW_SPLIT], axis=0)
    cidx = lax.axis_index("c")
    oidx = 2 * lax.axis_index("x") + lax.axis_index("y")
    now, later = [w2['w_in'].astype(_MXU), cpack], [stacked, w2['w_up'].astype(_MXU)]
    h1, *others = _rms_fwd(x2d, w2['norm_mix_w'], "norm_mix", gather=now)
    g_in, call = [lax.dynamic_update_index_in_dim(g, m, oidx, 0) for m, g in zip(now, others)]
    full = {'w_in': _from_shards('w_in', g_in)}
    full.update({n: _from_shards(n, a) for n, a in zip(CONVW, _unpack_rows(call, [sh2[n] for n in CONVW]))})

    o1 = SSD_INNER
    o2 = o1 + SSD_XBC
    o3 = o2 + SSD_HEADS
    Wi = full['w_in']
    W_z, W_xbc, W_qkv = Wi[:, :o1], Wi[:, o1:o2], Wi[:, o3:]
    W_dt = jnp.pad(Wi[:, o2:o3], ((0, 0), (0, DT_PAD - SSD_HEADS)))
    W_in_r = jnp.concatenate([W_z, W_xbc, W_qkv, W_dt], axis=1)
    dskip_rep = jnp.repeat(w2['d_skip'], P, axis=1)

    z = _mm(h1, W_z, name="proj_z")
    xbc = _mm(h1, W_xbc, name="proj_xbc")
    dtp = _mm(h1, W_dt, name="proj_dt")
    qkv = _mm(h1, W_qkv, out_dtype=_MXU, name="proj_qkv")
    act = _conv_silu_fwd(xbc, full['conv_ssd_w'], w2['conv_ssd_b'], "ssd_conv_silu")
    dt, cs = _ssd_prep(dtp, w2['dt_bias'], w2['a_log'], "ssd_prep")
    csT = cs.T
    def heads(a, nh):
        return a.reshape(S, nh, a.shape[1] // nh).transpose(1, 0, 2)

    def unheads(a):
        return a.transpose(1, 0, 2).reshape(S, a.shape[0] * a.shape[2])

    xs_h = heads(act[:, :o1], H)
    Bm = heads(act[:, o1:o1 + SSD_GROUPS * N], SSD_GROUPS)
    Cm = heads(act[:, o1 + SSD_GROUPS * N:], SSD_GROUPS)
    y_h, prev = _ssdg_fwd(xs_h, Bm, Cm, dt, cs, csT, "ssd_scan")
    y_scan = unheads(y_h)
    y_ssd = _ssd_gate_fwd(y_scan, act, z, dskip_rep, w2['ssd_norm_w'], "ssd_gate")
    nsb = SB_WIDTH // (SB_GROUP * SB_LANES)
    qkv_cols = (0, nsb, 2 * nsb)
    o_sb, y_sb, *others = _sbg_fwd(qkv, qkv, qkv, qkv_cols, w2['sb_norm_w'], "sb_attn", gather=later)
    g_rows, g_up = [lax.dynamic_update_index_in_dim(g, m, oidx, 0) for m, g in zip(later, others)]
    full['w_up'] = _from_shards('w_up', g_up)
    full.update({n: _from_shards(n, a) for n, a in zip(ROW_SPLIT, _split_rows(g_rows, [sh2[n][0] for n in ROW_SPLIT]))})
    ycat = jnp.concatenate([y_ssd, y_sb], axis=1)
    x_2 = _mm(ycat, full['w_out'], res=x2d, name="out_proj")
    h2 = _rms_fwd(x_2, w2['norm_mem_w'], "norm_mem")
    qm = _mm(h2, full['w_mq'], out_dtype=_MXU, name="mem_q")
    mn = _rms_fwd(mem[0], w2['norm_memkv_w'], "norm_memkv")
    km = _mm(mn, full['w_mk'], out_dtype=_MXU, name="mem_k")
    vm = _mm(mn, full['w_mv'], out_dtype=_MXU, name="mem_v")
    om = _xattn_fwd(qm, km, vm, "mem_attn")
    x_3 = _mm(om, full['w_mo'], res=x_2, name="mem_o")
    h3 = _rms_fwd(x_3, w2['norm_ffn_w'], "norm_ffn")
    up = _mm(h3, full['w_up'], name="ffn_up")
    a_ffn = _conv_glu_fwd(up, full['conv_ffn_w'], w2['conv_ffn_b'], "ffn_conv_glu")
    x_4 = _mm(a_ffn, full['w_down'], res=x_3, name="ffn_down")
    dx4, dx4b, g_final, loss_blk = _loss_bwd(x_4, loss_target[0], w2['norm_final_w'], "loss_head")

    G = {'norm_final_w': g_final}
    dact = _mm(dx4b, full['w_down'], tb=True, name="d_ffn_act")
    G['w_down'] = _mm(a_ffn, dx4b, ta=True, name="g_w_down")
    du, G['conv_ffn_w'], G['conv_ffn_b'] = _conv_glu_bwd(up, dact, full['conv_ffn_w'], w2['conv_ffn_b'], "d_ffn_conv_glu")
    dup = _dwconv_bwd_x(du, full['conv_ffn_w'], "d_ffn_conv")
    dh3 = _mm(dup, full['w_up'], tb=True, name="d_h3")
    G['w_up'] = _mm(h3, dup, ta=True, name="g_w_up")
    dx3, dx3b, G['norm_ffn_w'] = _rms_bwd(dh3, x_3, w2['norm_ffn_w'], dx4, "d_norm_ffn")
    dom = _mm(dx3b, full['w_mo'], tb=True, out_dtype=_MXU, name="d_mem_o")
    G['w_mo'] = _mm(om, dx3b, ta=True, name="g_w_mo")
    dqm, dkm, dvm = _xattn_bwd(qm, km, vm, dom, "d_mem_attn")
    G['w_mq'] = _mm(h2, dqm, ta=True, name="g_w_mq")
    dh2 = _mm(dqm, full['w_mq'], tb=True, name="d_h2")
    dx2, dx2b, G['norm_mem_w'] = _rms_bwd(dh2, x_2, w2['norm_mem_w'], dx3, "d_norm_mem")
    G['w_mk'] = _mm(mn, dkm, ta=True, name="g_w_mk")
    G['w_mv'] = _mm(mn, dvm, ta=True, name="g_w_mv")
    dmn = _mm(dvm, full['w_mv'], tb=True, res=_mm(dkm, full['w_mk'], tb=True, name="d_mn_k"), name="d_mn_v")
    _, _, G['norm_memkv_w'] = _rms_bwd(dmn, mem[0], w2['norm_memkv_w'], None, "d_norm_memkv")
    dycat = _mm(dx2b, full['w_out'], tb=True, name="d_ycat")
    G['w_out'] = _mm(ycat, dx2b, ta=True, name="g_w_out")
    dy1, dz, g_dskip_lane, G['ssd_norm_w'] = _ssd_gate_bwd(dycat, y_scan, act, z, dskip_rep, w2['ssd_norm_w'], "d_ssd_gate")
    dxs_h, dB, dC, ddt, dA = _ssdg_bwd(xs_h, Bm, Cm, dt, cs, csT, prev, heads(dy1, H), w2['a_log'], w2['d_skip'], "d_ssd_scan")
    G['d_skip'] = jnp.sum(g_dskip_lane.reshape(H, P), axis=1)[None, :]
    dact_xbc = jnp.concatenate([unheads(dxs_h), unheads(dB), unheads(dC)], axis=1)
    dpre, G['conv_ssd_w'], G['conv_ssd_b'] = _conv_silu_bwd(xbc, dact_xbc, full['conv_ssd_w'], w2['conv_ssd_b'], "d_ssd_conv_silu")
    dxbc = _dwconv_bwd_x(dpre, full['conv_ssd_w'], "d_ssd_conv")
    ddtp, G['dt_bias'], G['a_log'] = _dt_bwd(ddt, dA, dtp, w2['dt_bias'], w2['a_log'], "d_dt")
    def pair_sums(to_pair, tag):
        got = _exchange(to_pair, group='c', same_src=False, own=False, chunks=4, name="reduce_pair" + tag)
        sums, wires = [], []
        for k, (t, g) in enumerate(zip(to_pair, got)):
            _, _, r, cw = t.shape
            full_sum, wire_sum = _sum_lead(g.reshape(1, 4 * r, cw), "reduce_pair_sum%s%d" % (tag, k), first=t.reshape(2, 4 * r, cw),
                                           pick=cidx.reshape(1), wire=True)
            sums.append(full_sum.reshape(4, r, cw))
            wires.append(wire_sum.reshape(4, r, cw))
        return sums, wires

    def chip_sums(sums, got, tag):
        return [_sum_lead(g, "reduce_chips_sum%s%d" % (tag, k), first=p, pick=oidx.reshape(1)) for k, (p, g) in enumerate(zip(sums, got))]

    by_owner = [jnp.concatenate([_to_shards(n, G[n]) for n in ROW_SPLIT], axis=1), _to_shards('w_up', G['w_up'])]
    pair_a, wire_a = pair_sums([a.reshape(4, 2, a.shape[1] // 2, a.shape[2]).transpose(1, 0, 2, 3) for a in by_owner], "_a")

    dq, dk, dv, G['sb_norm_w'], *got_a = _sbg_bwd(qkv, qkv, qkv, qkv_cols, o_sb, dycat, o1 // (SB_GROUP * SB_LANES), w2['sb_norm_w'],
                                                  "d_sb_attn", swap=wire_a)
    chips_rows, chips_up = chip_sums(pair_a, got_a, "_a")
    dproj = jnp.concatenate([dz, dxbc, dq, dk.astype(_MXU), dv.astype(_MXU), ddtp], axis=1)
    dh1 = _mm(dproj, W_in_r, tb=True, name="d_h1")
    g_in_r = _mm(h1, dproj, ta=True, name="g_w_in")
    nq = 3 * SB_WIDTH

    def in_cols(lo, hi):
        spans = []
        for a, b, shift in ((0, o2, 0), (o2, o3, nq), (o3, o3 + nq, o2 - o3)):
            s, e = max(lo, a), min(hi, b)
            if s < e:
                spans.append((s + shift, e + shift))
        return spans

    hr_in, cs_in = g_in_r.shape[0] // 2, (o3 + nq) // 4
    g_in_pair = jnp.stack([jnp.stack([jnp.concatenate([g_in_r[h * hr_in:(h + 1) * hr_in, s:e] for s, e in in_cols(j * cs_in, (j + 1) * cs_in)],
                                                      axis=1) for j in range(4)]) for h in range(2)])
    grad_x, _, G['norm_mix_w'] = _rms_bwd(dh1, x2d, w2['norm_mix_w'], dx2, "d_norm_mix")

    pair_b, wire_b = pair_sums([g_in_pair], "_b")
    (chips_in,) = chip_sums(pair_b, _exchange(wire_b, group='xy', same_src=False, own=False, chunks=1, name="reduce_chips"), "_b")
    chips = [chips_rows, chips_in, chips_up]
    got = _exchange(chips, group='c', same_src=True, own=False, chunks=4, name="share_pair")
    red = [jnp.where(cidx == 0, jnp.concatenate([m, g[0]], axis=0), jnp.concatenate([g[0], m], axis=0))[None]
           for m, g in zip(chips, got)]
    gsh = dict(zip(ROW_SPLIT, [a[0] for a in _split_rows(red[0], [sh2[n][0] for n in ROW_SPLIT])]))
    gsh['w_in'], gsh['w_up'] = red[1][0], red[2][0]

    small_parts = [G[n].reshape(1, -1) for n in SMALL + CONVW] + [loss_blk[:1, :1]]
    small_shapes = [sh2[n] for n in SMALL] + [G[n].shape for n in CONVW] + [(1, 1)]
    small_rows = _ceil_to(-(-sum(math.prod(s) for s in small_shapes) // 128), 8)
    spack = _pack_rows(small_parts, 128, small_rows)[0]
    (gathered,) = _exchange([spack], group='xyc', same_src=True, own=True, chunks=1, name="gather_small")
    parts = [a[0] for a in _unpack_rows(_sum_lead(gathered, "small_sum")[None], small_shapes)]
    gsh.update(zip(SMALL, parts))
    for n, a in zip(CONVW, parts[len(SMALL):-1]):
        gsh[n] = lax.dynamic_index_in_dim(_to_shards(n, a), oidx, 0, keepdims=False)
    loss = parts[-1].reshape(())

    delta, new_m, new_v = {}, {}, {}
    for n in BIG:
        delta[n], new_m[n], new_v[n] = _adamw(w2[n], gsh[n], Mo[n].reshape(sh2[n]), Vo[n].reshape(sh2[n]), "adamw_" + n)
    for grp, width, tag in ((CONVW, 128, "adamw_conv"), (SMALL, 128, "adamw_small")):
        rows = _ceil_to(-(-sum(math.prod(sh2[n]) for n in grp) // width), 8)
        packed = [_pack_rows([src[n].reshape(1, -1) for n in grp], width, rows)[0]
                  for src in (w2, gsh, {n: Mo[n] for n in grp}, {n: Vo[n] for n in grp})]
        outs = _adamw(*packed, tag)
        for dst, o in zip((delta, new_m, new_v), outs):
            dst.update(zip(grp, [a[0] for a in _unpack_rows(o[None], [sh2[n] for n in grp])]))

    def shaped(d):
        return [d[n].reshape(shapes[n]) for n in WEIGHTS]

    return (loss, grad_x[None], *shaped(gsh), *shaped(delta), *shaped(new_m), *shaped(new_v))
```

```python
import math

import jax
import jax.numpy as jnp
from jax import lax
from jax.experimental import pallas as pl
from jax.experimental.pallas import tpu as pltpu

F32 = jnp.float32
_MXU = jnp.bfloat16
_WIRE = jnp.bfloat16
EPS = 1e-6
_VMEM_LIMIT = 48 * 1024 * 1024
_HI = lax.Precision.HIGHEST

SSD_HEADS = 16
SSD_HEAD_DIM = 64
SSD_GROUPS = 2
SSD_STATE = 128
SSD_CHUNK = 128
SSD_INNER = SSD_HEADS * SSD_HEAD_DIM
SSD_XBC = SSD_INNER + 2 * SSD_GROUPS * SSD_STATE
SB_HEADS = 16
SB_HEAD_DIM = 64
SB_WIDTH = SB_HEADS * SB_HEAD_DIM
MEM_HEADS = 4
DT_PAD = 128

ADAM_LR = 0.001
ADAM_B1 = 0.9
ADAM_B2 = 0.999
ADAM_EPS = 1e-08
ADAM_WD = 0.01
ADAM_STEP = 10

WEIGHTS = ['norm_mix_w', 'w_in', 'conv_ssd_w', 'conv_ssd_b', 'dt_bias', 'a_log', 'd_skip', 'ssd_norm_w',
           'sb_norm_w', 'w_out', 'norm_mem_w', 'norm_memkv_w', 'w_mq', 'w_mk', 'w_mv', 'w_mo', 'norm_ffn_w',
           'w_up', 'conv_ffn_w', 'conv_ffn_b', 'w_down', 'norm_final_w']
BIG = ['w_in', 'w_out', 'w_mq', 'w_mk', 'w_mv', 'w_mo', 'w_up', 'w_down']
COL_SPLIT = ('w_in', 'w_up', 'conv_ssd_w', 'conv_ffn_w')
ROW_SPLIT = ['w_out', 'w_mq', 'w_mk', 'w_mv', 'w_mo', 'w_down']
CONVW = ['conv_ssd_w', 'conv_ffn_w']
SMALL = ['norm_mix_w', 'conv_ssd_b', 'dt_bias', 'a_log', 'd_skip', 'ssd_norm_w', 'sb_norm_w', 'norm_mem_w',
         'norm_memkv_w', 'norm_ffn_w', 'conv_ffn_b', 'norm_final_w']


def _cparams(*sem):
    return pltpu.CompilerParams(dimension_semantics=sem if sem else None, vmem_limit_bytes=_VMEM_LIMIT)


def _pick(n, cap, mult=128):
    best = None
    for d in range(mult, min(n, cap) + 1, mult):
        if n % d == 0:
            best = d
    return n if best is None else best


def _dot(a, b, ca, cb):
    return lax.dot_general(a.astype(_MXU), b.astype(_MXU), (((ca,), (cb,)), ((), ())), preferred_element_type=F32)


def _sigmoid(v):
    return 1.0 / (1.0 + jnp.exp(-v))


def _log1p(u):
    w = 1.0 + u
    return jnp.where(w == 1.0, u, jnp.log(w) * (u / (w - 1.0)))


def _mm(a, b, *, ta=False, tb=False, res=None, out_dtype=F32, name):
    if ta:
        K, M = a.shape
    else:
        M, K = a.shape
    if tb:
        N, K2 = b.shape
    else:
        K2, N = b.shape
    assert K == K2, (a.shape, b.shape)
    tm = _pick(M, 1408, 128 if ta else 16)
    tn = _pick(N, 1536)
    tk = _pick(K, 1536)
    nk = K // tk
    a_spec = pl.BlockSpec((tk, tm), lambda i, j, k: (k, i)) if ta else pl.BlockSpec((tm, tk), lambda i, j, k: (i, k))
    b_spec = pl.BlockSpec((tn, tk), lambda i, j, k: (j, k)) if tb else pl.BlockSpec((tk, tn), lambda i, j, k: (k, j))
    o_spec = pl.BlockSpec((tm, tn), lambda i, j, k: (i, j))
    ca, cb = (0 if ta else 1), (1 if tb else 0)

    def body(*refs):
        if res is None:
            a_ref, b_ref, o_ref, acc_ref = refs
            r_ref = None
        else:
            a_ref, b_ref, r_ref, o_ref, acc_ref = refs
        k = pl.program_id(2)

        @pl.when(k == 0)
        def _():
            acc_ref[...] = jnp.zeros_like(acc_ref)

        acc_ref[...] += _dot(a_ref[...], b_ref[...], ca, cb)

        @pl.when(k == nk - 1)
        def _():
            r = acc_ref[...]
            if r_ref is not None:
                r = r + r_ref[...].astype(F32)
            o_ref[...] = r.astype(o_ref.dtype)

    ins = [a, b] + ([] if res is None else [res])
    in_specs = [a_spec, b_spec] + ([] if res is None else [o_spec])
    return pl.pallas_call(
        body, grid=(M // tm, N // tn, nk), in_specs=in_specs, out_specs=o_spec,
        out_shape=jax.ShapeDtypeStruct((M, N), out_dtype), scratch_shapes=[pltpu.VMEM((tm, tn), F32)],
        compiler_params=_cparams("parallel", "parallel", "arbitrary"), name=name)(*ins)


def _rows(S, cap):
    return _pick(S, cap, 8)


def _rms_fwd(x, w, name, gather=()):
    S, D = x.shape
    tm = _rows(S, 512)
    ng = len(gather)
    send, finish = _gather_phases([g.shape[0] // 2 for g in gather])

    def body(x_ref, w_ref, *rest):
        o_ref = rest[ng]
        comm = (rest[:ng], rest[ng + 1:2 * ng + 1], rest[2 * ng + 1:])
        if ng:
            @pl.when(pl.program_id(0) == 0)
            def _():
                send(*comm)

        xv = x_ref[...]
        r = lax.rsqrt(jnp.mean(xv * xv, axis=-1, keepdims=True) + EPS)
        o_ref[...] = ((xv * r) * w_ref[...]).astype(o_ref.dtype)
        if ng:
            @pl.when(pl.program_id(0) == S // tm - 1)
            def _():
                finish(*comm)

    row = pl.BlockSpec((tm, D), lambda i: (i, 0))
    hbm = pl.BlockSpec(memory_space=pl.ANY)
    res = pl.pallas_call(
        body, grid=(S // tm,), in_specs=[row, pl.BlockSpec((1, D), lambda i: (0, 0))] + [hbm] * ng, out_specs=[row] + [hbm] * ng,
        out_shape=[jax.ShapeDtypeStruct((S, D), _MXU)] + [jax.ShapeDtypeStruct((4,) + tuple(g.shape), g.dtype) for g in gather],
        scratch_shapes=_gather_sems(ng) if ng else [],
        compiler_params=_cparams("arbitrary") if ng else _cparams("parallel"), name=name)(x, w, *gather)
    return res if ng else res[0]


def _rms_bwd(dh, x, w, dres, name, swap=()):
    S, D = x.shape
    tm = _rows(S, 256)
    ns = len(swap)
    nin = 3 if dres is None else 4
    send, finish = _swap_phases(ns)

    def body(*refs):
        comm = (refs[nin:nin + ns], refs[nin + ns + 3:nin + 2 * ns + 3], refs[nin + 2 * ns + 3:])
        if ns:
            @pl.when(pl.program_id(0) == 0)
            def _():
                send(*comm)

            @pl.when(pl.program_id(0) == S // tm - 1)
            def _():
                finish(*comm)

        refs = refs[:nin] + refs[nin + ns:nin + ns + 3]
        if dres is None:
            dh_ref, x_ref, w_ref, dx_ref, dxb_ref, dw_ref = refs
            dres_ref = None
        else:
            dh_ref, x_ref, w_ref, dres_ref, dx_ref, dxb_ref, dw_ref = refs
        xv = x_ref[...]
        r = lax.rsqrt(jnp.mean(xv * xv, axis=-1, keepdims=True) + EPS)
        xn = xv * r
        dy = dh_ref[...].astype(F32)

        @pl.when(pl.program_id(0) == 0)
        def _():
            dw_ref[...] = jnp.zeros_like(dw_ref)

        dw_ref[...] += jnp.sum(dy * xn, axis=0, keepdims=True)
        dxn = dy * w_ref[...]
        dx = r * (dxn - xn * jnp.mean(dxn * xn, axis=-1, keepdims=True))
        if dres_ref is not None:
            dx = dx + dres_ref[...]
        dx_ref[...] = dx
        dxb_ref[...] = dx.astype(dxb_ref.dtype)

    row = pl.BlockSpec((tm, D), lambda i: (i, 0))
    vec = pl.BlockSpec((1, D), lambda i: (0, 0))
    hbm = pl.BlockSpec(memory_space=pl.ANY)
    ins = [dh, x, w] + ([] if dres is None else [dres]) + list(swap)
    in_specs = [row, row, vec] + ([] if dres is None else [row]) + [hbm] * ns
    return pl.pallas_call(
        body, grid=(S // tm,), in_specs=in_specs, out_specs=[row, row, vec] + [hbm] * ns,
        out_shape=[jax.ShapeDtypeStruct((S, D), F32), jax.ShapeDtypeStruct((S, D), _MXU), jax.ShapeDtypeStruct((1, D), F32)]
                  + [jax.ShapeDtypeStruct((3,) + tuple(a.shape[1:]), a.dtype) for a in swap],
        scratch_shapes=[pltpu.SemaphoreType.DMA((3 * ns,))] * 2 if ns else [],
        compiler_params=_cparams("arbitrary"), name=name)(*ins)


def _loss_bwd(x, tgt, w, name):
    S, D = x.shape
    tm = _rows(S, 256)

    def body(x_ref, t_ref, w_ref, dx_ref, dxb_ref, dw_ref, loss_ref):
        xv = x_ref[...]
        r = lax.rsqrt(jnp.mean(xv * xv, axis=-1, keepdims=True) + EPS)
        xn = xv * r
        e = xn * w_ref[...] - t_ref[...]

        @pl.when(pl.program_id(0) == 0)
        def _():
            dw_ref[...] = jnp.zeros_like(dw_ref)
            loss_ref[...] = jnp.zeros_like(loss_ref)

        tok = jnp.mean(e * e, axis=-1, keepdims=True)
        loss_ref[...] += jnp.broadcast_to(0.5 * jnp.sum(tok, axis=0, keepdims=True), loss_ref.shape)
        dy = e * (1.0 / D)
        dw_ref[...] += jnp.sum(dy * xn, axis=0, keepdims=True)
        dxn = dy * w_ref[...]
        dx = r * (dxn - xn * jnp.mean(dxn * xn, axis=-1, keepdims=True))
        dx_ref[...] = dx
        dxb_ref[...] = dx.astype(dxb_ref.dtype)

    row = pl.BlockSpec((tm, D), lambda i: (i, 0))
    vec = pl.BlockSpec((1, D), lambda i: (0, 0))
    return pl.pallas_call(
        body, grid=(S // tm,), in_specs=[row, row, vec],
        out_specs=[row, row, vec, pl.BlockSpec((8, 128), lambda i: (0, 0))],
        out_shape=[jax.ShapeDtypeStruct((S, D), F32), jax.ShapeDtypeStruct((S, D), _MXU),
                   jax.ShapeDtypeStruct((1, D), F32), jax.ShapeDtypeStruct((8, 128), F32)],
        compiler_params=_cparams("arbitrary"), name=name)(x, tgt, w)


def _conv_tiles(S, C):
    return _rows(S, 256), _pick(C, 1536)


def _conv_silu_fwd(x, w, b, name):
    S, C = x.shape
    K = w.shape[0]
    tm, tc = _conv_tiles(S, C)

    def body(x_ref, p_ref, w_ref, b_ref, o_ref):
        sh = _shifted_rows(x_ref[...], jnp.where(pl.program_id(0) > 0, p_ref[...], 0.0), K)
        u = b_ref[...] + sum(sh[d] * w_ref[K - 1 - d:K - d, :] for d in range(K))
        o_ref[...] = u * _sigmoid(u)

    return pl.pallas_call(
        body, grid=(S // tm, C // tc),
        in_specs=[pl.BlockSpec((tm, tc), lambda i, j: (i, j)),
                  pl.BlockSpec((8, tc), lambda i, j: (jnp.maximum(i * (tm // 8) - 1, 0), j)),
                  pl.BlockSpec((K, tc), lambda i, j: (0, j)), pl.BlockSpec((1, tc), lambda i, j: (0, j))],
        out_specs=pl.BlockSpec((tm, tc), lambda i, j: (i, j)), out_shape=jax.ShapeDtypeStruct((S, C), F32),
        compiler_params=_cparams("parallel", "parallel"), name=name)(x, x, w, b)


def _conv_silu_bwd(x, dact, w, b, name):
    S, C = x.shape
    K = w.shape[0]
    tm, tc = _conv_tiles(S, C)

    def body(x_ref, p_ref, g_ref, w_ref, b_ref, du_ref, dw_ref, db_ref):
        i = pl.program_id(1)

        @pl.when(i == 0)
        def _():
            dw_ref[...] = jnp.zeros_like(dw_ref)
            db_ref[...] = jnp.zeros_like(db_ref)

        sh = _shifted_rows(x_ref[...], jnp.where(i > 0, p_ref[...], 0.0), K)
        u = b_ref[...] + sum(sh[d] * w_ref[K - 1 - d:K - d, :] for d in range(K))
        s = _sigmoid(u)
        du = g_ref[...].astype(F32) * (s * (1.0 + u * (1.0 - s)))
        du_ref[...] = du.astype(du_ref.dtype)
        db_ref[...] += jnp.sum(du, axis=0, keepdims=True)
        for d in range(K):
            dw_ref[K - 1 - d:K - d, :] += jnp.sum(du * sh[d], axis=0, keepdims=True)

    blk = pl.BlockSpec((tm, tc), lambda j, i: (i, j))
    kc, bc = pl.BlockSpec((K, tc), lambda j, i: (0, j)), pl.BlockSpec((1, tc), lambda j, i: (0, j))
    return pl.pallas_call(
        body, grid=(C // tc, S // tm),
        in_specs=[blk, pl.BlockSpec((8, tc), lambda j, i: (jnp.maximum(i * (tm // 8) - 1, 0), j)), blk, kc, bc],
        out_specs=[blk, kc, bc],
        out_shape=[jax.ShapeDtypeStruct((S, C), _MXU), jax.ShapeDtypeStruct((K, C), F32), jax.ShapeDtypeStruct((1, C), F32)],
        compiler_params=_cparams("parallel", "arbitrary"), name=name)(x, x, dact, w, b)


def _dwconv_bwd_x(dy, w, name):
    S, C = dy.shape
    K = w.shape[0]
    tm, tc = _conv_tiles(S, C)
    last = S // tm - 1
    hr = 8 * (4 // dy.dtype.itemsize)

    def body(g_ref, n_ref, w_ref, o_ref):
        cur = g_ref[...].astype(F32)
        nxt = jnp.where(pl.program_id(0) < last, n_ref[...].astype(F32), 0.0)
        xx = jnp.concatenate([cur, nxt], axis=0)
        acc = cur * w_ref[K - 1:K, :]
        for d in range(1, K):
            acc = acc + pltpu.roll(xx, tm + hr - d, 0)[:tm, :] * w_ref[K - 1 - d:K - d, :]
        o_ref[...] = acc.astype(o_ref.dtype)

    return pl.pallas_call(
        body, grid=(S // tm, C // tc),
        in_specs=[pl.BlockSpec((tm, tc), lambda i, j: (i, j)),
                  pl.BlockSpec((hr, tc), lambda i, j: (jnp.minimum((i + 1) * (tm // hr), S // hr - 1), j)),
                  pl.BlockSpec((K, tc), lambda i, j: (0, j))],
        out_specs=pl.BlockSpec((tm, tc), lambda i, j: (i, j)), out_shape=jax.ShapeDtypeStruct((S, C), _MXU),
        compiler_params=_cparams("parallel", "parallel"), name=name)(dy, dy, w)


def _shifted_rows(cur, prev, K):
    xx = jnp.concatenate([prev, cur], axis=0)
    return [cur] + [pltpu.roll(xx, d, 0)[8:, :] for d in range(1, K)]


def _conv_glu_fwd(x, w, b, name):
    S, C = x.shape
    K = w.shape[0]
    Fh = C // 2
    tm = _rows(S, 128)

    def body(x_ref, p_ref, w_ref, b_ref, o_ref):
        sh = _shifted_rows(x_ref[...], jnp.where(pl.program_id(0) > 0, p_ref[...], 0.0), K)
        u = b_ref[...] + sum(sh[d] * w_ref[K - 1 - d:K - d, :] for d in range(K))
        g = u[:, :Fh]
        o_ref[...] = (g * _sigmoid(g) * u[:, Fh:]).astype(o_ref.dtype)

    return pl.pallas_call(
        body, grid=(S // tm,),
        in_specs=[pl.BlockSpec((tm, C), lambda i: (i, 0)), pl.BlockSpec((8, C), lambda i: (jnp.maximum(i * (tm // 8) - 1, 0), 0)),
                  pl.BlockSpec((K, C), lambda i: (0, 0)), pl.BlockSpec((1, C), lambda i: (0, 0))],
        out_specs=pl.BlockSpec((tm, Fh), lambda i: (i, 0)), out_shape=jax.ShapeDtypeStruct((S, Fh), _MXU),
        compiler_params=_cparams("parallel"), name=name)(x, x, w, b)


def _conv_glu_bwd(x, dact, w, b, name):
    S, C = x.shape
    K = w.shape[0]
    Fh = C // 2
    tm = _rows(S, 128)

    def body(x_ref, p_ref, g_ref, w_ref, b_ref, du_ref, dw_ref, db_ref):
        i = pl.program_id(0)

        @pl.when(i == 0)
        def _():
            dw_ref[...] = jnp.zeros_like(dw_ref)
            db_ref[...] = jnp.zeros_like(db_ref)

        sh = _shifted_rows(x_ref[...], jnp.where(i > 0, p_ref[...], 0.0), K)
        u = b_ref[...] + sum(sh[d] * w_ref[K - 1 - d:K - d, :] for d in range(K))
        g = u[:, :Fh]
        da = g_ref[...].astype(F32)
        s = _sigmoid(g)
        halves = ((slice(0, Fh), da * u[:, Fh:] * (s * (1.0 + g * (1.0 - s)))), (slice(Fh, C), da * (g * s)))
        for cols, du in halves:
            du_ref[:, cols] = du.astype(du_ref.dtype)
            db_ref[:, cols] += jnp.sum(du, axis=0, keepdims=True)
            for d in range(K):
                dw_ref[K - 1 - d:K - d, cols] += jnp.sum(du * sh[d][:, cols], axis=0, keepdims=True)

    return pl.pallas_call(
        body, grid=(S // tm,),
        in_specs=[pl.BlockSpec((tm, C), lambda i: (i, 0)), pl.BlockSpec((8, C), lambda i: (jnp.maximum(i * (tm // 8) - 1, 0), 0)),
                  pl.BlockSpec((tm, Fh), lambda i: (i, 0)), pl.BlockSpec((K, C), lambda i: (0, 0)), pl.BlockSpec((1, C), lambda i: (0, 0))],
        out_specs=[pl.BlockSpec((tm, C), lambda i: (i, 0)), pl.BlockSpec((K, C), lambda i: (0, 0)), pl.BlockSpec((1, C), lambda i: (0, 0))],
        out_shape=[jax.ShapeDtypeStruct((S, C), _MXU), jax.ShapeDtypeStruct((K, C), F32), jax.ShapeDtypeStruct((1, C), F32)],
        compiler_params=_cparams("arbitrary"), name=name)(x, x, dact, w, b)


def _xattn_fwd(q, k, v, name):
    S, D = q.shape
    M = k.shape[0]
    hd = D // MEM_HEADS
    tm = _rows(S, 512)
    scale = 1.0 / math.sqrt(hd)

    def body(q_ref, k_ref, v_ref, o_ref):
        for h in range(MEM_HEADS):
            sl = slice(h * hd, (h + 1) * hd)
            s = _dot(q_ref[:, sl], k_ref[:, sl], 1, 1) * scale
            p = jnp.exp(s - jnp.max(s, axis=-1, keepdims=True))
            p = p / jnp.sum(p, axis=-1, keepdims=True)
            o_ref[:, sl] = _dot(p, v_ref[:, sl], 1, 0).astype(o_ref.dtype)

    kv = pl.BlockSpec((M, D), lambda i: (0, 0))
    row = pl.BlockSpec((tm, D), lambda i: (i, 0))
    return pl.pallas_call(body, grid=(S // tm,), in_specs=[row, kv, kv], out_specs=row,
                          out_shape=jax.ShapeDtypeStruct((S, D), _MXU), compiler_params=_cparams("parallel"), name=name)(q, k, v)


def _xattn_bwd(q, k, v, do, name):
    S, D = q.shape
    M = k.shape[0]
    hd = D // MEM_HEADS
    tm = _rows(S, 512)
    scale = 1.0 / math.sqrt(hd)

    def body(q_ref, k_ref, v_ref, do_ref, dq_ref, dk_ref, dv_ref):
        @pl.when(pl.program_id(0) == 0)
        def _():
            dk_ref[...] = jnp.zeros_like(dk_ref)
            dv_ref[...] = jnp.zeros_like(dv_ref)

        for h in range(MEM_HEADS):
            sl = slice(h * hd, (h + 1) * hd)
            qh, kh, vh, doh = q_ref[:, sl], k_ref[:, sl], v_ref[:, sl], do_ref[:, sl]
            s = _dot(qh, kh, 1, 1) * scale
            p = jnp.exp(s - jnp.max(s, axis=-1, keepdims=True))
            p = p / jnp.sum(p, axis=-1, keepdims=True)
            dp = _dot(doh, vh, 1, 1)
            dv_ref[:, sl] += _dot(p, doh, 0, 0)
            ds = (p * (dp - jnp.sum(dp * p, axis=-1, keepdims=True))) * scale
            dq_ref[:, sl] = _dot(ds, kh, 1, 0).astype(dq_ref.dtype)
            dk_ref[:, sl] += _dot(ds, qh, 0, 0)

    kv = pl.BlockSpec((M, D), lambda i: (0, 0))
    row = pl.BlockSpec((tm, D), lambda i: (i, 0))
    return pl.pallas_call(
        body, grid=(S // tm,), in_specs=[row, kv, kv, row], out_specs=[row, kv, kv],
        out_shape=[jax.ShapeDtypeStruct((S, D), _MXU), jax.ShapeDtypeStruct((M, D), F32), jax.ShapeDtypeStruct((M, D), F32)],
        compiler_params=_cparams("arbitrary"), name=name)(q, k, v, do)


def _tri(n, strict, upper):
    r = lax.broadcasted_iota(jnp.int32, (n, n), 0)
    c = lax.broadcasted_iota(jnp.int32, (n, n), 1)
    if upper:
        return (c > r) if strict else (c >= r)
    return (r > c) if strict else (r >= c)


def _ssd_prep(dtp, dt_bias, a_log, name):
    S = dtp.shape[0]
    L, H = SSD_CHUNK, SSD_HEADS

    def body(p_ref, b_ref, al_ref, dt_ref, cs_ref):
        v = p_ref[:, :H] + b_ref[...]
        dt = jnp.maximum(v, 0.0) + _log1p(jnp.exp(-jnp.abs(v)))
        dt_ref[...] = dt
        a = dt * (-jnp.exp(al_ref[...]))
        cs_ref[...] = jnp.dot(_tri(L, False, False).astype(F32), a, precision=_HI, preferred_element_type=F32)

    blk = pl.BlockSpec((L, H), lambda c: (c, 0))
    vec = pl.BlockSpec((1, H), lambda c: (0, 0))
    return pl.pallas_call(body, grid=(S // L,), in_specs=[pl.BlockSpec((L, DT_PAD), lambda c: (c, 0)), vec, vec],
                          out_specs=[blk, blk], out_shape=[jax.ShapeDtypeStruct((S, H), F32)] * 2,
                          compiler_params=_cparams("parallel"), name=name)(dtp, dt_bias, a_log)


def _head_col(blk_ref, h):
    sel = lax.broadcasted_iota(jnp.int32, (1, SSD_HEADS), 1) == h
    return jnp.sum(jnp.where(sel, blk_ref[...], 0.0), axis=1, keepdims=True)


def _ssdg_fwd(xs, Bm, Cm, dt, cs, csT, name):
    H, S, P = xs.shape
    L, N = SSD_CHUNK, SSD_STATE
    nc = S // L
    rep = H // SSD_GROUPS
    hs = range(rep)

    def body(x_ref, b_ref, c_ref, dt_ref, cs_ref, csT_ref, y_ref, prev_ref, st_ref):
        c, g = pl.program_id(0), pl.program_id(1)

        @pl.when(c == 0)
        def _():
            for hh in hs:
                st_ref[g * rep + hh] = jnp.zeros((P, N), F32)

        Bv, Cv = b_ref[...], c_ref[...]
        tril = _tri(L, False, False)
        dtc = [_head_col(dt_ref, g * rep + hh) for hh in hs]
        csc = [_head_col(cs_ref, g * rep + hh) for hh in hs]
        csr = [csT_ref[hh:hh + 1, :] for hh in hs]
        last = [r[:, L - 1:L] for r in csr]
        xc = [x_ref[hh] * dtc[hh] for hh in hs]
        cb = _dot(Cv, Bv, 1, 1)
        m = [cb * jnp.where(tril, jnp.exp(jnp.where(tril, csc[hh] - csr[hh], 0.0)), 0.0) for hh in hs]
        prev = [st_ref[g * rep + hh] for hh in hs]
        yd = [_dot(m[hh], xc[hh], 1, 0) for hh in hs]
        yo = [_dot(Cv, prev[hh], 1, 1) for hh in hs]
        new = [_dot(xc[hh] * jnp.exp(last[hh] - csc[hh]), Bv, 0, 0) for hh in hs]
        for hh in hs:
            y_ref[hh] = yd[hh] + yo[hh] * jnp.exp(csc[hh])
            prev_ref[hh] = prev[hh]
            st_ref[g * rep + hh] = prev[hh] * jnp.exp(last[hh]) + new[hh]

    tok = pl.BlockSpec((L, H), lambda c, g: (c, 0))
    return pl.pallas_call(
        body, grid=(nc, SSD_GROUPS),
        in_specs=[pl.BlockSpec((rep, L, P), lambda c, g: (g, c, 0)), pl.BlockSpec((None, L, N), lambda c, g: (g, c, 0)),
                  pl.BlockSpec((None, L, N), lambda c, g: (g, c, 0)), tok, tok, pl.BlockSpec((rep, L), lambda c, g: (g, c))],
        out_specs=[pl.BlockSpec((rep, L, P), lambda c, g: (g, c, 0)),
                   pl.BlockSpec((rep, None, P, N), lambda c, g: (g, c, 0, 0))],
        out_shape=[jax.ShapeDtypeStruct((H, S, P), F32), jax.ShapeDtypeStruct((H, nc, P, N), F32)],
        scratch_shapes=[pltpu.VMEM((H, P, N), F32)],
        compiler_params=_cparams("arbitrary", "arbitrary"), name=name)(xs, Bm, Cm, dt, cs, csT)


def _ssdg_bwd(xs, Bm, Cm, dt, cs, csT, prev, dy, a_log, d_skip, name):
    H, S, P = xs.shape
    L, N = SSD_CHUNK, SSD_STATE
    nc = S // L
    rep = H // SSD_GROUPS
    hs = range(rep)

    def rowsum(a):
        return jnp.sum(a, axis=1, keepdims=True)

    def body(x_ref, b_ref, c_ref, dt_ref, cs_ref, csT_ref, prev_ref, dy_ref, al_ref, dk_ref,
             dx_ref, db_ref, dc_ref, ddt_ref, da_ref, g_ref):
        ci, g = pl.program_id(0), pl.program_id(1)

        @pl.when(ci == 0)
        def _():
            for hh in hs:
                g_ref[g * rep + hh] = jnp.zeros((P, N), F32)

        @pl.when((ci == 0) & (g == 0))
        def _():
            da_ref[...] = jnp.zeros_like(da_ref)

        @pl.when(g == 0)
        def _():
            ddt_ref[...] = jnp.zeros_like(ddt_ref)

        lane = lax.broadcasted_iota(jnp.int32, (1, H), 1)
        sel = [lane == g * rep + hh for hh in hs]
        A_h = [-jnp.exp(rowsum(jnp.where(s, al_ref[...], 0.0))) for s in sel]
        dsk = [rowsum(jnp.where(s, dk_ref[...], 0.0)) for s in sel]
        dtc = [_head_col(dt_ref, g * rep + hh) for hh in hs]
        csc = [_head_col(cs_ref, g * rep + hh) for hh in hs]
        csr = [csT_ref[hh:hh + 1, :] for hh in hs]
        last = [r[:, L - 1:L] for r in csr]
        Bv, Cv = b_ref[...], c_ref[...]
        xv = [x_ref[hh] for hh in hs]
        xc = [xv[hh] * dtc[hh] for hh in hs]
        dY = [dy_ref[hh] for hh in hs]
        prv = [prev_ref[hh] for hh in hs]
        G = [g_ref[g * rep + hh] for hh in hs]
        ecs = [jnp.exp(v) for v in csc]
        w = [jnp.exp(last[hh] - csc[hh]) for hh in hs]
        cd = [jnp.exp(v) for v in last]
        tril = _tri(L, False, False)
        triu = _tri(L, False, True)
        lam = [jnp.where(tril, jnp.exp(jnp.where(tril, csc[hh] - csr[hh], 0.0)), 0.0) for hh in hs]
        lamT = [jnp.where(triu, jnp.exp(jnp.where(triu, csr[hh] - csc[hh], 0.0)), 0.0) for hh in hs]
        cb = _dot(Cv, Bv, 1, 1)
        bc = _dot(Bv, Cv, 1, 1)
        dM = [_dot(dY[hh], xc[hh], 1, 1) for hh in hs]
        dMT = [_dot(xc[hh], dY[hh], 1, 1) for hh in hs]
        cp = [_dot(Cv, prv[hh], 1, 1) for hh in hs]
        BG = [_dot(Bv, G[hh], 1, 1) for hh in hs]
        dYe = [dY[hh] * ecs[hh] for hh in hs]
        dprev = [_dot(dYe[hh], Cv, 0, 0) for hh in hs]
        m = [cb * lam[hh] for hh in hs]
        mT = [bc * lamT[hh] for hh in hs]
        dxc = [_dot(mT[hh], dY[hh], 1, 0) + w[hh] * BG[hh] for hh in hs]
        dcb = sum([dM[hh] * lam[hh] for hh in hs][1:], dM[0] * lam[0])
        dcbT = sum([dMT[hh] * lamT[hh] for hh in hs][1:], dMT[0] * lamT[0])
        dC = _dot(dcb, Bv, 1, 0)
        dB = _dot(dcbT, Cv, 1, 0)
        for hh in hs:
            dC = dC + _dot(dYe[hh], prv[hh], 1, 0)
            dB = dB + _dot(xc[hh] * w[hh], G[hh], 1, 0)
        dc_ref[...] = dC
        db_ref[...] = dB
        ddt_acc = jnp.zeros((L, H), F32)
        da_acc = jnp.zeros((1, H), F32)
        rev = _tri(L, False, True).astype(F32)
        for hh in hs:
            dww = rowsum(xc[hh] * BG[hh]) * w[hh]
            dcs = (rowsum(dM[hh] * m[hh]) - rowsum(dMT[hh] * mT[hh]) + rowsum(dY[hh] * (cp[hh] * ecs[hh])) - dww)
            extra = jnp.sum(dww, axis=0, keepdims=True) + cd[hh] * jnp.sum(rowsum(G[hh] * prv[hh]), axis=0, keepdims=True)
            g_ref[g * rep + hh] = G[hh] * cd[hh] + dprev[hh]
            da = jnp.dot(rev, dcs, precision=_HI, preferred_element_type=F32) + extra
            dx_ref[hh] = dxc[hh] * dtc[hh] + dY[hh] * dsk[hh]
            ddt_acc = ddt_acc + jnp.where(sel[hh], da * A_h[hh] + rowsum(dxc[hh] * xv[hh]), 0.0)
            da_acc = da_acc + jnp.where(sel[hh], jnp.sum(da * dtc[hh], axis=0, keepdims=True), 0.0)
        ddt_ref[...] += ddt_acc
        da_ref[...] += da_acc

    rc = lambda ci: nc - 1 - ci
    hd = pl.BlockSpec((rep, L, P), lambda ci, g: (g, rc(ci), 0))
    grp = pl.BlockSpec((None, L, N), lambda ci, g: (g, rc(ci), 0))
    tok = pl.BlockSpec((L, H), lambda ci, g: (rc(ci), 0))
    vec = pl.BlockSpec((1, H), lambda ci, g: (0, 0))
    return pl.pallas_call(
        body, grid=(nc, SSD_GROUPS),
        in_specs=[hd, grp, grp, tok, tok, pl.BlockSpec((rep, L), lambda ci, g: (g, rc(ci))),
                  pl.BlockSpec((rep, None, P, N), lambda ci, g: (g, rc(ci), 0, 0)), hd, vec, vec],
        out_specs=[hd, grp, grp, tok, vec],
        out_shape=[jax.ShapeDtypeStruct((H, S, P), F32), jax.ShapeDtypeStruct((SSD_GROUPS, S, N), F32),
                   jax.ShapeDtypeStruct((SSD_GROUPS, S, N), F32), jax.ShapeDtypeStruct((S, H), F32),
                   jax.ShapeDtypeStruct((1, H), F32)],
        scratch_shapes=[pltpu.VMEM((H, P, N), F32)],
        compiler_params=_cparams("arbitrary", "arbitrary"), name=name)(xs, Bm, Cm, dt, cs, csT, prev, dy, a_log, d_skip)


def _dt_bwd(ddt, dA, dtp, dt_bias, a_log, name):
    S, H = ddt.shape
    tm = _rows(S, 512)

    def body(g_ref, da_ref, p_ref, b_ref, al_ref, o_ref, db_ref, dal_ref):
        @pl.when(pl.program_id(0) == 0)
        def _():
            db_ref[...] = jnp.zeros_like(db_ref)
            dal_ref[...] = da_ref[...] * (-jnp.exp(al_ref[...]))

        g = g_ref[...] * _sigmoid(p_ref[:, :H] + b_ref[...])
        db_ref[...] += jnp.sum(g, axis=0, keepdims=True)
        o_ref[...] = jnp.zeros_like(o_ref)
        o_ref[:, :H] = g.astype(o_ref.dtype)

    vec = pl.BlockSpec((1, H), lambda i: (0, 0))
    return pl.pallas_call(
        body, grid=(S // tm,),
        in_specs=[pl.BlockSpec((tm, H), lambda i: (i, 0)), vec, pl.BlockSpec((tm, DT_PAD), lambda i: (i, 0)), vec, vec],
        out_specs=[pl.BlockSpec((tm, DT_PAD), lambda i: (i, 0)), vec, vec],
        out_shape=[jax.ShapeDtypeStruct((S, DT_PAD), _MXU), jax.ShapeDtypeStruct((1, H), F32), jax.ShapeDtypeStruct((1, H), F32)],
        compiler_params=_cparams("arbitrary"), name=name)(ddt, dA, dtp, dt_bias, a_log)


def _ssd_gate_fwd(y, act, z, dskip, w, name):
    S, D = y.shape
    tm = _rows(S, 256)
    Gw = D // SSD_GROUPS

    def body(y_ref, x_ref, z_ref, k_ref, w_ref, o_ref):
        zv = z_ref[...]
        y2 = (y_ref[...] + x_ref[...] * k_ref[...]) * (zv * _sigmoid(zv))
        for g in range(SSD_GROUPS):
            sl = slice(g * Gw, (g + 1) * Gw)
            v = y2[:, sl]
            r = lax.rsqrt(jnp.mean(v * v, axis=-1, keepdims=True) + EPS)
            o_ref[:, sl] = ((v * r) * w_ref[:, sl]).astype(o_ref.dtype)

    row = pl.BlockSpec((tm, D), lambda i: (i, 0))
    vec = pl.BlockSpec((1, D), lambda i: (0, 0))
    return pl.pallas_call(body, grid=(S // tm,), in_specs=[row, row, row, vec, vec], out_specs=row,
                          out_shape=jax.ShapeDtypeStruct((S, D), _MXU), compiler_params=_cparams("parallel"),
                          name=name)(y, act, z, dskip, w)


def _ssd_gate_bwd(dyn, y, act, z, dskip, w, name):
    S, D = y.shape
    tm = _rows(S, 256)
    Gw = D // SSD_GROUPS

    def body(g_ref, y_ref, x_ref, z_ref, k_ref, w_ref, dy_ref, dz_ref, dk_ref, dw_ref):
        @pl.when(pl.program_id(0) == 0)
        def _():
            dk_ref[...] = jnp.zeros_like(dk_ref)
            dw_ref[...] = jnp.zeros_like(dw_ref)

        zv = z_ref[...]
        xv = x_ref[...]
        s = _sigmoid(zv)
        sz = zv * s
        y1 = y_ref[...] + xv * k_ref[...]
        y2 = y1 * sz
        for g in range(SSD_GROUPS):
            sl = slice(g * Gw, (g + 1) * Gw)
            v = y2[:, sl]
            r = lax.rsqrt(jnp.mean(v * v, axis=-1, keepdims=True) + EPS)
            vn = v * r
            gy = g_ref[:, sl].astype(F32)
            dw_ref[:, sl] += jnp.sum(gy * vn, axis=0, keepdims=True)
            dvn = gy * w_ref[:, sl]
            dy2 = r * (dvn - vn * jnp.mean(dvn * vn, axis=-1, keepdims=True))
            dy1 = dy2 * sz[:, sl]
            dy_ref[:, sl] = dy1
            dz_ref[:, sl] = (dy2 * y1[:, sl] * (s[:, sl] * (1.0 + zv[:, sl] * (1.0 - s[:, sl])))).astype(dz_ref.dtype)
            dk_ref[:, sl] += jnp.sum(dy1 * xv[:, sl], axis=0, keepdims=True)

    row = pl.BlockSpec((tm, D), lambda i: (i, 0))
    vec = pl.BlockSpec((1, D), lambda i: (0, 0))
    return pl.pallas_call(
        body, grid=(S // tm,), in_specs=[row, row, row, row, vec, vec], out_specs=[row, row, vec, vec],
        out_shape=[jax.ShapeDtypeStruct((S, D), F32), jax.ShapeDtypeStruct((S, D), _MXU),
                   jax.ShapeDtypeStruct((1, D), F32), jax.ShapeDtypeStruct((1, D), F32)],
        compiler_params=_cparams("arbitrary"), name=name)(dyn, y, act, z, dskip, w)


def _split_dot(v, u):
    hi = v.astype(_MXU)
    lo = (v - hi.astype(F32)).astype(_MXU)
    dn = (((1,), (0,)), ((), ()))
    return (lax.dot_general(hi, u, dn, preferred_element_type=F32) + lax.dot_general(lo, u, dn, preferred_element_type=F32))


def _sb_tiles(S):
    return _pick(S, 256, 128)


SB_LANES = 128
SB_PACK = SB_LANES // SB_HEAD_DIM
SB_ROWS = 128
SB_SCALE = 1.0 / math.sqrt(SB_HEAD_DIM)


def _head_masks():
    lane = lax.broadcasted_iota(jnp.int32, (1, SB_LANES), 1)
    return [(lane // SB_HEAD_DIM) == hh for hh in range(SB_PACK)]


def _by_head(hm, vals):
    out = vals[-1]
    for hh in range(SB_PACK - 2, -1, -1):
        out = jnp.where(hm[hh], vals[hh], out)
    return out


SB_DEAD = -110.0


def _sb_alive(Rs):
    m = Rs[0]
    for R in Rs[1:]:
        m = jnp.maximum(m, R)
    return jnp.max(m) > SB_DEAD


def _sb_rows(a, r):
    return a[r * SB_ROWS:(r + 1) * SB_ROWS]


def _sb_assemble(hm, vals):
    nr = len(vals) // SB_PACK
    return jnp.concatenate([_by_head(hm, vals[r * SB_PACK:(r + 1) * SB_PACK]) for r in range(nr)], axis=0)


def _sb_scores(zs, U, Rs, masks):
    ls = [-jnp.maximum(z, 0.0) - jnp.log(1.0 + jnp.exp(-jnp.abs(z))) for z in zs]
    if masks is not None:
        ls = [jnp.where(m, l, 0.0) for m, l in zip(masks, ls)]
    Es = [lax.dot_general(l.astype(_MXU), U, (((1,), (0,)), ((), ())), preferred_element_type=F32) for l in ls]
    As = [jnp.exp(l + z + (E + R)) for l, z, E, R in zip(ls, zs, Es, Rs)]
    if masks is not None:
        As = [jnp.where(m, A, 0.0) for m, A in zip(masks, As)]
    return ls, [A.astype(_MXU) for A in As]


SB_GROUP = 2


def _sbg_chains(T):
    return [(b, r, hh) for b in range(SB_GROUP) for r in range(T // SB_ROWS) for hh in range(SB_PACK)]


def _lanes(a, b):
    return a[:, b * SB_LANES:(b + 1) * SB_LANES]


def _sbg_join(hm, vals):
    per = len(vals) // SB_GROUP
    return jnp.concatenate([_sb_assemble(hm, vals[b * per:(b + 1) * per]) for b in range(SB_GROUP)], axis=1)


def _sbg_head_sum(hm, a):
    return jnp.concatenate([_by_head(hm, [jnp.sum(jnp.where(m, _lanes(a, b), 0.0), axis=1, keepdims=True) for m in hm])
                            for b in range(SB_GROUP)], axis=1)


def _sbg_fwd(q_arr, k_arr, v_arr, cols, w, name, gather=()):
    S = q_arr.shape[0]
    T = _sb_tiles(S)
    cq, ck, cv = cols
    GW = SB_GROUP * SB_LANES
    nb = SB_WIDTH // GW
    ng = len(gather)
    send, finish = _gather_phases([g.shape[0] // 2 for g in gather])

    def body(q_ref, k_ref, v_ref, w_ref, *rest):
        o_ref, y_ref = rest[ng:ng + 2]
        comm = (rest[:ng], rest[ng + 2:2 * ng + 2], rest[2 * ng + 2:])
        i = pl.program_id(1)
        if ng:
            @pl.when((pl.program_id(0) == 0) & (i == 0))
            def _():
                send(*comm)
        hm = _head_masks()
        qs = q_ref[...] * SB_SCALE
        chains = _sbg_chains(T)
        qcs = [_sb_rows(jnp.where(hm[hh], _lanes(qs, b), jnp.zeros((T, SB_LANES), qs.dtype)), r) for b, r, hh in chains]
        U = _tri(T, True, False).astype(_MXU)

        def scores_of(j):
            kj = k_ref[pl.ds(pl.multiple_of(j * T, T), T), :]
            return [_dot(qc, _lanes(kj, b), 1, 1) for qc, (b, _, _) in zip(qcs, chains)]

        def weighted(Abs, j):
            vj = v_ref[pl.ds(pl.multiple_of(j * T, T), T), :]
            return _sbg_join(hm, [_dot(Ab, _lanes(vj, b), 1, 0) for Ab, (b, _, _) in zip(Abs, chains)])

        def step(carry):
            jj, acc, Rs, Aprev = carry
            j = i - 1 - jj
            zs = scores_of(j)
            acc = acc + weighted(Aprev, j + 1)
            ls, Abs = _sb_scores(zs, U, Rs, None)
            return jj + 1, acc, tuple(R + jnp.sum(l, axis=1, keepdims=True) for R, l in zip(Rs, ls)), tuple(Abs)

        masks = [_sb_rows(_tri(T, True, False), r) for _, r, _ in chains]
        zero = jnp.zeros((SB_ROWS, 1), F32)
        ls, Abs = _sb_scores(scores_of(i), U, (zero,) * len(chains), masks)
        carry = (jnp.int32(0), jnp.zeros((T, GW), F32), tuple(jnp.sum(l, axis=1, keepdims=True) for l in ls), tuple(Abs))
        jj, acc, _, Alast = lax.while_loop(lambda c: (c[0] < i) & _sb_alive(c[2]), step, carry)
        acc = acc + weighted(Alast, i - jj)
        o_ref[...] = acc
        r = lax.rsqrt(_sbg_head_sum(hm, acc * acc) * (1.0 / SB_HEAD_DIM) + EPS)
        y_ref[...] = ((acc * r) * w_ref[...]).astype(y_ref.dtype)
        if ng:
            @pl.when((pl.program_id(0) == nb - 1) & (i == S // T - 1))
            def _():
                finish(*comm)

    blk = pl.BlockSpec((T, GW), lambda h, i: (i, h))
    hbm = pl.BlockSpec(memory_space=pl.ANY)
    return pl.pallas_call(
        body, grid=(nb, S // T),
        in_specs=[pl.BlockSpec((T, GW), lambda h, i: (i, cq + h)), pl.BlockSpec((S, GW), lambda h, i: (0, ck + h), pipeline_mode=pl.Buffered(1)),
                  pl.BlockSpec((S, GW), lambda h, i: (0, cv + h), pipeline_mode=pl.Buffered(1)), pl.BlockSpec((1, GW), lambda h, i: (0, h))]
                 + [hbm] * ng,
        out_specs=[blk, blk] + [hbm] * ng,
        out_shape=[jax.ShapeDtypeStruct((S, SB_WIDTH), F32), jax.ShapeDtypeStruct((S, SB_WIDTH), _MXU)]
                  + [jax.ShapeDtypeStruct((4,) + tuple(g.shape), g.dtype) for g in gather],
        scratch_shapes=_gather_sems(ng) if ng else [],
        compiler_params=_cparams("arbitrary", "arbitrary") if ng else _cparams("parallel", "parallel"), name=name)(q_arr, k_arr, v_arr, w, *gather)


def _sbg_bwd(q_arr, k_arr, v_arr, cols, o, dy_arr, cdy, w, name, swap=()):
    S = q_arr.shape[0]
    T = _sb_tiles(S)
    cq, ck, cv = cols
    GW = SB_GROUP * SB_LANES
    nb = SB_WIDTH // GW
    ns = len(swap)
    send, finish = _swap_phases(ns)

    def body(q_ref, k_ref, v_ref, o_ref, dy_ref, w_ref, *rest):
        dq_ref, dk_ref, dv_ref, dw_ref = rest[ns:ns + 4]
        comm = (rest[:ns], rest[ns + 4:2 * ns + 4], rest[2 * ns + 4:])
        i = pl.program_id(1)
        if ns:
            @pl.when((pl.program_id(0) == 0) & (i == 0))
            def _():
                send(*comm)

        @pl.when(i == 0)
        def _():
            dk_ref[...] = jnp.zeros_like(dk_ref)
            dv_ref[...] = jnp.zeros_like(dv_ref)
            dw_ref[...] = jnp.zeros_like(dw_ref)

        hm = _head_masks()
        chains = _sbg_chains(T)
        qs = q_ref[...] * SB_SCALE
        ov = o_ref[...]
        gy = dy_ref[...]
        r = lax.rsqrt(_sbg_head_sum(hm, ov * ov) * (1.0 / SB_HEAD_DIM) + EPS)
        on = ov * r
        dw_ref[...] += jnp.sum(gy * on, axis=0, keepdims=True)
        don = gy * w_ref[...]
        do = r * (don - on * (_sbg_head_sum(hm, don * on) * (1.0 / SB_HEAD_DIM)))
        dob = do.astype(_MXU)
        dprod = dob.astype(F32) * ov
        zt = jnp.zeros((T, SB_LANES), dob.dtype)
        qm = [[jnp.where(hm[hh], _lanes(qs, b), zt) for hh in range(SB_PACK)] for b in range(SB_GROUP)]
        dm = [[jnp.where(hm[hh], _lanes(dob, b), zt) for hh in range(SB_PACK)] for b in range(SB_GROUP)]
        qcs = [_sb_rows(qm[b][hh], r_) for b, r_, hh in chains]
        doc = [_sb_rows(dm[b][hh], r_) for b, r_, hh in chains]
        Dt = [_sb_rows(jnp.sum(jnp.where(hm[hh], _lanes(dprod, b), 0.0), axis=1, keepdims=True), r_) for b, r_, hh in chains]
        U = _tri(T, True, False).astype(_MXU)
        Ui = _tri(T, False, False).astype(_MXU)

        def products_of(j):
            off = pl.multiple_of(j * T, T)
            kj = k_ref[pl.ds(off, T), :]
            vj = v_ref[pl.ds(off, T), :]
            return ([_dot(qc, _lanes(kj, b), 1, 1) for qc, (b, _, _) in zip(qcs, chains)],
                    [_dot(d, _lanes(vj, b), 1, 1) for d, (b, _, _) in zip(doc, chains)])

        def core(zs, dAs, Rs, Qs, masks):
            ls, Abs = _sb_scores(zs, U, Rs, masks)
            Gs = [dA * Ab.astype(F32) for dA, Ab in zip(dAs, Abs)]
            sfx = [_split_dot(G, Ui) for G in Gs]
            dzs = []
            for c, (l, G, s, D, Q) in enumerate(zip(ls, Gs, sfx, Dt, Qs)):
                P = D - (s + Q)
                dz = jnp.exp(l) * (G + P) - P
                if masks is not None:
                    dz = jnp.where(masks[c], dz, 0.0)
                dzs.append(dz.astype(_MXU))
            newR = tuple(R + jnp.sum(l, axis=1, keepdims=True) for R, l in zip(Rs, ls))
            newQ = tuple(Q + jnp.sum(G, axis=1, keepdims=True) for Q, G in zip(Qs, Gs))
            return tuple(Abs), tuple(dzs), newR, newQ

        def over_rows(vals, other):
            nr = T // SB_ROWS
            tiles = []
            for b in range(SB_GROUP):
                acc = None
                for hh in range(SB_PACK):
                    rows = jnp.concatenate([vals[(b * nr + r_) * SB_PACK + hh] for r_ in range(nr)], axis=0)
                    part = _dot(rows, other[b][hh], 0, 0)
                    acc = part if acc is None else acc + part
                tiles.append(acc)
            return jnp.concatenate(tiles, axis=1)

        def emit(Abs, dzs, j):
            off = pl.multiple_of(j * T, T)
            kj = k_ref[pl.ds(off, T), :]
            dk_ref[pl.ds(off, T), :] += over_rows(dzs, qm)
            dv_ref[pl.ds(off, T), :] += over_rows(Abs, dm)
            return _sbg_join(hm, [_dot(dzb, _lanes(kj, b), 1, 0) for dzb, (b, _, _) in zip(dzs, chains)])

        def step(carry):
            jj, dq, Rs, Qs, Aprev, dzprev = carry
            j = i - 1 - jj
            zs, dAs = products_of(j)
            dq = dq + emit(Aprev, dzprev, j + 1)
            Abs, dzs, Rs, Qs = core(zs, dAs, Rs, Qs, None)
            return jj + 1, dq, Rs, Qs, Abs, dzs

        masks = [_sb_rows(_tri(T, True, False), r_) for _, r_, _ in chains]
        zero = (jnp.zeros((SB_ROWS, 1), F32),) * len(chains)
        zs, dAs = products_of(i)
        Abs, dzs, Rs, Qs = core(zs, dAs, zero, zero, masks)
        jj, dq, _, _, Alast, dzlast = lax.while_loop(lambda c: (c[0] < i) & _sb_alive(c[2]), step,
                                                     (jnp.int32(0), jnp.zeros((T, GW), F32), Rs, Qs, Abs, dzs))
        dq = dq + emit(Alast, dzlast, i - jj)
        dq_ref[...] = (dq * SB_SCALE).astype(dq_ref.dtype)
        if ns:
            @pl.when((pl.program_id(0) == nb - 1) & (i == S // T - 1))
            def _():
                finish(*comm)

    blk = pl.BlockSpec((T, GW), lambda h, i: (i, h))
    full = pl.BlockSpec((S, GW), lambda h, i: (0, h), pipeline_mode=pl.Buffered(1))
    wsp = pl.BlockSpec((1, GW), lambda h, i: (0, h))
    hbm = pl.BlockSpec(memory_space=pl.ANY)
    return pl.pallas_call(
        body, grid=(nb, S // T),
        in_specs=[pl.BlockSpec((T, GW), lambda h, i: (i, cq + h)), pl.BlockSpec((S, GW), lambda h, i: (0, ck + h), pipeline_mode=pl.Buffered(1)),
                  pl.BlockSpec((S, GW), lambda h, i: (0, cv + h), pipeline_mode=pl.Buffered(1)), blk,
                  pl.BlockSpec((T, GW), lambda h, i: (i, cdy + h)), wsp] + [hbm] * ns,
        out_specs=[blk, full, full, wsp] + [hbm] * ns,
        out_shape=[jax.ShapeDtypeStruct((S, SB_WIDTH), _MXU), jax.ShapeDtypeStruct((S, SB_WIDTH), F32),
                   jax.ShapeDtypeStruct((S, SB_WIDTH), F32), jax.ShapeDtypeStruct((1, SB_WIDTH), F32)]
                  + [jax.ShapeDtypeStruct((3,) + tuple(a.shape[1:]), a.dtype) for a in swap],
        scratch_shapes=[pltpu.SemaphoreType.DMA((3 * ns,))] * 2 if ns else [],
        compiler_params=_cparams("arbitrary", "arbitrary") if ns else _cparams("parallel", "arbitrary"),
        name=name)(q_arr, k_arr, v_arr, o, dy_arr, w, *swap)


def _adamw(w, g, m, v, name):
    R, C = w.shape
    tm = _rows(R, 256) if R % 8 == 0 else R
    c1 = 1.0 - ADAM_B1 ** ADAM_STEP
    c2 = 1.0 - ADAM_B2 ** ADAM_STEP

    def body(w_ref, g_ref, m_ref, v_ref, d_ref, nm_ref, nv_ref):
        gv = g_ref[...]
        mn = ADAM_B1 * m_ref[...] + (1.0 - ADAM_B1) * gv
        vn = ADAM_B2 * v_ref[...] + (1.0 - ADAM_B2) * (gv * gv)
        d_ref[...] = -ADAM_LR * ((mn / c1) / (jnp.sqrt(vn / c2) + ADAM_EPS) + ADAM_WD * w_ref[...])
        nm_ref[...] = mn
        nv_ref[...] = vn

    blk = pl.BlockSpec((tm, C), lambda i: (i, 0))
    return pl.pallas_call(body, grid=(R // tm,), in_specs=[blk] * 4, out_specs=[blk] * 3,
                          out_shape=[jax.ShapeDtypeStruct((R, C), F32)] * 3, compiler_params=_cparams("parallel"),
                          name=name)(w, g, m, v)


def _sum_lead(a, name, first=None, pick=None, wire=False):
    n, R, C = a.shape
    tm = _rows(R, 256)
    nin = 1 if first is None else 2

    def body(*refs):
        refs = refs[nin - 1:]
        a_ref = refs[nin - 1]
        s = a_ref[0].astype(F32) if first is None else refs[0][...] + a_ref[0]
        for p in range(1, n):
            s = s + a_ref[p]
        for o_ref in refs[nin:]:
            o_ref[...] = s.astype(o_ref.dtype)

    outs = [jax.ShapeDtypeStruct((R, C), F32)] + ([jax.ShapeDtypeStruct((R, C), _WIRE)] if wire else [])
    if first is None:
        row = pl.BlockSpec((tm, C), lambda i: (i, 0))
        res = pl.pallas_call(body, grid=(R // tm,), in_specs=[pl.BlockSpec((n, tm, C), lambda i: (0, i, 0))],
                             out_specs=[row] * len(outs), out_shape=outs, compiler_params=_cparams("parallel"), name=name)(a)
    else:
        row = pl.BlockSpec((tm, C), lambda i, p: (i, 0))
        grid_spec = pltpu.PrefetchScalarGridSpec(
            num_scalar_prefetch=1, grid=(R // tm,),
            in_specs=[pl.BlockSpec((None, tm, C), lambda i, p: (p[0], i, 0)), pl.BlockSpec((n, tm, C), lambda i, p: (0, i, 0))],
            out_specs=[row] * len(outs))
        res = pl.pallas_call(body, grid_spec=grid_spec, out_shape=outs, compiler_params=_cparams("parallel"), name=name)(pick, first, a)
    return res if wire else res[0]


_GROUP_BITS = {'c': ((0, 0, 1),), 'xy': ((0, 1, 0), (1, 0, 0), (1, 1, 0)),
               'xyc': tuple((k >> 2 & 1, k >> 1 & 1, k & 1) for k in range(1, 8))}


def _exchange(srcs, *, group, same_src, own, chunks, name):
    flips = _GROUP_BITS[group]
    n = len(flips) + 1
    na = len(srcs)
    blk_shapes = [tuple(s.shape) if same_src else tuple(s.shape[1:]) for s in srcs]
    assert all(bs[0] % chunks == 0 for bs in blk_shapes), blk_shapes

    def body(*refs):
        src_refs, dst_refs = refs[:na], refs[na:2 * na]
        send_sems, recv_sems, loc_sems = refs[2 * na:]
        x, y, c = lax.axis_index("x"), lax.axis_index("y"), lax.axis_index("c")

        def member(px, py, pc):
            return {'c': pc, 'xy': 2 * px + py, 'xyc': 4 * px + 2 * py + pc}[group]

        def piece(ref, a, q):
            rows = blk_shapes[a][0] // chunks
            return ref.at[pl.ds(q * rows, rows)]

        me = member(x, y, c)
        started, arrivals = [], []
        for a in range(na):
            mine = src_refs[a] if same_src else src_refs[a].at[me]
            if own:
                for q in range(chunks):
                    cp = pltpu.make_async_copy(piece(mine, a, q), piece(dst_refs[a].at[me], a, q), loc_sems.at[a * chunks + q])
                    cp.start()
                    started.append(cp.wait)
            for kk, (fx, fy, fc) in enumerate(flips):
                px, py, pc = (1 - x if fx else x), (1 - y if fy else y), (1 - c if fc else c)
                peer = member(px, py, pc)
                out_blk = src_refs[a] if same_src else src_refs[a].at[peer]
                there = dst_refs[a].at[me if own else kk]
                here = dst_refs[a].at[peer if own else kk]
                for q in range(chunks):
                    s = (a * (n - 1) + kk) * chunks + q
                    out = pltpu.make_async_remote_copy(
                        src_ref=piece(out_blk, a, q), dst_ref=piece(there, a, q), send_sem=send_sems.at[s],
                        recv_sem=recv_sems.at[s], device_id=(px, py, pc), device_id_type=pl.DeviceIdType.MESH)
                    out.start()
                    started.append(out.wait_send)
                    arrivals.append(pltpu.make_async_remote_copy(
                        src_ref=piece(mine, a, q), dst_ref=piece(here, a, q), send_sem=send_sems.at[s],
                        recv_sem=recv_sems.at[s], device_id=(px, py, pc), device_id_type=pl.DeviceIdType.MESH).wait_recv)
        for wait in arrivals + started:
            wait()

    nsem = na * (n - 1) * chunks
    hbm = pl.BlockSpec(memory_space=pl.ANY)
    return pl.pallas_call(
        body, in_specs=[hbm] * na, out_specs=[hbm] * na,
        out_shape=[jax.ShapeDtypeStruct(((n if own else n - 1),) + bs, s.dtype) for bs, s in zip(blk_shapes, srcs)],
        scratch_shapes=[pltpu.SemaphoreType.DMA((nsem,)), pltpu.SemaphoreType.DMA((nsem,)),
                        pltpu.SemaphoreType.DMA((na * chunks,))],
        compiler_params=pltpu.CompilerParams(has_side_effects=True), name=name)(*srcs)


def _gather_phases(halves):
    flips = _GROUP_BITS['xy']
    nf = len(flips)
    na = len(halves)

    def copies(src_refs, dst_refs, sems):
        send_sems, recv_sems, fsend_sems, frecv_sems = sems
        x, y, c = lax.axis_index("x"), lax.axis_index("y"), lax.axis_index("c")

        def half(ref, a, which):
            return ref.at[pl.ds(pl.multiple_of(which * halves[a], 8), halves[a])]

        def copy(src, dst, pair, s, to):
            return pltpu.make_async_remote_copy(src_ref=src, dst_ref=dst, send_sem=pair[0].at[s], recv_sem=pair[1].at[s],
                                                device_id=to, device_id_type=pl.DeviceIdType.MESH)

        out = []
        for a in range(na):
            for kk, (fx, fy, _) in enumerate(flips):
                peer = ((1 - x if fx else x), (1 - y if fy else y), c)
                there = dst_refs[a].at[2 * peer[0] + peer[1]]
                s = a * nf + kk
                ici, d2d = (send_sems, recv_sems), (fsend_sems, frecv_sems)
                out.append((copy(half(src_refs[a], a, c), half(dst_refs[a].at[2 * x + y], a, c), ici, s, peer),
                            copy(half(src_refs[a], a, c), half(there, a, c), ici, s, (x, y, c)),
                            copy(half(there, a, c), half(there, a, c), d2d, s, (x, y, 1 - c)),
                            copy(half(there, a, 1 - c), half(there, a, 1 - c), d2d, s, (x, y, 1 - c))))
        return out

    def send(src_refs, dst_refs, sems):
        for first, _, _, _ in copies(src_refs, dst_refs, sems):
            first.start()

    def finish(src_refs, dst_refs, sems):
        cs = copies(src_refs, dst_refs, sems)
        for _, landed, onward, _ in cs:
            landed.wait_recv()
            onward.start()
        for _, _, _, passed in cs:
            passed.wait_recv()
        for first, _, onward, _ in cs:
            first.wait_send()
            onward.wait_send()

    return send, finish


def _swap_phases(na):
    flips = _GROUP_BITS['xy']

    def copies(src_refs, dst_refs, sems):
        x, y, c = lax.axis_index("x"), lax.axis_index("y"), lax.axis_index("c")
        out = []
        for a in range(na):
            for kk, (fx, fy, _) in enumerate(flips):
                px, py = (1 - x if fx else x), (1 - y if fy else y)
                s = a * len(flips) + kk
                out.append(pltpu.make_async_remote_copy(
                    src_ref=src_refs[a].at[2 * px + py], dst_ref=dst_refs[a].at[kk], send_sem=sems[0].at[s], recv_sem=sems[1].at[s],
                    device_id=(px, py, c), device_id_type=pl.DeviceIdType.MESH))
        return out

    def send(src_refs, dst_refs, sems):
        for cp in copies(src_refs, dst_refs, sems):
            cp.start()

    def finish(src_refs, dst_refs, sems):
        cs = copies(src_refs, dst_refs, sems)
        for cp in cs:
            cp.wait_recv()
        for cp in cs:
            cp.wait_send()

    return send, finish


def _gather_sems(na):
    return [pltpu.SemaphoreType.DMA((na * len(_GROUP_BITS['xy']),))] * 4


def _to_shards(name, full):
    R, C = full.shape
    if name in COL_SPLIT:
        return full.reshape(R, 4, C // 4).transpose(1, 0, 2)
    return full.reshape(4, R // 4, C)


def _from_shards(name, sh):
    n, R, C = sh.shape
    if name in COL_SPLIT:
        return sh.transpose(1, 0, 2).reshape(R, n * C)
    return sh.reshape(n * R, C)


def _pack_rows(parts, width, rows):
    n = parts[0].shape[0]
    flat = jnp.concatenate([p.reshape(n, -1) for p in parts], axis=1)
    return jnp.pad(flat, ((0, 0), (0, rows * width - flat.shape[1]))).reshape(n, rows, width)


def _unpack_rows(buf, shapes):
    n = buf.shape[0]
    flat = buf.reshape(n, -1)
    out, o = [], 0
    for s in shapes:
        sz = math.prod(s)
        out.append(flat[:, o:o + sz].reshape((n,) + tuple(s)))
        o += sz
    return out


def _split_rows(a, rows):
    out, o = [], 0
    for r in rows:
        out.append(a[:, o:o + r])
        o += r
    return out


def _ceil_to(v, m):
    return -(-v // m) * m


def kernel(x, mem, norm_mix_w, w_in, conv_ssd_w, conv_ssd_b, dt_bias, a_log, d_skip, ssd_norm_w, sb_norm_w, w_out, norm_mem_w, norm_memkv_w, w_mq, w_mk, w_mv, w_mo, norm_ffn_w, w_up, conv_ffn_w, conv_ffn_b, w_down, norm_final_w, loss_target, m_norm_mix_w, m_w_in, m_conv_ssd_w, m_conv_ssd_b, m_dt_bias, m_a_log, m_d_skip, m_ssd_norm_w, m_sb_norm_w, m_w_out, m_norm_mem_w, m_norm_memkv_w, m_w_mq, m_w_mk, m_w_mv, m_w_mo, m_norm_ffn_w, m_w_up, m_conv_ffn_w, m_conv_ffn_b, m_w_down, m_norm_final_w, v_norm_mix_w, v_w_in, v_conv_ssd_w, v_conv_ssd_b, v_dt_bias, v_a_log, v_d_skip, v_ssd_norm_w, v_sb_norm_w, v_w_out, v_norm_mem_w, v_norm_memkv_w, v_w_mq, v_w_mk, v_w_mv, v_w_mo, v_norm_ffn_w, v_w_up, v_conv_ffn_w, v_conv_ffn_b, v_w_down, v_norm_final_w):
    W = dict(norm_mix_w=norm_mix_w, w_in=w_in, conv_ssd_w=conv_ssd_w, conv_ssd_b=conv_ssd_b, dt_bias=dt_bias, a_log=a_log,
             d_skip=d_skip, ssd_norm_w=ssd_norm_w, sb_norm_w=sb_norm_w, w_out=w_out, norm_mem_w=norm_mem_w,
             norm_memkv_w=norm_memkv_w, w_mq=w_mq, w_mk=w_mk, w_mv=w_mv, w_mo=w_mo, norm_ffn_w=norm_ffn_w, w_up=w_up,
             conv_ffn_w=conv_ffn_w, conv_ffn_b=conv_ffn_b, w_down=w_down, norm_final_w=norm_final_w)
    Mo = dict(norm_mix_w=m_norm_mix_w, w_in=m_w_in, conv_ssd_w=m_conv_ssd_w, conv_ssd_b=m_conv_ssd_b, dt_bias=m_dt_bias,
              a_log=m_a_log, d_skip=m_d_skip, ssd_norm_w=m_ssd_norm_w, sb_norm_w=m_sb_norm_w, w_out=m_w_out,
              norm_mem_w=m_norm_mem_w, norm_memkv_w=m_norm_memkv_w, w_mq=m_w_mq, w_mk=m_w_mk, w_mv=m_w_mv, w_mo=m_w_mo,
              norm_ffn_w=m_norm_ffn_w, w_up=m_w_up, conv_ffn_w=m_conv_ffn_w, conv_ffn_b=m_conv_ffn_b, w_down=m_w_down,
              norm_final_w=m_norm_final_w)
    Vo = dict(norm_mix_w=v_norm_mix_w, w_in=v_w_in, conv_ssd_w=v_conv_ssd_w, conv_ssd_b=v_conv_ssd_b, dt_bias=v_dt_bias,
              a_log=v_a_log, d_skip=v_d_skip, ssd_norm_w=v_ssd_norm_w, sb_norm_w=v_sb_norm_w, w_out=v_w_out,
              norm_mem_w=v_norm_mem_w, norm_memkv_w=v_norm_memkv_w, w_mq=v_w_mq, w_mk=v_w_mk, w_mv=v_w_mv, w_mo=v_w_mo,
              norm_ffn_w=v_norm_ffn_w, w_up=v_w_up, conv_ffn_w=v_conv_ffn_w, conv_ffn_b=v_conv_ffn_b, w_down=v_w_down,
              norm_final_w=v_norm_final_w)
    shapes = {n: W[n].shape for n in WEIGHTS}
    sh2 = {n: (1, a.shape[-1]) if a.ndim < 3 else a.shape[-2:] for n, a in W.items()}
    w2 = {n: W[n].reshape(sh2[n]) for n in WEIGHTS}
    x2d = x[0]
    S, D = x2d.shape
    H, P, N = SSD_HEADS, SSD_HEAD_DIM, SSD_STATE

    cv_rows = _ceil_to(-(-sum(math.prod(sh2[n]) for n in CONVW) // 128), 32)
    cpack = _pack_rows([w2[n][None] for n in CONVW], 128, cv_rows)[0]
    stacked = jnp.concatenate([w2[n].astype(_MXU) for n in ROW_SPLIT], axis=0)
    cidx = lax.axis_index("c")
    oidx = 2 * lax.axis_index("x") + lax.axis_index("y")
    now, later = [w2['w_in'].astype(_MXU), cpack], [stacked, w2['w_up'].astype(_MXU)]
    h1, *others = _rms_fwd(x2d, w2['norm_mix_w'], "norm_mix", gather=now)
    g_in, call = [lax.dynamic_update_index_in_dim(g, m, oidx, 0) for m, g in zip(now, others)]
    full = {'w_in': _from_shards('w_in', g_in)}
    full.update({n: _from_shards(n, a) for n, a in zip(CONVW, _unpack_rows(call, [sh2[n] for n in CONVW]))})

    o1 = SSD_INNER
    o2 = o1 + SSD_XBC
    o3 = o2 + SSD_HEADS
    Wi = full['w_in']
    W_z, W_xbc, W_qkv = Wi[:, :o1], Wi[:, o1:o2], Wi[:, o3:]
    W_dt = jnp.pad(Wi[:, o2:o3], ((0, 0), (0, DT_PAD - SSD_HEADS)))
    W_in_r = jnp.concatenate([W_z, W_xbc, W_qkv, W_dt], axis=1)
    dskip_rep = jnp.repeat(w2['d_skip'], P, axis=1)

    z = _mm(h1, W_z, name="proj_z")
    xbc = _mm(h1, W_xbc, name="proj_xbc")
    dtp = _mm(h1, W_dt, name="proj_dt")
    qkv = _mm(h1, W_qkv, out_dtype=_MXU, name="proj_qkv")
    act = _conv_silu_fwd(xbc, full['conv_ssd_w'], w2['conv_ssd_b'], "ssd_conv_silu")
    dt, cs = _ssd_prep(dtp, w2['dt_bias'], w2['a_log'], "ssd_prep")
    csT = cs.T
    def heads(a, nh):
        return a.reshape(S, nh, a.shape[1] // nh).transpose(1, 0, 2)

    def unheads(a):
        return a.transpose(1, 0, 2).reshape(S, a.shape[0] * a.shape[2])

    xs_h = heads(act[:, :o1], H)
    Bm = heads(act[:, o1:o1 + SSD_GROUPS * N], SSD_GROUPS)
    Cm = heads(act[:, o1 + SSD_GROUPS * N:], SSD_GROUPS)
    y_h, prev = _ssdg_fwd(xs_h, Bm, Cm, dt, cs, csT, "ssd_scan")
    y_scan = unheads(y_h)
    y_ssd = _ssd_gate_fwd(y_scan, act, z, dskip_rep, w2['ssd_norm_w'], "ssd_gate")
    nsb = SB_WIDTH // (SB_GROUP * SB_LANES)
    qkv_cols = (0, nsb, 2 * nsb)
    o_sb, y_sb, *others = _sbg_fwd(qkv, qkv, qkv, qkv_cols, w2['sb_norm_w'], "sb_attn", gather=later)
    g_rows, g_up = [lax.dynamic_update_index_in_dim(g, m, oidx, 0) for m, g in zip(later, others)]
    full['w_up'] = _from_shards('w_up', g_up)
    full.update({n: _from_shards(n, a) for n, a in zip(ROW_SPLIT, _split_rows(g_rows, [sh2[n][0] for n in ROW_SPLIT]))})
    ycat = jnp.concatenate([y_ssd, y_sb], axis=1)
    x_2 = _mm(ycat, full['w_out'], res=x2d, name="out_proj")
    h2 = _rms_fwd(x_2, w2['norm_mem_w'], "norm_mem")
    qm = _mm(h2, full['w_mq'], out_dtype=_MXU, name="mem_q")
    mn = _rms_fwd(mem[0], w2['norm_memkv_w'], "norm_memkv")
    km = _mm(mn, full['w_mk'], out_dtype=_MXU, name="mem_k")
    vm = _mm(mn, full['w_mv'], out_dtype=_MXU, name="mem_v")
    om = _xattn_fwd(qm, km, vm, "mem_attn")
    x_3 = _mm(om, full['w_mo'], res=x_2, name="mem_o")
    h3 = _rms_fwd(x_3, w2['norm_ffn_w'], "norm_ffn")
    up = _mm(h3, full['w_up'], name="ffn_up")
    a_ffn = _conv_glu_fwd(up, full['conv_ffn_w'], w2['conv_ffn_b'], "ffn_conv_glu")
    x_4 = _mm(a_ffn, full['w_down'], res=x_3, name="ffn_down")
    dx4, dx4b, g_final, loss_blk = _loss_bwd(x_4, loss_target[0], w2['norm_final_w'], "loss_head")

    G = {'norm_final_w': g_final}
    dact = _mm(dx4b, full['w_down'], tb=True, name="d_ffn_act")
    G['w_down'] = _mm(a_ffn, dx4b, ta=True, name="g_w_down")
    du, G['conv_ffn_w'], G['conv_ffn_b'] = _conv_glu_bwd(up, dact, full['conv_ffn_w'], w2['conv_ffn_b'], "d_ffn_conv_glu")
    dup = _dwconv_bwd_x(du, full['conv_ffn_w'], "d_ffn_conv")
    dh3 = _mm(dup, full['w_up'], tb=True, name="d_h3")
    G['w_up'] = _mm(h3, dup, ta=True, name="g_w_up")
    dx3, dx3b, G['norm_ffn_w'] = _rms_bwd(dh3, x_3, w2['norm_ffn_w'], dx4, "d_norm_ffn")
    dom = _mm(dx3b, full['w_mo'], tb=True, out_dtype=_MXU, name="d_mem_o")
    G['w_mo'] = _mm(om, dx3b, ta=True, name="g_w_mo")
    dqm, dkm, dvm = _xattn_bwd(qm, km, vm, dom, "d_mem_attn")
    G['w_mq'] = _mm(h2, dqm, ta=True, name="g_w_mq")
    dh2 = _mm(dqm, full['w_mq'], tb=True, name="d_h2")
    dx2, dx2b, G['norm_mem_w'] = _rms_bwd(dh2, x_2, w2['norm_mem_w'], dx3, "d_norm_mem")
    G['w_mk'] = _mm(mn, dkm, ta=True, name="g_w_mk")
    G['w_mv'] = _mm(mn, dvm, ta=True, name="g_w_mv")
    dmn = _mm(dvm, full['w_mv'], tb=True, res=_mm(dkm, full['w_mk'], tb=True, name="d_mn_k"), name="d_mn_v")
    _, _, G['norm_memkv_w'] = _rms_bwd(dmn, mem[0], w2['norm_memkv_w'], None, "d_norm_memkv")
    dycat = _mm(dx2b, full['w_out'], tb=True, name="d_ycat")
    G['w_out'] = _mm(ycat, dx2b, ta=True, name="g_w_out")
    dy1, dz, g_dskip_lane, G['ssd_norm_w'] = _ssd_gate_bwd(dycat, y_scan, act, z, dskip_rep, w2['ssd_norm_w'], "d_ssd_gate")
    dxs_h, dB, dC, ddt, dA = _ssdg_bwd(xs_h, Bm, Cm, dt, cs, csT, prev, heads(dy1, H), w2['a_log'], w2['d_skip'], "d_ssd_scan")
    G['d_skip'] = jnp.sum(g_dskip_lane.reshape(H, P), axis=1)[None, :]
    dact_xbc = jnp.concatenate([unheads(dxs_h), unheads(dB), unheads(dC)], axis=1)
    dpre, G['conv_ssd_w'], G['conv_ssd_b'] = _conv_silu_bwd(xbc, dact_xbc, full['conv_ssd_w'], w2['conv_ssd_b'], "d_ssd_conv_silu")
    dxbc = _dwconv_bwd_x(dpre, full['conv_ssd_w'], "d_ssd_conv")
    ddtp, G['dt_bias'], G['a_log'] = _dt_bwd(ddt, dA, dtp, w2['dt_bias'], w2['a_log'], "d_dt")
    def pair_sums(to_pair, tag):
        got = _exchange(to_pair, group='c', same_src=False, own=False, chunks=4, name="reduce_pair" + tag)
        sums, wires = [], []
        for k, (t, g) in enumerate(zip(to_pair, got)):
            _, _, r, cw = t.shape
            full_sum, wire_sum = _sum_lead(g.reshape(1, 4 * r, cw), "reduce_pair_sum%s%d" % (tag, k), first=t.reshape(2, 4 * r, cw),
                                           pick=cidx.reshape(1), wire=True)
            sums.append(full_sum.reshape(4, r, cw))
            wires.append(wire_sum.reshape(4, r, cw))
        return sums, wires

    def chip_sums(sums, got, tag):
        return [_sum_lead(g, "reduce_chips_sum%s%d" % (tag, k), first=p, pick=oidx.reshape(1)) for k, (p, g) in enumerate(zip(sums, got))]

    by_owner = [jnp.concatenate([_to_shards(n, G[n]) for n in ROW_SPLIT], axis=1), _to_shards('w_up', G['w_up'])]
    pair_a, wire_a = pair_sums([a.reshape(4, 2, a.shape[1] // 2, a.shape[2]).transpose(1, 0, 2, 3) for a in by_owner], "_a")

    dq, dk, dv, G['sb_norm_w'], *got_a = _sbg_bwd(qkv, qkv, qkv, qkv_cols, o_sb, dycat, o1 // (SB_GROUP * SB_LANES), w2['sb_norm_w'],
                                                  "d_sb_attn", swap=wire_a)
    chips_rows, chips_up = chip_sums(pair_a, got_a, "_a")
    dproj = jnp.concatenate([dz, dxbc, dq, dk.astype(_MXU), dv.astype(_MXU), ddtp], axis=1)
    dh1 = _mm(dproj, W_in_r, tb=True, name="d_h1")
    g_in_r = _mm(h1, dproj, ta=True, name="g_w_in")
    nq = 3 * SB_WIDTH

    def in_cols(lo, hi):
        spans = []
        for a, b, shift in ((0, o2, 0), (o2, o3, nq), (o3, o3 + nq, o2 - o3)):
            s, e = max(lo, a), min(hi, b)
            if s < e:
                spans.append((s + shift, e + shift))
        return spans

    hr_in, cs_in = g_in_r.shape[0] // 2, (o3 + nq) // 4
    g_in_pair = jnp.stack([jnp.stack([jnp.concatenate([g_in_r[h * hr_in:(h + 1) * hr_in, s:e] for s, e in in_cols(j * cs_in, (j + 1) * cs_in)],
                                                      axis=1) for j in range(4)]) for h in range(2)])

    pair_b, wire_b = pair_sums([g_in_pair], "_b")
    grad_x, _, G['norm_mix_w'], *got_b = _rms_bwd(dh1, x2d, w2['norm_mix_w'], dx2, "d_norm_mix", swap=wire_b)
    (chips_in,) = chip_sums(pair_b, got_b, "_b")
    chips = [chips_rows, chips_in, chips_up]
    got = _exchange(chips, group='c', same_src=True, own=False, chunks=4, name="share_pair")
    red = [jnp.where(cidx == 0, jnp.concatenate([m, g[0]], axis=0), jnp.concatenate([g[0], m], axis=0))[None]
           for m, g in zip(chips, got)]
    gsh = dict(zip(ROW_SPLIT, [a[0] for a in _split_rows(red[0], [sh2[n][0] for n in ROW_SPLIT])]))
    gsh['w_in'], gsh['w_up'] = red[1][0], red[2][0]

    small_parts = [G[n].reshape(1, -1) for n in SMALL + CONVW] + [loss_blk[:1, :1]]
    small_shapes = [sh2[n] for n in SMALL] + [G[n].shape for n in CONVW] + [(1, 1)]
    small_rows = _ceil_to(-(-sum(math.prod(s) for s in small_shapes) // 128), 8)
    spack = _pack_rows(small_parts, 128, small_rows)[0]
    (gathered,) = _exchange([spack], group='xyc', same_src=True, own=True, chunks=1, name="gather_small")
    parts = [a[0] for a in _unpack_rows(_sum_lead(gathered, "small_sum")[None], small_shapes)]
    gsh.update(zip(SMALL, parts))
    for n, a in zip(CONVW, parts[len(SMALL):-1]):
        gsh[n] = lax.dynamic_index_in_dim(_to_shards(n, a), oidx, 0, keepdims=False)
    loss = parts[-1].reshape(())

    delta, new_m, new_v = {}, {}, {}
    for n in BIG:
        delta[n], new_m[n], new_v[n] = _adamw(w2[n], gsh[n], Mo[n].reshape(sh2[n]), Vo[n].reshape(sh2[n]), "adamw_" + n)
    for grp, width, tag in ((CONVW, 128, "adamw_conv"), (SMALL, 128, "adamw_small")):
        rows = _ceil_to(-(-sum(math.prod(sh2[n]) for n in grp) // width), 8)
        packed = [_pack_rows([src[n].reshape(1, -1) for n in grp], width, rows)[0]
                  for src in (w2, gsh, {n: Mo[n] for n in grp}, {n: Vo[n] for n in grp})]
        outs = _adamw(*packed, tag)
        for dst, o in zip((delta, new_m, new_v), outs):
            dst.update(zip(grp, [a[0] for a in _unpack_rows(o[None], [sh2[n] for n in grp])]))

    def shaped(d):
        return [d[n].reshape(shapes[n]) for n in WEIGHTS]

    return (loss, grad_x[None], *shaped(gsh), *shaped(delta), *shaped(new_m), *shaped(new_v))
```

```python
import math

import jax
import jax.numpy as jnp
from jax import lax
from jax.experimental import pallas as pl
from jax.experimental.pallas import tpu as pltpu

F32 = jnp.float32
_MXU = jnp.bfloat16
_WIRE = jnp.bfloat16
EPS = 1e-6
_VMEM_LIMIT = 48 * 1024 * 1024
_HI = lax.Precision.HIGHEST

SSD_HEADS = 16
SSD_HEAD_DIM = 64
SSD_GROUPS = 2
SSD_STATE = 128
SSD_CHUNK = 128
SSD_INNER = SSD_HEADS * SSD_HEAD_DIM
SSD_XBC = SSD_INNER + 2 * SSD_GROUPS * SSD_STATE
SB_HEADS = 16
SB_HEAD_DIM = 64
SB_WIDTH = SB_HEADS * SB_HEAD_DIM
MEM_HEADS = 4
DT_PAD = 128

ADAM_LR = 0.001
ADAM_B1 = 0.9
ADAM_B2 = 0.999
ADAM_EPS = 1e-08
ADAM_WD = 0.01
ADAM_STEP = 10

WEIGHTS = ['norm_mix_w', 'w_in', 'conv_ssd_w', 'conv_ssd_b', 'dt_bias', 'a_log', 'd_skip', 'ssd_norm_w',
           'sb_norm_w', 'w_out', 'norm_mem_w', 'norm_memkv_w', 'w_mq', 'w_mk', 'w_mv', 'w_mo', 'norm_ffn_w',
           'w_up', 'conv_ffn_w', 'conv_ffn_b', 'w_down', 'norm_final_w']
BIG = ['w_in', 'w_out', 'w_mq', 'w_mk', 'w_mv', 'w_mo', 'w_up', 'w_down']
COL_SPLIT = ('w_in', 'w_up', 'conv_ssd_w', 'conv_ffn_w')
ROW_SPLIT = ['w_out', 'w_mq', 'w_mk', 'w_mv', 'w_mo', 'w_down']
CONVW = ['conv_ssd_w', 'conv_ffn_w']
SMALL = ['norm_mix_w', 'conv_ssd_b', 'dt_bias', 'a_log', 'd_skip', 'ssd_norm_w', 'sb_norm_w', 'norm_mem_w',
         'norm_memkv_w', 'norm_ffn_w', 'conv_ffn_b', 'norm_final_w']


def _cparams(*sem):
    return pltpu.CompilerParams(dimension_semantics=sem if sem else None, vmem_limit_bytes=_VMEM_LIMIT)


def _pick(n, cap, mult=128):
    best = None
    for d in range(mult, min(n, cap) + 1, mult):
        if n % d == 0:
            best = d
    return n if best is None else best


def _dot(a, b, ca, cb):
    return lax.dot_general(a.astype(_MXU), b.astype(_MXU), (((ca,), (cb,)), ((), ())), preferred_element_type=F32)


def _sigmoid(v):
    return 1.0 / (1.0 + jnp.exp(-v))


def _log1p(u):
    w = 1.0 + u
    return jnp.where(w == 1.0, u, jnp.log(w) * (u / (w - 1.0)))


def _mm(a, b, *, ta=False, tb=False, res=None, out_dtype=F32, name):
    if ta:
        K, M = a.shape
    else:
        M, K = a.shape
    if tb:
        N, K2 = b.shape
    else:
        K2, N = b.shape
    assert K == K2, (a.shape, b.shape)
    tm = _pick(M, 1408, 128 if ta else 16)
    tn = _pick(N, 1536)
    tk = _pick(K, 1536)
    nk = K // tk
    a_spec = pl.BlockSpec((tk, tm), lambda i, j, k: (k, i)) if ta else pl.BlockSpec((tm, tk), lambda i, j, k: (i, k))
    b_spec = pl.BlockSpec((tn, tk), lambda i, j, k: (j, k)) if tb else pl.BlockSpec((tk, tn), lambda i, j, k: (k, j))
    o_spec = pl.BlockSpec((tm, tn), lambda i, j, k: (i, j))
    ca, cb = (0 if ta else 1), (1 if tb else 0)

    def body(*refs):
        if res is None:
            a_ref, b_ref, o_ref, acc_ref = refs
            r_ref = None
        else:
            a_ref, b_ref, r_ref, o_ref, acc_ref = refs
        k = pl.program_id(2)

        @pl.when(k == 0)
        def _():
            acc_ref[...] = jnp.zeros_like(acc_ref)

        acc_ref[...] += _dot(a_ref[...], b_ref[...], ca, cb)

        @pl.when(k == nk - 1)
        def _():
            r = acc_ref[...]
            if r_ref is not None:
                r = r + r_ref[...].astype(F32)
            o_ref[...] = r.astype(o_ref.dtype)

    ins = [a, b] + ([] if res is None else [res])
    in_specs = [a_spec, b_spec] + ([] if res is None else [o_spec])
    return pl.pallas_call(
        body, grid=(M // tm, N // tn, nk), in_specs=in_specs, out_specs=o_spec,
        out_shape=jax.ShapeDtypeStruct((M, N), out_dtype), scratch_shapes=[pltpu.VMEM((tm, tn), F32)],
        compiler_params=_cparams("parallel", "parallel", "arbitrary"), name=name)(*ins)


def _rows(S, cap):
    return _pick(S, cap, 8)


def _rms_fwd(x, w, name, gather=()):
    S, D = x.shape
    tm = _rows(S, 512)
    ng = len(gather)
    send, finish = _gather_phases([g.shape[0] // 2 for g in gather])

    def body(x_ref, w_ref, *rest):
        o_ref = rest[ng]
        comm = (rest[:ng], rest[ng + 1:2 * ng + 1], rest[2 * ng + 1:])
        if ng:
            @pl.when(pl.program_id(0) == 0)
            def _():
                send(*comm)

        xv = x_ref[...]
        r = lax.rsqrt(jnp.mean(xv * xv, axis=-1, keepdims=True) + EPS)
        o_ref[...] = ((xv * r) * w_ref[...]).astype(o_ref.dtype)
        if ng:
            @pl.when(pl.program_id(0) == S // tm - 1)
            def _():
                finish(*comm)

    row = pl.BlockSpec((tm, D), lambda i: (i, 0))
    hbm = pl.BlockSpec(memory_space=pl.ANY)
    res = pl.pallas_call(
        body, grid=(S // tm,), in_specs=[row, pl.BlockSpec((1, D), lambda i: (0, 0))] + [hbm] * ng, out_specs=[row] + [hbm] * ng,
        out_shape=[jax.ShapeDtypeStruct((S, D), _MXU)] + [jax.ShapeDtypeStruct((4,) + tuple(g.shape), g.dtype) for g in gather],
        scratch_shapes=_gather_sems(ng) if ng else [],
        compiler_params=_cparams("arbitrary") if ng else _cparams("parallel"), name=name)(x, w, *gather)
    return res if ng else res[0]


def _rms_bwd(dh, x, w, dres, name, swap=()):
    S, D = x.shape
    tm = _rows(S, 256)
    ns = len(swap)
    nin = 3 if dres is None else 4
    send, finish = _swap_phases(ns)

    def body(*refs):
        comm = (refs[nin:nin + ns], refs[nin + ns + 3:nin + 2 * ns + 3], refs[nin + 2 * ns + 3:])
        if ns:
            @pl.when(pl.program_id(0) == 0)
            def _():
                send(*comm)

            @pl.when(pl.program_id(0) == S // tm - 1)
            def _():
                finish(*comm)

        refs = refs[:nin] + refs[nin + ns:nin + ns + 3]
        if dres is None:
            dh_ref, x_ref, w_ref, dx_ref, dxb_ref, dw_ref = refs
            dres_ref = None
        else:
            dh_ref, x_ref, w_ref, dres_ref, dx_ref, dxb_ref, dw_ref = refs
        xv = x_ref[...]
        r = lax.rsqrt(jnp.mean(xv * xv, axis=-1, keepdims=True) + EPS)
        xn = xv * r
        dy = dh_ref[...].astype(F32)

        @pl.when(pl.program_id(0) == 0)
        def _():
            dw_ref[...] = jnp.zeros_like(dw_ref)

        dw_ref[...] += jnp.sum(dy * xn, axis=0, keepdims=True)
        dxn = dy * w_ref[...]
        dx = r * (dxn - xn * jnp.mean(dxn * xn, axis=-1, keepdims=True))
        if dres_ref is not None:
            dx = dx + dres_ref[...]
        dx_ref[...] = dx
        dxb_ref[...] = dx.astype(dxb_ref.dtype)

    row = pl.BlockSpec((tm, D), lambda i: (i, 0))
    vec = pl.BlockSpec((1, D), lambda i: (0, 0))
    hbm = pl.BlockSpec(memory_space=pl.ANY)
    ins = [dh, x, w] + ([] if dres is None else [dres]) + list(swap)
    in_specs = [row, row, vec] + ([] if dres is None else [row]) + [hbm] * ns
    return pl.pallas_call(
        body, grid=(S // tm,), in_specs=in_specs, out_specs=[row, row, vec] + [hbm] * ns,
        out_shape=[jax.ShapeDtypeStruct((S, D), F32), jax.ShapeDtypeStruct((S, D), _MXU), jax.ShapeDtypeStruct((1, D), F32)]
                  + [jax.ShapeDtypeStruct((3,) + tuple(a.shape[1:]), a.dtype) for a in swap],
        scratch_shapes=[pltpu.SemaphoreType.DMA((3 * ns,))] * 2 if ns else [],
        compiler_params=_cparams("arbitrary"), name=name)(*ins)


def _loss_bwd(x, tgt, w, name):
    S, D = x.shape
    tm = _rows(S, 256)

    def body(x_ref, t_ref, w_ref, dx_ref, dxb_ref, dw_ref, loss_ref):
        xv = x_ref[...]
        r = lax.rsqrt(jnp.mean(xv * xv, axis=-1, keepdims=True) + EPS)
        xn = xv * r
        e = xn * w_ref[...] - t_ref[...]

        @pl.when(pl.program_id(0) == 0)
        def _():
            dw_ref[...] = jnp.zeros_like(dw_ref)
            loss_ref[...] = jnp.zeros_like(loss_ref)

        tok = jnp.mean(e * e, axis=-1, keepdims=True)
        loss_ref[...] += jnp.broadcast_to(0.5 * jnp.sum(tok, axis=0, keepdims=True), loss_ref.shape)
        dy = e * (1.0 / D)
        dw_ref[...] += jnp.sum(dy * xn, axis=0, keepdims=True)
        dxn = dy * w_ref[...]
        dx = r * (dxn - xn * jnp.mean(dxn * xn, axis=-1, keepdims=True))
        dx_ref[...] = dx
        dxb_ref[...] = dx.astype(dxb_ref.dtype)

    row = pl.BlockSpec((tm, D), lambda i: (i, 0))
    vec = pl.BlockSpec((1, D), lambda i: (0, 0))
    return pl.pallas_call(
        body, grid=(S // tm,), in_specs=[row, row, vec],
        out_specs=[row, row, vec, pl.BlockSpec((8, 128), lambda i: (0, 0))],
        out_shape=[jax.ShapeDtypeStruct((S, D), F32), jax.ShapeDtypeStruct((S, D), _MXU),
                   jax.ShapeDtypeStruct((1, D), F32), jax.ShapeDtypeStruct((8, 128), F32)],
        compiler_params=_cparams("arbitrary"), name=name)(x, tgt, w)


def _conv_tiles(S, C):
    return _rows(S, 256), _pick(C, 1536)


def _conv_silu_fwd(x, w, b, name):
    S, C = x.shape
    K = w.shape[0]
    tm, tc = _conv_tiles(S, C)

    def body(x_ref, p_ref, w_ref, b_ref, o_ref):
        sh = _shifted_rows(x_ref[...], jnp.where(pl.program_id(0) > 0, p_ref[...], 0.0), K)
        u = b_ref[...] + sum(sh[d] * w_ref[K - 1 - d:K - d, :] for d in range(K))
        o_ref[...] = u * _sigmoid(u)

    return pl.pallas_call(
        body, grid=(S // tm, C // tc),
        in_specs=[pl.BlockSpec((tm, tc), lambda i, j: (i, j)),
                  pl.BlockSpec((8, tc), lambda i, j: (jnp.maximum(i * (tm // 8) - 1, 0), j)),
                  pl.BlockSpec((K, tc), lambda i, j: (0, j)), pl.BlockSpec((1, tc), lambda i, j: (0, j))],
        out_specs=pl.BlockSpec((tm, tc), lambda i, j: (i, j)), out_shape=jax.ShapeDtypeStruct((S, C), F32),
        compiler_params=_cparams("parallel", "parallel"), name=name)(x, x, w, b)


def _conv_silu_bwd(x, dact, w, b, name):
    S, C = x.shape
    K = w.shape[0]
    tm, tc = _conv_tiles(S, C)

    def body(x_ref, p_ref, g_ref, w_ref, b_ref, du_ref, dw_ref, db_ref):
        i = pl.program_id(1)

        @pl.when(i == 0)
        def _():
            dw_ref[...] = jnp.zeros_like(dw_ref)
            db_ref[...] = jnp.zeros_like(db_ref)

        sh = _shifted_rows(x_ref[...], jnp.where(i > 0, p_ref[...], 0.0), K)
        u = b_ref[...] + sum(sh[d] * w_ref[K - 1 - d:K - d, :] for d in range(K))
        s = _sigmoid(u)
        du = g_ref[...].astype(F32) * (s * (1.0 + u * (1.0 - s)))
        du_ref[...] = du.astype(du_ref.dtype)
        db_ref[...] += jnp.sum(du, axis=0, keepdims=True)
        for d in range(K):
            dw_ref[K - 1 - d:K - d, :] += jnp.sum(du * sh[d], axis=0, keepdims=True)

    blk = pl.BlockSpec((tm, tc), lambda j, i: (i, j))
    kc, bc = pl.BlockSpec((K, tc), lambda j, i: (0, j)), pl.BlockSpec((1, tc), lambda j, i: (0, j))
    return pl.pallas_call(
        body, grid=(C // tc, S // tm),
        in_specs=[blk, pl.BlockSpec((8, tc), lambda j, i: (jnp.maximum(i * (tm // 8) - 1, 0), j)), blk, kc, bc],
        out_specs=[blk, kc, bc],
        out_shape=[jax.ShapeDtypeStruct((S, C), _MXU), jax.ShapeDtypeStruct((K, C), F32), jax.ShapeDtypeStruct((1, C), F32)],
        compiler_params=_cparams("parallel", "arbitrary"), name=name)(x, x, dact, w, b)


def _dwconv_bwd_x(dy, w, name):
    S, C = dy.shape
    K = w.shape[0]
    tm, tc = _conv_tiles(S, C)
    last = S // tm - 1
    hr = 8 * (4 // dy.dtype.itemsize)

    def body(g_ref, n_ref, w_ref, o_ref):
        cur = g_ref[...].astype(F32)
        nxt = jnp.where(pl.program_id(0) < last, n_ref[...].astype(F32), 0.0)
        xx = jnp.concatenate([cur, nxt], axis=0)
        acc = cur * w_ref[K - 1:K, :]
        for d in range(1, K):
            acc = acc + pltpu.roll(xx, tm + hr - d, 0)[:tm, :] * w_ref[K - 1 - d:K - d, :]
        o_ref[...] = acc.astype(o_ref.dtype)

    return pl.pallas_call(
        body, grid=(S // tm, C // tc),
        in_specs=[pl.BlockSpec((tm, tc), lambda i, j: (i, j)),
                  pl.BlockSpec((hr, tc), lambda i, j: (jnp.minimum((i + 1) * (tm // hr), S // hr - 1), j)),
                  pl.BlockSpec((K, tc), lambda i, j: (0, j))],
        out_specs=pl.BlockSpec((tm, tc), lambda i, j: (i, j)), out_shape=jax.ShapeDtypeStruct((S, C), _MXU),
        compiler_params=_cparams("parallel", "parallel"), name=name)(dy, dy, w)


def _shifted_rows(cur, prev, K):
    xx = jnp.concatenate([prev, cur], axis=0)
    return [cur] + [pltpu.roll(xx, d, 0)[8:, :] for d in range(1, K)]


def _conv_glu_fwd(x, w, b, name):
    S, C = x.shape
    K = w.shape[0]
    Fh = C // 2
    tm = _rows(S, 128)

    def body(x_ref, p_ref, w_ref, b_ref, o_ref):
        sh = _shifted_rows(x_ref[...], jnp.where(pl.program_id(0) > 0, p_ref[...], 0.0), K)
        u = b_ref[...] + sum(sh[d] * w_ref[K - 1 - d:K - d, :] for d in range(K))
        g = u[:, :Fh]
        o_ref[...] = (g * _sigmoid(g) * u[:, Fh:]).astype(o_ref.dtype)

    return pl.pallas_call(
        body, grid=(S // tm,),
        in_specs=[pl.BlockSpec((tm, C), lambda i: (i, 0)), pl.BlockSpec((8, C), lambda i: (jnp.maximum(i * (tm // 8) - 1, 0), 0)),
                  pl.BlockSpec((K, C), lambda i: (0, 0)), pl.BlockSpec((1, C), lambda i: (0, 0))],
        out_specs=pl.BlockSpec((tm, Fh), lambda i: (i, 0)), out_shape=jax.ShapeDtypeStruct((S, Fh), _MXU),
        compiler_params=_cparams("parallel"), name=name)(x, x, w, b)


def _conv_glu_bwd(x, dact, w, b, name):
    S, C = x.shape
    K = w.shape[0]
    Fh = C // 2
    tm = _rows(S, 128)

    def body(x_ref, p_ref, g_ref, w_ref, b_ref, du_ref, dw_ref, db_ref):
        i = pl.program_id(0)

        @pl.when(i == 0)
        def _():
            dw_ref[...] = jnp.zeros_like(dw_ref)
            db_ref[...] = jnp.zeros_like(db_ref)

        sh = _shifted_rows(x_ref[...], jnp.where(i > 0, p_ref[...], 0.0), K)
        u = b_ref[...] + sum(sh[d] * w_ref[K - 1 - d:K - d, :] for d in range(K))
        g = u[:, :Fh]
        da = g_ref[...].astype(F32)
        s = _sigmoid(g)
        halves = ((slice(0, Fh), da * u[:, Fh:] * (s * (1.0 + g * (1.0 - s)))), (slice(Fh, C), da * (g * s)))
        for cols, du in halves:
            du_ref[:, cols] = du.astype(du_ref.dtype)
            db_ref[:, cols] += jnp.sum(du, axis=0, keepdims=True)
            for d in range(K):
                dw_ref[K - 1 - d:K - d, cols] += jnp.sum(du * sh[d][:, cols], axis=0, keepdims=True)

    return pl.pallas_call(
        body, grid=(S // tm,),
        in_specs=[pl.BlockSpec((tm, C), lambda i: (i, 0)), pl.BlockSpec((8, C), lambda i: (jnp.maximum(i * (tm // 8) - 1, 0), 0)),
                  pl.BlockSpec((tm, Fh), lambda i: (i, 0)), pl.BlockSpec((K, C), lambda i: (0, 0)), pl.BlockSpec((1, C), lambda i: (0, 0))],
        out_specs=[pl.BlockSpec((tm, C), lambda i: (i, 0)), pl.BlockSpec((K, C), lambda i: (0, 0)), pl.BlockSpec((1, C), lambda i: (0, 0))],
        out_shape=[jax.ShapeDtypeStruct((S, C), _MXU), jax.ShapeDtypeStruct((K, C), F32), jax.ShapeDtypeStruct((1, C), F32)],
        compiler_params=_cparams("arbitrary"), name=name)(x, x, dact, w, b)


def _xattn_fwd(q, k, v, name):
    S, D = q.shape
    M = k.shape[0]
    hd = D // MEM_HEADS
    tm = _rows(S, 512)
    scale = 1.0 / math.sqrt(hd)

    def body(q_ref, k_ref, v_ref, o_ref):
        for h in range(MEM_HEADS):
            sl = slice(h * hd, (h + 1) * hd)
            s = _dot(q_ref[:, sl], k_ref[:, sl], 1, 1) * scale
            p = jnp.exp(s - jnp.max(s, axis=-1, keepdims=True))
            p = p / jnp.sum(p, axis=-1, keepdims=True)
            o_ref[:, sl] = _dot(p, v_ref[:, sl], 1, 0).astype(o_ref.dtype)

    kv = pl.BlockSpec((M, D), lambda i: (0, 0))
    row = pl.BlockSpec((tm, D), lambda i: (i, 0))
    return pl.pallas_call(body, grid=(S // tm,), in_specs=[row, kv, kv], out_specs=row,
                          out_shape=jax.ShapeDtypeStruct((S, D), _MXU), compiler_params=_cparams("parallel"), name=name)(q, k, v)


def _xattn_bwd(q, k, v, do, name):
    S, D = q.shape
    M = k.shape[0]
    hd = D // MEM_HEADS
    tm = _rows(S, 512)
    scale = 1.0 / math.sqrt(hd)

    def body(q_ref, k_ref, v_ref, do_ref, dq_ref, dk_ref, dv_ref):
        @pl.when(pl.program_id(0) == 0)
        def _():
            dk_ref[...] = jnp.zeros_like(dk_ref)
            dv_ref[...] = jnp.zeros_like(dv_ref)

        for h in range(MEM_HEADS):
            sl = slice(h * hd, (h + 1) * hd)
            qh, kh, vh, doh = q_ref[:, sl], k_ref[:, sl], v_ref[:, sl], do_ref[:, sl]
            s = _dot(qh, kh, 1, 1) * scale
            p = jnp.exp(s - jnp.max(s, axis=-1, keepdims=True))
            p = p / jnp.sum(p, axis=-1, keepdims=True)
            dp = _dot(doh, vh, 1, 1)
            dv_ref[:, sl] += _dot(p, doh, 0, 0)
            ds = (p * (dp - jnp.sum(dp * p, axis=-1, keepdims=True))) * scale
            dq_ref[:, sl] = _dot(ds, kh, 1, 0).astype(dq_ref.dtype)
            dk_ref[:, sl] += _dot(ds, qh, 0, 0)

    kv = pl.BlockSpec((M, D), lambda i: (0, 0))
    row = pl.BlockSpec((tm, D), lambda i: (i, 0))
    return pl.pallas_call(
        body, grid=(S // tm,), in_specs=[row, kv, kv, row], out_specs=[row, kv, kv],
        out_shape=[jax.ShapeDtypeStruct((S, D), _MXU), jax.ShapeDtypeStruct((M, D), F32), jax.ShapeDtypeStruct((M, D), F32)],
        compiler_params=_cparams("arbitrary"), name=name)(q, k, v, do)


def _tri(n, strict, upper):
    r = lax.broadcasted_iota(jnp.int32, (n, n), 0)
    c = lax.broadcasted_iota(jnp.int32, (n, n), 1)
    if upper:
        return (c > r) if strict else (c >= r)
    return (r > c) if strict else (r >= c)


def _ssd_prep(dtp, dt_bias, a_log, name):
    S = dtp.shape[0]
    L, H = SSD_CHUNK, SSD_HEADS

    def body(p_ref, b_ref, al_ref, dt_ref, cs_ref):
        v = p_ref[:, :H] + b_ref[...]
        dt = jnp.maximum(v, 0.0) + _log1p(jnp.exp(-jnp.abs(v)))
        dt_ref[...] = dt
        a = dt * (-jnp.exp(al_ref[...]))
        cs_ref[...] = jnp.dot(_tri(L, False, False).astype(F32), a, precision=_HI, preferred_element_type=F32)

    blk = pl.BlockSpec((L, H), lambda c: (c, 0))
    vec = pl.BlockSpec((1, H), lambda c: (0, 0))
    return pl.pallas_call(body, grid=(S // L,), in_specs=[pl.BlockSpec((L, DT_PAD), lambda c: (c, 0)), vec, vec],
                          out_specs=[blk, blk], out_shape=[jax.ShapeDtypeStruct((S, H), F32)] * 2,
                          compiler_params=_cparams("parallel"), name=name)(dtp, dt_bias, a_log)


def _head_col(blk_ref, h):
    sel = lax.broadcasted_iota(jnp.int32, (1, SSD_HEADS), 1) == h
    return jnp.sum(jnp.where(sel, blk_ref[...], 0.0), axis=1, keepdims=True)


def _ssdg_fwd(xs, Bm, Cm, dt, cs, csT, name):
    H, S, P = xs.shape
    L, N = SSD_CHUNK, SSD_STATE
    nc = S // L
    rep = H // SSD_GROUPS
    hs = range(rep)

    def body(x_ref, b_ref, c_ref, dt_ref, cs_ref, csT_ref, y_ref, prev_ref, st_ref):
        c, g = pl.program_id(0), pl.program_id(1)

        @pl.when(c == 0)
        def _():
            for hh in hs:
                st_ref[g * rep + hh] = jnp.zeros((P, N), F32)

        Bv, Cv = b_ref[...], c_ref[...]
        tril = _tri(L, False, False)
        dtc = [_head_col(dt_ref, g * rep + hh) for hh in hs]
        csc = [_head_col(cs_ref, g * rep + hh) for hh in hs]
        csr = [csT_ref[hh:hh + 1, :] for hh in hs]
        last = [r[:, L - 1:L] for r in csr]
        xc = [x_ref[hh] * dtc[hh] for hh in hs]
        cb = _dot(Cv, Bv, 1, 1)
        m = [cb * jnp.where(tril, jnp.exp(jnp.where(tril, csc[hh] - csr[hh], 0.0)), 0.0) for hh in hs]
        prev = [st_ref[g * rep + hh] for hh in hs]
        yd = [_dot(m[hh], xc[hh], 1, 0) for hh in hs]
        yo = [_dot(Cv, prev[hh], 1, 1) for hh in hs]
        new = [_dot(xc[hh] * jnp.exp(last[hh] - csc[hh]), Bv, 0, 0) for hh in hs]
        for hh in hs:
            y_ref[hh] = yd[hh] + yo[hh] * jnp.exp(csc[hh])
            prev_ref[hh] = prev[hh]
            st_ref[g * rep + hh] = prev[hh] * jnp.exp(last[hh]) + new[hh]

    tok = pl.BlockSpec((L, H), lambda c, g: (c, 0))
    return pl.pallas_call(
        body, grid=(nc, SSD_GROUPS),
        in_specs=[pl.BlockSpec((rep, L, P), lambda c, g: (g, c, 0)), pl.BlockSpec((None, L, N), lambda c, g: (g, c, 0)),
                  pl.BlockSpec((None, L, N), lambda c, g: (g, c, 0)), tok, tok, pl.BlockSpec((rep, L), lambda c, g: (g, c))],
        out_specs=[pl.BlockSpec((rep, L, P), lambda c, g: (g, c, 0)),
                   pl.BlockSpec((rep, None, P, N), lambda c, g: (g, c, 0, 0))],
        out_shape=[jax.ShapeDtypeStruct((H, S, P), F32), jax.ShapeDtypeStruct((H, nc, P, N), F32)],
        scratch_shapes=[pltpu.VMEM((H, P, N), F32)],
        compiler_params=_cparams("arbitrary", "arbitrary"), name=name)(xs, Bm, Cm, dt, cs, csT)


def _ssdg_bwd(xs, Bm, Cm, dt, cs, csT, prev, dy, a_log, d_skip, name):
    H, S, P = xs.shape
    L, N = SSD_CHUNK, SSD_STATE
    nc = S // L
    rep = H // SSD_GROUPS
    hs = range(rep)

    def rowsum(a):
        return jnp.sum(a, axis=1, keepdims=True)

    def body(x_ref, b_ref, c_ref, dt_ref, cs_ref, csT_ref, prev_ref, dy_ref, al_ref, dk_ref,
             dx_ref, db_ref, dc_ref, ddt_ref, da_ref, g_ref):
        ci, g = pl.program_id(0), pl.program_id(1)

        @pl.when(ci == 0)
        def _():
            for hh in hs:
                g_ref[g * rep + hh] = jnp.zeros((P, N), F32)

        @pl.when((ci == 0) & (g == 0))
        def _():
            da_ref[...] = jnp.zeros_like(da_ref)

        @pl.when(g == 0)
        def _():
            ddt_ref[...] = jnp.zeros_like(ddt_ref)

        lane = lax.broadcasted_iota(jnp.int32, (1, H), 1)
        sel = [lane == g * rep + hh for hh in hs]
        A_h = [-jnp.exp(rowsum(jnp.where(s, al_ref[...], 0.0))) for s in sel]
        dsk = [rowsum(jnp.where(s, dk_ref[...], 0.0)) for s in sel]
        dtc = [_head_col(dt_ref, g * rep + hh) for hh in hs]
        csc = [_head_col(cs_ref, g * rep + hh) for hh in hs]
        csr = [csT_ref[hh:hh + 1, :] for hh in hs]
        last = [r[:, L - 1:L] for r in csr]
        Bv, Cv = b_ref[...], c_ref[...]
        xv = [x_ref[hh] for hh in hs]
        xc = [xv[hh] * dtc[hh] for hh in hs]
        dY = [dy_ref[hh] for hh in hs]
        prv = [prev_ref[hh] for hh in hs]
        G = [g_ref[g * rep + hh] for hh in hs]
        ecs = [jnp.exp(v) for v in csc]
        w = [jnp.exp(last[hh] - csc[hh]) for hh in hs]
        cd = [jnp.exp(v) for v in last]
        tril = _tri(L, False, False)
        triu = _tri(L, False, True)
        lam = [jnp.where(tril, jnp.exp(jnp.where(tril, csc[hh] - csr[hh], 0.0)), 0.0) for hh in hs]
        lamT = [jnp.where(triu, jnp.exp(jnp.where(triu, csr[hh] - csc[hh], 0.0)), 0.0) for hh in hs]
        cb = _dot(Cv, Bv, 1, 1)
        bc = _dot(Bv, Cv, 1, 1)
        dM = [_dot(dY[hh], xc[hh], 1, 1) for hh in hs]
        dMT = [_dot(xc[hh], dY[hh], 1, 1) for hh in hs]
        cp = [_dot(Cv, prv[hh], 1, 1) for hh in hs]
        BG = [_dot(Bv, G[hh], 1, 1) for hh in hs]
        dYe = [dY[hh] * ecs[hh] for hh in hs]
        dprev = [_dot(dYe[hh], Cv, 0, 0) for hh in hs]
        m = [cb * lam[hh] for hh in hs]
        mT = [bc * lamT[hh] for hh in hs]
        dxc = [_dot(mT[hh], dY[hh], 1, 0) + w[hh] * BG[hh] for hh in hs]
        dcb = sum([dM[hh] * lam[hh] for hh in hs][1:], dM[0] * lam[0])
        dcbT = sum([dMT[hh] * lamT[hh] for hh in hs][1:], dMT[0] * lamT[0])
        dC = _dot(dcb, Bv, 1, 0)
        dB = _dot(dcbT, Cv, 1, 0)
        for hh in hs:
            dC = dC + _dot(dYe[hh], prv[hh], 1, 0)
            dB = dB + _dot(xc[hh] * w[hh], G[hh], 1, 0)
        dc_ref[...] = dC
        db_ref[...] = dB
        ddt_acc = jnp.zeros((L, H), F32)
        da_acc = jnp.zeros((1, H), F32)
        rev = _tri(L, False, True).astype(F32)
        for hh in hs:
            dww = rowsum(xc[hh] * BG[hh]) * w[hh]
            dcs = (rowsum(dM[hh] * m[hh]) - rowsum(dMT[hh] * mT[hh]) + rowsum(dY[hh] * (cp[hh] * ecs[hh])) - dww)
            extra = jnp.sum(dww, axis=0, keepdims=True) + cd[hh] * jnp.sum(rowsum(G[hh] * prv[hh]), axis=0, keepdims=True)
            g_ref[g * rep + hh] = G[hh] * cd[hh] + dprev[hh]
            da = jnp.dot(rev, dcs, precision=_HI, preferred_element_type=F32) + extra
            dx_ref[hh] = dxc[hh] * dtc[hh] + dY[hh] * dsk[hh]
            ddt_acc = ddt_acc + jnp.where(sel[hh], da * A_h[hh] + rowsum(dxc[hh] * xv[hh]), 0.0)
            da_acc = da_acc + jnp.where(sel[hh], jnp.sum(da * dtc[hh], axis=0, keepdims=True), 0.0)
        ddt_ref[...] += ddt_acc
        da_ref[...] += da_acc

    rc = lambda ci: nc - 1 - ci
    hd = pl.BlockSpec((rep, L, P), lambda ci, g: (g, rc(ci), 0))
    grp = pl.BlockSpec((None, L, N), lambda ci, g: (g, rc(ci), 0))
    tok = pl.BlockSpec((L, H), lambda ci, g: (rc(ci), 0))
    vec = pl.BlockSpec((1, H), lambda ci, g: (0, 0))
    return pl.pallas_call(
        body, grid=(nc, SSD_GROUPS),
        in_specs=[hd, grp, grp, tok, tok, pl.BlockSpec((rep, L), lambda ci, g: (g, rc(ci))),
                  pl.BlockSpec((rep, None, P, N), lambda ci, g: (g, rc(ci), 0, 0)), hd, vec, vec],
        out_specs=[hd, grp, grp, tok, vec],
        out_shape=[jax.ShapeDtypeStruct((H, S, P), F32), jax.ShapeDtypeStruct((SSD_GROUPS, S, N), F32),
                   jax.ShapeDtypeStruct((SSD_GROUPS, S, N), F32), jax.ShapeDtypeStruct((S, H), F32),
                   jax.ShapeDtypeStruct((1, H), F32)],
        scratch_shapes=[pltpu.VMEM((H, P, N), F32)],
        compiler_params=_cparams("arbitrary", "arbitrary"), name=name)(xs, Bm, Cm, dt, cs, csT, prev, dy, a_log, d_skip)


def _dt_bwd(ddt, dA, dtp, dt_bias, a_log, name):
    S, H = ddt.shape
    tm = _rows(S, 512)

    def body(g_ref, da_ref, p_ref, b_ref, al_ref, o_ref, db_ref, dal_ref):
        @pl.when(pl.program_id(0) == 0)
        def _():
            db_ref[...] = jnp.zeros_like(db_ref)
            dal_ref[...] = da_ref[...] * (-jnp.exp(al_ref[...]))

        g = g_ref[...] * _sigmoid(p_ref[:, :H] + b_ref[...])
        db_ref[...] += jnp.sum(g, axis=0, keepdims=True)
        o_ref[...] = jnp.zeros_like(o_ref)
        o_ref[:, :H] = g.astype(o_ref.dtype)

    vec = pl.BlockSpec((1, H), lambda i: (0, 0))
    return pl.pallas_call(
        body, grid=(S // tm,),
        in_specs=[pl.BlockSpec((tm, H), lambda i: (i, 0)), vec, pl.BlockSpec((tm, DT_PAD), lambda i: (i, 0)), vec, vec],
        out_specs=[pl.BlockSpec((tm, DT_PAD), lambda i: (i, 0)), vec, vec],
        out_shape=[jax.ShapeDtypeStruct((S, DT_PAD), _MXU), jax.ShapeDtypeStruct((1, H), F32), jax.ShapeDtypeStruct((1, H), F32)],
        compiler_params=_cparams("arbitrary"), name=name)(ddt, dA, dtp, dt_bias, a_log)


def _ssd_gate_fwd(y, act, z, dskip, w, name):
    S, D = y.shape
    tm = _rows(S, 256)
    Gw = D // SSD_GROUPS

    def body(y_ref, x_ref, z_ref, k_ref, w_ref, o_ref):
        zv = z_ref[...]
        y2 = (y_ref[...] + x_ref[...] * k_ref[...]) * (zv * _sigmoid(zv))
        for g in range(SSD_GROUPS):
            sl = slice(g * Gw, (g + 1) * Gw)
            v = y2[:, sl]
            r = lax.rsqrt(jnp.mean(v * v, axis=-1, keepdims=True) + EPS)
            o_ref[:, sl] = ((v * r) * w_ref[:, sl]).astype(o_ref.dtype)

    row = pl.BlockSpec((tm, D), lambda i: (i, 0))
    vec = pl.BlockSpec((1, D), lambda i: (0, 0))
    return pl.pallas_call(body, grid=(S // tm,), in_specs=[row, row, row, vec, vec], out_specs=row,
                          out_shape=jax.ShapeDtypeStruct((S, D), _MXU), compiler_params=_cparams("parallel"),
                          name=name)(y, act, z, dskip, w)


def _ssd_gate_bwd(dyn, y, act, z, dskip, w, name):
    S, D = y.shape
    tm = _rows(S, 256)
    Gw = D // SSD_GROUPS

    def body(g_ref, y_ref, x_ref, z_ref, k_ref, w_ref, dy_ref, dz_ref, dk_ref, dw_ref):
        @pl.when(pl.program_id(0) == 0)
        def _():
            dk_ref[...] = jnp.zeros_like(dk_ref)
            dw_ref[...] = jnp.zeros_like(dw_ref)

        zv = z_ref[...]
        xv = x_ref[...]
        s = _sigmoid(zv)
        sz = zv * s
        y1 = y_ref[...] + xv * k_ref[...]
        y2 = y1 * sz
        for g in range(SSD_GROUPS):
            sl = slice(g * Gw, (g + 1) * Gw)
            v = y2[:, sl]
            r = lax.rsqrt(jnp.mean(v * v, axis=-1, keepdims=True) + EPS)
            vn = v * r
            gy = g_ref[:, sl].astype(F32)
            dw_ref[:, sl] += jnp.sum(gy * vn, axis=0, keepdims=True)
            dvn = gy * w_ref[:, sl]
            dy2 = r * (dvn - vn * jnp.mean(dvn * vn, axis=-1, keepdims=True))
            dy1 = dy2 * sz[:, sl]
            dy_ref[:, sl] = dy1
            dz_ref[:, sl] = (dy2 * y1[:, sl] * (s[:, sl] * (1.0 + zv[:, sl] * (1.0 - s[:, sl])))).astype(dz_ref.dtype)
            dk_ref[:, sl] += jnp.sum(dy1 * xv[:, sl], axis=0, keepdims=True)

    row = pl.BlockSpec((tm, D), lambda i: (i, 0))
    vec = pl.BlockSpec((1, D), lambda i: (0, 0))
    return pl.pallas_call(
        body, grid=(S // tm,), in_specs=[row, row, row, row, vec, vec], out_specs=[row, row, vec, vec],
        out_shape=[jax.ShapeDtypeStruct((S, D), F32), jax.ShapeDtypeStruct((S, D), _MXU),
                   jax.ShapeDtypeStruct((1, D), F32), jax.ShapeDtypeStruct((1, D), F32)],
        compiler_params=_cparams("arbitrary"), name=name)(dyn, y, act, z, dskip, w)


def _split_dot(v, u):
    hi = v.astype(_MXU)
    lo = (v - hi.astype(F32)).astype(_MXU)
    dn = (((1,), (0,)), ((), ()))
    return (lax.dot_general(hi, u, dn, preferred_element_type=F32) + lax.dot_general(lo, u, dn, preferred_element_type=F32))


def _sb_tiles(S):
    return _pick(S, 256, 128)


SB_LANES = 128
SB_PACK = SB_LANES // SB_HEAD_DIM
SB_ROWS = 128
SB_SCALE = 1.0 / math.sqrt(SB_HEAD_DIM)


def _head_masks():
    lane = lax.broadcasted_iota(jnp.int32, (1, SB_LANES), 1)
    return [(lane // SB_HEAD_DIM) == hh for hh in range(SB_PACK)]


def _by_head(hm, vals):
    out = vals[-1]
    for hh in range(SB_PACK - 2, -1, -1):
        out = jnp.where(hm[hh], vals[hh], out)
    return out


SB_DEAD = -110.0


def _sb_alive(Rs):
    m = Rs[0]
    for R in Rs[1:]:
        m = jnp.maximum(m, R)
    return jnp.max(m) > SB_DEAD


def _sb_rows(a, r):
    return a[r * SB_ROWS:(r + 1) * SB_ROWS]


def _sb_assemble(hm, vals):
    nr = len(vals) // SB_PACK
    return jnp.concatenate([_by_head(hm, vals[r * SB_PACK:(r + 1) * SB_PACK]) for r in range(nr)], axis=0)


def _sb_scores(zs, U, Rs, masks):
    ls = [-jnp.maximum(z, 0.0) - jnp.log(1.0 + jnp.exp(-jnp.abs(z))) for z in zs]
    if masks is not None:
        ls = [jnp.where(m, l, 0.0) for m, l in zip(masks, ls)]
    Es = [lax.dot_general(l.astype(_MXU), U, (((1,), (0,)), ((), ())), preferred_element_type=F32) for l in ls]
    As = [jnp.exp(l + z + (E + R)) for l, z, E, R in zip(ls, zs, Es, Rs)]
    if masks is not None:
        As = [jnp.where(m, A, 0.0) for m, A in zip(masks, As)]
    return ls, [A.astype(_MXU) for A in As]


SB_GROUP = 2
SB_GROUP_BWD = 1


def _sbg_chains(T, group=SB_GROUP):
    return [(b, r, hh) for b in range(group) for r in range(T // SB_ROWS) for hh in range(SB_PACK)]


def _lanes(a, b):
    return a[:, b * SB_LANES:(b + 1) * SB_LANES]


def _sbg_join(hm, vals, group=SB_GROUP):
    per = len(vals) // group
    return jnp.concatenate([_sb_assemble(hm, vals[b * per:(b + 1) * per]) for b in range(group)], axis=1)


def _sbg_head_sum(hm, a, group=SB_GROUP):
    return jnp.concatenate([_by_head(hm, [jnp.sum(jnp.where(m, _lanes(a, b), 0.0), axis=1, keepdims=True) for m in hm])
                            for b in range(group)], axis=1)


def _sbg_fwd(q_arr, k_arr, v_arr, cols, w, name, gather=()):
    S = q_arr.shape[0]
    T = _sb_tiles(S)
    cq, ck, cv = cols
    GW = SB_GROUP * SB_LANES
    nb = SB_WIDTH // GW
    ng = len(gather)
    send, finish = _gather_phases([g.shape[0] // 2 for g in gather])

    def body(q_ref, k_ref, v_ref, w_ref, *rest):
        o_ref, y_ref = rest[ng:ng + 2]
        comm = (rest[:ng], rest[ng + 2:2 * ng + 2], rest[2 * ng + 2:])
        i = pl.program_id(1)
        if ng:
            @pl.when((pl.program_id(0) == 0) & (i == 0))
            def _():
                send(*comm)
        hm = _head_masks()
        qs = q_ref[...] * SB_SCALE
        chains = _sbg_chains(T)
        qcs = [_sb_rows(jnp.where(hm[hh], _lanes(qs, b), jnp.zeros((T, SB_LANES), qs.dtype)), r) for b, r, hh in chains]
        U = _tri(T, True, False).astype(_MXU)

        def scores_of(j):
            kj = k_ref[pl.ds(pl.multiple_of(j * T, T), T), :]
            return [_dot(qc, _lanes(kj, b), 1, 1) for qc, (b, _, _) in zip(qcs, chains)]

        def weighted(Abs, j):
            vj = v_ref[pl.ds(pl.multiple_of(j * T, T), T), :]
            return _sbg_join(hm, [_dot(Ab, _lanes(vj, b), 1, 0) for Ab, (b, _, _) in zip(Abs, chains)])

        def step(carry):
            jj, acc, Rs, Aprev = carry
            j = i - 1 - jj
            zs = scores_of(j)
            acc = acc + weighted(Aprev, j + 1)
            ls, Abs = _sb_scores(zs, U, Rs, None)
            return jj + 1, acc, tuple(R + jnp.sum(l, axis=1, keepdims=True) for R, l in zip(Rs, ls)), tuple(Abs)

        masks = [_sb_rows(_tri(T, True, False), r) for _, r, _ in chains]
        zero = jnp.zeros((SB_ROWS, 1), F32)
        ls, Abs = _sb_scores(scores_of(i), U, (zero,) * len(chains), masks)
        carry = (jnp.int32(0), jnp.zeros((T, GW), F32), tuple(jnp.sum(l, axis=1, keepdims=True) for l in ls), tuple(Abs))
        jj, acc, _, Alast = lax.while_loop(lambda c: (c[0] < i) & _sb_alive(c[2]), step, carry)
        acc = acc + weighted(Alast, i - jj)
        o_ref[...] = acc
        r = lax.rsqrt(_sbg_head_sum(hm, acc * acc) * (1.0 / SB_HEAD_DIM) + EPS)
        y_ref[...] = ((acc * r) * w_ref[...]).astype(y_ref.dtype)
        if ng:
            @pl.when((pl.program_id(0) == nb - 1) & (i == S // T - 1))
            def _():
                finish(*comm)

    blk = pl.BlockSpec((T, GW), lambda h, i: (i, h))
    hbm = pl.BlockSpec(memory_space=pl.ANY)
    return pl.pallas_call(
        body, grid=(nb, S // T),
        in_specs=[pl.BlockSpec((T, GW), lambda h, i: (i, cq + h)), pl.BlockSpec((S, GW), lambda h, i: (0, ck + h), pipeline_mode=pl.Buffered(1)),
                  pl.BlockSpec((S, GW), lambda h, i: (0, cv + h), pipeline_mode=pl.Buffered(1)), pl.BlockSpec((1, GW), lambda h, i: (0, h))]
                 + [hbm] * ng,
        out_specs=[blk, blk] + [hbm] * ng,
        out_shape=[jax.ShapeDtypeStruct((S, SB_WIDTH), F32), jax.ShapeDtypeStruct((S, SB_WIDTH), _MXU)]
                  + [jax.ShapeDtypeStruct((4,) + tuple(g.shape), g.dtype) for g in gather],
        scratch_shapes=_gather_sems(ng) if ng else [],
        compiler_params=_cparams("arbitrary", "arbitrary") if ng else _cparams("parallel", "parallel"), name=name)(q_arr, k_arr, v_arr, w, *gather)


def _sbg_bwd(q_arr, k_arr, v_arr, cols, o, dy_arr, cdy, w, name, swap=()):
    S = q_arr.shape[0]
    T = _sb_tiles(S)
    cq, ck, cv = cols
    group = SB_GROUP_BWD
    GW = group * SB_LANES
    nb = SB_WIDTH // GW
    ns = len(swap)
    send, finish = _swap_phases(ns)

    def body(q_ref, k_ref, v_ref, o_ref, dy_ref, w_ref, *rest):
        dq_ref, dk_ref, dv_ref, dw_ref = rest[ns:ns + 4]
        comm = (rest[:ns], rest[ns + 4:2 * ns + 4], rest[2 * ns + 4:])
        i = pl.program_id(1)
        if ns:
            @pl.when((pl.program_id(0) == 0) & (i == 0))
            def _():
                send(*comm)

        @pl.when(i == 0)
        def _():
            dk_ref[...] = jnp.zeros_like(dk_ref)
            dv_ref[...] = jnp.zeros_like(dv_ref)
            dw_ref[...] = jnp.zeros_like(dw_ref)

        hm = _head_masks()
        chains = _sbg_chains(T, group)
        qs = q_ref[...] * SB_SCALE
        ov = o_ref[...]
        gy = dy_ref[...]
        r = lax.rsqrt(_sbg_head_sum(hm, ov * ov, group) * (1.0 / SB_HEAD_DIM) + EPS)
        on = ov * r
        dw_ref[...] += jnp.sum(gy * on, axis=0, keepdims=True)
        don = gy * w_ref[...]
        do = r * (don - on * (_sbg_head_sum(hm, don * on, group) * (1.0 / SB_HEAD_DIM)))
        dob = do.astype(_MXU)
        dprod = dob.astype(F32) * ov
        zt = jnp.zeros((T, SB_LANES), dob.dtype)
        qm = [[jnp.where(hm[hh], _lanes(qs, b), zt) for hh in range(SB_PACK)] for b in range(group)]
        dm = [[jnp.where(hm[hh], _lanes(dob, b), zt) for hh in range(SB_PACK)] for b in range(group)]
        qcs = [_sb_rows(qm[b][hh], r_) for b, r_, hh in chains]
        doc = [_sb_rows(dm[b][hh], r_) for b, r_, hh in chains]
        Dt = [_sb_rows(jnp.sum(jnp.where(hm[hh], _lanes(dprod, b), 0.0), axis=1, keepdims=True), r_) for b, r_, hh in chains]
        U = _tri(T, True, False).astype(_MXU)
        Ui = _tri(T, False, False).astype(_MXU)

        def products_of(j):
            off = pl.multiple_of(j * T, T)
            kj = k_ref[pl.ds(off, T), :]
            vj = v_ref[pl.ds(off, T), :]
            return ([_dot(qc, _lanes(kj, b), 1, 1) for qc, (b, _, _) in zip(qcs, chains)],
                    [_dot(d, _lanes(vj, b), 1, 1) for d, (b, _, _) in zip(doc, chains)])

        def core(zs, dAs, Rs, Qs, masks):
            ls, Abs = _sb_scores(zs, U, Rs, masks)
            Gs = [dA * Ab.astype(F32) for dA, Ab in zip(dAs, Abs)]
            sfx = [_split_dot(G, Ui) for G in Gs]
            dzs = []
            for c, (l, G, s, D, Q) in enumerate(zip(ls, Gs, sfx, Dt, Qs)):
                P = D - (s + Q)
                dz = jnp.exp(l) * (G + P) - P
                if masks is not None:
                    dz = jnp.where(masks[c], dz, 0.0)
                dzs.append(dz.astype(_MXU))
            newR = tuple(R + jnp.sum(l, axis=1, keepdims=True) for R, l in zip(Rs, ls))
            newQ = tuple(Q + jnp.sum(G, axis=1, keepdims=True) for Q, G in zip(Qs, Gs))
            return tuple(Abs), tuple(dzs), newR, newQ

        def over_rows(vals, other):
            nr = T // SB_ROWS
            tiles = []
            for b in range(group):
                acc = None
                for hh in range(SB_PACK):
                    rows = jnp.concatenate([vals[(b * nr + r_) * SB_PACK + hh] for r_ in range(nr)], axis=0)
                    part = _dot(rows, other[b][hh], 0, 0)
                    acc = part if acc is None else acc + part
                tiles.append(acc)
            return jnp.concatenate(tiles, axis=1)

        def emit(Abs, dzs, j):
            off = pl.multiple_of(j * T, T)
            kj = k_ref[pl.ds(off, T), :]
            dk_ref[pl.ds(off, T), :] += over_rows(dzs, qm)
            dv_ref[pl.ds(off, T), :] += over_rows(Abs, dm)
            return _sbg_join(hm, [_dot(dzb, _lanes(kj, b), 1, 0) for dzb, (b, _, _) in zip(dzs, chains)], group)

        def step(carry):
            jj, dq, Rs, Qs, Aprev, dzprev = carry
            j = i - 1 - jj
            zs, dAs = products_of(j)
            dq = dq + emit(Aprev, dzprev, j + 1)
            Abs, dzs, Rs, Qs = core(zs, dAs, Rs, Qs, None)
            return jj + 1, dq, Rs, Qs, Abs, dzs

        masks = [_sb_rows(_tri(T, True, False), r_) for _, r_, _ in chains]
        zero = (jnp.zeros((SB_ROWS, 1), F32),) * len(chains)
        zs, dAs = products_of(i)
        Abs, dzs, Rs, Qs = core(zs, dAs, zero, zero, masks)
        jj, dq, _, _, Alast, dzlast = lax.while_loop(lambda c: (c[0] < i) & _sb_alive(c[2]), step,
                                                     (jnp.int32(0), jnp.zeros((T, GW), F32), Rs, Qs, Abs, dzs))
        dq = dq + emit(Alast, dzlast, i - jj)
        dq_ref[...] = (dq * SB_SCALE).astype(dq_ref.dtype)
        if ns:
            @pl.when((pl.program_id(0) == nb - 1) & (i == S // T - 1))
            def _():
                finish(*comm)

    blk = pl.BlockSpec((T, GW), lambda h, i: (i, h))
    full = pl.BlockSpec((S, GW), lambda h, i: (0, h), pipeline_mode=pl.Buffered(1))
    wsp = pl.BlockSpec((1, GW), lambda h, i: (0, h))
    hbm = pl.BlockSpec(memory_space=pl.ANY)
    return pl.pallas_call(
        body, grid=(nb, S // T),
        in_specs=[pl.BlockSpec((T, GW), lambda h, i: (i, cq + h)), pl.BlockSpec((S, GW), lambda h, i: (0, ck + h), pipeline_mode=pl.Buffered(1)),
                  pl.BlockSpec((S, GW), lambda h, i: (0, cv + h), pipeline_mode=pl.Buffered(1)), blk,
                  pl.BlockSpec((T, GW), lambda h, i: (i, cdy + h)), wsp] + [hbm] * ns,
        out_specs=[blk, full, full, wsp] + [hbm] * ns,
        out_shape=[jax.ShapeDtypeStruct((S, SB_WIDTH), _MXU), jax.ShapeDtypeStruct((S, SB_WIDTH), F32),
                   jax.ShapeDtypeStruct((S, SB_WIDTH), F32), jax.ShapeDtypeStruct((1, SB_WIDTH), F32)]
                  + [jax.ShapeDtypeStruct((3,) + tuple(a.shape[1:]), a.dtype) for a in swap],
        scratch_shapes=[pltpu.SemaphoreType.DMA((3 * ns,))] * 2 if ns else [],
        compiler_params=_cparams("arbitrary", "arbitrary") if ns else _cparams("parallel", "arbitrary"),
        name=name)(q_arr, k_arr, v_arr, o, dy_arr, w, *swap)


def _adamw(w, g, m, v, name):
    R, C = w.shape
    tm = _rows(R, 256) if R % 8 == 0 else R
    c1 = 1.0 - ADAM_B1 ** ADAM_STEP
    c2 = 1.0 - ADAM_B2 ** ADAM_STEP

    def body(w_ref, g_ref, m_ref, v_ref, d_ref, nm_ref, nv_ref):
        gv = g_ref[...]
        mn = ADAM_B1 * m_ref[...] + (1.0 - ADAM_B1) * gv
        vn = ADAM_B2 * v_ref[...] + (1.0 - ADAM_B2) * (gv * gv)
        d_ref[...] = -ADAM_LR * ((mn / c1) / (jnp.sqrt(vn / c2) + ADAM_EPS) + ADAM_WD * w_ref[...])
        nm_ref[...] = mn
        nv_ref[...] = vn

    blk = pl.BlockSpec((tm, C), lambda i: (i, 0))
    return pl.pallas_call(body, grid=(R // tm,), in_specs=[blk] * 4, out_specs=[blk] * 3,
                          out_shape=[jax.ShapeDtypeStruct((R, C), F32)] * 3, compiler_params=_cparams("parallel"),
                          name=name)(w, g, m, v)


def _sum_lead(a, name, first=None, pick=None, wire=False):
    n, R, C = a.shape
    tm = _rows(R, 256)
    nin = 1 if first is None else 2

    def body(*refs):
        refs = refs[nin - 1:]
        a_ref = refs[nin - 1]
        s = a_ref[0].astype(F32) if first is None else refs[0][...] + a_ref[0]
        for p in range(1, n):
            s = s + a_ref[p]
        for o_ref in refs[nin:]:
            o_ref[...] = s.astype(o_ref.dtype)

    outs = [jax.ShapeDtypeStruct((R, C), F32)] + ([jax.ShapeDtypeStruct((R, C), _WIRE)] if wire else [])
    if first is None:
        row = pl.BlockSpec((tm, C), lambda i: (i, 0))
        res = pl.pallas_call(body, grid=(R // tm,), in_specs=[pl.BlockSpec((n, tm, C), lambda i: (0, i, 0))],
                             out_specs=[row] * len(outs), out_shape=outs, compiler_params=_cparams("parallel"), name=name)(a)
    else:
        row = pl.BlockSpec((tm, C), lambda i, p: (i, 0))
        grid_spec = pltpu.PrefetchScalarGridSpec(
            num_scalar_prefetch=1, grid=(R // tm,),
            in_specs=[pl.BlockSpec((None, tm, C), lambda i, p: (p[0], i, 0)), pl.BlockSpec((n, tm, C), lambda i, p: (0, i, 0))],
            out_specs=[row] * len(outs))
        res = pl.pallas_call(body, grid_spec=grid_spec, out_shape=outs, compiler_params=_cparams("parallel"), name=name)(pick, first, a)
    return res if wire else res[0]


_GROUP_BITS = {'c': ((0, 0, 1),), 'xy': ((0, 1, 0), (1, 0, 0), (1, 1, 0)),
               'xyc': tuple((k >> 2 & 1, k >> 1 & 1, k & 1) for k in range(1, 8))}


def _exchange(srcs, *, group, same_src, own, chunks, name):
    flips = _GROUP_BITS[group]
    n = len(flips) + 1
    na = len(srcs)
    blk_shapes = [tuple(s.shape) if same_src else tuple(s.shape[1:]) for s in srcs]
    assert all(bs[0] % chunks == 0 for bs in blk_shapes), blk_shapes

    def body(*refs):
        src_refs, dst_refs = refs[:na], refs[na:2 * na]
        send_sems, recv_sems, loc_sems = refs[2 * na:]
        x, y, c = lax.axis_index("x"), lax.axis_index("y"), lax.axis_index("c")

        def member(px, py, pc):
            return {'c': pc, 'xy': 2 * px + py, 'xyc': 4 * px + 2 * py + pc}[group]

        def piece(ref, a, q):
            rows = blk_shapes[a][0] // chunks
            return ref.at[pl.ds(q * rows, rows)]

        me = member(x, y, c)
        started, arrivals = [], []
        for a in range(na):
            mine = src_refs[a] if same_src else src_refs[a].at[me]
            if own:
                for q in range(chunks):
                    cp = pltpu.make_async_copy(piece(mine, a, q), piece(dst_refs[a].at[me], a, q), loc_sems.at[a * chunks + q])
                    cp.start()
                    started.append(cp.wait)
            for kk, (fx, fy, fc) in enumerate(flips):
                px, py, pc = (1 - x if fx else x), (1 - y if fy else y), (1 - c if fc else c)
                peer = member(px, py, pc)
                out_blk = src_refs[a] if same_src else src_refs[a].at[peer]
                there = dst_refs[a].at[me if own else kk]
                here = dst_refs[a].at[peer if own else kk]
                for q in range(chunks):
                    s = (a * (n - 1) + kk) * chunks + q
                    out = pltpu.make_async_remote_copy(
                        src_ref=piece(out_blk, a, q), dst_ref=piece(there, a, q), send_sem=send_sems.at[s],
                        recv_sem=recv_sems.at[s], device_id=(px, py, pc), device_id_type=pl.DeviceIdType.MESH)
                    out.start()
                    started.append(out.wait_send)
                    arrivals.append(pltpu.make_async_remote_copy(
                        src_ref=piece(mine, a, q), dst_ref=piece(here, a, q), send_sem=send_sems.at[s],
                        recv_sem=recv_sems.at[s], device_id=(px, py, pc), device_id_type=pl.DeviceIdType.MESH).wait_recv)
        for wait in arrivals + started:
            wait()

    nsem = na * (n - 1) * chunks
    hbm = pl.BlockSpec(memory_space=pl.ANY)
    return pl.pallas_call(
        body, in_specs=[hbm] * na, out_specs=[hbm] * na,
        out_shape=[jax.ShapeDtypeStruct(((n if own else n - 1),) + bs, s.dtype) for bs, s in zip(blk_shapes, srcs)],
        scratch_shapes=[pltpu.SemaphoreType.DMA((nsem,)), pltpu.SemaphoreType.DMA((nsem,)),
                        pltpu.SemaphoreType.DMA((na * chunks,))],
        compiler_params=pltpu.CompilerParams(has_side_effects=True), name=name)(*srcs)


def _gather_phases(halves):
    flips = _GROUP_BITS['xy']
    nf = len(flips)
    na = len(halves)

    def copies(src_refs, dst_refs, sems):
        send_sems, recv_sems, fsend_sems, frecv_sems = sems
        x, y, c = lax.axis_index("x"), lax.axis_index("y"), lax.axis_index("c")

        def half(ref, a, which):
            return ref.at[pl.ds(pl.multiple_of(which * halves[a], 8), halves[a])]

        def copy(src, dst, pair, s, to):
            return pltpu.make_async_remote_copy(src_ref=src, dst_ref=dst, send_sem=pair[0].at[s], recv_sem=pair[1].at[s],
                                                device_id=to, device_id_type=pl.DeviceIdType.MESH)

        out = []
        for a in range(na):
            for kk, (fx, fy, _) in enumerate(flips):
                peer = ((1 - x if fx else x), (1 - y if fy else y), c)
                there = dst_refs[a].at[2 * peer[0] + peer[1]]
                s = a * nf + kk
                ici, d2d = (send_sems, recv_sems), (fsend_sems, frecv_sems)
                out.append((copy(half(src_refs[a], a, c), half(dst_refs[a].at[2 * x + y], a, c), ici, s, peer),
                            copy(half(src_refs[a], a, c), half(there, a, c), ici, s, (x, y, c)),
                            copy(half(there, a, c), half(there, a, c), d2d, s, (x, y, 1 - c)),
                            copy(half(there, a, 1 - c), half(there, a, 1 - c), d2d, s, (x, y, 1 - c))))
        return out

    def send(src_refs, dst_refs, sems):
        for first, _, _, _ in copies(src_refs, dst_refs, sems):
            first.start()

    def finish(src_refs, dst_refs, sems):
        cs = copies(src_refs, dst_refs, sems)
        for _, landed, onward, _ in cs:
            landed.wait_recv()
            onward.start()
        for _, _, _, passed in cs:
            passed.wait_recv()
        for first, _, onward, _ in cs:
            first.wait_send()
            onward.wait_send()

    return send, finish


def _swap_phases(na):
    flips = _GROUP_BITS['xy']

    def copies(src_refs, dst_refs, sems):
        x, y, c = lax.axis_index("x"), lax.axis_index("y"), lax.axis_index("c")
        out = []
        for a in range(na):
            for kk, (fx, fy, _) in enumerate(flips):
                px, py = (1 - x if fx else x), (1 - y if fy else y)
                s = a * len(flips) + kk
                out.append(pltpu.make_async_remote_copy(
                    src_ref=src_refs[a].at[2 * px + py], dst_ref=dst_refs[a].at[kk], send_sem=sems[0].at[s], recv_sem=sems[1].at[s],
                    device_id=(px, py, c), device_id_type=pl.DeviceIdType.MESH))
        return out

    def send(src_refs, dst_refs, sems):
        for cp in copies(src_refs, dst_refs, sems):
            cp.start()

    def finish(src_refs, dst_refs, sems):
        cs = copies(src_refs, dst_refs, sems)
        for cp in cs:
            cp.wait_recv()
        for cp in cs:
            cp.wait_send()

    return send, finish


def _gather_sems(na):
    return [pltpu.SemaphoreType.DMA((na * len(_GROUP_BITS['xy']),))] * 4


def _to_shards(name, full):
    R, C = full.shape
    if name in COL_SPLIT:
        return full.reshape(R, 4, C // 4).transpose(1, 0, 2)
    return full.reshape(4, R // 4, C)


def _from_shards(name, sh):
    n, R, C = sh.shape
    if name in COL_SPLIT:
        return sh.transpose(1, 0, 2).reshape(R, n * C)
    return sh.reshape(n * R, C)


def _pack_rows(parts, width, rows):
    n = parts[0].shape[0]
    flat = jnp.concatenate([p.reshape(n, -1) for p in parts], axis=1)
    return jnp.pad(flat, ((0, 0), (0, rows * width - flat.shape[1]))).reshape(n, rows, width)


def _unpack_rows(buf, shapes):
    n = buf.shape[0]
    flat = buf.reshape(n, -1)
    out, o = [], 0
    for s in shapes:
        sz = math.prod(s)
        out.append(flat[:, o:o + sz].reshape((n,) + tuple(s)))
        o += sz
    return out


def _split_rows(a, rows):
    out, o = [], 0
    for r in rows:
        out.append(a[:, o:o + r])
        o += r
    return out


def _ceil_to(v, m):
    return -(-v // m) * m


def kernel(x, mem, norm_mix_w, w_in, conv_ssd_w, conv_ssd_b, dt_bias, a_log, d_skip, ssd_norm_w, sb_norm_w, w_out, norm_mem_w, norm_memkv_w, w_mq, w_mk, w_mv, w_mo, norm_ffn_w, w_up, conv_ffn_w, conv_ffn_b, w_down, norm_final_w, loss_target, m_norm_mix_w, m_w_in, m_conv_ssd_w, m_conv_ssd_b, m_dt_bias, m_a_log, m_d_skip, m_ssd_norm_w, m_sb_norm_w, m_w_out, m_norm_mem_w, m_norm_memkv_w, m_w_mq, m_w_mk, m_w_mv, m_w_mo, m_norm_ffn_w, m_w_up, m_conv_ffn_w, m_conv_ffn_b, m_w_down, m_norm_final_w, v_norm_mix_w, v_w_in, v_conv_ssd_w, v_conv_ssd_b, v_dt_bias, v_a_log, v_d_skip, v_ssd_norm_w, v_sb_norm_w, v_w_out, v_norm_mem_w, v_norm_memkv_w, v_w_mq, v_w_mk, v_w_mv, v_w_mo, v_norm_ffn_w, v_w_up, v_conv_ffn_w, v_conv_ffn_b, v_w_down, v_norm_final_w):
    W = dict(norm_mix_w=norm_mix_w, w_in=w_in, conv_ssd_w=conv_ssd_w, conv_ssd_b=conv_ssd_b, dt_bias=dt_bias, a_log=a_log,
             d_skip=d_skip, ssd_norm_w=ssd_norm_w, sb_norm_w=sb_norm_w, w_out=w_out, norm_mem_w=norm_mem_w,
             norm_memkv_w=norm_memkv_w, w_mq=w_mq, w_mk=w_mk, w_mv=w_mv, w_mo=w_mo, norm_ffn_w=norm_ffn_w, w_up=w_up,
             conv_ffn_w=conv_ffn_w, conv_ffn_b=conv_ffn_b, w_down=w_down, norm_final_w=norm_final_w)
    Mo = dict(norm_mix_w=m_norm_mix_w, w_in=m_w_in, conv_ssd_w=m_conv_ssd_w, conv_ssd_b=m_conv_ssd_b, dt_bias=m_dt_bias,
              a_log=m_a_log, d_skip=m_d_skip, ssd_norm_w=m_ssd_norm_w, sb_norm_w=m_sb_norm_w, w_out=m_w_out,
              norm_mem_w=m_norm_mem_w, norm_memkv_w=m_norm_memkv_w, w_mq=m_w_mq, w_mk=m_w_mk, w_mv=m_w_mv, w_mo=m_w_mo,
              norm_ffn_w=m_norm_ffn_w, w_up=m_w_up, conv_ffn_w=m_conv_ffn_w, conv_ffn_b=m_conv_ffn_b, w_down=m_w_down,
              norm_final_w=m_norm_final_w)
    Vo = dict(norm_mix_w=v_norm_mix_w, w_in=v_w_in, conv_ssd_w=v_conv_ssd_w, conv_ssd_b=v_conv_ssd_b, dt_bias=v_dt_bias,
              a_log=v_a_log, d_skip=v_d_skip, ssd_norm_w=v_ssd_norm_w, sb_norm_w=v_sb_norm_w, w_out=v_w_out,
              norm_mem_w=v_norm_mem_w, norm_memkv_w=v_norm_memkv_w, w_mq=v_w_mq, w_mk=v_w_mk, w_mv=v_w_mv, w_mo=v_w_mo,
              norm_ffn_w=v_norm_ffn_w, w_up=v_w_up, conv_ffn_w=v_conv_ffn_w, conv_ffn_b=v_conv_ffn_b, w_down=v_w_down,
              norm_final_w=v_norm_final_w)
    shapes = {n: W[n].shape for n in WEIGHTS}
    sh2 = {n: (1, a.shape[-1]) if a.ndim < 3 else a.shape[-2:] for n, a in W.items()}
    w2 = {n: W[n].reshape(sh2[n]) for n in WEIGHTS}
    x2d = x[0]
    S, D = x2d.shape
    H, P, N = SSD_HEADS, SSD_HEAD_DIM, SSD_STATE

    cv_rows = _ceil_to(-(-sum(math.prod(sh2[n]) for n in CONVW) // 128), 32)
    cpack = _pack_rows([w2[n][None] for n in CONVW], 128, cv_rows)[0]
    stacked = jnp.concatenate([w2[n].astype(_MXU) for n in ROW_SPLIT], axis=0)
    cidx = lax.axis_index("c")
    oidx = 2 * lax.axis_index("x") + lax.axis_index("y")
    now, later = [w2['w_in'].astype(_MXU), cpack], [stacked, w2['w_up'].astype(_MXU)]
    h1, *others = _rms_fwd(x2d, w2['norm_mix_w'], "norm_mix", gather=now)
    g_in, call = [lax.dynamic_update_index_in_dim(g, m, oidx, 0) for m, g in zip(now, others)]
    full = {'w_in': _from_shards('w_in', g_in)}
    full.update({n: _from_shards(n, a) for n, a in zip(CONVW, _unpack_rows(call, [sh2[n] for n in CONVW]))})

    o1 = SSD_INNER
    o2 = o1 + SSD_XBC
    o3 = o2 + SSD_HEADS
    Wi = full['w_in']
    W_z, W_xbc, W_qkv = Wi[:, :o1], Wi[:, o1:o2], Wi[:, o3:]
    W_dt = jnp.pad(Wi[:, o2:o3], ((0, 0), (0, DT_PAD - SSD_HEADS)))
    W_in_r = jnp.concatenate([W_z, W_xbc, W_qkv, W_dt], axis=1)
    dskip_rep = jnp.repeat(w2['d_skip'], P, axis=1)

    z = _mm(h1, W_z, name="proj_z")
    xbc = _mm(h1, W_xbc, name="proj_xbc")
    dtp = _mm(h1, W_dt, name="proj_dt")
    qkv = _mm(h1, W_qkv, out_dtype=_MXU, name="proj_qkv")
    act = _conv_silu_fwd(xbc, full['conv_ssd_w'], w2['conv_ssd_b'], "ssd_conv_silu")
    dt, cs = _ssd_prep(dtp, w2['dt_bias'], w2['a_log'], "ssd_prep")
    csT = cs.T
    def heads(a, nh):
        return a.reshape(S, nh, a.shape[1] // nh).transpose(1, 0, 2)

    def unheads(a):
        return a.transpose(1, 0, 2).reshape(S, a.shape[0] * a.shape[2])

    xs_h = heads(act[:, :o1], H)
    Bm = heads(act[:, o1:o1 + SSD_GROUPS * N], SSD_GROUPS)
    Cm = heads(act[:, o1 + SSD_GROUPS * N:], SSD_GROUPS)
    y_h, prev = _ssdg_fwd(xs_h, Bm, Cm, dt, cs, csT, "ssd_scan")
    y_scan = unheads(y_h)
    y_ssd = _ssd_gate_fwd(y_scan, act, z, dskip_rep, w2['ssd_norm_w'], "ssd_gate")
    nsb = SB_WIDTH // (SB_GROUP * SB_LANES)
    qkv_cols = (0, nsb, 2 * nsb)
    o_sb, y_sb, *others = _sbg_fwd(qkv, qkv, qkv, qkv_cols, w2['sb_norm_w'], "sb_attn", gather=later)
    g_rows, g_up = [lax.dynamic_update_index_in_dim(g, m, oidx, 0) for m, g in zip(later, others)]
    full['w_up'] = _from_shards('w_up', g_up)
    full.update({n: _from_shards(n, a) for n, a in zip(ROW_SPLIT, _split_rows(g_rows, [sh2[n][0] for n in ROW_SPLIT]))})
    ycat = jnp.concatenate([y_ssd, y_sb], axis=1)
    x_2 = _mm(ycat, full['w_out'], res=x2d, name="out_proj")
    h2 = _rms_fwd(x_2, w2['norm_mem_w'], "norm_mem")
    qm = _mm(h2, full['w_mq'], out_dtype=_MXU, name="mem_q")
    mn = _rms_fwd(mem[0], w2['norm_memkv_w'], "norm_memkv")
    km = _mm(mn, full['w_mk'], out_dtype=_MXU, name="mem_k")
    vm = _mm(mn, full['w_mv'], out_dtype=_MXU, name="mem_v")
    om = _xattn_fwd(qm, km, vm, "mem_attn")
    x_3 = _mm(om, full['w_mo'], res=x_2, name="mem_o")
    h3 = _rms_fwd(x_3, w2['norm_ffn_w'], "norm_ffn")
    up = _mm(h3, full['w_up'], name="ffn_up")
    a_ffn = _conv_glu_fwd(up, full['conv_ffn_w'], w2['conv_ffn_b'], "ffn_conv_glu")
    x_4 = _mm(a_ffn, full['w_down'], res=x_3, name="ffn_down")
    dx4, dx4b, g_final, loss_blk = _loss_bwd(x_4, loss_target[0], w2['norm_final_w'], "loss_head")

    G = {'norm_final_w': g_final}
    dact = _mm(dx4b, full['w_down'], tb=True, name="d_ffn_act")
    G['w_down'] = _mm(a_ffn, dx4b, ta=True, name="g_w_down")
    du, G['conv_ffn_w'], G['conv_ffn_b'] = _conv_glu_bwd(up, dact, full['conv_ffn_w'], w2['conv_ffn_b'], "d_ffn_conv_glu")
    dup = _dwconv_bwd_x(du, full['conv_ffn_w'], "d_ffn_conv")
    dh3 = _mm(dup, full['w_up'], tb=True, name="d_h3")
    G['w_up'] = _mm(h3, dup, ta=True, name="g_w_up")
    dx3, dx3b, G['norm_ffn_w'] = _rms_bwd(dh3, x_3, w2['norm_ffn_w'], dx4, "d_norm_ffn")
    dom = _mm(dx3b, full['w_mo'], tb=True, out_dtype=_MXU, name="d_mem_o")
    G['w_mo'] = _mm(om, dx3b, ta=True, name="g_w_mo")
    dqm, dkm, dvm = _xattn_bwd(qm, km, vm, dom, "d_mem_attn")
    G['w_mq'] = _mm(h2, dqm, ta=True, name="g_w_mq")
    dh2 = _mm(dqm, full['w_mq'], tb=True, name="d_h2")
    dx2, dx2b, G['norm_mem_w'] = _rms_bwd(dh2, x_2, w2['norm_mem_w'], dx3, "d_norm_mem")
    G['w_mk'] = _mm(mn, dkm, ta=True, name="g_w_mk")
    G['w_mv'] = _mm(mn, dvm, ta=True, name="g_w_mv")
    dmn = _mm(dvm, full['w_mv'], tb=True, res=_mm(dkm, full['w_mk'], tb=True, name="d_mn_k"), name="d_mn_v")
    _, _, G['norm_memkv_w'] = _rms_bwd(dmn, mem[0], w2['norm_memkv_w'], None, "d_norm_memkv")
    dycat = _mm(dx2b, full['w_out'], tb=True, name="d_ycat")
    G['w_out'] = _mm(ycat, dx2b, ta=True, name="g_w_out")
    dy1, dz, g_dskip_lane, G['ssd_norm_w'] = _ssd_gate_bwd(dycat, y_scan, act, z, dskip_rep, w2['ssd_norm_w'], "d_ssd_gate")
    dxs_h, dB, dC, ddt, dA = _ssdg_bwd(xs_h, Bm, Cm, dt, cs, csT, prev, heads(dy1, H), w2['a_log'], w2['d_skip'], "d_ssd_scan")
    G['d_skip'] = jnp.sum(g_dskip_lane.reshape(H, P), axis=1)[None, :]
    dact_xbc = jnp.concatenate([unheads(dxs_h), unheads(dB), unheads(dC)], axis=1)
    dpre, G['conv_ssd_w'], G['conv_ssd_b'] = _conv_silu_bwd(xbc, dact_xbc, full['conv_ssd_w'], w2['conv_ssd_b'], "d_ssd_conv_silu")
    dxbc = _dwconv_bwd_x(dpre, full['conv_ssd_w'], "d_ssd_conv")
    ddtp, G['dt_bias'], G['a_log'] = _dt_bwd(ddt, dA, dtp, w2['dt_bias'], w2['a_log'], "d_dt")
    def pair_sums(to_pair, tag):
        got = _exchange(to_pair, group='c', same_src=False, own=False, chunks=4, name="reduce_pair" + tag)
        sums, wires = [], []
        for k, (t, g) in enumerate(zip(to_pair, got)):
            _, _, r, cw = t.shape
            full_sum, wire_sum = _sum_lead(g.reshape(1, 4 * r, cw), "reduce_pair_sum%s%d" % (tag, k), first=t.reshape(2, 4 * r, cw),
                                           pick=cidx.reshape(1), wire=True)
            sums.append(full_sum.reshape(4, r, cw))
            wires.append(wire_sum.reshape(4, r, cw))
        return sums, wires

    def chip_sums(sums, got, tag):
        return [_sum_lead(g, "reduce_chips_sum%s%d" % (tag, k), first=p, pick=oidx.reshape(1)) for k, (p, g) in enumerate(zip(sums, got))]

    by_owner = [jnp.concatenate([_to_shards(n, G[n]) for n in ROW_SPLIT], axis=1), _to_shards('w_up', G['w_up'])]
    pair_a, wire_a = pair_sums([a.reshape(4, 2, a.shape[1] // 2, a.shape[2]).transpose(1, 0, 2, 3) for a in by_owner], "_a")

    nsb_b = SB_WIDTH // (SB_GROUP_BWD * SB_LANES)
    dq, dk, dv, G['sb_norm_w'], *got_a = _sbg_bwd(qkv, qkv, qkv, (0, nsb_b, 2 * nsb_b), o_sb, dycat, nsb_b, w2['sb_norm_w'],
                                                  "d_sb_attn", swap=wire_a)
    chips_rows, chips_up = chip_sums(pair_a, got_a, "_a")
    dproj = jnp.concatenate([dz, dxbc, dq, dk.astype(_MXU), dv.astype(_MXU), ddtp], axis=1)
    dh1 = _mm(dproj, W_in_r, tb=True, name="d_h1")
    g_in_r = _mm(h1, dproj, ta=True, name="g_w_in")
    nq = 3 * SB_WIDTH

    def in_cols(lo, hi):
        spans = []
        for a, b, shift in ((0, o2, 0), (o2, o3, nq), (o3, o3 + nq, o2 - o3)):
            s, e = max(lo, a), min(hi, b)
            if s < e:
                spans.append((s + shift, e + shift))
        return spans

    hr_in, cs_in = g_in_r.shape[0] // 2, (o3 + nq) // 4
    g_in_pair = jnp.stack([jnp.stack([jnp.concatenate([g_in_r[h * hr_in:(h + 1) * hr_in, s:e] for s, e in in_cols(j * cs_in, (j + 1) * cs_in)],
                                                      axis=1) for j in range(4)]) for h in range(2)])

    pair_b, wire_b = pair_sums([g_in_pair], "_b")
    grad_x, _, G['norm_mix_w'], *got_b = _rms_bwd(dh1, x2d, w2['norm_mix_w'], dx2, "d_norm_mix", swap=wire_b)
    (chips_in,) = chip_sums(pair_b, got_b, "_b")
    chips = [chips_rows, chips_in, chips_up]
    got = _exchange(chips, group='c', same_src=True, own=False, chunks=4, name="share_pair")
    red = [jnp.where(cidx == 0, jnp.concatenate([m, g[0]], axis=0), jnp.concatenate([g[0], m], axis=0))[None]
           for m, g in zip(chips, got)]
    gsh = dict(zip(ROW_SPLIT, [a[0] for a in _split_rows(red[0], [sh2[n][0] for n in ROW_SPLIT])]))
    gsh['w_in'], gsh['w_up'] = red[1][0], red[2][0]

    small_parts = [G[n].reshape(1, -1) for n in SMALL + CONVW] + [loss_blk[:1, :1]]
    small_shapes = [sh2[n] for n in SMALL] + [G[n].shape for n in CONVW] + [(1, 1)]
    small_rows = _ceil_to(-(-sum(math.prod(s) for s in small_shapes) // 128), 8)
    spack = _pack_rows(small_parts, 128, small_rows)[0]
    (gathered,) = _exchange([spack], group='xyc', same_src=True, own=True, chunks=1, name="gather_small")
    parts = [a[0] for a in _unpack_rows(_sum_lead(gathered, "small_sum")[None], small_shapes)]
    gsh.update(zip(SMALL, parts))
    for n, a in zip(CONVW, parts[len(SMALL):-1]):
        gsh[n] = lax.dynamic_index_in_dim(_to_shards(n, a), oidx, 0, keepdims=False)
    loss = parts[-1].reshape(())

    delta, new_m, new_v = {}, {}, {}
    for n in BIG:
        delta[n], new_m[n], new_v[n] = _adamw(w2[n], gsh[n], Mo[n].reshape(sh2[n]), Vo[n].reshape(sh2[n]), "adamw_" + n)
    for grp, width, tag in ((CONVW, 128, "adamw_conv"), (SMALL, 128, "adamw_small")):
        rows = _ceil_to(-(-sum(math.prod(sh2[n]) for n in grp) // width), 8)
        packed = [_pack_rows([src[n].reshape(1, -1) for n in grp], width, rows)[0]
                  for src in (w2, gsh, {n: Mo[n] for n in grp}, {n: Vo[n] for n in grp})]
        outs = _adamw(*packed, tag)
        for dst, o in zip((delta, new_m, new_v), outs):
            dst.update(zip(grp, [a[0] for a in _unpack_rows(o[None], [sh2[n] for n in grp])]))

    def shaped(d):
        return [d[n].reshape(shapes[n]) for n in WEIGHTS]

    return (loss, grad_x[None], *shaped(gsh), *shaped(delta), *shaped(new_m), *shaped(new_v))
```
